```python
import jax, jax.numpy as jnp
from jax import lax
import numpy as np


D_MODEL = 1024
BATCH = 8
SEQ = 2048
DEPTH = 1
DEC_BATCH = 128
DEC_SEQ = 4
PAST_LEN = 8192
PAGE_SIZE = 128

ATTN_HEADS = 8
ATTN_KV_HEADS = 2
GROUP = ATTN_HEADS // ATTN_KV_HEADS
HEAD_DIM = 64
ATTN_WIDTH = ATTN_HEADS * HEAD_DIM
KV_WIDTH = ATTN_KV_HEADS * HEAD_DIM
WINDOW = 128
ROPE_THETA = 10000.0
GLA_HEADS = 4
GLA_KEY_WIDTH = D_MODEL // 2
GLA_VALUE_WIDTH = D_MODEL
GLA_DK = GLA_KEY_WIDTH // GLA_HEADS
GLA_DV = GLA_VALUE_WIDTH // GLA_HEADS
GLA_GATE_RANK = 16
GLA_GATE_NORMALIZER = 16.0
GLA_CHUNK = 16
N_GROUPS = 4
EXPERTS_PER_GROUP = 8
N_EXPERTS = N_GROUPS * EXPERTS_PER_GROUP
TOP_K_IN_GROUP = 2
EXPERT_FF = 256
EPS = 1e-6
IN_PROJ_WIDTH = ATTN_WIDTH + 2 * KV_WIDTH + 2 * GLA_KEY_WIDTH + 2 * GLA_VALUE_WIDTH + GLA_GATE_RANK + 2 * D_MODEL

kernel_name = 'hybrid_swa_gla_hmoe_step'


def _rmsnorm(x, g):
    xf = x.astype(jnp.float32)
    y = xf * lax.rsqrt(jnp.mean(xf * xf, axis=-1, keepdims=True) + EPS)
    return (y * g.astype(jnp.float32)).astype(x.dtype)


def _split_points():
    widths = (ATTN_WIDTH, KV_WIDTH, KV_WIDTH, GLA_KEY_WIDTH, GLA_KEY_WIDTH,
              GLA_VALUE_WIDTH, GLA_VALUE_WIDTH, GLA_GATE_RANK, D_MODEL, D_MODEL)
    points, acc = [], 0
    for w in widths[:-1]:
        acc += w
        points.append(acc)
    return points


def _rope(x, positions):
    half = HEAD_DIM // 2
    inv_freq = ROPE_THETA ** (-jnp.arange(half, dtype=jnp.float32) / half)
    ang = positions.astype(jnp.float32)[:, None] * inv_freq[None, :]
    cos = jnp.cos(ang)[:, None, :]
    sin = jnp.sin(ang)[:, None, :]
    xf = x.astype(jnp.float32)
    x1, x2 = xf[..., :half], xf[..., half:]
    return jnp.concatenate([x1 * cos - x2 * sin, x2 * cos + x1 * sin], axis=-1).astype(x.dtype)


def _sink_softmax(s, mask, sinks):
    s = jnp.where(mask, s, -jnp.inf)
    sink = jnp.broadcast_to(sinks.astype(jnp.float32).reshape(ATTN_KV_HEADS, GROUP, 1, 1), s.shape[:-1] + (1,))
    p = jax.nn.softmax(jnp.concatenate([s, sink], axis=-1), axis=-1)
    return p[..., :-1]


def _swa_prompt(q, k, v, sinks):
    B, T = q.shape[0], q.shape[1]
    nb = T // WINDOW
    qb = q.reshape(B, nb, WINDOW, ATTN_KV_HEADS, GROUP, HEAD_DIM)

    def with_prev(xb):
        xb = xb.reshape(B, nb, WINDOW, ATTN_KV_HEADS, HEAD_DIM)
        prev = jnp.pad(xb[:, :-1], ((0, 0), (1, 0), (0, 0), (0, 0), (0, 0)))
        return jnp.concatenate([prev, xb], axis=2)

    kk, vv = with_prev(k), with_prev(v)
    s = jnp.einsum('bnqkgd,bnskd->bnkgqs', qb, kk, preferred_element_type=jnp.float32) * (HEAD_DIM ** -0.5)
    q_pos = jnp.arange(WINDOW)[:, None]
    k_pos = jnp.arange(2 * WINDOW)[None, :] - WINDOW
    rel = q_pos - k_pos
    blk = jnp.arange(nb)[:, None, None]
    mask = (rel >= 0) & (rel <= WINDOW) & (blk * WINDOW + k_pos >= 0)
    p = _sink_softmax(s, mask[:, None, None], sinks)
    o = jnp.einsum('bnkgqs,bnskd->bnqkgd', p.astype(v.dtype), vv)
    return o.reshape(B, T, ATTN_WIDTH)


def _swa_sample(q, k, v, cache_k, cache_v, sinks):
    B, T = q.shape[0], q.shape[1]
    kk = jnp.concatenate([cache_k.astype(k.dtype), k], axis=1)
    vv = jnp.concatenate([cache_v.astype(v.dtype), v], axis=1)
    qg = q.reshape(B, T, ATTN_KV_HEADS, GROUP, HEAD_DIM)
    s = jnp.einsum('bqkgd,bskd->bkgqs', qg, kk, preferred_element_type=jnp.float32) * (HEAD_DIM ** -0.5)
    rel = (jnp.arange(T)[:, None] + WINDOW) - jnp.arange(WINDOW + T)[None, :]
    mask = (rel >= 0) & (rel <= WINDOW)
    p = _sink_softmax(s, mask, sinks)
    o = jnp.einsum('bkgqs,bskd->bqkgd', p.astype(v.dtype), vv).reshape(B, T, ATTN_WIDTH)
    return o, kk[:, -WINDOW:], vv[:, -WINDOW:]


def _gla_scan(q, k, v, log_a, s0):
    B, T, H, _ = q.shape
    DV = v.shape[-1]
    pad = (-T) % GLA_CHUNK
    nc = (T + pad) // GLA_CHUNK

    def chunks(a):
        a = jnp.pad(a.astype(jnp.float32), ((0, 0), (0, pad), (0, 0), (0, 0)))
        return a.reshape(B, nc, GLA_CHUNK, H, a.shape[-1]).transpose(1, 0, 3, 2, 4)

    causal = jnp.tril(jnp.ones((GLA_CHUNK, GLA_CHUNK), dtype=bool))[:, :, None]

    def step(S, inp):
        qc, kc, vc, ac = inp
        b = jnp.cumsum(ac, axis=2)
        b_last = b[:, :, -1:]
        decay = jnp.exp(jnp.where(causal, b[:, :, :, None] - b[:, :, None], -jnp.inf))
        A = jnp.einsum('bhtd,bhsd,bhtsd->bhts', qc, kc, decay)
        o = jnp.einsum('bhts,bhsv->bhtv', A, vc) + jnp.einsum('bhtd,bhdv->bhtv', qc * jnp.exp(b), S)
        S = jnp.exp(b_last[:, :, 0, :, None]) * S + jnp.einsum('bhsd,bhsv->bhdv', kc * jnp.exp(b_last - b), vc)
        return S, o

    S, o = lax.scan(step, s0.astype(jnp.float32), (chunks(q), chunks(k), chunks(v), chunks(log_a)))
    o = o.transpose(1, 0, 3, 2, 4).reshape(B, nc * GLA_CHUNK, H, DV)[:, :T]
    return o, S


def _gla(q, k, v, f_lr, r, s0, w_f2, b_f, norm_g):
    B, T = q.shape[0], q.shape[1]
    log_a = jax.nn.log_sigmoid((f_lr @ w_f2 + b_f).astype(jnp.float32)) / GLA_GATE_NORMALIZER
    qh = q.reshape(B, T, GLA_HEADS, GLA_DK) * (GLA_DK ** -0.5)
    kh = k.reshape(B, T, GLA_HEADS, GLA_DK)
    vh = v.reshape(B, T, GLA_HEADS, GLA_DV)
    o, S = _gla_scan(qh, kh, vh, log_a.reshape(B, T, GLA_HEADS, GLA_DK), s0)
    o = _rmsnorm(o, norm_g) * jax.nn.silu(r.reshape(B, T, GLA_HEADS, GLA_DV).astype(jnp.float32))
    return o.reshape(B, T, GLA_VALUE_WIDTH).astype(q.dtype), S


def _moe(h, w_rg, b_rg, w_re, b_re, w_eg, w_eu, w_ed):
    lead = h.shape[:-1]
    hf = h.reshape(-1, D_MODEL)
    n = hf.shape[0]
    p_group = jax.nn.softmax((hf @ w_rg + b_rg).astype(jnp.float32), axis=-1)
    g_star = jnp.argmax(p_group, axis=-1)
    p_sel = jnp.take_along_axis(p_group, g_star[:, None], axis=-1)
    e_logits = (hf @ w_re + b_re).astype(jnp.float32).reshape(n, N_GROUPS, EXPERTS_PER_GROUP)
    e_sel = jnp.take_along_axis(e_logits, g_star[:, None, None], axis=1)[:, 0]
    top_v, top_i = lax.top_k(e_sel, TOP_K_IN_GROUP)
    w_top = jax.nn.softmax(top_v, axis=-1) * p_sel
    expert = g_star[:, None] * EXPERTS_PER_GROUP + top_i
    combine = jnp.sum(jax.nn.one_hot(expert, N_EXPERTS, dtype=jnp.float32) * w_top[..., None], axis=1)
    combine = combine.astype(hf.dtype)
    out = jnp.zeros_like(hf)
    for g in range(N_GROUPS):
        sl = slice(g * EXPERTS_PER_GROUP, (g + 1) * EXPERTS_PER_GROUP)
        a = jnp.einsum('nd,edf->nef', hf, w_eg[sl])
        u = jnp.einsum('nd,edf->nef', hf, w_eu[sl])
        act = jax.nn.silu(a) * u * combine[:, sl, None]
        out = out + jnp.einsum('nef,efd->nd', act, w_ed[sl])
    return out.reshape(lead + (D_MODEL,))


def _layer(x, positions, cache_k, cache_v, gla_state, norm_mix, w_in, w_gla_f2, b_gla_f, gla_norm,
           attn_sinks, w_proj_attn, w_proj_gla, w_out, norm_ffn, w_router_group, b_router_group,
           w_router_expert, b_router_expert, w_exp_gate, w_exp_up, w_exp_down):
    B, T = x.shape[0], x.shape[1]
    h = _rmsnorm(x, norm_mix)
    q_a, k_a, v_a, q_g, k_g, v_g, r_g, f_lr, gate_a, gate_b = jnp.split(h @ w_in, _split_points(), axis=-1)
    q_a = _rope(q_a.reshape(B, T, ATTN_HEADS, HEAD_DIM), positions)
    k_a = _rope(k_a.reshape(B, T, ATTN_KV_HEADS, HEAD_DIM), positions)
    v_a = v_a.reshape(B, T, ATTN_KV_HEADS, HEAD_DIM)
    if cache_k is None:
        a_out = _swa_prompt(q_a, k_a, v_a, attn_sinks)
        new_k, new_v = k_a[:, -WINDOW:], v_a[:, -WINDOW:]
        s0 = jnp.zeros((B, GLA_HEADS, GLA_DK, GLA_DV), jnp.float32)
    else:
        a_out, new_k, new_v = _swa_sample(q_a, k_a, v_a, cache_k, cache_v, attn_sinks)
        s0 = gla_state
    g_out, new_s = _gla(q_g, k_g, v_g, f_lr, r_g, s0, w_gla_f2, b_gla_f, gla_norm)
    merged = jax.nn.sigmoid(gate_a) * (a_out @ w_proj_attn) + jax.nn.sigmoid(gate_b) * (g_out @ w_proj_gla)
    x = x + merged @ w_out
    x = x + _moe(_rmsnorm(x, norm_ffn), w_router_group, b_router_group, w_router_expert, b_router_expert,
                 w_exp_gate, w_exp_up, w_exp_down)
    return x, new_k, new_v, new_s


def setup_inputs(seed: int = 0) -> dict:
    key = jax.random.key(seed)
    ks = jax.random.split(key, 24)
    f32 = jnp.float32
    L = DEPTH

    def nrm(k, shape, scale):
        return jax.random.normal(k, shape, f32) * scale

    return {
        'x_prompt': nrm(ks[0], (BATCH, SEQ, D_MODEL), 1.0),
        'x_sample': nrm(ks[1], (DEC_BATCH, DEC_SEQ, D_MODEL), 1.0),
        'cache_win_k': nrm(ks[2], (L, DEC_BATCH, WINDOW, ATTN_KV_HEADS, HEAD_DIM), 1.0),
        'cache_win_v': nrm(ks[3], (L, DEC_BATCH, WINDOW, ATTN_KV_HEADS, HEAD_DIM), 1.0),
        'state_gla': nrm(ks[4], (L, DEC_BATCH, GLA_HEADS, GLA_DK, GLA_DV), 0.5),
        'norm_mix': 1.0 + nrm(ks[5], (L, D_MODEL), 0.02),
        'w_in': nrm(ks[6], (L, D_MODEL, IN_PROJ_WIDTH), D_MODEL ** -0.5),
        'w_gla_f2': nrm(ks[7], (L, GLA_GATE_RANK, GLA_KEY_WIDTH), GLA_GATE_RANK ** -0.5),
        'b_gla_f': nrm(ks[8], (L, GLA_KEY_WIDTH), 0.1),
        'gla_norm': 1.0 + nrm(ks[9], (L, GLA_DV), 0.02),
        'attn_sinks': nrm(ks[10], (L, ATTN_HEADS), 0.5),
        'w_proj_attn': nrm(ks[11], (L, ATTN_WIDTH, D_MODEL), ATTN_WIDTH ** -0.5),
        'w_proj_gla': nrm(ks[12], (L, GLA_VALUE_WIDTH, D_MODEL), GLA_VALUE_WIDTH ** -0.5),
        'w_out': nrm(ks[13], (L, D_MODEL, D_MODEL), D_MODEL ** -0.5),
        'norm_ffn': 1.0 + nrm(ks[14], (L, D_MODEL), 0.02),
        'w_router_group': nrm(ks[15], (L, D_MODEL, N_GROUPS), D_MODEL ** -0.5),
        'b_router_group': nrm(ks[16], (L, N_GROUPS), 0.01),
        'w_router_expert': nrm(ks[17], (L, D_MODEL, N_EXPERTS), D_MODEL ** -0.5),
        'b_router_expert': nrm(ks[18], (L, N_EXPERTS), 0.01),
        'w_exp_gate': nrm(ks[19], (L, N_EXPERTS, D_MODEL, EXPERT_FF), D_MODEL ** -0.5),
        'w_exp_up': nrm(ks[20], (L, N_EXPERTS, D_MODEL, EXPERT_FF), D_MODEL ** -0.5),
        'w_exp_down': nrm(ks[21], (L, N_EXPERTS, EXPERT_FF, D_MODEL), EXPERT_FF ** -0.5),
        'norm_final': 1.0 + nrm(ks[22], (D_MODEL,), 0.02),
    }


def reference(x_prompt, x_sample, cache_win_k, cache_win_v, state_gla, norm_mix, w_in, w_gla_f2, b_gla_f,
              gla_norm, attn_sinks, w_proj_attn, w_proj_gla, w_out, norm_ffn, w_router_group, b_router_group,
              w_router_expert, b_router_expert, w_exp_gate, w_exp_up, w_exp_down, norm_final):
    pos_prompt = jnp.arange(x_prompt.shape[1], dtype=jnp.int32)
    pos_sample = PAST_LEN + jnp.arange(x_sample.shape[1], dtype=jnp.int32)
    yp, ys = x_prompt, x_sample
    pk, pv, ps, sk, sv, ss = [], [], [], [], [], []
    for l in range(DEPTH):
        lw = (norm_mix[l], w_in[l], w_gla_f2[l], b_gla_f[l], gla_norm[l], attn_sinks[l], w_proj_attn[l],
              w_proj_gla[l], w_out[l], norm_ffn[l], w_router_group[l], b_router_group[l], w_router_expert[l],
              b_router_expert[l], w_exp_gate[l], w_exp_up[l], w_exp_down[l])
        yp, k1, v1, s1 = _layer(yp, pos_prompt, None, None, None, *lw)
        ys, k2, v2, s2 = _layer(ys, pos_sample, cache_win_k[l], cache_win_v[l], state_gla[l], *lw)
        pk.append(k1)
        pv.append(v1)
        ps.append(s1)
        sk.append(k2)
        sv.append(v2)
        ss.append(s2)
    y_prompt = _rmsnorm(yp, norm_final)
    y_sample = _rmsnorm(ys, norm_final)
    prompt_win_k = jnp.stack(pk)
    prompt_win_v = jnp.stack(pv)
    prompt_gla_state = jnp.stack(ps)
    sample_win_k = jnp.stack(sk)
    sample_win_v = jnp.stack(sv)
    sample_gla_state = jnp.stack(ss)
    return (y_prompt, y_sample, prompt_win_k, prompt_win_v, prompt_gla_state, sample_win_k, sample_win_v, sample_gla_state)
```

```python
import functools

import numpy as np
import jax
import jax.numpy as jnp
from jax import lax
from jax.experimental import pallas as pl
from jax.experimental.pallas import tpu as pltpu

F32 = jnp.float32
BF16 = jnp.bfloat16

D_MODEL = 1024
ATTN_HEADS = 8
ATTN_KV_HEADS = 2
GROUP = ATTN_HEADS // ATTN_KV_HEADS
HEAD_DIM = 64
ATTN_WIDTH = ATTN_HEADS * HEAD_DIM
KV_WIDTH = ATTN_KV_HEADS * HEAD_DIM
WINDOW = 128
ROPE_THETA = 10000.0
PAST_LEN = 8192
GLA_HEADS = 4
GLA_KEY_WIDTH = D_MODEL // 2
GLA_VALUE_WIDTH = D_MODEL
GLA_DK = GLA_KEY_WIDTH // GLA_HEADS
GLA_DV = GLA_VALUE_WIDTH // GLA_HEADS
GLA_GATE_RANK = 16
GLA_GATE_NORMALIZER = 16.0
N_GROUPS = 4
EXPERTS_PER_GROUP = 8
N_EXPERTS = N_GROUPS * EXPERTS_PER_GROUP
EXPERT_FF = 256
EPS = 1e-6

LANES = 128
GLA_CHUNK = 128
SAMPLE_SEQ_BLOCK = 8
VMEM_LIMIT = 56 * 1024 * 1024


def _cparams(sem):
    return pltpu.CompilerParams(dimension_semantics=sem, vmem_limit_bytes=VMEM_LIMIT)


def _rms(x, g):
    return x * lax.rsqrt(jnp.mean(x * x, axis=-1, keepdims=True) + EPS) * g


def _sigmoid(x):
    return 1.0 / (1.0 + jnp.exp(-x))


def _dot(a, b):
    return jnp.dot(a, b, preferred_element_type=F32)


def _dot_nt(a, b):
    return lax.dot_general(a, b, (((1,), (1,)), ((), ())), preferred_element_type=F32)


def _dot_tn(a, b):
    return lax.dot_general(a, b, (((0,), (0,)), ((), ())), preferred_element_type=F32)


def _split_bf16(x):
    hi = x.astype(BF16)
    lo = (x - hi.astype(F32)).astype(BF16)
    return hi, lo


def _inproj_kernel(x_ref, g_ref, cos_ref, sin_ref, wqa, wkva, wqg, wkg, wvg, wrg, wga, wgb, wf, wf2, bf,
                   qa_o, ka_o, va_o, qg_o, kg_o, vg_o, rg_o, la_o, ga_o, gb_o):
    hb = _rms(x_ref[...], g_ref[...]).astype(BF16)
    cos = cos_ref[...]
    sin = sin_ref[...]
    lane = lax.broadcasted_iota(jnp.int32, cos.shape, 1)
    first_half = (lane % HEAD_DIM) < (HEAD_DIM // 2)

    def rope(t):
        swapped = jnp.where(first_half, pltpu.roll(t, LANES - HEAD_DIM // 2, 1), pltpu.roll(t, HEAD_DIM // 2, 1))
        return t * cos + swapped * sin

    qa = _dot(hb, wqa[...])
    for c in range(ATTN_WIDTH // LANES):
        sl = slice(c * LANES, (c + 1) * LANES)
        qa_o[:, sl] = (rope(qa[:, sl]) * (HEAD_DIM ** -0.5)).astype(BF16)
    kva = _dot(hb, wkva[...])
    ka_o[...] = rope(kva[:, :KV_WIDTH])
    va_o[...] = kva[:, KV_WIDTH:]
    qg_o[...] = (_dot(hb, wqg[...]) * (GLA_DK ** -0.5)).astype(BF16)
    kg_o[...] = _dot(hb, wkg[...]).astype(BF16)
    vg_o[...] = _dot(hb, wvg[...]).astype(BF16)
    rg_o[...] = _dot(hb, wrg[...]).astype(BF16)
    z = _dot(_dot(hb, wf[...]).astype(BF16), wf2[...]) + bf[...]
    la_o[...] = (jnp.minimum(z, 0.0) - jnp.log1p(jnp.exp(-jnp.abs(z)))) * (1.0 / GLA_GATE_NORMALIZER)
    ga_o[...] = _sigmoid(_dot(hb, wga[...])).astype(BF16)
    gb_o[...] = _sigmoid(_dot(hb, wgb[...])).astype(BF16)


def _inproj(x2d, norm_g, cos_t, sin_t, w, tm):
    n = x2d.shape[0]
    ntab = cos_t.shape[0] // tm
    row = lambda i: (i, 0)
    const = lambda i: (0, 0)
    tab = lambda i: (i % ntab, 0)
    wnames = ("wqa", "wkva", "wqg", "wkg", "wvg", "wrg", "wga", "wgb", "wf", "wf2", "bf")
    wspecs = [pl.BlockSpec(w[k].shape, const) for k in wnames]
    widths = (ATTN_WIDTH, KV_WIDTH, KV_WIDTH, GLA_KEY_WIDTH, GLA_KEY_WIDTH, GLA_VALUE_WIDTH, GLA_VALUE_WIDTH,
              GLA_KEY_WIDTH, D_MODEL, D_MODEL)
    dtypes = (BF16, F32, F32, BF16, BF16, BF16, BF16, F32, BF16, BF16)
    return pl.pallas_call(
        _inproj_kernel,
        grid=(n // tm,),
        in_specs=[pl.BlockSpec((tm, D_MODEL), row), pl.BlockSpec((1, D_MODEL), const),
                  pl.BlockSpec((tm, LANES), tab), pl.BlockSpec((tm, LANES), tab)] + wspecs,
        out_specs=[pl.BlockSpec((tm, wd), row) for wd in widths],
        out_shape=[jax.ShapeDtypeStruct((n, wd), dt) for wd, dt in zip(widths, dtypes)],
        compiler_params=_cparams(("parallel",)),
        name="inproj",
    )(x2d, norm_g, cos_t, sin_t, *[w[k] for k in wnames])


def _softmax_sink_pv(s, sink, v_parts):
    m = sink
    for si in s:
        m = jnp.maximum(m, jnp.max(si, axis=-1, keepdims=True))
    l = jnp.exp(sink - m)
    acc = None
    for si, vi in zip(s, v_parts):
        p = jnp.exp(si - m)
        l = l + jnp.sum(p, axis=-1, keepdims=True)
        pv = _dot(p.astype(BF16), vi)
        acc = pv if acc is None else acc + pv
    return acc / l


def _swa_prompt_kernel(sink_ref, q_ref, kc_ref, kp_ref, vc_ref, vp_ref, o_ref):
    n = pl.program_id(1)
    k2 = jnp.concatenate([kp_ref[...], kc_ref[...]], axis=0).astype(BF16)
    v2 = jnp.concatenate([vp_ref[...], vc_ref[...]], axis=0).astype(BF16)
    t = lax.broadcasted_iota(jnp.int32, (WINDOW, 2 * WINDOW), 0)
    j = lax.broadcasted_iota(jnp.int32, (WINDOW, 2 * WINDOW), 1)
    valid = (j >= t) & (j <= t + WINDOW) & ((j >= WINDOW) | (n > 0))
    for kv in range(ATTN_KV_HEADS):
        kk = k2[:, kv * HEAD_DIM:(kv + 1) * HEAD_DIM]
        vv = v2[:, kv * HEAD_DIM:(kv + 1) * HEAD_DIM]
        for g in range(GROUP):
            h = kv * GROUP + g
            qh = q_ref[:, h * HEAD_DIM:(h + 1) * HEAD_DIM]
            s = jnp.where(valid, _dot_nt(qh, kk), -jnp.inf)
            o = _softmax_sink_pv([s], sink_ref[h], [vv])
            o_ref[:, h * HEAD_DIM:(h + 1) * HEAD_DIM] = o.astype(BF16)


def _swa_prompt(sinks, qa, ka, va, batch, seq):
    nb = seq // WINDOW
    cur = lambda b, n: (b * nb + n, 0)
    prev = lambda b, n: (b * nb + jnp.maximum(n - 1, 0), 0)
    return pl.pallas_call(
        _swa_prompt_kernel,
        grid=(batch, nb),
        in_specs=[pl.BlockSpec(memory_space=pltpu.SMEM),
                  pl.BlockSpec((WINDOW, ATTN_WIDTH), cur),
                  pl.BlockSpec((WINDOW, KV_WIDTH), cur), pl.BlockSpec((WINDOW, KV_WIDTH), prev),
                  pl.BlockSpec((WINDOW, KV_WIDTH), cur), pl.BlockSpec((WINDOW, KV_WIDTH), prev)],
        out_specs=pl.BlockSpec((WINDOW, ATTN_WIDTH), cur),
        out_shape=jax.ShapeDtypeStruct(qa.shape, BF16),
        compiler_params=_cparams(("parallel", "parallel")),
        name="swa_prompt",
    )(sinks, qa, ka, ka, va, va)


def _swa_sample_kernel(sink_ref, q_ref, kn_ref, vn_ref, kn3_ref, vn3_ref, ck_ref, cv_ref, o_ref, nk_ref, nv_ref,
                       *, t_new):
    sb = ck_ref.shape[0]
    rows = sb * t_new
    ck = ck_ref[...].reshape(sb * WINDOW, KV_WIDTH).astype(BF16)
    cv = cv_ref[...].reshape(sb * WINDOW, KV_WIDTH).astype(BF16)
    kn = kn_ref[...].astype(BF16)
    vn = vn_ref[...].astype(BF16)
    qi = lax.broadcasted_iota(jnp.int32, (rows, sb * WINDOW), 0)
    ci = lax.broadcasted_iota(jnp.int32, (rows, sb * WINDOW), 1)
    valid_c = (qi // t_new == ci // WINDOW) & (ci % WINDOW >= qi % t_new)
    qn = lax.broadcasted_iota(jnp.int32, (rows, rows), 0)
    cn = lax.broadcasted_iota(jnp.int32, (rows, rows), 1)
    valid_n = (qn // t_new == cn // t_new) & (cn <= qn)
    for kv in range(ATTN_KV_HEADS):
        cs = slice(kv * HEAD_DIM, (kv + 1) * HEAD_DIM)
        for g in range(GROUP):
            h = kv * GROUP + g
            qh = q_ref[:, h * HEAD_DIM:(h + 1) * HEAD_DIM]
            s_c = jnp.where(valid_c, _dot_nt(qh, ck[:, cs]), -jnp.inf)
            s_n = jnp.where(valid_n, _dot_nt(qh, kn[:, cs]), -jnp.inf)
            o = _softmax_sink_pv([s_c, s_n], sink_ref[h], [cv[:, cs], vn[:, cs]])
            o_ref[:, h * HEAD_DIM:(h + 1) * HEAD_DIM] = o.astype(BF16)
    nk_ref[:, 0:WINDOW - t_new, :] = ck_ref[:, t_new:WINDOW, :]
    nk_ref[:, WINDOW - t_new:WINDOW, :] = kn3_ref[...]
    nv_ref[:, 0:WINDOW - t_new, :] = cv_ref[:, t_new:WINDOW, :]
    nv_ref[:, WINDOW - t_new:WINDOW, :] = vn3_ref[...]


def _swa_sample(sinks, qa, ka, va, cache_k, cache_v, batch, t_new):
    sb = SAMPLE_SEQ_BLOCK
    rows = sb * t_new
    r2 = lambda i: (i, 0)
    r3 = lambda i: (i, 0, 0)
    ka3 = ka.reshape(batch, t_new, KV_WIDTH)
    va3 = va.reshape(batch, t_new, KV_WIDTH)
    return pl.pallas_call(
        functools.partial(_swa_sample_kernel, t_new=t_new),
        grid=(batch // sb,),
        in_specs=[pl.BlockSpec(memory_space=pltpu.SMEM),
                  pl.BlockSpec((rows, ATTN_WIDTH), r2),
                  pl.BlockSpec((rows, KV_WIDTH), r2), pl.BlockSpec((rows, KV_WIDTH), r2),
                  pl.BlockSpec((sb, t_new, KV_WIDTH), r3), pl.BlockSpec((sb, t_new, KV_WIDTH), r3),
                  pl.BlockSpec((sb, WINDOW, KV_WIDTH), r3), pl.BlockSpec((sb, WINDOW, KV_WIDTH), r3)],
        out_specs=[pl.BlockSpec((rows, ATTN_WIDTH), r2),
                   pl.BlockSpec((sb, WINDOW, KV_WIDTH), r3), pl.BlockSpec((sb, WINDOW, KV_WIDTH), r3)],
        out_shape=[jax.ShapeDtypeStruct(qa.shape, BF16),
                   jax.ShapeDtypeStruct(cache_k.shape, F32), jax.ShapeDtypeStruct(cache_v.shape, F32)],
        compiler_params=_cparams(("parallel",)),
        name="swa_sample",
    )(sinks, qa, ka, va, ka3, va3, cache_k, cache_v)


def _gla_constants(c, seg):
    t = np.arange(c)
    sid = t // seg
    same = sid[:, None] == sid[None, :]
    levels = []
    m = seg // 2
    while m >= 1:
        levels.append(m)
        m //= 2
    mats, roles, masks = [], [], []
    for m in levels:
        blk = t // (2 * m)
        second = (t // m) % 2 == 1
        p = blk * 2 * m + m - 1
        u = t[None, :]
        mq = (u > p[:, None]) & (u <= t[:, None])
        mk = (u > t[:, None]) & (u <= p[:, None])
        mats.append(np.where(second[:, None], mq, mk))
        roles.append(np.broadcast_to(second[:, None], (c, LANES)))
        masks.append((blk[:, None] == blk[None, :]) & second[:, None] & ~second[None, :])
    masks.append(np.eye(c, dtype=bool))
    mats.append(same & (t[None, :] <= t[:, None]))
    mats.append(same & (t[None, :] > t[:, None]))
    mall = jnp.asarray(np.concatenate(mats, 0).astype(np.float32), BF16)
    role = jnp.asarray(np.concatenate(roles, 0).astype(np.float32))
    mask = jnp.asarray(np.concatenate(masks, 0).astype(np.float32))
    return len(levels), mall, role, mask


def _gla_scores(qb, kb, la, mall, role_ref, mask_ref, nlev, c):
    hi, lo = _split_bf16(la)
    e = _dot(mall, hi) + _dot(mall, lo)
    qf = qb.astype(F32)
    kf = kb.astype(F32)
    a = mask_ref[nlev * c:(nlev + 1) * c, :] * _dot_nt(qb, kb)
    for lv in range(nlev):
        sl = slice(lv * c, (lv + 1) * c)
        x = (jnp.where(role_ref[sl, :] > 0.5, qf, kf) * jnp.exp(e[sl])).astype(BF16)
        a = a + mask_ref[sl, :] * _dot_nt(x, x)
    return a, e[nlev * c:(nlev + 1) * c], e[(nlev + 1) * c:(nlev + 2) * c], qf, kf, hi, lo


def _gla_out(o, g, r):
    r = r.astype(F32)
    return (_rms(o, g) * (r * _sigmoid(r))).astype(BF16)


def _gla_prompt_kernel(q_ref, k_ref, v_ref, la_ref, r_ref, mall_ref, role_ref, mask_ref, g_ref, o_ref, s_ref,
                       s_scr, *, nlev, nchunks):
    c = GLA_CHUNK
    s_scr[...] = jnp.zeros_like(s_scr)
    ones = jnp.ones((c, LANES), BF16)

    def chunk(i, carry):
        rows = pl.ds(pl.multiple_of(i * c, c), c)
        v = v_ref[rows, :]
        a, b, rem, qf, kf, hi, lo = _gla_scores(q_ref[rows, :], k_ref[rows, :], la_ref[rows, :], mall_ref[...],
                                                role_ref, mask_ref, nlev, c)
        s = s_scr[...]
        o = _dot(a.astype(BF16), v) + _dot((qf * jnp.exp(b)).astype(BF16), s.astype(BF16))
        kt = (kf * jnp.exp(rem)).astype(BF16)
        dec = jnp.exp(_dot_tn(hi, ones) + _dot_tn(lo, ones))
        s_scr[...] = jnp.concatenate([dec] * (GLA_DV // LANES), axis=1) * s + _dot_tn(kt, v)
        o_ref[rows, :] = _gla_out(o, g_ref[...], r_ref[rows, :])
        return carry

    lax.fori_loop(0, nchunks, chunk, 0)
    s_ref[0, 0] = s_scr[...]


def _gla_prompt(qg, kg, vg, la, rg, gnorm, batch, seq):
    nlev, mall, role, mask = _gla_constants(GLA_CHUNK, GLA_CHUNK)
    bh = lambda b, h: (b, h)
    const = lambda b, h: (0, 0)
    return pl.pallas_call(
        functools.partial(_gla_prompt_kernel, nlev=nlev, nchunks=seq // GLA_CHUNK),
        grid=(batch, GLA_HEADS),
        in_specs=[pl.BlockSpec((seq, GLA_DK), bh), pl.BlockSpec((seq, GLA_DK), bh), pl.BlockSpec((seq, GLA_DV), bh),
                  pl.BlockSpec((seq, GLA_DK), bh), pl.BlockSpec((seq, GLA_DV), bh),
                  pl.BlockSpec(mall.shape, const), pl.BlockSpec(role.shape, const), pl.BlockSpec(mask.shape, const),
                  pl.BlockSpec((1, GLA_DV), const)],
        out_specs=[pl.BlockSpec((seq, GLA_DV), bh),
                   pl.BlockSpec((1, 1, GLA_DK, GLA_DV), lambda b, h: (b, h, 0, 0))],
        out_shape=[jax.ShapeDtypeStruct(vg.shape, BF16),
                   jax.ShapeDtypeStruct((batch, GLA_HEADS, GLA_DK, GLA_DV), F32)],
        scratch_shapes=[pltpu.VMEM((GLA_DK, GLA_DV), F32)],
        compiler_params=_cparams(("parallel", "parallel")),
        name="gla_prompt",
    )(qg, kg, vg, la, rg, mall, role, mask, gnorm)


def _gla_sample_kernel(q_ref, k_ref, v_ref, la_ref, r_ref, s0_ref, mall_ref, role_ref, mask_ref, msum_ref, g_ref,
                       o_ref, s_ref, *, nlev, t_new):
    sb = s0_ref.shape[0]
    c = sb * t_new
    v = v_ref[...]
    a, b, rem, qf, kf, hi, lo = _gla_scores(q_ref[...], k_ref[...], la_ref[...], mall_ref[...], role_ref, mask_ref,
                                            nlev, c)
    qe = (qf * jnp.exp(b)).astype(BF16)
    kt_t = jnp.transpose(kf * jnp.exp(rem)).astype(BF16)
    bl_t = jnp.transpose(_dot(msum_ref[...], hi) + _dot(msum_ref[...], lo))
    dec_t = jnp.exp(bl_t)
    seq_of_row = lax.broadcasted_iota(jnp.int32, (c, GLA_DV), 0) // t_new
    o = _dot(a.astype(BF16), v)
    for j in range(sb):
        s0 = s0_ref[j, 0]
        mine = seq_of_row == j
        o = o + jnp.where(mine, _dot(qe, s0.astype(BF16)), 0.0)
        vj = jnp.where(mine, v, jnp.zeros_like(v))
        s_ref[j, 0] = dec_t[:, j:j + 1] * s0 + _dot(kt_t, vj)
    o_ref[...] = _gla_out(o, g_ref[...], r_ref[...])


def _gla_sample(qg, kg, vg, la, rg, state, gnorm, batch, t_new):
    sb = SAMPLE_SEQ_BLOCK
    c = sb * t_new
    nlev, mall, role, mask = _gla_constants(c, t_new)
    msum = jnp.asarray((np.arange(c)[None, :] // t_new == np.arange(sb)[:, None]).astype(np.float32), BF16)
    rh = lambda i, h: (i, h)
    const = lambda i, h: (0, 0)
    st = lambda i, h: (i, h, 0, 0)
    return pl.pallas_call(
        functools.partial(_gla_sample_kernel, nlev=nlev, t_new=t_new),
        grid=(batch // sb, GLA_HEADS),
        in_specs=[pl.BlockSpec((c, GLA_DK), rh), pl.BlockSpec((c, GLA_DK), rh), pl.BlockSpec((c, GLA_DV), rh),
                  pl.BlockSpec((c, GLA_DK), rh), pl.BlockSpec((c, GLA_DV), rh),
                  pl.BlockSpec((sb, 1, GLA_DK, GLA_DV), st),
                  pl.BlockSpec(mall.shape, const), pl.BlockSpec(role.shape, const), pl.BlockSpec(mask.shape, const),
                  pl.BlockSpec(msum.shape, const), pl.BlockSpec((1, GLA_DV), const)],
        out_specs=[pl.BlockSpec((c, GLA_DV), rh), pl.BlockSpec((sb, 1, GLA_DK, GLA_DV), st)],
        out_shape=[jax.ShapeDtypeStruct(vg.shape, BF16), jax.ShapeDtypeStruct(state.shape, F32)],
        compiler_params=_cparams(("parallel", "parallel")),
        name="gla_sample",
    )(qg, kg, vg, la, rg, state, mall, role, mask, msum, gnorm)


def _post_kernel(x_ref, a_ref, gl_ref, ga_ref, gb_ref, wpa, wpg, wo, nf_ref, wr, br, x1_o, h2_o, comb_o):
    pa = _dot(a_ref[...], wpa[...])
    pg = _dot(gl_ref[...], wpg[...])
    merged = ga_ref[...].astype(F32) * pa + gb_ref[...].astype(F32) * pg
    x1 = x_ref[...] + _dot(merged.astype(BF16), wo[...])
    x1_o[...] = x1
    h2 = _rms(x1, nf_ref[...])
    h2_o[...] = h2.astype(BF16)
    logits = jnp.dot(h2, wr[...], precision=lax.Precision.HIGHEST, preferred_element_type=F32) + br[...]
    lane = lax.broadcasted_iota(jnp.int32, logits.shape, 1)
    big = jnp.int32(4 * LANES)
    ninf = -jnp.inf

    def first_max(vals):
        mx = jnp.max(vals, axis=-1, keepdims=True)
        return mx, jnp.min(jnp.where(vals == mx, lane, big), axis=-1, keepdims=True)

    gl = jnp.where((lane >= N_EXPERTS) & (lane < N_EXPERTS + N_GROUPS), logits, ninf)
    gmax, gidx = first_max(gl)
    p_sel = 1.0 / jnp.sum(jnp.exp(gl - gmax), axis=-1, keepdims=True)
    lo = (gidx - N_EXPERTS) * EXPERTS_PER_GROUP
    el = jnp.where((lane >= lo) & (lane < lo + EXPERTS_PER_GROUP), logits, ninf)
    v1, i1 = first_max(el)
    el2 = jnp.where(lane == i1, ninf, el)
    v2, i2 = first_max(el2)
    t = jnp.exp(v2 - v1)
    w1 = p_sel / (1.0 + t)
    w2 = p_sel * t / (1.0 + t)
    comb_o[...] = jnp.where(lane == i1, w1, 0.0) + jnp.where(lane == i2, w2, 0.0)


def _post(x2d, a_out, g_out, ga, gb, w, tm):
    n = x2d.shape[0]
    row = lambda i: (i, 0)
    const = lambda i: (0, 0)
    wnames = ("wpa", "wpg", "wo", "nf", "wr", "br")
    return pl.pallas_call(
        _post_kernel,
        grid=(n // tm,),
        in_specs=[pl.BlockSpec((tm, D_MODEL), row), pl.BlockSpec((tm, ATTN_WIDTH), row),
                  pl.BlockSpec((tm, GLA_VALUE_WIDTH), row), pl.BlockSpec((tm, D_MODEL), row),
                  pl.BlockSpec((tm, D_MODEL), row)] + [pl.BlockSpec(w[k].shape, const) for k in wnames],
        out_specs=[pl.BlockSpec((tm, D_MODEL), row), pl.BlockSpec((tm, D_MODEL), row), pl.BlockSpec((tm, LANES), row)],
        out_shape=[jax.ShapeDtypeStruct((n, D_MODEL), F32), jax.ShapeDtypeStruct((n, D_MODEL), BF16),
                   jax.ShapeDtypeStruct((n, LANES), F32)],
        compiler_params=_cparams(("parallel",)),
        name="post_mixer",
    )(x2d, a_out, g_out, ga, gb, *[w[k] for k in wnames])


def _moe_kernel(x1_ref, h2_ref, comb_ref, wg_ref, wu_ref, wd_ref, nfin_ref, y_ref, acc):
    e = pl.program_id(1)

    @pl.when(e == 0)
    def _():
        acc[...] = x1_ref[...]

    comb = comb_ref[...]
    lane = lax.broadcasted_iota(jnp.int32, comb.shape, 1)
    ce = jnp.sum(jnp.where(lane == e, comb, 0.0), axis=-1, keepdims=True)
    h = h2_ref[...]
    a = _dot(h, wg_ref[0])
    u = _dot(h, wu_ref[0])
    act = (a * _sigmoid(a)) * u * ce
    acc[...] += _dot(act.astype(BF16), wd_ref[0])

    @pl.when(e == pl.num_programs(1) - 1)
    def _():
        y_ref[...] = _rms(acc[...], nfin_ref[...])


def _moe(x1, h2, comb, weg, weu, wed, nfin, tm):
    n = x1.shape[0]
    row = lambda i, e: (i, 0)
    ex = lambda i, e: (e, 0, 0)
    return pl.pallas_call(
        _moe_kernel,
        grid=(n // tm, N_EXPERTS),
        in_specs=[pl.BlockSpec((tm, D_MODEL), row), pl.BlockSpec((tm, D_MODEL), row), pl.BlockSpec((tm, LANES), row),
                  pl.BlockSpec((1, D_MODEL, EXPERT_FF), ex), pl.BlockSpec((1, D_MODEL, EXPERT_FF), ex),
                  pl.BlockSpec((1, EXPERT_FF, D_MODEL), ex), pl.BlockSpec((1, D_MODEL), lambda i, e: (0, 0))],
        out_specs=pl.BlockSpec((tm, D_MODEL), row),
        out_shape=jax.ShapeDtypeStruct((n, D_MODEL), F32),
        scratch_shapes=[pltpu.VMEM((tm, D_MODEL), F32)],
        compiler_params=_cparams(("parallel", "arbitrary")),
        name="moe",
    )(x1, h2, comb, weg, weu, wed, nfin)


def _rope_tables(positions):
    half = HEAD_DIM // 2
    inv_freq = ROPE_THETA ** (-jnp.arange(half, dtype=F32) / half)
    ang = positions.astype(F32)[:, None] * inv_freq[None, :]
    cos, sin = jnp.cos(ang), jnp.sin(ang)
    reps = LANES // HEAD_DIM
    return (jnp.tile(jnp.concatenate([cos, cos], -1), (1, reps)),
            jnp.tile(jnp.concatenate([-sin, sin], -1), (1, reps)))


def _prep_weights(norm_mix, w_in, w_gla_f2, b_gla_f, gla_norm, w_proj_attn, w_proj_gla, w_out, norm_ffn,
                  w_router_group, b_router_group, w_router_expert, b_router_expert):
    widths = (ATTN_WIDTH, 2 * KV_WIDTH, GLA_KEY_WIDTH, GLA_KEY_WIDTH, GLA_VALUE_WIDTH, GLA_VALUE_WIDTH,
              GLA_GATE_RANK, D_MODEL, D_MODEL)
    names = ("wqa", "wkva", "wqg", "wkg", "wvg", "wrg", "wf", "wga", "wgb")
    w, off = {}, 0
    for nm, wd in zip(names, widths):
        w[nm] = w_in[:, off:off + wd].astype(BF16)
        off += wd
    w["wf"] = jnp.pad(w["wf"], ((0, 0), (0, LANES - GLA_GATE_RANK)))
    w["wf2"] = jnp.pad(w_gla_f2.astype(BF16), ((0, LANES - GLA_GATE_RANK), (0, 0)))
    w["bf"] = b_gla_f.reshape(1, -1)
    w["norm_mix"] = norm_mix.reshape(1, -1)
    w["gla_norm"] = gla_norm.reshape(1, -1)
    w["wpa"] = w_proj_attn.astype(BF16)
    w["wpg"] = w_proj_gla.astype(BF16)
    w["wo"] = w_out.astype(BF16)
    w["nf"] = norm_ffn.reshape(1, -1)
    pad = LANES - N_EXPERTS - N_GROUPS
    w["wr"] = jnp.pad(jnp.concatenate([w_router_expert, w_router_group], axis=1), ((0, 0), (0, pad)))
    w["br"] = jnp.pad(jnp.concatenate([b_router_expert, b_router_group]), (0, pad)).reshape(1, -1)
    return w


def _layer(x, positions_tab, cache, w, sinks, weg, weu, wed, nfin, tm):
    batch, seq, _ = x.shape
    n = batch * seq
    x2d = x.reshape(n, D_MODEL)
    cos_t, sin_t = positions_tab
    qa, ka, va, qg, kg, vg, rg, la, ga, gb = _inproj(x2d, w["norm_mix"], cos_t, sin_t, w, tm)
    if cache is None:
        a_out = _swa_prompt(sinks, qa, ka, va, batch, seq)
        new_k = ka.reshape(batch, seq, ATTN_KV_HEADS, HEAD_DIM)[:, -WINDOW:]
        new_v = va.reshape(batch, seq, ATTN_KV_HEADS, HEAD_DIM)[:, -WINDOW:]
        g_out, new_s = _gla_prompt(qg, kg, vg, la, rg, w["gla_norm"], batch, seq)
    else:
        cache_k, cache_v, state = cache
        a_out, new_k, new_v = _swa_sample(sinks, qa, ka, va, cache_k.reshape(batch, WINDOW, KV_WIDTH),
                                          cache_v.reshape(batch, WINDOW, KV_WIDTH), batch, seq)
        new_k = new_k.reshape(batch, WINDOW, ATTN_KV_HEADS, HEAD_DIM)
        new_v = new_v.reshape(batch, WINDOW, ATTN_KV_HEADS, HEAD_DIM)
        g_out, new_s = _gla_sample(qg, kg, vg, la, rg, state, w["gla_norm"], batch, seq)
    x1, h2, comb = _post(x2d, a_out, g_out, ga, gb, w, tm)
    y = _moe(x1, h2, comb, weg, weu, wed, nfin, tm)
    return y.reshape(batch, seq, D_MODEL), new_k, new_v, new_s


def kernel(x_prompt, x_sample, cache_win_k, cache_win_v, state_gla, norm_mix, w_in, w_gla_f2, b_gla_f, gla_norm,
           attn_sinks, w_proj_attn, w_proj_gla, w_out, norm_ffn, w_router_group, b_router_group, w_router_expert,
           b_router_expert, w_exp_gate, w_exp_up, w_exp_down, norm_final):
    assert norm_mix.shape[0] == 1, "single-layer step"
    seq_p = x_prompt.shape[1]
    dec_b, dec_t = x_sample.shape[0], x_sample.shape[1]
    w = _prep_weights(norm_mix[0], w_in[0], w_gla_f2[0], b_gla_f[0], gla_norm[0], w_proj_attn[0], w_proj_gla[0],
                      w_out[0], norm_ffn[0], w_router_group[0], b_router_group[0], w_router_expert[0],
                      b_router_expert[0])
    weg = w_exp_gate[0].astype(BF16)
    weu = w_exp_up[0].astype(BF16)
    wed = w_exp_down[0].astype(BF16)
    nfin = norm_final.reshape(1, -1)
    sinks = attn_sinks[0]
    tab_p = _rope_tables(jnp.arange(seq_p, dtype=jnp.int32))
    pos_s = PAST_LEN + jnp.arange(dec_t, dtype=jnp.int32)
    tab_s = tuple(jnp.tile(t, (dec_b, 1)) for t in _rope_tables(pos_s))
    tm_p = min(512, seq_p)
    tm_s = dec_b * dec_t
    yp, pk, pv, ps = _layer(x_prompt, tab_p, None, w, sinks, weg, weu, wed, nfin, tm_p)
    ys, sk, sv, ss = _layer(x_sample, tab_s, (cache_win_k[0], cache_win_v[0], state_gla[0]), w, sinks, weg, weu, wed,
                            nfin, tm_s)
    return (yp, ys, pk[None], pv[None], ps[None], sk[None], sv[None], ss[None])
```

```python
import functools

import numpy as np
import jax
import jax.numpy as jnp
from jax import lax
from jax.experimental import pallas as pl
from jax.experimental.pallas import tpu as pltpu

F32 = jnp.float32
BF16 = jnp.bfloat16

D_MODEL = 1024
ATTN_HEADS = 8
ATTN_KV_HEADS = 2
GROUP = ATTN_HEADS // ATTN_KV_HEADS
HEAD_DIM = 64
ATTN_WIDTH = ATTN_HEADS * HEAD_DIM
KV_WIDTH = ATTN_KV_HEADS * HEAD_DIM
WINDOW = 128
ROPE_THETA = 10000.0
PAST_LEN = 8192
GLA_HEADS = 4
GLA_KEY_WIDTH = D_MODEL // 2
GLA_VALUE_WIDTH = D_MODEL
GLA_DK = GLA_KEY_WIDTH // GLA_HEADS
GLA_DV = GLA_VALUE_WIDTH // GLA_HEADS
GLA_GATE_RANK = 16
GLA_GATE_NORMALIZER = 16.0
N_GROUPS = 4
EXPERTS_PER_GROUP = 8
N_EXPERTS = N_GROUPS * EXPERTS_PER_GROUP
EXPERT_FF = 256
EPS = 1e-6

LANES = 128
GLA_CHUNK = 128
SAMPLE_SEQ_BLOCK = 8
MOE_TILE = 256
ROW_EXT = D_MODEL + LANES
VMEM_LIMIT = 56 * 1024 * 1024


def _cparams(sem):
    return pltpu.CompilerParams(dimension_semantics=sem, vmem_limit_bytes=VMEM_LIMIT)


def _rms(x, g):
    return x * lax.rsqrt(jnp.mean(x * x, axis=-1, keepdims=True) + EPS) * g


def _sigmoid(x):
    return 1.0 / (1.0 + jnp.exp(-x))


def _dot(a, b):
    return jnp.dot(a, b, preferred_element_type=F32)


def _dot_nt(a, b):
    return lax.dot_general(a, b, (((1,), (1,)), ((), ())), preferred_element_type=F32)


def _dot_tn(a, b):
    return lax.dot_general(a, b, (((0,), (0,)), ((), ())), preferred_element_type=F32)


def _split_bf16(x):
    hi = x.astype(BF16)
    lo = (x - hi.astype(F32)).astype(BF16)
    return hi, lo


def _inproj_kernel(x_ref, g_ref, cos_ref, sin_ref, wqa, wkva, wqg, wkg, wvg, wrg, wga, wgb, wf, wf2, bf,
                   qa_o, ka_o, va_o, qg_o, kg_o, vg_o, rg_o, la_o, ga_o, gb_o):
    hb = _rms(x_ref[...], g_ref[...]).astype(BF16)
    cos = cos_ref[...]
    sin = sin_ref[...]
    lane = lax.broadcasted_iota(jnp.int32, cos.shape, 1)
    first_half = (lane % HEAD_DIM) < (HEAD_DIM // 2)

    def rope(t):
        swapped = jnp.where(first_half, pltpu.roll(t, LANES - HEAD_DIM // 2, 1), pltpu.roll(t, HEAD_DIM // 2, 1))
        return t * cos + swapped * sin

    qa = _dot(hb, wqa[...])
    for c in range(ATTN_WIDTH // LANES):
        sl = slice(c * LANES, (c + 1) * LANES)
        qa_o[:, sl] = (rope(qa[:, sl]) * (HEAD_DIM ** -0.5)).astype(BF16)
    kva = _dot(hb, wkva[...])
    ka_o[...] = rope(kva[:, :KV_WIDTH])
    va_o[...] = kva[:, KV_WIDTH:]
    qg_o[...] = (_dot(hb, wqg[...]) * (GLA_DK ** -0.5)).astype(BF16)
    kg_o[...] = _dot(hb, wkg[...]).astype(BF16)
    vg_o[...] = _dot(hb, wvg[...]).astype(BF16)
    rg_o[...] = _dot(hb, wrg[...]).astype(BF16)
    z = _dot(_dot(hb, wf[...]).astype(BF16), wf2[...]) + bf[...]
    la_o[...] = (jnp.minimum(z, 0.0) - jnp.log1p(jnp.exp(-jnp.abs(z)))) * (1.0 / GLA_GATE_NORMALIZER)
    ga_o[...] = _sigmoid(_dot(hb, wga[...])).astype(BF16)
    gb_o[...] = _sigmoid(_dot(hb, wgb[...])).astype(BF16)


def _inproj(x2d, norm_g, cos_t, sin_t, w, tm):
    n = x2d.shape[0]
    ntab = cos_t.shape[0] // tm
    row = lambda i: (i, 0)
    const = lambda i: (0, 0)
    tab = lambda i: (i % ntab, 0)
    wnames = ("wqa", "wkva", "wqg", "wkg", "wvg", "wrg", "wga", "wgb", "wf", "wf2", "bf")
    wspecs = [pl.BlockSpec(w[k].shape, const) for k in wnames]
    widths = (ATTN_WIDTH, KV_WIDTH, KV_WIDTH, GLA_KEY_WIDTH, GLA_KEY_WIDTH, GLA_VALUE_WIDTH, GLA_VALUE_WIDTH,
              GLA_KEY_WIDTH, D_MODEL, D_MODEL)
    dtypes = (BF16, F32, F32, BF16, BF16, BF16, BF16, F32, BF16, BF16)
    return pl.pallas_call(
        _inproj_kernel,
        grid=(n // tm,),
        in_specs=[pl.BlockSpec((tm, D_MODEL), row), pl.BlockSpec((1, D_MODEL), const),
                  pl.BlockSpec((tm, LANES), tab), pl.BlockSpec((tm, LANES), tab)] + wspecs,
        out_specs=[pl.BlockSpec((tm, wd), row) for wd in widths],
        out_shape=[jax.ShapeDtypeStruct((n, wd), dt) for wd, dt in zip(widths, dtypes)],
        compiler_params=_cparams(("parallel",)),
        name="inproj",
    )(x2d, norm_g, cos_t, sin_t, *[w[k] for k in wnames])


def _softmax_sink_pv(s, sink, v_parts):
    m = sink
    for si in s:
        m = jnp.maximum(m, jnp.max(si, axis=-1, keepdims=True))
    l = jnp.exp(sink - m)
    acc = None
    for si, vi in zip(s, v_parts):
        p = jnp.exp(si - m)
        l = l + jnp.sum(p, axis=-1, keepdims=True)
        pv = _dot(p.astype(BF16), vi)
        acc = pv if acc is None else acc + pv
    return acc / l


def _swa_prompt_kernel(sink_ref, q_ref, kc_ref, kp_ref, vc_ref, vp_ref, o_ref):
    n = pl.program_id(1)
    k2 = jnp.concatenate([kp_ref[...], kc_ref[...]], axis=0).astype(BF16)
    v2 = jnp.concatenate([vp_ref[...], vc_ref[...]], axis=0).astype(BF16)
    t = lax.broadcasted_iota(jnp.int32, (WINDOW, 2 * WINDOW), 0)
    j = lax.broadcasted_iota(jnp.int32, (WINDOW, 2 * WINDOW), 1)
    valid = (j >= t) & (j <= t + WINDOW) & ((j >= WINDOW) | (n > 0))
    for kv in range(ATTN_KV_HEADS):
        kk = k2[:, kv * HEAD_DIM:(kv + 1) * HEAD_DIM]
        vv = v2[:, kv * HEAD_DIM:(kv + 1) * HEAD_DIM]
        for g in range(GROUP):
            h = kv * GROUP + g
            qh = q_ref[:, h * HEAD_DIM:(h + 1) * HEAD_DIM]
            s = jnp.where(valid, _dot_nt(qh, kk), -jnp.inf)
            o = _softmax_sink_pv([s], sink_ref[h], [vv])
            o_ref[:, h * HEAD_DIM:(h + 1) * HEAD_DIM] = o.astype(BF16)


def _swa_prompt(sinks, qa, ka, va, batch, seq):
    nb = seq // WINDOW
    cur = lambda b, n: (b * nb + n, 0)
    prev = lambda b, n: (b * nb + jnp.maximum(n - 1, 0), 0)
    return pl.pallas_call(
        _swa_prompt_kernel,
        grid=(batch, nb),
        in_specs=[pl.BlockSpec(memory_space=pltpu.SMEM),
                  pl.BlockSpec((WINDOW, ATTN_WIDTH), cur),
                  pl.BlockSpec((WINDOW, KV_WIDTH), cur), pl.BlockSpec((WINDOW, KV_WIDTH), prev),
                  pl.BlockSpec((WINDOW, KV_WIDTH), cur), pl.BlockSpec((WINDOW, KV_WIDTH), prev)],
        out_specs=pl.BlockSpec((WINDOW, ATTN_WIDTH), cur),
        out_shape=jax.ShapeDtypeStruct(qa.shape, BF16),
        compiler_params=_cparams(("parallel", "parallel")),
        name="swa_prompt",
    )(sinks, qa, ka, ka, va, va)


def _swa_sample_kernel(sink_ref, q_ref, kn_ref, vn_ref, kn3_ref, vn3_ref, ck_ref, cv_ref, o_ref, nk_ref, nv_ref,
                       *, t_new):
    sb = ck_ref.shape[0]
    rows = sb * t_new
    ck = ck_ref[...].reshape(sb * WINDOW, KV_WIDTH).astype(BF16)
    cv = cv_ref[...].reshape(sb * WINDOW, KV_WIDTH).astype(BF16)
    kn = kn_ref[...].astype(BF16)
    vn = vn_ref[...].astype(BF16)
    qi = lax.broadcasted_iota(jnp.int32, (rows, sb * WINDOW), 0)
    ci = lax.broadcasted_iota(jnp.int32, (rows, sb * WINDOW), 1)
    valid_c = (qi // t_new == ci // WINDOW) & (ci % WINDOW >= qi % t_new)
    qn = lax.broadcasted_iota(jnp.int32, (rows, rows), 0)
    cn = lax.broadcasted_iota(jnp.int32, (rows, rows), 1)
    valid_n = (qn // t_new == cn // t_new) & (cn <= qn)
    for kv in range(ATTN_KV_HEADS):
        cs = slice(kv * HEAD_DIM, (kv + 1) * HEAD_DIM)
        for g in range(GROUP):
            h = kv * GROUP + g
            qh = q_ref[:, h * HEAD_DIM:(h + 1) * HEAD_DIM]
            s_c = jnp.where(valid_c, _dot_nt(qh, ck[:, cs]), -jnp.inf)
            s_n = jnp.where(valid_n, _dot_nt(qh, kn[:, cs]), -jnp.inf)
            o = _softmax_sink_pv([s_c, s_n], sink_ref[h], [cv[:, cs], vn[:, cs]])
            o_ref[:, h * HEAD_DIM:(h + 1) * HEAD_DIM] = o.astype(BF16)
    nk_ref[:, 0:WINDOW - t_new, :] = ck_ref[:, t_new:WINDOW, :]
    nk_ref[:, WINDOW - t_new:WINDOW, :] = kn3_ref[...]
    nv_ref[:, 0:WINDOW - t_new, :] = cv_ref[:, t_new:WINDOW, :]
    nv_ref[:, WINDOW - t_new:WINDOW, :] = vn3_ref[...]


def _swa_sample(sinks, qa, ka, va, cache_k, cache_v, batch, t_new):
    sb = SAMPLE_SEQ_BLOCK
    rows = sb * t_new
    r2 = lambda i: (i, 0)
    r3 = lambda i: (i, 0, 0)
    ka3 = ka.reshape(batch, t_new, KV_WIDTH)
    va3 = va.reshape(batch, t_new, KV_WIDTH)
    return pl.pallas_call(
        functools.partial(_swa_sample_kernel, t_new=t_new),
        grid=(batch // sb,),
        in_specs=[pl.BlockSpec(memory_space=pltpu.SMEM),
                  pl.BlockSpec((rows, ATTN_WIDTH), r2),
                  pl.BlockSpec((rows, KV_WIDTH), r2), pl.BlockSpec((rows, KV_WIDTH), r2),
                  pl.BlockSpec((sb, t_new, KV_WIDTH), r3), pl.BlockSpec((sb, t_new, KV_WIDTH), r3),
                  pl.BlockSpec((sb, WINDOW, KV_WIDTH), r3), pl.BlockSpec((sb, WINDOW, KV_WIDTH), r3)],
        out_specs=[pl.BlockSpec((rows, ATTN_WIDTH), r2),
                   pl.BlockSpec((sb, WINDOW, KV_WIDTH), r3), pl.BlockSpec((sb, WINDOW, KV_WIDTH), r3)],
        out_shape=[jax.ShapeDtypeStruct(qa.shape, BF16),
                   jax.ShapeDtypeStruct(cache_k.shape, F32), jax.ShapeDtypeStruct(cache_v.shape, F32)],
        compiler_params=_cparams(("parallel",)),
        name="swa_sample",
    )(sinks, qa, ka, va, ka3, va3, cache_k, cache_v)


def _gla_constants(c, seg):
    t = np.arange(c)
    sid = t // seg
    same = sid[:, None] == sid[None, :]
    levels = []
    m = seg // 2
    while m >= 1:
        levels.append(m)
        m //= 2
    mats, roles, masks = [], [], []
    for m in levels:
        blk = t // (2 * m)
        second = (t // m) % 2 == 1
        p = blk * 2 * m + m - 1
        u = t[None, :]
        mq = (u > p[:, None]) & (u <= t[:, None])
        mk = (u > t[:, None]) & (u <= p[:, None])
        mats.append(np.where(second[:, None], mq, mk))
        roles.append(np.broadcast_to(second[:, None], (c, LANES)))
        masks.append((blk[:, None] == blk[None, :]) & second[:, None] & ~second[None, :])
    masks.append(np.eye(c, dtype=bool))
    mats.append(same & (t[None, :] <= t[:, None]))
    mats.append(same & (t[None, :] > t[:, None]))
    mall = jnp.asarray(np.concatenate(mats, 0).astype(np.float32), BF16)
    role = jnp.asarray(np.concatenate(roles, 0).astype(np.float32))
    mask = jnp.asarray(np.concatenate(masks, 0).astype(np.float32))
    return len(levels), mall, role, mask


def _gla_scores(qb, kb, la, mall, role_ref, mask_ref, nlev, c):
    hi, lo = _split_bf16(la)
    e = _dot(mall, hi) + _dot(mall, lo)
    qf = qb.astype(F32)
    kf = kb.astype(F32)
    a = mask_ref[nlev * c:(nlev + 1) * c, :] * _dot_nt(qb, kb)
    for lv in range(nlev):
        sl = slice(lv * c, (lv + 1) * c)
        x = (jnp.where(role_ref[sl, :] > 0.5, qf, kf) * jnp.exp(e[sl])).astype(BF16)
        a = a + mask_ref[sl, :] * _dot_nt(x, x)
    return a, e[nlev * c:(nlev + 1) * c], e[(nlev + 1) * c:(nlev + 2) * c], qf, kf, hi, lo


def _gla_out(o, g, r):
    r = r.astype(F32)
    return (_rms(o, g) * (r * _sigmoid(r))).astype(BF16)


def _gla_prompt_kernel(q_ref, k_ref, v_ref, la_ref, r_ref, mall_ref, role_ref, mask_ref, g_ref, o_ref, s_ref,
                       s_scr, *, nlev, nchunks):
    c = GLA_CHUNK
    s_scr[...] = jnp.zeros_like(s_scr)
    ones = jnp.ones((c, LANES), BF16)

    def chunk(i, carry):
        rows = pl.ds(pl.multiple_of(i * c, c), c)
        v = v_ref[rows, :]
        a, b, rem, qf, kf, hi, lo = _gla_scores(q_ref[rows, :], k_ref[rows, :], la_ref[rows, :], mall_ref[...],
                                                role_ref, mask_ref, nlev, c)
        s = s_scr[...]
        o = _dot(a.astype(BF16), v) + _dot((qf * jnp.exp(b)).astype(BF16), s.astype(BF16))
        kt = (kf * jnp.exp(rem)).astype(BF16)
        dec = jnp.exp(_dot_tn(hi, ones) + _dot_tn(lo, ones))
        s_scr[...] = jnp.concatenate([dec] * (GLA_DV // LANES), axis=1) * s + _dot_tn(kt, v)
        o_ref[rows, :] = _gla_out(o, g_ref[...], r_ref[rows, :])
        return carry

    lax.fori_loop(0, nchunks, chunk, 0)
    s_ref[0, 0] = s_scr[...]


def _gla_prompt(qg, kg, vg, la, rg, gnorm, batch, seq):
    nlev, mall, role, mask = _gla_constants(GLA_CHUNK, GLA_CHUNK)
    bh = lambda b, h: (b, h)
    const = lambda b, h: (0, 0)
    return pl.pallas_call(
        functools.partial(_gla_prompt_kernel, nlev=nlev, nchunks=seq // GLA_CHUNK),
        grid=(batch, GLA_HEADS),
        in_specs=[pl.BlockSpec((seq, GLA_DK), bh), pl.BlockSpec((seq, GLA_DK), bh), pl.BlockSpec((seq, GLA_DV), bh),
                  pl.BlockSpec((seq, GLA_DK), bh), pl.BlockSpec((seq, GLA_DV), bh),
                  pl.BlockSpec(mall.shape, const), pl.BlockSpec(role.shape, const), pl.BlockSpec(mask.shape, const),
                  pl.BlockSpec((1, GLA_DV), const)],
        out_specs=[pl.BlockSpec((seq, GLA_DV), bh),
                   pl.BlockSpec((1, 1, GLA_DK, GLA_DV), lambda b, h: (b, h, 0, 0))],
        out_shape=[jax.ShapeDtypeStruct(vg.shape, BF16),
                   jax.ShapeDtypeStruct((batch, GLA_HEADS, GLA_DK, GLA_DV), F32)],
        scratch_shapes=[pltpu.VMEM((GLA_DK, GLA_DV), F32)],
        compiler_params=_cparams(("parallel", "parallel")),
        name="gla_prompt",
    )(qg, kg, vg, la, rg, mall, role, mask, gnorm)


def _gla_sample_kernel(q_ref, k_ref, v_ref, la_ref, r_ref, s0_ref, mall_ref, role_ref, mask_ref, msum_ref, g_ref,
                       o_ref, s_ref, *, nlev, t_new):
    sb = s0_ref.shape[0]
    c = sb * t_new
    v = v_ref[...]
    a, b, rem, qf, kf, hi, lo = _gla_scores(q_ref[...], k_ref[...], la_ref[...], mall_ref[...], role_ref, mask_ref,
                                            nlev, c)
    qe = (qf * jnp.exp(b)).astype(BF16)
    kt_t = jnp.transpose(kf * jnp.exp(rem)).astype(BF16)
    bl_t = jnp.transpose(_dot(msum_ref[...], hi) + _dot(msum_ref[...], lo))
    dec_t = jnp.exp(bl_t)
    seq_of_row = lax.broadcasted_iota(jnp.int32, (c, GLA_DV), 0) // t_new
    o = _dot(a.astype(BF16), v)
    for j in range(sb):
        s0 = s0_ref[j, 0]
        mine = seq_of_row == j
        o = o + jnp.where(mine, _dot(qe, s0.astype(BF16)), 0.0)
        vj = jnp.where(mine, v, jnp.zeros_like(v))
        s_ref[j, 0] = dec_t[:, j:j + 1] * s0 + _dot(kt_t, vj)
    o_ref[...] = _gla_out(o, g_ref[...], r_ref[...])


def _gla_sample(qg, kg, vg, la, rg, state, gnorm, batch, t_new):
    sb = SAMPLE_SEQ_BLOCK
    c = sb * t_new
    nlev, mall, role, mask = _gla_constants(c, t_new)
    msum = jnp.asarray((np.arange(c)[None, :] // t_new == np.arange(sb)[:, None]).astype(np.float32), BF16)
    rh = lambda i, h: (i, h)
    const = lambda i, h: (0, 0)
    st = lambda i, h: (i, h, 0, 0)
    return pl.pallas_call(
        functools.partial(_gla_sample_kernel, nlev=nlev, t_new=t_new),
        grid=(batch // sb, GLA_HEADS),
        in_specs=[pl.BlockSpec((c, GLA_DK), rh), pl.BlockSpec((c, GLA_DK), rh), pl.BlockSpec((c, GLA_DV), rh),
                  pl.BlockSpec((c, GLA_DK), rh), pl.BlockSpec((c, GLA_DV), rh),
                  pl.BlockSpec((sb, 1, GLA_DK, GLA_DV), st),
                  pl.BlockSpec(mall.shape, const), pl.BlockSpec(role.shape, const), pl.BlockSpec(mask.shape, const),
                  pl.BlockSpec(msum.shape, const), pl.BlockSpec((1, GLA_DV), const)],
        out_specs=[pl.BlockSpec((c, GLA_DV), rh), pl.BlockSpec((sb, 1, GLA_DK, GLA_DV), st)],
        out_shape=[jax.ShapeDtypeStruct(vg.shape, BF16), jax.ShapeDtypeStruct(state.shape, F32)],
        compiler_params=_cparams(("parallel", "parallel")),
        name="gla_sample",
    )(qg, kg, vg, la, rg, state, mall, role, mask, msum, gnorm)


def _post_kernel(x_ref, a_ref, gl_ref, ga_ref, gb_ref, wpa, wpg, wo, nf_ref, wr, br, x1e_o, ids_o):
    pa = _dot(a_ref[...], wpa[...])
    pg = _dot(gl_ref[...], wpg[...])
    merged = ga_ref[...].astype(F32) * pa + gb_ref[...].astype(F32) * pg
    x1 = x_ref[...] + _dot(merged.astype(BF16), wo[...])
    x1e_o[:, :D_MODEL] = x1
    h2 = _rms(x1, nf_ref[...])
    logits = jnp.dot(h2, wr[...], precision=lax.Precision.HIGHEST, preferred_element_type=F32) + br[...]
    lane = lax.broadcasted_iota(jnp.int32, logits.shape, 1)
    big = jnp.int32(4 * LANES)
    ninf = -jnp.inf

    def first_max(vals):
        mx = jnp.max(vals, axis=-1, keepdims=True)
        return mx, jnp.min(jnp.where(vals == mx, lane, big), axis=-1, keepdims=True)

    gl = jnp.where((lane >= N_EXPERTS) & (lane < N_EXPERTS + N_GROUPS), logits, ninf)
    gmax, gidx = first_max(gl)
    p_sel = 1.0 / jnp.sum(jnp.exp(gl - gmax), axis=-1, keepdims=True)
    lo = (gidx - N_EXPERTS) * EXPERTS_PER_GROUP
    el = jnp.where((lane >= lo) & (lane < lo + EXPERTS_PER_GROUP), logits, ninf)
    v1, i1 = first_max(el)
    el2 = jnp.where(lane == i1, ninf, el)
    v2, i2 = first_max(el2)
    t = jnp.exp(v2 - v1)
    w1 = p_sel / (1.0 + t)
    w2 = p_sel * t / (1.0 + t)
    x1e_o[:, D_MODEL:] = jnp.where(lane == i1, w1, 0.0) + jnp.where(lane == i2, w2, 0.0)
    ids_o[...] = jnp.where(lane == 0, i1, jnp.where(lane == 1, i2, 0))


def _post(x2d, a_out, g_out, ga, gb, w, tm):
    n = x2d.shape[0]
    row = lambda i: (i, 0)
    const = lambda i: (0, 0)
    wnames = ("wpa", "wpg", "wo", "nf", "wr", "br")
    return pl.pallas_call(
        _post_kernel,
        grid=(n // tm,),
        in_specs=[pl.BlockSpec((tm, D_MODEL), row), pl.BlockSpec((tm, ATTN_WIDTH), row),
                  pl.BlockSpec((tm, GLA_VALUE_WIDTH), row), pl.BlockSpec((tm, D_MODEL), row),
                  pl.BlockSpec((tm, D_MODEL), row)] + [pl.BlockSpec(w[k].shape, const) for k in wnames],
        out_specs=[pl.BlockSpec((tm, ROW_EXT), row), pl.BlockSpec((tm, LANES), row)],
        out_shape=[jax.ShapeDtypeStruct((n, ROW_EXT), F32), jax.ShapeDtypeStruct((n, LANES), jnp.int32)],
        compiler_params=_cparams(("parallel",)),
        name="post_mixer",
    )(x2d, a_out, g_out, ga, gb, *[w[k] for k in wnames])


def _moe_plan(ids, tme):
    n = ids.shape[0]
    ntiles = n // tme
    max_items = ntiles + N_GROUPS - 1
    i1, i2 = ids[:, 0], ids[:, 1]
    grp = i1 // EXPERTS_PER_GROUP
    lo = jnp.minimum(i1, i2) % EXPERTS_PER_GROUP
    hi = jnp.maximum(i1, i2) % EXPERTS_PER_GROUP
    key = (grp * EXPERTS_PER_GROUP + lo) * EXPERTS_PER_GROUP + hi
    order = jnp.argsort(key, stable=True).astype(jnp.int32)
    skey = key[order].reshape(ntiles, tme)
    sg = skey // (EXPERTS_PER_GROUP * EXPERTS_PER_GROUP)
    slo = (skey // EXPERTS_PER_GROUP) % EXPERTS_PER_GROUP
    shi = skey % EXPERTS_PER_GROUP
    ev = jnp.arange(EXPERTS_PER_GROUP)
    in_g = sg[:, :, None] == jnp.arange(N_GROUPS)
    uses_e = (slo[:, :, None] == ev) | (shi[:, :, None] == ev)
    flags_tge = jnp.any(in_g[:, :, :, None] & uses_e[:, :, None, :], axis=1)
    present = jnp.any(in_g, axis=1).reshape(-1)
    pos = jnp.cumsum(present) - 1
    n_items = pos[-1] + 1
    src = jnp.zeros((max_items,), jnp.int32).at[jnp.where(present, pos, max_items)].set(
        jnp.arange(ntiles * N_GROUPS, dtype=jnp.int32), mode="drop")
    it = jnp.arange(max_items)
    valid = it < n_items
    last_src = src[n_items - 1]
    src = jnp.where(valid, src, last_src)
    item_tile = src // N_GROUPS
    item_group = src % N_GROUPS
    prev_tile = jnp.concatenate([jnp.full((1,), -1, jnp.int32), item_tile[:-1]])
    next_tile = jnp.concatenate([item_tile[1:], jnp.full((1,), -1, jnp.int32)])
    first = valid & (item_tile != prev_tile)
    last = valid & ((item_tile != next_tile) | (it == n_items - 1))
    flags = flags_tge[item_tile, item_group] & valid[:, None]
    return (order, item_tile.astype(jnp.int32), item_group.astype(jnp.int32), first.astype(jnp.int32),
            last.astype(jnp.int32), flags.reshape(-1).astype(jnp.int32))


def _moe_kernel(order, itile, igroup, ifirst, ilast, flags, x_hbm, wg, wu, wd, nffn, nfin, y_hbm,
                xbuf, ybuf, acc, hbuf, gsem, ssem, *, tme, ntiles):
    i = pl.program_id(0)
    t = itile[i]
    slot = t % 2

    def gather_row(tile, sl, r):
        tok = order[tile * tme + r]
        return pltpu.make_async_copy(x_hbm.at[pl.ds(tok, 1)], xbuf.at[sl, pl.ds(r, 1)], gsem.at[sl])

    def scatter_row(tile, sl, r):
        tok = order[tile * tme + r]
        return pltpu.make_async_copy(ybuf.at[sl, pl.ds(r, 1)], y_hbm.at[pl.ds(tok, 1)], ssem.at[sl])

    def start_rows(make, tile, sl):
        def body(r, c):
            make(tile, sl, r).start()
            return c
        lax.fori_loop(0, tme, body, 0, unroll=8)

    def wait_gather(sl):
        pltpu.make_async_copy(x_hbm.at[pl.ds(0, tme)], xbuf.at[sl], gsem.at[sl]).wait()

    def wait_scatter(sl):
        pltpu.make_async_copy(ybuf.at[sl], y_hbm.at[pl.ds(0, tme)], ssem.at[sl]).wait()

    @pl.when(i == 0)
    def _():
        start_rows(gather_row, 0, 0)

    @pl.when(ifirst[i] == 1)
    def _():
        @pl.when(t + 1 < ntiles)
        def _():
            start_rows(gather_row, t + 1, 1 - slot)

        wait_gather(slot)
        x1 = xbuf[slot, :, :D_MODEL]
        acc[...] = x1
        hbuf[...] = _rms(x1, nffn[...]).astype(BF16)

    g = igroup[i]
    for e in range(EXPERTS_PER_GROUP):
        @pl.when(flags[i * EXPERTS_PER_GROUP + e] == 1)
        def _():
            comb = xbuf[slot, :, D_MODEL:]
            lane = lax.broadcasted_iota(jnp.int32, comb.shape, 1)
            ce = jnp.sum(jnp.where(lane == g * EXPERTS_PER_GROUP + e, comb, 0.0), axis=-1, keepdims=True)
            h = hbuf[...]
            a = _dot(h, wg[e])
            u = _dot(h, wu[e])
            act = (a * _sigmoid(a)) * u * ce
            acc[...] += _dot(act.astype(BF16), wd[e])

    @pl.when(ilast[i] == 1)
    def _():
        @pl.when(t >= 2)
        def _():
            wait_scatter(slot)

        ybuf[slot] = _rms(acc[...], nfin[...])
        start_rows(scatter_row, t, slot)

    @pl.when(i == pl.num_programs(0) - 1)
    def _():
        for sl in range(min(2, ntiles)):
            wait_scatter(sl)


def _moe(x1e, ids, weg, weu, wed, nffn, nfin):
    n = x1e.shape[0]
    tme = min(MOE_TILE, n)
    assert n % tme == 0
    ntiles = n // tme
    plan = _moe_plan(ids, tme)
    max_items = ntiles + N_GROUPS - 1
    grp = lambda i, order, itile, igroup, *_: (igroup[i], 0, 0)
    const = lambda i, *_: (0, 0)
    grid_spec = pltpu.PrefetchScalarGridSpec(
        num_scalar_prefetch=len(plan),
        grid=(max_items,),
        in_specs=[pl.BlockSpec(memory_space=pl.ANY),
                  pl.BlockSpec((EXPERTS_PER_GROUP, D_MODEL, EXPERT_FF), grp),
                  pl.BlockSpec((EXPERTS_PER_GROUP, D_MODEL, EXPERT_FF), grp),
                  pl.BlockSpec((EXPERTS_PER_GROUP, EXPERT_FF, D_MODEL), grp),
                  pl.BlockSpec((1, D_MODEL), const), pl.BlockSpec((1, D_MODEL), const)],
        out_specs=pl.BlockSpec(memory_space=pl.ANY),
        scratch_shapes=[pltpu.VMEM((2, tme, ROW_EXT), F32), pltpu.VMEM((2, tme, D_MODEL), F32),
                        pltpu.VMEM((tme, D_MODEL), F32), pltpu.VMEM((tme, D_MODEL), BF16),
                        pltpu.SemaphoreType.DMA((2,)), pltpu.SemaphoreType.DMA((2,))],
    )
    return pl.pallas_call(
        functools.partial(_moe_kernel, tme=tme, ntiles=ntiles),
        grid_spec=grid_spec,
        out_shape=jax.ShapeDtypeStruct((n, D_MODEL), F32),
        compiler_params=_cparams(("arbitrary",)),
        name="moe",
    )(*plan, x1e, weg, weu, wed, nffn, nfin)


def _rope_tables(positions):
    half = HEAD_DIM // 2
    inv_freq = ROPE_THETA ** (-jnp.arange(half, dtype=F32) / half)
    ang = positions.astype(F32)[:, None] * inv_freq[None, :]
    cos, sin = jnp.cos(ang), jnp.sin(ang)
    reps = LANES // HEAD_DIM
    return (jnp.tile(jnp.concatenate([cos, cos], -1), (1, reps)),
            jnp.tile(jnp.concatenate([-sin, sin], -1), (1, reps)))


def _prep_weights(norm_mix, w_in, w_gla_f2, b_gla_f, gla_norm, w_proj_attn, w_proj_gla, w_out, norm_ffn,
                  w_router_group, b_router_group, w_router_expert, b_router_expert):
    widths = (ATTN_WIDTH, 2 * KV_WIDTH, GLA_KEY_WIDTH, GLA_KEY_WIDTH, GLA_VALUE_WIDTH, GLA_VALUE_WIDTH,
              GLA_GATE_RANK, D_MODEL, D_MODEL)
    names = ("wqa", "wkva", "wqg", "wkg", "wvg", "wrg", "wf", "wga", "wgb")
    w, off = {}, 0
    for nm, wd in zip(names, widths):
        w[nm] = w_in[:, off:off + wd].astype(BF16)
        off += wd
    w["wf"] = jnp.pad(w["wf"], ((0, 0), (0, LANES - GLA_GATE_RANK)))
    w["wf2"] = jnp.pad(w_gla_f2.astype(BF16), ((0, LANES - GLA_GATE_RANK), (0, 0)))
    w["bf"] = b_gla_f.reshape(1, -1)
    w["norm_mix"] = norm_mix.reshape(1, -1)
    w["gla_norm"] = gla_norm.reshape(1, -1)
    w["wpa"] = w_proj_attn.astype(BF16)
    w["wpg"] = w_proj_gla.astype(BF16)
    w["wo"] = w_out.astype(BF16)
    w["nf"] = norm_ffn.reshape(1, -1)
    pad = LANES - N_EXPERTS - N_GROUPS
    w["wr"] = jnp.pad(jnp.concatenate([w_router_expert, w_router_group], axis=1), ((0, 0), (0, pad)))
    w["br"] = jnp.pad(jnp.concatenate([b_router_expert, b_router_group]), (0, pad)).reshape(1, -1)
    return w


def _layer(x, positions_tab, cache, w, sinks, weg, weu, wed, nfin, tm):
    batch, seq, _ = x.shape
    n = batch * seq
    x2d = x.reshape(n, D_MODEL)
    cos_t, sin_t = positions_tab
    qa, ka, va, qg, kg, vg, rg, la, ga, gb = _inproj(x2d, w["norm_mix"], cos_t, sin_t, w, tm)
    if cache is None:
        a_out = _swa_prompt(sinks, qa, ka, va, batch, seq)
        new_k = ka.reshape(batch, seq, ATTN_KV_HEADS, HEAD_DIM)[:, -WINDOW:]
        new_v = va.reshape(batch, seq, ATTN_KV_HEADS, HEAD_DIM)[:, -WINDOW:]
        g_out, new_s = _gla_prompt(qg, kg, vg, la, rg, w["gla_norm"], batch, seq)
    else:
        cache_k, cache_v, state = cache
        a_out, new_k, new_v = _swa_sample(sinks, qa, ka, va, cache_k.reshape(batch, WINDOW, KV_WIDTH),
                                          cache_v.reshape(batch, WINDOW, KV_WIDTH), batch, seq)
        new_k = new_k.reshape(batch, WINDOW, ATTN_KV_HEADS, HEAD_DIM)
        new_v = new_v.reshape(batch, WINDOW, ATTN_KV_HEADS, HEAD_DIM)
        g_out, new_s = _gla_sample(qg, kg, vg, la, rg, state, w["gla_norm"], batch, seq)
    x1e, ids = _post(x2d, a_out, g_out, ga, gb, w, tm)
    y = _moe(x1e, ids, weg, weu, wed, w["nf"], nfin)
    return y.reshape(batch, seq, D_MODEL), new_k, new_v, new_s


def kernel(x_prompt, x_sample, cache_win_k, cache_win_v, state_gla, norm_mix, w_in, w_gla_f2, b_gla_f, gla_norm,
           attn_sinks, w_proj_attn, w_proj_gla, w_out, norm_ffn, w_router_group, b_router_group, w_router_expert,
           b_router_expert, w_exp_gate, w_exp_up, w_exp_down, norm_final):
    assert norm_mix.shape[0] == 1, "single-layer step"
    seq_p = x_prompt.shape[1]
    dec_b, dec_t = x_sample.shape[0], x_sample.shape[1]
    w = _prep_weights(norm_mix[0], w_in[0], w_gla_f2[0], b_gla_f[0], gla_norm[0], w_proj_attn[0], w_proj_gla[0],
                      w_out[0], norm_ffn[0], w_router_group[0], b_router_group[0], w_router_expert[0],
                      b_router_expert[0])
    weg = w_exp_gate[0].astype(BF16)
    weu = w_exp_up[0].astype(BF16)
    wed = w_exp_down[0].astype(BF16)
    nfin = norm_final.reshape(1, -1)
    sinks = attn_sinks[0]
    tab_p = _rope_tables(jnp.arange(seq_p, dtype=jnp.int32))
    pos_s = PAST_LEN + jnp.arange(dec_t, dtype=jnp.int32)
    tab_s = tuple(jnp.tile(t, (dec_b, 1)) for t in _rope_tables(pos_s))
    tm_p = min(512, seq_p)
    tm_s = dec_b * dec_t
    yp, pk, pv, ps = _layer(x_prompt, tab_p, None, w, sinks, weg, weu, wed, nfin, tm_p)
    ys, sk, sv, ss = _layer(x_sample, tab_s, (cache_win_k[0], cache_win_v[0], state_gla[0]), w, sinks, weg, weu, wed,
                            nfin, tm_s)
    return (yp, ys, pk[None], pv[None], ps[None], sk[None], sv[None], ss[None])
```

```python
import functools

import numpy as np
import jax
import jax.numpy as jnp
from jax import lax
from jax.experimental import pallas as pl
from jax.experimental.pallas import tpu as pltpu

F32 = jnp.float32
BF16 = jnp.bfloat16

D_MODEL = 1024
ATTN_HEADS = 8
ATTN_KV_HEADS = 2
GROUP = ATTN_HEADS // ATTN_KV_HEADS
HEAD_DIM = 64
ATTN_WIDTH = ATTN_HEADS * HEAD_DIM
KV_WIDTH = ATTN_KV_HEADS * HEAD_DIM
WINDOW = 128
ROPE_THETA = 10000.0
PAST_LEN = 8192
GLA_HEADS = 4
GLA_KEY_WIDTH = D_MODEL // 2
GLA_VALUE_WIDTH = D_MODEL
GLA_DK = GLA_KEY_WIDTH // GLA_HEADS
GLA_DV = GLA_VALUE_WIDTH // GLA_HEADS
GLA_GATE_RANK = 16
GLA_GATE_NORMALIZER = 16.0
N_GROUPS = 4
EXPERTS_PER_GROUP = 8
N_EXPERTS = N_GROUPS * EXPERTS_PER_GROUP
EXPERT_FF = 256
EPS = 1e-6
LOG2_E = 1.4426950408889634

LANES = 128
GLA_CHUNK = 128
GLA_HEADS_PER_STEP = 4
SAMPLE_SEQ_BLOCK = 8
MOE_TILE = 256
ROW_EXT = D_MODEL + LANES
ROUTER_ROWS = 40
VMEM_LIMIT = 56 * 1024 * 1024


def _cparams(sem):
    return pltpu.CompilerParams(dimension_semantics=sem, vmem_limit_bytes=VMEM_LIMIT)


def _rms(x, g):
    return x * lax.rsqrt(jnp.mean(x * x, axis=-1, keepdims=True) + EPS) * g


def _sigmoid(x):
    return 1.0 / (1.0 + jnp.exp(-x))


def _dot(a, b):
    return jnp.dot(a, b, preferred_element_type=F32)


def _dot_nt(a, b):
    return lax.dot_general(a, b, (((1,), (1,)), ((), ())), preferred_element_type=F32)


def _dot_tn(a, b):
    return lax.dot_general(a, b, (((0,), (0,)), ((), ())), preferred_element_type=F32)


def _split_bf16(x):
    hi = x.astype(BF16)
    lo = (x - hi.astype(F32)).astype(BF16)
    return hi, lo


def _inproj_kernel(x_ref, g_ref, cos_ref, sin_ref, wqa, wkva, wqg, wkg, wvg, wrg, wga, wgb, wf, wf2, bf,
                   qa_o, ka_o, va_o, qg_o, kg_o, vg_o, rg_o, la_o, ga_o, gb_o):
    hb = _rms(x_ref[...], g_ref[...]).astype(BF16)
    cos = cos_ref[...]
    sin = sin_ref[...]
    lane = lax.broadcasted_iota(jnp.int32, cos.shape, 1)
    first_half = (lane % HEAD_DIM) < (HEAD_DIM // 2)

    def rope(t):
        swapped = jnp.where(first_half, pltpu.roll(t, LANES - HEAD_DIM // 2, 1), pltpu.roll(t, HEAD_DIM // 2, 1))
        return t * cos + swapped * sin

    qa = _dot(hb, wqa[...])
    for c in range(ATTN_WIDTH // LANES):
        sl = slice(c * LANES, (c + 1) * LANES)
        qa_o[:, sl] = (rope(qa[:, sl]) * (HEAD_DIM ** -0.5)).astype(BF16)
    kva = _dot(hb, wkva[...])
    ka_o[...] = rope(kva[:, :KV_WIDTH])
    va_o[...] = kva[:, KV_WIDTH:]
    qg_o[...] = (_dot(hb, wqg[...]) * (GLA_DK ** -0.5)).astype(BF16)
    kg_o[...] = _dot(hb, wkg[...]).astype(BF16)
    vg_o[...] = _dot(hb, wvg[...]).astype(BF16)
    rg_o[...] = _dot(hb, wrg[...]).astype(BF16)
    z = _dot(_dot(hb, wf[...]).astype(BF16), wf2[...]) + bf[...]
    la_o[...] = (jnp.minimum(z, 0.0) - jnp.log1p(jnp.exp(-jnp.abs(z)))) * (1.0 / GLA_GATE_NORMALIZER)
    ga_o[...] = _sigmoid(_dot(hb, wga[...])).astype(BF16)
    gb_o[...] = _sigmoid(_dot(hb, wgb[...])).astype(BF16)


def _inproj(x2d, norm_g, cos_t, sin_t, w, tm):
    n = x2d.shape[0]
    ntab = cos_t.shape[0] // tm
    row = lambda i: (i, 0)
    const = lambda i: (0, 0)
    tab = lambda i: (i % ntab, 0)
    wnames = ("wqa", "wkva", "wqg", "wkg", "wvg", "wrg", "wga", "wgb", "wf", "wf2", "bf")
    wspecs = [pl.BlockSpec(w[k].shape, const) for k in wnames]
    widths = (ATTN_WIDTH, KV_WIDTH, KV_WIDTH, GLA_KEY_WIDTH, GLA_KEY_WIDTH, GLA_VALUE_WIDTH, GLA_VALUE_WIDTH,
              GLA_KEY_WIDTH, D_MODEL, D_MODEL)
    dtypes = (BF16, F32, F32, BF16, BF16, BF16, BF16, F32, BF16, BF16)
    return pl.pallas_call(
        _inproj_kernel,
        grid=(n // tm,),
        in_specs=[pl.BlockSpec((tm, D_MODEL), row), pl.BlockSpec((1, D_MODEL), const),
                  pl.BlockSpec((tm, LANES), tab), pl.BlockSpec((tm, LANES), tab)] + wspecs,
        out_specs=[pl.BlockSpec((tm, wd), row) for wd in widths],
        out_shape=[jax.ShapeDtypeStruct((n, wd), dt) for wd, dt in zip(widths, dtypes)],
        compiler_params=_cparams(("parallel",)),
        name="inproj",
    )(x2d, norm_g, cos_t, sin_t, *[w[k] for k in wnames])


def _softmax_sink_pv(s, sink, v_parts):
    m = sink
    for si in s:
        m = jnp.maximum(m, jnp.max(si, axis=-1, keepdims=True))
    l = jnp.exp(sink - m)
    acc = None
    for si, vi in zip(s, v_parts):
        p = jnp.exp(si - m)
        l = l + jnp.sum(p, axis=-1, keepdims=True)
        pv = _dot(p.astype(BF16), vi)
        acc = pv if acc is None else acc + pv
    return acc / l


def _swa_prompt_kernel(sink_ref, q_ref, kc_ref, kp_ref, vc_ref, vp_ref, o_ref):
    n = pl.program_id(1)
    k2 = jnp.concatenate([kp_ref[...], kc_ref[...]], axis=0).astype(BF16)
    v2 = jnp.concatenate([vp_ref[...], vc_ref[...]], axis=0).astype(BF16)
    t = lax.broadcasted_iota(jnp.int32, (WINDOW, 2 * WINDOW), 0)
    j = lax.broadcasted_iota(jnp.int32, (WINDOW, 2 * WINDOW), 1)
    valid = (j >= t) & (j <= t + WINDOW) & ((j >= WINDOW) | (n > 0))
    for kv in range(ATTN_KV_HEADS):
        kk = k2[:, kv * HEAD_DIM:(kv + 1) * HEAD_DIM]
        vv = v2[:, kv * HEAD_DIM:(kv + 1) * HEAD_DIM]
        for g in range(GROUP):
            h = kv * GROUP + g
            qh = q_ref[:, h * HEAD_DIM:(h + 1) * HEAD_DIM]
            s = jnp.where(valid, _dot_nt(qh, kk), -jnp.inf)
            o = _softmax_sink_pv([s], sink_ref[h], [vv])
            o_ref[:, h * HEAD_DIM:(h + 1) * HEAD_DIM] = o.astype(BF16)


def _swa_prompt(sinks, qa, ka, va, batch, seq):
    nb = seq // WINDOW
    cur = lambda b, n: (b * nb + n, 0)
    prev = lambda b, n: (b * nb + jnp.maximum(n - 1, 0), 0)
    return pl.pallas_call(
        _swa_prompt_kernel,
        grid=(batch, nb),
        in_specs=[pl.BlockSpec(memory_space=pltpu.SMEM),
                  pl.BlockSpec((WINDOW, ATTN_WIDTH), cur),
                  pl.BlockSpec((WINDOW, KV_WIDTH), cur), pl.BlockSpec((WINDOW, KV_WIDTH), prev),
                  pl.BlockSpec((WINDOW, KV_WIDTH), cur), pl.BlockSpec((WINDOW, KV_WIDTH), prev)],
        out_specs=pl.BlockSpec((WINDOW, ATTN_WIDTH), cur),
        out_shape=jax.ShapeDtypeStruct(qa.shape, BF16),
        compiler_params=_cparams(("parallel", "parallel")),
        name="swa_prompt",
    )(sinks, qa, ka, ka, va, va)


def _swa_sample_kernel(sink_ref, q_ref, kn_ref, vn_ref, kn3_ref, vn3_ref, ck_ref, cv_ref, o_ref, nk_ref, nv_ref,
                       *, t_new):
    sb = ck_ref.shape[0]
    rows = sb * t_new
    ck = ck_ref[...].reshape(sb * WINDOW, KV_WIDTH).astype(BF16)
    cv = cv_ref[...].reshape(sb * WINDOW, KV_WIDTH).astype(BF16)
    kn = kn_ref[...].astype(BF16)
    vn = vn_ref[...].astype(BF16)
    qi = lax.broadcasted_iota(jnp.int32, (rows, sb * WINDOW), 0)
    ci = lax.broadcasted_iota(jnp.int32, (rows, sb * WINDOW), 1)
    valid_c = (qi // t_new == ci // WINDOW) & (ci % WINDOW >= qi % t_new)
    qn = lax.broadcasted_iota(jnp.int32, (rows, rows), 0)
    cn = lax.broadcasted_iota(jnp.int32, (rows, rows), 1)
    valid_n = (qn // t_new == cn // t_new) & (cn <= qn)
    for kv in range(ATTN_KV_HEADS):
        cs = slice(kv * HEAD_DIM, (kv + 1) * HEAD_DIM)
        for g in range(GROUP):
            h = kv * GROUP + g
            qh = q_ref[:, h * HEAD_DIM:(h + 1) * HEAD_DIM]
            s_c = jnp.where(valid_c, _dot_nt(qh, ck[:, cs]), -jnp.inf)
            s_n = jnp.where(valid_n, _dot_nt(qh, kn[:, cs]), -jnp.inf)
            o = _softmax_sink_pv([s_c, s_n], sink_ref[h], [cv[:, cs], vn[:, cs]])
            o_ref[:, h * HEAD_DIM:(h + 1) * HEAD_DIM] = o.astype(BF16)
    nk_ref[:, 0:WINDOW - t_new, :] = ck_ref[:, t_new:WINDOW, :]
    nk_ref[:, WINDOW - t_new:WINDOW, :] = kn3_ref[...]
    nv_ref[:, 0:WINDOW - t_new, :] = cv_ref[:, t_new:WINDOW, :]
    nv_ref[:, WINDOW - t_new:WINDOW, :] = vn3_ref[...]


def _swa_sample(sinks, qa, ka, va, cache_k, cache_v, batch, t_new):
    sb = SAMPLE_SEQ_BLOCK
    rows = sb * t_new
    r2 = lambda i: (i, 0)
    r3 = lambda i: (i, 0, 0)
    ka3 = ka.reshape(batch, t_new, KV_WIDTH)
    va3 = va.reshape(batch, t_new, KV_WIDTH)
    return pl.pallas_call(
        functools.partial(_swa_sample_kernel, t_new=t_new),
        grid=(batch // sb,),
        in_specs=[pl.BlockSpec(memory_space=pltpu.SMEM),
                  pl.BlockSpec((rows, ATTN_WIDTH), r2),
                  pl.BlockSpec((rows, KV_WIDTH), r2), pl.BlockSpec((rows, KV_WIDTH), r2),
                  pl.BlockSpec((sb, t_new, KV_WIDTH), r3), pl.BlockSpec((sb, t_new, KV_WIDTH), r3),
                  pl.BlockSpec((sb, WINDOW, KV_WIDTH), r3), pl.BlockSpec((sb, WINDOW, KV_WIDTH), r3)],
        out_specs=[pl.BlockSpec((rows, ATTN_WIDTH), r2),
                   pl.BlockSpec((sb, WINDOW, KV_WIDTH), r3), pl.BlockSpec((sb, WINDOW, KV_WIDTH), r3)],
        out_shape=[jax.ShapeDtypeStruct(qa.shape, BF16),
                   jax.ShapeDtypeStruct(cache_k.shape, F32), jax.ShapeDtypeStruct(cache_v.shape, F32)],
        compiler_params=_cparams(("parallel",)),
        name="swa_sample",
    )(sinks, qa, ka, va, ka3, va3, cache_k, cache_v)


def _gla_constants(c, seg):
    t = np.arange(c)
    sid = t // seg
    same = sid[:, None] == sid[None, :]
    levels = []
    m = seg // 2
    while m >= 1:
        levels.append(m)
        m //= 2
    mats, roles, masks = [], [], []
    for m in levels:
        blk = t // (2 * m)
        second = (t // m) % 2 == 1
        p = blk * 2 * m + m - 1
        u = t[None, :]
        mq = (u > p[:, None]) & (u <= t[:, None])
        mk = (u > t[:, None]) & (u <= p[:, None])
        mats.append(np.where(second[:, None], mq, mk))
        roles.append(np.broadcast_to(second[:, None], (c, LANES)))
        masks.append((blk[:, None] == blk[None, :]) & second[:, None] & ~second[None, :])
    masks.append(np.eye(c, dtype=bool))
    mats.append(same & (t[None, :] <= t[:, None]))
    mats.append(same & (t[None, :] > t[:, None]))
    mall = np.concatenate(mats, 0).astype(np.float32)
    mall = jnp.asarray(np.concatenate([mall, mall], 1), BF16)
    role = jnp.asarray(np.concatenate(roles, 0).astype(np.float32))
    mask = jnp.asarray(np.concatenate(masks, 0).astype(np.float32))
    return len(levels), mall, role, mask


def _gla_exponents(la, mall):
    hl = jnp.concatenate(_split_bf16(la * LOG2_E), axis=0)
    return _dot(mall, hl), hl


def _gla_scores(qb, kb, e2, role_ref, mask_ref, nlev, c):
    qf = qb.astype(F32)
    kf = kb.astype(F32)
    terms = []
    for lv in range(nlev):
        sl = slice(lv * c, (lv + 1) * c)
        x = (jnp.where(role_ref[sl, :] > 0.5, qf, kf) * jnp.exp2(e2[sl])).astype(BF16)
        terms.append((x, x, lv))
    terms.append((qb, kb, nlev))
    mask = lambda i: mask_ref[i * c:(i + 1) * c, :]
    a = None
    pair = c % LANES == 0
    while terms:
        if pair and len(terms) >= 2:
            (l0, r0, i0), (l1, r1, i1) = terms.pop(), terms.pop()
            z = jnp.zeros_like(r0)
            rhs = jnp.concatenate([jnp.concatenate([r0, z], axis=1), jnp.concatenate([z, r1], axis=1)], axis=0)
            g = _dot_nt(jnp.concatenate([l0, l1], axis=1), rhs)
            t = mask(i0) * g[:, :c] + mask(i1) * g[:, c:]
        else:
            l0, r0, i0 = terms.pop()
            t = mask(i0) * _dot_nt(l0, r0)
        a = t if a is None else a + t
    return a, qf, kf


def _gla_out(o, g, r):
    r = r.astype(F32)
    return (_rms(o, g) * (r * _sigmoid(r))).astype(BF16)


def _gla_prompt_kernel(q_ref, k_ref, v_ref, la_ref, r_ref, mall_ref, role_ref, mask_ref, g_ref, o_ref, s_ref,
                       s_scr, *, nlev, nchunks):
    c = GLA_CHUNK
    hp = GLA_HEADS_PER_STEP
    s_scr[...] = jnp.zeros_like(s_scr)
    ones = jnp.ones((2 * c, LANES), BF16)

    def chunk(i, carry):
        rows = pl.ds(pl.multiple_of(i * c, c), c)
        e2_all, hl_all = _gla_exponents(la_ref[rows, :], mall_ref[...])
        for h in range(hp):
            ks = slice(h * GLA_DK, (h + 1) * GLA_DK)
            vs = slice(h * GLA_DV, (h + 1) * GLA_DV)
            v = v_ref[rows, vs]
            e2 = e2_all[:, ks]
            a, qf, kf = _gla_scores(q_ref[rows, ks], k_ref[rows, ks], e2, role_ref, mask_ref, nlev, c)
            b = e2[nlev * c:(nlev + 1) * c]
            rem = e2[(nlev + 1) * c:(nlev + 2) * c]
            s = s_scr[h]
            o = _dot(a.astype(BF16), v) + _dot((qf * jnp.exp2(b)).astype(BF16), s.astype(BF16))
            kt = (kf * jnp.exp2(rem)).astype(BF16)
            dec = jnp.exp2(_dot_tn(hl_all[:, ks], ones))
            s_scr[h] = jnp.concatenate([dec] * (GLA_DV // LANES), axis=1) * s + _dot_tn(kt, v)
            o_ref[rows, vs] = _gla_out(o, g_ref[...], r_ref[rows, vs])
        return carry

    lax.fori_loop(0, nchunks, chunk, 0)
    s_ref[0] = s_scr[...]


def _gla_prompt(qg, kg, vg, la, rg, gnorm, batch, seq):
    nlev, mall, role, mask = _gla_constants(GLA_CHUNK, GLA_CHUNK)
    hp = GLA_HEADS_PER_STEP
    bh = lambda b, h: (b, h)
    const = lambda b, h: (0, 0)
    return pl.pallas_call(
        functools.partial(_gla_prompt_kernel, nlev=nlev, nchunks=seq // GLA_CHUNK),
        grid=(batch, GLA_HEADS // hp),
        in_specs=[pl.BlockSpec((seq, hp * GLA_DK), bh), pl.BlockSpec((seq, hp * GLA_DK), bh),
                  pl.BlockSpec((seq, hp * GLA_DV), bh), pl.BlockSpec((seq, hp * GLA_DK), bh),
                  pl.BlockSpec((seq, hp * GLA_DV), bh),
                  pl.BlockSpec(mall.shape, const), pl.BlockSpec(role.shape, const), pl.BlockSpec(mask.shape, const),
                  pl.BlockSpec((1, GLA_DV), const)],
        out_specs=[pl.BlockSpec((seq, hp * GLA_DV), bh),
                   pl.BlockSpec((1, hp, GLA_DK, GLA_DV), lambda b, h: (b, h, 0, 0))],
        out_shape=[jax.ShapeDtypeStruct(vg.shape, BF16),
                   jax.ShapeDtypeStruct((batch, GLA_HEADS, GLA_DK, GLA_DV), F32)],
        scratch_shapes=[pltpu.VMEM((hp, GLA_DK, GLA_DV), F32)],
        compiler_params=_cparams(("parallel", "parallel")),
        name="gla_prompt",
    )(qg, kg, vg, la, rg, mall, role, mask, gnorm)


def _gla_sample_kernel(q_ref, k_ref, v_ref, la_ref, r_ref, s0_ref, mall_ref, role_ref, mask_ref, msum_ref, g_ref,
                       o_ref, s_ref, *, nlev, t_new):
    sb = s0_ref.shape[0]
    c = sb * t_new
    v = v_ref[...]
    e2, hl = _gla_exponents(la_ref[...], mall_ref[...])
    a, qf, kf = _gla_scores(q_ref[...], k_ref[...], e2, role_ref, mask_ref, nlev, c)
    qe = (qf * jnp.exp2(e2[nlev * c:(nlev + 1) * c])).astype(BF16)
    kt_t = jnp.transpose(kf * jnp.exp2(e2[(nlev + 1) * c:(nlev + 2) * c])).astype(BF16)
    bl_t = jnp.transpose(_dot(msum_ref[...], hl[:c]) + _dot(msum_ref[...], hl[c:]))
    dec_t = jnp.exp2(bl_t)
    seq_of_row = lax.broadcasted_iota(jnp.int32, (c, GLA_DV), 0) // t_new
    o = _dot(a.astype(BF16), v)
    for j in range(sb):
        s0 = s0_ref[j, 0]
        mine = seq_of_row == j
        o = o + jnp.where(mine, _dot(qe, s0.astype(BF16)), 0.0)
        vj = jnp.where(mine, v, jnp.zeros_like(v))
        s_ref[j, 0] = dec_t[:, j:j + 1] * s0 + _dot(kt_t, vj)
    o_ref[...] = _gla_out(o, g_ref[...], r_ref[...])


def _gla_sample(qg, kg, vg, la, rg, state, gnorm, batch, t_new):
    sb = SAMPLE_SEQ_BLOCK
    c = sb * t_new
    nlev, mall, role, mask = _gla_constants(c, t_new)
    msum = jnp.asarray((np.arange(c)[None, :] // t_new == np.arange(sb)[:, None]).astype(np.float32), BF16)
    rh = lambda i, h: (i, h)
    const = lambda i, h: (0, 0)
    st = lambda i, h: (i, h, 0, 0)
    return pl.pallas_call(
        functools.partial(_gla_sample_kernel, nlev=nlev, t_new=t_new),
        grid=(batch // sb, GLA_HEADS),
        in_specs=[pl.BlockSpec((c, GLA_DK), rh), pl.BlockSpec((c, GLA_DK), rh), pl.BlockSpec((c, GLA_DV), rh),
                  pl.BlockSpec((c, GLA_DK), rh), pl.BlockSpec((c, GLA_DV), rh),
                  pl.BlockSpec((sb, 1, GLA_DK, GLA_DV), st),
                  pl.BlockSpec(mall.shape, const), pl.BlockSpec(role.shape, const), pl.BlockSpec(mask.shape, const),
                  pl.BlockSpec(msum.shape, const), pl.BlockSpec((1, GLA_DV), const)],
        out_specs=[pl.BlockSpec((c, GLA_DV), rh), pl.BlockSpec((sb, 1, GLA_DK, GLA_DV), st)],
        out_shape=[jax.ShapeDtypeStruct(vg.shape, BF16), jax.ShapeDtypeStruct(state.shape, F32)],
        compiler_params=_cparams(("parallel", "parallel")),
        name="gla_sample",
    )(qg, kg, vg, la, rg, state, mall, role, mask, msum, gnorm)


def _post_kernel(x_ref, a_ref, gl_ref, ga_ref, gb_ref, wpa, wpg, wo, nf_ref, wr_hi, wr_lo, br, x1e_o, ids_o):
    pa = _dot(a_ref[...], wpa[...])
    pg = _dot(gl_ref[...], wpg[...])
    merged = ga_ref[...].astype(F32) * pa + gb_ref[...].astype(F32) * pg
    x1 = x_ref[...] + _dot(merged.astype(BF16), wo[...])
    x1e_o[:, :D_MODEL] = x1
    h2 = _rms(x1, nf_ref[...])
    h_hi, h_lo = _split_bf16(h2)
    lt = _dot_nt(wr_hi[...], h_hi) + _dot_nt(wr_hi[...], h_lo) + _dot_nt(wr_lo[...], h_hi)
    nrow = ROUTER_ROWS
    lt = lt[:nrow] + br[:nrow, 0:1]
    row = lax.broadcasted_iota(jnp.int32, lt.shape, 0)
    big = jnp.int32(LANES)
    ninf = -jnp.inf

    def first_max(vals):
        mx = jnp.max(vals, axis=0, keepdims=True)
        return mx, jnp.min(jnp.where(vals == mx, row, big), axis=0, keepdims=True)

    gl = jnp.where((row >= N_EXPERTS) & (row < N_EXPERTS + N_GROUPS), lt, ninf)
    gmax, gidx = first_max(gl)
    p_sel = 1.0 / jnp.sum(jnp.exp(gl - gmax), axis=0, keepdims=True)
    lo = (gidx - N_EXPERTS) * EXPERTS_PER_GROUP
    el = jnp.where((row >= lo) & (row < lo + EXPERTS_PER_GROUP), lt, ninf)
    v1, i1 = first_max(el)
    el2 = jnp.where(row == i1, ninf, el)
    v2, i2 = first_max(el2)
    t = jnp.exp(v2 - v1)
    w1 = p_sel / (1.0 + t)
    w2 = p_sel * t / (1.0 + t)
    comb_t = jnp.where(row == i1, w1, 0.0) + jnp.where(row == i2, w2, 0.0)
    comb_t = jnp.concatenate([comb_t, jnp.zeros((LANES - nrow, comb_t.shape[1]), F32)], axis=0)
    x1e_o[:, D_MODEL:] = jnp.transpose(comb_t)
    row8 = lax.broadcasted_iota(jnp.int32, ids_o.shape, 0)
    ids_o[...] = jnp.where(row8 == 0, i1, jnp.where(row8 == 1, i2, 0))


def _post(x2d, a_out, g_out, ga, gb, w, tm):
    n = x2d.shape[0]
    row = lambda i: (i, 0)
    const = lambda i: (0, 0)
    wnames = ("wpa", "wpg", "wo", "nf", "wr_hi", "wr_lo", "br")
    ids_rows = 8
    return pl.pallas_call(
        _post_kernel,
        grid=(n // tm,),
        in_specs=[pl.BlockSpec((tm, D_MODEL), row), pl.BlockSpec((tm, ATTN_WIDTH), row),
                  pl.BlockSpec((tm, GLA_VALUE_WIDTH), row), pl.BlockSpec((tm, D_MODEL), row),
                  pl.BlockSpec((tm, D_MODEL), row)] + [pl.BlockSpec(w[k].shape, const) for k in wnames],
        out_specs=[pl.BlockSpec((tm, ROW_EXT), row), pl.BlockSpec((ids_rows, tm), lambda i: (0, i))],
        out_shape=[jax.ShapeDtypeStruct((n, ROW_EXT), F32), jax.ShapeDtypeStruct((ids_rows, n), jnp.int32)],
        compiler_params=_cparams(("parallel",)),
        name="post_mixer",
    )(x2d, a_out, g_out, ga, gb, *[w[k] for k in wnames])


def _moe_plan(ids, tme):
    n = ids.shape[1]
    ntiles = n // tme
    max_items = ntiles + N_GROUPS - 1
    i1, i2 = ids[0], ids[1]
    grp = i1 // EXPERTS_PER_GROUP
    lo = jnp.minimum(i1, i2) % EXPERTS_PER_GROUP
    hi = jnp.maximum(i1, i2) % EXPERTS_PER_GROUP
    key = (grp * EXPERTS_PER_GROUP + lo) * EXPERTS_PER_GROUP + hi
    order = jnp.argsort(key, stable=True).astype(jnp.int32)
    skey = key[order].reshape(ntiles, tme)
    sg = skey // (EXPERTS_PER_GROUP * EXPERTS_PER_GROUP)
    slo = (skey // EXPERTS_PER_GROUP) % EXPERTS_PER_GROUP
    shi = skey % EXPERTS_PER_GROUP
    ev = jnp.arange(EXPERTS_PER_GROUP)
    in_g = sg[:, :, None] == jnp.arange(N_GROUPS)
    uses_e = (slo[:, :, None] == ev) | (shi[:, :, None] == ev)
    flags_tge = jnp.any(in_g[:, :, :, None] & uses_e[:, :, None, :], axis=1)
    present = jnp.any(in_g, axis=1).reshape(-1)
    pos = jnp.cumsum(present) - 1
    n_items = pos[-1] + 1
    src = jnp.zeros((max_items,), jnp.int32).at[jnp.where(present, pos, max_items)].set(
        jnp.arange(ntiles * N_GROUPS, dtype=jnp.int32), mode="drop")
    it = jnp.arange(max_items)
    valid = it < n_items
    last_src = src[n_items - 1]
    src = jnp.where(valid, src, last_src)
    item_tile = src // N_GROUPS
    item_group = src % N_GROUPS
    prev_tile = jnp.concatenate([jnp.full((1,), -1, jnp.int32), item_tile[:-1]])
    next_tile = jnp.concatenate([item_tile[1:], jnp.full((1,), -1, jnp.int32)])
    first = valid & (item_tile != prev_tile)
    last = valid & ((item_tile != next_tile) | (it == n_items - 1))
    flags = flags_tge[item_tile, item_group] & valid[:, None]
    return (order, item_tile.astype(jnp.int32), item_group.astype(jnp.int32), first.astype(jnp.int32),
            last.astype(jnp.int32), flags.reshape(-1).astype(jnp.int32))


def _moe_kernel(order, itile, igroup, ifirst, ilast, flags, x_hbm, wg, wu, wd, nffn, nfin, y_hbm,
                xbuf, ybuf, acc, hbuf, gsem, ssem, *, tme, ntiles):
    i = pl.program_id(0)
    t = itile[i]
    slot = t % 2

    def gather_row(tile, sl, r):
        tok = order[tile * tme + r]
        return pltpu.make_async_copy(x_hbm.at[pl.ds(tok, 1)], xbuf.at[sl, pl.ds(r, 1)], gsem.at[sl])

    def scatter_row(tile, sl, r):
        tok = order[tile * tme + r]
        return pltpu.make_async_copy(ybuf.at[sl, pl.ds(r, 1)], y_hbm.at[pl.ds(tok, 1)], ssem.at[sl])

    def start_rows(make, tile, sl):
        def body(r, c):
            make(tile, sl, r).start()
            return c
        lax.fori_loop(0, tme, body, 0, unroll=8)

    def wait_gather(sl):
        pltpu.make_async_copy(x_hbm.at[pl.ds(0, tme)], xbuf.at[sl], gsem.at[sl]).wait()

    def wait_scatter(sl):
        pltpu.make_async_copy(ybuf.at[sl], y_hbm.at[pl.ds(0, tme)], ssem.at[sl]).wait()

    @pl.when(i == 0)
    def _():
        start_rows(gather_row, 0, 0)

    @pl.when(ifirst[i] == 1)
    def _():
        @pl.when(t + 1 < ntiles)
        def _():
            start_rows(gather_row, t + 1, 1 - slot)

        wait_gather(slot)
        x1 = xbuf[slot, :, :D_MODEL]
        acc[...] = x1
        hbuf[...] = _rms(x1, nffn[...]).astype(BF16)

    g = igroup[i]
    for e in range(EXPERTS_PER_GROUP):
        @pl.when(flags[i * EXPERTS_PER_GROUP + e] == 1)
        def _():
            comb = xbuf[slot, :, D_MODEL:]
            lane = lax.broadcasted_iota(jnp.int32, comb.shape, 1)
            ce = jnp.sum(jnp.where(lane == g * EXPERTS_PER_GROUP + e, comb, 0.0), axis=-1, keepdims=True)
            h = hbuf[...]
            a = _dot(h, wg[e])
            u = _dot(h, wu[e])
            act = (a * _sigmoid(a)) * u * ce
            acc[...] += _dot(act.astype(BF16), wd[e])

    @pl.when(ilast[i] == 1)
    def _():
        @pl.when(t >= 2)
        def _():
            wait_scatter(slot)

        ybuf[slot] = _rms(acc[...], nfin[...])
        start_rows(scatter_row, t, slot)

    @pl.when(i == pl.num_programs(0) - 1)
    def _():
        for sl in range(min(2, ntiles)):
            wait_scatter(sl)


def _moe(x1e, ids, weg, weu, wed, nffn, nfin):
    n = x1e.shape[0]
    tme = min(MOE_TILE, n)
    assert n % tme == 0
    ntiles = n // tme
    plan = _moe_plan(ids, tme)
    max_items = ntiles + N_GROUPS - 1
    grp = lambda i, order, itile, igroup, *_: (igroup[i], 0, 0)
    const = lambda i, *_: (0, 0)
    grid_spec = pltpu.PrefetchScalarGridSpec(
        num_scalar_prefetch=len(plan),
        grid=(max_items,),
        in_specs=[pl.BlockSpec(memory_space=pl.ANY),
                  pl.BlockSpec((EXPERTS_PER_GROUP, D_MODEL, EXPERT_FF), grp),
                  pl.BlockSpec((EXPERTS_PER_GROUP, D_MODEL, EXPERT_FF), grp),
                  pl.BlockSpec((EXPERTS_PER_GROUP, EXPERT_FF, D_MODEL), grp),
                  pl.BlockSpec((1, D_MODEL), const), pl.BlockSpec((1, D_MODEL), const)],
        out_specs=pl.BlockSpec(memory_space=pl.ANY),
        scratch_shapes=[pltpu.VMEM((2, tme, ROW_EXT), F32), pltpu.VMEM((2, tme, D_MODEL), F32),
                        pltpu.VMEM((tme, D_MODEL), F32), pltpu.VMEM((tme, D_MODEL), BF16),
                        pltpu.SemaphoreType.DMA((2,)), pltpu.SemaphoreType.DMA((2,))],
    )
    return pl.pallas_call(
        functools.partial(_moe_kernel, tme=tme, ntiles=ntiles),
        grid_spec=grid_spec,
        out_shape=jax.ShapeDtypeStruct((n, D_MODEL), F32),
        compiler_params=_cparams(("arbitrary",)),
        name="moe",
    )(*plan, x1e, weg, weu, wed, nffn, nfin)


def _rope_tables(positions):
    half = HEAD_DIM // 2
    inv_freq = ROPE_THETA ** (-jnp.arange(half, dtype=F32) / half)
    ang = positions.astype(F32)[:, None] * inv_freq[None, :]
    cos, sin = jnp.cos(ang), jnp.sin(ang)
    reps = LANES // HEAD_DIM
    return (jnp.tile(jnp.concatenate([cos, cos], -1), (1, reps)),
            jnp.tile(jnp.concatenate([-sin, sin], -1), (1, reps)))


def _prep_weights(norm_mix, w_in, w_gla_f2, b_gla_f, gla_norm, w_proj_attn, w_proj_gla, w_out, norm_ffn,
                  w_router_group, b_router_group, w_router_expert, b_router_expert):
    widths = (ATTN_WIDTH, 2 * KV_WIDTH, GLA_KEY_WIDTH, GLA_KEY_WIDTH, GLA_VALUE_WIDTH, GLA_VALUE_WIDTH,
              GLA_GATE_RANK, D_MODEL, D_MODEL)
    names = ("wqa", "wkva", "wqg", "wkg", "wvg", "wrg", "wf", "wga", "wgb")
    w, off = {}, 0
    for nm, wd in zip(names, widths):
        w[nm] = w_in[:, off:off + wd].astype(BF16)
        off += wd
    w["wf"] = jnp.pad(w["wf"], ((0, 0), (0, LANES - GLA_GATE_RANK)))
    w["wf2"] = jnp.pad(w_gla_f2.astype(BF16), ((0, LANES - GLA_GATE_RANK), (0, 0)))
    w["bf"] = b_gla_f.reshape(1, -1)
    w["norm_mix"] = norm_mix.reshape(1, -1)
    w["gla_norm"] = gla_norm.reshape(1, -1)
    w["wpa"] = w_proj_attn.astype(BF16)
    w["wpg"] = w_proj_gla.astype(BF16)
    w["wo"] = w_out.astype(BF16)
    w["nf"] = norm_ffn.reshape(1, -1)
    pad = LANES - N_EXPERTS - N_GROUPS
    wr_t = jnp.pad(jnp.concatenate([w_router_expert, w_router_group], axis=1), ((0, 0), (0, pad))).T
    w["wr_hi"], w["wr_lo"] = _split_bf16(wr_t)
    w["br"] = jnp.broadcast_to(jnp.pad(jnp.concatenate([b_router_expert, b_router_group]), (0, pad))[:, None],
                               (LANES, LANES))
    return w


def _layer(x, positions_tab, cache, w, sinks, weg, weu, wed, nfin, tm):
    batch, seq, _ = x.shape
    n = batch * seq
    x2d = x.reshape(n, D_MODEL)
    cos_t, sin_t = positions_tab
    qa, ka, va, qg, kg, vg, rg, la, ga, gb = _inproj(x2d, w["norm_mix"], cos_t, sin_t, w, tm)
    if cache is None:
        a_out = _swa_prompt(sinks, qa, ka, va, batch, seq)
        new_k = ka.reshape(batch, seq, ATTN_KV_HEADS, HEAD_DIM)[:, -WINDOW:]
        new_v = va.reshape(batch, seq, ATTN_KV_HEADS, HEAD_DIM)[:, -WINDOW:]
        g_out, new_s = _gla_prompt(qg, kg, vg, la, rg, w["gla_norm"], batch, seq)
    else:
        cache_k, cache_v, state = cache
        a_out, new_k, new_v = _swa_sample(sinks, qa, ka, va, cache_k.reshape(batch, WINDOW, KV_WIDTH),
                                          cache_v.reshape(batch, WINDOW, KV_WIDTH), batch, seq)
        new_k = new_k.reshape(batch, WINDOW, ATTN_KV_HEADS, HEAD_DIM)
        new_v = new_v.reshape(batch, WINDOW, ATTN_KV_HEADS, HEAD_DIM)
        g_out, new_s = _gla_sample(qg, kg, vg, la, rg, state, w["gla_norm"], batch, seq)
    x1e, ids = _post(x2d, a_out, g_out, ga, gb, w, tm)
    y = _moe(x1e, ids, weg, weu, wed, w["nf"], nfin)
    return y.reshape(batch, seq, D_MODEL), new_k, new_v, new_s


def kernel(x_prompt, x_sample, cache_win_k, cache_win_v, state_gla, norm_mix, w_in, w_gla_f2, b_gla_f, gla_norm,
           attn_sinks, w_proj_attn, w_proj_gla, w_out, norm_ffn, w_router_group, b_router_group, w_router_expert,
           b_router_expert, w_exp_gate, w_exp_up, w_exp_down, norm_final):
    assert norm_mix.shape[0] == 1, "single-layer step"
    seq_p = x_prompt.shape[1]
    dec_b, dec_t = x_sample.shape[0], x_sample.shape[1]
    w = _prep_weights(norm_mix[0], w_in[0], w_gla_f2[0], b_gla_f[0], gla_norm[0], w_proj_attn[0], w_proj_gla[0],
                      w_out[0], norm_ffn[0], w_router_group[0], b_router_group[0], w_router_expert[0],
                      b_router_expert[0])
    weg = w_exp_gate[0].astype(BF16)
    weu = w_exp_up[0].astype(BF16)
    wed = w_exp_down[0].astype(BF16)
    nfin = norm_final.reshape(1, -1)
    sinks = attn_sinks[0]
    tab_p = _rope_tables(jnp.arange(seq_p, dtype=jnp.int32))
    pos_s = PAST_LEN + jnp.arange(dec_t, dtype=jnp.int32)
    tab_s = tuple(jnp.tile(t, (dec_b, 1)) for t in _rope_tables(pos_s))
    tm_p = min(512, seq_p)
    tm_s = dec_b * dec_t
    yp, pk, pv, ps = _layer(x_prompt, tab_p, None, w, sinks, weg, weu, wed, nfin, tm_p)
    ys, sk, sv, ss = _layer(x_sample, tab_s, (cache_win_k[0], cache_win_v[0], state_gla[0]), w, sinks, weg, weu, wed,
                            nfin, tm_s)
    return (yp, ys, pk[None], pv[None], ps[None], sk[None], sv[None], ss[None])
```

```python
import functools

import numpy as np
import jax
import jax.numpy as jnp
from jax import lax
from jax.experimental import pallas as pl
from jax.experimental.pallas import tpu as pltpu

F32 = jnp.float32
BF16 = jnp.bfloat16

D_MODEL = 1024
ATTN_HEADS = 8
ATTN_KV_HEADS = 2
GROUP = ATTN_HEADS // ATTN_KV_HEADS
HEAD_DIM = 64
ATTN_WIDTH = ATTN_HEADS * HEAD_DIM
KV_WIDTH = ATTN_KV_HEADS * HEAD_DIM
WINDOW = 128
ROPE_THETA = 10000.0
PAST_LEN = 8192
GLA_HEADS = 4
GLA_KEY_WIDTH = D_MODEL // 2
GLA_VALUE_WIDTH = D_MODEL
GLA_DK = GLA_KEY_WIDTH // GLA_HEADS
GLA_DV = GLA_VALUE_WIDTH // GLA_HEADS
GLA_GATE_RANK = 16
GLA_GATE_NORMALIZER = 16.0
N_GROUPS = 4
EXPERTS_PER_GROUP = 8
N_EXPERTS = N_GROUPS * EXPERTS_PER_GROUP
EXPERT_FF = 256
EPS = 1e-6
LOG2_E = 1.4426950408889634

LANES = 128
GLA_CHUNK = 128
GLA_HEADS_PER_STEP = 4
SAMPLE_SEQ_BLOCK = 8
MOE_TILE = 256
TOKEN_TILE = (D_MODEL // LANES, LANES)
ROUTER_ROWS = 40
VMEM_LIMIT = 56 * 1024 * 1024


def _cparams(sem):
    return pltpu.CompilerParams(dimension_semantics=sem, vmem_limit_bytes=VMEM_LIMIT)


def _rms(x, g):
    return x * lax.rsqrt(jnp.mean(x * x, axis=-1, keepdims=True) + EPS) * g


def _sigmoid(x):
    return 1.0 / (1.0 + jnp.exp(-x))


def _dot(a, b):
    return jnp.dot(a, b, preferred_element_type=F32)


def _dot_nt(a, b):
    return lax.dot_general(a, b, (((1,), (1,)), ((), ())), preferred_element_type=F32)


def _dot_tn(a, b):
    return lax.dot_general(a, b, (((0,), (0,)), ((), ())), preferred_element_type=F32)


def _split_bf16(x):
    hi = x.astype(BF16)
    lo = (x - hi.astype(F32)).astype(BF16)
    return hi, lo


def _inproj_kernel(x_ref, g_ref, cos_ref, sin_ref, wqa, wkva, wqg, wkg, wvg, wrg, wga, wgb, wf, wf2, bf,
                   qa_o, ka_o, va_o, qg_o, kg_o, vg_o, rg_o, la_o, ga_o, gb_o):
    hb = _rms(x_ref[...], g_ref[...]).astype(BF16)
    cos = cos_ref[...]
    sin = sin_ref[...]
    lane = lax.broadcasted_iota(jnp.int32, cos.shape, 1)
    first_half = (lane % HEAD_DIM) < (HEAD_DIM // 2)

    def rope(t):
        swapped = jnp.where(first_half, pltpu.roll(t, LANES - HEAD_DIM // 2, 1), pltpu.roll(t, HEAD_DIM // 2, 1))
        return t * cos + swapped * sin

    qa = _dot(hb, wqa[...])
    for c in range(ATTN_WIDTH // LANES):
        sl = slice(c * LANES, (c + 1) * LANES)
        qa_o[:, sl] = (rope(qa[:, sl]) * (HEAD_DIM ** -0.5)).astype(BF16)
    kva = _dot(hb, wkva[...])
    ka_o[...] = rope(kva[:, :KV_WIDTH])
    va_o[...] = kva[:, KV_WIDTH:]
    qg_o[...] = (_dot(hb, wqg[...]) * (GLA_DK ** -0.5)).astype(BF16)
    kg_o[...] = _dot(hb, wkg[...]).astype(BF16)
    vg_o[...] = _dot(hb, wvg[...]).astype(BF16)
    rg_o[...] = _dot(hb, wrg[...]).astype(BF16)
    z = _dot(_dot(hb, wf[...]).astype(BF16), wf2[...]) + bf[...]
    la_o[...] = (jnp.minimum(z, 0.0) - jnp.log1p(jnp.exp(-jnp.abs(z)))) * (1.0 / GLA_GATE_NORMALIZER)
    ga_o[...] = _sigmoid(_dot(hb, wga[...])).astype(BF16)
    gb_o[...] = _sigmoid(_dot(hb, wgb[...])).astype(BF16)


def _inproj(x2d, norm_g, cos_t, sin_t, w, tm):
    n = x2d.shape[0]
    ntab = cos_t.shape[0] // tm
    row = lambda i: (i, 0)
    const = lambda i: (0, 0)
    tab = lambda i: (i % ntab, 0)
    wnames = ("wqa", "wkva", "wqg", "wkg", "wvg", "wrg", "wga", "wgb", "wf", "wf2", "bf")
    wspecs = [pl.BlockSpec(w[k].shape, const) for k in wnames]
    widths = (ATTN_WIDTH, KV_WIDTH, KV_WIDTH, GLA_KEY_WIDTH, GLA_KEY_WIDTH, GLA_VALUE_WIDTH, GLA_VALUE_WIDTH,
              GLA_KEY_WIDTH, D_MODEL, D_MODEL)
    dtypes = (BF16, F32, F32, BF16, BF16, BF16, BF16, F32, BF16, BF16)
    return pl.pallas_call(
        _inproj_kernel,
        grid=(n // tm,),
        in_specs=[pl.BlockSpec((tm, D_MODEL), row), pl.BlockSpec((1, D_MODEL), const),
                  pl.BlockSpec((tm, LANES), tab), pl.BlockSpec((tm, LANES), tab)] + wspecs,
        out_specs=[pl.BlockSpec((tm, wd), row) for wd in widths],
        out_shape=[jax.ShapeDtypeStruct((n, wd), dt) for wd, dt in zip(widths, dtypes)],
        compiler_params=_cparams(("parallel",)),
        name="inproj",
    )(x2d, norm_g, cos_t, sin_t, *[w[k] for k in wnames])


def _softmax_sink_pv(s, sink, v_parts):
    m = sink
    for si in s:
        m = jnp.maximum(m, jnp.max(si, axis=-1, keepdims=True))
    l = jnp.exp(sink - m)
    acc = None
    for si, vi in zip(s, v_parts):
        p = jnp.exp(si - m)
        l = l + jnp.sum(p, axis=-1, keepdims=True)
        pv = _dot(p.astype(BF16), vi)
        acc = pv if acc is None else acc + pv
    return acc / l


def _swa_prompt_kernel(sink_ref, q_ref, kc_ref, kp_ref, vc_ref, vp_ref, o_ref):
    n = pl.program_id(1)
    k2 = jnp.concatenate([kp_ref[...], kc_ref[...]], axis=0).astype(BF16)
    v2 = jnp.concatenate([vp_ref[...], vc_ref[...]], axis=0).astype(BF16)
    t = lax.broadcasted_iota(jnp.int32, (WINDOW, 2 * WINDOW), 0)
    j = lax.broadcasted_iota(jnp.int32, (WINDOW, 2 * WINDOW), 1)
    valid = (j >= t) & (j <= t + WINDOW) & ((j >= WINDOW) | (n > 0))
    zeros = jnp.zeros((2 * WINDOW, HEAD_DIM), BF16)
    ones = jnp.ones((2 * WINDOW, HEAD_DIM), BF16)
    lane = lax.broadcasted_iota(jnp.int32, (WINDOW, LANES), 1)
    for kv in range(ATTN_KV_HEADS):
        kk = k2[:, kv * HEAD_DIM:(kv + 1) * HEAD_DIM]
        vv = v2[:, kv * HEAD_DIM:(kv + 1) * HEAD_DIM]
        vext = (jnp.concatenate([vv, zeros, ones, zeros], axis=1), jnp.concatenate([zeros, vv, zeros, ones], axis=1))
        for pr in range(GROUP // 2):
            acc, sink_terms = None, []
            for par in range(2):
                h = kv * GROUP + pr * 2 + par
                qh = q_ref[:, h * HEAD_DIM:(h + 1) * HEAD_DIM]
                s = jnp.where(valid, _dot_nt(qh, kk), -jnp.inf)
                m = jnp.maximum(jnp.max(s, axis=-1, keepdims=True), sink_ref[h])
                r = _dot(jnp.exp(s - m).astype(BF16), vext[par])
                acc = r if acc is None else acc + r
                sink_terms.append(jnp.exp(sink_ref[h] - m))
            l = acc[:, LANES:] + jnp.where(lane < HEAD_DIM, sink_terms[0], sink_terms[1])
            c0 = (kv * GROUP + pr * 2) * HEAD_DIM
            o_ref[:, c0:c0 + LANES] = (acc[:, :LANES] / l).astype(BF16)


def _swa_prompt(sinks, qa, ka, va, batch, seq):
    nb = seq // WINDOW
    cur = lambda b, n: (b * nb + n, 0)
    prev = lambda b, n: (b * nb + jnp.maximum(n - 1, 0), 0)
    return pl.pallas_call(
        _swa_prompt_kernel,
        grid=(batch, nb),
        in_specs=[pl.BlockSpec(memory_space=pltpu.SMEM),
                  pl.BlockSpec((WINDOW, ATTN_WIDTH), cur),
                  pl.BlockSpec((WINDOW, KV_WIDTH), cur), pl.BlockSpec((WINDOW, KV_WIDTH), prev),
                  pl.BlockSpec((WINDOW, KV_WIDTH), cur), pl.BlockSpec((WINDOW, KV_WIDTH), prev)],
        out_specs=pl.BlockSpec((WINDOW, ATTN_WIDTH), cur),
        out_shape=jax.ShapeDtypeStruct(qa.shape, BF16),
        compiler_params=_cparams(("parallel", "parallel")),
        name="swa_prompt",
    )(sinks, qa, ka, ka, va, va)


def _swa_sample_kernel(sink_ref, q_ref, kn_ref, vn_ref, kn3_ref, vn3_ref, ck_ref, cv_ref, o_ref, nk_ref, nv_ref,
                       *, t_new):
    sb = ck_ref.shape[0]
    rows = sb * t_new
    ck = ck_ref[...].reshape(sb * WINDOW, KV_WIDTH).astype(BF16)
    cv = cv_ref[...].reshape(sb * WINDOW, KV_WIDTH).astype(BF16)
    kn = kn_ref[...].astype(BF16)
    vn = vn_ref[...].astype(BF16)
    qi = lax.broadcasted_iota(jnp.int32, (rows, sb * WINDOW), 0)
    ci = lax.broadcasted_iota(jnp.int32, (rows, sb * WINDOW), 1)
    valid_c = (qi // t_new == ci // WINDOW) & (ci % WINDOW >= qi % t_new)
    qn = lax.broadcasted_iota(jnp.int32, (rows, rows), 0)
    cn = lax.broadcasted_iota(jnp.int32, (rows, rows), 1)
    valid_n = (qn // t_new == cn // t_new) & (cn <= qn)
    for kv in range(ATTN_KV_HEADS):
        cs = slice(kv * HEAD_DIM, (kv + 1) * HEAD_DIM)
        for g in range(GROUP):
            h = kv * GROUP + g
            qh = q_ref[:, h * HEAD_DIM:(h + 1) * HEAD_DIM]
            s_c = jnp.where(valid_c, _dot_nt(qh, ck[:, cs]), -jnp.inf)
            s_n = jnp.where(valid_n, _dot_nt(qh, kn[:, cs]), -jnp.inf)
            o = _softmax_sink_pv([s_c, s_n], sink_ref[h], [cv[:, cs], vn[:, cs]])
            o_ref[:, h * HEAD_DIM:(h + 1) * HEAD_DIM] = o.astype(BF16)
    nk_ref[:, 0:WINDOW - t_new, :] = ck_ref[:, t_new:WINDOW, :]
    nk_ref[:, WINDOW - t_new:WINDOW, :] = kn3_ref[...]
    nv_ref[:, 0:WINDOW - t_new, :] = cv_ref[:, t_new:WINDOW, :]
    nv_ref[:, WINDOW - t_new:WINDOW, :] = vn3_ref[...]


def _swa_sample(sinks, qa, ka, va, cache_k, cache_v, batch, t_new):
    sb = SAMPLE_SEQ_BLOCK
    rows = sb * t_new
    r2 = lambda i: (i, 0)
    r3 = lambda i: (i, 0, 0)
    ka3 = ka.reshape(batch, t_new, KV_WIDTH)
    va3 = va.reshape(batch, t_new, KV_WIDTH)
    return pl.pallas_call(
        functools.partial(_swa_sample_kernel, t_new=t_new),
        grid=(batch // sb,),
        in_specs=[pl.BlockSpec(memory_space=pltpu.SMEM),
                  pl.BlockSpec((rows, ATTN_WIDTH), r2),
                  pl.BlockSpec((rows, KV_WIDTH), r2), pl.BlockSpec((rows, KV_WIDTH), r2),
                  pl.BlockSpec((sb, t_new, KV_WIDTH), r3), pl.BlockSpec((sb, t_new, KV_WIDTH), r3),
                  pl.BlockSpec((sb, WINDOW, KV_WIDTH), r3), pl.BlockSpec((sb, WINDOW, KV_WIDTH), r3)],
        out_specs=[pl.BlockSpec((rows, ATTN_WIDTH), r2),
                   pl.BlockSpec((sb, WINDOW, KV_WIDTH), r3), pl.BlockSpec((sb, WINDOW, KV_WIDTH), r3)],
        out_shape=[jax.ShapeDtypeStruct(qa.shape, BF16),
                   jax.ShapeDtypeStruct(cache_k.shape, F32), jax.ShapeDtypeStruct(cache_v.shape, F32)],
        compiler_params=_cparams(("parallel",)),
        name="swa_sample",
    )(sinks, qa, ka, va, ka3, va3, cache_k, cache_v)


def _gla_constants(c, seg):
    t = np.arange(c)
    sid = t // seg
    same = sid[:, None] == sid[None, :]
    levels = []
    m = seg // 2
    while m >= 1:
        levels.append(m)
        m //= 2
    mats, roles, masks = [], [], []
    for m in levels:
        blk = t // (2 * m)
        second = (t // m) % 2 == 1
        p = blk * 2 * m + m - 1
        u = t[None, :]
        mq = (u > p[:, None]) & (u <= t[:, None])
        mk = (u > t[:, None]) & (u <= p[:, None])
        mats.append(np.where(second[:, None], mq, mk))
        roles.append(np.broadcast_to(second[:, None], (c, LANES)))
        masks.append((blk[:, None] == blk[None, :]) & second[:, None] & ~second[None, :])
    masks.append(np.eye(c, dtype=bool))
    mats.append(same & (t[None, :] <= t[:, None]))
    mats.append(same & (t[None, :] > t[:, None]))
    mall = np.concatenate(mats, 0).astype(np.float32)
    mall = jnp.asarray(np.concatenate([mall, mall], 1), BF16)
    role = jnp.asarray(np.concatenate(roles, 0).astype(np.float32))
    mask = jnp.asarray(np.concatenate(masks, 0).astype(np.float32))
    return len(levels), mall, role, mask


def _gla_exponents(la, mall):
    hl = jnp.concatenate(_split_bf16(la * LOG2_E), axis=0)
    return _dot(mall, hl), hl


def _gla_scores(qb, kb, e2, role_ref, mask_ref, nlev, c):
    qf = qb.astype(F32)
    kf = kb.astype(F32)
    terms = []
    for lv in range(nlev):
        sl = slice(lv * c, (lv + 1) * c)
        x = (jnp.where(role_ref[sl, :] > 0.5, qf, kf) * jnp.exp2(e2[sl])).astype(BF16)
        terms.append((x, x, lv))
    terms.append((qb, kb, nlev))
    mask = lambda i: mask_ref[i * c:(i + 1) * c, :]
    a = None
    pair = c % LANES == 0
    while terms:
        if pair and len(terms) >= 2:
            (l0, r0, i0), (l1, r1, i1) = terms.pop(), terms.pop()
            z = jnp.zeros_like(r0)
            rhs = jnp.concatenate([jnp.concatenate([r0, z], axis=1), jnp.concatenate([z, r1], axis=1)], axis=0)
            g = _dot_nt(jnp.concatenate([l0, l1], axis=1), rhs)
            t = mask(i0) * g[:, :c] + mask(i1) * g[:, c:]
        else:
            l0, r0, i0 = terms.pop()
            t = mask(i0) * _dot_nt(l0, r0)
        a = t if a is None else a + t
    return a, qf, kf


def _gla_out(o, g, r):
    r = r.astype(F32)
    return (_rms(o, g) * (r * _sigmoid(r))).astype(BF16)


def _gla_prompt_kernel(q_ref, k_ref, v_ref, la_ref, r_ref, mall_ref, role_ref, mask_ref, g_ref, o_ref, s_ref,
                       s_scr, *, nlev, nchunks):
    c = GLA_CHUNK
    hp = GLA_HEADS_PER_STEP
    s_scr[...] = jnp.zeros_like(s_scr)
    ones = jnp.ones((2 * c, LANES), BF16)

    def chunk(i, carry):
        rows = pl.ds(pl.multiple_of(i * c, c), c)
        e2_all, hl_all = _gla_exponents(la_ref[rows, :], mall_ref[...])
        for h in range(hp):
            ks = slice(h * GLA_DK, (h + 1) * GLA_DK)
            vs = slice(h * GLA_DV, (h + 1) * GLA_DV)
            v = v_ref[rows, vs]
            e2 = e2_all[:, ks]
            a, qf, kf = _gla_scores(q_ref[rows, ks], k_ref[rows, ks], e2, role_ref, mask_ref, nlev, c)
            b = e2[nlev * c:(nlev + 1) * c]
            rem = e2[(nlev + 1) * c:(nlev + 2) * c]
            s = s_scr[h]
            o = _dot(a.astype(BF16), v) + _dot((qf * jnp.exp2(b)).astype(BF16), s.astype(BF16))
            kt = (kf * jnp.exp2(rem)).astype(BF16)
            dec = jnp.exp2(_dot_tn(hl_all[:, ks], ones))
            s_scr[h] = jnp.concatenate([dec] * (GLA_DV // LANES), axis=1) * s + _dot_tn(kt, v)
            o_ref[rows, vs] = _gla_out(o, g_ref[...], r_ref[rows, vs])
        return carry

    lax.fori_loop(0, nchunks, chunk, 0)
    s_ref[0] = s_scr[...]


def _gla_prompt(qg, kg, vg, la, rg, gnorm, batch, seq):
    nlev, mall, role, mask = _gla_constants(GLA_CHUNK, GLA_CHUNK)
    hp = GLA_HEADS_PER_STEP
    bh = lambda b, h: (b, h)
    const = lambda b, h: (0, 0)
    return pl.pallas_call(
        functools.partial(_gla_prompt_kernel, nlev=nlev, nchunks=seq // GLA_CHUNK),
        grid=(batch, GLA_HEADS // hp),
        in_specs=[pl.BlockSpec((seq, hp * GLA_DK), bh), pl.BlockSpec((seq, hp * GLA_DK), bh),
                  pl.BlockSpec((seq, hp * GLA_DV), bh), pl.BlockSpec((seq, hp * GLA_DK), bh),
                  pl.BlockSpec((seq, hp * GLA_DV), bh),
                  pl.BlockSpec(mall.shape, const), pl.BlockSpec(role.shape, const), pl.BlockSpec(mask.shape, const),
                  pl.BlockSpec((1, GLA_DV), const)],
        out_specs=[pl.BlockSpec((seq, hp * GLA_DV), bh),
                   pl.BlockSpec((1, hp, GLA_DK, GLA_DV), lambda b, h: (b, h, 0, 0))],
        out_shape=[jax.ShapeDtypeStruct(vg.shape, BF16),
                   jax.ShapeDtypeStruct((batch, GLA_HEADS, GLA_DK, GLA_DV), F32)],
        scratch_shapes=[pltpu.VMEM((hp, GLA_DK, GLA_DV), F32)],
        compiler_params=_cparams(("parallel", "parallel")),
        name="gla_prompt",
    )(qg, kg, vg, la, rg, mall, role, mask, gnorm)


def _gla_sample_kernel(q_ref, k_ref, v_ref, la_ref, r_ref, s0_ref, mall_ref, role_ref, mask_ref, msum_ref, g_ref,
                       o_ref, s_ref, *, nlev, t_new):
    sb = s0_ref.shape[0]
    c = sb * t_new
    v = v_ref[...]
    e2, hl = _gla_exponents(la_ref[...], mall_ref[...])
    a, qf, kf = _gla_scores(q_ref[...], k_ref[...], e2, role_ref, mask_ref, nlev, c)
    qe = (qf * jnp.exp2(e2[nlev * c:(nlev + 1) * c])).astype(BF16)
    kt_t = jnp.transpose(kf * jnp.exp2(e2[(nlev + 1) * c:(nlev + 2) * c])).astype(BF16)
    bl_t = jnp.transpose(_dot(msum_ref[...], hl[:c]) + _dot(msum_ref[...], hl[c:]))
    dec_t = jnp.exp2(bl_t)
    seq_of_row = lax.broadcasted_iota(jnp.int32, (c, GLA_DV), 0) // t_new
    o = _dot(a.astype(BF16), v)
    for j in range(sb):
        s0 = s0_ref[j, 0]
        mine = seq_of_row == j
        o = o + jnp.where(mine, _dot(qe, s0.astype(BF16)), 0.0)
        vj = jnp.where(mine, v, jnp.zeros_like(v))
        s_ref[j, 0] = dec_t[:, j:j + 1] * s0 + _dot(kt_t, vj)
    o_ref[...] = _gla_out(o, g_ref[...], r_ref[...])


def _gla_sample(qg, kg, vg, la, rg, state, gnorm, batch, t_new):
    sb = SAMPLE_SEQ_BLOCK
    c = sb * t_new
    nlev, mall, role, mask = _gla_constants(c, t_new)
    msum = jnp.asarray((np.arange(c)[None, :] // t_new == np.arange(sb)[:, None]).astype(np.float32), BF16)
    rh = lambda i, h: (i, h)
    const = lambda i, h: (0, 0)
    st = lambda i, h: (i, h, 0, 0)
    return pl.pallas_call(
        functools.partial(_gla_sample_kernel, nlev=nlev, t_new=t_new),
        grid=(batch // sb, GLA_HEADS),
        in_specs=[pl.BlockSpec((c, GLA_DK), rh), pl.BlockSpec((c, GLA_DK), rh), pl.BlockSpec((c, GLA_DV), rh),
                  pl.BlockSpec((c, GLA_DK), rh), pl.BlockSpec((c, GLA_DV), rh),
                  pl.BlockSpec((sb, 1, GLA_DK, GLA_DV), st),
                  pl.BlockSpec(mall.shape, const), pl.BlockSpec(role.shape, const), pl.BlockSpec(mask.shape, const),
                  pl.BlockSpec(msum.shape, const), pl.BlockSpec((1, GLA_DV), const)],
        out_specs=[pl.BlockSpec((c, GLA_DV), rh), pl.BlockSpec((sb, 1, GLA_DK, GLA_DV), st)],
        out_shape=[jax.ShapeDtypeStruct(vg.shape, BF16), jax.ShapeDtypeStruct(state.shape, F32)],
        compiler_params=_cparams(("parallel", "parallel")),
        name="gla_sample",
    )(qg, kg, vg, la, rg, state, mall, role, mask, msum, gnorm)


def _post_kernel(x_ref, a_ref, gl_ref, ga_ref, gb_ref, wpa, wpg, wo, nf_ref, wr_hi, wr_lo, br, x1t_o, rt_o):
    pa = _dot(a_ref[...], wpa[...])
    pg = _dot(gl_ref[...], wpg[...])
    merged = ga_ref[...].astype(F32) * pa + gb_ref[...].astype(F32) * pg
    x1 = x_ref[...] + _dot(merged.astype(BF16), wo[...])
    for j in range(D_MODEL // LANES):
        x1t_o[:, j, :] = x1[:, j * LANES:(j + 1) * LANES]
    h2 = _rms(x1, nf_ref[...])
    h_hi, h_lo = _split_bf16(h2)
    lt = _dot_nt(wr_hi[...], h_hi) + _dot_nt(wr_hi[...], h_lo) + _dot_nt(wr_lo[...], h_hi)
    nrow = ROUTER_ROWS
    lt = lt[:nrow] + br[:nrow, 0:1]
    row = lax.broadcasted_iota(jnp.int32, lt.shape, 0)
    big = jnp.int32(LANES)
    ninf = -jnp.inf

    def first_max(vals):
        mx = jnp.max(vals, axis=0, keepdims=True)
        return mx, jnp.min(jnp.where(vals == mx, row, big), axis=0, keepdims=True)

    gl = jnp.where((row >= N_EXPERTS) & (row < N_EXPERTS + N_GROUPS), lt, ninf)
    gmax, gidx = first_max(gl)
    p_sel = 1.0 / jnp.sum(jnp.exp(gl - gmax), axis=0, keepdims=True)
    lo = (gidx - N_EXPERTS) * EXPERTS_PER_GROUP
    el = jnp.where((row >= lo) & (row < lo + EXPERTS_PER_GROUP), lt, ninf)
    v1, i1 = first_max(el)
    el2 = jnp.where(row == i1, ninf, el)
    v2, i2 = first_max(el2)
    t = jnp.exp(v2 - v1)
    w1 = p_sel / (1.0 + t)
    w2 = p_sel * t / (1.0 + t)
    row8 = lax.broadcasted_iota(jnp.int32, rt_o.shape, 0)
    pick = lambda k, val, rest: jnp.where(row8 == k, val, rest)
    rt_o[...] = pick(0, i1.astype(F32), pick(1, i2.astype(F32), pick(2, w1, pick(3, w2, 0.0))))


def _post(x2d, a_out, g_out, ga, gb, w, tm):
    n = x2d.shape[0]
    row = lambda i: (i, 0)
    const = lambda i: (0, 0)
    wnames = ("wpa", "wpg", "wo", "nf", "wr_hi", "wr_lo", "br")
    rt_rows = 8
    return pl.pallas_call(
        _post_kernel,
        grid=(n // tm,),
        in_specs=[pl.BlockSpec((tm, D_MODEL), row), pl.BlockSpec((tm, ATTN_WIDTH), row),
                  pl.BlockSpec((tm, GLA_VALUE_WIDTH), row), pl.BlockSpec((tm, D_MODEL), row),
                  pl.BlockSpec((tm, D_MODEL), row)] + [pl.BlockSpec(w[k].shape, const) for k in wnames],
        out_specs=[pl.BlockSpec((tm, TOKEN_TILE[0], TOKEN_TILE[1]), lambda i: (i, 0, 0)),
                   pl.BlockSpec((rt_rows, tm), lambda i: (0, i))],
        out_shape=[jax.ShapeDtypeStruct((n,) + TOKEN_TILE, F32), jax.ShapeDtypeStruct((rt_rows, n), F32)],
        compiler_params=_cparams(("parallel",)),
        name="post_mixer",
    )(x2d, a_out, g_out, ga, gb, *[w[k] for k in wnames])


def _moe_plan(rt, tme):
    n = rt.shape[1]
    ntiles = n // tme
    max_items = ntiles + N_GROUPS - 1
    i1, i2 = rt[0].astype(jnp.int32), rt[1].astype(jnp.int32)
    grp = i1 // EXPERTS_PER_GROUP
    lo = jnp.minimum(i1, i2) % EXPERTS_PER_GROUP
    hi = jnp.maximum(i1, i2) % EXPERTS_PER_GROUP
    key = (grp * EXPERTS_PER_GROUP + lo) * EXPERTS_PER_GROUP + hi
    order = jnp.argsort(key, stable=True).astype(jnp.int32)
    skey = key[order].reshape(ntiles, tme)
    sg = skey // (EXPERTS_PER_GROUP * EXPERTS_PER_GROUP)
    slo = (skey // EXPERTS_PER_GROUP) % EXPERTS_PER_GROUP
    shi = skey % EXPERTS_PER_GROUP
    ev = jnp.arange(EXPERTS_PER_GROUP)
    in_g = sg[:, :, None] == jnp.arange(N_GROUPS)
    uses_e = (slo[:, :, None] == ev) | (shi[:, :, None] == ev)
    flags_tge = jnp.any(in_g[:, :, :, None] & uses_e[:, :, None, :], axis=1)
    present = jnp.any(in_g, axis=1).reshape(-1)
    pos = jnp.cumsum(present) - 1
    n_items = pos[-1] + 1
    src = jnp.zeros((max_items,), jnp.int32).at[jnp.where(present, pos, max_items)].set(
        jnp.arange(ntiles * N_GROUPS, dtype=jnp.int32), mode="drop")
    it = jnp.arange(max_items)
    valid = it < n_items
    last_src = src[n_items - 1]
    src = jnp.where(valid, src, last_src)
    item_tile = src // N_GROUPS
    item_group = src % N_GROUPS
    prev_tile = jnp.concatenate([jnp.full((1,), -1, jnp.int32), item_tile[:-1]])
    next_tile = jnp.concatenate([item_tile[1:], jnp.full((1,), -1, jnp.int32)])
    first = valid & (item_tile != prev_tile)
    last = valid & ((item_tile != next_tile) | (it == n_items - 1))
    flags = flags_tge[item_tile, item_group] & valid[:, None]
    rt_sorted = rt[:, order].reshape(rt.shape[0], ntiles, tme).transpose(1, 0, 2)
    plan = (order, item_tile.astype(jnp.int32), item_group.astype(jnp.int32), first.astype(jnp.int32),
            last.astype(jnp.int32), flags.reshape(-1).astype(jnp.int32))
    return plan, rt_sorted


def _moe_kernel(order, itile, igroup, ifirst, ilast, flags, x_hbm, rt_ref, wg, wu, wd, nffn, nfin, y_hbm,
                xbuf, ybuf, acc, hbuf, rcol, gsem, ssem, *, tme, ntiles):
    i = pl.program_id(0)
    t = itile[i]
    slot = t % 2

    def gather_row(tile, sl, r):
        tok = order[tile * tme + r]
        return pltpu.make_async_copy(x_hbm.at[pl.ds(tok, 1)], xbuf.at[sl, pl.ds(r, 1)], gsem.at[sl])

    def scatter_row(tile, sl, r):
        tok = order[tile * tme + r]
        return pltpu.make_async_copy(ybuf.at[sl, pl.ds(r, 1)], y_hbm.at[pl.ds(tok, 1)], ssem.at[sl])

    def start_rows(make, tile, sl):
        def body(r, c):
            make(tile, sl, r).start()
            return c
        lax.fori_loop(0, tme, body, 0, unroll=8)

    def wait_gather(sl):
        pltpu.make_async_copy(x_hbm.at[pl.ds(0, tme)], xbuf.at[sl], gsem.at[sl]).wait()

    def wait_scatter(sl):
        pltpu.make_async_copy(ybuf.at[sl], y_hbm.at[pl.ds(0, tme)], ssem.at[sl]).wait()

    @pl.when(i == 0)
    def _():
        start_rows(gather_row, 0, 0)

    @pl.when(ifirst[i] == 1)
    def _():
        @pl.when(t + 1 < ntiles)
        def _():
            start_rows(gather_row, t + 1, 1 - slot)

        wait_gather(slot)
        x1 = jnp.concatenate([xbuf[slot, :, j, :] for j in range(TOKEN_TILE[0])], axis=1)
        acc[...] = x1
        hbuf[...] = _rms(x1, nffn[...]).astype(BF16)
        rt = rt_ref[0]
        rcol[...] = jnp.transpose(jnp.concatenate([rt, jnp.zeros((LANES - rt.shape[0], tme), F32)], axis=0))

    g = igroup[i]
    for e in range(EXPERTS_PER_GROUP):
        @pl.when(flags[i * EXPERTS_PER_GROUP + e] == 1)
        def _():
            eid = (g * EXPERTS_PER_GROUP + e).astype(F32)
            ce = (jnp.where(rcol[:, 0:1] == eid, rcol[:, 2:3], 0.0)
                  + jnp.where(rcol[:, 1:2] == eid, rcol[:, 3:4], 0.0))
            h = hbuf[...]
            a = _dot(h, wg[e])
            u = _dot(h, wu[e])
            act = (a * _sigmoid(a)) * u * ce
            acc[...] += _dot(act.astype(BF16), wd[e])

    @pl.when(ilast[i] == 1)
    def _():
        @pl.when(t >= 2)
        def _():
            wait_scatter(slot)

        ybuf[slot] = _rms(acc[...], nfin[...])
        start_rows(scatter_row, t, slot)

    @pl.when(i == pl.num_programs(0) - 1)
    def _():
        for sl in range(min(2, ntiles)):
            wait_scatter(sl)


def _moe(x1t, rt, weg, weu, wed, nffn, nfin):
    n = x1t.shape[0]
    tme = min(MOE_TILE, n)
    assert n % tme == 0
    ntiles = n // tme
    plan, rt_sorted = _moe_plan(rt, tme)
    max_items = ntiles + N_GROUPS - 1
    grp = lambda i, order, itile, igroup, *_: (igroup[i], 0, 0)
    til = lambda i, order, itile, *_: (itile[i], 0, 0)
    const = lambda i, *_: (0, 0)
    grid_spec = pltpu.PrefetchScalarGridSpec(
        num_scalar_prefetch=len(plan),
        grid=(max_items,),
        in_specs=[pl.BlockSpec(memory_space=pl.ANY),
                  pl.BlockSpec((1,) + rt_sorted.shape[1:], til),
                  pl.BlockSpec((EXPERTS_PER_GROUP, D_MODEL, EXPERT_FF), grp),
                  pl.BlockSpec((EXPERTS_PER_GROUP, D_MODEL, EXPERT_FF), grp),
                  pl.BlockSpec((EXPERTS_PER_GROUP, EXPERT_FF, D_MODEL), grp),
                  pl.BlockSpec((1, D_MODEL), const), pl.BlockSpec((1, D_MODEL), const)],
        out_specs=pl.BlockSpec(memory_space=pl.ANY),
        scratch_shapes=[pltpu.VMEM((2, tme) + TOKEN_TILE, F32), pltpu.VMEM((2, tme, D_MODEL), F32),
                        pltpu.VMEM((tme, D_MODEL), F32), pltpu.VMEM((tme, D_MODEL), BF16),
                        pltpu.VMEM((tme, LANES), F32),
                        pltpu.SemaphoreType.DMA((2,)), pltpu.SemaphoreType.DMA((2,))],
    )
    return pl.pallas_call(
        functools.partial(_moe_kernel, tme=tme, ntiles=ntiles),
        grid_spec=grid_spec,
        out_shape=jax.ShapeDtypeStruct((n, D_MODEL), F32),
        compiler_params=_cparams(("arbitrary",)),
        name="moe",
    )(*plan, x1t, rt_sorted, weg, weu, wed, nffn, nfin)


def _rope_tables(positions):
    half = HEAD_DIM // 2
    inv_freq = ROPE_THETA ** (-jnp.arange(half, dtype=F32) / half)
    ang = positions.astype(F32)[:, None] * inv_freq[None, :]
    cos, sin = jnp.cos(ang), jnp.sin(ang)
    reps = LANES // HEAD_DIM
    return (jnp.tile(jnp.concatenate([cos, cos], -1), (1, reps)),
            jnp.tile(jnp.concatenate([-sin, sin], -1), (1, reps)))


def _prep_weights(norm_mix, w_in, w_gla_f2, b_gla_f, gla_norm, w_proj_attn, w_proj_gla, w_out, norm_ffn,
                  w_router_group, b_router_group, w_router_expert, b_router_expert):
    widths = (ATTN_WIDTH, 2 * KV_WIDTH, GLA_KEY_WIDTH, GLA_KEY_WIDTH, GLA_VALUE_WIDTH, GLA_VALUE_WIDTH,
              GLA_GATE_RANK, D_MODEL, D_MODEL)
    names = ("wqa", "wkva", "wqg", "wkg", "wvg", "wrg", "wf", "wga", "wgb")
    w, off = {}, 0
    for nm, wd in zip(names, widths):
        w[nm] = w_in[:, off:off + wd].astype(BF16)
        off += wd
    w["wf"] = jnp.pad(w["wf"], ((0, 0), (0, LANES - GLA_GATE_RANK)))
    w["wf2"] = jnp.pad(w_gla_f2.astype(BF16), ((0, LANES - GLA_GATE_RANK), (0, 0)))
    w["bf"] = b_gla_f.reshape(1, -1)
    w["norm_mix"] = norm_mix.reshape(1, -1)
    w["gla_norm"] = gla_norm.reshape(1, -1)
    w["wpa"] = w_proj_attn.astype(BF16)
    w["wpg"] = w_proj_gla.astype(BF16)
    w["wo"] = w_out.astype(BF16)
    w["nf"] = norm_ffn.reshape(1, -1)
    pad = LANES - N_EXPERTS - N_GROUPS
    wr_t = jnp.pad(jnp.concatenate([w_router_expert, w_router_group], axis=1), ((0, 0), (0, pad))).T
    w["wr_hi"], w["wr_lo"] = _split_bf16(wr_t)
    w["br"] = jnp.broadcast_to(jnp.pad(jnp.concatenate([b_router_expert, b_router_group]), (0, pad))[:, None],
                               (LANES, LANES))
    return w


def _layer(x, positions_tab, cache, w, sinks, weg, weu, wed, nfin, tm):
    batch, seq, _ = x.shape
    n = batch * seq
    x2d = x.reshape(n, D_MODEL)
    cos_t, sin_t = positions_tab
    qa, ka, va, qg, kg, vg, rg, la, ga, gb = _inproj(x2d, w["norm_mix"], cos_t, sin_t, w, tm)
    if cache is None:
        a_out = _swa_prompt(sinks, qa, ka, va, batch, seq)
        last = lambda z: z.reshape(batch, seq, KV_WIDTH)[:, seq - WINDOW:].reshape(batch, WINDOW, ATTN_KV_HEADS, HEAD_DIM)
        new_k, new_v = last(ka), last(va)
        g_out, new_s = _gla_prompt(qg, kg, vg, la, rg, w["gla_norm"], batch, seq)
    else:
        cache_k, cache_v, state = cache
        a_out, new_k, new_v = _swa_sample(sinks, qa, ka, va, cache_k.reshape(batch, WINDOW, KV_WIDTH),
                                          cache_v.reshape(batch, WINDOW, KV_WIDTH), batch, seq)
        new_k = new_k.reshape(batch, WINDOW, ATTN_KV_HEADS, HEAD_DIM)
        new_v = new_v.reshape(batch, WINDOW, ATTN_KV_HEADS, HEAD_DIM)
        g_out, new_s = _gla_sample(qg, kg, vg, la, rg, state, w["gla_norm"], batch, seq)
    x1t, rt = _post(x2d, a_out, g_out, ga, gb, w, tm)
    y = _moe(x1t, rt, weg, weu, wed, w["nf"], nfin)
    return y.reshape(batch, seq, D_MODEL), new_k, new_v, new_s


def kernel(x_prompt, x_sample, cache_win_k, cache_win_v, state_gla, norm_mix, w_in, w_gla_f2, b_gla_f, gla_norm,
           attn_sinks, w_proj_attn, w_proj_gla, w_out, norm_ffn, w_router_group, b_router_group, w_router_expert,
           b_router_expert, w_exp_gate, w_exp_up, w_exp_down, norm_final):
    assert norm_mix.shape[0] == 1, "single-layer step"
    seq_p = x_prompt.shape[1]
    dec_b, dec_t = x_sample.shape[0], x_sample.shape[1]
    w = _prep_weights(norm_mix[0], w_in[0], w_gla_f2[0], b_gla_f[0], gla_norm[0], w_proj_attn[0], w_proj_gla[0],
                      w_out[0], norm_ffn[0], w_router_group[0], b_router_group[0], w_router_expert[0],
                      b_router_expert[0])
    weg = w_exp_gate[0].astype(BF16)
    weu = w_exp_up[0].astype(BF16)
    wed = w_exp_down[0].astype(BF16)
    nfin = norm_final.reshape(1, -1)
    sinks = attn_sinks[0]
    tab_p = _rope_tables(jnp.arange(seq_p, dtype=jnp.int32))
    pos_s = PAST_LEN + jnp.arange(dec_t, dtype=jnp.int32)
    tab_s = tuple(jnp.tile(t, (dec_b, 1)) for t in _rope_tables(pos_s))
    tm_p = min(512, seq_p)
    tm_s = dec_b * dec_t
    yp, pk, pv, ps = _layer(x_prompt, tab_p, None, w, sinks, weg, weu, wed, nfin, tm_p)
    ys, sk, sv, ss = _layer(x_sample, tab_s, (cache_win_k[0], cache_win_v[0], state_gla[0]), w, sinks, weg, weu, wed,
                            nfin, tm_s)
    return (yp, ys, pk[None], pv[None], ps[None], sk[None], sv[None], ss[None])
```

```python
import functools

import numpy as np
import jax
import jax.numpy as jnp
from jax import lax
from jax.experimental import pallas as pl
from jax.experimental.pallas import tpu as pltpu

F32 = jnp.float32
BF16 = jnp.bfloat16

D_MODEL = 1024
ATTN_HEADS = 8
ATTN_KV_HEADS = 2
GROUP = ATTN_HEADS // ATTN_KV_HEADS
HEAD_DIM = 64
ATTN_WIDTH = ATTN_HEADS * HEAD_DIM
KV_WIDTH = ATTN_KV_HEADS * HEAD_DIM
WINDOW = 128
ROPE_THETA = 10000.0
PAST_LEN = 8192
GLA_HEADS = 4
GLA_KEY_WIDTH = D_MODEL // 2
GLA_VALUE_WIDTH = D_MODEL
GLA_DK = GLA_KEY_WIDTH // GLA_HEADS
GLA_DV = GLA_VALUE_WIDTH // GLA_HEADS
GLA_GATE_RANK = 16
GLA_GATE_NORMALIZER = 16.0
N_GROUPS = 4
EXPERTS_PER_GROUP = 8
N_EXPERTS = N_GROUPS * EXPERTS_PER_GROUP
EXPERT_FF = 256
EPS = 1e-6
LOG2_E = 1.4426950408889634

LANES = 128
GLA_CHUNK = 128
GLA_HEADS_PER_STEP = 4
SAMPLE_SEQ_BLOCK = 8
MOE_TILE = 256
TOKEN_ROWS = D_MODEL // LANES
ROUTER_ROWS = 40
VMEM_LIMIT = 56 * 1024 * 1024


def _cparams(sem):
    return pltpu.CompilerParams(dimension_semantics=sem, vmem_limit_bytes=VMEM_LIMIT)


def _rms(x, g):
    return x * lax.rsqrt(jnp.mean(x * x, axis=-1, keepdims=True) + EPS) * g


def _sigmoid(x):
    return 1.0 / (1.0 + jnp.exp(-x))


def _dot(a, b):
    return jnp.dot(a, b, preferred_element_type=F32)


def _dot_nt(a, b):
    return lax.dot_general(a, b, (((1,), (1,)), ((), ())), preferred_element_type=F32)


def _dot_tn(a, b):
    return lax.dot_general(a, b, (((0,), (0,)), ((), ())), preferred_element_type=F32)


def _split_bf16(x):
    hi = x.astype(BF16)
    lo = (x - hi.astype(F32)).astype(BF16)
    return hi, lo


def _inproj_kernel(x_ref, g_ref, cos_ref, sin_ref, wqa, wkva, wqg, wkg, wvg, wrg, wga, wgb, wf, wf2, bf,
                   qa_o, ka_o, va_o, qg_o, kg_o, vg_o, rg_o, la_o, ga_o, gb_o):
    hb = _rms(x_ref[...], g_ref[...]).astype(BF16)
    cos = cos_ref[...]
    sin = sin_ref[...]
    lane = lax.broadcasted_iota(jnp.int32, cos.shape, 1)
    first_half = (lane % HEAD_DIM) < (HEAD_DIM // 2)

    def rope(t):
        swapped = jnp.where(first_half, pltpu.roll(t, LANES - HEAD_DIM // 2, 1), pltpu.roll(t, HEAD_DIM // 2, 1))
        return t * cos + swapped * sin

    qa = _dot(hb, wqa[...])
    for c in range(ATTN_WIDTH // LANES):
        sl = slice(c * LANES, (c + 1) * LANES)
        qa_o[:, sl] = (rope(qa[:, sl]) * (HEAD_DIM ** -0.5)).astype(BF16)
    kva = _dot(hb, wkva[...])
    ka_o[...] = rope(kva[:, :KV_WIDTH])
    va_o[...] = kva[:, KV_WIDTH:]
    qg_o[...] = (_dot(hb, wqg[...]) * (GLA_DK ** -0.5)).astype(BF16)
    kg_o[...] = _dot(hb, wkg[...]).astype(BF16)
    vg_o[...] = _dot(hb, wvg[...]).astype(BF16)
    rg_o[...] = _dot(hb, wrg[...]).astype(BF16)
    z = _dot(_dot(hb, wf[...]).astype(BF16), wf2[...]) + bf[...]
    la_o[...] = (jnp.minimum(z, 0.0) - jnp.log1p(jnp.exp(-jnp.abs(z)))) * (1.0 / GLA_GATE_NORMALIZER)
    ga_o[...] = _sigmoid(_dot(hb, wga[...])).astype(BF16)
    gb_o[...] = _sigmoid(_dot(hb, wgb[...])).astype(BF16)


def _inproj(x2d, norm_g, cos_t, sin_t, w, tm):
    n = x2d.shape[0]
    ntab = cos_t.shape[0] // tm
    row = lambda i: (i, 0)
    const = lambda i: (0, 0)
    tab = lambda i: (i % ntab, 0)
    wnames = ("wqa", "wkva", "wqg", "wkg", "wvg", "wrg", "wga", "wgb", "wf", "wf2", "bf")
    wspecs = [pl.BlockSpec(w[k].shape, const) for k in wnames]
    widths = (ATTN_WIDTH, KV_WIDTH, KV_WIDTH, GLA_KEY_WIDTH, GLA_KEY_WIDTH, GLA_VALUE_WIDTH, GLA_VALUE_WIDTH,
              GLA_KEY_WIDTH, D_MODEL, D_MODEL)
    dtypes = (BF16, F32, F32, BF16, BF16, BF16, BF16, F32, BF16, BF16)
    return pl.pallas_call(
        _inproj_kernel,
        grid=(n // tm,),
        in_specs=[pl.BlockSpec((tm, D_MODEL), row), pl.BlockSpec((1, D_MODEL), const),
                  pl.BlockSpec((tm, LANES), tab), pl.BlockSpec((tm, LANES), tab)] + wspecs,
        out_specs=[pl.BlockSpec((tm, wd), row) for wd in widths],
        out_shape=[jax.ShapeDtypeStruct((n, wd), dt) for wd, dt in zip(widths, dtypes)],
        compiler_params=_cparams(("parallel",)),
        name="inproj",
    )(x2d, norm_g, cos_t, sin_t, *[w[k] for k in wnames])


def _softmax_sink_pv(s, sink, v_parts):
    m = sink
    for si in s:
        m = jnp.maximum(m, jnp.max(si, axis=-1, keepdims=True))
    l = jnp.exp(sink - m)
    acc = None
    for si, vi in zip(s, v_parts):
        p = jnp.exp(si - m)
        l = l + jnp.sum(p, axis=-1, keepdims=True)
        pv = _dot(p.astype(BF16), vi)
        acc = pv if acc is None else acc + pv
    return acc / l


def _swa_prompt_kernel(sink_ref, q_ref, kc_ref, kp_ref, vc_ref, vp_ref, o_ref):
    n = pl.program_id(1)
    k2 = jnp.concatenate([kp_ref[...], kc_ref[...]], axis=0).astype(BF16)
    v2 = jnp.concatenate([vp_ref[...], vc_ref[...]], axis=0).astype(BF16)
    t = lax.broadcasted_iota(jnp.int32, (WINDOW, 2 * WINDOW), 0)
    j = lax.broadcasted_iota(jnp.int32, (WINDOW, 2 * WINDOW), 1)
    valid = (j >= t) & (j <= t + WINDOW) & ((j >= WINDOW) | (n > 0))
    zeros = jnp.zeros((2 * WINDOW, HEAD_DIM), BF16)
    ones = jnp.ones((2 * WINDOW, HEAD_DIM), BF16)
    lane = lax.broadcasted_iota(jnp.int32, (WINDOW, LANES), 1)
    for kv in range(ATTN_KV_HEADS):
        kk = k2[:, kv * HEAD_DIM:(kv + 1) * HEAD_DIM]
        vv = v2[:, kv * HEAD_DIM:(kv + 1) * HEAD_DIM]
        vext = (jnp.concatenate([vv, zeros, ones, zeros], axis=1), jnp.concatenate([zeros, vv, zeros, ones], axis=1))
        for pr in range(GROUP // 2):
            acc, sink_terms = None, []
            for par in range(2):
                h = kv * GROUP + pr * 2 + par
                qh = q_ref[:, h * HEAD_DIM:(h + 1) * HEAD_DIM]
                s = jnp.where(valid, _dot_nt(qh, kk), -jnp.inf)
                m = jnp.maximum(jnp.max(s, axis=-1, keepdims=True), sink_ref[h])
                r = _dot(jnp.exp(s - m).astype(BF16), vext[par])
                acc = r if acc is None else acc + r
                sink_terms.append(jnp.exp(sink_ref[h] - m))
            l = acc[:, LANES:] + jnp.where(lane < HEAD_DIM, sink_terms[0], sink_terms[1])
            c0 = (kv * GROUP + pr * 2) * HEAD_DIM
            o_ref[:, c0:c0 + LANES] = (acc[:, :LANES] / l).astype(BF16)


def _swa_prompt(sinks, qa, ka, va, batch, seq):
    nb = seq // WINDOW
    cur = lambda b, n: (b * nb + n, 0)
    prev = lambda b, n: (b * nb + jnp.maximum(n - 1, 0), 0)
    return pl.pallas_call(
        _swa_prompt_kernel,
        grid=(batch, nb),
        in_specs=[pl.BlockSpec(memory_space=pltpu.SMEM),
                  pl.BlockSpec((WINDOW, ATTN_WIDTH), cur),
                  pl.BlockSpec((WINDOW, KV_WIDTH), cur), pl.BlockSpec((WINDOW, KV_WIDTH), prev),
                  pl.BlockSpec((WINDOW, KV_WIDTH), cur), pl.BlockSpec((WINDOW, KV_WIDTH), prev)],
        out_specs=pl.BlockSpec((WINDOW, ATTN_WIDTH), cur),
        out_shape=jax.ShapeDtypeStruct(qa.shape, BF16),
        compiler_params=_cparams(("parallel", "parallel")),
        name="swa_prompt",
    )(sinks, qa, ka, ka, va, va)


def _swa_sample_kernel(sink_ref, q_ref, kn_ref, vn_ref, kn3_ref, vn3_ref, ck_ref, cv_ref, o_ref, nk_ref, nv_ref,
                       *, t_new):
    sb = ck_ref.shape[0]
    rows = sb * t_new
    ck = ck_ref[...].reshape(sb * WINDOW, KV_WIDTH).astype(BF16)
    cv = cv_ref[...].reshape(sb * WINDOW, KV_WIDTH).astype(BF16)
    kn = kn_ref[...].astype(BF16)
    vn = vn_ref[...].astype(BF16)
    qi = lax.broadcasted_iota(jnp.int32, (rows, sb * WINDOW), 0)
    ci = lax.broadcasted_iota(jnp.int32, (rows, sb * WINDOW), 1)
    valid_c = (qi // t_new == ci // WINDOW) & (ci % WINDOW >= qi % t_new)
    qn = lax.broadcasted_iota(jnp.int32, (rows, rows), 0)
    cn = lax.broadcasted_iota(jnp.int32, (rows, rows), 1)
    valid_n = (qn // t_new == cn // t_new) & (cn <= qn)
    for kv in range(ATTN_KV_HEADS):
        cs = slice(kv * HEAD_DIM, (kv + 1) * HEAD_DIM)
        for g in range(GROUP):
            h = kv * GROUP + g
            qh = q_ref[:, h * HEAD_DIM:(h + 1) * HEAD_DIM]
            s_c = jnp.where(valid_c, _dot_nt(qh, ck[:, cs]), -jnp.inf)
            s_n = jnp.where(valid_n, _dot_nt(qh, kn[:, cs]), -jnp.inf)
            o = _softmax_sink_pv([s_c, s_n], sink_ref[h], [cv[:, cs], vn[:, cs]])
            o_ref[:, h * HEAD_DIM:(h + 1) * HEAD_DIM] = o.astype(BF16)
    nk_ref[:, 0:WINDOW - t_new, :] = ck_ref[:, t_new:WINDOW, :]
    nk_ref[:, WINDOW - t_new:WINDOW, :] = kn3_ref[...]
    nv_ref[:, 0:WINDOW - t_new, :] = cv_ref[:, t_new:WINDOW, :]
    nv_ref[:, WINDOW - t_new:WINDOW, :] = vn3_ref[...]


def _swa_sample(sinks, qa, ka, va, cache_k, cache_v, batch, t_new):
    sb = SAMPLE_SEQ_BLOCK
    rows = sb * t_new
    r2 = lambda i: (i, 0)
    r3 = lambda i: (i, 0, 0)
    ka3 = ka.reshape(batch, t_new, KV_WIDTH)
    va3 = va.reshape(batch, t_new, KV_WIDTH)
    return pl.pallas_call(
        functools.partial(_swa_sample_kernel, t_new=t_new),
        grid=(batch // sb,),
        in_specs=[pl.BlockSpec(memory_space=pltpu.SMEM),
                  pl.BlockSpec((rows, ATTN_WIDTH), r2),
                  pl.BlockSpec((rows, KV_WIDTH), r2), pl.BlockSpec((rows, KV_WIDTH), r2),
                  pl.BlockSpec((sb, t_new, KV_WIDTH), r3), pl.BlockSpec((sb, t_new, KV_WIDTH), r3),
                  pl.BlockSpec((sb, WINDOW, KV_WIDTH), r3), pl.BlockSpec((sb, WINDOW, KV_WIDTH), r3)],
        out_specs=[pl.BlockSpec((rows, ATTN_WIDTH), r2),
                   pl.BlockSpec((sb, WINDOW, KV_WIDTH), r3), pl.BlockSpec((sb, WINDOW, KV_WIDTH), r3)],
        out_shape=[jax.ShapeDtypeStruct(qa.shape, BF16),
                   jax.ShapeDtypeStruct(cache_k.shape, F32), jax.ShapeDtypeStruct(cache_v.shape, F32)],
        compiler_params=_cparams(("parallel",)),
        name="swa_sample",
    )(sinks, qa, ka, va, ka3, va3, cache_k, cache_v)


def _gla_constants(c, seg):
    t = np.arange(c)
    sid = t // seg
    same = sid[:, None] == sid[None, :]
    levels = []
    m = seg // 2
    while m >= 1:
        levels.append(m)
        m //= 2
    mats, roles, masks = [], [], []
    for m in levels:
        blk = t // (2 * m)
        second = (t // m) % 2 == 1
        p = blk * 2 * m + m - 1
        u = t[None, :]
        mq = (u > p[:, None]) & (u <= t[:, None])
        mk = (u > t[:, None]) & (u <= p[:, None])
        mats.append(np.where(second[:, None], mq, mk))
        roles.append(np.broadcast_to(second[:, None], (c, LANES)))
        masks.append((blk[:, None] == blk[None, :]) & second[:, None] & ~second[None, :])
    masks.append(np.eye(c, dtype=bool))
    mats.append(same & (t[None, :] <= t[:, None]))
    mats.append(same & (t[None, :] > t[:, None]))
    mall = np.concatenate(mats, 0).astype(np.float32)
    mall = jnp.asarray(np.concatenate([mall, mall], 1), BF16)
    role = jnp.asarray(np.concatenate(roles, 0).astype(np.float32))
    mask = jnp.asarray(np.concatenate(masks, 0).astype(np.float32))
    return len(levels), mall, role, mask


def _gla_exponents(la, mall):
    hl = jnp.concatenate(_split_bf16(la * LOG2_E), axis=0)
    return _dot(mall, hl), hl


def _gla_scores(qb, kb, e2, role_ref, mask_ref, nlev, c):
    qf = qb.astype(F32)
    kf = kb.astype(F32)
    terms = []
    for lv in range(nlev):
        sl = slice(lv * c, (lv + 1) * c)
        x = (jnp.where(role_ref[sl, :] > 0.5, qf, kf) * jnp.exp2(e2[sl])).astype(BF16)
        terms.append((x, x, lv))
    terms.append((qb, kb, nlev))
    mask = lambda i: mask_ref[i * c:(i + 1) * c, :]
    a = None
    pair = c % LANES == 0
    while terms:
        if pair and len(terms) >= 2:
            (l0, r0, i0), (l1, r1, i1) = terms.pop(), terms.pop()
            z = jnp.zeros_like(r0)
            rhs = jnp.concatenate([jnp.concatenate([r0, z], axis=1), jnp.concatenate([z, r1], axis=1)], axis=0)
            g = _dot_nt(jnp.concatenate([l0, l1], axis=1), rhs)
            t = mask(i0) * g[:, :c] + mask(i1) * g[:, c:]
        else:
            l0, r0, i0 = terms.pop()
            t = mask(i0) * _dot_nt(l0, r0)
        a = t if a is None else a + t
    return a, qf, kf


def _gla_out(o, g, r):
    r = r.astype(F32)
    return (_rms(o, g) * (r * _sigmoid(r))).astype(BF16)


def _gla_prompt_kernel(q_ref, k_ref, v_ref, la_ref, r_ref, mall_ref, role_ref, mask_ref, g_ref, o_ref, s_ref,
                       s_scr, *, nlev, nchunks):
    c = GLA_CHUNK
    hp = GLA_HEADS_PER_STEP
    s_scr[...] = jnp.zeros_like(s_scr)
    ones = jnp.ones((2 * c, LANES), BF16)

    def chunk(i, carry):
        rows = pl.ds(pl.multiple_of(i * c, c), c)
        e2_all, hl_all = _gla_exponents(la_ref[rows, :], mall_ref[...])
        for h in range(hp):
            ks = slice(h * GLA_DK, (h + 1) * GLA_DK)
            vs = slice(h * GLA_DV, (h + 1) * GLA_DV)
            v = v_ref[rows, vs]
            e2 = e2_all[:, ks]
            a, qf, kf = _gla_scores(q_ref[rows, ks], k_ref[rows, ks], e2, role_ref, mask_ref, nlev, c)
            b = e2[nlev * c:(nlev + 1) * c]
            rem = e2[(nlev + 1) * c:(nlev + 2) * c]
            s = s_scr[h]
            o = _dot(a.astype(BF16), v) + _dot((qf * jnp.exp2(b)).astype(BF16), s.astype(BF16))
            kt = (kf * jnp.exp2(rem)).astype(BF16)
            dec = jnp.exp2(_dot_tn(hl_all[:, ks], ones))
            s_scr[h] = jnp.concatenate([dec] * (GLA_DV // LANES), axis=1) * s + _dot_tn(kt, v)
            o_ref[rows, vs] = _gla_out(o, g_ref[...], r_ref[rows, vs])
        return carry

    lax.fori_loop(0, nchunks, chunk, 0)
    s_ref[0] = s_scr[...]


def _gla_prompt(qg, kg, vg, la, rg, gnorm, batch, seq):
    nlev, mall, role, mask = _gla_constants(GLA_CHUNK, GLA_CHUNK)
    hp = GLA_HEADS_PER_STEP
    bh = lambda b, h: (b, h)
    const = lambda b, h: (0, 0)
    return pl.pallas_call(
        functools.partial(_gla_prompt_kernel, nlev=nlev, nchunks=seq // GLA_CHUNK),
        grid=(batch, GLA_HEADS // hp),
        in_specs=[pl.BlockSpec((seq, hp * GLA_DK), bh), pl.BlockSpec((seq, hp * GLA_DK), bh),
                  pl.BlockSpec((seq, hp * GLA_DV), bh), pl.BlockSpec((seq, hp * GLA_DK), bh),
                  pl.BlockSpec((seq, hp * GLA_DV), bh),
                  pl.BlockSpec(mall.shape, const), pl.BlockSpec(role.shape, const), pl.BlockSpec(mask.shape, const),
                  pl.BlockSpec((1, GLA_DV), const)],
        out_specs=[pl.BlockSpec((seq, hp * GLA_DV), bh),
                   pl.BlockSpec((1, hp, GLA_DK, GLA_DV), lambda b, h: (b, h, 0, 0))],
        out_shape=[jax.ShapeDtypeStruct(vg.shape, BF16),
                   jax.ShapeDtypeStruct((batch, GLA_HEADS, GLA_DK, GLA_DV), F32)],
        scratch_shapes=[pltpu.VMEM((hp, GLA_DK, GLA_DV), F32)],
        compiler_params=_cparams(("parallel", "parallel")),
        name="gla_prompt",
    )(qg, kg, vg, la, rg, mall, role, mask, gnorm)


def _gla_sample_kernel(q_ref, k_ref, v_ref, la_ref, r_ref, s0_ref, mall_ref, role_ref, mask_ref, msum_ref, g_ref,
                       o_ref, s_ref, *, nlev, t_new):
    sb = s0_ref.shape[0]
    c = sb * t_new
    v = v_ref[...]
    e2, hl = _gla_exponents(la_ref[...], mall_ref[...])
    a, qf, kf = _gla_scores(q_ref[...], k_ref[...], e2, role_ref, mask_ref, nlev, c)
    qe = (qf * jnp.exp2(e2[nlev * c:(nlev + 1) * c])).astype(BF16)
    kt_t = jnp.transpose(kf * jnp.exp2(e2[(nlev + 1) * c:(nlev + 2) * c])).astype(BF16)
    bl_t = jnp.transpose(_dot(msum_ref[...], hl[:c]) + _dot(msum_ref[...], hl[c:]))
    dec_t = jnp.exp2(bl_t)
    seq_of_row = lax.broadcasted_iota(jnp.int32, (c, GLA_DV), 0) // t_new
    o = _dot(a.astype(BF16), v)
    for j in range(sb):
        s0 = s0_ref[j, 0]
        mine = seq_of_row == j
        o = o + jnp.where(mine, _dot(qe, s0.astype(BF16)), 0.0)
        vj = jnp.where(mine, v, jnp.zeros_like(v))
        s_ref[j, 0] = dec_t[:, j:j + 1] * s0 + _dot(kt_t, vj)
    o_ref[...] = _gla_out(o, g_ref[...], r_ref[...])


def _gla_sample(qg, kg, vg, la, rg, state, gnorm, batch, t_new):
    sb = SAMPLE_SEQ_BLOCK
    c = sb * t_new
    nlev, mall, role, mask = _gla_constants(c, t_new)
    msum = jnp.asarray((np.arange(c)[None, :] // t_new == np.arange(sb)[:, None]).astype(np.float32), BF16)
    rh = lambda i, h: (i, h)
    const = lambda i, h: (0, 0)
    st = lambda i, h: (i, h, 0, 0)
    return pl.pallas_call(
        functools.partial(_gla_sample_kernel, nlev=nlev, t_new=t_new),
        grid=(batch // sb, GLA_HEADS),
        in_specs=[pl.BlockSpec((c, GLA_DK), rh), pl.BlockSpec((c, GLA_DK), rh), pl.BlockSpec((c, GLA_DV), rh),
                  pl.BlockSpec((c, GLA_DK), rh), pl.BlockSpec((c, GLA_DV), rh),
                  pl.BlockSpec((sb, 1, GLA_DK, GLA_DV), st),
                  pl.BlockSpec(mall.shape, const), pl.BlockSpec(role.shape, const), pl.BlockSpec(mask.shape, const),
                  pl.BlockSpec(msum.shape, const), pl.BlockSpec((1, GLA_DV), const)],
        out_specs=[pl.BlockSpec((c, GLA_DV), rh), pl.BlockSpec((sb, 1, GLA_DK, GLA_DV), st)],
        out_shape=[jax.ShapeDtypeStruct(vg.shape, BF16), jax.ShapeDtypeStruct(state.shape, F32)],
        compiler_params=_cparams(("parallel", "parallel")),
        name="gla_sample",
    )(qg, kg, vg, la, rg, state, mall, role, mask, msum, gnorm)


def _post_kernel(x_ref, a_ref, gl_ref, ga_ref, gb_ref, wpa, wpg, wo, nf_ref, wr_hi, wr_lo, br, x1t_o, rt_o):
    pa = _dot(a_ref[...], wpa[...])
    pg = _dot(gl_ref[...], wpg[...])
    merged = ga_ref[...].astype(F32) * pa + gb_ref[...].astype(F32) * pg
    x1 = x_ref[...] + _dot(merged.astype(BF16), wo[...])
    tm = x1.shape[0]
    for j in range(TOKEN_ROWS):
        x1t_o[pl.ds(j, tm, stride=TOKEN_ROWS), :] = x1[:, j * LANES:(j + 1) * LANES]
    h2 = _rms(x1, nf_ref[...])
    h_hi, h_lo = _split_bf16(h2)
    lt = _dot_nt(wr_hi[...], h_hi) + _dot_nt(wr_hi[...], h_lo) + _dot_nt(wr_lo[...], h_hi)
    nrow = ROUTER_ROWS
    lt = lt[:nrow] + br[:nrow, 0:1]
    row = lax.broadcasted_iota(jnp.int32, lt.shape, 0)
    big = jnp.int32(LANES)
    ninf = -jnp.inf

    def first_max(vals):
        mx = jnp.max(vals, axis=0, keepdims=True)
        return mx, jnp.min(jnp.where(vals == mx, row, big), axis=0, keepdims=True)

    gl = jnp.where((row >= N_EXPERTS) & (row < N_EXPERTS + N_GROUPS), lt, ninf)
    gmax, gidx = first_max(gl)
    p_sel = 1.0 / jnp.sum(jnp.exp(gl - gmax), axis=0, keepdims=True)
    lo = (gidx - N_EXPERTS) * EXPERTS_PER_GROUP
    el = jnp.where((row >= lo) & (row < lo + EXPERTS_PER_GROUP), lt, ninf)
    v1, i1 = first_max(el)
    el2 = jnp.where(row == i1, ninf, el)
    v2, i2 = first_max(el2)
    t = jnp.exp(v2 - v1)
    w1 = p_sel / (1.0 + t)
    w2 = p_sel * t / (1.0 + t)
    row8 = lax.broadcasted_iota(jnp.int32, rt_o.shape, 0)
    pick = lambda k, val, rest: jnp.where(row8 == k, val, rest)
    rt_o[...] = pick(0, i1.astype(F32), pick(1, i2.astype(F32), pick(2, w1, pick(3, w2, 0.0))))


def _post(x2d, a_out, g_out, ga, gb, w, tm):
    n = x2d.shape[0]
    row = lambda i: (i, 0)
    const = lambda i: (0, 0)
    wnames = ("wpa", "wpg", "wo", "nf", "wr_hi", "wr_lo", "br")
    rt_rows = 8
    return pl.pallas_call(
        _post_kernel,
        grid=(n // tm,),
        in_specs=[pl.BlockSpec((tm, D_MODEL), row), pl.BlockSpec((tm, ATTN_WIDTH), row),
                  pl.BlockSpec((tm, GLA_VALUE_WIDTH), row), pl.BlockSpec((tm, D_MODEL), row),
                  pl.BlockSpec((tm, D_MODEL), row)] + [pl.BlockSpec(w[k].shape, const) for k in wnames],
        out_specs=[pl.BlockSpec((tm * TOKEN_ROWS, LANES), row),
                   pl.BlockSpec((rt_rows, tm), lambda i: (0, i))],
        out_shape=[jax.ShapeDtypeStruct((n * TOKEN_ROWS, LANES), F32), jax.ShapeDtypeStruct((rt_rows, n), F32)],
        compiler_params=_cparams(("parallel",)),
        name="post_mixer",
    )(x2d, a_out, g_out, ga, gb, *[w[k] for k in wnames])


def _moe_plan(rt, tme):
    n = rt.shape[1]
    ntiles = n // tme
    max_items = ntiles + N_GROUPS - 1
    i1, i2 = rt[0].astype(jnp.int32), rt[1].astype(jnp.int32)
    grp = i1 // EXPERTS_PER_GROUP
    lo = jnp.minimum(i1, i2) % EXPERTS_PER_GROUP
    hi = jnp.maximum(i1, i2) % EXPERTS_PER_GROUP
    key = (grp * EXPERTS_PER_GROUP + lo) * EXPERTS_PER_GROUP + hi
    order = jnp.argsort(key, stable=True).astype(jnp.int32)
    skey = key[order].reshape(ntiles, tme)
    sg = skey // (EXPERTS_PER_GROUP * EXPERTS_PER_GROUP)
    slo = (skey // EXPERTS_PER_GROUP) % EXPERTS_PER_GROUP
    shi = skey % EXPERTS_PER_GROUP
    ev = jnp.arange(EXPERTS_PER_GROUP)
    in_g = sg[:, :, None] == jnp.arange(N_GROUPS)
    uses_e = (slo[:, :, None] == ev) | (shi[:, :, None] == ev)
    flags_tge = jnp.any(in_g[:, :, :, None] & uses_e[:, :, None, :], axis=1)
    present = jnp.any(in_g, axis=1).reshape(-1)
    pos = jnp.cumsum(present) - 1
    n_items = pos[-1] + 1
    src = jnp.zeros((max_items,), jnp.int32).at[jnp.where(present, pos, max_items)].set(
        jnp.arange(ntiles * N_GROUPS, dtype=jnp.int32), mode="drop")
    it = jnp.arange(max_items)
    valid = it < n_items
    last_src = src[n_items - 1]
    src = jnp.where(valid, src, last_src)
    item_tile = src // N_GROUPS
    item_group = src % N_GROUPS
    prev_tile = jnp.concatenate([jnp.full((1,), -1, jnp.int32), item_tile[:-1]])
    next_tile = jnp.concatenate([item_tile[1:], jnp.full((1,), -1, jnp.int32)])
    first = valid & (item_tile != prev_tile)
    last = valid & ((item_tile != next_tile) | (it == n_items - 1))
    flags = flags_tge[item_tile, item_group] & valid[:, None]
    e0 = jnp.argmax(flags, axis=1)
    rest = flags & (ev[None, :] != e0[:, None])
    e1 = jnp.argmax(rest, axis=1)
    rest = rest & (ev[None, :] != e1[:, None])
    rt_sorted = rt[:, order].reshape(rt.shape[0], ntiles, tme).transpose(1, 0, 2)
    i32 = lambda z: z.astype(jnp.int32)
    plan = (order, i32(item_tile), i32(item_group), i32(first), i32(last), i32(valid), i32(e0), i32(e1),
            i32(rest.reshape(-1)))
    return plan, rt_sorted


def _moe_kernel(order, itile, igroup, ifirst, ilast, ivalid, ie0, ie1, flags, x_hbm, rt_ref, wg, wu, wd, nffn, nfin,
                y_hbm, xbuf, ybuf, acc, hbuf, rcol, gsem, ssem, *, tme, ntiles):
    i = pl.program_id(0)
    t = itile[i]
    slot = t % 2
    g = igroup[i]
    is_first = ifirst[i] == 1

    def gather_row(tile, sl, r):
        tok = order[tile * tme + r]
        src = x_hbm.at[pl.ds(pl.multiple_of(tok * TOKEN_ROWS, TOKEN_ROWS), TOKEN_ROWS)]
        dst = xbuf.at[pl.ds(pl.multiple_of((sl * tme + r) * TOKEN_ROWS, TOKEN_ROWS), TOKEN_ROWS)]
        return pltpu.make_async_copy(src, dst, gsem.at[sl])

    def scatter_row(tile, sl, r):
        tok = order[tile * tme + r]
        return pltpu.make_async_copy(ybuf.at[sl, pl.ds(r, 1)], y_hbm.at[pl.ds(tok, 1)], ssem.at[sl])

    def start_rows(make, tile, sl):
        def body(r, c):
            make(tile, sl, r).start()
            return c
        lax.fori_loop(0, tme, body, 0, unroll=8)

    def wait_gather(sl):
        rows = tme * TOKEN_ROWS
        pltpu.make_async_copy(x_hbm.at[pl.ds(0, rows)], xbuf.at[pl.ds(pl.multiple_of(sl * rows, rows), rows)],
                              gsem.at[sl]).wait()

    def wait_scatter(sl):
        pltpu.make_async_copy(ybuf.at[sl], y_hbm.at[pl.ds(0, tme)], ssem.at[sl]).wait()

    def start_rows_inline(make, tile, sl):
        for r in range(tme):
            make(tile, sl, r).start()

    def expert(e):
        eid = (g * EXPERTS_PER_GROUP + e).astype(F32)
        ce = (jnp.where(rcol[:, 0:1] == eid, rcol[:, 2:3], 0.0)
              + jnp.where(rcol[:, 1:2] == eid, rcol[:, 3:4], 0.0))
        h = hbuf[...]
        a = _dot(h, wg[e])
        u = _dot(h, wu[e])
        act = (a * _sigmoid(a)) * u * ce
        acc[...] += _dot(act.astype(BF16), wd[e])

    def expert_with(e, cond, side_work):
        @pl.when(cond)
        def _():
            side_work()
            expert(e)

        @pl.when(jnp.logical_and(jnp.logical_not(cond), ivalid[i] == 1))
        def _():
            expert(e)

    @pl.when(i == 0)
    def _():
        start_rows(gather_row, 0, 0)

    @pl.when(is_first)
    def _():
        wait_gather(slot)
        base = pl.multiple_of(slot * (tme * TOKEN_ROWS), tme * TOKEN_ROWS)
        x1 = jnp.concatenate([xbuf[pl.ds(base + j, tme, stride=TOKEN_ROWS), :] for j in range(TOKEN_ROWS)], axis=1)
        acc[...] = x1
        hbuf[...] = _rms(x1, nffn[...]).astype(BF16)
        rt = rt_ref[0]
        rcol[...] = jnp.transpose(jnp.concatenate([rt, jnp.zeros((LANES - rt.shape[0], tme), F32)], axis=0))

    expert_with(ie0[i], jnp.logical_and(is_first, t + 1 < ntiles),
                lambda: start_rows_inline(gather_row, t + 1, 1 - slot))
    expert_with(ie1[i], jnp.logical_and(is_first, t >= 1),
                lambda: start_rows_inline(scatter_row, t - 1, 1 - slot))
    for e in range(EXPERTS_PER_GROUP):
        @pl.when(flags[i * EXPERTS_PER_GROUP + e] == 1)
        def _():
            expert(e)

    @pl.when(ilast[i] == 1)
    def _():
        @pl.when(t >= 2)
        def _():
            wait_scatter(slot)

        ybuf[slot] = _rms(acc[...], nfin[...])

        @pl.when(t == ntiles - 1)
        def _():
            start_rows(scatter_row, t, slot)

    @pl.when(i == pl.num_programs(0) - 1)
    def _():
        for sl in range(min(2, ntiles)):
            wait_scatter(sl)


def _moe(x1t, rt, weg, weu, wed, nffn, nfin):
    n = x1t.shape[0] // TOKEN_ROWS
    tme = min(MOE_TILE, n)
    assert n % tme == 0
    ntiles = n // tme
    plan, rt_sorted = _moe_plan(rt, tme)
    max_items = ntiles + N_GROUPS - 1
    grp = lambda i, order, itile, igroup, *_: (igroup[i], 0, 0)
    til = lambda i, order, itile, *_: (itile[i], 0, 0)
    const = lambda i, *_: (0, 0)
    grid_spec = pltpu.PrefetchScalarGridSpec(
        num_scalar_prefetch=len(plan),
        grid=(max_items,),
        in_specs=[pl.BlockSpec(memory_space=pl.ANY),
                  pl.BlockSpec((1,) + rt_sorted.shape[1:], til),
                  pl.BlockSpec((EXPERTS_PER_GROUP, D_MODEL, EXPERT_FF), grp),
                  pl.BlockSpec((EXPERTS_PER_GROUP, D_MODEL, EXPERT_FF), grp),
                  pl.BlockSpec((EXPERTS_PER_GROUP, EXPERT_FF, D_MODEL), grp),
                  pl.BlockSpec((1, D_MODEL), const), pl.BlockSpec((1, D_MODEL), const)],
        out_specs=pl.BlockSpec(memory_space=pl.ANY),
        scratch_shapes=[pltpu.VMEM((2 * tme * TOKEN_ROWS, LANES), F32), pltpu.VMEM((2, tme, D_MODEL), F32),
                        pltpu.VMEM((tme, D_MODEL), F32), pltpu.VMEM((tme, D_MODEL), BF16),
                        pltpu.VMEM((tme, LANES), F32),
                        pltpu.SemaphoreType.DMA((2,)), pltpu.SemaphoreType.DMA((2,))],
    )
    return pl.pallas_call(
        functools.partial(_moe_kernel, tme=tme, ntiles=ntiles),
        grid_spec=grid_spec,
        out_shape=jax.ShapeDtypeStruct((n, D_MODEL), F32),
        compiler_params=_cparams(("arbitrary",)),
        name="moe",
    )(*plan, x1t, rt_sorted, weg, weu, wed, nffn, nfin)


def _rope_tables(positions):
    half = HEAD_DIM // 2
    inv_freq = ROPE_THETA ** (-jnp.arange(half, dtype=F32) / half)
    ang = positions.astype(F32)[:, None] * inv_freq[None, :]
    cos, sin = jnp.cos(ang), jnp.sin(ang)
    reps = LANES // HEAD_DIM
    return (jnp.tile(jnp.concatenate([cos, cos], -1), (1, reps)),
            jnp.tile(jnp.concatenate([-sin, sin], -1), (1, reps)))


def _prep_weights(norm_mix, w_in, w_gla_f2, b_gla_f, gla_norm, w_proj_attn, w_proj_gla, w_out, norm_ffn,
                  w_router_group, b_router_group, w_router_expert, b_router_expert):
    widths = (ATTN_WIDTH, 2 * KV_WIDTH, GLA_KEY_WIDTH, GLA_KEY_WIDTH, GLA_VALUE_WIDTH, GLA_VALUE_WIDTH,
              GLA_GATE_RANK, D_MODEL, D_MODEL)
    names = ("wqa", "wkva", "wqg", "wkg", "wvg", "wrg", "wf", "wga", "wgb")
    w, off = {}, 0
    for nm, wd in zip(names, widths):
        w[nm] = w_in[:, off:off + wd].astype(BF16)
        off += wd
    w["wf"] = jnp.pad(w["wf"], ((0, 0), (0, LANES - GLA_GATE_RANK)))
    w["wf2"] = jnp.pad(w_gla_f2.astype(BF16), ((0, LANES - GLA_GATE_RANK), (0, 0)))
    w["bf"] = b_gla_f.reshape(1, -1)
    w["norm_mix"] = norm_mix.reshape(1, -1)
    w["gla_norm"] = gla_norm.reshape(1, -1)
    w["wpa"] = w_proj_attn.astype(BF16)
    w["wpg"] = w_proj_gla.astype(BF16)
    w["wo"] = w_out.astype(BF16)
    w["nf"] = norm_ffn.reshape(1, -1)
    pad = LANES - N_EXPERTS - N_GROUPS
    wr_t = jnp.pad(jnp.concatenate([w_router_expert, w_router_group], axis=1), ((0, 0), (0, pad))).T
    w["wr_hi"], w["wr_lo"] = _split_bf16(wr_t)
    w["br"] = jnp.broadcast_to(jnp.pad(jnp.concatenate([b_router_expert, b_router_group]), (0, pad))[:, None],
                               (LANES, LANES))
    return w


def _layer(x, positions_tab, cache, w, sinks, weg, weu, wed, nfin, tm):
    batch, seq, _ = x.shape
    n = batch * seq
    x2d = x.reshape(n, D_MODEL)
    cos_t, sin_t = positions_tab
    qa, ka, va, qg, kg, vg, rg, la, ga, gb = _inproj(x2d, w["norm_mix"], cos_t, sin_t, w, tm)
    if cache is None:
        a_out = _swa_prompt(sinks, qa, ka, va, batch, seq)
        last = lambda z: z.reshape(batch, seq, KV_WIDTH)[:, seq - WINDOW:].reshape(batch, WINDOW, ATTN_KV_HEADS, HEAD_DIM)
        new_k, new_v = last(ka), last(va)
        g_out, new_s = _gla_prompt(qg, kg, vg, la, rg, w["gla_norm"], batch, seq)
    else:
        cache_k, cache_v, state = cache
        a_out, new_k, new_v = _swa_sample(sinks, qa, ka, va, cache_k.reshape(batch, WINDOW, KV_WIDTH),
                                          cache_v.reshape(batch, WINDOW, KV_WIDTH), batch, seq)
        new_k = new_k.reshape(batch, WINDOW, ATTN_KV_HEADS, HEAD_DIM)
        new_v = new_v.reshape(batch, WINDOW, ATTN_KV_HEADS, HEAD_DIM)
        g_out, new_s = _gla_sample(qg, kg, vg, la, rg, state, w["gla_norm"], batch, seq)
    x1t, rt = _post(x2d, a_out, g_out, ga, gb, w, tm)
    y = _moe(x1t, rt, weg, weu, wed, w["nf"], nfin)
    return y.reshape(batch, seq, D_MODEL), new_k, new_v, new_s


def kernel(x_prompt, x_sample, cache_win_k, cache_win_v, state_gla, norm_mix, w_in, w_gla_f2, b_gla_f, gla_norm,
           attn_sinks, w_proj_attn, w_proj_gla, w_out, norm_ffn, w_router_group, b_router_group, w_router_expert,
           b_router_expert, w_exp_gate, w_exp_up, w_exp_down, norm_final):
    assert norm_mix.shape[0] == 1, "single-layer step"
    seq_p = x_prompt.shape[1]
    dec_b, dec_t = x_sample.shape[0], x_sample.shape[1]
    w = _prep_weights(norm_mix[0], w_in[0], w_gla_f2[0], b_gla_f[0], gla_norm[0], w_proj_attn[0], w_proj_gla[0],
                      w_out[0], norm_ffn[0], w_router_group[0], b_router_group[0], w_router_expert[0],
                      b_router_expert[0])
    weg = w_exp_gate[0].astype(BF16)
    weu = w_exp_up[0].astype(BF16)
    wed = w_exp_down[0].astype(BF16)
    nfin = norm_final.reshape(1, -1)
    sinks = attn_sinks[0]
    tab_p = _rope_tables(jnp.arange(seq_p, dtype=jnp.int32))
    pos_s = PAST_LEN + jnp.arange(dec_t, dtype=jnp.int32)
    tab_s = tuple(jnp.tile(t, (dec_b, 1)) for t in _rope_tables(pos_s))
    tm_p = min(512, seq_p)
    tm_s = dec_b * dec_t
    yp, pk, pv, ps = _layer(x_prompt, tab_p, None, w, sinks, weg, weu, wed, nfin, tm_p)
    ys, sk, sv, ss = _layer(x_sample, tab_s, (cache_win_k[0], cache_win_v[0], state_gla[0]), w, sinks, weg, weu, wed,
                            nfin, tm_s)
    return (yp, ys, pk[None], pv[None], ps[None], sk[None], sv[None], ss[None])
```

```python
import functools

import numpy as np
import jax
import jax.numpy as jnp
from jax import lax
from jax.experimental import pallas as pl
from jax.experimental.pallas import tpu as pltpu

F32 = jnp.float32
BF16 = jnp.bfloat16

D_MODEL = 1024
ATTN_HEADS = 8
ATTN_KV_HEADS = 2
GROUP = ATTN_HEADS // ATTN_KV_HEADS
HEAD_DIM = 64
ATTN_WIDTH = ATTN_HEADS * HEAD_DIM
KV_WIDTH = ATTN_KV_HEADS * HEAD_DIM
WINDOW = 128
ROPE_THETA = 10000.0
PAST_LEN = 8192
GLA_HEADS = 4
GLA_KEY_WIDTH = D_MODEL // 2
GLA_VALUE_WIDTH = D_MODEL
GLA_DK = GLA_KEY_WIDTH // GLA_HEADS
GLA_DV = GLA_VALUE_WIDTH // GLA_HEADS
GLA_GATE_RANK = 16
GLA_GATE_NORMALIZER = 16.0
N_GROUPS = 4
EXPERTS_PER_GROUP = 8
N_EXPERTS = N_GROUPS * EXPERTS_PER_GROUP
EXPERT_FF = 256
EPS = 1e-6
LOG2_E = 1.4426950408889634

LANES = 128
GLA_CHUNK = 128
GLA_HEADS_PER_STEP = 4
SAMPLE_SEQ_BLOCK = 8
MOE_TILE = 256
TOKEN_ROWS = D_MODEL // LANES
ROUTER_ROWS = 40
VMEM_LIMIT = 56 * 1024 * 1024


def _cparams(sem):
    return pltpu.CompilerParams(dimension_semantics=sem, vmem_limit_bytes=VMEM_LIMIT)


def _rms(x, g):
    return x * lax.rsqrt(jnp.mean(x * x, axis=-1, keepdims=True) + EPS) * g


def _sigmoid(x):
    return 1.0 / (1.0 + jnp.exp(-x))


def _dot(a, b):
    return jnp.dot(a, b, preferred_element_type=F32)


def _dot_nt(a, b):
    return lax.dot_general(a, b, (((1,), (1,)), ((), ())), preferred_element_type=F32)


def _dot_tn(a, b):
    return lax.dot_general(a, b, (((0,), (0,)), ((), ())), preferred_element_type=F32)


def _split_bf16(x):
    hi = x.astype(BF16)
    lo = (x - hi.astype(F32)).astype(BF16)
    return hi, lo


def _inproj_kernel(x_ref, g_ref, cos_ref, sin_ref, wqa, wkva, wqg, wkg, wvg, wrg, wga, wgb, wf, wf2, bf,
                   qa_o, ka_o, va_o, qg_o, kg_o, vg_o, rg_o, la_o, ga_o, gb_o):
    hb = _rms(x_ref[...], g_ref[...]).astype(BF16)
    cos = cos_ref[...]
    sin = sin_ref[...]
    lane = lax.broadcasted_iota(jnp.int32, cos.shape, 1)
    first_half = (lane % HEAD_DIM) < (HEAD_DIM // 2)

    def rope(t):
        swapped = jnp.where(first_half, pltpu.roll(t, LANES - HEAD_DIM // 2, 1), pltpu.roll(t, HEAD_DIM // 2, 1))
        return t * cos + swapped * sin

    qa = _dot(hb, wqa[...])
    for c in range(ATTN_WIDTH // LANES):
        sl = slice(c * LANES, (c + 1) * LANES)
        qa_o[:, sl] = (rope(qa[:, sl]) * (HEAD_DIM ** -0.5)).astype(BF16)
    kva = _dot(hb, wkva[...])
    ka_o[...] = rope(kva[:, :KV_WIDTH])
    va_o[...] = kva[:, KV_WIDTH:]
    qg_o[...] = (_dot(hb, wqg[...]) * (GLA_DK ** -0.5)).astype(BF16)
    kg_o[...] = _dot(hb, wkg[...]).astype(BF16)
    vg_o[...] = _dot(hb, wvg[...]).astype(BF16)
    rg_o[...] = _dot(hb, wrg[...]).astype(BF16)
    z = _dot(_dot(hb, wf[...]).astype(BF16), wf2[...]) + bf[...]
    la_o[...] = (jnp.minimum(z, 0.0) - jnp.log1p(jnp.exp(-jnp.abs(z)))) * (1.0 / GLA_GATE_NORMALIZER)
    ga_o[...] = _sigmoid(_dot(hb, wga[...])).astype(BF16)
    gb_o[...] = _sigmoid(_dot(hb, wgb[...])).astype(BF16)


def _inproj(x2d, norm_g, cos_t, sin_t, w, tm):
    n = x2d.shape[0]
    ntab = cos_t.shape[0] // tm
    row = lambda i: (i, 0)
    const = lambda i: (0, 0)
    tab = lambda i: (i % ntab, 0)
    wnames = ("wqa", "wkva", "wqg", "wkg", "wvg", "wrg", "wga", "wgb", "wf", "wf2", "bf")
    wspecs = [pl.BlockSpec(w[k].shape, const) for k in wnames]
    widths = (ATTN_WIDTH, KV_WIDTH, KV_WIDTH, GLA_KEY_WIDTH, GLA_KEY_WIDTH, GLA_VALUE_WIDTH, GLA_VALUE_WIDTH,
              GLA_KEY_WIDTH, D_MODEL, D_MODEL)
    dtypes = (BF16, F32, F32, BF16, BF16, BF16, BF16, F32, BF16, BF16)
    return pl.pallas_call(
        _inproj_kernel,
        grid=(n // tm,),
        in_specs=[pl.BlockSpec((tm, D_MODEL), row), pl.BlockSpec((1, D_MODEL), const),
                  pl.BlockSpec((tm, LANES), tab), pl.BlockSpec((tm, LANES), tab)] + wspecs,
        out_specs=[pl.BlockSpec((tm, wd), row) for wd in widths],
        out_shape=[jax.ShapeDtypeStruct((n, wd), dt) for wd, dt in zip(widths, dtypes)],
        compiler_params=_cparams(("parallel",)),
        name="inproj",
    )(x2d, norm_g, cos_t, sin_t, *[w[k] for k in wnames])


def _swa_prompt_kernel(sink_ref, q_ref, kc_ref, kp_ref, vc_ref, vp_ref, o_ref):
    n = pl.program_id(1)
    k2 = jnp.concatenate([kp_ref[...], kc_ref[...]], axis=0).astype(BF16)
    v2 = jnp.concatenate([vp_ref[...], vc_ref[...]], axis=0).astype(BF16)
    t = lax.broadcasted_iota(jnp.int32, (WINDOW, 2 * WINDOW), 0)
    j = lax.broadcasted_iota(jnp.int32, (WINDOW, 2 * WINDOW), 1)
    valid = (j >= t) & (j <= t + WINDOW) & ((j >= WINDOW) | (n > 0))
    zeros = jnp.zeros((2 * WINDOW, HEAD_DIM), BF16)
    ones = jnp.ones((2 * WINDOW, HEAD_DIM), BF16)
    lane = lax.broadcasted_iota(jnp.int32, (WINDOW, LANES), 1)
    scores = []
    for h in range(ATTN_HEADS):
        kv = h // GROUP
        s = _dot_nt(q_ref[:, h * HEAD_DIM:(h + 1) * HEAD_DIM], k2[:, kv * HEAD_DIM:(kv + 1) * HEAD_DIM])
        scores.append(jnp.where(valid, s, -jnp.inf))
    probs, sink_terms = [], []
    for h, s in enumerate(scores):
        m = jnp.maximum(jnp.max(s, axis=-1, keepdims=True), sink_ref[h])
        probs.append(jnp.exp(s - m).astype(BF16))
        sink_terms.append(jnp.exp(sink_ref[h] - m))
    for kv in range(ATTN_KV_HEADS):
        vv = v2[:, kv * HEAD_DIM:(kv + 1) * HEAD_DIM]
        vext = (jnp.concatenate([vv, zeros, ones, zeros], axis=1), jnp.concatenate([zeros, vv, zeros, ones], axis=1))
        for pr in range(GROUP // 2):
            h0 = kv * GROUP + pr * 2
            acc = _dot(probs[h0], vext[0]) + _dot(probs[h0 + 1], vext[1])
            l = acc[:, LANES:] + jnp.where(lane < HEAD_DIM, sink_terms[h0], sink_terms[h0 + 1])
            o_ref[:, h0 * HEAD_DIM:h0 * HEAD_DIM + LANES] = (acc[:, :LANES] / l).astype(BF16)


def _swa_prompt(sinks, qa, ka, va, batch, seq):
    nb = seq // WINDOW
    cur = lambda b, n: (b * nb + n, 0)
    prev = lambda b, n: (b * nb + jnp.maximum(n - 1, 0), 0)
    return pl.pallas_call(
        _swa_prompt_kernel,
        grid=(batch, nb),
        in_specs=[pl.BlockSpec(memory_space=pltpu.SMEM),
                  pl.BlockSpec((WINDOW, ATTN_WIDTH), cur),
                  pl.BlockSpec((WINDOW, KV_WIDTH), cur), pl.BlockSpec((WINDOW, KV_WIDTH), prev),
                  pl.BlockSpec((WINDOW, KV_WIDTH), cur), pl.BlockSpec((WINDOW, KV_WIDTH), prev)],
        out_specs=pl.BlockSpec((WINDOW, ATTN_WIDTH), cur),
        out_shape=jax.ShapeDtypeStruct(qa.shape, BF16),
        compiler_params=_cparams(("parallel", "parallel")),
        name="swa_prompt",
    )(sinks, qa, ka, ka, va, va)


def _swa_sample_kernel(sink_ref, q_ref, kn_ref, vn_ref, kn3_ref, vn3_ref, ck_ref, cv_ref, o_ref, nk_ref, nv_ref,
                       *, t_new):
    sb = ck_ref.shape[0]
    spv = 8 // t_new
    nq = GROUP * 8
    nc = spv * WINDOW
    qi = lax.broadcasted_iota(jnp.int32, (nq, nc), 0) % 8
    ci = lax.broadcasted_iota(jnp.int32, (nq, nc), 1)
    valid_c = (qi // t_new == ci // WINDOW) & (ci % WINDOW >= qi % t_new)
    qn = lax.broadcasted_iota(jnp.int32, (nq, 8), 0) % 8
    cn = lax.broadcasted_iota(jnp.int32, (nq, 8), 1)
    valid_n = (qn // t_new == cn // t_new) & (cn <= qn)
    grow = lax.broadcasted_iota(jnp.int32, (nq, 1), 0) // 8
    chains = [(vr, kv) for vr in range(sb // spv) for kv in range(ATTN_KV_HEADS)]
    scored = []
    for vr, kv in chains:
        r8 = slice(8 * vr, 8 * vr + 8)
        cs = slice(kv * HEAD_DIM, (kv + 1) * HEAD_DIM)
        heads = [kv * GROUP + g for g in range(GROUP)]
        qs = jnp.concatenate([q_ref[r8, h * HEAD_DIM:(h + 1) * HEAD_DIM] for h in heads], axis=0)
        ck = ck_ref[vr * spv:(vr + 1) * spv, :, cs].reshape(nc, HEAD_DIM).astype(BF16)
        kn = kn_ref[r8, cs].astype(BF16)
        scored.append((jnp.where(valid_c, _dot_nt(qs, ck), -jnp.inf), jnp.where(valid_n, _dot_nt(qs, kn), -jnp.inf)))
    soft = []
    for (vr, kv), (s_c, s_n) in zip(chains, scored):
        sink = sink_ref[kv * GROUP]
        for g in range(1, GROUP):
            sink = jnp.where(grow == g, sink_ref[kv * GROUP + g], sink)
        m = jnp.maximum(jnp.maximum(jnp.max(s_c, axis=-1, keepdims=True), jnp.max(s_n, axis=-1, keepdims=True)), sink)
        p_c = jnp.exp(s_c - m)
        p_n = jnp.exp(s_n - m)
        l = jnp.sum(p_c, axis=-1, keepdims=True) + jnp.sum(p_n, axis=-1, keepdims=True) + jnp.exp(sink - m)
        soft.append((p_c.astype(BF16), p_n.astype(BF16), l))
    for (vr, kv), (p_c, p_n, l) in zip(chains, soft):
        r8 = slice(8 * vr, 8 * vr + 8)
        cs = slice(kv * HEAD_DIM, (kv + 1) * HEAD_DIM)
        cv = cv_ref[vr * spv:(vr + 1) * spv, :, cs].reshape(nc, HEAD_DIM).astype(BF16)
        o = (_dot(p_c, cv) + _dot(p_n, vn_ref[r8, cs].astype(BF16))) / l
        for a in range(GROUP // 2):
            pair = jnp.concatenate([o[16 * a:16 * a + 8], o[16 * a + 8:16 * a + 16]], axis=1)
            c0 = (kv * GROUP + 2 * a) * HEAD_DIM
            o_ref[r8, c0:c0 + LANES] = pair.astype(BF16)
    nk_ref[:, 0:WINDOW - t_new, :] = ck_ref[:, t_new:WINDOW, :]
    nk_ref[:, WINDOW - t_new:WINDOW, :] = kn3_ref[...]
    nv_ref[:, 0:WINDOW - t_new, :] = cv_ref[:, t_new:WINDOW, :]
    nv_ref[:, WINDOW - t_new:WINDOW, :] = vn3_ref[...]


def _swa_sample(sinks, qa, ka, va, cache_k, cache_v, batch, t_new):
    sb = SAMPLE_SEQ_BLOCK
    rows = sb * t_new
    r2 = lambda i: (i, 0)
    r3 = lambda i: (i, 0, 0)
    ka3 = ka.reshape(batch, t_new, KV_WIDTH)
    va3 = va.reshape(batch, t_new, KV_WIDTH)
    return pl.pallas_call(
        functools.partial(_swa_sample_kernel, t_new=t_new),
        grid=(batch // sb,),
        in_specs=[pl.BlockSpec(memory_space=pltpu.SMEM),
                  pl.BlockSpec((rows, ATTN_WIDTH), r2),
                  pl.BlockSpec((rows, KV_WIDTH), r2), pl.BlockSpec((rows, KV_WIDTH), r2),
                  pl.BlockSpec((sb, t_new, KV_WIDTH), r3), pl.BlockSpec((sb, t_new, KV_WIDTH), r3),
                  pl.BlockSpec((sb, WINDOW, KV_WIDTH), r3), pl.BlockSpec((sb, WINDOW, KV_WIDTH), r3)],
        out_specs=[pl.BlockSpec((rows, ATTN_WIDTH), r2),
                   pl.BlockSpec((sb, WINDOW, KV_WIDTH), r3), pl.BlockSpec((sb, WINDOW, KV_WIDTH), r3)],
        out_shape=[jax.ShapeDtypeStruct(qa.shape, BF16),
                   jax.ShapeDtypeStruct(cache_k.shape, F32), jax.ShapeDtypeStruct(cache_v.shape, F32)],
        compiler_params=_cparams(("parallel",)),
        name="swa_sample",
    )(sinks, qa, ka, va, ka3, va3, cache_k, cache_v)


def _gla_constants(c, seg):
    t = np.arange(c)
    sid = t // seg
    same = sid[:, None] == sid[None, :]
    levels = []
    m = seg // 2
    while m >= 1:
        levels.append(m)
        m //= 2
    mats, roles, masks = [], [], []
    for m in levels:
        blk = t // (2 * m)
        second = (t // m) % 2 == 1
        p = blk * 2 * m + m - 1
        u = t[None, :]
        mq = (u > p[:, None]) & (u <= t[:, None])
        mk = (u > t[:, None]) & (u <= p[:, None])
        mats.append(np.where(second[:, None], mq, mk))
        roles.append(np.broadcast_to(second[:, None], (c, LANES)))
        masks.append((blk[:, None] == blk[None, :]) & second[:, None] & ~second[None, :])
    masks.append(np.eye(c, dtype=bool))
    mats.append(same & (t[None, :] <= t[:, None]))
    mats.append(same & (t[None, :] > t[:, None]))
    mall = np.concatenate(mats, 0).astype(np.float32)
    mall = jnp.asarray(np.concatenate([mall, mall], 1), BF16)
    role = jnp.asarray(np.concatenate(roles, 0).astype(np.float32))
    mask = jnp.asarray(np.concatenate(masks, 0).astype(np.float32))
    return len(levels), mall, role, mask


def _gla_exponents(la, mall):
    hl = jnp.concatenate(_split_bf16(la * LOG2_E), axis=0)
    return _dot(mall, hl), hl


def _gla_scores(qb, kb, e2, role_ref, mask_ref, nlev, c):
    qf = qb.astype(F32)
    kf = kb.astype(F32)
    terms = []
    for lv in range(nlev):
        sl = slice(lv * c, (lv + 1) * c)
        x = (jnp.where(role_ref[sl, :] > 0.5, qf, kf) * jnp.exp2(e2[sl])).astype(BF16)
        terms.append((x, x, lv))
    terms.append((qb, kb, nlev))
    mask = lambda i: mask_ref[i * c:(i + 1) * c, :]
    a = None
    pair = c % LANES == 0
    while terms:
        if pair and len(terms) >= 2:
            (l0, r0, i0), (l1, r1, i1) = terms.pop(), terms.pop()
            z = jnp.zeros_like(r0)
            rhs = jnp.concatenate([jnp.concatenate([r0, z], axis=1), jnp.concatenate([z, r1], axis=1)], axis=0)
            g = _dot_nt(jnp.concatenate([l0, l1], axis=1), rhs)
            t = mask(i0) * g[:, :c] + mask(i1) * g[:, c:]
        else:
            l0, r0, i0 = terms.pop()
            t = mask(i0) * _dot_nt(l0, r0)
        a = t if a is None else a + t
    return a, qf, kf


def _gla_out(o, g, r):
    r = r.astype(F32)
    return (_rms(o, g) * (r * _sigmoid(r))).astype(BF16)


def _gla_prompt_kernel(q_ref, k_ref, v_ref, la_ref, r_ref, mall_ref, role_ref, mask_ref, g_ref, o_ref, s_ref,
                       s_scr, *, nlev, nchunks):
    c = GLA_CHUNK
    hp = GLA_HEADS_PER_STEP
    s_scr[...] = jnp.zeros_like(s_scr)
    ones = jnp.ones((2 * c, LANES), BF16)

    def chunk(i, carry):
        rows = pl.ds(pl.multiple_of(i * c, c), c)
        e2_all, hl_all = _gla_exponents(la_ref[rows, :], mall_ref[...])
        ksl = [slice(h * GLA_DK, (h + 1) * GLA_DK) for h in range(hp)]
        vsl = [slice(h * GLA_DV, (h + 1) * GLA_DV) for h in range(hp)]
        inter = []
        for h in range(hp):
            e2 = e2_all[:, ksl[h]]
            v = v_ref[rows, vsl[h]]
            qf = q_ref[rows, ksl[h]].astype(F32)
            kf = k_ref[rows, ksl[h]].astype(F32)
            s = s_scr[h]
            inter.append(_dot((qf * jnp.exp2(e2[nlev * c:(nlev + 1) * c])).astype(BF16), s.astype(BF16)))
            kt = (kf * jnp.exp2(e2[(nlev + 1) * c:(nlev + 2) * c])).astype(BF16)
            dec = jnp.exp2(_dot_tn(hl_all[:, ksl[h]], ones))
            s_scr[h] = jnp.concatenate([dec] * (GLA_DV // LANES), axis=1) * s + _dot_tn(kt, v)
        scores = [_gla_scores(q_ref[rows, ksl[h]], k_ref[rows, ksl[h]], e2_all[:, ksl[h]], role_ref, mask_ref, nlev, c)[0]
                  for h in range(hp)]
        for h in range(hp):
            o = _dot(scores[h].astype(BF16), v_ref[rows, vsl[h]]) + inter[h]
            o_ref[rows, vsl[h]] = _gla_out(o, g_ref[...], r_ref[rows, vsl[h]])
        return carry

    lax.fori_loop(0, nchunks, chunk, 0)
    s_ref[0] = s_scr[...]


def _gla_prompt(qg, kg, vg, la, rg, gnorm, batch, seq):
    nlev, mall, role, mask = _gla_constants(GLA_CHUNK, GLA_CHUNK)
    hp = GLA_HEADS_PER_STEP
    bh = lambda b, h: (b, h)
    const = lambda b, h: (0, 0)
    return pl.pallas_call(
        functools.partial(_gla_prompt_kernel, nlev=nlev, nchunks=seq // GLA_CHUNK),
        grid=(batch, GLA_HEADS // hp),
        in_specs=[pl.BlockSpec((seq, hp * GLA_DK), bh), pl.BlockSpec((seq, hp * GLA_DK), bh),
                  pl.BlockSpec((seq, hp * GLA_DV), bh), pl.BlockSpec((seq, hp * GLA_DK), bh),
                  pl.BlockSpec((seq, hp * GLA_DV), bh),
                  pl.BlockSpec(mall.shape, const), pl.BlockSpec(role.shape, const), pl.BlockSpec(mask.shape, const),
                  pl.BlockSpec((1, GLA_DV), const)],
        out_specs=[pl.BlockSpec((seq, hp * GLA_DV), bh),
                   pl.BlockSpec((1, hp, GLA_DK, GLA_DV), lambda b, h: (b, h, 0, 0))],
        out_shape=[jax.ShapeDtypeStruct(vg.shape, BF16),
                   jax.ShapeDtypeStruct((batch, GLA_HEADS, GLA_DK, GLA_DV), F32)],
        scratch_shapes=[pltpu.VMEM((hp, GLA_DK, GLA_DV), F32)],
        compiler_params=_cparams(("parallel", "parallel")),
        name="gla_prompt",
    )(qg, kg, vg, la, rg, mall, role, mask, gnorm)


def _gla_sample_kernel(q_ref, k_ref, v_ref, la_ref, r_ref, s0_ref, mall_ref, role_ref, mask_ref, msum_ref, g_ref,
                       o_ref, s_ref, *, nlev, t_new):
    sb = s0_ref.shape[0]
    c = sb * t_new
    spv = 8 // t_new
    e2_all, hl_all = _gla_exponents(la_ref[...], mall_ref[...])
    seq_in_tile = lax.broadcasted_iota(jnp.int32, (8, GLA_DV), 0) // t_new
    seq_of_row = lax.broadcasted_iota(jnp.int32, (c, GLA_DV), 0) // t_new
    ksl = [slice(h * GLA_DK, (h + 1) * GLA_DK) for h in range(GLA_HEADS)]
    scores = [_gla_scores(q_ref[:, ksl[h]], k_ref[:, ksl[h]], e2_all[:, ksl[h]], role_ref, mask_ref, nlev, c)
              for h in range(GLA_HEADS)]
    for h in range(GLA_HEADS):
        ks = ksl[h]
        vs = slice(h * GLA_DV, (h + 1) * GLA_DV)
        v = v_ref[:, vs]
        e2 = e2_all[:, ks]
        hl = hl_all[:, ks]
        a, qf, kf = scores[h]
        qe = (qf * jnp.exp2(e2[nlev * c:(nlev + 1) * c])).astype(BF16)
        kt_t = jnp.transpose(kf * jnp.exp2(e2[(nlev + 1) * c:(nlev + 2) * c])).astype(BF16)
        bl_t = jnp.transpose(_dot(msum_ref[...], hl[:c]) + _dot(msum_ref[...], hl[c:]))
        dec_t = jnp.exp2(bl_t)
        inter = []
        for vr in range(c // 8):
            rows8 = qe[8 * vr:8 * vr + 8]
            tile = None
            for u in range(spv):
                j = vr * spv + u
                s0 = s0_ref[j, h]
                r = _dot(rows8, s0.astype(BF16))
                tile = r if tile is None else jnp.where(seq_in_tile == u, r, tile)
                vj = jnp.where(seq_of_row == j, v, jnp.zeros_like(v))
                s_ref[j, h] = dec_t[:, j:j + 1] * s0 + _dot(kt_t, vj)
            inter.append(tile)
        o = _dot(a.astype(BF16), v) + jnp.concatenate(inter, axis=0)
        o_ref[:, vs] = _gla_out(o, g_ref[...], r_ref[:, vs])


def _gla_sample(qg, kg, vg, la, rg, state, gnorm, batch, t_new):
    sb = SAMPLE_SEQ_BLOCK
    c = sb * t_new
    assert 8 % t_new == 0 and c % 8 == 0
    nlev, mall, role, mask = _gla_constants(c, t_new)
    msum = jnp.asarray((np.arange(c)[None, :] // t_new == np.arange(sb)[:, None]).astype(np.float32), BF16)
    rows = lambda i: (i, 0)
    const = lambda i: (0, 0)
    st = lambda i: (i, 0, 0, 0)
    return pl.pallas_call(
        functools.partial(_gla_sample_kernel, nlev=nlev, t_new=t_new),
        grid=(batch // sb,),
        in_specs=[pl.BlockSpec((c, GLA_KEY_WIDTH), rows), pl.BlockSpec((c, GLA_KEY_WIDTH), rows),
                  pl.BlockSpec((c, GLA_VALUE_WIDTH), rows), pl.BlockSpec((c, GLA_KEY_WIDTH), rows),
                  pl.BlockSpec((c, GLA_VALUE_WIDTH), rows),
                  pl.BlockSpec((sb, GLA_HEADS, GLA_DK, GLA_DV), st),
                  pl.BlockSpec(mall.shape, const), pl.BlockSpec(role.shape, const), pl.BlockSpec(mask.shape, const),
                  pl.BlockSpec(msum.shape, const), pl.BlockSpec((1, GLA_DV), const)],
        out_specs=[pl.BlockSpec((c, GLA_VALUE_WIDTH), rows), pl.BlockSpec((sb, GLA_HEADS, GLA_DK, GLA_DV), st)],
        out_shape=[jax.ShapeDtypeStruct(vg.shape, BF16), jax.ShapeDtypeStruct(state.shape, F32)],
        compiler_params=_cparams(("parallel",)),
        name="gla_sample",
    )(qg, kg, vg, la, rg, state, mall, role, mask, msum, gnorm)


def _post_kernel(x_ref, a_ref, gl_ref, ga_ref, gb_ref, wpa, wpg, wo, nf_ref, wr_hi, wr_lo, br, x1t_o, rt_o):
    pa = _dot(a_ref[...], wpa[...])
    pg = _dot(gl_ref[...], wpg[...])
    merged = ga_ref[...].astype(F32) * pa + gb_ref[...].astype(F32) * pg
    x1 = x_ref[...] + _dot(merged.astype(BF16), wo[...])
    tm = x1.shape[0]
    for j in range(TOKEN_ROWS):
        x1t_o[pl.ds(j, tm, stride=TOKEN_ROWS), :] = x1[:, j * LANES:(j + 1) * LANES]
    h2 = _rms(x1, nf_ref[...])
    h_hi, h_lo = _split_bf16(h2)
    lt = _dot_nt(wr_hi[...], h_hi) + _dot_nt(wr_hi[...], h_lo) + _dot_nt(wr_lo[...], h_hi)
    nrow = ROUTER_ROWS
    lt = lt[:nrow] + br[:nrow, 0:1]
    row = lax.broadcasted_iota(jnp.int32, lt.shape, 0)
    big = jnp.int32(LANES)
    ninf = -jnp.inf

    def first_max(vals):
        mx = jnp.max(vals, axis=0, keepdims=True)
        return mx, jnp.min(jnp.where(vals == mx, row, big), axis=0, keepdims=True)

    gl = jnp.where((row >= N_EXPERTS) & (row < N_EXPERTS + N_GROUPS), lt, ninf)
    gmax, gidx = first_max(gl)
    p_sel = 1.0 / jnp.sum(jnp.exp(gl - gmax), axis=0, keepdims=True)
    lo = (gidx - N_EXPERTS) * EXPERTS_PER_GROUP
    el = jnp.where((row >= lo) & (row < lo + EXPERTS_PER_GROUP), lt, ninf)
    v1, i1 = first_max(el)
    el2 = jnp.where(row == i1, ninf, el)
    v2, i2 = first_max(el2)
    t = jnp.exp(v2 - v1)
    w1 = p_sel / (1.0 + t)
    w2 = p_sel * t / (1.0 + t)
    row8 = lax.broadcasted_iota(jnp.int32, rt_o.shape, 0)
    pick = lambda k, val, rest: jnp.where(row8 == k, val, rest)
    rt_o[...] = pick(0, i1.astype(F32), pick(1, i2.astype(F32), pick(2, w1, pick(3, w2, 0.0))))


def _post(x2d, a_out, g_out, ga, gb, w, tm):
    n = x2d.shape[0]
    row = lambda i: (i, 0)
    const = lambda i: (0, 0)
    wnames = ("wpa", "wpg", "wo", "nf", "wr_hi", "wr_lo", "br")
    rt_rows = 8
    return pl.pallas_call(
        _post_kernel,
        grid=(n // tm,),
        in_specs=[pl.BlockSpec((tm, D_MODEL), row), pl.BlockSpec((tm, ATTN_WIDTH), row),
                  pl.BlockSpec((tm, GLA_VALUE_WIDTH), row), pl.BlockSpec((tm, D_MODEL), row),
                  pl.BlockSpec((tm, D_MODEL), row)] + [pl.BlockSpec(w[k].shape, const) for k in wnames],
        out_specs=[pl.BlockSpec((tm * TOKEN_ROWS, LANES), row),
                   pl.BlockSpec((rt_rows, tm), lambda i: (0, i))],
        out_shape=[jax.ShapeDtypeStruct((n * TOKEN_ROWS, LANES), F32), jax.ShapeDtypeStruct((rt_rows, n), F32)],
        compiler_params=_cparams(("parallel",)),
        name="post_mixer",
    )(x2d, a_out, g_out, ga, gb, *[w[k] for k in wnames])


def _moe_plan(rt, tme):
    n = rt.shape[1]
    ntiles = n // tme
    max_items = ntiles + N_GROUPS - 1
    i1, i2 = rt[0].astype(jnp.int32), rt[1].astype(jnp.int32)
    grp = i1 // EXPERTS_PER_GROUP
    lo = jnp.minimum(i1, i2) % EXPERTS_PER_GROUP
    hi = jnp.maximum(i1, i2) % EXPERTS_PER_GROUP
    key = (grp * EXPERTS_PER_GROUP + lo) * EXPERTS_PER_GROUP + hi
    order = jnp.argsort(key, stable=True).astype(jnp.int32)
    skey = key[order].reshape(ntiles, tme)
    sg = skey // (EXPERTS_PER_GROUP * EXPERTS_PER_GROUP)
    slo = (skey // EXPERTS_PER_GROUP) % EXPERTS_PER_GROUP
    shi = skey % EXPERTS_PER_GROUP
    ev = jnp.arange(EXPERTS_PER_GROUP)
    in_g = sg[:, :, None] == jnp.arange(N_GROUPS)
    uses_e = (slo[:, :, None] == ev) | (shi[:, :, None] == ev)
    flags_tge = jnp.any(in_g[:, :, :, None] & uses_e[:, :, None, :], axis=1)
    present = jnp.any(in_g, axis=1).reshape(-1)
    pos = jnp.cumsum(present) - 1
    n_items = pos[-1] + 1
    src = jnp.zeros((max_items,), jnp.int32).at[jnp.where(present, pos, max_items)].set(
        jnp.arange(ntiles * N_GROUPS, dtype=jnp.int32), mode="drop")
    it = jnp.arange(max_items)
    valid = it < n_items
    last_src = src[n_items - 1]
    src = jnp.where(valid, src, last_src)
    item_tile = src // N_GROUPS
    item_group = src % N_GROUPS
    prev_tile = jnp.concatenate([jnp.full((1,), -1, jnp.int32), item_tile[:-1]])
    next_tile = jnp.concatenate([item_tile[1:], jnp.full((1,), -1, jnp.int32)])
    first = valid & (item_tile != prev_tile)
    last = valid & ((item_tile != next_tile) | (it == n_items - 1))
    flags = flags_tge[item_tile, item_group] & valid[:, None]
    e0 = jnp.argmax(flags, axis=1)
    rest = flags & (ev[None, :] != e0[:, None])
    e1 = jnp.argmax(rest, axis=1)
    rest = rest & (ev[None, :] != e1[:, None])
    rt_sorted = rt[:, order].reshape(rt.shape[0], ntiles, tme).transpose(1, 0, 2)
    i32 = lambda z: z.astype(jnp.int32)
    plan = (order, i32(item_tile), i32(item_group), i32(first), i32(last), i32(valid), i32(e0), i32(e1),
            i32(rest.reshape(-1)))
    return plan, rt_sorted


def _moe_kernel(order, itile, igroup, ifirst, ilast, ivalid, ie0, ie1, flags, x_hbm, rt_ref, wg, wu, wd, nffn, nfin,
                y_hbm, xbuf, ybuf, acc, hbuf, rcol, gsem, ssem, *, tme, ntiles):
    i = pl.program_id(0)
    t = itile[i]
    slot = t % 2
    g = igroup[i]
    is_first = ifirst[i] == 1

    def gather_row(tile, sl, r):
        tok = order[tile * tme + r]
        src = x_hbm.at[pl.ds(pl.multiple_of(tok * TOKEN_ROWS, TOKEN_ROWS), TOKEN_ROWS)]
        dst = xbuf.at[pl.ds(pl.multiple_of((sl * tme + r) * TOKEN_ROWS, TOKEN_ROWS), TOKEN_ROWS)]
        return pltpu.make_async_copy(src, dst, gsem.at[sl])

    def scatter_row(tile, sl, r):
        tok = order[tile * tme + r]
        return pltpu.make_async_copy(ybuf.at[sl, pl.ds(r, 1)], y_hbm.at[pl.ds(tok, 1)], ssem.at[sl])

    def start_rows(make, tile, sl):
        def body(r, c):
            make(tile, sl, r).start()
            return c
        lax.fori_loop(0, tme, body, 0, unroll=8)

    def wait_gather(sl):
        rows = tme * TOKEN_ROWS
        pltpu.make_async_copy(x_hbm.at[pl.ds(0, rows)], xbuf.at[pl.ds(pl.multiple_of(sl * rows, rows), rows)],
                              gsem.at[sl]).wait()

    def wait_scatter(sl):
        pltpu.make_async_copy(ybuf.at[sl], y_hbm.at[pl.ds(0, tme)], ssem.at[sl]).wait()

    def start_rows_inline(make, tile, sl):
        for r in range(tme):
            make(tile, sl, r).start()

    def expert(e):
        eid = (g * EXPERTS_PER_GROUP + e).astype(F32)
        ce = (jnp.where(rcol[:, 0:1] == eid, rcol[:, 2:3], 0.0)
              + jnp.where(rcol[:, 1:2] == eid, rcol[:, 3:4], 0.0))
        h = hbuf[...]
        a = _dot(h, wg[e])
        u = _dot(h, wu[e])
        act = (a * _sigmoid(a)) * u * ce
        acc[...] += _dot(act.astype(BF16), wd[e])

    def expert_with(e, cond, side_work):
        @pl.when(cond)
        def _():
            side_work()
            expert(e)

        @pl.when(jnp.logical_and(jnp.logical_not(cond), ivalid[i] == 1))
        def _():
            expert(e)

    @pl.when(i == 0)
    def _():
        start_rows(gather_row, 0, 0)

    @pl.when(is_first)
    def _():
        wait_gather(slot)
        base = pl.multiple_of(slot * (tme * TOKEN_ROWS), tme * TOKEN_ROWS)
        x1 = jnp.concatenate([xbuf[pl.ds(base + j, tme, stride=TOKEN_ROWS), :] for j in range(TOKEN_ROWS)], axis=1)
        acc[...] = x1
        hbuf[...] = _rms(x1, nffn[...]).astype(BF16)
        rt = rt_ref[0]
        rcol[...] = jnp.transpose(jnp.concatenate([rt, jnp.zeros((LANES - rt.shape[0], tme), F32)], axis=0))

    expert_with(ie0[i], jnp.logical_and(is_first, t + 1 < ntiles),
                lambda: start_rows_inline(gather_row, t + 1, 1 - slot))
    expert_with(ie1[i], jnp.logical_and(is_first, t >= 1),
                lambda: start_rows_inline(scatter_row, t - 1, 1 - slot))
    for e in range(EXPERTS_PER_GROUP):
        @pl.when(flags[i * EXPERTS_PER_GROUP + e] == 1)
        def _():
            expert(e)

    @pl.when(ilast[i] == 1)
    def _():
        @pl.when(t >= 2)
        def _():
            wait_scatter(slot)

        ybuf[slot] = _rms(acc[...], nfin[...])

        @pl.when(t == ntiles - 1)
        def _():
            start_rows(scatter_row, t, slot)

    @pl.when(i == pl.num_programs(0) - 1)
    def _():
        for sl in range(min(2, ntiles)):
            wait_scatter(sl)


def _moe(x1t, rt, weg, weu, wed, nffn, nfin):
    n = x1t.shape[0] // TOKEN_ROWS
    tme = min(MOE_TILE, n)
    assert n % tme == 0
    ntiles = n // tme
    plan, rt_sorted = _moe_plan(rt, tme)
    max_items = ntiles + N_GROUPS - 1
    grp = lambda i, order, itile, igroup, *_: (igroup[i], 0, 0)
    til = lambda i, order, itile, *_: (itile[i], 0, 0)
    const = lambda i, *_: (0, 0)
    grid_spec = pltpu.PrefetchScalarGridSpec(
        num_scalar_prefetch=len(plan),
        grid=(max_items,),
        in_specs=[pl.BlockSpec(memory_space=pl.ANY),
                  pl.BlockSpec((1,) + rt_sorted.shape[1:], til),
                  pl.BlockSpec((EXPERTS_PER_GROUP, D_MODEL, EXPERT_FF), grp),
                  pl.BlockSpec((EXPERTS_PER_GROUP, D_MODEL, EXPERT_FF), grp),
                  pl.BlockSpec((EXPERTS_PER_GROUP, EXPERT_FF, D_MODEL), grp),
                  pl.BlockSpec((1, D_MODEL), const), pl.BlockSpec((1, D_MODEL), const)],
        out_specs=pl.BlockSpec(memory_space=pl.ANY),
        scratch_shapes=[pltpu.VMEM((2 * tme * TOKEN_ROWS, LANES), F32), pltpu.VMEM((2, tme, D_MODEL), F32),
                        pltpu.VMEM((tme, D_MODEL), F32), pltpu.VMEM((tme, D_MODEL), BF16),
                        pltpu.VMEM((tme, LANES), F32),
                        pltpu.SemaphoreType.DMA((2,)), pltpu.SemaphoreType.DMA((2,))],
    )
    return pl.pallas_call(
        functools.partial(_moe_kernel, tme=tme, ntiles=ntiles),
        grid_spec=grid_spec,
        out_shape=jax.ShapeDtypeStruct((n, D_MODEL), F32),
        compiler_params=_cparams(("arbitrary",)),
        name="moe",
    )(*plan, x1t, rt_sorted, weg, weu, wed, nffn, nfin)


def _rope_tables(positions):
    half = HEAD_DIM // 2
    inv_freq = ROPE_THETA ** (-jnp.arange(half, dtype=F32) / half)
    ang = positions.astype(F32)[:, None] * inv_freq[None, :]
    cos, sin = jnp.cos(ang), jnp.sin(ang)
    reps = LANES // HEAD_DIM
    return (jnp.tile(jnp.concatenate([cos, cos], -1), (1, reps)),
            jnp.tile(jnp.concatenate([-sin, sin], -1), (1, reps)))


def _prep_weights(norm_mix, w_in, w_gla_f2, b_gla_f, gla_norm, w_proj_attn, w_proj_gla, w_out, norm_ffn,
                  w_router_group, b_router_group, w_router_expert, b_router_expert):
    widths = (ATTN_WIDTH, 2 * KV_WIDTH, GLA_KEY_WIDTH, GLA_KEY_WIDTH, GLA_VALUE_WIDTH, GLA_VALUE_WIDTH,
              GLA_GATE_RANK, D_MODEL, D_MODEL)
    names = ("wqa", "wkva", "wqg", "wkg", "wvg", "wrg", "wf", "wga", "wgb")
    w, off = {}, 0
    for nm, wd in zip(names, widths):
        w[nm] = w_in[:, off:off + wd].astype(BF16)
        off += wd
    w["wf"] = jnp.pad(w["wf"], ((0, 0), (0, LANES - GLA_GATE_RANK)))
    w["wf2"] = jnp.pad(w_gla_f2.astype(BF16), ((0, LANES - GLA_GATE_RANK), (0, 0)))
    w["bf"] = b_gla_f.reshape(1, -1)
    w["norm_mix"] = norm_mix.reshape(1, -1)
    w["gla_norm"] = gla_norm.reshape(1, -1)
    w["wpa"] = w_proj_attn.astype(BF16)
    w["wpg"] = w_proj_gla.astype(BF16)
    w["wo"] = w_out.astype(BF16)
    w["nf"] = norm_ffn.reshape(1, -1)
    pad = LANES - N_EXPERTS - N_GROUPS
    wr_t = jnp.pad(jnp.concatenate([w_router_expert, w_router_group], axis=1), ((0, 0), (0, pad))).T
    w["wr_hi"], w["wr_lo"] = _split_bf16(wr_t)
    w["br"] = jnp.broadcast_to(jnp.pad(jnp.concatenate([b_router_expert, b_router_group]), (0, pad))[:, None],
                               (LANES, LANES))
    return w


def _layer(x, positions_tab, cache, w, sinks, weg, weu, wed, nfin, tm):
    batch, seq, _ = x.shape
    n = batch * seq
    x2d = x.reshape(n, D_MODEL)
    cos_t, sin_t = positions_tab
    qa, ka, va, qg, kg, vg, rg, la, ga, gb = _inproj(x2d, w["norm_mix"], cos_t, sin_t, w, tm)
    if cache is None:
        a_out = _swa_prompt(sinks, qa, ka, va, batch, seq)
        last = lambda z: z.reshape(batch, seq, KV_WIDTH)[:, seq - WINDOW:].reshape(batch, WINDOW, ATTN_KV_HEADS, HEAD_DIM)
        new_k, new_v = last(ka), last(va)
        g_out, new_s = _gla_prompt(qg, kg, vg, la, rg, w["gla_norm"], batch, seq)
    else:
        cache_k, cache_v, state = cache
        a_out, new_k, new_v = _swa_sample(sinks, qa, ka, va, cache_k.reshape(batch, WINDOW, KV_WIDTH),
                                          cache_v.reshape(batch, WINDOW, KV_WIDTH), batch, seq)
        new_k = new_k.reshape(batch, WINDOW, ATTN_KV_HEADS, HEAD_DIM)
        new_v = new_v.reshape(batch, WINDOW, ATTN_KV_HEADS, HEAD_DIM)
        g_out, new_s = _gla_sample(qg, kg, vg, la, rg, state, w["gla_norm"], batch, seq)
    x1t, rt = _post(x2d, a_out, g_out, ga, gb, w, tm)
    y = _moe(x1t, rt, weg, weu, wed, w["nf"], nfin)
    return y.reshape(batch, seq, D_MODEL), new_k, new_v, new_s


def kernel(x_prompt, x_sample, cache_win_k, cache_win_v, state_gla, norm_mix, w_in, w_gla_f2, b_gla_f, gla_norm,
           attn_sinks, w_proj_attn, w_proj_gla, w_out, norm_ffn, w_router_group, b_router_group, w_router_expert,
           b_router_expert, w_exp_gate, w_exp_up, w_exp_down, norm_final):
    assert norm_mix.shape[0] == 1, "single-layer step"
    seq_p = x_prompt.shape[1]
    dec_b, dec_t = x_sample.shape[0], x_sample.shape[1]
    w = _prep_weights(norm_mix[0], w_in[0], w_gla_f2[0], b_gla_f[0], gla_norm[0], w_proj_attn[0], w_proj_gla[0],
                      w_out[0], norm_ffn[0], w_router_group[0], b_router_group[0], w_router_expert[0],
                      b_router_expert[0])
    weg = w_exp_gate[0].astype(BF16)
    weu = w_exp_up[0].astype(BF16)
    wed = w_exp_down[0].astype(BF16)
    nfin = norm_final.reshape(1, -1)
    sinks = attn_sinks[0]
    tab_p = _rope_tables(jnp.arange(seq_p, dtype=jnp.int32))
    pos_s = PAST_LEN + jnp.arange(dec_t, dtype=jnp.int32)
    tab_s = tuple(jnp.tile(t, (dec_b, 1)) for t in _rope_tables(pos_s))
    tm_p = min(512, seq_p)
    tm_s = dec_b * dec_t
    yp, pk, pv, ps = _layer(x_prompt, tab_p, None, w, sinks, weg, weu, wed, nfin, tm_p)
    ys, sk, sv, ss = _layer(x_sample, tab_s, (cache_win_k[0], cache_win_v[0], state_gla[0]), w, sinks, weg, weu, wed,
                            nfin, tm_s)
    return (yp, ys, pk[None], pv[None], ps[None], sk[None], sv[None], ss[None])
```

```python
import functools
import math

import numpy as np
import jax
import jax.numpy as jnp
from jax import lax
from jax.experimental import pallas as pl
from jax.experimental.pallas import tpu as pltpu

F32 = jnp.float32
BF16 = jnp.bfloat16

D_MODEL = 1024
ATTN_HEADS = 8
ATTN_KV_HEADS = 2
GROUP = ATTN_HEADS // ATTN_KV_HEADS
HEAD_DIM = 64
ATTN_WIDTH = ATTN_HEADS * HEAD_DIM
KV_WIDTH = ATTN_KV_HEADS * HEAD_DIM
WINDOW = 128
ROPE_THETA = 10000.0
PAST_LEN = 8192
GLA_HEADS = 4
GLA_KEY_WIDTH = D_MODEL // 2
GLA_VALUE_WIDTH = D_MODEL
GLA_DK = GLA_KEY_WIDTH // GLA_HEADS
GLA_DV = GLA_VALUE_WIDTH // GLA_HEADS
GLA_GATE_RANK = 16
GLA_GATE_NORMALIZER = 16.0
N_GROUPS = 4
EXPERTS_PER_GROUP = 8
N_EXPERTS = N_GROUPS * EXPERTS_PER_GROUP
EXPERT_FF = 256
EPS = 1e-6
LOG2_E = 1.4426950408889634

LANES = 128
GLA_CHUNK = 128
GLA_HEADS_PER_STEP = 4
SWA_BLOCKS_PER_STEP = 4
SAMPLE_SEQ_BLOCK = 8
MOE_TILE = 256
TOKEN_ROWS = D_MODEL // LANES
INPROJ_NAMES = ("wqa", "wkva", "wqg", "wkg", "wvg", "wrg", "wga", "wgb", "wf")
INPROJ_WIDTHS = (ATTN_WIDTH, 2 * KV_WIDTH, GLA_KEY_WIDTH, GLA_KEY_WIDTH, GLA_VALUE_WIDTH, GLA_VALUE_WIDTH,
                 D_MODEL, D_MODEL, LANES)
ROUTER_ROWS = 40
VMEM_LIMIT = 56 * 1024 * 1024


def _cparams(sem):
    return pltpu.CompilerParams(dimension_semantics=sem, vmem_limit_bytes=VMEM_LIMIT)


def _rms(x, g):
    return x * lax.rsqrt(jnp.mean(x * x, axis=-1, keepdims=True) + EPS) * g


def _sigmoid(x):
    return 1.0 / (1.0 + jnp.exp(-x))


def _dot(a, b):
    return jnp.dot(a, b, preferred_element_type=F32)


def _dot_nt(a, b):
    return lax.dot_general(a, b, (((1,), (1,)), ((), ())), preferred_element_type=F32)


def _dot_tn(a, b):
    return lax.dot_general(a, b, (((0,), (0,)), ((), ())), preferred_element_type=F32)


def _split_bf16(x):
    hi = x.astype(BF16)
    lo = (x - hi.astype(F32)).astype(BF16)
    return hi, lo


def _inproj_kernel(x_ref, g_ref, cos_ref, sin_ref, win, wf2, bf,
                   qa_o, ka_o, va_o, qg_o, kg_o, vg_o, rg_o, la_o, ga_o, gb_o):
    hb = _rms(x_ref[...], g_ref[...]).astype(BF16)
    cols = dict(zip(INPROJ_NAMES, np.cumsum((0,) + INPROJ_WIDTHS[:-1])))
    wqa, wkva, wqg, wkg, wvg, wrg, wga, wgb, wf = (
        win.at[:, int(cols[k]):int(cols[k]) + wd] for k, wd in zip(INPROJ_NAMES, INPROJ_WIDTHS))
    cos = cos_ref[...]
    sin = sin_ref[...]
    lane = lax.broadcasted_iota(jnp.int32, cos.shape, 1)
    first_half = (lane % HEAD_DIM) < (HEAD_DIM // 2)

    def rope(t):
        swapped = jnp.where(first_half, pltpu.roll(t, LANES - HEAD_DIM // 2, 1), pltpu.roll(t, HEAD_DIM // 2, 1))
        return t * cos + swapped * sin

    qa = _dot(hb, wqa[...])
    for c in range(ATTN_WIDTH // LANES):
        sl = slice(c * LANES, (c + 1) * LANES)
        qa_o[:, sl] = (rope(qa[:, sl]) * (HEAD_DIM ** -0.5)).astype(BF16)
    kva = _dot(hb, wkva[...])
    ka_o[...] = rope(kva[:, :KV_WIDTH])
    va_o[...] = kva[:, KV_WIDTH:]
    qg_o[...] = (_dot(hb, wqg[...]) * (GLA_DK ** -0.5)).astype(BF16)
    kg_o[...] = _dot(hb, wkg[...]).astype(BF16)
    vg_o[...] = _dot(hb, wvg[...]).astype(BF16)
    rg_o[...] = _dot(hb, wrg[...]).astype(BF16)
    z = _dot(_dot(hb, wf[...]).astype(BF16), wf2[...]) + bf[...]
    la_o[...] = (jnp.minimum(z, 0.0) - jnp.log1p(jnp.exp(-jnp.abs(z)))) * (1.0 / GLA_GATE_NORMALIZER)
    ga_o[...] = _sigmoid(_dot(hb, wga[...])).astype(BF16)
    gb_o[...] = _sigmoid(_dot(hb, wgb[...])).astype(BF16)


def _inproj(x2d, norm_g, cos_t, sin_t, w, tm):
    n = x2d.shape[0]
    ntab = cos_t.shape[0] // tm
    row = lambda i: (i, 0)
    const = lambda i: (0, 0)
    tab = lambda i: (i % ntab, 0)
    wnames = ("win", "wf2", "bf")
    wspecs = [pl.BlockSpec(w[k].shape, const) for k in wnames]
    widths = (ATTN_WIDTH, KV_WIDTH, KV_WIDTH, GLA_KEY_WIDTH, GLA_KEY_WIDTH, GLA_VALUE_WIDTH, GLA_VALUE_WIDTH,
              GLA_KEY_WIDTH, D_MODEL, D_MODEL)
    dtypes = (BF16, F32, F32, BF16, BF16, BF16, BF16, F32, BF16, BF16)
    return pl.pallas_call(
        _inproj_kernel,
        grid=(n // tm,),
        in_specs=[pl.BlockSpec((tm, D_MODEL), row), pl.BlockSpec((1, D_MODEL), const),
                  pl.BlockSpec((tm, LANES), tab), pl.BlockSpec((tm, LANES), tab)] + wspecs,
        out_specs=[pl.BlockSpec((tm, wd), row) for wd in widths],
        out_shape=[jax.ShapeDtypeStruct((n, wd), dt) for wd, dt in zip(widths, dtypes)],
        compiler_params=_cparams(("parallel",)),
        name="inproj",
    )(x2d, norm_g, cos_t, sin_t, *[w[k] for k in wnames])


def _swa_prompt_kernel(sink_ref, q_ref, kc_ref, kp_ref, vc_ref, vp_ref, o_ref, *, qb):
    n = pl.program_id(1)
    k3 = jnp.concatenate([kp_ref[...], kc_ref[...]], axis=0).astype(BF16)
    v3 = jnp.concatenate([vp_ref[...], vc_ref[...]], axis=0).astype(BF16)
    t = lax.broadcasted_iota(jnp.int32, (WINDOW, 2 * WINDOW), 0)
    j = lax.broadcasted_iota(jnp.int32, (WINDOW, 2 * WINDOW), 1)
    band = (j >= t) & (j <= t + WINDOW)
    zeros = jnp.zeros((2 * WINDOW, HEAD_DIM), BF16)
    ones = jnp.ones((2 * WINDOW, HEAD_DIM), BF16)
    lane = lax.broadcasted_iota(jnp.int32, (WINDOW, LANES), 1)
    chains = [(blk, h) for blk in range(qb) for h in range(ATTN_HEADS)]
    scores = []
    for blk, h in chains:
        kv = h // GROUP
        rows = slice(blk * WINDOW, (blk + 1) * WINDOW)
        keys = slice(blk * WINDOW, (blk + 2) * WINDOW)
        s = _dot_nt(q_ref[rows, h * HEAD_DIM:(h + 1) * HEAD_DIM], k3[keys, kv * HEAD_DIM:(kv + 1) * HEAD_DIM])
        valid = band & ((j >= WINDOW) | (n > 0)) if blk == 0 else band
        scores.append(jnp.where(valid, s, -jnp.inf))
    probs, sink_terms = [], []
    for (blk, h), s in zip(chains, scores):
        m = jnp.maximum(jnp.max(s, axis=-1, keepdims=True), sink_ref[h])
        probs.append(jnp.exp(s - m).astype(BF16))
        sink_terms.append(jnp.exp(sink_ref[h] - m))
    for blk in range(qb):
        rows = slice(blk * WINDOW, (blk + 1) * WINDOW)
        keys = slice(blk * WINDOW, (blk + 2) * WINDOW)
        for kv in range(ATTN_KV_HEADS):
            vv = v3[keys, kv * HEAD_DIM:(kv + 1) * HEAD_DIM]
            vext = (jnp.concatenate([vv, zeros, ones, zeros], axis=1), jnp.concatenate([zeros, vv, zeros, ones], axis=1))
            for pr in range(GROUP // 2):
                h0 = kv * GROUP + pr * 2
                c0 = blk * ATTN_HEADS + h0
                acc = _dot(probs[c0], vext[0]) + _dot(probs[c0 + 1], vext[1])
                l = acc[:, LANES:] + jnp.where(lane < HEAD_DIM, sink_terms[c0], sink_terms[c0 + 1])
                o_ref[rows, h0 * HEAD_DIM:h0 * HEAD_DIM + LANES] = (acc[:, :LANES] / l).astype(BF16)


def _swa_prompt(sinks, qa, ka, va, batch, seq):
    nb = seq // WINDOW
    qb = math.gcd(SWA_BLOCKS_PER_STEP, nb)
    steps = nb // qb
    cur = lambda b, n: (b * steps + n, 0)
    prev = lambda b, n: (b * nb + jnp.maximum(n * qb - 1, 0), 0)
    return pl.pallas_call(
        functools.partial(_swa_prompt_kernel, qb=qb),
        grid=(batch, steps),
        in_specs=[pl.BlockSpec(memory_space=pltpu.SMEM),
                  pl.BlockSpec((qb * WINDOW, ATTN_WIDTH), cur),
                  pl.BlockSpec((qb * WINDOW, KV_WIDTH), cur), pl.BlockSpec((WINDOW, KV_WIDTH), prev),
                  pl.BlockSpec((qb * WINDOW, KV_WIDTH), cur), pl.BlockSpec((WINDOW, KV_WIDTH), prev)],
        out_specs=pl.BlockSpec((qb * WINDOW, ATTN_WIDTH), cur),
        out_shape=jax.ShapeDtypeStruct(qa.shape, BF16),
        compiler_params=_cparams(("parallel", "parallel")),
        name="swa_prompt",
    )(sinks, qa, ka, ka, va, va)


def _swa_sample_kernel(sink_ref, q_ref, kn_ref, vn_ref, kn3_ref, vn3_ref, ck_ref, cv_ref, o_ref, nk_ref, nv_ref,
                       *, t_new):
    sb = ck_ref.shape[0]
    spv = 8 // t_new
    nq = GROUP * 8
    nc = spv * WINDOW
    qi = lax.broadcasted_iota(jnp.int32, (nq, nc), 0) % 8
    ci = lax.broadcasted_iota(jnp.int32, (nq, nc), 1)
    valid_c = (qi // t_new == ci // WINDOW) & (ci % WINDOW >= qi % t_new)
    qn = lax.broadcasted_iota(jnp.int32, (nq, 8), 0) % 8
    cn = lax.broadcasted_iota(jnp.int32, (nq, 8), 1)
    valid_n = (qn // t_new == cn // t_new) & (cn <= qn)
    grow = lax.broadcasted_iota(jnp.int32, (nq, 1), 0) // 8
    chains = [(vr, kv) for vr in range(sb // spv) for kv in range(ATTN_KV_HEADS)]
    scored = []
    for vr, kv in chains:
        r8 = slice(8 * vr, 8 * vr + 8)
        cs = slice(kv * HEAD_DIM, (kv + 1) * HEAD_DIM)
        heads = [kv * GROUP + g for g in range(GROUP)]
        qs = jnp.concatenate([q_ref[r8, h * HEAD_DIM:(h + 1) * HEAD_DIM] for h in heads], axis=0)
        ck = ck_ref[vr * spv:(vr + 1) * spv, :, cs].reshape(nc, HEAD_DIM).astype(BF16)
        kn = kn_ref[r8, cs].astype(BF16)
        scored.append((jnp.where(valid_c, _dot_nt(qs, ck), -jnp.inf), jnp.where(valid_n, _dot_nt(qs, kn), -jnp.inf)))
    soft = []
    for (vr, kv), (s_c, s_n) in zip(chains, scored):
        sink = sink_ref[kv * GROUP]
        for g in range(1, GROUP):
            sink = jnp.where(grow == g, sink_ref[kv * GROUP + g], sink)
        m = jnp.maximum(jnp.maximum(jnp.max(s_c, axis=-1, keepdims=True), jnp.max(s_n, axis=-1, keepdims=True)), sink)
        p_c = jnp.exp(s_c - m)
        p_n = jnp.exp(s_n - m)
        l = jnp.sum(p_c, axis=-1, keepdims=True) + jnp.sum(p_n, axis=-1, keepdims=True) + jnp.exp(sink - m)
        soft.append((p_c.astype(BF16), p_n.astype(BF16), l))
    for (vr, kv), (p_c, p_n, l) in zip(chains, soft):
        r8 = slice(8 * vr, 8 * vr + 8)
        cs = slice(kv * HEAD_DIM, (kv + 1) * HEAD_DIM)
        cv = cv_ref[vr * spv:(vr + 1) * spv, :, cs].reshape(nc, HEAD_DIM).astype(BF16)
        o = (_dot(p_c, cv) + _dot(p_n, vn_ref[r8, cs].astype(BF16))) / l
        for a in range(GROUP // 2):
            pair = jnp.concatenate([o[16 * a:16 * a + 8], o[16 * a + 8:16 * a + 16]], axis=1)
            c0 = (kv * GROUP + 2 * a) * HEAD_DIM
            o_ref[r8, c0:c0 + LANES] = pair.astype(BF16)
    nk_ref[:, 0:WINDOW - t_new, :] = ck_ref[:, t_new:WINDOW, :]
    nk_ref[:, WINDOW - t_new:WINDOW, :] = kn3_ref[...]
    nv_ref[:, 0:WINDOW - t_new, :] = cv_ref[:, t_new:WINDOW, :]
    nv_ref[:, WINDOW - t_new:WINDOW, :] = vn3_ref[...]


def _swa_sample(sinks, qa, ka, va, cache_k, cache_v, batch, t_new):
    sb = SAMPLE_SEQ_BLOCK
    rows = sb * t_new
    r2 = lambda i: (i, 0)
    r3 = lambda i: (i, 0, 0)
    ka3 = ka.reshape(batch, t_new, KV_WIDTH)
    va3 = va.reshape(batch, t_new, KV_WIDTH)
    return pl.pallas_call(
        functools.partial(_swa_sample_kernel, t_new=t_new),
        grid=(batch // sb,),
        in_specs=[pl.BlockSpec(memory_space=pltpu.SMEM),
                  pl.BlockSpec((rows, ATTN_WIDTH), r2),
                  pl.BlockSpec((rows, KV_WIDTH), r2), pl.BlockSpec((rows, KV_WIDTH), r2),
                  pl.BlockSpec((sb, t_new, KV_WIDTH), r3), pl.BlockSpec((sb, t_new, KV_WIDTH), r3),
                  pl.BlockSpec((sb, WINDOW, KV_WIDTH), r3), pl.BlockSpec((sb, WINDOW, KV_WIDTH), r3)],
        out_specs=[pl.BlockSpec((rows, ATTN_WIDTH), r2),
                   pl.BlockSpec((sb, WINDOW, KV_WIDTH), r3), pl.BlockSpec((sb, WINDOW, KV_WIDTH), r3)],
        out_shape=[jax.ShapeDtypeStruct(qa.shape, BF16),
                   jax.ShapeDtypeStruct(cache_k.shape, F32), jax.ShapeDtypeStruct(cache_v.shape, F32)],
        compiler_params=_cparams(("parallel",)),
        name="swa_sample",
    )(sinks, qa, ka, va, ka3, va3, cache_k, cache_v)


def _gla_constants(c, seg):
    t = np.arange(c)
    sid = t // seg
    same = sid[:, None] == sid[None, :]
    levels = []
    m = seg // 2
    while m >= 1:
        levels.append(m)
        m //= 2
    mats, roles, masks = [], [], []
    for m in levels:
        blk = t // (2 * m)
        second = (t // m) % 2 == 1
        p = blk * 2 * m + m - 1
        u = t[None, :]
        mq = (u > p[:, None]) & (u <= t[:, None])
        mk = (u > t[:, None]) & (u <= p[:, None])
        mats.append(np.where(second[:, None], mq, mk))
        roles.append(np.broadcast_to(second[:, None], (c, LANES)))
        masks.append((blk[:, None] == blk[None, :]) & second[:, None] & ~second[None, :])
    masks.append(np.eye(c, dtype=bool))
    mats.append(same & (t[None, :] <= t[:, None]))
    mats.append(same & (t[None, :] > t[:, None]))
    mall = np.concatenate(mats, 0).astype(np.float32)
    mall = jnp.asarray(np.concatenate([mall, mall], 1), BF16)
    role = jnp.asarray(np.concatenate(roles, 0).astype(np.float32))
    mask = jnp.asarray(np.concatenate(masks, 0).astype(np.float32))
    return len(levels), mall, role, mask


def _gla_exponents(la, mall):
    hl = jnp.concatenate(_split_bf16(la * LOG2_E), axis=0)
    return _dot(mall, hl), hl


def _gla_scores(qb, kb, e2, role_ref, mask_ref, nlev, c):
    qf = qb.astype(F32)
    kf = kb.astype(F32)
    terms = []
    for lv in range(nlev):
        sl = slice(lv * c, (lv + 1) * c)
        x = (jnp.where(role_ref[sl, :] > 0.5, qf, kf) * jnp.exp2(e2[sl])).astype(BF16)
        terms.append((x, x, lv))
    terms.append((qb, kb, nlev))
    mask = lambda i: mask_ref[i * c:(i + 1) * c, :]
    a = None
    pair = c % LANES == 0
    while terms:
        if pair and len(terms) >= 2:
            (l0, r0, i0), (l1, r1, i1) = terms.pop(), terms.pop()
            z = jnp.zeros_like(r0)
            rhs = jnp.concatenate([jnp.concatenate([r0, z], axis=1), jnp.concatenate([z, r1], axis=1)], axis=0)
            g = _dot_nt(jnp.concatenate([l0, l1], axis=1), rhs)
            t = mask(i0) * g[:, :c] + mask(i1) * g[:, c:]
        else:
            l0, r0, i0 = terms.pop()
            t = mask(i0) * _dot_nt(l0, r0)
        a = t if a is None else a + t
    return a, qf, kf


def _gla_out(o, g, r):
    r = r.astype(F32)
    return (_rms(o, g) * (r * _sigmoid(r))).astype(BF16)


def _gla_prompt_kernel(q_ref, k_ref, v_ref, la_ref, r_ref, mall_ref, role_ref, mask_ref, g_ref, o_ref, s_ref,
                       s_scr, *, nlev, nchunks):
    c = GLA_CHUNK
    hp = GLA_HEADS_PER_STEP
    s_scr[...] = jnp.zeros_like(s_scr)
    ones = jnp.ones((2 * c, LANES), BF16)

    def chunk(i, carry):
        rows = pl.ds(pl.multiple_of(i * c, c), c)
        e2_all, hl_all = _gla_exponents(la_ref[rows, :], mall_ref[...])
        ksl = [slice(h * GLA_DK, (h + 1) * GLA_DK) for h in range(hp)]
        vsl = [slice(h * GLA_DV, (h + 1) * GLA_DV) for h in range(hp)]
        inter = []
        for h in range(hp):
            e2 = e2_all[:, ksl[h]]
            v = v_ref[rows, vsl[h]]
            qf = q_ref[rows, ksl[h]].astype(F32)
            kf = k_ref[rows, ksl[h]].astype(F32)
            s = s_scr[h]
            inter.append(_dot((qf * jnp.exp2(e2[nlev * c:(nlev + 1) * c])).astype(BF16), s.astype(BF16)))
            kt = (kf * jnp.exp2(e2[(nlev + 1) * c:(nlev + 2) * c])).astype(BF16)
            dec = jnp.exp2(_dot_tn(hl_all[:, ksl[h]], ones))
            s_scr[h] = jnp.concatenate([dec] * (GLA_DV // LANES), axis=1) * s + _dot_tn(kt, v)
        scores = [_gla_scores(q_ref[rows, ksl[h]], k_ref[rows, ksl[h]], e2_all[:, ksl[h]], role_ref, mask_ref, nlev, c)[0]
                  for h in range(hp)]
        for h in range(hp):
            o = _dot(scores[h].astype(BF16), v_ref[rows, vsl[h]]) + inter[h]
            o_ref[rows, vsl[h]] = _gla_out(o, g_ref[...], r_ref[rows, vsl[h]])
        return carry

    lax.fori_loop(0, nchunks, chunk, 0)
    s_ref[0] = s_scr[...]


def _gla_prompt(qg, kg, vg, la, rg, gnorm, batch, seq):
    nlev, mall, role, mask = _gla_constants(GLA_CHUNK, GLA_CHUNK)
    hp = GLA_HEADS_PER_STEP
    bh = lambda b, h: (b, h)
    const = lambda b, h: (0, 0)
    return pl.pallas_call(
        functools.partial(_gla_prompt_kernel, nlev=nlev, nchunks=seq // GLA_CHUNK),
        grid=(batch, GLA_HEADS // hp),
        in_specs=[pl.BlockSpec((seq, hp * GLA_DK), bh), pl.BlockSpec((seq, hp * GLA_DK), bh),
                  pl.BlockSpec((seq, hp * GLA_DV), bh), pl.BlockSpec((seq, hp * GLA_DK), bh),
                  pl.BlockSpec((seq, hp * GLA_DV), bh),
                  pl.BlockSpec(mall.shape, const), pl.BlockSpec(role.shape, const), pl.BlockSpec(mask.shape, const),
                  pl.BlockSpec((1, GLA_DV), const)],
        out_specs=[pl.BlockSpec((seq, hp * GLA_DV), bh),
                   pl.BlockSpec((1, hp, GLA_DK, GLA_DV), lambda b, h: (b, h, 0, 0))],
        out_shape=[jax.ShapeDtypeStruct(vg.shape, BF16),
                   jax.ShapeDtypeStruct((batch, GLA_HEADS, GLA_DK, GLA_DV), F32)],
        scratch_shapes=[pltpu.VMEM((hp, GLA_DK, GLA_DV), F32)],
        compiler_params=_cparams(("parallel", "parallel")),
        name="gla_prompt",
    )(qg, kg, vg, la, rg, mall, role, mask, gnorm)


def _gla_sample_kernel(q_ref, k_ref, v_ref, la_ref, r_ref, s0_ref, mall_ref, role_ref, mask_ref, msum_ref, g_ref,
                       o_ref, s_ref, *, nlev, t_new):
    sb = s0_ref.shape[0]
    c = sb * t_new
    spv = 8 // t_new
    e2_all, hl_all = _gla_exponents(la_ref[...], mall_ref[...])
    seq_in_tile = lax.broadcasted_iota(jnp.int32, (8, GLA_DV), 0) // t_new
    seq_of_row = lax.broadcasted_iota(jnp.int32, (c, GLA_DV), 0) // t_new
    ksl = [slice(h * GLA_DK, (h + 1) * GLA_DK) for h in range(GLA_HEADS)]
    scores = [_gla_scores(q_ref[:, ksl[h]], k_ref[:, ksl[h]], e2_all[:, ksl[h]], role_ref, mask_ref, nlev, c)
              for h in range(GLA_HEADS)]
    for h in range(GLA_HEADS):
        ks = ksl[h]
        vs = slice(h * GLA_DV, (h + 1) * GLA_DV)
        v = v_ref[:, vs]
        e2 = e2_all[:, ks]
        hl = hl_all[:, ks]
        a, qf, kf = scores[h]
        qe = (qf * jnp.exp2(e2[nlev * c:(nlev + 1) * c])).astype(BF16)
        kt_t = jnp.transpose(kf * jnp.exp2(e2[(nlev + 1) * c:(nlev + 2) * c])).astype(BF16)
        bl_t = jnp.transpose(_dot(msum_ref[...], hl[:c]) + _dot(msum_ref[...], hl[c:]))
        dec_t = jnp.exp2(bl_t)
        inter = []
        for vr in range(c // 8):
            rows8 = qe[8 * vr:8 * vr + 8]
            tile = None
            for u in range(spv):
                j = vr * spv + u
                s0 = s0_ref[j, h]
                r = _dot(rows8, s0.astype(BF16))
                tile = r if tile is None else jnp.where(seq_in_tile == u, r, tile)
                vj = jnp.where(seq_of_row == j, v, jnp.zeros_like(v))
                s_ref[j, h] = dec_t[:, j:j + 1] * s0 + _dot(kt_t, vj)
            inter.append(tile)
        o = _dot(a.astype(BF16), v) + jnp.concatenate(inter, axis=0)
        o_ref[:, vs] = _gla_out(o, g_ref[...], r_ref[:, vs])


def _gla_sample(qg, kg, vg, la, rg, state, gnorm, batch, t_new):
    sb = SAMPLE_SEQ_BLOCK
    c = sb * t_new
    assert 8 % t_new == 0 and c % 8 == 0
    nlev, mall, role, mask = _gla_constants(c, t_new)
    msum = jnp.asarray((np.arange(c)[None, :] // t_new == np.arange(sb)[:, None]).astype(np.float32), BF16)
    rows = lambda i: (i, 0)
    const = lambda i: (0, 0)
    st = lambda i: (i, 0, 0, 0)
    return pl.pallas_call(
        functools.partial(_gla_sample_kernel, nlev=nlev, t_new=t_new),
        grid=(batch // sb,),
        in_specs=[pl.BlockSpec((c, GLA_KEY_WIDTH), rows), pl.BlockSpec((c, GLA_KEY_WIDTH), rows),
                  pl.BlockSpec((c, GLA_VALUE_WIDTH), rows), pl.BlockSpec((c, GLA_KEY_WIDTH), rows),
                  pl.BlockSpec((c, GLA_VALUE_WIDTH), rows),
                  pl.BlockSpec((sb, GLA_HEADS, GLA_DK, GLA_DV), st),
                  pl.BlockSpec(mall.shape, const), pl.BlockSpec(role.shape, const), pl.BlockSpec(mask.shape, const),
                  pl.BlockSpec(msum.shape, const), pl.BlockSpec((1, GLA_DV), const)],
        out_specs=[pl.BlockSpec((c, GLA_VALUE_WIDTH), rows), pl.BlockSpec((sb, GLA_HEADS, GLA_DK, GLA_DV), st)],
        out_shape=[jax.ShapeDtypeStruct(vg.shape, BF16), jax.ShapeDtypeStruct(state.shape, F32)],
        compiler_params=_cparams(("parallel",)),
        name="gla_sample",
    )(qg, kg, vg, la, rg, state, mall, role, mask, msum, gnorm)


def _post_kernel(x_ref, a_ref, gl_ref, ga_ref, gb_ref, wpa, wpg, wo, nf_ref, wr_hi, wr_lo, br, x1t_o, rt_o):
    pa = _dot(a_ref[...], wpa[...])
    pg = _dot(gl_ref[...], wpg[...])
    merged = ga_ref[...].astype(F32) * pa + gb_ref[...].astype(F32) * pg
    x1 = x_ref[...] + _dot(merged.astype(BF16), wo[...])
    tm = x1.shape[0]
    for j in range(TOKEN_ROWS):
        x1t_o[pl.ds(j, tm, stride=TOKEN_ROWS), :] = x1[:, j * LANES:(j + 1) * LANES]
    h2 = _rms(x1, nf_ref[...])
    h_hi, h_lo = _split_bf16(h2)
    lt = _dot_nt(wr_hi[...], h_hi) + _dot_nt(wr_hi[...], h_lo) + _dot_nt(wr_lo[...], h_hi)
    nrow = ROUTER_ROWS
    lt = lt[:nrow] + br[:nrow, 0:1]
    row = lax.broadcasted_iota(jnp.int32, lt.shape, 0)
    big = jnp.int32(LANES)
    ninf = -jnp.inf

    def first_max(vals):
        mx = jnp.max(vals, axis=0, keepdims=True)
        return mx, jnp.min(jnp.where(vals == mx, row, big), axis=0, keepdims=True)

    gl = jnp.where((row >= N_EXPERTS) & (row < N_EXPERTS + N_GROUPS), lt, ninf)
    gmax, gidx = first_max(gl)
    p_sel = 1.0 / jnp.sum(jnp.exp(gl - gmax), axis=0, keepdims=True)
    lo = (gidx - N_EXPERTS) * EXPERTS_PER_GROUP
    el = jnp.where((row >= lo) & (row < lo + EXPERTS_PER_GROUP), lt, ninf)
    v1, i1 = first_max(el)
    el2 = jnp.where(row == i1, ninf, el)
    v2, i2 = first_max(el2)
    t = jnp.exp(v2 - v1)
    w1 = p_sel / (1.0 + t)
    w2 = p_sel * t / (1.0 + t)
    row8 = lax.broadcasted_iota(jnp.int32, rt_o.shape, 0)
    pick = lambda k, val, rest: jnp.where(row8 == k, val, rest)
    rt_o[...] = pick(0, i1.astype(F32), pick(1, i2.astype(F32), pick(2, w1, pick(3, w2, 0.0))))


def _post(x2d, a_out, g_out, ga, gb, w, tm):
    n = x2d.shape[0]
    row = lambda i: (i, 0)
    const = lambda i: (0, 0)
    wnames = ("wpa", "wpg", "wo", "nf", "wr_hi", "wr_lo", "br")
    rt_rows = 8
    return pl.pallas_call(
        _post_kernel,
        grid=(n // tm,),
        in_specs=[pl.BlockSpec((tm, D_MODEL), row), pl.BlockSpec((tm, ATTN_WIDTH), row),
                  pl.BlockSpec((tm, GLA_VALUE_WIDTH), row), pl.BlockSpec((tm, D_MODEL), row),
                  pl.BlockSpec((tm, D_MODEL), row)] + [pl.BlockSpec(w[k].shape, const) for k in wnames],
        out_specs=[pl.BlockSpec((tm * TOKEN_ROWS, LANES), row),
                   pl.BlockSpec((rt_rows, tm), lambda i: (0, i))],
        out_shape=[jax.ShapeDtypeStruct((n * TOKEN_ROWS, LANES), F32), jax.ShapeDtypeStruct((rt_rows, n), F32)],
        compiler_params=_cparams(("parallel",)),
        name="post_mixer",
    )(x2d, a_out, g_out, ga, gb, *[w[k] for k in wnames])


def _moe_plan(rt, tme):
    n = rt.shape[1]
    ntiles = n // tme
    max_items = ntiles + N_GROUPS - 1
    i1, i2 = rt[0].astype(jnp.int32), rt[1].astype(jnp.int32)
    grp = i1 // EXPERTS_PER_GROUP
    lo = jnp.minimum(i1, i2) % EXPERTS_PER_GROUP
    hi = jnp.maximum(i1, i2) % EXPERTS_PER_GROUP
    key = (grp * EXPERTS_PER_GROUP + lo) * EXPERTS_PER_GROUP + hi
    order = jnp.argsort(key, stable=True).astype(jnp.int32)
    skey = key[order].reshape(ntiles, tme)
    sg = skey // (EXPERTS_PER_GROUP * EXPERTS_PER_GROUP)
    slo = (skey // EXPERTS_PER_GROUP) % EXPERTS_PER_GROUP
    shi = skey % EXPERTS_PER_GROUP
    ev = jnp.arange(EXPERTS_PER_GROUP)
    in_g = sg[:, :, None] == jnp.arange(N_GROUPS)
    uses_e = (slo[:, :, None] == ev) | (shi[:, :, None] == ev)
    flags_tge = jnp.any(in_g[:, :, :, None] & uses_e[:, :, None, :], axis=1)
    present = jnp.any(in_g, axis=1).reshape(-1)
    pos = jnp.cumsum(present) - 1
    n_items = pos[-1] + 1
    src = jnp.zeros((max_items,), jnp.int32).at[jnp.where(present, pos, max_items)].set(
        jnp.arange(ntiles * N_GROUPS, dtype=jnp.int32), mode="drop")
    it = jnp.arange(max_items)
    valid = it < n_items
    last_src = src[n_items - 1]
    src = jnp.where(valid, src, last_src)
    item_tile = src // N_GROUPS
    item_group = src % N_GROUPS
    prev_tile = jnp.concatenate([jnp.full((1,), -1, jnp.int32), item_tile[:-1]])
    next_tile = jnp.concatenate([item_tile[1:], jnp.full((1,), -1, jnp.int32)])
    first = valid & (item_tile != prev_tile)
    last = valid & ((item_tile != next_tile) | (it == n_items - 1))
    flags = flags_tge[item_tile, item_group] & valid[:, None]
    e0 = jnp.argmax(flags, axis=1)
    rest = flags & (ev[None, :] != e0[:, None])
    e1 = jnp.argmax(rest, axis=1)
    rest = rest & (ev[None, :] != e1[:, None])
    has2 = jnp.any(rest, axis=1)
    e2 = jnp.argmax(rest, axis=1)
    rest = rest & (ev[None, :] != e2[:, None])
    rt_sorted = rt[:, order].reshape(rt.shape[0], ntiles, tme).transpose(1, 0, 2)
    i32 = lambda z: z.astype(jnp.int32)
    plan = (order, i32(item_tile), i32(item_group), i32(first), i32(last), i32(valid), i32(e0), i32(e1), i32(e2),
            i32(has2), i32(rest.reshape(-1)))
    return plan, rt_sorted


def _moe_kernel(order, itile, igroup, ifirst, ilast, ivalid, ie0, ie1, ie2, ihas2, flags, x_hbm, rt_ref, wg, wu, wd, nffn, nfin,
                y_hbm, xbuf, ybuf, acc, hbuf, rcol, gsem, ssem, *, tme, ntiles):
    i = pl.program_id(0)
    t = itile[i]
    slot = t % 2
    g = igroup[i]
    is_first = ifirst[i] == 1

    def gather_row(tile, sl, r):
        tok = order[tile * tme + r]
        src = x_hbm.at[pl.ds(pl.multiple_of(tok * TOKEN_ROWS, TOKEN_ROWS), TOKEN_ROWS)]
        dst = xbuf.at[pl.ds(pl.multiple_of((sl * tme + r) * TOKEN_ROWS, TOKEN_ROWS), TOKEN_ROWS)]
        return pltpu.make_async_copy(src, dst, gsem.at[sl])

    def scatter_row(tile, sl, r):
        tok = order[tile * tme + r]
        return pltpu.make_async_copy(ybuf.at[sl, pl.ds(r, 1)], y_hbm.at[pl.ds(tok, 1)], ssem.at[sl])

    def start_rows(make, tile, sl):
        def body(r, c):
            make(tile, sl, r).start()
            return c
        lax.fori_loop(0, tme, body, 0, unroll=8)

    def wait_gather(sl):
        rows = tme * TOKEN_ROWS
        pltpu.make_async_copy(x_hbm.at[pl.ds(0, rows)], xbuf.at[pl.ds(pl.multiple_of(sl * rows, rows), rows)],
                              gsem.at[sl]).wait()

    def wait_scatter(sl):
        pltpu.make_async_copy(ybuf.at[sl], y_hbm.at[pl.ds(0, tme)], ssem.at[sl]).wait()

    def start_rows_inline(make, tile, sl, r0=0, r1=tme):
        for r in range(r0, r1):
            make(tile, sl, r).start()

    def expert(e):
        eid = (g * EXPERTS_PER_GROUP + e).astype(F32)
        ce = (jnp.where(rcol[:, 0:1] == eid, rcol[:, 2:3], 0.0)
              + jnp.where(rcol[:, 1:2] == eid, rcol[:, 3:4], 0.0))
        h = hbuf[...]
        a = _dot(h, wg[e])
        u = _dot(h, wu[e])
        act = (a * _sigmoid(a)) * u * ce
        acc[...] += _dot(act.astype(BF16), wd[e])

    def expert_with(e, run, alternatives):
        plain = run
        for cond, side_work in alternatives:
            @pl.when(cond)
            def _():
                side_work()
                expert(e)

            plain = jnp.logical_and(plain, jnp.logical_not(cond))

        @pl.when(plain)
        def _():
            expert(e)

    @pl.when(i == 0)
    def _():
        start_rows(gather_row, 0, 0)

    @pl.when(is_first)
    def _():
        wait_gather(slot)
        base = pl.multiple_of(slot * (tme * TOKEN_ROWS), tme * TOKEN_ROWS)
        x1 = jnp.concatenate([xbuf[pl.ds(base + j, tme, stride=TOKEN_ROWS), :] for j in range(TOKEN_ROWS)], axis=1)
        acc[...] = x1
        hbuf[...] = _rms(x1, nffn[...]).astype(BF16)
        rt = rt_ref[0]
        rcol[...] = jnp.transpose(jnp.concatenate([rt, jnp.zeros((LANES - rt.shape[0], tme), F32)], axis=0))

    valid = ivalid[i] == 1
    has2 = ihas2[i] == 1
    do_gather = jnp.logical_and(is_first, t + 1 < ntiles)
    do_scatter = jnp.logical_and(is_first, t >= 1)
    half = tme // 2
    scatter_rows = lambda r0, r1: (lambda: start_rows_inline(scatter_row, t - 1, 1 - slot, r0, r1))
    expert_with(ie0[i], valid, [(do_gather, lambda: start_rows_inline(gather_row, t + 1, 1 - slot))])
    expert_with(ie1[i], valid, [(jnp.logical_and(do_scatter, has2), scatter_rows(0, half)),
                                (jnp.logical_and(do_scatter, jnp.logical_not(has2)), scatter_rows(0, tme))])
    expert_with(ie2[i], has2, [(jnp.logical_and(do_scatter, has2), scatter_rows(half, tme))])
    for e in range(EXPERTS_PER_GROUP):
        @pl.when(flags[i * EXPERTS_PER_GROUP + e] == 1)
        def _():
            expert(e)

    @pl.when(ilast[i] == 1)
    def _():
        @pl.when(t >= 2)
        def _():
            wait_scatter(slot)

        ybuf[slot] = _rms(acc[...], nfin[...])

        @pl.when(t == ntiles - 1)
        def _():
            start_rows(scatter_row, t, slot)

    @pl.when(i == pl.num_programs(0) - 1)
    def _():
        for sl in range(min(2, ntiles)):
            wait_scatter(sl)


def _moe(x1t, rt, weg, weu, wed, nffn, nfin):
    n = x1t.shape[0] // TOKEN_ROWS
    tme = min(MOE_TILE, n)
    assert n % tme == 0
    ntiles = n // tme
    plan, rt_sorted = _moe_plan(rt, tme)
    max_items = ntiles + N_GROUPS - 1
    grp = lambda i, order, itile, igroup, *_: (igroup[i], 0, 0)
    til = lambda i, order, itile, *_: (itile[i], 0, 0)
    const = lambda i, *_: (0, 0)
    grid_spec = pltpu.PrefetchScalarGridSpec(
        num_scalar_prefetch=len(plan),
        grid=(max_items,),
        in_specs=[pl.BlockSpec(memory_space=pl.ANY),
                  pl.BlockSpec((1,) + rt_sorted.shape[1:], til),
                  pl.BlockSpec((EXPERTS_PER_GROUP, D_MODEL, EXPERT_FF), grp),
                  pl.BlockSpec((EXPERTS_PER_GROUP, D_MODEL, EXPERT_FF), grp),
                  pl.BlockSpec((EXPERTS_PER_GROUP, EXPERT_FF, D_MODEL), grp),
                  pl.BlockSpec((1, D_MODEL), const), pl.BlockSpec((1, D_MODEL), const)],
        out_specs=pl.BlockSpec(memory_space=pl.ANY),
        scratch_shapes=[pltpu.VMEM((2 * tme * TOKEN_ROWS, LANES), F32), pltpu.VMEM((2, tme, D_MODEL), F32),
                        pltpu.VMEM((tme, D_MODEL), F32), pltpu.VMEM((tme, D_MODEL), BF16),
                        pltpu.VMEM((tme, LANES), F32),
                        pltpu.SemaphoreType.DMA((2,)), pltpu.SemaphoreType.DMA((2,))],
    )
    return pl.pallas_call(
        functools.partial(_moe_kernel, tme=tme, ntiles=ntiles),
        grid_spec=grid_spec,
        out_shape=jax.ShapeDtypeStruct((n, D_MODEL), F32),
        compiler_params=_cparams(("arbitrary",)),
        name="moe",
    )(*plan, x1t, rt_sorted, weg, weu, wed, nffn, nfin)


def _rope_tables(positions):
    half = HEAD_DIM // 2
    inv_freq = ROPE_THETA ** (-jnp.arange(half, dtype=F32) / half)
    ang = positions.astype(F32)[:, None] * inv_freq[None, :]
    cos, sin = jnp.cos(ang), jnp.sin(ang)
    reps = LANES // HEAD_DIM
    return (jnp.tile(jnp.concatenate([cos, cos], -1), (1, reps)),
            jnp.tile(jnp.concatenate([-sin, sin], -1), (1, reps)))


def _prep_weights(norm_mix, w_in, w_gla_f2, b_gla_f, gla_norm, w_proj_attn, w_proj_gla, w_out, norm_ffn,
                  w_router_group, b_router_group, w_router_expert, b_router_expert):
    f0 = sum(INPROJ_WIDTHS[:6])
    f1 = f0 + GLA_GATE_RANK
    w = {"win": jnp.concatenate([w_in[:, :f0], w_in[:, f1:], w_in[:, f0:f1],
                                 jnp.zeros((D_MODEL, LANES - GLA_GATE_RANK), w_in.dtype)], axis=1).astype(BF16)}
    w["wf2"] = jnp.pad(w_gla_f2.astype(BF16), ((0, LANES - GLA_GATE_RANK), (0, 0)))
    w["bf"] = b_gla_f.reshape(1, -1)
    w["norm_mix"] = norm_mix.reshape(1, -1)
    w["gla_norm"] = gla_norm.reshape(1, -1)
    w["wpa"] = w_proj_attn.astype(BF16)
    w["wpg"] = w_proj_gla.astype(BF16)
    w["wo"] = w_out.astype(BF16)
    w["nf"] = norm_ffn.reshape(1, -1)
    pad = LANES - N_EXPERTS - N_GROUPS
    wr_t = jnp.pad(jnp.concatenate([w_router_expert, w_router_group], axis=1), ((0, 0), (0, pad))).T
    w["wr_hi"], w["wr_lo"] = _split_bf16(wr_t)
    w["br"] = jnp.broadcast_to(jnp.pad(jnp.concatenate([b_router_expert, b_router_group]), (0, pad))[:, None],
                               (LANES, LANES))
    return w


def _layer(x, positions_tab, cache, w, sinks, weg, weu, wed, nfin, tm):
    batch, seq, _ = x.shape
    n = batch * seq
    x2d = x.reshape(n, D_MODEL)
    cos_t, sin_t = positions_tab
    qa, ka, va, qg, kg, vg, rg, la, ga, gb = _inproj(x2d, w["norm_mix"], cos_t, sin_t, w, tm)
    if cache is None:
        a_out = _swa_prompt(sinks, qa, ka, va, batch, seq)
        last = lambda z: z.reshape(batch, seq, KV_WIDTH)[:, seq - WINDOW:].reshape(batch, WINDOW, ATTN_KV_HEADS, HEAD_DIM)
        new_k, new_v = last(ka), last(va)
        g_out, new_s = _gla_prompt(qg, kg, vg, la, rg, w["gla_norm"], batch, seq)
    else:
        cache_k, cache_v, state = cache
        a_out, new_k, new_v = _swa_sample(sinks, qa, ka, va, cache_k.reshape(batch, WINDOW, KV_WIDTH),
                                          cache_v.reshape(batch, WINDOW, KV_WIDTH), batch, seq)
        new_k = new_k.reshape(batch, WINDOW, ATTN_KV_HEADS, HEAD_DIM)
        new_v = new_v.reshape(batch, WINDOW, ATTN_KV_HEADS, HEAD_DIM)
        g_out, new_s = _gla_sample(qg, kg, vg, la, rg, state, w["gla_norm"], batch, seq)
    x1t, rt = _post(x2d, a_out, g_out, ga, gb, w, tm)
    y = _moe(x1t, rt, weg, weu, wed, w["nf"], nfin)
    return y.reshape(batch, seq, D_MODEL), new_k, new_v, new_s


def kernel(x_prompt, x_sample, cache_win_k, cache_win_v, state_gla, norm_mix, w_in, w_gla_f2, b_gla_f, gla_norm,
           attn_sinks, w_proj_attn, w_proj_gla, w_out, norm_ffn, w_router_group, b_router_group, w_router_expert,
           b_router_expert, w_exp_gate, w_exp_up, w_exp_down, norm_final):
    assert norm_mix.shape[0] == 1, "single-layer step"
    seq_p = x_prompt.shape[1]
    dec_b, dec_t = x_sample.shape[0], x_sample.shape[1]
    w = _prep_weights(norm_mix[0], w_in[0], w_gla_f2[0], b_gla_f[0], gla_norm[0], w_proj_attn[0], w_proj_gla[0],
                      w_out[0], norm_ffn[0], w_router_group[0], b_router_group[0], w_router_expert[0],
                      b_router_expert[0])
    weg = w_exp_gate[0].astype(BF16)
    weu = w_exp_up[0].astype(BF16)
    wed = w_exp_down[0].astype(BF16)
    nfin = norm_final.reshape(1, -1)
    sinks = attn_sinks[0]
    tab_p = _rope_tables(jnp.arange(seq_p, dtype=jnp.int32))
    pos_s = PAST_LEN + jnp.arange(dec_t, dtype=jnp.int32)
    tab_s = tuple(jnp.tile(t, (dec_b, 1)) for t in _rope_tables(pos_s))
    tm_p = min(512, seq_p)
    tm_s = dec_b * dec_t
    yp, pk, pv, ps = _layer(x_prompt, tab_p, None, w, sinks, weg, weu, wed, nfin, tm_p)
    ys, sk, sv, ss = _layer(x_sample, tab_s, (cache_win_k[0], cache_win_v[0], state_gla[0]), w, sinks, weg, weu, wed,
                            nfin, tm_s)
    return (yp, ys, pk[None], pv[None], ps[None], sk[None], sv[None], ss[None])
```

```python
import functools
import math

import numpy as np
import jax
import jax.numpy as jnp
from jax import lax
from jax.experimental import pallas as pl
from jax.experimental.pallas import tpu as pltpu

F32 = jnp.float32
BF16 = jnp.bfloat16

D_MODEL = 1024
ATTN_HEADS = 8
ATTN_KV_HEADS = 2
GROUP = ATTN_HEADS // ATTN_KV_HEADS
HEAD_DIM = 64
ATTN_WIDTH = ATTN_HEADS * HEAD_DIM
KV_WIDTH = ATTN_KV_HEADS * HEAD_DIM
WINDOW = 128
ROPE_THETA = 10000.0
PAST_LEN = 8192
GLA_HEADS = 4
GLA_KEY_WIDTH = D_MODEL // 2
GLA_VALUE_WIDTH = D_MODEL
GLA_DK = GLA_KEY_WIDTH // GLA_HEADS
GLA_DV = GLA_VALUE_WIDTH // GLA_HEADS
GLA_GATE_RANK = 16
GLA_GATE_NORMALIZER = 16.0
N_GROUPS = 4
EXPERTS_PER_GROUP = 8
N_EXPERTS = N_GROUPS * EXPERTS_PER_GROUP
EXPERT_FF = 256
EPS = 1e-6
LOG2_E = 1.4426950408889634

LANES = 128
GLA_CHUNK = 128
GLA_HEADS_PER_STEP = 4
SWA_BLOCKS_PER_STEP = 4
SAMPLE_SEQ_BLOCK = 8
MOE_TILE = 256
TOKEN_ROWS = D_MODEL // LANES
INPROJ_WIDTHS = (ATTN_WIDTH, 2 * KV_WIDTH, GLA_KEY_WIDTH, GLA_KEY_WIDTH, GLA_VALUE_WIDTH, GLA_VALUE_WIDTH)
ROUTER_ROWS = 40
VMEM_LIMIT = 56 * 1024 * 1024


def _cparams(sem):
    return pltpu.CompilerParams(dimension_semantics=sem, vmem_limit_bytes=VMEM_LIMIT)


def _rms(x, g):
    return x * lax.rsqrt(jnp.mean(x * x, axis=-1, keepdims=True) + EPS) * g


def _sigmoid(x):
    return 1.0 / (1.0 + jnp.exp(-x))


def _dot(a, b):
    return jnp.dot(a, b, preferred_element_type=F32)


def _dot_nt(a, b):
    return lax.dot_general(a, b, (((1,), (1,)), ((), ())), preferred_element_type=F32)


def _dot_tn(a, b):
    return lax.dot_general(a, b, (((0,), (0,)), ((), ())), preferred_element_type=F32)


def _split_bf16(x):
    hi = x.astype(BF16)
    lo = (x - hi.astype(F32)).astype(BF16)
    return hi, lo


def _inproj_kernel(x_ref, g_ref, cos_ref, sin_ref, win, wf2, bf,
                   qa_o, ka_o, va_o, qg_o, kg_o, vg_o, rg_o, la_o, ga_o, gb_o, wgate):
    f0 = sum(INPROJ_WIDTHS)

    @pl.when(pl.program_id(0) == 0)
    def _():
        tail = win[:, f0:]
        wgate[...] = tail[:, GLA_GATE_RANK:GLA_GATE_RANK + 2 * D_MODEL]

    tm = x_ref.shape[0]
    halves = [slice(0, tm // 2), slice(tm // 2, tm)] if tm % 16 == 0 else [slice(0, tm)]
    hbs = [_rms(x_ref[r, :], g_ref[...]).astype(BF16) for r in halves]
    cols = np.cumsum((0,) + INPROJ_WIDTHS[:-1])
    wqa, wkva, wqg, wkg, wvg, wrg = (win.at[:, int(c):int(c) + wd] for c, wd in zip(cols, INPROJ_WIDTHS))
    wga = wgate.at[:, :D_MODEL]
    wgb = wgate.at[:, D_MODEL:]
    wf = win.at[:, f0:f0 + LANES]
    lane = lax.broadcasted_iota(jnp.int32, (halves[0].stop - halves[0].start, LANES), 1)
    first_half = (lane % HEAD_DIM) < (HEAD_DIM // 2)

    def rope(t, r):
        swapped = jnp.where(first_half, pltpu.roll(t, LANES - HEAD_DIM // 2, 1), pltpu.roll(t, HEAD_DIM // 2, 1))
        return t * cos_ref[r, :] + swapped * sin_ref[r, :]

    for r, hb in zip(halves, hbs):
        qa = _dot(hb, wqa[...])
        for c in range(ATTN_WIDTH // LANES):
            sl = slice(c * LANES, (c + 1) * LANES)
            qa_o[r, sl] = (rope(qa[:, sl], r) * (HEAD_DIM ** -0.5)).astype(BF16)
    for r, hb in zip(halves, hbs):
        kva = _dot(hb, wkva[...])
        ka_o[r, :] = rope(kva[:, :KV_WIDTH], r)
        va_o[r, :] = kva[:, KV_WIDTH:]
    for r, hb in zip(halves, hbs):
        qg_o[r, :] = (_dot(hb, wqg[...]) * (GLA_DK ** -0.5)).astype(BF16)
    for r, hb in zip(halves, hbs):
        kg_o[r, :] = _dot(hb, wkg[...]).astype(BF16)
    for r, hb in zip(halves, hbs):
        vg_o[r, :] = _dot(hb, wvg[...]).astype(BF16)
    for r, hb in zip(halves, hbs):
        rg_o[r, :] = _dot(hb, wrg[...]).astype(BF16)
    for r, hb in zip(halves, hbs):
        z = _dot(_dot(hb, wf[...]).astype(BF16), wf2[...]) + bf[...]
        la_o[r, :] = (jnp.minimum(z, 0.0) - jnp.log1p(jnp.exp(-jnp.abs(z)))) * (1.0 / GLA_GATE_NORMALIZER)
    for r, hb in zip(halves, hbs):
        ga_o[r, :] = _sigmoid(_dot(hb, wga[...])).astype(BF16)
    for r, hb in zip(halves, hbs):
        gb_o[r, :] = _sigmoid(_dot(hb, wgb[...])).astype(BF16)


def _inproj(x2d, norm_g, cos_t, sin_t, w, tm):
    n = x2d.shape[0]
    ntab = cos_t.shape[0] // tm
    row = lambda i: (i, 0)
    const = lambda i: (0, 0)
    tab = lambda i: (i % ntab, 0)
    wnames = ("win", "wf2", "bf")
    wspecs = [pl.BlockSpec(w[k].shape, const) for k in wnames]
    widths = (ATTN_WIDTH, KV_WIDTH, KV_WIDTH, GLA_KEY_WIDTH, GLA_KEY_WIDTH, GLA_VALUE_WIDTH, GLA_VALUE_WIDTH,
              GLA_KEY_WIDTH, D_MODEL, D_MODEL)
    dtypes = (BF16, F32, F32, BF16, BF16, BF16, BF16, F32, BF16, BF16)
    return pl.pallas_call(
        _inproj_kernel,
        grid=(n // tm,),
        in_specs=[pl.BlockSpec((tm, D_MODEL), row), pl.BlockSpec((1, D_MODEL), const),
                  pl.BlockSpec((tm, LANES), tab), pl.BlockSpec((tm, LANES), tab)] + wspecs,
        out_specs=[pl.BlockSpec((tm, wd), row) for wd in widths],
        out_shape=[jax.ShapeDtypeStruct((n, wd), dt) for wd, dt in zip(widths, dtypes)],
        scratch_shapes=[pltpu.VMEM((D_MODEL, 2 * D_MODEL), BF16)],
        compiler_params=_cparams(("arbitrary",)),
        name="inproj",
    )(x2d, norm_g, cos_t, sin_t, *[w[k] for k in wnames])


def _swa_prompt_kernel(sink_ref, q_ref, kc_ref, kp_ref, vc_ref, vp_ref, o_ref, *, qb):
    n = pl.program_id(1)
    k3 = jnp.concatenate([kp_ref[...], kc_ref[...]], axis=0).astype(BF16)
    v3 = jnp.concatenate([vp_ref[...], vc_ref[...]], axis=0).astype(BF16)
    t = lax.broadcasted_iota(jnp.int32, (WINDOW, 2 * WINDOW), 0)
    j = lax.broadcasted_iota(jnp.int32, (WINDOW, 2 * WINDOW), 1)
    band = (j >= t) & (j <= t + WINDOW)
    zeros = jnp.zeros((2 * WINDOW, HEAD_DIM), BF16)
    ones = jnp.ones((2 * WINDOW, HEAD_DIM), BF16)
    lane = lax.broadcasted_iota(jnp.int32, (WINDOW, LANES), 1)
    chains = [(blk, h) for blk in range(qb) for h in range(ATTN_HEADS)]
    scores = []
    for blk, h in chains:
        kv = h // GROUP
        rows = slice(blk * WINDOW, (blk + 1) * WINDOW)
        keys = slice(blk * WINDOW, (blk + 2) * WINDOW)
        s = _dot_nt(q_ref[rows, h * HEAD_DIM:(h + 1) * HEAD_DIM], k3[keys, kv * HEAD_DIM:(kv + 1) * HEAD_DIM])
        valid = band & ((j >= WINDOW) | (n > 0)) if blk == 0 else band
        scores.append(jnp.where(valid, s, -jnp.inf))
    probs, sink_terms = [], []
    for (blk, h), s in zip(chains, scores):
        m = jnp.maximum(jnp.max(s, axis=-1, keepdims=True), sink_ref[h])
        probs.append(jnp.exp(s - m).astype(BF16))
        sink_terms.append(jnp.exp(sink_ref[h] - m))
    for blk in range(qb):
        rows = slice(blk * WINDOW, (blk + 1) * WINDOW)
        keys = slice(blk * WINDOW, (blk + 2) * WINDOW)
        for kv in range(ATTN_KV_HEADS):
            vv = v3[keys, kv * HEAD_DIM:(kv + 1) * HEAD_DIM]
            vext = (jnp.concatenate([vv, zeros, ones, zeros], axis=1), jnp.concatenate([zeros, vv, zeros, ones], axis=1))
            for pr in range(GROUP // 2):
                h0 = kv * GROUP + pr * 2
                c0 = blk * ATTN_HEADS + h0
                acc = _dot(probs[c0], vext[0]) + _dot(probs[c0 + 1], vext[1])
                l = acc[:, LANES:] + jnp.where(lane < HEAD_DIM, sink_terms[c0], sink_terms[c0 + 1])
                o_ref[rows, h0 * HEAD_DIM:h0 * HEAD_DIM + LANES] = (acc[:, :LANES] / l).astype(BF16)


def _swa_prompt(sinks, qa, ka, va, batch, seq):
    nb = seq // WINDOW
    qb = math.gcd(SWA_BLOCKS_PER_STEP, nb)
    steps = nb // qb
    cur = lambda b, n: (b * steps + n, 0)
    prev = lambda b, n: (b * nb + jnp.maximum(n * qb - 1, 0), 0)
    return pl.pallas_call(
        functools.partial(_swa_prompt_kernel, qb=qb),
        grid=(batch, steps),
        in_specs=[pl.BlockSpec(memory_space=pltpu.SMEM),
                  pl.BlockSpec((qb * WINDOW, ATTN_WIDTH), cur),
                  pl.BlockSpec((qb * WINDOW, KV_WIDTH), cur), pl.BlockSpec((WINDOW, KV_WIDTH), prev),
                  pl.BlockSpec((qb * WINDOW, KV_WIDTH), cur), pl.BlockSpec((WINDOW, KV_WIDTH), prev)],
        out_specs=pl.BlockSpec((qb * WINDOW, ATTN_WIDTH), cur),
        out_shape=jax.ShapeDtypeStruct(qa.shape, BF16),
        compiler_params=_cparams(("parallel", "parallel")),
        name="swa_prompt",
    )(sinks, qa, ka, ka, va, va)


def _swa_sample_kernel(sink_ref, q_ref, kn_ref, vn_ref, kn3_ref, vn3_ref, ck_ref, cv_ref, o_ref, nk_ref, nv_ref,
                       *, t_new):
    sb = ck_ref.shape[0]
    spv = 8 // t_new
    nq = GROUP * 8
    nc = spv * WINDOW
    qi = lax.broadcasted_iota(jnp.int32, (nq, nc), 0) % 8
    ci = lax.broadcasted_iota(jnp.int32, (nq, nc), 1)
    valid_c = (qi // t_new == ci // WINDOW) & (ci % WINDOW >= qi % t_new)
    qn = lax.broadcasted_iota(jnp.int32, (nq, 8), 0) % 8
    cn = lax.broadcasted_iota(jnp.int32, (nq, 8), 1)
    valid_n = (qn // t_new == cn // t_new) & (cn <= qn)
    grow = lax.broadcasted_iota(jnp.int32, (nq, 1), 0) // 8
    chains = [(vr, kv) for vr in range(sb // spv) for kv in range(ATTN_KV_HEADS)]
    scored = []
    for vr, kv in chains:
        r8 = slice(8 * vr, 8 * vr + 8)
        cs = slice(kv * HEAD_DIM, (kv + 1) * HEAD_DIM)
        heads = [kv * GROUP + g for g in range(GROUP)]
        qs = jnp.concatenate([q_ref[r8, h * HEAD_DIM:(h + 1) * HEAD_DIM] for h in heads], axis=0)
        ck = ck_ref[vr * spv:(vr + 1) * spv, :, cs].reshape(nc, HEAD_DIM).astype(BF16)
        kn = kn_ref[r8, cs].astype(BF16)
        scored.append((jnp.where(valid_c, _dot_nt(qs, ck), -jnp.inf), jnp.where(valid_n, _dot_nt(qs, kn), -jnp.inf)))
    soft = []
    for (vr, kv), (s_c, s_n) in zip(chains, scored):
        sink = sink_ref[kv * GROUP]
        for g in range(1, GROUP):
            sink = jnp.where(grow == g, sink_ref[kv * GROUP + g], sink)
        m = jnp.maximum(jnp.maximum(jnp.max(s_c, axis=-1, keepdims=True), jnp.max(s_n, axis=-1, keepdims=True)), sink)
        p_c = jnp.exp(s_c - m)
        p_n = jnp.exp(s_n - m)
        l = jnp.sum(p_c, axis=-1, keepdims=True) + jnp.sum(p_n, axis=-1, keepdims=True) + jnp.exp(sink - m)
        soft.append((p_c.astype(BF16), p_n.astype(BF16), l))
    for (vr, kv), (p_c, p_n, l) in zip(chains, soft):
        r8 = slice(8 * vr, 8 * vr + 8)
        cs = slice(kv * HEAD_DIM, (kv + 1) * HEAD_DIM)
        cv = cv_ref[vr * spv:(vr + 1) * spv, :, cs].reshape(nc, HEAD_DIM).astype(BF16)
        o = (_dot(p_c, cv) + _dot(p_n, vn_ref[r8, cs].astype(BF16))) / l
        for a in range(GROUP // 2):
            pair = jnp.concatenate([o[16 * a:16 * a + 8], o[16 * a + 8:16 * a + 16]], axis=1)
            c0 = (kv * GROUP + 2 * a) * HEAD_DIM
            o_ref[r8, c0:c0 + LANES] = pair.astype(BF16)
    nk_ref[:, 0:WINDOW - t_new, :] = ck_ref[:, t_new:WINDOW, :]
    nk_ref[:, WINDOW - t_new:WINDOW, :] = kn3_ref[...]
    nv_ref[:, 0:WINDOW - t_new, :] = cv_ref[:, t_new:WINDOW, :]
    nv_ref[:, WINDOW - t_new:WINDOW, :] = vn3_ref[...]


def _swa_sample(sinks, qa, ka, va, cache_k, cache_v, batch, t_new):
    sb = SAMPLE_SEQ_BLOCK
    rows = sb * t_new
    r2 = lambda i: (i, 0)
    r3 = lambda i: (i, 0, 0)
    ka3 = ka.reshape(batch, t_new, KV_WIDTH)
    va3 = va.reshape(batch, t_new, KV_WIDTH)
    return pl.pallas_call(
        functools.partial(_swa_sample_kernel, t_new=t_new),
        grid=(batch // sb,),
        in_specs=[pl.BlockSpec(memory_space=pltpu.SMEM),
                  pl.BlockSpec((rows, ATTN_WIDTH), r2),
                  pl.BlockSpec((rows, KV_WIDTH), r2), pl.BlockSpec((rows, KV_WIDTH), r2),
                  pl.BlockSpec((sb, t_new, KV_WIDTH), r3), pl.BlockSpec((sb, t_new, KV_WIDTH), r3),
                  pl.BlockSpec((sb, WINDOW, KV_WIDTH), r3), pl.BlockSpec((sb, WINDOW, KV_WIDTH), r3)],
        out_specs=[pl.BlockSpec((rows, ATTN_WIDTH), r2),
                   pl.BlockSpec((sb, WINDOW, KV_WIDTH), r3), pl.BlockSpec((sb, WINDOW, KV_WIDTH), r3)],
        out_shape=[jax.ShapeDtypeStruct(qa.shape, BF16),
                   jax.ShapeDtypeStruct(cache_k.shape, F32), jax.ShapeDtypeStruct(cache_v.shape, F32)],
        compiler_params=_cparams(("parallel",)),
        name="swa_sample",
    )(sinks, qa, ka, va, ka3, va3, cache_k, cache_v)


def _gla_constants(c, seg):
    t = np.arange(c)
    sid = t // seg
    same = sid[:, None] == sid[None, :]
    levels = []
    m = seg // 2
    while m >= 1:
        levels.append(m)
        m //= 2
    mats, roles, masks = [], [], []
    for m in levels:
        blk = t // (2 * m)
        second = (t // m) % 2 == 1
        p = blk * 2 * m + m - 1
        u = t[None, :]
        mq = (u > p[:, None]) & (u <= t[:, None])
        mk = (u > t[:, None]) & (u <= p[:, None])
        mats.append(np.where(second[:, None], mq, mk))
        roles.append(np.broadcast_to(second[:, None], (c, LANES)))
        masks.append((blk[:, None] == blk[None, :]) & second[:, None] & ~second[None, :])
    masks.append(np.eye(c, dtype=bool))
    mats.append(same & (t[None, :] <= t[:, None]))
    mats.append(same & (t[None, :] > t[:, None]))
    mall = np.concatenate(mats, 0).astype(np.float32)
    mall = jnp.asarray(np.concatenate([mall, mall], 1), BF16)
    role = jnp.asarray(np.concatenate(roles, 0).astype(np.float32))
    mask = jnp.asarray(np.concatenate(masks, 0).astype(np.float32))
    return len(levels), mall, role, mask


def _gla_exponents(la, mall):
    hl = jnp.concatenate(_split_bf16(la * LOG2_E), axis=0)
    return _dot(mall, hl), hl


def _gla_scores(qb, kb, e2, role_ref, mask_ref, nlev, c):
    qf = qb.astype(F32)
    kf = kb.astype(F32)
    terms = []
    for lv in range(nlev):
        sl = slice(lv * c, (lv + 1) * c)
        x = (jnp.where(role_ref[sl, :] > 0.5, qf, kf) * jnp.exp2(e2[sl])).astype(BF16)
        terms.append((x, x, lv))
    terms.append((qb, kb, nlev))
    mask = lambda i: mask_ref[i * c:(i + 1) * c, :]
    a = None
    pair = c % LANES == 0
    while terms:
        if pair and len(terms) >= 2:
            (l0, r0, i0), (l1, r1, i1) = terms.pop(), terms.pop()
            z = jnp.zeros_like(r0)
            rhs = jnp.concatenate([jnp.concatenate([r0, z], axis=1), jnp.concatenate([z, r1], axis=1)], axis=0)
            g = _dot_nt(jnp.concatenate([l0, l1], axis=1), rhs)
            t = mask(i0) * g[:, :c] + mask(i1) * g[:, c:]
        else:
            l0, r0, i0 = terms.pop()
            t = mask(i0) * _dot_nt(l0, r0)
        a = t if a is None else a + t
    return a, qf, kf


def _gla_out(o, g, r):
    r = r.astype(F32)
    return (_rms(o, g) * (r * _sigmoid(r))).astype(BF16)


def _gla_prompt_kernel(q_ref, k_ref, v_ref, la_ref, r_ref, mall_ref, role_ref, mask_ref, g_ref, o_ref, s_ref,
                       s_scr, *, nlev, nchunks):
    c = GLA_CHUNK
    hp = GLA_HEADS_PER_STEP
    s_scr[...] = jnp.zeros_like(s_scr)
    ones = jnp.ones((2 * c, LANES), BF16)

    def chunk(i, carry):
        rows = pl.ds(pl.multiple_of(i * c, c), c)
        e2_all, hl_all = _gla_exponents(la_ref[rows, :], mall_ref[...])
        ksl = [slice(h * GLA_DK, (h + 1) * GLA_DK) for h in range(hp)]
        vsl = [slice(h * GLA_DV, (h + 1) * GLA_DV) for h in range(hp)]
        inter = []
        for h in range(hp):
            e2 = e2_all[:, ksl[h]]
            v = v_ref[rows, vsl[h]]
            qf = q_ref[rows, ksl[h]].astype(F32)
            kf = k_ref[rows, ksl[h]].astype(F32)
            s = s_scr[h]
            inter.append(_dot((qf * jnp.exp2(e2[nlev * c:(nlev + 1) * c])).astype(BF16), s.astype(BF16)))
            kt = (kf * jnp.exp2(e2[(nlev + 1) * c:(nlev + 2) * c])).astype(BF16)
            dec = jnp.exp2(_dot_tn(hl_all[:, ksl[h]], ones))
            s_scr[h] = jnp.concatenate([dec] * (GLA_DV // LANES), axis=1) * s + _dot_tn(kt, v)
        scores = [_gla_scores(q_ref[rows, ksl[h]], k_ref[rows, ksl[h]], e2_all[:, ksl[h]], role_ref, mask_ref, nlev, c)[0]
                  for h in range(hp)]
        for h in range(hp):
            o = _dot(scores[h].astype(BF16), v_ref[rows, vsl[h]]) + inter[h]
            o_ref[rows, vsl[h]] = _gla_out(o, g_ref[...], r_ref[rows, vsl[h]])
        return carry

    lax.fori_loop(0, nchunks, chunk, 0)
    s_ref[0] = s_scr[...]


def _gla_prompt(qg, kg, vg, la, rg, gnorm, batch, seq):
    nlev, mall, role, mask = _gla_constants(GLA_CHUNK, GLA_CHUNK)
    hp = GLA_HEADS_PER_STEP
    bh = lambda b, h: (b, h)
    const = lambda b, h: (0, 0)
    return pl.pallas_call(
        functools.partial(_gla_prompt_kernel, nlev=nlev, nchunks=seq // GLA_CHUNK),
        grid=(batch, GLA_HEADS // hp),
        in_specs=[pl.BlockSpec((seq, hp * GLA_DK), bh), pl.BlockSpec((seq, hp * GLA_DK), bh),
                  pl.BlockSpec((seq, hp * GLA_DV), bh), pl.BlockSpec((seq, hp * GLA_DK), bh),
                  pl.BlockSpec((seq, hp * GLA_DV), bh),
                  pl.BlockSpec(mall.shape, const), pl.BlockSpec(role.shape, const), pl.BlockSpec(mask.shape, const),
                  pl.BlockSpec((1, GLA_DV), const)],
        out_specs=[pl.BlockSpec((seq, hp * GLA_DV), bh),
                   pl.BlockSpec((1, hp, GLA_DK, GLA_DV), lambda b, h: (b, h, 0, 0))],
        out_shape=[jax.ShapeDtypeStruct(vg.shape, BF16),
                   jax.ShapeDtypeStruct((batch, GLA_HEADS, GLA_DK, GLA_DV), F32)],
        scratch_shapes=[pltpu.VMEM((hp, GLA_DK, GLA_DV), F32)],
        compiler_params=_cparams(("parallel", "parallel")),
        name="gla_prompt",
    )(qg, kg, vg, la, rg, mall, role, mask, gnorm)


def _gla_sample_kernel(q_ref, k_ref, v_ref, la_ref, r_ref, s0_ref, mall_ref, role_ref, mask_ref, msum_ref, g_ref,
                       o_ref, s_ref, *, nlev, t_new):
    sb = s0_ref.shape[0]
    c = sb * t_new
    spv = 8 // t_new
    e2_all, hl_all = _gla_exponents(la_ref[...], mall_ref[...])
    seq_in_tile = lax.broadcasted_iota(jnp.int32, (8, GLA_DV), 0) // t_new
    seq_of_row = lax.broadcasted_iota(jnp.int32, (c, GLA_DV), 0) // t_new
    ksl = [slice(h * GLA_DK, (h + 1) * GLA_DK) for h in range(GLA_HEADS)]
    scores = [_gla_scores(q_ref[:, ksl[h]], k_ref[:, ksl[h]], e2_all[:, ksl[h]], role_ref, mask_ref, nlev, c)
              for h in range(GLA_HEADS)]
    for h in range(GLA_HEADS):
        ks = ksl[h]
        vs = slice(h * GLA_DV, (h + 1) * GLA_DV)
        v = v_ref[:, vs]
        e2 = e2_all[:, ks]
        hl = hl_all[:, ks]
        a, qf, kf = scores[h]
        qe = (qf * jnp.exp2(e2[nlev * c:(nlev + 1) * c])).astype(BF16)
        kt_t = jnp.transpose(kf * jnp.exp2(e2[(nlev + 1) * c:(nlev + 2) * c])).astype(BF16)
        bl_t = jnp.transpose(_dot(msum_ref[...], hl[:c]) + _dot(msum_ref[...], hl[c:]))
        dec_t = jnp.exp2(bl_t)
        inter = []
        for vr in range(c // 8):
            rows8 = qe[8 * vr:8 * vr + 8]
            tile = None
            for u in range(spv):
                j = vr * spv + u
                s0 = s0_ref[j, h]
                r = _dot(rows8, s0.astype(BF16))
                tile = r if tile is None else jnp.where(seq_in_tile == u, r, tile)
                vj = jnp.where(seq_of_row == j, v, jnp.zeros_like(v))
                s_ref[j, h] = dec_t[:, j:j + 1] * s0 + _dot(kt_t, vj)
            inter.append(tile)
        o = _dot(a.astype(BF16), v) + jnp.concatenate(inter, axis=0)
        o_ref[:, vs] = _gla_out(o, g_ref[...], r_ref[:, vs])


def _gla_sample(qg, kg, vg, la, rg, state, gnorm, batch, t_new):
    sb = SAMPLE_SEQ_BLOCK
    c = sb * t_new
    assert 8 % t_new == 0 and c % 8 == 0
    nlev, mall, role, mask = _gla_constants(c, t_new)
    msum = jnp.asarray((np.arange(c)[None, :] // t_new == np.arange(sb)[:, None]).astype(np.float32), BF16)
    rows = lambda i: (i, 0)
    const = lambda i: (0, 0)
    st = lambda i: (i, 0, 0, 0)
    return pl.pallas_call(
        functools.partial(_gla_sample_kernel, nlev=nlev, t_new=t_new),
        grid=(batch // sb,),
        in_specs=[pl.BlockSpec((c, GLA_KEY_WIDTH), rows), pl.BlockSpec((c, GLA_KEY_WIDTH), rows),
                  pl.BlockSpec((c, GLA_VALUE_WIDTH), rows), pl.BlockSpec((c, GLA_KEY_WIDTH), rows),
                  pl.BlockSpec((c, GLA_VALUE_WIDTH), rows),
                  pl.BlockSpec((sb, GLA_HEADS, GLA_DK, GLA_DV), st),
                  pl.BlockSpec(mall.shape, const), pl.BlockSpec(role.shape, const), pl.BlockSpec(mask.shape, const),
                  pl.BlockSpec(msum.shape, const), pl.BlockSpec((1, GLA_DV), const)],
        out_specs=[pl.BlockSpec((c, GLA_VALUE_WIDTH), rows), pl.BlockSpec((sb, GLA_HEADS, GLA_DK, GLA_DV), st)],
        out_shape=[jax.ShapeDtypeStruct(vg.shape, BF16), jax.ShapeDtypeStruct(state.shape, F32)],
        compiler_params=_cparams(("parallel",)),
        name="gla_sample",
    )(qg, kg, vg, la, rg, state, mall, role, mask, msum, gnorm)


def _post_kernel(x_ref, a_ref, gl_ref, ga_ref, gb_ref, wpa, wpg, wo, nf_ref, wr_hi, wr_lo, br, x1t_o, rt_o):
    tm = x_ref.shape[0]
    halves = [slice(0, tm // 2), slice(tm // 2, tm)] if tm % (2 * LANES) == 0 else [slice(0, tm)]
    proj = [(_dot(a_ref[r, :], wpa[...]), _dot(gl_ref[r, :], wpg[...])) for r in halves]
    x1s = []
    for r, (pa, pg) in zip(halves, proj):
        merged = ga_ref[r, :].astype(F32) * pa + gb_ref[r, :].astype(F32) * pg
        x1s.append(x_ref[r, :] + _dot(merged.astype(BF16), wo[...]))
    logits = []
    for r, x1 in zip(halves, x1s):
        rows = r.stop - r.start
        for j in range(TOKEN_ROWS):
            x1t_o[pl.ds(r.start * TOKEN_ROWS + j, rows, stride=TOKEN_ROWS), :] = x1[:, j * LANES:(j + 1) * LANES]
        h_hi, h_lo = _split_bf16(_rms(x1, nf_ref[...]))
        logits.append(_dot_nt(wr_hi[...], h_hi) + _dot_nt(wr_hi[...], h_lo) + _dot_nt(wr_lo[...], h_hi))
    nrow = ROUTER_ROWS
    big = jnp.int32(LANES)
    ninf = -jnp.inf
    for r, lt in zip(halves, logits):
        lt = lt[:nrow] + br[:nrow, 0:1]
        row = lax.broadcasted_iota(jnp.int32, lt.shape, 0)

        def first_max(vals):
            mx = jnp.max(vals, axis=0, keepdims=True)
            return mx, jnp.min(jnp.where(vals == mx, row, big), axis=0, keepdims=True)

        gl = jnp.where((row >= N_EXPERTS) & (row < N_EXPERTS + N_GROUPS), lt, ninf)
        gmax, gidx = first_max(gl)
        p_sel = 1.0 / jnp.sum(jnp.exp(gl - gmax), axis=0, keepdims=True)
        lo = (gidx - N_EXPERTS) * EXPERTS_PER_GROUP
        el = jnp.where((row >= lo) & (row < lo + EXPERTS_PER_GROUP), lt, ninf)
        v1, i1 = first_max(el)
        el2 = jnp.where(row == i1, ninf, el)
        v2, i2 = first_max(el2)
        t = jnp.exp(v2 - v1)
        w1 = p_sel / (1.0 + t)
        w2 = p_sel * t / (1.0 + t)
        row8 = lax.broadcasted_iota(jnp.int32, (rt_o.shape[0], r.stop - r.start), 0)
        pick = lambda k, val, rest: jnp.where(row8 == k, val, rest)
        rt_o[:, r] = pick(0, i1.astype(F32), pick(1, i2.astype(F32), pick(2, w1, pick(3, w2, 0.0))))


def _post(x2d, a_out, g_out, ga, gb, w, tm):
    n = x2d.shape[0]
    row = lambda i: (i, 0)
    const = lambda i: (0, 0)
    wnames = ("wpa", "wpg", "wo", "nf", "wr_hi", "wr_lo", "br")
    rt_rows = 8
    return pl.pallas_call(
        _post_kernel,
        grid=(n // tm,),
        in_specs=[pl.BlockSpec((tm, D_MODEL), row), pl.BlockSpec((tm, ATTN_WIDTH), row),
                  pl.BlockSpec((tm, GLA_VALUE_WIDTH), row), pl.BlockSpec((tm, D_MODEL), row),
                  pl.BlockSpec((tm, D_MODEL), row)] + [pl.BlockSpec(w[k].shape, const) for k in wnames],
        out_specs=[pl.BlockSpec((tm * TOKEN_ROWS, LANES), row),
                   pl.BlockSpec((rt_rows, tm), lambda i: (0, i))],
        out_shape=[jax.ShapeDtypeStruct((n * TOKEN_ROWS, LANES), F32), jax.ShapeDtypeStruct((rt_rows, n), F32)],
        compiler_params=_cparams(("parallel",)),
        name="post_mixer",
    )(x2d, a_out, g_out, ga, gb, *[w[k] for k in wnames])


def _moe_plan(rt, tme):
    n = rt.shape[1]
    ntiles = n // tme
    max_items = ntiles + N_GROUPS - 1
    i1, i2 = rt[0].astype(jnp.int32), rt[1].astype(jnp.int32)
    grp = i1 // EXPERTS_PER_GROUP
    lo = jnp.minimum(i1, i2) % EXPERTS_PER_GROUP
    hi = jnp.maximum(i1, i2) % EXPERTS_PER_GROUP
    key = (grp * EXPERTS_PER_GROUP + lo) * EXPERTS_PER_GROUP + hi
    order = jnp.argsort(key, stable=True).astype(jnp.int32)
    skey = key[order].reshape(ntiles, tme)
    sg = skey // (EXPERTS_PER_GROUP * EXPERTS_PER_GROUP)
    slo = (skey // EXPERTS_PER_GROUP) % EXPERTS_PER_GROUP
    shi = skey % EXPERTS_PER_GROUP
    ev = jnp.arange(EXPERTS_PER_GROUP)
    in_g = sg[:, :, None] == jnp.arange(N_GROUPS)
    uses_e = (slo[:, :, None] == ev) | (shi[:, :, None] == ev)
    flags_tge = jnp.any(in_g[:, :, :, None] & uses_e[:, :, None, :], axis=1)
    present = jnp.any(in_g, axis=1).reshape(-1)
    pos = jnp.cumsum(present) - 1
    n_items = pos[-1] + 1
    src = jnp.zeros((max_items,), jnp.int32).at[jnp.where(present, pos, max_items)].set(
        jnp.arange(ntiles * N_GROUPS, dtype=jnp.int32), mode="drop")
    it = jnp.arange(max_items)
    valid = it < n_items
    last_src = src[n_items - 1]
    src = jnp.where(valid, src, last_src)
    item_tile = src // N_GROUPS
    item_group = src % N_GROUPS
    prev_tile = jnp.concatenate([jnp.full((1,), -1, jnp.int32), item_tile[:-1]])
    next_tile = jnp.concatenate([item_tile[1:], jnp.full((1,), -1, jnp.int32)])
    first = valid & (item_tile != prev_tile)
    last = valid & ((item_tile != next_tile) | (it == n_items - 1))
    flags = flags_tge[item_tile, item_group] & valid[:, None]
    e0 = jnp.argmax(flags, axis=1)
    rest = flags & (ev[None, :] != e0[:, None])
    e1 = jnp.argmax(rest, axis=1)
    rest = rest & (ev[None, :] != e1[:, None])
    has2 = jnp.any(rest, axis=1)
    e2 = jnp.argmax(rest, axis=1)
    rest = rest & (ev[None, :] != e2[:, None])
    rt_sorted = rt[:, order].reshape(rt.shape[0], ntiles, tme).transpose(1, 0, 2)
    i32 = lambda z: z.astype(jnp.int32)
    plan = (order, i32(item_tile), i32(item_group), i32(first), i32(last), i32(valid), i32(e0), i32(e1), i32(e2),
            i32(has2), i32(rest.reshape(-1)))
    return plan, rt_sorted


def _moe_kernel(order, itile, igroup, ifirst, ilast, ivalid, ie0, ie1, ie2, ihas2, flags, x_hbm, rt_ref, wg, wu, wd, nffn, nfin,
                y_hbm, xbuf, ybuf, acc, hbuf, rcol, gsem, ssem, *, tme, ntiles):
    i = pl.program_id(0)
    t = itile[i]
    slot = t % 2
    g = igroup[i]
    is_first = ifirst[i] == 1

    def gather_row(tile, sl, r):
        tok = order[tile * tme + r]
        src = x_hbm.at[pl.ds(pl.multiple_of(tok * TOKEN_ROWS, TOKEN_ROWS), TOKEN_ROWS)]
        dst = xbuf.at[pl.ds(pl.multiple_of((sl * tme + r) * TOKEN_ROWS, TOKEN_ROWS), TOKEN_ROWS)]
        return pltpu.make_async_copy(src, dst, gsem.at[sl])

    def scatter_row(tile, sl, r):
        tok = order[tile * tme + r]
        return pltpu.make_async_copy(ybuf.at[sl, pl.ds(r, 1)], y_hbm.at[pl.ds(tok, 1)], ssem.at[sl])

    def start_rows(make, tile, sl):
        def body(r, c):
            make(tile, sl, r).start()
            return c
        lax.fori_loop(0, tme, body, 0, unroll=8)

    def wait_gather(sl):
        rows = tme * TOKEN_ROWS
        pltpu.make_async_copy(x_hbm.at[pl.ds(0, rows)], xbuf.at[pl.ds(pl.multiple_of(sl * rows, rows), rows)],
                              gsem.at[sl]).wait()

    def wait_scatter(sl):
        pltpu.make_async_copy(ybuf.at[sl], y_hbm.at[pl.ds(0, tme)], ssem.at[sl]).wait()

    def start_rows_inline(make, tile, sl, r0=0, r1=tme):
        for r in range(r0, r1):
            make(tile, sl, r).start()

    def expert(e):
        eid = (g * EXPERTS_PER_GROUP + e).astype(F32)
        ce = (jnp.where(rcol[:, 0:1] == eid, rcol[:, 2:3], 0.0)
              + jnp.where(rcol[:, 1:2] == eid, rcol[:, 3:4], 0.0))
        h = hbuf[...]
        a = _dot(h, wg[e])
        u = _dot(h, wu[e])
        act = (a * _sigmoid(a)) * u * ce
        acc[...] += _dot(act.astype(BF16), wd[e])

    def expert_with(e, run, alternatives):
        plain = run
        for cond, side_work in alternatives:
            @pl.when(cond)
            def _():
                side_work()
                expert(e)

            plain = jnp.logical_and(plain, jnp.logical_not(cond))

        @pl.when(plain)
        def _():
            expert(e)

    @pl.when(i == 0)
    def _():
        start_rows(gather_row, 0, 0)

    @pl.when(is_first)
    def _():
        wait_gather(slot)
        base = pl.multiple_of(slot * (tme * TOKEN_ROWS), tme * TOKEN_ROWS)
        x1 = jnp.concatenate([xbuf[pl.ds(base + j, tme, stride=TOKEN_ROWS), :] for j in range(TOKEN_ROWS)], axis=1)
        acc[...] = x1
        hbuf[...] = _rms(x1, nffn[...]).astype(BF16)
        rt = rt_ref[0]
        rcol[...] = jnp.transpose(jnp.concatenate([rt, jnp.zeros((LANES - rt.shape[0], tme), F32)], axis=0))

    valid = ivalid[i] == 1
    has2 = ihas2[i] == 1
    do_gather = jnp.logical_and(is_first, t + 1 < ntiles)
    do_scatter = jnp.logical_and(is_first, t >= 1)
    half = tme // 2
    scatter_rows = lambda r0, r1: (lambda: start_rows_inline(scatter_row, t - 1, 1 - slot, r0, r1))
    expert_with(ie0[i], valid, [(do_gather, lambda: start_rows_inline(gather_row, t + 1, 1 - slot))])
    expert_with(ie1[i], valid, [(jnp.logical_and(do_scatter, has2), scatter_rows(0, half)),
                                (jnp.logical_and(do_scatter, jnp.logical_not(has2)), scatter_rows(0, tme))])
    expert_with(ie2[i], has2, [(jnp.logical_and(do_scatter, has2), scatter_rows(half, tme))])
    for e in range(EXPERTS_PER_GROUP):
        @pl.when(flags[i * EXPERTS_PER_GROUP + e] == 1)
        def _():
            expert(e)

    @pl.when(ilast[i] == 1)
    def _():
        @pl.when(t >= 2)
        def _():
            wait_scatter(slot)

        ybuf[slot] = _rms(acc[...], nfin[...])

        @pl.when(t == ntiles - 1)
        def _():
            start_rows(scatter_row, t, slot)

    @pl.when(i == pl.num_programs(0) - 1)
    def _():
        for sl in range(min(2, ntiles)):
            wait_scatter(sl)


def _moe(x1t, rt, weg, weu, wed, nffn, nfin):
    n = x1t.shape[0] // TOKEN_ROWS
    tme = min(MOE_TILE, n)
    assert n % tme == 0
    ntiles = n // tme
    plan, rt_sorted = _moe_plan(rt, tme)
    max_items = ntiles + N_GROUPS - 1
    grp = lambda i, order, itile, igroup, *_: (igroup[i], 0, 0)
    til = lambda i, order, itile, *_: (itile[i], 0, 0)
    const = lambda i, *_: (0, 0)
    grid_spec = pltpu.PrefetchScalarGridSpec(
        num_scalar_prefetch=len(plan),
        grid=(max_items,),
        in_specs=[pl.BlockSpec(memory_space=pl.ANY),
                  pl.BlockSpec((1,) + rt_sorted.shape[1:], til),
                  pl.BlockSpec((EXPERTS_PER_GROUP, D_MODEL, EXPERT_FF), grp),
                  pl.BlockSpec((EXPERTS_PER_GROUP, D_MODEL, EXPERT_FF), grp),
                  pl.BlockSpec((EXPERTS_PER_GROUP, EXPERT_FF, D_MODEL), grp),
                  pl.BlockSpec((1, D_MODEL), const), pl.BlockSpec((1, D_MODEL), const)],
        out_specs=pl.BlockSpec(memory_space=pl.ANY),
        scratch_shapes=[pltpu.VMEM((2 * tme * TOKEN_ROWS, LANES), F32), pltpu.VMEM((2, tme, D_MODEL), F32),
                        pltpu.VMEM((tme, D_MODEL), F32), pltpu.VMEM((tme, D_MODEL), BF16),
                        pltpu.VMEM((tme, LANES), F32),
                        pltpu.SemaphoreType.DMA((2,)), pltpu.SemaphoreType.DMA((2,))],
    )
    return pl.pallas_call(
        functools.partial(_moe_kernel, tme=tme, ntiles=ntiles),
        grid_spec=grid_spec,
        out_shape=jax.ShapeDtypeStruct((n, D_MODEL), F32),
        compiler_params=_cparams(("arbitrary",)),
        name="moe",
    )(*plan, x1t, rt_sorted, weg, weu, wed, nffn, nfin)


def _rope_tables(positions):
    half = HEAD_DIM // 2
    inv_freq = ROPE_THETA ** (-jnp.arange(half, dtype=F32) / half)
    ang = positions.astype(F32)[:, None] * inv_freq[None, :]
    cos, sin = jnp.cos(ang), jnp.sin(ang)
    reps = LANES // HEAD_DIM
    return (jnp.tile(jnp.concatenate([cos, cos], -1), (1, reps)),
            jnp.tile(jnp.concatenate([-sin, sin], -1), (1, reps)))


def _prep_weights(norm_mix, w_in, w_gla_f2, b_gla_f, gla_norm, w_proj_attn, w_proj_gla, w_out, norm_ffn,
                  w_router_group, b_router_group, w_router_expert, b_router_expert):
    w = {"win": w_in.astype(BF16)}
    w["wf2"] = jnp.pad(w_gla_f2.astype(BF16), ((0, LANES - GLA_GATE_RANK), (0, 0)))
    w["bf"] = b_gla_f.reshape(1, -1)
    w["norm_mix"] = norm_mix.reshape(1, -1)
    w["gla_norm"] = gla_norm.reshape(1, -1)
    w["wpa"] = w_proj_attn.astype(BF16)
    w["wpg"] = w_proj_gla.astype(BF16)
    w["wo"] = w_out.astype(BF16)
    w["nf"] = norm_ffn.reshape(1, -1)
    pad = LANES - N_EXPERTS - N_GROUPS
    wr_t = jnp.pad(jnp.concatenate([w_router_expert, w_router_group], axis=1), ((0, 0), (0, pad))).T
    w["wr_hi"], w["wr_lo"] = _split_bf16(wr_t)
    w["br"] = jnp.broadcast_to(jnp.pad(jnp.concatenate([b_router_expert, b_router_group]), (0, pad))[:, None],
                               (LANES, LANES))
    return w


def _layer(x, positions_tab, cache, w, sinks, weg, weu, wed, nfin, tm):
    batch, seq, _ = x.shape
    n = batch * seq
    x2d = x.reshape(n, D_MODEL)
    cos_t, sin_t = positions_tab
    qa, ka, va, qg, kg, vg, rg, la, ga, gb = _inproj(x2d, w["norm_mix"], cos_t, sin_t, w, tm)
    if cache is None:
        a_out = _swa_prompt(sinks, qa, ka, va, batch, seq)
        last = lambda z: z.reshape(batch, seq, KV_WIDTH)[:, seq - WINDOW:].reshape(batch, WINDOW, ATTN_KV_HEADS, HEAD_DIM)
        new_k, new_v = last(ka), last(va)
        g_out, new_s = _gla_prompt(qg, kg, vg, la, rg, w["gla_norm"], batch, seq)
    else:
        cache_k, cache_v, state = cache
        a_out, new_k, new_v = _swa_sample(sinks, qa, ka, va, cache_k.reshape(batch, WINDOW, KV_WIDTH),
                                          cache_v.reshape(batch, WINDOW, KV_WIDTH), batch, seq)
        new_k = new_k.reshape(batch, WINDOW, ATTN_KV_HEADS, HEAD_DIM)
        new_v = new_v.reshape(batch, WINDOW, ATTN_KV_HEADS, HEAD_DIM)
        g_out, new_s = _gla_sample(qg, kg, vg, la, rg, state, w["gla_norm"], batch, seq)
    x1t, rt = _post(x2d, a_out, g_out, ga, gb, w, tm)
    y = _moe(x1t, rt, weg, weu, wed, w["nf"], nfin)
    return y.reshape(batch, seq, D_MODEL), new_k, new_v, new_s


def kernel(x_prompt, x_sample, cache_win_k, cache_win_v, state_gla, norm_mix, w_in, w_gla_f2, b_gla_f, gla_norm,
           attn_sinks, w_proj_attn, w_proj_gla, w_out, norm_ffn, w_router_group, b_router_group, w_router_expert,
           b_router_expert, w_exp_gate, w_exp_up, w_exp_down, norm_final):
    assert norm_mix.shape[0] == 1, "single-layer step"
    seq_p = x_prompt.shape[1]
    dec_b, dec_t = x_sample.shape[0], x_sample.shape[1]
    w = _prep_weights(norm_mix[0], w_in[0], w_gla_f2[0], b_gla_f[0], gla_norm[0], w_proj_attn[0], w_proj_gla[0],
                      w_out[0], norm_ffn[0], w_router_group[0], b_router_group[0], w_router_expert[0],
                      b_router_expert[0])
    weg = w_exp_gate[0].astype(BF16)
    weu = w_exp_up[0].astype(BF16)
    wed = w_exp_down[0].astype(BF16)
    nfin = norm_final.reshape(1, -1)
    sinks = attn_sinks[0]
    tab_p = _rope_tables(jnp.arange(seq_p, dtype=jnp.int32))
    pos_s = PAST_LEN + jnp.arange(dec_t, dtype=jnp.int32)
    tab_s = tuple(jnp.tile(t, (dec_b, 1)) for t in _rope_tables(pos_s))
    tm_p = min(512, seq_p)
    tm_s = dec_b * dec_t
    yp, pk, pv, ps = _layer(x_prompt, tab_p, None, w, sinks, weg, weu, wed, nfin, tm_p)
    ys, sk, sv, ss = _layer(x_sample, tab_s, (cache_win_k[0], cache_win_v[0], state_gla[0]), w, sinks, weg, weu, wed,
                            nfin, tm_s)
    return (yp, ys, pk[None], pv[None], ps[None], sk[None], sv[None], ss[None])
```

```python
import functools
import math

import numpy as np
import jax
import jax.numpy as jnp
from jax import lax
from jax.experimental import pallas as pl
from jax.experimental.pallas import tpu as pltpu

F32 = jnp.float32
BF16 = jnp.bfloat16

D_MODEL = 1024
ATTN_HEADS = 8
ATTN_KV_HEADS = 2
GROUP = ATTN_HEADS // ATTN_KV_HEADS
HEAD_DIM = 64
ATTN_WIDTH = ATTN_HEADS * HEAD_DIM
KV_WIDTH = ATTN_KV_HEADS * HEAD_DIM
WINDOW = 128
ROPE_THETA = 10000.0
PAST_LEN = 8192
GLA_HEADS = 4
GLA_KEY_WIDTH = D_MODEL // 2
GLA_VALUE_WIDTH = D_MODEL
GLA_DK = GLA_KEY_WIDTH // GLA_HEADS
GLA_DV = GLA_VALUE_WIDTH // GLA_HEADS
GLA_GATE_RANK = 16
GLA_GATE_NORMALIZER = 16.0
N_GROUPS = 4
EXPERTS_PER_GROUP = 8
N_EXPERTS = N_GROUPS * EXPERTS_PER_GROUP
EXPERT_FF = 256
EPS = 1e-6
LOG2_E = 1.4426950408889634

LANES = 128
GLA_CHUNK = 128
GLA_HEADS_PER_STEP = 4
SWA_BLOCKS_PER_STEP = 4
SAMPLE_SEQ_BLOCK = 8
MOE_TILE = 256
TOKEN_ROWS = D_MODEL // LANES
INPROJ_WIDTHS = (ATTN_WIDTH, 2 * KV_WIDTH, GLA_KEY_WIDTH, GLA_KEY_WIDTH, GLA_VALUE_WIDTH, GLA_VALUE_WIDTH)
ROUTER_ROWS = 40
VMEM_LIMIT = 56 * 1024 * 1024


def _cparams(sem):
    return pltpu.CompilerParams(dimension_semantics=sem, vmem_limit_bytes=VMEM_LIMIT)


def _rms(x, g):
    return x * lax.rsqrt(jnp.mean(x * x, axis=-1, keepdims=True) + EPS) * g


def _sigmoid(x):
    return 1.0 / (1.0 + jnp.exp(-x))


def _dot(a, b):
    return jnp.dot(a, b, preferred_element_type=F32)


def _dot_nt(a, b):
    return lax.dot_general(a, b, (((1,), (1,)), ((), ())), preferred_element_type=F32)


def _dot_tn(a, b):
    return lax.dot_general(a, b, (((0,), (0,)), ((), ())), preferred_element_type=F32)


def _split_bf16(x):
    hi = x.astype(BF16)
    lo = (x - hi.astype(F32)).astype(BF16)
    return hi, lo


def _inproj_kernel(x_ref, g_ref, cos_ref, sin_ref, win, wf2, bf,
                   qa_o, ka_o, va_o, qg_o, kg_o, vg_o, rg_o, la_o, ga_o, gb_o, wgate):
    f0 = sum(INPROJ_WIDTHS)

    @pl.when(pl.program_id(0) == 0)
    def _():
        tail = win[:, f0:]
        wgate[...] = tail[:, GLA_GATE_RANK:GLA_GATE_RANK + 2 * D_MODEL]

    tm = x_ref.shape[0]
    halves = [slice(0, tm // 2), slice(tm // 2, tm)] if tm % 16 == 0 else [slice(0, tm)]
    hbs = [_rms(x_ref[r, :], g_ref[...]).astype(BF16) for r in halves]
    cols = np.cumsum((0,) + INPROJ_WIDTHS[:-1])
    wqa, wkva, wqg, wkg, wvg, wrg = (win.at[:, int(c):int(c) + wd] for c, wd in zip(cols, INPROJ_WIDTHS))
    wga = wgate.at[:, :D_MODEL]
    wgb = wgate.at[:, D_MODEL:]
    wf = win.at[:, f0:f0 + LANES]
    lane = lax.broadcasted_iota(jnp.int32, (halves[0].stop - halves[0].start, LANES), 1)
    first_half = (lane % HEAD_DIM) < (HEAD_DIM // 2)

    def rope(t, r):
        swapped = jnp.where(first_half, pltpu.roll(t, LANES - HEAD_DIM // 2, 1), pltpu.roll(t, HEAD_DIM // 2, 1))
        return t * cos_ref[r, :] + swapped * sin_ref[r, :]

    for r, hb in zip(halves, hbs):
        qa = _dot(hb, wqa[...])
        for c in range(ATTN_WIDTH // LANES):
            sl = slice(c * LANES, (c + 1) * LANES)
            qa_o[r, sl] = (rope(qa[:, sl], r) * (HEAD_DIM ** -0.5)).astype(BF16)
    for r, hb in zip(halves, hbs):
        kva = _dot(hb, wkva[...])
        ka_o[r, :] = rope(kva[:, :KV_WIDTH], r)
        va_o[r, :] = kva[:, KV_WIDTH:]
    for r, hb in zip(halves, hbs):
        qg_o[r, :] = (_dot(hb, wqg[...]) * (GLA_DK ** -0.5)).astype(BF16)
    for r, hb in zip(halves, hbs):
        kg_o[r, :] = _dot(hb, wkg[...]).astype(BF16)
    for r, hb in zip(halves, hbs):
        vg_o[r, :] = _dot(hb, wvg[...]).astype(BF16)
    for r, hb in zip(halves, hbs):
        rg_o[r, :] = _dot(hb, wrg[...]).astype(BF16)
    for r, hb in zip(halves, hbs):
        z = _dot(_dot(hb, wf[...]).astype(BF16), wf2[...]) + bf[...]
        la_o[r, :] = (jnp.minimum(z, 0.0) - jnp.log1p(jnp.exp(-jnp.abs(z)))) * (1.0 / GLA_GATE_NORMALIZER)
    for r, hb in zip(halves, hbs):
        ga_o[r, :] = _sigmoid(_dot(hb, wga[...])).astype(BF16)
    for r, hb in zip(halves, hbs):
        gb_o[r, :] = _sigmoid(_dot(hb, wgb[...])).astype(BF16)


def _inproj(x2d, norm_g, cos_t, sin_t, w, tm):
    n = x2d.shape[0]
    ntab = cos_t.shape[0] // tm
    row = lambda i: (i, 0)
    const = lambda i: (0, 0)
    tab = lambda i: (i % ntab, 0)
    wnames = ("win", "wf2", "bf")
    wspecs = [pl.BlockSpec(w[k].shape, const) for k in wnames]
    widths = (ATTN_WIDTH, KV_WIDTH, KV_WIDTH, GLA_KEY_WIDTH, GLA_KEY_WIDTH, GLA_VALUE_WIDTH, GLA_VALUE_WIDTH,
              GLA_KEY_WIDTH, D_MODEL, D_MODEL)
    dtypes = (BF16, F32, F32, BF16, BF16, BF16, BF16, F32, BF16, BF16)
    return pl.pallas_call(
        _inproj_kernel,
        grid=(n // tm,),
        in_specs=[pl.BlockSpec((tm, D_MODEL), row), pl.BlockSpec((1, D_MODEL), const),
                  pl.BlockSpec((tm, LANES), tab), pl.BlockSpec((tm, LANES), tab)] + wspecs,
        out_specs=[pl.BlockSpec((tm, wd), row) for wd in widths],
        out_shape=[jax.ShapeDtypeStruct((n, wd), dt) for wd, dt in zip(widths, dtypes)],
        scratch_shapes=[pltpu.VMEM((D_MODEL, 2 * D_MODEL), BF16)],
        compiler_params=_cparams(("arbitrary",)),
        name="inproj",
    )(x2d, norm_g, cos_t, sin_t, *[w[k] for k in wnames])


def _swa_prompt_kernel(sink_ref, q_ref, kc_ref, kp_ref, vc_ref, vp_ref, o_ref, *, qb):
    n = pl.program_id(1)
    k3 = jnp.concatenate([kp_ref[...], kc_ref[...]], axis=0).astype(BF16)
    v3 = jnp.concatenate([vp_ref[...], vc_ref[...]], axis=0).astype(BF16)
    t = lax.broadcasted_iota(jnp.int32, (WINDOW, 2 * WINDOW), 0)
    j = lax.broadcasted_iota(jnp.int32, (WINDOW, 2 * WINDOW), 1)
    band = (j >= t) & (j <= t + WINDOW)
    zeros = jnp.zeros((2 * WINDOW, HEAD_DIM), BF16)
    ones = jnp.ones((2 * WINDOW, HEAD_DIM), BF16)
    lane = lax.broadcasted_iota(jnp.int32, (WINDOW, LANES), 1)
    chains = [(blk, h) for blk in range(qb) for h in range(ATTN_HEADS)]
    scores = []
    for blk, h in chains:
        kv = h // GROUP
        rows = slice(blk * WINDOW, (blk + 1) * WINDOW)
        keys = slice(blk * WINDOW, (blk + 2) * WINDOW)
        s = _dot_nt(q_ref[rows, h * HEAD_DIM:(h + 1) * HEAD_DIM], k3[keys, kv * HEAD_DIM:(kv + 1) * HEAD_DIM])
        valid = band & ((j >= WINDOW) | (n > 0)) if blk == 0 else band
        scores.append(jnp.where(valid, s, -jnp.inf))
    probs, sink_terms = [], []
    for (blk, h), s in zip(chains, scores):
        m = jnp.maximum(jnp.max(s, axis=-1, keepdims=True), sink_ref[h])
        probs.append(jnp.exp(s - m).astype(BF16))
        sink_terms.append(jnp.exp(sink_ref[h] - m))
    for blk in range(qb):
        rows = slice(blk * WINDOW, (blk + 1) * WINDOW)
        keys = slice(blk * WINDOW, (blk + 2) * WINDOW)
        for kv in range(ATTN_KV_HEADS):
            vv = v3[keys, kv * HEAD_DIM:(kv + 1) * HEAD_DIM]
            vext = (jnp.concatenate([vv, zeros, ones, zeros], axis=1), jnp.concatenate([zeros, vv, zeros, ones], axis=1))
            for pr in range(GROUP // 2):
                h0 = kv * GROUP + pr * 2
                c0 = blk * ATTN_HEADS + h0
                acc = _dot(probs[c0], vext[0]) + _dot(probs[c0 + 1], vext[1])
                l = acc[:, LANES:] + jnp.where(lane < HEAD_DIM, sink_terms[c0], sink_terms[c0 + 1])
                o_ref[rows, h0 * HEAD_DIM:h0 * HEAD_DIM + LANES] = (acc[:, :LANES] / l).astype(BF16)


def _swa_prompt(sinks, qa, ka, va, batch, seq):
    nb = seq // WINDOW
    qb = math.gcd(SWA_BLOCKS_PER_STEP, nb)
    steps = nb // qb
    cur = lambda b, n: (b * steps + n, 0)
    prev = lambda b, n: (b * nb + jnp.maximum(n * qb - 1, 0), 0)
    return pl.pallas_call(
        functools.partial(_swa_prompt_kernel, qb=qb),
        grid=(batch, steps),
        in_specs=[pl.BlockSpec(memory_space=pltpu.SMEM),
                  pl.BlockSpec((qb * WINDOW, ATTN_WIDTH), cur),
                  pl.BlockSpec((qb * WINDOW, KV_WIDTH), cur), pl.BlockSpec((WINDOW, KV_WIDTH), prev),
                  pl.BlockSpec((qb * WINDOW, KV_WIDTH), cur), pl.BlockSpec((WINDOW, KV_WIDTH), prev)],
        out_specs=pl.BlockSpec((qb * WINDOW, ATTN_WIDTH), cur),
        out_shape=jax.ShapeDtypeStruct(qa.shape, BF16),
        compiler_params=_cparams(("parallel", "parallel")),
        name="swa_prompt",
    )(sinks, qa, ka, ka, va, va)


def _swa_sample_kernel(sink_ref, q_ref, kn_ref, vn_ref, kn3_ref, vn3_ref, ck_ref, cv_ref, o_ref, nk_ref, nv_ref,
                       *, t_new):
    sb = ck_ref.shape[0]
    spv = 8 // t_new
    nq = GROUP * 8
    nc = spv * WINDOW
    qi = lax.broadcasted_iota(jnp.int32, (nq, nc), 0) % 8
    ci = lax.broadcasted_iota(jnp.int32, (nq, nc), 1)
    valid_c = (qi // t_new == ci // WINDOW) & (ci % WINDOW >= qi % t_new)
    qn = lax.broadcasted_iota(jnp.int32, (nq, 8), 0) % 8
    cn = lax.broadcasted_iota(jnp.int32, (nq, 8), 1)
    valid_n = (qn // t_new == cn // t_new) & (cn <= qn)
    grow = lax.broadcasted_iota(jnp.int32, (nq, 1), 0) // 8
    chains = [(vr, kv) for vr in range(sb // spv) for kv in range(ATTN_KV_HEADS)]
    scored = []
    for vr, kv in chains:
        r8 = slice(8 * vr, 8 * vr + 8)
        cs = slice(kv * HEAD_DIM, (kv + 1) * HEAD_DIM)
        heads = [kv * GROUP + g for g in range(GROUP)]
        qs = jnp.concatenate([q_ref[r8, h * HEAD_DIM:(h + 1) * HEAD_DIM] for h in heads], axis=0)
        ck = ck_ref[vr * spv:(vr + 1) * spv, :, cs].reshape(nc, HEAD_DIM).astype(BF16)
        kn = kn_ref[r8, cs].astype(BF16)
        scored.append((jnp.where(valid_c, _dot_nt(qs, ck), -jnp.inf), jnp.where(valid_n, _dot_nt(qs, kn), -jnp.inf)))
    soft = []
    for (vr, kv), (s_c, s_n) in zip(chains, scored):
        sink = sink_ref[kv * GROUP]
        for g in range(1, GROUP):
            sink = jnp.where(grow == g, sink_ref[kv * GROUP + g], sink)
        m = jnp.maximum(jnp.maximum(jnp.max(s_c, axis=-1, keepdims=True), jnp.max(s_n, axis=-1, keepdims=True)), sink)
        p_c = jnp.exp(s_c - m)
        p_n = jnp.exp(s_n - m)
        l = jnp.sum(p_c, axis=-1, keepdims=True) + jnp.sum(p_n, axis=-1, keepdims=True) + jnp.exp(sink - m)
        soft.append((p_c.astype(BF16), p_n.astype(BF16), l))
    for (vr, kv), (p_c, p_n, l) in zip(chains, soft):
        r8 = slice(8 * vr, 8 * vr + 8)
        cs = slice(kv * HEAD_DIM, (kv + 1) * HEAD_DIM)
        cv = cv_ref[vr * spv:(vr + 1) * spv, :, cs].reshape(nc, HEAD_DIM).astype(BF16)
        o = (_dot(p_c, cv) + _dot(p_n, vn_ref[r8, cs].astype(BF16))) / l
        for a in range(GROUP // 2):
            pair = jnp.concatenate([o[16 * a:16 * a + 8], o[16 * a + 8:16 * a + 16]], axis=1)
            c0 = (kv * GROUP + 2 * a) * HEAD_DIM
            o_ref[r8, c0:c0 + LANES] = pair.astype(BF16)
    nk_ref[:, 0:WINDOW - t_new, :] = ck_ref[:, t_new:WINDOW, :]
    nk_ref[:, WINDOW - t_new:WINDOW, :] = kn3_ref[...]
    nv_ref[:, 0:WINDOW - t_new, :] = cv_ref[:, t_new:WINDOW, :]
    nv_ref[:, WINDOW - t_new:WINDOW, :] = vn3_ref[...]


def _swa_sample(sinks, qa, ka, va, cache_k, cache_v, batch, t_new):
    sb = SAMPLE_SEQ_BLOCK
    rows = sb * t_new
    r2 = lambda i: (i, 0)
    r3 = lambda i: (i, 0, 0)
    ka3 = ka.reshape(batch, t_new, KV_WIDTH)
    va3 = va.reshape(batch, t_new, KV_WIDTH)
    return pl.pallas_call(
        functools.partial(_swa_sample_kernel, t_new=t_new),
        grid=(batch // sb,),
        in_specs=[pl.BlockSpec(memory_space=pltpu.SMEM),
                  pl.BlockSpec((rows, ATTN_WIDTH), r2),
                  pl.BlockSpec((rows, KV_WIDTH), r2), pl.BlockSpec((rows, KV_WIDTH), r2),
                  pl.BlockSpec((sb, t_new, KV_WIDTH), r3), pl.BlockSpec((sb, t_new, KV_WIDTH), r3),
                  pl.BlockSpec((sb, WINDOW, KV_WIDTH), r3), pl.BlockSpec((sb, WINDOW, KV_WIDTH), r3)],
        out_specs=[pl.BlockSpec((rows, ATTN_WIDTH), r2),
                   pl.BlockSpec((sb, WINDOW, KV_WIDTH), r3), pl.BlockSpec((sb, WINDOW, KV_WIDTH), r3)],
        out_shape=[jax.ShapeDtypeStruct(qa.shape, BF16),
                   jax.ShapeDtypeStruct(cache_k.shape, F32), jax.ShapeDtypeStruct(cache_v.shape, F32)],
        compiler_params=_cparams(("parallel",)),
        name="swa_sample",
    )(sinks, qa, ka, va, ka3, va3, cache_k, cache_v)


def _gla_constants(c, seg, with_rem):
    t = np.arange(c)
    sid = t // seg
    same = sid[:, None] == sid[None, :]
    levels = []
    m = seg // 2
    while m >= 1:
        levels.append(m)
        m //= 2
    mats, roles, masks = [], [], []
    for m in levels:
        blk = t // (2 * m)
        second = (t // m) % 2 == 1
        p = blk * 2 * m + m - 1
        u = t[None, :]
        mq = (u > p[:, None]) & (u <= t[:, None])
        mk = (u > t[:, None]) & (u <= p[:, None])
        if m > 1:
            mats.append(np.where(second[:, None], mq, mk))
        roles.append(np.broadcast_to(second[:, None], (c, LANES)))
        masks.append((blk[:, None] == blk[None, :]) & second[:, None] & ~second[None, :])
    masks.append(np.eye(c, dtype=bool))
    mats.append(same & (t[None, :] <= t[:, None]))
    if with_rem:
        mats.append(same & (t[None, :] > t[:, None]))
    mall = np.concatenate(mats, 0).astype(np.float32)
    mall = jnp.asarray(np.concatenate([mall, mall], 1), BF16)
    role = jnp.asarray(np.concatenate(roles, 0).astype(np.float32))
    mask = jnp.asarray(np.concatenate(masks, 0).astype(np.float32))
    return len(levels), mall, role, mask


def _gla_exponents(la, mall):
    la2 = la * LOG2_E
    hl = jnp.concatenate(_split_bf16(la2), axis=0)
    return _dot(mall, hl), hl, la2


def _gla_scores(qb, kb, e2, la2, role_ref, mask_ref, nlev, c):
    qf = qb.astype(F32)
    kf = kb.astype(F32)
    terms = []
    for lv in range(nlev):
        sl = slice(lv * c, (lv + 1) * c)
        m = 1 << (nlev - 1 - lv)
        if m % 8 == 0:
            pe = jnp.exp2(e2[sl])
            x = jnp.concatenate([(qf if blk % 2 else kf)[blk * m:(blk + 1) * m] * pe[blk * m:(blk + 1) * m]
                                 for blk in range(c // m)], axis=0).astype(BF16)
        else:
            second = role_ref[sl, :] > 0.5
            e = e2[sl] if lv < nlev - 1 else jnp.where(second, la2, 0.0)
            x = (jnp.where(second, qf, kf) * jnp.exp2(e)).astype(BF16)
        terms.append((x, x, lv))
    terms.append((qb, kb, nlev))
    mask = lambda i: mask_ref[i * c:(i + 1) * c, :]
    a = None
    pair = c % LANES == 0
    while terms:
        if pair and len(terms) >= 2:
            (l0, r0, i0), (l1, r1, i1) = terms.pop(), terms.pop()
            z = jnp.zeros_like(r0)
            rhs = jnp.concatenate([jnp.concatenate([r0, z], axis=1), jnp.concatenate([z, r1], axis=1)], axis=0)
            g = _dot_nt(jnp.concatenate([l0, l1], axis=1), rhs)
            t = mask(i0) * g[:, :c] + mask(i1) * g[:, c:]
        else:
            l0, r0, i0 = terms.pop()
            t = mask(i0) * _dot_nt(l0, r0)
        a = t if a is None else a + t
    return a, qf, kf


def _gla_out(o, g, r):
    r = r.astype(F32)
    return (_rms(o, g) * (r * _sigmoid(r))).astype(BF16)


def _gla_prompt_kernel(q_ref, k_ref, v_ref, la_ref, r_ref, mall_ref, role_ref, mask_ref, g_ref, o_ref, s_ref,
                       s_scr, *, nlev, nchunks):
    c = GLA_CHUNK
    hp = GLA_HEADS_PER_STEP
    s_scr[...] = jnp.zeros_like(s_scr)

    def chunk(i, carry):
        rows = pl.ds(pl.multiple_of(i * c, c), c)
        e2_all, _, la2_all = _gla_exponents(la_ref[rows, :], mall_ref[...])
        ksl = [slice(h * GLA_DK, (h + 1) * GLA_DK) for h in range(hp)]
        vsl = [slice(h * GLA_DV, (h + 1) * GLA_DV) for h in range(hp)]
        inter = []
        for h in range(hp):
            e2 = e2_all[:, ksl[h]]
            v = v_ref[rows, vsl[h]]
            qf = q_ref[rows, ksl[h]].astype(F32)
            kf = k_ref[rows, ksl[h]].astype(F32)
            s = s_scr[h]
            b = e2[(nlev - 1) * c:nlev * c]
            inter.append(_dot((qf * jnp.exp2(b)).astype(BF16), s.astype(BF16)))
            kt = (kf * jnp.exp2(b[c - 1:c, :] - b)).astype(BF16)
            dec = jnp.exp2(jnp.transpose(b[c - 8:c, :]))[:, 7:8]
            s_scr[h] = dec * s + _dot_tn(kt, v)
        scores = [_gla_scores(q_ref[rows, ksl[h]], k_ref[rows, ksl[h]], e2_all[:, ksl[h]], la2_all[:, ksl[h]], role_ref,
                              mask_ref, nlev, c)[0] for h in range(hp)]
        for h in range(hp):
            o = _dot(scores[h].astype(BF16), v_ref[rows, vsl[h]]) + inter[h]
            o_ref[rows, vsl[h]] = _gla_out(o, g_ref[...], r_ref[rows, vsl[h]])
        return carry

    lax.fori_loop(0, nchunks, chunk, 0)
    s_ref[0] = s_scr[...]


def _gla_prompt(qg, kg, vg, la, rg, gnorm, batch, seq):
    nlev, mall, role, mask = _gla_constants(GLA_CHUNK, GLA_CHUNK, with_rem=False)
    hp = GLA_HEADS_PER_STEP
    bh = lambda b, h: (b, h)
    const = lambda b, h: (0, 0)
    return pl.pallas_call(
        functools.partial(_gla_prompt_kernel, nlev=nlev, nchunks=seq // GLA_CHUNK),
        grid=(batch, GLA_HEADS // hp),
        in_specs=[pl.BlockSpec((seq, hp * GLA_DK), bh), pl.BlockSpec((seq, hp * GLA_DK), bh),
                  pl.BlockSpec((seq, hp * GLA_DV), bh), pl.BlockSpec((seq, hp * GLA_DK), bh),
                  pl.BlockSpec((seq, hp * GLA_DV), bh),
                  pl.BlockSpec(mall.shape, const), pl.BlockSpec(role.shape, const), pl.BlockSpec(mask.shape, const),
                  pl.BlockSpec((1, GLA_DV), const)],
        out_specs=[pl.BlockSpec((seq, hp * GLA_DV), bh),
                   pl.BlockSpec((1, hp, GLA_DK, GLA_DV), lambda b, h: (b, h, 0, 0))],
        out_shape=[jax.ShapeDtypeStruct(vg.shape, BF16),
                   jax.ShapeDtypeStruct((batch, GLA_HEADS, GLA_DK, GLA_DV), F32)],
        scratch_shapes=[pltpu.VMEM((hp, GLA_DK, GLA_DV), F32)],
        compiler_params=_cparams(("parallel", "parallel")),
        name="gla_prompt",
    )(qg, kg, vg, la, rg, mall, role, mask, gnorm)


def _gla_sample_kernel(q_ref, k_ref, v_ref, la_ref, r_ref, s0_ref, mall_ref, role_ref, mask_ref, msum_ref, g_ref,
                       o_ref, s_ref, *, nlev, t_new):
    sb = s0_ref.shape[0]
    c = sb * t_new
    spv = 8 // t_new
    e2_all, hl_all, la2_all = _gla_exponents(la_ref[...], mall_ref[...])
    seq_in_tile = lax.broadcasted_iota(jnp.int32, (8, GLA_DV), 0) // t_new
    seq_of_row = lax.broadcasted_iota(jnp.int32, (c, GLA_DV), 0) // t_new
    ksl = [slice(h * GLA_DK, (h + 1) * GLA_DK) for h in range(GLA_HEADS)]
    scores = [_gla_scores(q_ref[:, ksl[h]], k_ref[:, ksl[h]], e2_all[:, ksl[h]], la2_all[:, ksl[h]], role_ref, mask_ref,
                          nlev, c) for h in range(GLA_HEADS)]
    for h in range(GLA_HEADS):
        ks = ksl[h]
        vs = slice(h * GLA_DV, (h + 1) * GLA_DV)
        v = v_ref[:, vs]
        e2 = e2_all[:, ks]
        hl = hl_all[:, ks]
        a, qf, kf = scores[h]
        qe = (qf * jnp.exp2(e2[(nlev - 1) * c:nlev * c])).astype(BF16)
        kt_t = jnp.transpose(kf * jnp.exp2(e2[nlev * c:(nlev + 1) * c])).astype(BF16)
        bl_t = jnp.transpose(_dot(msum_ref[...], hl[:c]) + _dot(msum_ref[...], hl[c:]))
        dec_t = jnp.exp2(bl_t)
        inter = []
        for vr in range(c // 8):
            rows8 = qe[8 * vr:8 * vr + 8]
            tile = None
            for u in range(spv):
                j = vr * spv + u
                s0 = s0_ref[j, h]
                r = _dot(rows8, s0.astype(BF16))
                tile = r if tile is None else jnp.where(seq_in_tile == u, r, tile)
                vj = jnp.where(seq_of_row == j, v, jnp.zeros_like(v))
                s_ref[j, h] = dec_t[:, j:j + 1] * s0 + _dot(kt_t, vj)
            inter.append(tile)
        o = _dot(a.astype(BF16), v) + jnp.concatenate(inter, axis=0)
        o_ref[:, vs] = _gla_out(o, g_ref[...], r_ref[:, vs])


def _gla_sample(qg, kg, vg, la, rg, state, gnorm, batch, t_new):
    sb = SAMPLE_SEQ_BLOCK
    c = sb * t_new
    assert 8 % t_new == 0 and c % 8 == 0
    nlev, mall, role, mask = _gla_constants(c, t_new, with_rem=True)
    msum = jnp.asarray((np.arange(c)[None, :] // t_new == np.arange(sb)[:, None]).astype(np.float32), BF16)
    rows = lambda i: (i, 0)
    const = lambda i: (0, 0)
    st = lambda i: (i, 0, 0, 0)
    return pl.pallas_call(
        functools.partial(_gla_sample_kernel, nlev=nlev, t_new=t_new),
        grid=(batch // sb,),
        in_specs=[pl.BlockSpec((c, GLA_KEY_WIDTH), rows), pl.BlockSpec((c, GLA_KEY_WIDTH), rows),
                  pl.BlockSpec((c, GLA_VALUE_WIDTH), rows), pl.BlockSpec((c, GLA_KEY_WIDTH), rows),
                  pl.BlockSpec((c, GLA_VALUE_WIDTH), rows),
                  pl.BlockSpec((sb, GLA_HEADS, GLA_DK, GLA_DV), st),
                  pl.BlockSpec(mall.shape, const), pl.BlockSpec(role.shape, const), pl.BlockSpec(mask.shape, const),
                  pl.BlockSpec(msum.shape, const), pl.BlockSpec((1, GLA_DV), const)],
        out_specs=[pl.BlockSpec((c, GLA_VALUE_WIDTH), rows), pl.BlockSpec((sb, GLA_HEADS, GLA_DK, GLA_DV), st)],
        out_shape=[jax.ShapeDtypeStruct(vg.shape, BF16), jax.ShapeDtypeStruct(state.shape, F32)],
        compiler_params=_cparams(("parallel",)),
        name="gla_sample",
    )(qg, kg, vg, la, rg, state, mall, role, mask, msum, gnorm)


def _post_kernel(x_ref, a_ref, gl_ref, ga_ref, gb_ref, wpa, wpg, wo, nf_ref, wr_hi, wr_lo, br, x1t_o, rt_o):
    tm = x_ref.shape[0]
    halves = [slice(0, tm // 2), slice(tm // 2, tm)] if tm % (2 * LANES) == 0 else [slice(0, tm)]
    proj = [(_dot(a_ref[r, :], wpa[...]), _dot(gl_ref[r, :], wpg[...])) for r in halves]
    x1s = []
    for r, (pa, pg) in zip(halves, proj):
        merged = ga_ref[r, :].astype(F32) * pa + gb_ref[r, :].astype(F32) * pg
        x1s.append(x_ref[r, :] + _dot(merged.astype(BF16), wo[...]))
    logits = []
    for r, x1 in zip(halves, x1s):
        rows = r.stop - r.start
        for j in range(TOKEN_ROWS):
            x1t_o[pl.ds(r.start * TOKEN_ROWS + j, rows, stride=TOKEN_ROWS), :] = x1[:, j * LANES:(j + 1) * LANES]
        h_hi, h_lo = _split_bf16(_rms(x1, nf_ref[...]))
        logits.append(_dot_nt(wr_hi[...], h_hi) + _dot_nt(wr_hi[...], h_lo) + _dot_nt(wr_lo[...], h_hi))
    nrow = ROUTER_ROWS
    big = jnp.int32(LANES)
    ninf = -jnp.inf
    for r, lt in zip(halves, logits):
        lt = lt[:nrow] + br[:nrow, 0:1]
        row = lax.broadcasted_iota(jnp.int32, lt.shape, 0)

        def first_max(vals):
            mx = jnp.max(vals, axis=0, keepdims=True)
            return mx, jnp.min(jnp.where(vals == mx, row, big), axis=0, keepdims=True)

        gl = jnp.where((row >= N_EXPERTS) & (row < N_EXPERTS + N_GROUPS), lt, ninf)
        gmax, gidx = first_max(gl)
        p_sel = 1.0 / jnp.sum(jnp.exp(gl - gmax), axis=0, keepdims=True)
        lo = (gidx - N_EXPERTS) * EXPERTS_PER_GROUP
        el = jnp.where((row >= lo) & (row < lo + EXPERTS_PER_GROUP), lt, ninf)
        v1, i1 = first_max(el)
        el2 = jnp.where(row == i1, ninf, el)
        v2, i2 = first_max(el2)
        t = jnp.exp(v2 - v1)
        w1 = p_sel / (1.0 + t)
        w2 = p_sel * t / (1.0 + t)
        row8 = lax.broadcasted_iota(jnp.int32, (rt_o.shape[0], r.stop - r.start), 0)
        pick = lambda k, val, rest: jnp.where(row8 == k, val, rest)
        rt_o[:, r] = pick(0, i1.astype(F32), pick(1, i2.astype(F32), pick(2, w1, pick(3, w2, 0.0))))


def _post(x2d, a_out, g_out, ga, gb, w, tm):
    n = x2d.shape[0]
    row = lambda i: (i, 0)
    const = lambda i: (0, 0)
    wnames = ("wpa", "wpg", "wo", "nf", "wr_hi", "wr_lo", "br")
    rt_rows = 8
    return pl.pallas_call(
        _post_kernel,
        grid=(n // tm,),
        in_specs=[pl.BlockSpec((tm, D_MODEL), row), pl.BlockSpec((tm, ATTN_WIDTH), row),
                  pl.BlockSpec((tm, GLA_VALUE_WIDTH), row), pl.BlockSpec((tm, D_MODEL), row),
                  pl.BlockSpec((tm, D_MODEL), row)] + [pl.BlockSpec(w[k].shape, const) for k in wnames],
        out_specs=[pl.BlockSpec((tm * TOKEN_ROWS, LANES), row),
                   pl.BlockSpec((rt_rows, tm), lambda i: (0, i))],
        out_shape=[jax.ShapeDtypeStruct((n * TOKEN_ROWS, LANES), F32), jax.ShapeDtypeStruct((rt_rows, n), F32)],
        compiler_params=_cparams(("parallel",)),
        name="post_mixer",
    )(x2d, a_out, g_out, ga, gb, *[w[k] for k in wnames])


def _moe_plan(rt, tme):
    n = rt.shape[1]
    ntiles = n // tme
    max_items = ntiles + N_GROUPS - 1
    i1, i2 = rt[0].astype(jnp.int32), rt[1].astype(jnp.int32)
    grp = i1 // EXPERTS_PER_GROUP
    lo = jnp.minimum(i1, i2) % EXPERTS_PER_GROUP
    hi = jnp.maximum(i1, i2) % EXPERTS_PER_GROUP
    key = (grp * EXPERTS_PER_GROUP + lo) * EXPERTS_PER_GROUP + hi
    order = jnp.argsort(key, stable=True).astype(jnp.int32)
    skey = key[order].reshape(ntiles, tme)
    sg = skey // (EXPERTS_PER_GROUP * EXPERTS_PER_GROUP)
    slo = (skey // EXPERTS_PER_GROUP) % EXPERTS_PER_GROUP
    shi = skey % EXPERTS_PER_GROUP
    ev = jnp.arange(EXPERTS_PER_GROUP)
    in_g = sg[:, :, None] == jnp.arange(N_GROUPS)
    uses_e = (slo[:, :, None] == ev) | (shi[:, :, None] == ev)
    flags_tge = jnp.any(in_g[:, :, :, None] & uses_e[:, :, None, :], axis=1)
    present = jnp.any(in_g, axis=1).reshape(-1)
    pos = jnp.cumsum(present) - 1
    n_items = pos[-1] + 1
    src = jnp.zeros((max_items,), jnp.int32).at[jnp.where(present, pos, max_items)].set(
        jnp.arange(ntiles * N_GROUPS, dtype=jnp.int32), mode="drop")
    it = jnp.arange(max_items)
    valid = it < n_items
    last_src = src[n_items - 1]
    src = jnp.where(valid, src, last_src)
    item_tile = src // N_GROUPS
    item_group = src % N_GROUPS
    prev_tile = jnp.concatenate([jnp.full((1,), -1, jnp.int32), item_tile[:-1]])
    next_tile = jnp.concatenate([item_tile[1:], jnp.full((1,), -1, jnp.int32)])
    first = valid & (item_tile != prev_tile)
    last = valid & ((item_tile != next_tile) | (it == n_items - 1))
    flags = flags_tge[item_tile, item_group] & valid[:, None]
    e0 = jnp.argmax(flags, axis=1)
    rest = flags & (ev[None, :] != e0[:, None])
    e1 = jnp.argmax(rest, axis=1)
    rest = rest & (ev[None, :] != e1[:, None])
    has2 = jnp.any(rest, axis=1)
    e2 = jnp.argmax(rest, axis=1)
    rest = rest & (ev[None, :] != e2[:, None])
    rt_sorted = rt[:, order].reshape(rt.shape[0], ntiles, tme).transpose(1, 0, 2)
    i32 = lambda z: z.astype(jnp.int32)
    plan = (order, i32(item_tile), i32(item_group), i32(first), i32(last), i32(valid), i32(e0), i32(e1), i32(e2),
            i32(has2), i32(rest.reshape(-1)))
    return plan, rt_sorted


def _moe_kernel(order, itile, igroup, ifirst, ilast, ivalid, ie0, ie1, ie2, ihas2, flags, x_hbm, rt_ref, wg, wu, wd, nffn, nfin,
                y_hbm, xbuf, ybuf, acc, hbuf, rcol, gsem, ssem, *, tme, ntiles):
    i = pl.program_id(0)
    t = itile[i]
    slot = t % 2
    g = igroup[i]
    is_first = ifirst[i] == 1

    def gather_row(tile, sl, r):
        tok = order[tile * tme + r]
        src = x_hbm.at[pl.ds(pl.multiple_of(tok * TOKEN_ROWS, TOKEN_ROWS), TOKEN_ROWS)]
        dst = xbuf.at[pl.ds(pl.multiple_of((sl * tme + r) * TOKEN_ROWS, TOKEN_ROWS), TOKEN_ROWS)]
        return pltpu.make_async_copy(src, dst, gsem.at[sl])

    def scatter_row(tile, sl, r):
        tok = order[tile * tme + r]
        return pltpu.make_async_copy(ybuf.at[sl, pl.ds(r, 1)], y_hbm.at[pl.ds(tok, 1)], ssem.at[sl])

    def start_rows(make, tile, sl):
        def body(r, c):
            make(tile, sl, r).start()
            return c
        lax.fori_loop(0, tme, body, 0, unroll=8)

    def wait_gather(sl):
        rows = tme * TOKEN_ROWS
        pltpu.make_async_copy(x_hbm.at[pl.ds(0, rows)], xbuf.at[pl.ds(pl.multiple_of(sl * rows, rows), rows)],
                              gsem.at[sl]).wait()

    def wait_scatter(sl):
        pltpu.make_async_copy(ybuf.at[sl], y_hbm.at[pl.ds(0, tme)], ssem.at[sl]).wait()

    def start_rows_inline(make, tile, sl, r0=0, r1=tme):
        for r in range(r0, r1):
            make(tile, sl, r).start()

    def expert(e):
        eid = (g * EXPERTS_PER_GROUP + e).astype(F32)
        ce = (jnp.where(rcol[:, 0:1] == eid, rcol[:, 2:3], 0.0)
              + jnp.where(rcol[:, 1:2] == eid, rcol[:, 3:4], 0.0))
        h = hbuf[...]
        a = _dot(h, wg[e])
        u = _dot(h, wu[e])
        act = (a * _sigmoid(a)) * u * ce
        acc[...] += _dot(act.astype(BF16), wd[e])

    def expert_with(e, run, alternatives):
        plain = run
        for cond, side_work in alternatives:
            @pl.when(cond)
            def _():
                side_work()
                expert(e)

            plain = jnp.logical_and(plain, jnp.logical_not(cond))

        @pl.when(plain)
        def _():
            expert(e)

    @pl.when(i == 0)
    def _():
        start_rows(gather_row, 0, 0)

    @pl.when(is_first)
    def _():
        wait_gather(slot)
        base = pl.multiple_of(slot * (tme * TOKEN_ROWS), tme * TOKEN_ROWS)
        x1 = jnp.concatenate([xbuf[pl.ds(base + j, tme, stride=TOKEN_ROWS), :] for j in range(TOKEN_ROWS)], axis=1)
        acc[...] = x1
        hbuf[...] = _rms(x1, nffn[...]).astype(BF16)
        rt = rt_ref[0]
        rcol[...] = jnp.transpose(jnp.concatenate([rt, jnp.zeros((LANES - rt.shape[0], tme), F32)], axis=0))

    valid = ivalid[i] == 1
    has2 = ihas2[i] == 1
    do_gather = jnp.logical_and(is_first, t + 1 < ntiles)
    do_scatter = jnp.logical_and(is_first, t >= 1)
    half = tme // 2
    scatter_rows = lambda r0, r1: (lambda: start_rows_inline(scatter_row, t - 1, 1 - slot, r0, r1))
    expert_with(ie0[i], valid, [(do_gather, lambda: start_rows_inline(gather_row, t + 1, 1 - slot))])
    expert_with(ie1[i], valid, [(jnp.logical_and(do_scatter, has2), scatter_rows(0, half)),
                                (jnp.logical_and(do_scatter, jnp.logical_not(has2)), scatter_rows(0, tme))])
    expert_with(ie2[i], has2, [(jnp.logical_and(do_scatter, has2), scatter_rows(half, tme))])
    for e in range(EXPERTS_PER_GROUP):
        @pl.when(flags[i * EXPERTS_PER_GROUP + e] == 1)
        def _():
            expert(e)

    @pl.when(ilast[i] == 1)
    def _():
        @pl.when(t >= 2)
        def _():
            wait_scatter(slot)

        ybuf[slot] = _rms(acc[...], nfin[...])

        @pl.when(t == ntiles - 1)
        def _():
            start_rows(scatter_row, t, slot)

    @pl.when(i == pl.num_programs(0) - 1)
    def _():
        for sl in range(min(2, ntiles)):
            wait_scatter(sl)


def _moe(x1t, rt, weg, weu, wed, nffn, nfin):
    n = x1t.shape[0] // TOKEN_ROWS
    tme = min(MOE_TILE, n)
    assert n % tme == 0
    ntiles = n // tme
    plan, rt_sorted = _moe_plan(rt, tme)
    max_items = ntiles + N_GROUPS - 1
    grp = lambda i, order, itile, igroup, *_: (igroup[i], 0, 0)
    til = lambda i, order, itile, *_: (itile[i], 0, 0)
    const = lambda i, *_: (0, 0)
    grid_spec = pltpu.PrefetchScalarGridSpec(
        num_scalar_prefetch=len(plan),
        grid=(max_items,),
        in_specs=[pl.BlockSpec(memory_space=pl.ANY),
                  pl.BlockSpec((1,) + rt_sorted.shape[1:], til),
                  pl.BlockSpec((EXPERTS_PER_GROUP, D_MODEL, EXPERT_FF), grp),
                  pl.BlockSpec((EXPERTS_PER_GROUP, D_MODEL, EXPERT_FF), grp),
                  pl.BlockSpec((EXPERTS_PER_GROUP, EXPERT_FF, D_MODEL), grp),
                  pl.BlockSpec((1, D_MODEL), const), pl.BlockSpec((1, D_MODEL), const)],
        out_specs=pl.BlockSpec(memory_space=pl.ANY),
        scratch_shapes=[pltpu.VMEM((2 * tme * TOKEN_ROWS, LANES), F32), pltpu.VMEM((2, tme, D_MODEL), F32),
                        pltpu.VMEM((tme, D_MODEL), F32), pltpu.VMEM((tme, D_MODEL), BF16),
                        pltpu.VMEM((tme, LANES), F32),
                        pltpu.SemaphoreType.DMA((2,)), pltpu.SemaphoreType.DMA((2,))],
    )
    return pl.pallas_call(
        functools.partial(_moe_kernel, tme=tme, ntiles=ntiles),
        grid_spec=grid_spec,
        out_shape=jax.ShapeDtypeStruct((n, D_MODEL), F32),
        compiler_params=_cparams(("arbitrary",)),
        name="moe",
    )(*plan, x1t, rt_sorted, weg, weu, wed, nffn, nfin)


def _rope_tables(positions):
    half = HEAD_DIM // 2
    inv_freq = ROPE_THETA ** (-jnp.arange(half, dtype=F32) / half)
    ang = positions.astype(F32)[:, None] * inv_freq[None, :]
    cos, sin = jnp.cos(ang), jnp.sin(ang)
    reps = LANES // HEAD_DIM
    return (jnp.tile(jnp.concatenate([cos, cos], -1), (1, reps)),
            jnp.tile(jnp.concatenate([-sin, sin], -1), (1, reps)))


def _prep_weights(norm_mix, w_in, w_gla_f2, b_gla_f, gla_norm, w_proj_attn, w_proj_gla, w_out, norm_ffn,
                  w_router_group, b_router_group, w_router_expert, b_router_expert):
    w = {"win": w_in.astype(BF16)}
    w["wf2"] = jnp.pad(w_gla_f2.astype(BF16), ((0, LANES - GLA_GATE_RANK), (0, 0)))
    w["bf"] = b_gla_f.reshape(1, -1)
    w["norm_mix"] = norm_mix.reshape(1, -1)
    w["gla_norm"] = gla_norm.reshape(1, -1)
    w["wpa"] = w_proj_attn.astype(BF16)
    w["wpg"] = w_proj_gla.astype(BF16)
    w["wo"] = w_out.astype(BF16)
    w["nf"] = norm_ffn.reshape(1, -1)
    pad = LANES - N_EXPERTS - N_GROUPS
    wr_t = jnp.pad(jnp.concatenate([w_router_expert, w_router_group], axis=1), ((0, 0), (0, pad))).T
    w["wr_hi"], w["wr_lo"] = _split_bf16(wr_t)
    w["br"] = jnp.broadcast_to(jnp.pad(jnp.concatenate([b_router_expert, b_router_group]), (0, pad))[:, None],
                               (LANES, LANES))
    return w


def _layer(x, positions_tab, cache, w, sinks, weg, weu, wed, nfin, tm):
    batch, seq, _ = x.shape
    n = batch * seq
    x2d = x.reshape(n, D_MODEL)
    cos_t, sin_t = positions_tab
    qa, ka, va, qg, kg, vg, rg, la, ga, gb = _inproj(x2d, w["norm_mix"], cos_t, sin_t, w, tm)
    if cache is None:
        a_out = _swa_prompt(sinks, qa, ka, va, batch, seq)
        last = lambda z: z.reshape(batch, seq, KV_WIDTH)[:, seq - WINDOW:].reshape(batch, WINDOW, ATTN_KV_HEADS, HEAD_DIM)
        new_k, new_v = last(ka), last(va)
        g_out, new_s = _gla_prompt(qg, kg, vg, la, rg, w["gla_norm"], batch, seq)
    else:
        cache_k, cache_v, state = cache
        a_out, new_k, new_v = _swa_sample(sinks, qa, ka, va, cache_k.reshape(batch, WINDOW, KV_WIDTH),
                                          cache_v.reshape(batch, WINDOW, KV_WIDTH), batch, seq)
        new_k = new_k.reshape(batch, WINDOW, ATTN_KV_HEADS, HEAD_DIM)
        new_v = new_v.reshape(batch, WINDOW, ATTN_KV_HEADS, HEAD_DIM)
        g_out, new_s = _gla_sample(qg, kg, vg, la, rg, state, w["gla_norm"], batch, seq)
    x1t, rt = _post(x2d, a_out, g_out, ga, gb, w, tm)
    y = _moe(x1t, rt, weg, weu, wed, w["nf"], nfin)
    return y.reshape(batch, seq, D_MODEL), new_k, new_v, new_s


def kernel(x_prompt, x_sample, cache_win_k, cache_win_v, state_gla, norm_mix, w_in, w_gla_f2, b_gla_f, gla_norm,
           attn_sinks, w_proj_attn, w_proj_gla, w_out, norm_ffn, w_router_group, b_router_group, w_router_expert,
           b_router_expert, w_exp_gate, w_exp_up, w_exp_down, norm_final):
    assert norm_mix.shape[0] == 1, "single-layer step"
    seq_p = x_prompt.shape[1]
    dec_b, dec_t = x_sample.shape[0], x_sample.shape[1]
    w = _prep_weights(norm_mix[0], w_in[0], w_gla_f2[0], b_gla_f[0], gla_norm[0], w_proj_attn[0], w_proj_gla[0],
                      w_out[0], norm_ffn[0], w_router_group[0], b_router_group[0], w_router_expert[0],
                      b_router_expert[0])
    weg = w_exp_gate[0].astype(BF16)
    weu = w_exp_up[0].astype(BF16)
    wed = w_exp_down[0].astype(BF16)
    nfin = norm_final.reshape(1, -1)
    sinks = attn_sinks[0]
    tab_p = _rope_tables(jnp.arange(seq_p, dtype=jnp.int32))
    pos_s = PAST_LEN + jnp.arange(dec_t, dtype=jnp.int32)
    tab_s = tuple(jnp.tile(t, (dec_b, 1)) for t in _rope_tables(pos_s))
    tm_p = min(512, seq_p)
    tm_s = dec_b * dec_t
    yp, pk, pv, ps = _layer(x_prompt, tab_p, None, w, sinks, weg, weu, wed, nfin, tm_p)
    ys, sk, sv, ss = _layer(x_sample, tab_s, (cache_win_k[0], cache_win_v[0], state_gla[0]), w, sinks, weg, weu, wed,
                            nfin, tm_s)
    return (yp, ys, pk[None], pv[None], ps[None], sk[None], sv[None], ss[None])
```

```python
import functools
import math

import numpy as np
import jax
import jax.numpy as jnp
from jax import lax
from jax.experimental import pallas as pl
from jax.experimental.pallas import tpu as pltpu

F32 = jnp.float32
BF16 = jnp.bfloat16

D_MODEL = 1024
ATTN_HEADS = 8
ATTN_KV_HEADS = 2
GROUP = ATTN_HEADS // ATTN_KV_HEADS
HEAD_DIM = 64
ATTN_WIDTH = ATTN_HEADS * HEAD_DIM
KV_WIDTH = ATTN_KV_HEADS * HEAD_DIM
WINDOW = 128
ROPE_THETA = 10000.0
PAST_LEN = 8192
GLA_HEADS = 4
GLA_KEY_WIDTH = D_MODEL // 2
GLA_VALUE_WIDTH = D_MODEL
GLA_DK = GLA_KEY_WIDTH // GLA_HEADS
GLA_DV = GLA_VALUE_WIDTH // GLA_HEADS
GLA_GATE_RANK = 16
GLA_GATE_NORMALIZER = 16.0
N_GROUPS = 4
EXPERTS_PER_GROUP = 8
N_EXPERTS = N_GROUPS * EXPERTS_PER_GROUP
EXPERT_FF = 256
EPS = 1e-6
LOG2_E = 1.4426950408889634

LANES = 128
GLA_CHUNK = 128
GLA_HEADS_PER_STEP = 4
SWA_BLOCKS_PER_STEP = 4
SAMPLE_SEQ_BLOCK = 16
POST_PART_ROWS = 512
MOE_TILE = 256
TOKEN_ROWS = D_MODEL // LANES
INPROJ_WIDTHS = (ATTN_WIDTH, 2 * KV_WIDTH, GLA_KEY_WIDTH, GLA_KEY_WIDTH, GLA_VALUE_WIDTH, GLA_VALUE_WIDTH)
ROUTER_ROWS = 40
VMEM_LIMIT = 56 * 1024 * 1024


def _cparams(sem):
    return pltpu.CompilerParams(dimension_semantics=sem, vmem_limit_bytes=VMEM_LIMIT)


def _rms(x, g):
    return x * lax.rsqrt(jnp.mean(x * x, axis=-1, keepdims=True) + EPS) * g


def _sigmoid(x):
    return 1.0 / (1.0 + jnp.exp(-x))


def _dot(a, b):
    return jnp.dot(a, b, preferred_element_type=F32)


def _dot_nt(a, b):
    return lax.dot_general(a, b, (((1,), (1,)), ((), ())), preferred_element_type=F32)


def _dot_tn(a, b):
    return lax.dot_general(a, b, (((0,), (0,)), ((), ())), preferred_element_type=F32)


def _split_bf16(x):
    hi = x.astype(BF16)
    lo = (x - hi.astype(F32)).astype(BF16)
    return hi, lo


def _inproj_kernel(x_ref, g_ref, cos_ref, sin_ref, win, wf2, bf,
                   qa_o, ka_o, va_o, qg_o, kg_o, vg_o, rg_o, la_o, ga_o, gb_o, wgate):
    f0 = sum(INPROJ_WIDTHS)

    @pl.when(pl.program_id(0) == 0)
    def _():
        tail = win[:, f0:]
        wgate[...] = tail[:, GLA_GATE_RANK:GLA_GATE_RANK + 2 * D_MODEL]

    tm = x_ref.shape[0]
    halves = [slice(0, tm // 2), slice(tm // 2, tm)] if tm % 16 == 0 else [slice(0, tm)]
    hbs = [_rms(x_ref[r, :], g_ref[...]).astype(BF16) for r in halves]
    cols = np.cumsum((0,) + INPROJ_WIDTHS[:-1])
    wqa, wkva, wqg, wkg, wvg, wrg = (win.at[:, int(c):int(c) + wd] for c, wd in zip(cols, INPROJ_WIDTHS))
    wga = wgate.at[:, :D_MODEL]
    wgb = wgate.at[:, D_MODEL:]
    wf = win.at[:, f0:f0 + LANES]
    lane = lax.broadcasted_iota(jnp.int32, (halves[0].stop - halves[0].start, LANES), 1)
    first_half = (lane % HEAD_DIM) < (HEAD_DIM // 2)

    def rope(t, r):
        swapped = jnp.where(first_half, pltpu.roll(t, LANES - HEAD_DIM // 2, 1), pltpu.roll(t, HEAD_DIM // 2, 1))
        return t * cos_ref[r, :] + swapped * sin_ref[r, :]

    for r, hb in zip(halves, hbs):
        qa = _dot(hb, wqa[...])
        for c in range(ATTN_WIDTH // LANES):
            sl = slice(c * LANES, (c + 1) * LANES)
            qa_o[r, sl] = (rope(qa[:, sl], r) * (HEAD_DIM ** -0.5)).astype(BF16)
    for r, hb in zip(halves, hbs):
        kva = _dot(hb, wkva[...])
        ka_o[r, :] = rope(kva[:, :KV_WIDTH], r)
        va_o[r, :] = kva[:, KV_WIDTH:]
    for r, hb in zip(halves, hbs):
        qg_o[r, :] = (_dot(hb, wqg[...]) * (GLA_DK ** -0.5)).astype(BF16)
    for r, hb in zip(halves, hbs):
        kg_o[r, :] = _dot(hb, wkg[...]).astype(BF16)
    for r, hb in zip(halves, hbs):
        vg_o[r, :] = _dot(hb, wvg[...]).astype(BF16)
    for r, hb in zip(halves, hbs):
        rg_o[r, :] = _dot(hb, wrg[...]).astype(BF16)
    for r, hb in zip(halves, hbs):
        z = _dot(_dot(hb, wf[...]).astype(BF16), wf2[...]) + bf[...]
        la_o[r, :] = (jnp.minimum(z, 0.0) - jnp.log1p(jnp.exp(-jnp.abs(z)))) * (1.0 / GLA_GATE_NORMALIZER)
    for r, hb in zip(halves, hbs):
        ga_o[r, :] = _sigmoid(_dot(hb, wga[...])).astype(BF16)
    for r, hb in zip(halves, hbs):
        gb_o[r, :] = _sigmoid(_dot(hb, wgb[...])).astype(BF16)


def _inproj(x2d, norm_g, cos_t, sin_t, w, tm):
    n = x2d.shape[0]
    ntab = cos_t.shape[0] // tm
    row = lambda i: (i, 0)
    const = lambda i: (0, 0)
    tab = lambda i: (i % ntab, 0)
    wnames = ("win", "wf2", "bf")
    wspecs = [pl.BlockSpec(w[k].shape, const) for k in wnames]
    widths = (ATTN_WIDTH, KV_WIDTH, KV_WIDTH, GLA_KEY_WIDTH, GLA_KEY_WIDTH, GLA_VALUE_WIDTH, GLA_VALUE_WIDTH,
              GLA_KEY_WIDTH, D_MODEL, D_MODEL)
    dtypes = (BF16, F32, F32, BF16, BF16, BF16, BF16, F32, BF16, BF16)
    return pl.pallas_call(
        _inproj_kernel,
        grid=(n // tm,),
        in_specs=[pl.BlockSpec((tm, D_MODEL), row), pl.BlockSpec((1, D_MODEL), const),
                  pl.BlockSpec((tm, LANES), tab), pl.BlockSpec((tm, LANES), tab)] + wspecs,
        out_specs=[pl.BlockSpec((tm, wd), row) for wd in widths],
        out_shape=[jax.ShapeDtypeStruct((n, wd), dt) for wd, dt in zip(widths, dtypes)],
        scratch_shapes=[pltpu.VMEM((D_MODEL, 2 * D_MODEL), BF16)],
        compiler_params=_cparams(("arbitrary",)),
        name="inproj",
    )(x2d, norm_g, cos_t, sin_t, *[w[k] for k in wnames])


def _swa_prompt_kernel(sink_ref, q_ref, kc_ref, kp_ref, vc_ref, vp_ref, o_ref, *, qb):
    n = pl.program_id(1)
    k3 = jnp.concatenate([kp_ref[...], kc_ref[...]], axis=0).astype(BF16)
    v3 = jnp.concatenate([vp_ref[...], vc_ref[...]], axis=0).astype(BF16)
    t = lax.broadcasted_iota(jnp.int32, (WINDOW, 2 * WINDOW), 0)
    j = lax.broadcasted_iota(jnp.int32, (WINDOW, 2 * WINDOW), 1)
    band = (j >= t) & (j <= t + WINDOW)
    zeros = jnp.zeros((2 * WINDOW, HEAD_DIM), BF16)
    ones = jnp.ones((2 * WINDOW, HEAD_DIM), BF16)
    lane = lax.broadcasted_iota(jnp.int32, (WINDOW, LANES), 1)
    chains = [(blk, h) for blk in range(qb) for h in range(ATTN_HEADS)]
    scores = []
    for blk, h in chains:
        kv = h // GROUP
        rows = slice(blk * WINDOW, (blk + 1) * WINDOW)
        keys = slice(blk * WINDOW, (blk + 2) * WINDOW)
        s = _dot_nt(q_ref[rows, h * HEAD_DIM:(h + 1) * HEAD_DIM], k3[keys, kv * HEAD_DIM:(kv + 1) * HEAD_DIM])
        valid = band & ((j >= WINDOW) | (n > 0)) if blk == 0 else band
        scores.append(jnp.where(valid, s, -jnp.inf))
    probs, sink_terms = [], []
    for (blk, h), s in zip(chains, scores):
        m = jnp.maximum(jnp.max(s, axis=-1, keepdims=True), sink_ref[h])
        probs.append(jnp.exp(s - m).astype(BF16))
        sink_terms.append(jnp.exp(sink_ref[h] - m))
    for blk in range(qb):
        rows = slice(blk * WINDOW, (blk + 1) * WINDOW)
        keys = slice(blk * WINDOW, (blk + 2) * WINDOW)
        for kv in range(ATTN_KV_HEADS):
            vv = v3[keys, kv * HEAD_DIM:(kv + 1) * HEAD_DIM]
            vext = (jnp.concatenate([vv, zeros, ones, zeros], axis=1), jnp.concatenate([zeros, vv, zeros, ones], axis=1))
            for pr in range(GROUP // 2):
                h0 = kv * GROUP + pr * 2
                c0 = blk * ATTN_HEADS + h0
                acc = _dot(probs[c0], vext[0]) + _dot(probs[c0 + 1], vext[1])
                l = acc[:, LANES:] + jnp.where(lane < HEAD_DIM, sink_terms[c0], sink_terms[c0 + 1])
                o_ref[rows, h0 * HEAD_DIM:h0 * HEAD_DIM + LANES] = (acc[:, :LANES] / l).astype(BF16)


def _swa_prompt(sinks, qa, ka, va, batch, seq):
    nb = seq // WINDOW
    qb = math.gcd(SWA_BLOCKS_PER_STEP, nb)
    steps = nb // qb
    cur = lambda b, n: (b * steps + n, 0)
    prev = lambda b, n: (b * nb + jnp.maximum(n * qb - 1, 0), 0)
    return pl.pallas_call(
        functools.partial(_swa_prompt_kernel, qb=qb),
        grid=(batch, steps),
        in_specs=[pl.BlockSpec(memory_space=pltpu.SMEM),
                  pl.BlockSpec((qb * WINDOW, ATTN_WIDTH), cur),
                  pl.BlockSpec((qb * WINDOW, KV_WIDTH), cur), pl.BlockSpec((WINDOW, KV_WIDTH), prev),
                  pl.BlockSpec((qb * WINDOW, KV_WIDTH), cur), pl.BlockSpec((WINDOW, KV_WIDTH), prev)],
        out_specs=pl.BlockSpec((qb * WINDOW, ATTN_WIDTH), cur),
        out_shape=jax.ShapeDtypeStruct(qa.shape, BF16),
        compiler_params=_cparams(("parallel", "parallel")),
        name="swa_prompt",
    )(sinks, qa, ka, ka, va, va)


def _swa_sample_kernel(sink_ref, q_ref, kn_ref, vn_ref, kn3_ref, vn3_ref, ck_ref, cv_ref, o_ref, nk_ref, nv_ref,
                       *, t_new):
    sb = ck_ref.shape[0]
    spv = 8 // t_new
    nq = GROUP * 8
    nc = spv * WINDOW
    qi = lax.broadcasted_iota(jnp.int32, (nq, nc), 0) % 8
    ci = lax.broadcasted_iota(jnp.int32, (nq, nc), 1)
    valid_c = (qi // t_new == ci // WINDOW) & (ci % WINDOW >= qi % t_new)
    qn = lax.broadcasted_iota(jnp.int32, (nq, 8), 0) % 8
    cn = lax.broadcasted_iota(jnp.int32, (nq, 8), 1)
    valid_n = (qn // t_new == cn // t_new) & (cn <= qn)
    grow = lax.broadcasted_iota(jnp.int32, (nq, 1), 0) // 8
    chains = [(vr, kv) for vr in range(sb // spv) for kv in range(ATTN_KV_HEADS)]
    scored = []
    for vr, kv in chains:
        r8 = slice(8 * vr, 8 * vr + 8)
        cs = slice(kv * HEAD_DIM, (kv + 1) * HEAD_DIM)
        heads = [kv * GROUP + g for g in range(GROUP)]
        qs = jnp.concatenate([q_ref[r8, h * HEAD_DIM:(h + 1) * HEAD_DIM] for h in heads], axis=0)
        ck = ck_ref[vr * spv:(vr + 1) * spv, :, cs].reshape(nc, HEAD_DIM).astype(BF16)
        kn = kn_ref[r8, cs].astype(BF16)
        scored.append((jnp.where(valid_c, _dot_nt(qs, ck), -jnp.inf), jnp.where(valid_n, _dot_nt(qs, kn), -jnp.inf)))
    soft = []
    for (vr, kv), (s_c, s_n) in zip(chains, scored):
        sink = sink_ref[kv * GROUP]
        for g in range(1, GROUP):
            sink = jnp.where(grow == g, sink_ref[kv * GROUP + g], sink)
        m = jnp.maximum(jnp.maximum(jnp.max(s_c, axis=-1, keepdims=True), jnp.max(s_n, axis=-1, keepdims=True)), sink)
        p_c = jnp.exp(s_c - m)
        p_n = jnp.exp(s_n - m)
        l = jnp.sum(p_c, axis=-1, keepdims=True) + jnp.sum(p_n, axis=-1, keepdims=True) + jnp.exp(sink - m)
        soft.append((p_c.astype(BF16), p_n.astype(BF16), l))
    for (vr, kv), (p_c, p_n, l) in zip(chains, soft):
        r8 = slice(8 * vr, 8 * vr + 8)
        cs = slice(kv * HEAD_DIM, (kv + 1) * HEAD_DIM)
        cv = cv_ref[vr * spv:(vr + 1) * spv, :, cs].reshape(nc, HEAD_DIM).astype(BF16)
        o = (_dot(p_c, cv) + _dot(p_n, vn_ref[r8, cs].astype(BF16))) / l
        for a in range(GROUP // 2):
            pair = jnp.concatenate([o[16 * a:16 * a + 8], o[16 * a + 8:16 * a + 16]], axis=1)
            c0 = (kv * GROUP + 2 * a) * HEAD_DIM
            o_ref[r8, c0:c0 + LANES] = pair.astype(BF16)
    nk_ref[:, 0:WINDOW - t_new, :] = ck_ref[:, t_new:WINDOW, :]
    nk_ref[:, WINDOW - t_new:WINDOW, :] = kn3_ref[...]
    nv_ref[:, 0:WINDOW - t_new, :] = cv_ref[:, t_new:WINDOW, :]
    nv_ref[:, WINDOW - t_new:WINDOW, :] = vn3_ref[...]


def _swa_sample(sinks, qa, ka, va, cache_k, cache_v, batch, t_new):
    sb = SAMPLE_SEQ_BLOCK
    rows = sb * t_new
    r2 = lambda i: (i, 0)
    r3 = lambda i: (i, 0, 0)
    ka3 = ka.reshape(batch, t_new, KV_WIDTH)
    va3 = va.reshape(batch, t_new, KV_WIDTH)
    return pl.pallas_call(
        functools.partial(_swa_sample_kernel, t_new=t_new),
        grid=(batch // sb,),
        in_specs=[pl.BlockSpec(memory_space=pltpu.SMEM),
                  pl.BlockSpec((rows, ATTN_WIDTH), r2),
                  pl.BlockSpec((rows, KV_WIDTH), r2), pl.BlockSpec((rows, KV_WIDTH), r2),
                  pl.BlockSpec((sb, t_new, KV_WIDTH), r3), pl.BlockSpec((sb, t_new, KV_WIDTH), r3),
                  pl.BlockSpec((sb, WINDOW, KV_WIDTH), r3), pl.BlockSpec((sb, WINDOW, KV_WIDTH), r3)],
        out_specs=[pl.BlockSpec((rows, ATTN_WIDTH), r2),
                   pl.BlockSpec((sb, WINDOW, KV_WIDTH), r3), pl.BlockSpec((sb, WINDOW, KV_WIDTH), r3)],
        out_shape=[jax.ShapeDtypeStruct(qa.shape, BF16),
                   jax.ShapeDtypeStruct(cache_k.shape, F32), jax.ShapeDtypeStruct(cache_v.shape, F32)],
        compiler_params=_cparams(("parallel",)),
        name="swa_sample",
    )(sinks, qa, ka, va, ka3, va3, cache_k, cache_v)


def _gla_constants(c, seg, with_rem):
    t = np.arange(c)
    sid = t // seg
    same = sid[:, None] == sid[None, :]
    levels = []
    m = seg // 2
    while m >= 1:
        levels.append(m)
        m //= 2
    mats, roles, masks = [], [], []
    for m in levels:
        blk = t // (2 * m)
        second = (t // m) % 2 == 1
        p = blk * 2 * m + m - 1
        u = t[None, :]
        mq = (u > p[:, None]) & (u <= t[:, None])
        mk = (u > t[:, None]) & (u <= p[:, None])
        if m > 1:
            mats.append(np.where(second[:, None], mq, mk))
        roles.append(np.broadcast_to(second[:, None], (c, LANES)))
        masks.append((blk[:, None] == blk[None, :]) & second[:, None] & ~second[None, :])
    masks.append(np.eye(c, dtype=bool))
    mats.append(same & (t[None, :] <= t[:, None]))
    if with_rem:
        mats.append(same & (t[None, :] > t[:, None]))
    mall = np.concatenate(mats, 0).astype(np.float32)
    mall = jnp.asarray(np.concatenate([mall, mall], 1), BF16)
    role = jnp.asarray(np.concatenate(roles, 0).astype(np.float32))
    mask = jnp.asarray(np.concatenate(masks, 0).astype(np.float32))
    return len(levels), mall, role, mask


def _gla_exponents(la, mall):
    la2 = la * LOG2_E
    hl = jnp.concatenate(_split_bf16(la2), axis=0)
    return _dot(mall, hl), hl, la2


def _gla_scores(qb, kb, e2, la2, role_ref, mask_ref, nlev, c):
    qf = qb.astype(F32)
    kf = kb.astype(F32)
    terms = []
    for lv in range(nlev):
        sl = slice(lv * c, (lv + 1) * c)
        m = 1 << (nlev - 1 - lv)
        if m % 8 == 0:
            pe = jnp.exp2(e2[sl])
            x = jnp.concatenate([(qf if blk % 2 else kf)[blk * m:(blk + 1) * m] * pe[blk * m:(blk + 1) * m]
                                 for blk in range(c // m)], axis=0).astype(BF16)
        else:
            second = role_ref[sl, :] > 0.5
            e = e2[sl] if lv < nlev - 1 else jnp.where(second, la2, 0.0)
            x = (jnp.where(second, qf, kf) * jnp.exp2(e)).astype(BF16)
        terms.append((x, x, lv))
    terms.append((qb, kb, nlev))
    mask = lambda i: mask_ref[i * c:(i + 1) * c, :]
    a = None
    pair = c % LANES == 0
    while terms:
        if pair and len(terms) >= 2:
            (l0, r0, i0), (l1, r1, i1) = terms.pop(), terms.pop()
            z = jnp.zeros_like(r0)
            rhs = jnp.concatenate([jnp.concatenate([r0, z], axis=1), jnp.concatenate([z, r1], axis=1)], axis=0)
            g = _dot_nt(jnp.concatenate([l0, l1], axis=1), rhs)
            t = mask(i0) * g[:, :c] + mask(i1) * g[:, c:]
        else:
            l0, r0, i0 = terms.pop()
            t = mask(i0) * _dot_nt(l0, r0)
        a = t if a is None else a + t
    return a, qf, kf


def _gla_out(o, g, r):
    r = r.astype(F32)
    return (_rms(o, g) * (r * _sigmoid(r))).astype(BF16)


def _gla_prompt_kernel(q_ref, k_ref, v_ref, la_ref, r_ref, mall_ref, role_ref, mask_ref, g_ref, o_ref, s_ref,
                       s_scr, *, nlev, nchunks):
    c = GLA_CHUNK
    hp = GLA_HEADS_PER_STEP
    s_scr[...] = jnp.zeros_like(s_scr)

    def chunk(i, carry):
        rows = pl.ds(pl.multiple_of(i * c, c), c)
        e2_all, _, la2_all = _gla_exponents(la_ref[rows, :], mall_ref[...])
        ksl = [slice(h * GLA_DK, (h + 1) * GLA_DK) for h in range(hp)]
        vsl = [slice(h * GLA_DV, (h + 1) * GLA_DV) for h in range(hp)]
        inter = []
        for h in range(hp):
            e2 = e2_all[:, ksl[h]]
            v = v_ref[rows, vsl[h]]
            qf = q_ref[rows, ksl[h]].astype(F32)
            kf = k_ref[rows, ksl[h]].astype(F32)
            s = s_scr[h]
            b = e2[(nlev - 1) * c:nlev * c]
            inter.append(_dot((qf * jnp.exp2(b)).astype(BF16), s.astype(BF16)))
            kt = (kf * jnp.exp2(b[c - 1:c, :] - b)).astype(BF16)
            dec = jnp.exp2(jnp.transpose(b[c - 8:c, :]))[:, 7:8]
            s_scr[h] = dec * s + _dot_tn(kt, v)
        scores = [_gla_scores(q_ref[rows, ksl[h]], k_ref[rows, ksl[h]], e2_all[:, ksl[h]], la2_all[:, ksl[h]], role_ref,
                              mask_ref, nlev, c)[0] for h in range(hp)]
        for h in range(hp):
            o = _dot(scores[h].astype(BF16), v_ref[rows, vsl[h]]) + inter[h]
            o_ref[rows, vsl[h]] = _gla_out(o, g_ref[...], r_ref[rows, vsl[h]])
        return carry

    lax.fori_loop(0, nchunks, chunk, 0)
    s_ref[0] = s_scr[...]


def _gla_prompt(qg, kg, vg, la, rg, gnorm, batch, seq):
    nlev, mall, role, mask = _gla_constants(GLA_CHUNK, GLA_CHUNK, with_rem=False)
    hp = GLA_HEADS_PER_STEP
    bh = lambda b, h: (b, h)
    const = lambda b, h: (0, 0)
    return pl.pallas_call(
        functools.partial(_gla_prompt_kernel, nlev=nlev, nchunks=seq // GLA_CHUNK),
        grid=(batch, GLA_HEADS // hp),
        in_specs=[pl.BlockSpec((seq, hp * GLA_DK), bh), pl.BlockSpec((seq, hp * GLA_DK), bh),
                  pl.BlockSpec((seq, hp * GLA_DV), bh), pl.BlockSpec((seq, hp * GLA_DK), bh),
                  pl.BlockSpec((seq, hp * GLA_DV), bh),
                  pl.BlockSpec(mall.shape, const), pl.BlockSpec(role.shape, const), pl.BlockSpec(mask.shape, const),
                  pl.BlockSpec((1, GLA_DV), const)],
        out_specs=[pl.BlockSpec((seq, hp * GLA_DV), bh),
                   pl.BlockSpec((1, hp, GLA_DK, GLA_DV), lambda b, h: (b, h, 0, 0))],
        out_shape=[jax.ShapeDtypeStruct(vg.shape, BF16),
                   jax.ShapeDtypeStruct((batch, GLA_HEADS, GLA_DK, GLA_DV), F32)],
        scratch_shapes=[pltpu.VMEM((hp, GLA_DK, GLA_DV), F32)],
        compiler_params=_cparams(("parallel", "parallel")),
        name="gla_prompt",
    )(qg, kg, vg, la, rg, mall, role, mask, gnorm)


def _gla_sample_kernel(q_ref, k_ref, v_ref, la_ref, r_ref, s0_ref, mall_ref, role_ref, mask_ref, msum_ref, g_ref,
                       o_ref, s_ref, *, nlev, t_new):
    sb = s0_ref.shape[0]
    c = sb * t_new
    spv = 8 // t_new
    e2_all, hl_all, la2_all = _gla_exponents(la_ref[...], mall_ref[...])
    seq_in_tile = lax.broadcasted_iota(jnp.int32, (8, GLA_DV), 0) // t_new
    seq_of_row = lax.broadcasted_iota(jnp.int32, (c, GLA_DV), 0) // t_new
    ksl = [slice(h * GLA_DK, (h + 1) * GLA_DK) for h in range(GLA_HEADS)]
    scores = [_gla_scores(q_ref[:, ksl[h]], k_ref[:, ksl[h]], e2_all[:, ksl[h]], la2_all[:, ksl[h]], role_ref, mask_ref,
                          nlev, c) for h in range(GLA_HEADS)]
    for h in range(GLA_HEADS):
        ks = ksl[h]
        vs = slice(h * GLA_DV, (h + 1) * GLA_DV)
        v = v_ref[:, vs]
        e2 = e2_all[:, ks]
        hl = hl_all[:, ks]
        a, qf, kf = scores[h]
        qe = (qf * jnp.exp2(e2[(nlev - 1) * c:nlev * c])).astype(BF16)
        kt_t = jnp.transpose(kf * jnp.exp2(e2[nlev * c:(nlev + 1) * c])).astype(BF16)
        bl_t = jnp.transpose(_dot(msum_ref[...], hl[:c]) + _dot(msum_ref[...], hl[c:]))
        dec_t = jnp.exp2(bl_t)
        inter = []
        for vr in range(c // 8):
            rows8 = qe[8 * vr:8 * vr + 8]
            tile = None
            for u in range(spv):
                j = vr * spv + u
                s0 = s0_ref[j, h]
                r = _dot(rows8, s0.astype(BF16))
                tile = r if tile is None else jnp.where(seq_in_tile == u, r, tile)
                vj = jnp.where(seq_of_row == j, v, jnp.zeros_like(v))
                s_ref[j, h] = dec_t[:, j:j + 1] * s0 + _dot(kt_t, vj)
            inter.append(tile)
        o = _dot(a.astype(BF16), v) + jnp.concatenate(inter, axis=0)
        o_ref[:, vs] = _gla_out(o, g_ref[...], r_ref[:, vs])


def _gla_sample(qg, kg, vg, la, rg, state, gnorm, batch, t_new):
    sb = SAMPLE_SEQ_BLOCK
    c = sb * t_new
    assert 8 % t_new == 0 and c % 8 == 0
    nlev, mall, role, mask = _gla_constants(c, t_new, with_rem=True)
    msum = jnp.asarray((np.arange(c)[None, :] // t_new == np.arange(sb)[:, None]).astype(np.float32), BF16)
    rows = lambda i: (i, 0)
    const = lambda i: (0, 0)
    st = lambda i: (i, 0, 0, 0)
    return pl.pallas_call(
        functools.partial(_gla_sample_kernel, nlev=nlev, t_new=t_new),
        grid=(batch // sb,),
        in_specs=[pl.BlockSpec((c, GLA_KEY_WIDTH), rows), pl.BlockSpec((c, GLA_KEY_WIDTH), rows),
                  pl.BlockSpec((c, GLA_VALUE_WIDTH), rows), pl.BlockSpec((c, GLA_KEY_WIDTH), rows),
                  pl.BlockSpec((c, GLA_VALUE_WIDTH), rows),
                  pl.BlockSpec((sb, GLA_HEADS, GLA_DK, GLA_DV), st),
                  pl.BlockSpec(mall.shape, const), pl.BlockSpec(role.shape, const), pl.BlockSpec(mask.shape, const),
                  pl.BlockSpec(msum.shape, const), pl.BlockSpec((1, GLA_DV), const)],
        out_specs=[pl.BlockSpec((c, GLA_VALUE_WIDTH), rows), pl.BlockSpec((sb, GLA_HEADS, GLA_DK, GLA_DV), st)],
        out_shape=[jax.ShapeDtypeStruct(vg.shape, BF16), jax.ShapeDtypeStruct(state.shape, F32)],
        compiler_params=_cparams(("parallel",)),
        name="gla_sample",
    )(qg, kg, vg, la, rg, state, mall, role, mask, msum, gnorm)


def _post_kernel(x_ref, a_ref, gl_ref, ga_ref, gb_ref, wpa, wpg, wo, nf_ref, wr_hi, wr_lo, br, x1t_o, rt_o):
    tm = x_ref.shape[0]
    part = POST_PART_ROWS if tm % POST_PART_ROWS == 0 else tm
    halves = [slice(r0, r0 + part) for r0 in range(0, tm, part)]
    proj =[(_dot(a_ref[r, :], wpa[...]), _dot(gl_ref[r, :], wpg[...])) for r in halves]
    x1s = []
    for r, (pa, pg) in zip(halves, proj):
        merged = ga_ref[r, :].astype(F32) * pa + gb_ref[r, :].astype(F32) * pg
        x1s.append(x_ref[r, :] + _dot(merged.astype(BF16), wo[...]))
    logits = []
    for r, x1 in zip(halves, x1s):
        rows = r.stop - r.start
        for j in range(TOKEN_ROWS):
            x1t_o[pl.ds(r.start * TOKEN_ROWS + j, rows, stride=TOKEN_ROWS), :] = x1[:, j * LANES:(j + 1) * LANES]
        h_hi, h_lo = _split_bf16(_rms(x1, nf_ref[...]))
        logits.append(_dot_nt(wr_hi[...], h_hi) + _dot_nt(wr_hi[...], h_lo) + _dot_nt(wr_lo[...], h_hi))
    nrow = ROUTER_ROWS
    big = jnp.int32(LANES)
    ninf = -jnp.inf
    for r, lt in zip(halves, logits):
        lt = lt[:nrow] + br[:nrow, 0:1]
        row = lax.broadcasted_iota(jnp.int32, lt.shape, 0)

        def first_max(vals):
            mx = jnp.max(vals, axis=0, keepdims=True)
            return mx, jnp.min(jnp.where(vals == mx, row, big), axis=0, keepdims=True)

        gl = jnp.where((row >= N_EXPERTS) & (row < N_EXPERTS + N_GROUPS), lt, ninf)
        gmax, gidx = first_max(gl)
        p_sel = 1.0 / jnp.sum(jnp.exp(gl - gmax), axis=0, keepdims=True)
        lo = (gidx - N_EXPERTS) * EXPERTS_PER_GROUP
        el = jnp.where((row >= lo) & (row < lo + EXPERTS_PER_GROUP), lt, ninf)
        v1, i1 = first_max(el)
        el2 = jnp.where(row == i1, ninf, el)
        v2, i2 = first_max(el2)
        t = jnp.exp(v2 - v1)
        w1 = p_sel / (1.0 + t)
        w2 = p_sel * t / (1.0 + t)
        row8 = lax.broadcasted_iota(jnp.int32, (rt_o.shape[0], r.stop - r.start), 0)
        pick = lambda k, val, rest: jnp.where(row8 == k, val, rest)
        rt_o[:, r] = pick(0, i1.astype(F32), pick(1, i2.astype(F32), pick(2, w1, pick(3, w2, 0.0))))


def _post(x2d, a_out, g_out, ga, gb, w, tm):
    n = x2d.shape[0]
    row = lambda i: (i, 0)
    const = lambda i: (0, 0)
    wnames = ("wpa", "wpg", "wo", "nf", "wr_hi", "wr_lo", "br")
    rt_rows = 8
    return pl.pallas_call(
        _post_kernel,
        grid=(n // tm,),
        in_specs=[pl.BlockSpec((tm, D_MODEL), row), pl.BlockSpec((tm, ATTN_WIDTH), row),
                  pl.BlockSpec((tm, GLA_VALUE_WIDTH), row), pl.BlockSpec((tm, D_MODEL), row),
                  pl.BlockSpec((tm, D_MODEL), row)] + [pl.BlockSpec(w[k].shape, const) for k in wnames],
        out_specs=[pl.BlockSpec((tm * TOKEN_ROWS, LANES), row),
                   pl.BlockSpec((rt_rows, tm), lambda i: (0, i))],
        out_shape=[jax.ShapeDtypeStruct((n * TOKEN_ROWS, LANES), F32), jax.ShapeDtypeStruct((rt_rows, n), F32)],
        compiler_params=_cparams(("parallel",)),
        name="post_mixer",
    )(x2d, a_out, g_out, ga, gb, *[w[k] for k in wnames])


def _moe_plan(rt, tme):
    n = rt.shape[1]
    ntiles = n // tme
    max_items = ntiles + N_GROUPS - 1
    i1, i2 = rt[0].astype(jnp.int32), rt[1].astype(jnp.int32)
    grp = i1 // EXPERTS_PER_GROUP
    lo = jnp.minimum(i1, i2) % EXPERTS_PER_GROUP
    hi = jnp.maximum(i1, i2) % EXPERTS_PER_GROUP
    key = (grp * EXPERTS_PER_GROUP + lo) * EXPERTS_PER_GROUP + hi
    order = jnp.argsort(key, stable=True).astype(jnp.int32)
    skey = key[order].reshape(ntiles, tme)
    sg = skey // (EXPERTS_PER_GROUP * EXPERTS_PER_GROUP)
    slo = (skey // EXPERTS_PER_GROUP) % EXPERTS_PER_GROUP
    shi = skey % EXPERTS_PER_GROUP
    ev = jnp.arange(EXPERTS_PER_GROUP)
    in_g = sg[:, :, None] == jnp.arange(N_GROUPS)
    uses_e = (slo[:, :, None] == ev) | (shi[:, :, None] == ev)
    flags_tge = jnp.any(in_g[:, :, :, None] & uses_e[:, :, None, :], axis=1)
    present = jnp.any(in_g, axis=1).reshape(-1)
    pos = jnp.cumsum(present) - 1
    n_items = pos[-1] + 1
    src = jnp.zeros((max_items,), jnp.int32).at[jnp.where(present, pos, max_items)].set(
        jnp.arange(ntiles * N_GROUPS, dtype=jnp.int32), mode="drop")
    it = jnp.arange(max_items)
    valid = it < n_items
    last_src = src[n_items - 1]
    src = jnp.where(valid, src, last_src)
    item_tile = src // N_GROUPS
    item_group = src % N_GROUPS
    prev_tile = jnp.concatenate([jnp.full((1,), -1, jnp.int32), item_tile[:-1]])
    next_tile = jnp.concatenate([item_tile[1:], jnp.full((1,), -1, jnp.int32)])
    first = valid & (item_tile != prev_tile)
    last = valid & ((item_tile != next_tile) | (it == n_items - 1))
    flags = flags_tge[item_tile, item_group] & valid[:, None]
    e0 = jnp.argmax(flags, axis=1)
    rest = flags & (ev[None, :] != e0[:, None])
    e1 = jnp.argmax(rest, axis=1)
    rest = rest & (ev[None, :] != e1[:, None])
    has2 = jnp.any(rest, axis=1)
    e2 = jnp.argmax(rest, axis=1)
    rest = rest & (ev[None, :] != e2[:, None])
    rt_sorted = rt[:, order].reshape(rt.shape[0], ntiles, tme).transpose(1, 0, 2)
    i32 = lambda z: z.astype(jnp.int32)
    plan = (order, i32(item_tile), i32(item_group), i32(first), i32(last), i32(valid), i32(e0), i32(e1), i32(e2),
            i32(has2), i32(rest.reshape(-1)))
    return plan, rt_sorted


def _moe_kernel(order, itile, igroup, ifirst, ilast, ivalid, ie0, ie1, ie2, ihas2, flags, x_hbm, rt_ref, wg, wu, wd, nffn, nfin,
                y_hbm, xbuf, ybuf, acc, hbuf, rcol, gsem, ssem, *, tme, ntiles):
    i = pl.program_id(0)
    t = itile[i]
    slot = t % 2
    g = igroup[i]
    is_first = ifirst[i] == 1

    def gather_row(tile, sl, r):
        tok = order[tile * tme + r]
        src = x_hbm.at[pl.ds(pl.multiple_of(tok * TOKEN_ROWS, TOKEN_ROWS), TOKEN_ROWS)]
        dst = xbuf.at[pl.ds(pl.multiple_of((sl * tme + r) * TOKEN_ROWS, TOKEN_ROWS), TOKEN_ROWS)]
        return pltpu.make_async_copy(src, dst, gsem.at[sl])

    def scatter_row(tile, sl, r):
        tok = order[tile * tme + r]
        return pltpu.make_async_copy(ybuf.at[sl, pl.ds(r, 1)], y_hbm.at[pl.ds(tok, 1)], ssem.at[sl])

    def start_rows(make, tile, sl):
        def body(r, c):
            make(tile, sl, r).start()
            return c
        lax.fori_loop(0, tme, body, 0, unroll=8)

    def wait_gather(sl):
        rows = tme * TOKEN_ROWS
        pltpu.make_async_copy(x_hbm.at[pl.ds(0, rows)], xbuf.at[pl.ds(pl.multiple_of(sl * rows, rows), rows)],
                              gsem.at[sl]).wait()

    def wait_scatter(sl):
        pltpu.make_async_copy(ybuf.at[sl], y_hbm.at[pl.ds(0, tme)], ssem.at[sl]).wait()

    def start_rows_inline(make, tile, sl, r0=0, r1=tme):
        for r in range(r0, r1):
            make(tile, sl, r).start()

    def expert(e):
        eid = (g * EXPERTS_PER_GROUP + e).astype(F32)
        ce = (jnp.where(rcol[:, 0:1] == eid, rcol[:, 2:3], 0.0)
              + jnp.where(rcol[:, 1:2] == eid, rcol[:, 3:4], 0.0))
        h = hbuf[...]
        a = _dot(h, wg[e])
        u = _dot(h, wu[e])
        act = (a * _sigmoid(a)) * u * ce
        acc[...] += _dot(act.astype(BF16), wd[e])

    def expert_with(e, run, alternatives):
        plain = run
        for cond, side_work in alternatives:
            @pl.when(cond)
            def _():
                side_work()
                expert(e)

            plain = jnp.logical_and(plain, jnp.logical_not(cond))

        @pl.when(plain)
        def _():
            expert(e)

    @pl.when(i == 0)
    def _():
        start_rows(gather_row, 0, 0)

    @pl.when(is_first)
    def _():
        wait_gather(slot)
        base = pl.multiple_of(slot * (tme * TOKEN_ROWS), tme * TOKEN_ROWS)
        x1 = jnp.concatenate([xbuf[pl.ds(base + j, tme, stride=TOKEN_ROWS), :] for j in range(TOKEN_ROWS)], axis=1)
        acc[...] = x1
        hbuf[...] = _rms(x1, nffn[...]).astype(BF16)
        rt = rt_ref[0]
        rcol[...] = jnp.transpose(jnp.concatenate([rt, jnp.zeros((LANES - rt.shape[0], tme), F32)], axis=0))

    valid = ivalid[i] == 1
    has2 = ihas2[i] == 1
    do_gather = jnp.logical_and(is_first, t + 1 < ntiles)
    do_scatter = jnp.logical_and(is_first, t >= 1)
    half = tme // 2
    scatter_rows = lambda r0, r1: (lambda: start_rows_inline(scatter_row, t - 1, 1 - slot, r0, r1))
    expert_with(ie0[i], valid, [(do_gather, lambda: start_rows_inline(gather_row, t + 1, 1 - slot))])
    expert_with(ie1[i], valid, [(jnp.logical_and(do_scatter, has2), scatter_rows(0, half)),
                                (jnp.logical_and(do_scatter, jnp.logical_not(has2)), scatter_rows(0, tme))])
    expert_with(ie2[i], has2, [(jnp.logical_and(do_scatter, has2), scatter_rows(half, tme))])
    for e in range(EXPERTS_PER_GROUP):
        @pl.when(flags[i * EXPERTS_PER_GROUP + e] == 1)
        def _():
            expert(e)

    @pl.when(ilast[i] == 1)
    def _():
        @pl.when(t >= 2)
        def _():
            wait_scatter(slot)

        ybuf[slot] = _rms(acc[...], nfin[...])

        @pl.when(t == ntiles - 1)
        def _():
            start_rows(scatter_row, t, slot)

    @pl.when(i == pl.num_programs(0) - 1)
    def _():
        for sl in range(min(2, ntiles)):
            wait_scatter(sl)


def _moe(x1t, rt, weg, weu, wed, nffn, nfin):
    n = x1t.shape[0] // TOKEN_ROWS
    tme = min(MOE_TILE, n)
    assert n % tme == 0
    ntiles = n // tme
    plan, rt_sorted = _moe_plan(rt, tme)
    max_items = ntiles + N_GROUPS - 1
    grp = lambda i, order, itile, igroup, *_: (igroup[i], 0, 0)
    til = lambda i, order, itile, *_: (itile[i], 0, 0)
    const = lambda i, *_: (0, 0)
    grid_spec = pltpu.PrefetchScalarGridSpec(
        num_scalar_prefetch=len(plan),
        grid=(max_items,),
        in_specs=[pl.BlockSpec(memory_space=pl.ANY),
                  pl.BlockSpec((1,) + rt_sorted.shape[1:], til),
                  pl.BlockSpec((EXPERTS_PER_GROUP, D_MODEL, EXPERT_FF), grp),
                  pl.BlockSpec((EXPERTS_PER_GROUP, D_MODEL, EXPERT_FF), grp),
                  pl.BlockSpec((EXPERTS_PER_GROUP, EXPERT_FF, D_MODEL), grp),
                  pl.BlockSpec((1, D_MODEL), const), pl.BlockSpec((1, D_MODEL), const)],
        out_specs=pl.BlockSpec(memory_space=pl.ANY),
        scratch_shapes=[pltpu.VMEM((2 * tme * TOKEN_ROWS, LANES), F32), pltpu.VMEM((2, tme, D_MODEL), F32),
                        pltpu.VMEM((tme, D_MODEL), F32), pltpu.VMEM((tme, D_MODEL), BF16),
                        pltpu.VMEM((tme, LANES), F32),
                        pltpu.SemaphoreType.DMA((2,)), pltpu.SemaphoreType.DMA((2,))],
    )
    return pl.pallas_call(
        functools.partial(_moe_kernel, tme=tme, ntiles=ntiles),
        grid_spec=grid_spec,
        out_shape=jax.ShapeDtypeStruct((n, D_MODEL), F32),
        compiler_params=_cparams(("arbitrary",)),
        name="moe",
    )(*plan, x1t, rt_sorted, weg, weu, wed, nffn, nfin)


def _rope_tables(positions):
    half = HEAD_DIM // 2
    inv_freq = ROPE_THETA ** (-jnp.arange(half, dtype=F32) / half)
    ang = positions.astype(F32)[:, None] * inv_freq[None, :]
    cos, sin = jnp.cos(ang), jnp.sin(ang)
    reps = LANES // HEAD_DIM
    return (jnp.tile(jnp.concatenate([cos, cos], -1), (1, reps)),
            jnp.tile(jnp.concatenate([-sin, sin], -1), (1, reps)))


def _prep_weights(norm_mix, w_in, w_gla_f2, b_gla_f, gla_norm, w_proj_attn, w_proj_gla, w_out, norm_ffn,
                  w_router_group, b_router_group, w_router_expert, b_router_expert):
    w = {"win": w_in.astype(BF16)}
    w["wf2"] = jnp.pad(w_gla_f2.astype(BF16), ((0, LANES - GLA_GATE_RANK), (0, 0)))
    w["bf"] = b_gla_f.reshape(1, -1)
    w["norm_mix"] = norm_mix.reshape(1, -1)
    w["gla_norm"] = gla_norm.reshape(1, -1)
    w["wpa"] = w_proj_attn.astype(BF16)
    w["wpg"] = w_proj_gla.astype(BF16)
    w["wo"] = w_out.astype(BF16)
    w["nf"] = norm_ffn.reshape(1, -1)
    pad = LANES - N_EXPERTS - N_GROUPS
    wr_t = jnp.pad(jnp.concatenate([w_router_expert, w_router_group], axis=1), ((0, 0), (0, pad))).T
    w["wr_hi"], w["wr_lo"] = _split_bf16(wr_t)
    w["br"] = jnp.broadcast_to(jnp.pad(jnp.concatenate([b_router_expert, b_router_group]), (0, pad))[:, None],
                               (LANES, LANES))
    return w


def _layer(x, positions_tab, cache, w, sinks, weg, weu, wed, nfin, tm):
    batch, seq, _ = x.shape
    n = batch * seq
    x2d = x.reshape(n, D_MODEL)
    cos_t, sin_t = positions_tab
    qa, ka, va, qg, kg, vg, rg, la, ga, gb = _inproj(x2d, w["norm_mix"], cos_t, sin_t, w, tm)
    if cache is None:
        a_out = _swa_prompt(sinks, qa, ka, va, batch, seq)
        last = lambda z: z.reshape(batch, seq, KV_WIDTH)[:, seq - WINDOW:].reshape(batch, WINDOW, ATTN_KV_HEADS, HEAD_DIM)
        new_k, new_v = last(ka), last(va)
        g_out, new_s = _gla_prompt(qg, kg, vg, la, rg, w["gla_norm"], batch, seq)
    else:
        cache_k, cache_v, state = cache
        a_out, new_k, new_v = _swa_sample(sinks, qa, ka, va, cache_k.reshape(batch, WINDOW, KV_WIDTH),
                                          cache_v.reshape(batch, WINDOW, KV_WIDTH), batch, seq)
        new_k = new_k.reshape(batch, WINDOW, ATTN_KV_HEADS, HEAD_DIM)
        new_v = new_v.reshape(batch, WINDOW, ATTN_KV_HEADS, HEAD_DIM)
        g_out, new_s = _gla_sample(qg, kg, vg, la, rg, state, w["gla_norm"], batch, seq)
    x1t, rt = _post(x2d, a_out, g_out, ga, gb, w, tm * 2 if x2d.shape[0] % (tm * 2) == 0 else tm)
    y = _moe(x1t, rt, weg, weu, wed, w["nf"], nfin)
    return y.reshape(batch, seq, D_MODEL), new_k, new_v, new_s


def kernel(x_prompt, x_sample, cache_win_k, cache_win_v, state_gla, norm_mix, w_in, w_gla_f2, b_gla_f, gla_norm,
           attn_sinks, w_proj_attn, w_proj_gla, w_out, norm_ffn, w_router_group, b_router_group, w_router_expert,
           b_router_expert, w_exp_gate, w_exp_up, w_exp_down, norm_final):
    assert norm_mix.shape[0] == 1, "single-layer step"
    seq_p = x_prompt.shape[1]
    dec_b, dec_t = x_sample.shape[0], x_sample.shape[1]
    w = _prep_weights(norm_mix[0], w_in[0], w_gla_f2[0], b_gla_f[0], gla_norm[0], w_proj_attn[0], w_proj_gla[0],
                      w_out[0], norm_ffn[0], w_router_group[0], b_router_group[0], w_router_expert[0],
                      b_router_expert[0])
    weg = w_exp_gate[0].astype(BF16)
    weu = w_exp_up[0].astype(BF16)
    wed = w_exp_down[0].astype(BF16)
    nfin = norm_final.reshape(1, -1)
    sinks = attn_sinks[0]
    tab_p = _rope_tables(jnp.arange(seq_p, dtype=jnp.int32))
    pos_s = PAST_LEN + jnp.arange(dec_t, dtype=jnp.int32)
    tab_s = tuple(jnp.tile(t, (dec_b, 1)) for t in _rope_tables(pos_s))
    tm_p = min(512, seq_p)
    tm_s = dec_b * dec_t
    yp, pk, pv, ps = _layer(x_prompt, tab_p, None, w, sinks, weg, weu, wed, nfin, tm_p)
    ys, sk, sv, ss = _layer(x_sample, tab_s, (cache_win_k[0], cache_win_v[0], state_gla[0]), w, sinks, weg, weu, wed,
                            nfin, tm_s)
    return (yp, ys, pk[None], pv[None], ps[None], sk[None], sv[None], ss[None])
```

```python
import functools
import math

import numpy as np
import jax
import jax.numpy as jnp
from jax import lax
from jax.experimental import pallas as pl
from jax.experimental.pallas import tpu as pltpu

F32 = jnp.float32
BF16 = jnp.bfloat16

D_MODEL = 1024
ATTN_HEADS = 8
ATTN_KV_HEADS = 2
GROUP = ATTN_HEADS // ATTN_KV_HEADS
HEAD_DIM = 64
ATTN_WIDTH = ATTN_HEADS * HEAD_DIM
KV_WIDTH = ATTN_KV_HEADS * HEAD_DIM
WINDOW = 128
ROPE_THETA = 10000.0
PAST_LEN = 8192
GLA_HEADS = 4
GLA_KEY_WIDTH = D_MODEL // 2
GLA_VALUE_WIDTH = D_MODEL
GLA_DK = GLA_KEY_WIDTH // GLA_HEADS
GLA_DV = GLA_VALUE_WIDTH // GLA_HEADS
GLA_GATE_RANK = 16
GLA_GATE_NORMALIZER = 16.0
N_GROUPS = 4
EXPERTS_PER_GROUP = 8
N_EXPERTS = N_GROUPS * EXPERTS_PER_GROUP
EXPERT_FF = 256
EPS = 1e-6
LOG2_E = 1.4426950408889634

LANES = 128
GLA_CHUNK = 128
GLA_HEADS_PER_STEP = 4
SWA_BLOCKS_PER_STEP = 4
SAMPLE_SEQ_BLOCK = 16
POST_PART_ROWS = 512
MOE_TILE = 256
TOKEN_ROWS = D_MODEL // LANES
INPROJ_WIDTHS = (ATTN_WIDTH, 2 * KV_WIDTH, GLA_KEY_WIDTH, GLA_KEY_WIDTH, GLA_VALUE_WIDTH, GLA_VALUE_WIDTH)
ROUTER_ROWS = 40
VMEM_LIMIT = 56 * 1024 * 1024


def _cparams(sem):
    return pltpu.CompilerParams(dimension_semantics=sem, vmem_limit_bytes=VMEM_LIMIT)


def _rms(x, g):
    return x * lax.rsqrt(jnp.mean(x * x, axis=-1, keepdims=True) + EPS) * g


def _sigmoid(x):
    return 1.0 / (1.0 + jnp.exp(-x))


def _dot(a, b):
    return jnp.dot(a, b, preferred_element_type=F32)


def _dot_nt(a, b):
    return lax.dot_general(a, b, (((1,), (1,)), ((), ())), preferred_element_type=F32)


def _dot_tn(a, b):
    return lax.dot_general(a, b, (((0,), (0,)), ((), ())), preferred_element_type=F32)


def _split_bf16(x):
    hi = x.astype(BF16)
    lo = (x - hi.astype(F32)).astype(BF16)
    return hi, lo


def _inproj_kernel(x_ref, g_ref, cos_ref, sin_ref, win, wf2, bf,
                   qa_o, ka_o, va_o, qg_o, kg_o, vg_o, rg_o, la_o, ga_o, gb_o, wgate):
    f0 = sum(INPROJ_WIDTHS)

    @pl.when(pl.program_id(0) == 0)
    def _():
        tail = win[:, f0:]
        wgate[...] = tail[:, GLA_GATE_RANK:GLA_GATE_RANK + 2 * D_MODEL]

    tm = x_ref.shape[0]
    halves = [slice(0, tm // 2), slice(tm // 2, tm)] if tm % 16 == 0 else [slice(0, tm)]
    hbs = [_rms(x_ref[r, :], g_ref[...]).astype(BF16) for r in halves]
    cols = np.cumsum((0,) + INPROJ_WIDTHS[:-1])
    wqa, wkva, wqg, wkg, wvg, wrg = (win.at[:, int(c):int(c) + wd] for c, wd in zip(cols, INPROJ_WIDTHS))
    wga = wgate.at[:, :D_MODEL]
    wgb = wgate.at[:, D_MODEL:]
    wf = win.at[:, f0:f0 + LANES]
    lane = lax.broadcasted_iota(jnp.int32, (halves[0].stop - halves[0].start, LANES), 1)
    first_half = (lane % HEAD_DIM) < (HEAD_DIM // 2)

    def rope(t, r):
        swapped = jnp.where(first_half, pltpu.roll(t, LANES - HEAD_DIM // 2, 1), pltpu.roll(t, HEAD_DIM // 2, 1))
        return t * cos_ref[r, :] + swapped * sin_ref[r, :]

    for r, hb in zip(halves, hbs):
        qa = _dot(hb, wqa[...])
        for c in range(ATTN_WIDTH // LANES):
            sl = slice(c * LANES, (c + 1) * LANES)
            qa_o[r, sl] = (rope(qa[:, sl], r) * (HEAD_DIM ** -0.5)).astype(BF16)
    for r, hb in zip(halves, hbs):
        kva = _dot(hb, wkva[...])
        ka_o[r, :] = rope(kva[:, :KV_WIDTH], r)
        va_o[r, :] = kva[:, KV_WIDTH:]
    for r, hb in zip(halves, hbs):
        qg_o[r, :] = (_dot(hb, wqg[...]) * (GLA_DK ** -0.5)).astype(BF16)
    for r, hb in zip(halves, hbs):
        kg_o[r, :] = _dot(hb, wkg[...]).astype(BF16)
    for r, hb in zip(halves, hbs):
        vg_o[r, :] = _dot(hb, wvg[...]).astype(BF16)
    for r, hb in zip(halves, hbs):
        rg_o[r, :] = _dot(hb, wrg[...]).astype(BF16)
    for r, hb in zip(halves, hbs):
        z = _dot(_dot(hb, wf[...]).astype(BF16), wf2[...]) + bf[...]
        la_o[r, :] = (jnp.minimum(z, 0.0) - jnp.log1p(jnp.exp(-jnp.abs(z)))) * (1.0 / GLA_GATE_NORMALIZER)
    for r, hb in zip(halves, hbs):
        ga_o[r, :] = _sigmoid(_dot(hb, wga[...])).astype(BF16)
    for r, hb in zip(halves, hbs):
        gb_o[r, :] = _sigmoid(_dot(hb, wgb[...])).astype(BF16)


def _inproj(x2d, norm_g, cos_t, sin_t, w, tm):
    n = x2d.shape[0]
    ntab = cos_t.shape[0] // tm
    row = lambda i: (i, 0)
    const = lambda i: (0, 0)
    tab = lambda i: (i % ntab, 0)
    wnames = ("win", "wf2", "bf")
    wspecs = [pl.BlockSpec(w[k].shape, const) for k in wnames]
    widths = (ATTN_WIDTH, KV_WIDTH, KV_WIDTH, GLA_KEY_WIDTH, GLA_KEY_WIDTH, GLA_VALUE_WIDTH, GLA_VALUE_WIDTH,
              GLA_KEY_WIDTH, D_MODEL, D_MODEL)
    dtypes = (BF16, F32, F32, BF16, BF16, BF16, BF16, F32, BF16, BF16)
    return pl.pallas_call(
        _inproj_kernel,
        grid=(n // tm,),
        in_specs=[pl.BlockSpec((tm, D_MODEL), row), pl.BlockSpec((1, D_MODEL), const),
                  pl.BlockSpec((tm, LANES), tab), pl.BlockSpec((tm, LANES), tab)] + wspecs,
        out_specs=[pl.BlockSpec((tm, wd), row) for wd in widths],
        out_shape=[jax.ShapeDtypeStruct((n, wd), dt) for wd, dt in zip(widths, dtypes)],
        scratch_shapes=[pltpu.VMEM((D_MODEL, 2 * D_MODEL), BF16)],
        compiler_params=_cparams(("arbitrary",)),
        name="inproj",
    )(x2d, norm_g, cos_t, sin_t, *[w[k] for k in wnames])


def _swa_prompt_kernel(sink_ref, q_ref, kc_ref, kp_ref, vc_ref, vp_ref, o_ref, *, qb):
    n = pl.program_id(1)
    k3 = jnp.concatenate([kp_ref[...], kc_ref[...]], axis=0).astype(BF16)
    v3 = jnp.concatenate([vp_ref[...], vc_ref[...]], axis=0).astype(BF16)
    t = lax.broadcasted_iota(jnp.int32, (WINDOW, 2 * WINDOW), 0)
    j = lax.broadcasted_iota(jnp.int32, (WINDOW, 2 * WINDOW), 1)
    band = (j >= t) & (j <= t + WINDOW)
    zeros = jnp.zeros((2 * WINDOW, HEAD_DIM), BF16)
    ones = jnp.ones((2 * WINDOW, HEAD_DIM), BF16)
    lane = lax.broadcasted_iota(jnp.int32, (WINDOW, LANES), 1)
    chains = [(blk, h) for blk in range(qb) for h in range(ATTN_HEADS)]
    scores = []
    for blk, h in chains:
        kv = h // GROUP
        rows = slice(blk * WINDOW, (blk + 1) * WINDOW)
        keys = slice(blk * WINDOW, (blk + 2) * WINDOW)
        s = _dot_nt(q_ref[rows, h * HEAD_DIM:(h + 1) * HEAD_DIM], k3[keys, kv * HEAD_DIM:(kv + 1) * HEAD_DIM])
        valid = band & ((j >= WINDOW) | (n > 0)) if blk == 0 else band
        scores.append(jnp.where(valid, s, -jnp.inf))
    probs, sink_terms = [], []
    for (blk, h), s in zip(chains, scores):
        m = jnp.maximum(jnp.max(s, axis=-1, keepdims=True), sink_ref[h])
        probs.append(jnp.exp(s - m).astype(BF16))
        sink_terms.append(jnp.exp(sink_ref[h] - m))
    for blk in range(qb):
        rows = slice(blk * WINDOW, (blk + 1) * WINDOW)
        keys = slice(blk * WINDOW, (blk + 2) * WINDOW)
        for kv in range(ATTN_KV_HEADS):
            vv = v3[keys, kv * HEAD_DIM:(kv + 1) * HEAD_DIM]
            vext = (jnp.concatenate([vv, zeros, ones, zeros], axis=1), jnp.concatenate([zeros, vv, zeros, ones], axis=1))
            for pr in range(GROUP // 2):
                h0 = kv * GROUP + pr * 2
                c0 = blk * ATTN_HEADS + h0
                acc = _dot(probs[c0], vext[0]) + _dot(probs[c0 + 1], vext[1])
                l = acc[:, LANES:] + jnp.where(lane < HEAD_DIM, sink_terms[c0], sink_terms[c0 + 1])
                o_ref[rows, h0 * HEAD_DIM:h0 * HEAD_DIM + LANES] = (acc[:, :LANES] / l).astype(BF16)


def _swa_prompt(sinks, qa, ka, va, batch, seq):
    nb = seq // WINDOW
    qb = math.gcd(SWA_BLOCKS_PER_STEP, nb)
    steps = nb // qb
    cur = lambda b, n: (b * steps + n, 0)
    prev = lambda b, n: (b * nb + jnp.maximum(n * qb - 1, 0), 0)
    return pl.pallas_call(
        functools.partial(_swa_prompt_kernel, qb=qb),
        grid=(batch, steps),
        in_specs=[pl.BlockSpec(memory_space=pltpu.SMEM),
                  pl.BlockSpec((qb * WINDOW, ATTN_WIDTH), cur),
                  pl.BlockSpec((qb * WINDOW, KV_WIDTH), cur), pl.BlockSpec((WINDOW, KV_WIDTH), prev),
                  pl.BlockSpec((qb * WINDOW, KV_WIDTH), cur), pl.BlockSpec((WINDOW, KV_WIDTH), prev)],
        out_specs=pl.BlockSpec((qb * WINDOW, ATTN_WIDTH), cur),
        out_shape=jax.ShapeDtypeStruct(qa.shape, BF16),
        compiler_params=_cparams(("parallel", "parallel")),
        name="swa_prompt",
    )(sinks, qa, ka, ka, va, va)


def _swa_sample_kernel(sink_ref, q_ref, kn_ref, vn_ref, kn3_ref, vn3_ref, ck_ref, cv_ref, o_ref, nk_ref, nv_ref,
                       *, t_new):
    sb = ck_ref.shape[0]
    spv = 8 // t_new
    nq = GROUP * 8
    nc = spv * WINDOW
    qi = lax.broadcasted_iota(jnp.int32, (nq, nc), 0) % 8
    ci = lax.broadcasted_iota(jnp.int32, (nq, nc), 1)
    valid_c = (qi // t_new == ci // WINDOW) & (ci % WINDOW >= qi % t_new)
    qn = lax.broadcasted_iota(jnp.int32, (nq, 8), 0) % 8
    cn = lax.broadcasted_iota(jnp.int32, (nq, 8), 1)
    valid_n = (qn // t_new == cn // t_new) & (cn <= qn)
    grow = lax.broadcasted_iota(jnp.int32, (nq, 1), 0) // 8
    chains = [(vr, kv) for vr in range(sb // spv) for kv in range(ATTN_KV_HEADS)]
    scored = []
    for vr, kv in chains:
        r8 = slice(8 * vr, 8 * vr + 8)
        cs = slice(kv * HEAD_DIM, (kv + 1) * HEAD_DIM)
        heads = [kv * GROUP + g for g in range(GROUP)]
        qs = jnp.concatenate([q_ref[r8, h * HEAD_DIM:(h + 1) * HEAD_DIM] for h in heads], axis=0)
        ck = ck_ref[vr * spv:(vr + 1) * spv, :, cs].reshape(nc, HEAD_DIM).astype(BF16)
        kn = kn_ref[r8, cs].astype(BF16)
        scored.append((jnp.where(valid_c, _dot_nt(qs, ck), -jnp.inf), jnp.where(valid_n, _dot_nt(qs, kn), -jnp.inf)))
    soft = []
    for (vr, kv), (s_c, s_n) in zip(chains, scored):
        sink = sink_ref[kv * GROUP]
        for g in range(1, GROUP):
            sink = jnp.where(grow == g, sink_ref[kv * GROUP + g], sink)
        m = jnp.maximum(jnp.maximum(jnp.max(s_c, axis=-1, keepdims=True), jnp.max(s_n, axis=-1, keepdims=True)), sink)
        p_c = jnp.exp(s_c - m)
        p_n = jnp.exp(s_n - m)
        l = jnp.sum(p_c, axis=-1, keepdims=True) + jnp.sum(p_n, axis=-1, keepdims=True) + jnp.exp(sink - m)
        soft.append((p_c.astype(BF16), p_n.astype(BF16), l))
    for (vr, kv), (p_c, p_n, l) in zip(chains, soft):
        r8 = slice(8 * vr, 8 * vr + 8)
        cs = slice(kv * HEAD_DIM, (kv + 1) * HEAD_DIM)
        cv = cv_ref[vr * spv:(vr + 1) * spv, :, cs].reshape(nc, HEAD_DIM).astype(BF16)
        o = (_dot(p_c, cv) + _dot(p_n, vn_ref[r8, cs].astype(BF16))) / l
        for a in range(GROUP // 2):
            pair = jnp.concatenate([o[16 * a:16 * a + 8], o[16 * a + 8:16 * a + 16]], axis=1)
            c0 = (kv * GROUP + 2 * a) * HEAD_DIM
            o_ref[r8, c0:c0 + LANES] = pair.astype(BF16)
    nk_ref[:, 0:WINDOW - t_new, :] = ck_ref[:, t_new:WINDOW, :]
    nk_ref[:, WINDOW - t_new:WINDOW, :] = kn3_ref[...]
    nv_ref[:, 0:WINDOW - t_new, :] = cv_ref[:, t_new:WINDOW, :]
    nv_ref[:, WINDOW - t_new:WINDOW, :] = vn3_ref[...]


def _swa_sample(sinks, qa, ka, va, cache_k, cache_v, batch, t_new):
    sb = SAMPLE_SEQ_BLOCK
    rows = sb * t_new
    r2 = lambda i: (i, 0)
    r3 = lambda i: (i, 0, 0)
    ka3 = ka.reshape(batch, t_new, KV_WIDTH)
    va3 = va.reshape(batch, t_new, KV_WIDTH)
    return pl.pallas_call(
        functools.partial(_swa_sample_kernel, t_new=t_new),
        grid=(batch // sb,),
        in_specs=[pl.BlockSpec(memory_space=pltpu.SMEM),
                  pl.BlockSpec((rows, ATTN_WIDTH), r2),
                  pl.BlockSpec((rows, KV_WIDTH), r2), pl.BlockSpec((rows, KV_WIDTH), r2),
                  pl.BlockSpec((sb, t_new, KV_WIDTH), r3), pl.BlockSpec((sb, t_new, KV_WIDTH), r3),
                  pl.BlockSpec((sb, WINDOW, KV_WIDTH), r3), pl.BlockSpec((sb, WINDOW, KV_WIDTH), r3)],
        out_specs=[pl.BlockSpec((rows, ATTN_WIDTH), r2),
                   pl.BlockSpec((sb, WINDOW, KV_WIDTH), r3), pl.BlockSpec((sb, WINDOW, KV_WIDTH), r3)],
        out_shape=[jax.ShapeDtypeStruct(qa.shape, BF16),
                   jax.ShapeDtypeStruct(cache_k.shape, F32), jax.ShapeDtypeStruct(cache_v.shape, F32)],
        compiler_params=_cparams(("parallel",)),
        name="swa_sample",
    )(sinks, qa, ka, va, ka3, va3, cache_k, cache_v)


def _gla_constants(c, seg, with_rem):
    t = np.arange(c)
    sid = t // seg
    same = sid[:, None] == sid[None, :]
    levels = []
    m = seg // 2
    while m >= 1:
        levels.append(m)
        m //= 2
    mats, roles, masks = [], [], []
    for m in levels:
        blk = t // (2 * m)
        second = (t // m) % 2 == 1
        p = blk * 2 * m + m - 1
        u = t[None, :]
        mq = (u > p[:, None]) & (u <= t[:, None])
        mk = (u > t[:, None]) & (u <= p[:, None])
        if m > 1:
            mats.append(np.where(second[:, None], mq, mk))
        roles.append(np.broadcast_to(second[:, None], (c, LANES)))
        masks.append((blk[:, None] == blk[None, :]) & second[:, None] & ~second[None, :])
    masks.append(np.eye(c, dtype=bool))
    mats.append(same & (t[None, :] <= t[:, None]))
    if with_rem:
        mats.append(same & (t[None, :] > t[:, None]))
    mall = np.concatenate(mats, 0).astype(np.float32)
    mall = jnp.asarray(np.concatenate([mall, mall], 1), BF16)
    role = jnp.asarray(np.concatenate(roles, 0).astype(np.float32))
    mask = jnp.asarray(np.concatenate(masks, 0).astype(np.float32))
    return len(levels), mall, role, mask


def _gla_exponents(la, mall):
    la2 = la * LOG2_E
    hl = jnp.concatenate(_split_bf16(la2), axis=0)
    return _dot(mall, hl), hl, la2


def _gla_scores(qb, kb, e2, la2, role_ref, mask_ref, nlev, c):
    qf = qb.astype(F32)
    kf = kb.astype(F32)
    terms = []
    for lv in range(nlev):
        sl = slice(lv * c, (lv + 1) * c)
        m = 1 << (nlev - 1 - lv)
        if m % 8 == 0:
            pe = jnp.exp2(e2[sl])
            x = jnp.concatenate([(qf if blk % 2 else kf)[blk * m:(blk + 1) * m] * pe[blk * m:(blk + 1) * m]
                                 for blk in range(c // m)], axis=0).astype(BF16)
        else:
            second = role_ref[sl, :] > 0.5
            e = e2[sl] if lv < nlev - 1 else jnp.where(second, la2, 0.0)
            x = (jnp.where(second, qf, kf) * jnp.exp2(e)).astype(BF16)
        terms.append((x, x, lv))
    terms.append((qb, kb, nlev))
    mask = lambda i: mask_ref[i * c:(i + 1) * c, :]
    a = None
    pair = c % LANES == 0
    while terms:
        if pair and len(terms) >= 2:
            (l0, r0, i0), (l1, r1, i1) = terms.pop(), terms.pop()
            z = jnp.zeros_like(r0)
            rhs = jnp.concatenate([jnp.concatenate([r0, z], axis=1), jnp.concatenate([z, r1], axis=1)], axis=0)
            g = _dot_nt(jnp.concatenate([l0, l1], axis=1), rhs)
            t = mask(i0) * g[:, :c] + mask(i1) * g[:, c:]
        else:
            l0, r0, i0 = terms.pop()
            t = mask(i0) * _dot_nt(l0, r0)
        a = t if a is None else a + t
    return a, qf, kf


def _gla_out(o, g, r):
    r = r.astype(F32)
    return (_rms(o, g) * (r * _sigmoid(r))).astype(BF16)


def _gla_prompt_kernel(q_ref, k_ref, v_ref, la_ref, r_ref, mall_ref, role_ref, mask_ref, g_ref, o_ref, s_ref,
                       s_scr, *, nlev, nchunks):
    c = GLA_CHUNK
    hp = GLA_HEADS_PER_STEP
    s_scr[...] = jnp.zeros_like(s_scr)

    def chunk(i, carry):
        rows = pl.ds(pl.multiple_of(i * c, c), c)
        e2_all, _, la2_all = _gla_exponents(la_ref[rows, :], mall_ref[...])
        ksl = [slice(h * GLA_DK, (h + 1) * GLA_DK) for h in range(hp)]
        vsl = [slice(h * GLA_DV, (h + 1) * GLA_DV) for h in range(hp)]
        inter = []
        for h in range(hp):
            e2 = e2_all[:, ksl[h]]
            v = v_ref[rows, vsl[h]]
            qf = q_ref[rows, ksl[h]].astype(F32)
            kf = k_ref[rows, ksl[h]].astype(F32)
            s = s_scr[h]
            b = e2[(nlev - 1) * c:nlev * c]
            inter.append(_dot((qf * jnp.exp2(b)).astype(BF16), s.astype(BF16)))
            kt = (kf * jnp.exp2(b[c - 1:c, :] - b)).astype(BF16)
            dec = jnp.exp2(jnp.transpose(b[c - 8:c, :]))[:, 7:8]
            s_scr[h] = dec * s + _dot_tn(kt, v)
        scores = [_gla_scores(q_ref[rows, ksl[h]], k_ref[rows, ksl[h]], e2_all[:, ksl[h]], la2_all[:, ksl[h]], role_ref,
                              mask_ref, nlev, c)[0] for h in range(hp)]
        for h in range(hp):
            o = _dot(scores[h].astype(BF16), v_ref[rows, vsl[h]]) + inter[h]
            o_ref[rows, vsl[h]] = _gla_out(o, g_ref[...], r_ref[rows, vsl[h]])
        return carry

    lax.fori_loop(0, nchunks, chunk, 0)
    s_ref[0] = s_scr[...]


def _gla_prompt(qg, kg, vg, la, rg, gnorm, batch, seq):
    nlev, mall, role, mask = _gla_constants(GLA_CHUNK, GLA_CHUNK, with_rem=False)
    hp = GLA_HEADS_PER_STEP
    bh = lambda b, h: (b, h)
    const = lambda b, h: (0, 0)
    return pl.pallas_call(
        functools.partial(_gla_prompt_kernel, nlev=nlev, nchunks=seq // GLA_CHUNK),
        grid=(batch, GLA_HEADS // hp),
        in_specs=[pl.BlockSpec((seq, hp * GLA_DK), bh), pl.BlockSpec((seq, hp * GLA_DK), bh),
                  pl.BlockSpec((seq, hp * GLA_DV), bh), pl.BlockSpec((seq, hp * GLA_DK), bh),
                  pl.BlockSpec((seq, hp * GLA_DV), bh),
                  pl.BlockSpec(mall.shape, const), pl.BlockSpec(role.shape, const), pl.BlockSpec(mask.shape, const),
                  pl.BlockSpec((1, GLA_DV), const)],
        out_specs=[pl.BlockSpec((seq, hp * GLA_DV), bh),
                   pl.BlockSpec((1, hp, GLA_DK, GLA_DV), lambda b, h: (b, h, 0, 0))],
        out_shape=[jax.ShapeDtypeStruct(vg.shape, BF16),
                   jax.ShapeDtypeStruct((batch, GLA_HEADS, GLA_DK, GLA_DV), F32)],
        scratch_shapes=[pltpu.VMEM((hp, GLA_DK, GLA_DV), F32)],
        compiler_params=_cparams(("parallel", "parallel")),
        name="gla_prompt",
    )(qg, kg, vg, la, rg, mall, role, mask, gnorm)


def _gla_sample_kernel(q_ref, k_ref, v_ref, la_ref, r_ref, s0_ref, mall_ref, role_ref, mask_ref, msum_ref, g_ref,
                       o_ref, s_ref, *, nlev, t_new):
    sb = s0_ref.shape[0]
    c = sb * t_new
    spv = 8 // t_new
    e2_all, hl_all, la2_all = _gla_exponents(la_ref[...], mall_ref[...])
    seq_in_tile = lax.broadcasted_iota(jnp.int32, (8, GLA_DV), 0) // t_new
    seq_of_row = lax.broadcasted_iota(jnp.int32, (c, GLA_DV), 0) // t_new
    ksl = [slice(h * GLA_DK, (h + 1) * GLA_DK) for h in range(GLA_HEADS)]
    scores = [_gla_scores(q_ref[:, ksl[h]], k_ref[:, ksl[h]], e2_all[:, ksl[h]], la2_all[:, ksl[h]], role_ref, mask_ref,
                          nlev, c) for h in range(GLA_HEADS)]
    for h in range(GLA_HEADS):
        ks = ksl[h]
        vs = slice(h * GLA_DV, (h + 1) * GLA_DV)
        v = v_ref[:, vs]
        e2 = e2_all[:, ks]
        hl = hl_all[:, ks]
        a, qf, kf = scores[h]
        qe = (qf * jnp.exp2(e2[(nlev - 1) * c:nlev * c])).astype(BF16)
        kt_t = jnp.transpose(kf * jnp.exp2(e2[nlev * c:(nlev + 1) * c])).astype(BF16)
        bl_t = jnp.transpose(_dot(msum_ref[...], hl[:c]) + _dot(msum_ref[...], hl[c:]))
        dec_t = jnp.exp2(bl_t)
        inter = []
        for vr in range(c // 8):
            rows8 = qe[8 * vr:8 * vr + 8]
            tile = None
            for u in range(spv):
                j = vr * spv + u
                s0 = s0_ref[j, h]
                r = _dot(rows8, s0.astype(BF16))
                tile = r if tile is None else jnp.where(seq_in_tile == u, r, tile)
                vj = jnp.where(seq_of_row == j, v, jnp.zeros_like(v))
                s_ref[j, h] = dec_t[:, j:j + 1] * s0 + _dot(kt_t, vj)
            inter.append(tile)
        o = _dot(a.astype(BF16), v) + jnp.concatenate(inter, axis=0)
        o_ref[:, vs] = _gla_out(o, g_ref[...], r_ref[:, vs])


def _gla_sample(qg, kg, vg, la, rg, state, gnorm, batch, t_new):
    sb = SAMPLE_SEQ_BLOCK
    c = sb * t_new
    assert 8 % t_new == 0 and c % 8 == 0
    nlev, mall, role, mask = _gla_constants(c, t_new, with_rem=True)
    msum = jnp.asarray((np.arange(c)[None, :] // t_new == np.arange(sb)[:, None]).astype(np.float32), BF16)
    rows = lambda i: (i, 0)
    const = lambda i: (0, 0)
    st = lambda i: (i, 0, 0, 0)
    return pl.pallas_call(
        functools.partial(_gla_sample_kernel, nlev=nlev, t_new=t_new),
        grid=(batch // sb,),
        in_specs=[pl.BlockSpec((c, GLA_KEY_WIDTH), rows), pl.BlockSpec((c, GLA_KEY_WIDTH), rows),
                  pl.BlockSpec((c, GLA_VALUE_WIDTH), rows), pl.BlockSpec((c, GLA_KEY_WIDTH), rows),
                  pl.BlockSpec((c, GLA_VALUE_WIDTH), rows),
                  pl.BlockSpec((sb, GLA_HEADS, GLA_DK, GLA_DV), st),
                  pl.BlockSpec(mall.shape, const), pl.BlockSpec(role.shape, const), pl.BlockSpec(mask.shape, const),
                  pl.BlockSpec(msum.shape, const), pl.BlockSpec((1, GLA_DV), const)],
        out_specs=[pl.BlockSpec((c, GLA_VALUE_WIDTH), rows), pl.BlockSpec((sb, GLA_HEADS, GLA_DK, GLA_DV), st)],
        out_shape=[jax.ShapeDtypeStruct(vg.shape, BF16), jax.ShapeDtypeStruct(state.shape, F32)],
        compiler_params=_cparams(("parallel",)),
        name="gla_sample",
    )(qg, kg, vg, la, rg, state, mall, role, mask, msum, gnorm)


def _post_kernel(x_ref, a_ref, gl_ref, ga_ref, gb_ref, wpa, wpg, wo, nf_ref, wr_hi, wr_lo, br, x1t_o, rt_o):
    tm = x_ref.shape[0]
    part = POST_PART_ROWS if tm % POST_PART_ROWS == 0 else tm
    halves = [slice(r0, r0 + part) for r0 in range(0, tm, part)]
    proj =[(_dot(a_ref[r, :], wpa[...]), _dot(gl_ref[r, :], wpg[...])) for r in halves]
    x1s = []
    for r, (pa, pg) in zip(halves, proj):
        merged = ga_ref[r, :].astype(F32) * pa + gb_ref[r, :].astype(F32) * pg
        x1s.append(x_ref[r, :] + _dot(merged.astype(BF16), wo[...]))
    logits = []
    for r, x1 in zip(halves, x1s):
        rows = r.stop - r.start
        for j in range(TOKEN_ROWS):
            x1t_o[pl.ds(r.start * TOKEN_ROWS + j, rows, stride=TOKEN_ROWS), :] = x1[:, j * LANES:(j + 1) * LANES]
        h_hi, h_lo = _split_bf16(_rms(x1, nf_ref[...]))
        logits.append(_dot_nt(wr_hi[...], h_hi) + _dot_nt(wr_hi[...], h_lo) + _dot_nt(wr_lo[...], h_hi))
    nrow = ROUTER_ROWS
    big = jnp.int32(LANES)
    ninf = -jnp.inf
    for r, lt in zip(halves, logits):
        lt = lt[:nrow] + br[:nrow, 0:1]
        row = lax.broadcasted_iota(jnp.int32, lt.shape, 0)

        def first_max(vals):
            mx = jnp.max(vals, axis=0, keepdims=True)
            return mx, jnp.min(jnp.where(vals == mx, row, big), axis=0, keepdims=True)

        gl = jnp.where((row >= N_EXPERTS) & (row < N_EXPERTS + N_GROUPS), lt, ninf)
        gmax, gidx = first_max(gl)
        p_sel = 1.0 / jnp.sum(jnp.exp(gl - gmax), axis=0, keepdims=True)
        lo = (gidx - N_EXPERTS) * EXPERTS_PER_GROUP
        el = jnp.where((row >= lo) & (row < lo + EXPERTS_PER_GROUP), lt, ninf)
        v1, i1 = first_max(el)
        el2 = jnp.where(row == i1, ninf, el)
        v2, i2 = first_max(el2)
        t = jnp.exp(v2 - v1)
        w1 = p_sel / (1.0 + t)
        w2 = p_sel * t / (1.0 + t)
        row8 = lax.broadcasted_iota(jnp.int32, (rt_o.shape[0], r.stop - r.start), 0)
        pick = lambda k, val, rest: jnp.where(row8 == k, val, rest)
        rt_o[:, r] = pick(0, i1.astype(F32), pick(1, i2.astype(F32), pick(2, w1, pick(3, w2, 0.0))))


def _post(x2d, a_out, g_out, ga, gb, w, tm):
    n = x2d.shape[0]
    row = lambda i: (i, 0)
    const = lambda i: (0, 0)
    wnames = ("wpa", "wpg", "wo", "nf", "wr_hi", "wr_lo", "br")
    rt_rows = 8
    return pl.pallas_call(
        _post_kernel,
        grid=(n // tm,),
        in_specs=[pl.BlockSpec((tm, D_MODEL), row), pl.BlockSpec((tm, ATTN_WIDTH), row),
                  pl.BlockSpec((tm, GLA_VALUE_WIDTH), row), pl.BlockSpec((tm, D_MODEL), row),
                  pl.BlockSpec((tm, D_MODEL), row)] + [pl.BlockSpec(w[k].shape, const) for k in wnames],
        out_specs=[pl.BlockSpec((tm * TOKEN_ROWS, LANES), row),
                   pl.BlockSpec((rt_rows, tm), lambda i: (0, i))],
        out_shape=[jax.ShapeDtypeStruct((n * TOKEN_ROWS, LANES), F32), jax.ShapeDtypeStruct((rt_rows, n), F32)],
        compiler_params=_cparams(("parallel",)),
        name="post_mixer",
    )(x2d, a_out, g_out, ga, gb, *[w[k] for k in wnames])


def _moe_plan(rt, tme):
    n = rt.shape[1]
    ntiles = n // tme
    max_items = ntiles + N_GROUPS - 1
    i1, i2 = rt[0].astype(jnp.int32), rt[1].astype(jnp.int32)
    grp = i1 // EXPERTS_PER_GROUP
    lo = jnp.minimum(i1, i2) % EXPERTS_PER_GROUP
    hi = jnp.maximum(i1, i2) % EXPERTS_PER_GROUP
    key = (grp * EXPERTS_PER_GROUP + lo) * EXPERTS_PER_GROUP + hi
    order = jnp.argsort(key, stable=True).astype(jnp.int32)
    skey = key[order].reshape(ntiles, tme)
    sg = skey // (EXPERTS_PER_GROUP * EXPERTS_PER_GROUP)
    slo = (skey // EXPERTS_PER_GROUP) % EXPERTS_PER_GROUP
    shi = skey % EXPERTS_PER_GROUP
    ev = jnp.arange(EXPERTS_PER_GROUP)
    in_g = sg[:, :, None] == jnp.arange(N_GROUPS)
    uses_e = (slo[:, :, None] == ev) | (shi[:, :, None] == ev)
    flags_tge = jnp.any(in_g[:, :, :, None] & uses_e[:, :, None, :], axis=1)
    present = jnp.any(in_g, axis=1).reshape(-1)
    pos = jnp.cumsum(present) - 1
    n_items = pos[-1] + 1
    src = jnp.zeros((max_items,), jnp.int32).at[jnp.where(present, pos, max_items)].set(
        jnp.arange(ntiles * N_GROUPS, dtype=jnp.int32), mode="drop")
    it = jnp.arange(max_items)
    valid = it < n_items
    last_src = src[n_items - 1]
    src = jnp.where(valid, src, last_src)
    item_tile = src // N_GROUPS
    item_group = src % N_GROUPS
    prev_tile = jnp.concatenate([jnp.full((1,), -1, jnp.int32), item_tile[:-1]])
    next_tile = jnp.concatenate([item_tile[1:], jnp.full((1,), -1, jnp.int32)])
    first = valid & (item_tile != prev_tile)
    last = valid & ((item_tile != next_tile) | (it == n_items - 1))
    flags = flags_tge[item_tile, item_group] & valid[:, None]
    e0 = jnp.argmax(flags, axis=1)
    rest = flags & (ev[None, :] != e0[:, None])
    e1 = jnp.argmax(rest, axis=1)
    rest = rest & (ev[None, :] != e1[:, None])
    has2 = jnp.any(rest, axis=1)
    e2 = jnp.argmax(rest, axis=1)
    rest = rest & (ev[None, :] != e2[:, None])
    rt_sorted = rt[:, order].reshape(rt.shape[0], ntiles, tme).transpose(1, 0, 2)
    i32 = lambda z: z.astype(jnp.int32)
    plan = (order, i32(item_tile), i32(item_group), i32(first), i32(last), i32(valid), i32(e0), i32(e1), i32(e2),
            i32(has2), i32(rest.reshape(-1)))
    return plan, rt_sorted


def _moe_kernel(order, itile, igroup, ifirst, ilast, ivalid, ie0, ie1, ie2, ihas2, flags, x_hbm, rt_ref, wg, wu, wd, nffn, nfin,
                y_hbm, xbuf, ybuf, acc, hbuf, rcol, gsem, ssem, *, tme, ntiles):
    i = pl.program_id(0)
    t = itile[i]
    slot = t % 2
    g = igroup[i]
    is_first = ifirst[i] == 1

    def gather_row(tile, sl, r):
        tok = order[tile * tme + r]
        src = x_hbm.at[pl.ds(pl.multiple_of(tok * TOKEN_ROWS, TOKEN_ROWS), TOKEN_ROWS)]
        dst = xbuf.at[pl.ds(pl.multiple_of((sl * tme + r) * TOKEN_ROWS, TOKEN_ROWS), TOKEN_ROWS)]
        return pltpu.make_async_copy(src, dst, gsem.at[sl])

    def scatter_row(tile, sl, r):
        tok = order[tile * tme + r]
        return pltpu.make_async_copy(ybuf.at[sl, pl.ds(r, 1)], y_hbm.at[pl.ds(tok, 1)], ssem.at[sl])

    def start_rows(make, tile, sl):
        def body(r, c):
            make(tile, sl, r).start()
            return c
        lax.fori_loop(0, tme, body, 0, unroll=8)

    def wait_gather(sl):
        rows = tme * TOKEN_ROWS
        pltpu.make_async_copy(x_hbm.at[pl.ds(0, rows)], xbuf.at[pl.ds(pl.multiple_of(sl * rows, rows), rows)],
                              gsem.at[sl]).wait()

    def wait_scatter(sl):
        pltpu.make_async_copy(ybuf.at[sl], y_hbm.at[pl.ds(0, tme)], ssem.at[sl]).wait()

    def start_rows_inline(make, tile, sl, r0=0, r1=tme):
        for r in range(r0, r1):
            make(tile, sl, r).start()

    def expert(e):
        eid = (g * EXPERTS_PER_GROUP + e).astype(F32)
        ce = (jnp.where(rcol[:, 0:1] == eid, rcol[:, 2:3], 0.0)
              + jnp.where(rcol[:, 1:2] == eid, rcol[:, 3:4], 0.0))
        parts = [slice(r0, r0 + tme // 2) for r0 in (0, tme // 2)] if tme % 16 == 0 else [slice(0, tme)]
        gate_up = [(_dot(hbuf[r, :], wg[e]), _dot(hbuf[r, :], wu[e])) for r in parts]
        acts = [((a * _sigmoid(a)) * u * ce[r, :]).astype(BF16) for r, (a, u) in zip(parts, gate_up)]
        for r, act in zip(parts, acts):
            acc[r, :] += _dot(act, wd[e])

    def expert_with(e, run, alternatives):
        plain = run
        for cond, side_work in alternatives:
            @pl.when(cond)
            def _():
                side_work()
                expert(e)

            plain = jnp.logical_and(plain, jnp.logical_not(cond))

        @pl.when(plain)
        def _():
            expert(e)

    @pl.when(i == 0)
    def _():
        start_rows(gather_row, 0, 0)

    @pl.when(is_first)
    def _():
        wait_gather(slot)
        base = pl.multiple_of(slot * (tme * TOKEN_ROWS), tme * TOKEN_ROWS)
        x1 = jnp.concatenate([xbuf[pl.ds(base + j, tme, stride=TOKEN_ROWS), :] for j in range(TOKEN_ROWS)], axis=1)
        acc[...] = x1
        hbuf[...] = _rms(x1, nffn[...]).astype(BF16)
        rt = rt_ref[0]
        rcol[...] = jnp.transpose(jnp.concatenate([rt, jnp.zeros((LANES - rt.shape[0], tme), F32)], axis=0))

    valid = ivalid[i] == 1
    has2 = ihas2[i] == 1
    do_gather = jnp.logical_and(is_first, t + 1 < ntiles)
    do_scatter = jnp.logical_and(is_first, t >= 1)
    half = tme // 2
    scatter_rows = lambda r0, r1: (lambda: start_rows_inline(scatter_row, t - 1, 1 - slot, r0, r1))
    expert_with(ie0[i], valid, [(do_gather, lambda: start_rows_inline(gather_row, t + 1, 1 - slot))])
    expert_with(ie1[i], valid, [(jnp.logical_and(do_scatter, has2), scatter_rows(0, half)),
                                (jnp.logical_and(do_scatter, jnp.logical_not(has2)), scatter_rows(0, tme))])
    expert_with(ie2[i], has2, [(jnp.logical_and(do_scatter, has2), scatter_rows(half, tme))])
    for e in range(EXPERTS_PER_GROUP):
        @pl.when(flags[i * EXPERTS_PER_GROUP + e] == 1)
        def _():
            expert(e)

    @pl.when(ilast[i] == 1)
    def _():
        @pl.when(t >= 2)
        def _():
            wait_scatter(slot)

        ybuf[slot] = _rms(acc[...], nfin[...])

        @pl.when(t == ntiles - 1)
        def _():
            start_rows(scatter_row, t, slot)

    @pl.when(i == pl.num_programs(0) - 1)
    def _():
        for sl in range(min(2, ntiles)):
            wait_scatter(sl)


def _moe(x1t, rt, weg, weu, wed, nffn, nfin):
    n = x1t.shape[0] // TOKEN_ROWS
    tme = min(MOE_TILE, n)
    assert n % tme == 0
    ntiles = n // tme
    plan, rt_sorted = _moe_plan(rt, tme)
    max_items = ntiles + N_GROUPS - 1
    grp = lambda i, order, itile, igroup, *_: (igroup[i], 0, 0)
    til = lambda i, order, itile, *_: (itile[i], 0, 0)
    const = lambda i, *_: (0, 0)
    grid_spec = pltpu.PrefetchScalarGridSpec(
        num_scalar_prefetch=len(plan),
        grid=(max_items,),
        in_specs=[pl.BlockSpec(memory_space=pl.ANY),
                  pl.BlockSpec((1,) + rt_sorted.shape[1:], til),
                  pl.BlockSpec((EXPERTS_PER_GROUP, D_MODEL, EXPERT_FF), grp),
                  pl.BlockSpec((EXPERTS_PER_GROUP, D_MODEL, EXPERT_FF), grp),
                  pl.BlockSpec((EXPERTS_PER_GROUP, EXPERT_FF, D_MODEL), grp),
                  pl.BlockSpec((1, D_MODEL), const), pl.BlockSpec((1, D_MODEL), const)],
        out_specs=pl.BlockSpec(memory_space=pl.ANY),
        scratch_shapes=[pltpu.VMEM((2 * tme * TOKEN_ROWS, LANES), F32), pltpu.VMEM((2, tme, D_MODEL), F32),
                        pltpu.VMEM((tme, D_MODEL), F32), pltpu.VMEM((tme, D_MODEL), BF16),
                        pltpu.VMEM((tme, LANES), F32),
                        pltpu.SemaphoreType.DMA((2,)), pltpu.SemaphoreType.DMA((2,))],
    )
    return pl.pallas_call(
        functools.partial(_moe_kernel, tme=tme, ntiles=ntiles),
        grid_spec=grid_spec,
        out_shape=jax.ShapeDtypeStruct((n, D_MODEL), F32),
        compiler_params=_cparams(("arbitrary",)),
        name="moe",
    )(*plan, x1t, rt_sorted, weg, weu, wed, nffn, nfin)


def _rope_tables(positions):
    half = HEAD_DIM // 2
    inv_freq = ROPE_THETA ** (-jnp.arange(half, dtype=F32) / half)
    ang = positions.astype(F32)[:, None] * inv_freq[None, :]
    cos, sin = jnp.cos(ang), jnp.sin(ang)
    reps = LANES // HEAD_DIM
    return (jnp.tile(jnp.concatenate([cos, cos], -1), (1, reps)),
            jnp.tile(jnp.concatenate([-sin, sin], -1), (1, reps)))


def _prep_weights(norm_mix, w_in, w_gla_f2, b_gla_f, gla_norm, w_proj_attn, w_proj_gla, w_out, norm_ffn,
                  w_router_group, b_router_group, w_router_expert, b_router_expert):
    w = {"win": w_in.astype(BF16)}
    w["wf2"] = jnp.pad(w_gla_f2.astype(BF16), ((0, LANES - GLA_GATE_RANK), (0, 0)))
    w["bf"] = b_gla_f.reshape(1, -1)
    w["norm_mix"] = norm_mix.reshape(1, -1)
    w["gla_norm"] = gla_norm.reshape(1, -1)
    w["wpa"] = w_proj_attn.astype(BF16)
    w["wpg"] = w_proj_gla.astype(BF16)
    w["wo"] = w_out.astype(BF16)
    w["nf"] = norm_ffn.reshape(1, -1)
    pad = LANES - N_EXPERTS - N_GROUPS
    wr_t = jnp.pad(jnp.concatenate([w_router_expert, w_router_group], axis=1), ((0, 0), (0, pad))).T
    w["wr_hi"], w["wr_lo"] = _split_bf16(wr_t)
    w["br"] = jnp.broadcast_to(jnp.pad(jnp.concatenate([b_router_expert, b_router_group]), (0, pad))[:, None],
                               (LANES, LANES))
    return w


def _layer(x, positions_tab, cache, w, sinks, weg, weu, wed, nfin, tm):
    batch, seq, _ = x.shape
    n = batch * seq
    x2d = x.reshape(n, D_MODEL)
    cos_t, sin_t = positions_tab
    qa, ka, va, qg, kg, vg, rg, la, ga, gb = _inproj(x2d, w["norm_mix"], cos_t, sin_t, w, tm)
    if cache is None:
        a_out = _swa_prompt(sinks, qa, ka, va, batch, seq)
        last = lambda z: z.reshape(batch, seq, KV_WIDTH)[:, seq - WINDOW:].reshape(batch, WINDOW, ATTN_KV_HEADS, HEAD_DIM)
        new_k, new_v = last(ka), last(va)
        g_out, new_s = _gla_prompt(qg, kg, vg, la, rg, w["gla_norm"], batch, seq)
    else:
        cache_k, cache_v, state = cache
        a_out, new_k, new_v = _swa_sample(sinks, qa, ka, va, cache_k.reshape(batch, WINDOW, KV_WIDTH),
                                          cache_v.reshape(batch, WINDOW, KV_WIDTH), batch, seq)
        new_k = new_k.reshape(batch, WINDOW, ATTN_KV_HEADS, HEAD_DIM)
        new_v = new_v.reshape(batch, WINDOW, ATTN_KV_HEADS, HEAD_DIM)
        g_out, new_s = _gla_sample(qg, kg, vg, la, rg, state, w["gla_norm"], batch, seq)
    x1t, rt = _post(x2d, a_out, g_out, ga, gb, w, tm * 2 if x2d.shape[0] % (tm * 2) == 0 else tm)
    y = _moe(x1t, rt, weg, weu, wed, w["nf"], nfin)
    return y.reshape(batch, seq, D_MODEL), new_k, new_v, new_s


def kernel(x_prompt, x_sample, cache_win_k, cache_win_v, state_gla, norm_mix, w_in, w_gla_f2, b_gla_f, gla_norm,
           attn_sinks, w_proj_attn, w_proj_gla, w_out, norm_ffn, w_router_group, b_router_group, w_router_expert,
           b_router_expert, w_exp_gate, w_exp_up, w_exp_down, norm_final):
    assert norm_mix.shape[0] == 1, "single-layer step"
    seq_p = x_prompt.shape[1]
    dec_b, dec_t = x_sample.shape[0], x_sample.shape[1]
    w = _prep_weights(norm_mix[0], w_in[0], w_gla_f2[0], b_gla_f[0], gla_norm[0], w_proj_attn[0], w_proj_gla[0],
                      w_out[0], norm_ffn[0], w_router_group[0], b_router_group[0], w_router_expert[0],
                      b_router_expert[0])
    weg = w_exp_gate[0].astype(BF16)
    weu = w_exp_up[0].astype(BF16)
    wed = w_exp_down[0].astype(BF16)
    nfin = norm_final.reshape(1, -1)
    sinks = attn_sinks[0]
    tab_p = _rope_tables(jnp.arange(seq_p, dtype=jnp.int32))
    pos_s = PAST_LEN + jnp.arange(dec_t, dtype=jnp.int32)
    tab_s = tuple(jnp.tile(t, (dec_b, 1)) for t in _rope_tables(pos_s))
    tm_p = min(512, seq_p)
    tm_s = dec_b * dec_t
    yp, pk, pv, ps = _layer(x_prompt, tab_p, None, w, sinks, weg, weu, wed, nfin, tm_p)
    ys, sk, sv, ss = _layer(x_sample, tab_s, (cache_win_k[0], cache_win_v[0], state_gla[0]), w, sinks, weg, weu, wed,
                            nfin, tm_s)
    return (yp, ys, pk[None], pv[None], ps[None], sk[None], sv[None], ss[None])
```

```python
import functools
import math

import numpy as np
import jax
import jax.numpy as jnp
from jax import lax
from jax.experimental import pallas as pl
from jax.experimental.pallas import tpu as pltpu

F32 = jnp.float32
BF16 = jnp.bfloat16

D_MODEL = 1024
ATTN_HEADS = 8
ATTN_KV_HEADS = 2
GROUP = ATTN_HEADS // ATTN_KV_HEADS
HEAD_DIM = 64
ATTN_WIDTH = ATTN_HEADS * HEAD_DIM
KV_WIDTH = ATTN_KV_HEADS * HEAD_DIM
WINDOW = 128
ROPE_THETA = 10000.0
PAST_LEN = 8192
GLA_HEADS = 4
GLA_KEY_WIDTH = D_MODEL // 2
GLA_VALUE_WIDTH = D_MODEL
GLA_DK = GLA_KEY_WIDTH // GLA_HEADS
GLA_DV = GLA_VALUE_WIDTH // GLA_HEADS
GLA_GATE_RANK = 16
GLA_GATE_NORMALIZER = 16.0
N_GROUPS = 4
EXPERTS_PER_GROUP = 8
N_EXPERTS = N_GROUPS * EXPERTS_PER_GROUP
EXPERT_FF = 256
EPS = 1e-6
LOG2_E = 1.4426950408889634

LANES = 128
GLA_CHUNK = 128
GLA_HEADS_PER_STEP = 4
SWA_BLOCKS_PER_STEP = 4
SAMPLE_SEQ_BLOCK = 16
POST_PART_ROWS = 512
MOE_TILE = 256
TOKEN_ROWS = D_MODEL // LANES
INPROJ_WIDTHS = (ATTN_WIDTH, 2 * KV_WIDTH, GLA_KEY_WIDTH, GLA_KEY_WIDTH, GLA_VALUE_WIDTH, GLA_VALUE_WIDTH)
ROUTER_ROWS = 40
VMEM_LIMIT = 56 * 1024 * 1024


def _cparams(sem):
    return pltpu.CompilerParams(dimension_semantics=sem, vmem_limit_bytes=VMEM_LIMIT)


def _rms(x, g):
    return x * lax.rsqrt(jnp.mean(x * x, axis=-1, keepdims=True) + EPS) * g


def _sigmoid(x):
    return 1.0 / (1.0 + jnp.exp(-x))


def _dot(a, b):
    return jnp.dot(a, b, preferred_element_type=F32)


def _dot_nt(a, b):
    return lax.dot_general(a, b, (((1,), (1,)), ((), ())), preferred_element_type=F32)


def _dot_tn(a, b):
    return lax.dot_general(a, b, (((0,), (0,)), ((), ())), preferred_element_type=F32)


def _split_bf16(x):
    hi = x.astype(BF16)
    lo = (x - hi.astype(F32)).astype(BF16)
    return hi, lo


def _inproj_kernel(x_ref, g_ref, cos_ref, sin_ref, win, wf2, bf,
                   qa_o, ka_o, va_o, qg_o, kg_o, vg_o, rg_o, la_o, ga_o, gb_o, wgate):
    f0 = sum(INPROJ_WIDTHS)

    @pl.when(pl.program_id(0) == 0)
    def _():
        tail = win[:, f0:]
        wgate[...] = tail[:, GLA_GATE_RANK:GLA_GATE_RANK + 2 * D_MODEL]

    tm = x_ref.shape[0]
    halves = [slice(0, tm // 2), slice(tm // 2, tm)] if tm % 16 == 0 else [slice(0, tm)]
    hbs = [_rms(x_ref[r, :], g_ref[...]).astype(BF16) for r in halves]
    cols = np.cumsum((0,) + INPROJ_WIDTHS[:-1])
    wqa, wkva, wqg, wkg, wvg, wrg = (win.at[:, int(c):int(c) + wd] for c, wd in zip(cols, INPROJ_WIDTHS))
    wga = wgate.at[:, :D_MODEL]
    wgb = wgate.at[:, D_MODEL:]
    wf = win.at[:, f0:f0 + LANES]
    lane = lax.broadcasted_iota(jnp.int32, (halves[0].stop - halves[0].start, LANES), 1)
    first_half = (lane % HEAD_DIM) < (HEAD_DIM // 2)

    def rope(t, r):
        swapped = jnp.where(first_half, pltpu.roll(t, LANES - HEAD_DIM // 2, 1), pltpu.roll(t, HEAD_DIM // 2, 1))
        return t * cos_ref[r, :] + swapped * sin_ref[r, :]

    for r, hb in zip(halves, hbs):
        ga_o[r, :] = _sigmoid(_dot(hb, wga[...])).astype(BF16)
    for r, hb in zip(halves, hbs):
        gb_o[r, :] = _sigmoid(_dot(hb, wgb[...])).astype(BF16)
    for r, hb in zip(halves, hbs):
        z = _dot(_dot(hb, wf[...]).astype(BF16), wf2[...]) + bf[...]
        la_o[r, :] = (jnp.minimum(z, 0.0) - jnp.log1p(jnp.exp(-jnp.abs(z)))) * (1.0 / GLA_GATE_NORMALIZER)
    for r, hb in zip(halves, hbs):
        qa = _dot(hb, wqa[...])
        for c in range(ATTN_WIDTH // LANES):
            sl = slice(c * LANES, (c + 1) * LANES)
            qa_o[r, sl] = (rope(qa[:, sl], r) * (HEAD_DIM ** -0.5)).astype(BF16)
    for r, hb in zip(halves, hbs):
        kva = _dot(hb, wkva[...])
        ka_o[r, :] = rope(kva[:, :KV_WIDTH], r)
        va_o[r, :] = kva[:, KV_WIDTH:]
    for r, hb in zip(halves, hbs):
        qg_o[r, :] = (_dot(hb, wqg[...]) * (GLA_DK ** -0.5)).astype(BF16)
    for r, hb in zip(halves, hbs):
        kg_o[r, :] = _dot(hb, wkg[...]).astype(BF16)
    for r, hb in zip(halves, hbs):
        vg_o[r, :] = _dot(hb, wvg[...]).astype(BF16)
    for r, hb in zip(halves, hbs):
        rg_o[r, :] = _dot(hb, wrg[...]).astype(BF16)


def _inproj(x2d, norm_g, cos_t, sin_t, w, tm):
    n = x2d.shape[0]
    ntab = cos_t.shape[0] // tm
    row = lambda i: (i, 0)
    const = lambda i: (0, 0)
    tab = lambda i: (i % ntab, 0)
    wnames = ("win", "wf2", "bf")
    wspecs = [pl.BlockSpec(w[k].shape, const) for k in wnames]
    widths = (ATTN_WIDTH, KV_WIDTH, KV_WIDTH, GLA_KEY_WIDTH, GLA_KEY_WIDTH, GLA_VALUE_WIDTH, GLA_VALUE_WIDTH,
              GLA_KEY_WIDTH, D_MODEL, D_MODEL)
    dtypes = (BF16, F32, F32, BF16, BF16, BF16, BF16, F32, BF16, BF16)
    return pl.pallas_call(
        _inproj_kernel,
        grid=(n // tm,),
        in_specs=[pl.BlockSpec((tm, D_MODEL), row), pl.BlockSpec((1, D_MODEL), const),
                  pl.BlockSpec((tm, LANES), tab), pl.BlockSpec((tm, LANES), tab)] + wspecs,
        out_specs=[pl.BlockSpec((tm, wd), row) for wd in widths],
        out_shape=[jax.ShapeDtypeStruct((n, wd), dt) for wd, dt in zip(widths, dtypes)],
        scratch_shapes=[pltpu.VMEM((D_MODEL, 2 * D_MODEL), BF16)],
        compiler_params=_cparams(("arbitrary",)),
        name="inproj",
    )(x2d, norm_g, cos_t, sin_t, *[w[k] for k in wnames])


def _swa_prompt_kernel(sink_ref, q_ref, kc_ref, kp_ref, vc_ref, vp_ref, o_ref, *, qb):
    n = pl.program_id(1)
    k3 = jnp.concatenate([kp_ref[...], kc_ref[...]], axis=0).astype(BF16)
    v3 = jnp.concatenate([vp_ref[...], vc_ref[...]], axis=0).astype(BF16)
    t = lax.broadcasted_iota(jnp.int32, (WINDOW, 2 * WINDOW), 0)
    j = lax.broadcasted_iota(jnp.int32, (WINDOW, 2 * WINDOW), 1)
    band = (j >= t) & (j <= t + WINDOW)
    zeros = jnp.zeros((2 * WINDOW, HEAD_DIM), BF16)
    ones = jnp.ones((2 * WINDOW, HEAD_DIM), BF16)
    lane = lax.broadcasted_iota(jnp.int32, (WINDOW, LANES), 1)
    chains = [(blk, h) for blk in range(qb) for h in range(ATTN_HEADS)]
    scores = []
    for blk, h in chains:
        kv = h // GROUP
        rows = slice(blk * WINDOW, (blk + 1) * WINDOW)
        keys = slice(blk * WINDOW, (blk + 2) * WINDOW)
        s = _dot_nt(q_ref[rows, h * HEAD_DIM:(h + 1) * HEAD_DIM], k3[keys, kv * HEAD_DIM:(kv + 1) * HEAD_DIM])
        valid = band & ((j >= WINDOW) | (n > 0)) if blk == 0 else band
        scores.append(jnp.where(valid, s, -jnp.inf))
    probs, sink_terms = [], []
    for (blk, h), s in zip(chains, scores):
        m = jnp.maximum(jnp.max(s, axis=-1, keepdims=True), sink_ref[h])
        probs.append(jnp.exp(s - m).astype(BF16))
        sink_terms.append(jnp.exp(sink_ref[h] - m))
    for blk in range(qb):
        rows = slice(blk * WINDOW, (blk + 1) * WINDOW)
        keys = slice(blk * WINDOW, (blk + 2) * WINDOW)
        for kv in range(ATTN_KV_HEADS):
            vv = v3[keys, kv * HEAD_DIM:(kv + 1) * HEAD_DIM]
            vext = (jnp.concatenate([vv, zeros, ones, zeros], axis=1), jnp.concatenate([zeros, vv, zeros, ones], axis=1))
            for pr in range(GROUP // 2):
                h0 = kv * GROUP + pr * 2
                c0 = blk * ATTN_HEADS + h0
                acc = _dot(probs[c0], vext[0]) + _dot(probs[c0 + 1], vext[1])
                l = acc[:, LANES:] + jnp.where(lane < HEAD_DIM, sink_terms[c0], sink_terms[c0 + 1])
                o_ref[rows, h0 * HEAD_DIM:h0 * HEAD_DIM + LANES] = (acc[:, :LANES] / l).astype(BF16)


def _swa_prompt(sinks, qa, ka, va, batch, seq):
    nb = seq // WINDOW
    qb = math.gcd(SWA_BLOCKS_PER_STEP, nb)
    steps = nb // qb
    cur = lambda b, n: (b * steps + n, 0)
    prev = lambda b, n: (b * nb + jnp.maximum(n * qb - 1, 0), 0)
    return pl.pallas_call(
        functools.partial(_swa_prompt_kernel, qb=qb),
        grid=(batch, steps),
        in_specs=[pl.BlockSpec(memory_space=pltpu.SMEM),
                  pl.BlockSpec((qb * WINDOW, ATTN_WIDTH), cur),
                  pl.BlockSpec((qb * WINDOW, KV_WIDTH), cur), pl.BlockSpec((WINDOW, KV_WIDTH), prev),
                  pl.BlockSpec((qb * WINDOW, KV_WIDTH), cur), pl.BlockSpec((WINDOW, KV_WIDTH), prev)],
        out_specs=pl.BlockSpec((qb * WINDOW, ATTN_WIDTH), cur),
        out_shape=jax.ShapeDtypeStruct(qa.shape, BF16),
        compiler_params=_cparams(("parallel", "parallel")),
        name="swa_prompt",
    )(sinks, qa, ka, ka, va, va)


def _swa_sample_kernel(sink_ref, q_ref, kn_ref, vn_ref, kn3_ref, vn3_ref, ck_ref, cv_ref, o_ref, nk_ref, nv_ref,
                       *, t_new):
    sb = ck_ref.shape[0]
    spv = 8 // t_new
    nq = GROUP * 8
    nc = spv * WINDOW
    qi = lax.broadcasted_iota(jnp.int32, (nq, nc), 0) % 8
    ci = lax.broadcasted_iota(jnp.int32, (nq, nc), 1)
    valid_c = (qi // t_new == ci // WINDOW) & (ci % WINDOW >= qi % t_new)
    qn = lax.broadcasted_iota(jnp.int32, (nq, 8), 0) % 8
    cn = lax.broadcasted_iota(jnp.int32, (nq, 8), 1)
    valid_n = (qn // t_new == cn // t_new) & (cn <= qn)
    grow = lax.broadcasted_iota(jnp.int32, (nq, 1), 0) // 8
    chains = [(vr, kv) for vr in range(sb // spv) for kv in range(ATTN_KV_HEADS)]
    scored = []
    for vr, kv in chains:
        r8 = slice(8 * vr, 8 * vr + 8)
        cs = slice(kv * HEAD_DIM, (kv + 1) * HEAD_DIM)
        heads = [kv * GROUP + g for g in range(GROUP)]
        qs = jnp.concatenate([q_ref[r8, h * HEAD_DIM:(h + 1) * HEAD_DIM] for h in heads], axis=0)
        ck = ck_ref[vr * spv:(vr + 1) * spv, :, cs].reshape(nc, HEAD_DIM).astype(BF16)
        kn = kn_ref[r8, cs].astype(BF16)
        scored.append((jnp.where(valid_c, _dot_nt(qs, ck), -jnp.inf), jnp.where(valid_n, _dot_nt(qs, kn), -jnp.inf)))
    soft = []
    for (vr, kv), (s_c, s_n) in zip(chains, scored):
        sink = sink_ref[kv * GROUP]
        for g in range(1, GROUP):
            sink = jnp.where(grow == g, sink_ref[kv * GROUP + g], sink)
        m = jnp.maximum(jnp.maximum(jnp.max(s_c, axis=-1, keepdims=True), jnp.max(s_n, axis=-1, keepdims=True)), sink)
        p_c = jnp.exp(s_c - m)
        p_n = jnp.exp(s_n - m)
        l = jnp.sum(p_c, axis=-1, keepdims=True) + jnp.sum(p_n, axis=-1, keepdims=True) + jnp.exp(sink - m)
        soft.append((p_c.astype(BF16), p_n.astype(BF16), l))
    for (vr, kv), (p_c, p_n, l) in zip(chains, soft):
        r8 = slice(8 * vr, 8 * vr + 8)
        cs = slice(kv * HEAD_DIM, (kv + 1) * HEAD_DIM)
        cv = cv_ref[vr * spv:(vr + 1) * spv, :, cs].reshape(nc, HEAD_DIM).astype(BF16)
        o = (_dot(p_c, cv) + _dot(p_n, vn_ref[r8, cs].astype(BF16))) / l
        for a in range(GROUP // 2):
            pair = jnp.concatenate([o[16 * a:16 * a + 8], o[16 * a + 8:16 * a + 16]], axis=1)
            c0 = (kv * GROUP + 2 * a) * HEAD_DIM
            o_ref[r8, c0:c0 + LANES] = pair.astype(BF16)
    nk_ref[:, 0:WINDOW - t_new, :] = ck_ref[:, t_new:WINDOW, :]
    nk_ref[:, WINDOW - t_new:WINDOW, :] = kn3_ref[...]
    nv_ref[:, 0:WINDOW - t_new, :] = cv_ref[:, t_new:WINDOW, :]
    nv_ref[:, WINDOW - t_new:WINDOW, :] = vn3_ref[...]


def _swa_sample(sinks, qa, ka, va, cache_k, cache_v, batch, t_new):
    sb = SAMPLE_SEQ_BLOCK
    rows = sb * t_new
    r2 = lambda i: (i, 0)
    r3 = lambda i: (i, 0, 0)
    ka3 = ka.reshape(batch, t_new, KV_WIDTH)
    va3 = va.reshape(batch, t_new, KV_WIDTH)
    return pl.pallas_call(
        functools.partial(_swa_sample_kernel, t_new=t_new),
        grid=(batch // sb,),
        in_specs=[pl.BlockSpec(memory_space=pltpu.SMEM),
                  pl.BlockSpec((rows, ATTN_WIDTH), r2),
                  pl.BlockSpec((rows, KV_WIDTH), r2), pl.BlockSpec((rows, KV_WIDTH), r2),
                  pl.BlockSpec((sb, t_new, KV_WIDTH), r3), pl.BlockSpec((sb, t_new, KV_WIDTH), r3),
                  pl.BlockSpec((sb, WINDOW, KV_WIDTH), r3), pl.BlockSpec((sb, WINDOW, KV_WIDTH), r3)],
        out_specs=[pl.BlockSpec((rows, ATTN_WIDTH), r2),
                   pl.BlockSpec((sb, WINDOW, KV_WIDTH), r3), pl.BlockSpec((sb, WINDOW, KV_WIDTH), r3)],
        out_shape=[jax.ShapeDtypeStruct(qa.shape, BF16),
                   jax.ShapeDtypeStruct(cache_k.shape, F32), jax.ShapeDtypeStruct(cache_v.shape, F32)],
        compiler_params=_cparams(("parallel",)),
        name="swa_sample",
    )(sinks, qa, ka, va, ka3, va3, cache_k, cache_v)


def _gla_constants(c, seg, with_rem):
    t = np.arange(c)
    sid = t // seg
    same = sid[:, None] == sid[None, :]
    levels = []
    m = seg // 2
    while m >= 1:
        levels.append(m)
        m //= 2
    mats, roles, masks = [], [], []
    for m in levels:
        blk = t // (2 * m)
        second = (t // m) % 2 == 1
        p = blk * 2 * m + m - 1
        u = t[None, :]
        mq = (u > p[:, None]) & (u <= t[:, None])
        mk = (u > t[:, None]) & (u <= p[:, None])
        if m > 1:
            mats.append(np.where(second[:, None], mq, mk))
        roles.append(np.broadcast_to(second[:, None], (c, LANES)))
        masks.append((blk[:, None] == blk[None, :]) & second[:, None] & ~second[None, :])
    masks.append(np.eye(c, dtype=bool))
    mats.append(same & (t[None, :] <= t[:, None]))
    if with_rem:
        mats.append(same & (t[None, :] > t[:, None]))
    mall = np.concatenate(mats, 0).astype(np.float32)
    mall = jnp.asarray(np.concatenate([mall, mall], 1), BF16)
    role = jnp.asarray(np.concatenate(roles, 0).astype(np.float32))
    mask = jnp.asarray(np.concatenate(masks, 0).astype(np.float32))
    return len(levels), mall, role, mask


def _gla_exponents(la, mall):
    la2 = la * LOG2_E
    hl = jnp.concatenate(_split_bf16(la2), axis=0)
    return _dot(mall, hl), hl, la2


def _gla_scores(qb, kb, e2, la2, role_ref, mask_ref, nlev, c):
    qf = qb.astype(F32)
    kf = kb.astype(F32)
    terms = []
    for lv in range(nlev):
        sl = slice(lv * c, (lv + 1) * c)
        m = 1 << (nlev - 1 - lv)
        if m % 8 == 0:
            pe = jnp.exp2(e2[sl])
            x = jnp.concatenate([(qf if blk % 2 else kf)[blk * m:(blk + 1) * m] * pe[blk * m:(blk + 1) * m]
                                 for blk in range(c // m)], axis=0).astype(BF16)
        else:
            second = role_ref[sl, :] > 0.5
            e = e2[sl] if lv < nlev - 1 else jnp.where(second, la2, 0.0)
            x = (jnp.where(second, qf, kf) * jnp.exp2(e)).astype(BF16)
        terms.append((x, x, lv))
    terms.append((qb, kb, nlev))
    mask = lambda i: mask_ref[i * c:(i + 1) * c, :]
    a = None
    pair = c % LANES == 0
    while terms:
        if pair and len(terms) >= 2:
            (l0, r0, i0), (l1, r1, i1) = terms.pop(), terms.pop()
            z = jnp.zeros_like(r0)
            rhs = jnp.concatenate([jnp.concatenate([r0, z], axis=1), jnp.concatenate([z, r1], axis=1)], axis=0)
            g = _dot_nt(jnp.concatenate([l0, l1], axis=1), rhs)
            t = mask(i0) * g[:, :c] + mask(i1) * g[:, c:]
        else:
            l0, r0, i0 = terms.pop()
            t = mask(i0) * _dot_nt(l0, r0)
        a = t if a is None else a + t
    return a, qf, kf


def _gla_out(o, g, r):
    r = r.astype(F32)
    return (_rms(o, g) * (r * _sigmoid(r))).astype(BF16)


def _gla_prompt_kernel(q_ref, k_ref, v_ref, la_ref, r_ref, mall_ref, role_ref, mask_ref, g_ref, o_ref, s_ref,
                       s_scr, *, nlev, nchunks):
    c = GLA_CHUNK
    hp = GLA_HEADS_PER_STEP
    s_scr[...] = jnp.zeros_like(s_scr)

    def chunk(i, carry):
        rows = pl.ds(pl.multiple_of(i * c, c), c)
        e2_all, _, la2_all = _gla_exponents(la_ref[rows, :], mall_ref[...])
        ksl = [slice(h * GLA_DK, (h + 1) * GLA_DK) for h in range(hp)]
        vsl = [slice(h * GLA_DV, (h + 1) * GLA_DV) for h in range(hp)]
        scores = [_gla_scores(q_ref[rows, ksl[h]], k_ref[rows, ksl[h]], e2_all[:, ksl[h]], la2_all[:, ksl[h]], role_ref,
                              mask_ref, nlev, c)[0] for h in range(hp)]
        for h in range(hp):
            b = e2_all[(nlev - 1) * c:nlev * c, ksl[h]]
            qf = q_ref[rows, ksl[h]].astype(F32)
            o = (_dot(scores[h].astype(BF16), v_ref[rows, vsl[h]])
                 + _dot((qf * jnp.exp2(b)).astype(BF16), s_scr[h].astype(BF16)))
            o_ref[rows, vsl[h]] = _gla_out(o, g_ref[...], r_ref[rows, vsl[h]])
        for h in range(hp):
            b = e2_all[(nlev - 1) * c:nlev * c, ksl[h]]
            kf = k_ref[rows, ksl[h]].astype(F32)
            kt = (kf * jnp.exp2(b[c - 1:c, :] - b)).astype(BF16)
            dec = jnp.exp2(jnp.transpose(b[c - 8:c, :]))[:, 7:8]
            s_scr[h] = dec * s_scr[h] + _dot_tn(kt, v_ref[rows, vsl[h]])
        return carry

    lax.fori_loop(0, nchunks, chunk, 0)
    s_ref[0] = s_scr[...]


def _gla_prompt(qg, kg, vg, la, rg, gnorm, batch, seq):
    nlev, mall, role, mask = _gla_constants(GLA_CHUNK, GLA_CHUNK, with_rem=False)
    hp = GLA_HEADS_PER_STEP
    bh = lambda b, h: (b, h)
    const = lambda b, h: (0, 0)
    return pl.pallas_call(
        functools.partial(_gla_prompt_kernel, nlev=nlev, nchunks=seq // GLA_CHUNK),
        grid=(batch, GLA_HEADS // hp),
        in_specs=[pl.BlockSpec((seq, hp * GLA_DK), bh), pl.BlockSpec((seq, hp * GLA_DK), bh),
                  pl.BlockSpec((seq, hp * GLA_DV), bh), pl.BlockSpec((seq, hp * GLA_DK), bh),
                  pl.BlockSpec((seq, hp * GLA_DV), bh),
                  pl.BlockSpec(mall.shape, const), pl.BlockSpec(role.shape, const), pl.BlockSpec(mask.shape, const),
                  pl.BlockSpec((1, GLA_DV), const)],
        out_specs=[pl.BlockSpec((seq, hp * GLA_DV), bh),
                   pl.BlockSpec((1, hp, GLA_DK, GLA_DV), lambda b, h: (b, h, 0, 0))],
        out_shape=[jax.ShapeDtypeStruct(vg.shape, BF16),
                   jax.ShapeDtypeStruct((batch, GLA_HEADS, GLA_DK, GLA_DV), F32)],
        scratch_shapes=[pltpu.VMEM((hp, GLA_DK, GLA_DV), F32)],
        compiler_params=_cparams(("parallel", "parallel")),
        name="gla_prompt",
    )(qg, kg, vg, la, rg, mall, role, mask, gnorm)


def _gla_sample_kernel(q_ref, k_ref, v_ref, la_ref, r_ref, s0_ref, mall_ref, role_ref, mask_ref, msum_ref, g_ref,
                       o_ref, s_ref, *, nlev, t_new):
    sb = s0_ref.shape[0]
    c = sb * t_new
    spv = 8 // t_new
    e2_all, hl_all, la2_all = _gla_exponents(la_ref[...], mall_ref[...])
    seq_in_tile = lax.broadcasted_iota(jnp.int32, (8, GLA_DV), 0) // t_new
    seq_of_row = lax.broadcasted_iota(jnp.int32, (c, GLA_DV), 0) // t_new
    ksl = [slice(h * GLA_DK, (h + 1) * GLA_DK) for h in range(GLA_HEADS)]
    scores = [_gla_scores(q_ref[:, ksl[h]], k_ref[:, ksl[h]], e2_all[:, ksl[h]], la2_all[:, ksl[h]], role_ref, mask_ref,
                          nlev, c) for h in range(GLA_HEADS)]
    for h in range(GLA_HEADS):
        ks = ksl[h]
        vs = slice(h * GLA_DV, (h + 1) * GLA_DV)
        v = v_ref[:, vs]
        e2 = e2_all[:, ks]
        hl = hl_all[:, ks]
        a, qf, kf = scores[h]
        qe = (qf * jnp.exp2(e2[(nlev - 1) * c:nlev * c])).astype(BF16)
        kt_t = jnp.transpose(kf * jnp.exp2(e2[nlev * c:(nlev + 1) * c])).astype(BF16)
        bl_t = jnp.transpose(_dot(msum_ref[...], hl[:c]) + _dot(msum_ref[...], hl[c:]))
        dec_t = jnp.exp2(bl_t)
        inter = []
        for vr in range(c // 8):
            rows8 = qe[8 * vr:8 * vr + 8]
            tile = None
            for u in range(spv):
                j = vr * spv + u
                s0 = s0_ref[j, h]
                r = _dot(rows8, s0.astype(BF16))
                tile = r if tile is None else jnp.where(seq_in_tile == u, r, tile)
                vj = jnp.where(seq_of_row == j, v, jnp.zeros_like(v))
                s_ref[j, h] = dec_t[:, j:j + 1] * s0 + _dot(kt_t, vj)
            inter.append(tile)
        o = _dot(a.astype(BF16), v) + jnp.concatenate(inter, axis=0)
        o_ref[:, vs] = _gla_out(o, g_ref[...], r_ref[:, vs])


def _gla_sample(qg, kg, vg, la, rg, state, gnorm, batch, t_new):
    sb = SAMPLE_SEQ_BLOCK
    c = sb * t_new
    assert 8 % t_new == 0 and c % 8 == 0
    nlev, mall, role, mask = _gla_constants(c, t_new, with_rem=True)
    msum = jnp.asarray((np.arange(c)[None, :] // t_new == np.arange(sb)[:, None]).astype(np.float32), BF16)
    rows = lambda i: (i, 0)
    const = lambda i: (0, 0)
    st = lambda i: (i, 0, 0, 0)
    return pl.pallas_call(
        functools.partial(_gla_sample_kernel, nlev=nlev, t_new=t_new),
        grid=(batch // sb,),
        in_specs=[pl.BlockSpec((c, GLA_KEY_WIDTH), rows), pl.BlockSpec((c, GLA_KEY_WIDTH), rows),
                  pl.BlockSpec((c, GLA_VALUE_WIDTH), rows), pl.BlockSpec((c, GLA_KEY_WIDTH), rows),
                  pl.BlockSpec((c, GLA_VALUE_WIDTH), rows),
                  pl.BlockSpec((sb, GLA_HEADS, GLA_DK, GLA_DV), st),
                  pl.BlockSpec(mall.shape, const), pl.BlockSpec(role.shape, const), pl.BlockSpec(mask.shape, const),
                  pl.BlockSpec(msum.shape, const), pl.BlockSpec((1, GLA_DV), const)],
        out_specs=[pl.BlockSpec((c, GLA_VALUE_WIDTH), rows), pl.BlockSpec((sb, GLA_HEADS, GLA_DK, GLA_DV), st)],
        out_shape=[jax.ShapeDtypeStruct(vg.shape, BF16), jax.ShapeDtypeStruct(state.shape, F32)],
        compiler_params=_cparams(("parallel",)),
        name="gla_sample",
    )(qg, kg, vg, la, rg, state, mall, role, mask, msum, gnorm)


def _post_kernel(x_ref, a_ref, gl_ref, ga_ref, gb_ref, wpa, wpg, wo, nf_ref, wr_hi, wr_lo, br, x1t_o, rt_o):
    tm = x_ref.shape[0]
    part = POST_PART_ROWS if tm % POST_PART_ROWS == 0 else tm
    halves = [slice(r0, r0 + part) for r0 in range(0, tm, part)]
    proj =[(_dot(a_ref[r, :], wpa[...]), _dot(gl_ref[r, :], wpg[...])) for r in halves]
    x1s = []
    for r, (pa, pg) in zip(halves, proj):
        merged = ga_ref[r, :].astype(F32) * pa + gb_ref[r, :].astype(F32) * pg
        x1s.append(x_ref[r, :] + _dot(merged.astype(BF16), wo[...]))
    logits = []
    for r, x1 in zip(halves, x1s):
        rows = r.stop - r.start
        for j in range(TOKEN_ROWS):
            x1t_o[pl.ds(r.start * TOKEN_ROWS + j, rows, stride=TOKEN_ROWS), :] = x1[:, j * LANES:(j + 1) * LANES]
        h_hi, h_lo = _split_bf16(_rms(x1, nf_ref[...]))
        logits.append(_dot_nt(wr_hi[...], h_hi) + _dot_nt(wr_hi[...], h_lo) + _dot_nt(wr_lo[...], h_hi))
    nrow = ROUTER_ROWS
    big = jnp.int32(LANES)
    ninf = -jnp.inf
    for r, lt in zip(halves, logits):
        lt = lt[:nrow] + br[:nrow, 0:1]
        row = lax.broadcasted_iota(jnp.int32, lt.shape, 0)

        def first_max(vals):
            mx = jnp.max(vals, axis=0, keepdims=True)
            return mx, jnp.min(jnp.where(vals == mx, row, big), axis=0, keepdims=True)

        gl = jnp.where((row >= N_EXPERTS) & (row < N_EXPERTS + N_GROUPS), lt, ninf)
        gmax, gidx = first_max(gl)
        p_sel = 1.0 / jnp.sum(jnp.exp(gl - gmax), axis=0, keepdims=True)
        lo = (gidx - N_EXPERTS) * EXPERTS_PER_GROUP
        el = jnp.where((row >= lo) & (row < lo + EXPERTS_PER_GROUP), lt, ninf)
        v1, i1 = first_max(el)
        el2 = jnp.where(row == i1, ninf, el)
        v2, i2 = first_max(el2)
        t = jnp.exp(v2 - v1)
        w1 = p_sel / (1.0 + t)
        w2 = p_sel * t / (1.0 + t)
        row8 = lax.broadcasted_iota(jnp.int32, (rt_o.shape[0], r.stop - r.start), 0)
        pick = lambda k, val, rest: jnp.where(row8 == k, val, rest)
        rt_o[:, r] = pick(0, i1.astype(F32), pick(1, i2.astype(F32), pick(2, w1, pick(3, w2, 0.0))))


def _post(x2d, a_out, g_out, ga, gb, w, tm):
    n = x2d.shape[0]
    row = lambda i: (i, 0)
    const = lambda i: (0, 0)
    wnames = ("wpa", "wpg", "wo", "nf", "wr_hi", "wr_lo", "br")
    rt_rows = 8
    return pl.pallas_call(
        _post_kernel,
        grid=(n // tm,),
        in_specs=[pl.BlockSpec((tm, D_MODEL), row), pl.BlockSpec((tm, ATTN_WIDTH), row),
                  pl.BlockSpec((tm, GLA_VALUE_WIDTH), row), pl.BlockSpec((tm, D_MODEL), row),
                  pl.BlockSpec((tm, D_MODEL), row)] + [pl.BlockSpec(w[k].shape, const) for k in wnames],
        out_specs=[pl.BlockSpec((tm * TOKEN_ROWS, LANES), row),
                   pl.BlockSpec((rt_rows, tm), lambda i: (0, i))],
        out_shape=[jax.ShapeDtypeStruct((n * TOKEN_ROWS, LANES), F32), jax.ShapeDtypeStruct((rt_rows, n), F32)],
        compiler_params=_cparams(("parallel",)),
        name="post_mixer",
    )(x2d, a_out, g_out, ga, gb, *[w[k] for k in wnames])


def _moe_plan(rt, tme):
    n = rt.shape[1]
    ntiles = n // tme
    max_items = ntiles + N_GROUPS - 1
    i1, i2 = rt[0].astype(jnp.int32), rt[1].astype(jnp.int32)
    grp = i1 // EXPERTS_PER_GROUP
    lo = jnp.minimum(i1, i2) % EXPERTS_PER_GROUP
    hi = jnp.maximum(i1, i2) % EXPERTS_PER_GROUP
    key = (grp * EXPERTS_PER_GROUP + lo) * EXPERTS_PER_GROUP + hi
    order = jnp.argsort(key, stable=True).astype(jnp.int32)
    skey = key[order].reshape(ntiles, tme)
    sg = skey // (EXPERTS_PER_GROUP * EXPERTS_PER_GROUP)
    slo = (skey // EXPERTS_PER_GROUP) % EXPERTS_PER_GROUP
    shi = skey % EXPERTS_PER_GROUP
    ev = jnp.arange(EXPERTS_PER_GROUP)
    in_g = sg[:, :, None] == jnp.arange(N_GROUPS)
    uses_e = (slo[:, :, None] == ev) | (shi[:, :, None] == ev)
    flags_tge = jnp.any(in_g[:, :, :, None] & uses_e[:, :, None, :], axis=1)
    present = jnp.any(in_g, axis=1).reshape(-1)
    pos = jnp.cumsum(present) - 1
    n_items = pos[-1] + 1
    src = jnp.zeros((max_items,), jnp.int32).at[jnp.where(present, pos, max_items)].set(
        jnp.arange(ntiles * N_GROUPS, dtype=jnp.int32), mode="drop")
    it = jnp.arange(max_items)
    valid = it < n_items
    last_src = src[n_items - 1]
    src = jnp.where(valid, src, last_src)
    item_tile = src // N_GROUPS
    item_group = src % N_GROUPS
    prev_tile = jnp.concatenate([jnp.full((1,), -1, jnp.int32), item_tile[:-1]])
    next_tile = jnp.concatenate([item_tile[1:], jnp.full((1,), -1, jnp.int32)])
    first = valid & (item_tile != prev_tile)
    last = valid & ((item_tile != next_tile) | (it == n_items - 1))
    flags = flags_tge[item_tile, item_group] & valid[:, None]
    e0 = jnp.argmax(flags, axis=1)
    rest = flags & (ev[None, :] != e0[:, None])
    e1 = jnp.argmax(rest, axis=1)
    rest = rest & (ev[None, :] != e1[:, None])
    has2 = jnp.any(rest, axis=1)
    e2 = jnp.argmax(rest, axis=1)
    rest = rest & (ev[None, :] != e2[:, None])
    rt_sorted = rt[:, order].reshape(rt.shape[0], ntiles, tme).transpose(1, 0, 2)
    i32 = lambda z: z.astype(jnp.int32)
    plan = (order, i32(item_tile), i32(item_group), i32(first), i32(last), i32(valid), i32(e0), i32(e1), i32(e2),
            i32(has2), i32(rest.reshape(-1)))
    return plan, rt_sorted


def _moe_kernel(order, itile, igroup, ifirst, ilast, ivalid, ie0, ie1, ie2, ihas2, flags, x_hbm, rt_ref, wg, wu, wd, nffn, nfin,
                y_hbm, xbuf, ybuf, acc, hbuf, rcol, gsem, ssem, *, tme, ntiles):
    i = pl.program_id(0)
    t = itile[i]
    slot = t % 2
    g = igroup[i]
    is_first = ifirst[i] == 1

    def gather_row(tile, sl, r):
        tok = order[tile * tme + r]
        src = x_hbm.at[pl.ds(pl.multiple_of(tok * TOKEN_ROWS, TOKEN_ROWS), TOKEN_ROWS)]
        dst = xbuf.at[pl.ds(pl.multiple_of((sl * tme + r) * TOKEN_ROWS, TOKEN_ROWS), TOKEN_ROWS)]
        return pltpu.make_async_copy(src, dst, gsem.at[sl])

    def scatter_row(tile, sl, r):
        tok = order[tile * tme + r]
        return pltpu.make_async_copy(ybuf.at[sl, pl.ds(r, 1)], y_hbm.at[pl.ds(tok, 1)], ssem.at[sl])

    def start_rows(make, tile, sl):
        def body(r, c):
            make(tile, sl, r).start()
            return c
        lax.fori_loop(0, tme, body, 0, unroll=8)

    def wait_gather(sl):
        rows = tme * TOKEN_ROWS
        pltpu.make_async_copy(x_hbm.at[pl.ds(0, rows)], xbuf.at[pl.ds(pl.multiple_of(sl * rows, rows), rows)],
                              gsem.at[sl]).wait()

    def wait_scatter(sl):
        pltpu.make_async_copy(ybuf.at[sl], y_hbm.at[pl.ds(0, tme)], ssem.at[sl]).wait()

    def start_rows_inline(make, tile, sl, r0=0, r1=tme):
        for r in range(r0, r1):
            make(tile, sl, r).start()

    def expert(e):
        eid = (g * EXPERTS_PER_GROUP + e).astype(F32)
        ce = (jnp.where(rcol[:, 0:1] == eid, rcol[:, 2:3], 0.0)
              + jnp.where(rcol[:, 1:2] == eid, rcol[:, 3:4], 0.0))
        h = hbuf[...]
        a = _dot(h, wg[e])
        u = _dot(h, wu[e])
        act = (a * _sigmoid(a)) * u * ce
        acc[...] += _dot(act.astype(BF16), wd[e])

    def expert_with(e, run, alternatives):
        plain = run
        for cond, side_work in alternatives:
            @pl.when(cond)
            def _():
                side_work()
                expert(e)

            plain = jnp.logical_and(plain, jnp.logical_not(cond))

        @pl.when(plain)
        def _():
            expert(e)

    @pl.when(i == 0)
    def _():
        start_rows(gather_row, 0, 0)

    @pl.when(is_first)
    def _():
        wait_gather(slot)
        base = pl.multiple_of(slot * (tme * TOKEN_ROWS), tme * TOKEN_ROWS)
        x1 = jnp.concatenate([xbuf[pl.ds(base + j, tme, stride=TOKEN_ROWS), :] for j in range(TOKEN_ROWS)], axis=1)
        acc[...] = x1
        hbuf[...] = _rms(x1, nffn[...]).astype(BF16)
        rt = rt_ref[0]
        rcol[...] = jnp.transpose(jnp.concatenate([rt, jnp.zeros((LANES - rt.shape[0], tme), F32)], axis=0))

    valid = ivalid[i] == 1
    has2 = ihas2[i] == 1
    do_gather = jnp.logical_and(is_first, t + 1 < ntiles)
    do_scatter = jnp.logical_and(is_first, t >= 1)
    half = tme // 2
    scatter_rows = lambda r0, r1: (lambda: start_rows_inline(scatter_row, t - 1, 1 - slot, r0, r1))
    expert_with(ie0[i], valid, [(do_gather, lambda: start_rows_inline(gather_row, t + 1, 1 - slot))])
    expert_with(ie1[i], valid, [(jnp.logical_and(do_scatter, has2), scatter_rows(0, half)),
                                (jnp.logical_and(do_scatter, jnp.logical_not(has2)), scatter_rows(0, tme))])
    expert_with(ie2[i], has2, [(jnp.logical_and(do_scatter, has2), scatter_rows(half, tme))])
    for e in range(EXPERTS_PER_GROUP):
        @pl.when(flags[i * EXPERTS_PER_GROUP + e] == 1)
        def _():
            expert(e)

    @pl.when(ilast[i] == 1)
    def _():
        @pl.when(t >= 2)
        def _():
            wait_scatter(slot)

        ybuf[slot] = _rms(acc[...], nfin[...])

        @pl.when(t == ntiles - 1)
        def _():
            start_rows(scatter_row, t, slot)

    @pl.when(i == pl.num_programs(0) - 1)
    def _():
        for sl in range(min(2, ntiles)):
            wait_scatter(sl)


def _moe(x1t, rt, weg, weu, wed, nffn, nfin):
    n = x1t.shape[0] // TOKEN_ROWS
    tme = min(MOE_TILE, n)
    assert n % tme == 0
    ntiles = n // tme
    plan, rt_sorted = _moe_plan(rt, tme)
    max_items = ntiles + N_GROUPS - 1
    grp = lambda i, order, itile, igroup, *_: (igroup[i], 0, 0)
    til = lambda i, order, itile, *_: (itile[i], 0, 0)
    const = lambda i, *_: (0, 0)
    grid_spec = pltpu.PrefetchScalarGridSpec(
        num_scalar_prefetch=len(plan),
        grid=(max_items,),
        in_specs=[pl.BlockSpec(memory_space=pl.ANY),
                  pl.BlockSpec((1,) + rt_sorted.shape[1:], til),
                  pl.BlockSpec((EXPERTS_PER_GROUP, D_MODEL, EXPERT_FF), grp),
                  pl.BlockSpec((EXPERTS_PER_GROUP, D_MODEL, EXPERT_FF), grp),
                  pl.BlockSpec((EXPERTS_PER_GROUP, EXPERT_FF, D_MODEL), grp),
                  pl.BlockSpec((1, D_MODEL), const), pl.BlockSpec((1, D_MODEL), const)],
        out_specs=pl.BlockSpec(memory_space=pl.ANY),
        scratch_shapes=[pltpu.VMEM((2 * tme * TOKEN_ROWS, LANES), F32), pltpu.VMEM((2, tme, D_MODEL), F32),
                        pltpu.VMEM((tme, D_MODEL), F32), pltpu.VMEM((tme, D_MODEL), BF16),
                        pltpu.VMEM((tme, LANES), F32),
                        pltpu.SemaphoreType.DMA((2,)), pltpu.SemaphoreType.DMA((2,))],
    )
    return pl.pallas_call(
        functools.partial(_moe_kernel, tme=tme, ntiles=ntiles),
        grid_spec=grid_spec,
        out_shape=jax.ShapeDtypeStruct((n, D_MODEL), F32),
        compiler_params=_cparams(("arbitrary",)),
        name="moe",
    )(*plan, x1t, rt_sorted, weg, weu, wed, nffn, nfin)


def _rope_tables(positions):
    half = HEAD_DIM // 2
    inv_freq = ROPE_THETA ** (-jnp.arange(half, dtype=F32) / half)
    ang = positions.astype(F32)[:, None] * inv_freq[None, :]
    cos, sin = jnp.cos(ang), jnp.sin(ang)
    reps = LANES // HEAD_DIM
    return (jnp.tile(jnp.concatenate([cos, cos], -1), (1, reps)),
            jnp.tile(jnp.concatenate([-sin, sin], -1), (1, reps)))


def _prep_weights(norm_mix, w_in, w_gla_f2, b_gla_f, gla_norm, w_proj_attn, w_proj_gla, w_out, norm_ffn,
                  w_router_group, b_router_group, w_router_expert, b_router_expert):
    w = {"win": w_in.astype(BF16)}
    w["wf2"] = jnp.pad(w_gla_f2.astype(BF16), ((0, LANES - GLA_GATE_RANK), (0, 0)))
    w["bf"] = b_gla_f.reshape(1, -1)
    w["norm_mix"] = norm_mix.reshape(1, -1)
    w["gla_norm"] = gla_norm.reshape(1, -1)
    w["wpa"] = w_proj_attn.astype(BF16)
    w["wpg"] = w_proj_gla.astype(BF16)
    w["wo"] = w_out.astype(BF16)
    w["nf"] = norm_ffn.reshape(1, -1)
    pad = LANES - N_EXPERTS - N_GROUPS
    wr_t = jnp.pad(jnp.concatenate([w_router_expert, w_router_group], axis=1), ((0, 0), (0, pad))).T
    w["wr_hi"], w["wr_lo"] = _split_bf16(wr_t)
    w["br"] = jnp.broadcast_to(jnp.pad(jnp.concatenate([b_router_expert, b_router_group]), (0, pad))[:, None],
                               (LANES, LANES))
    return w


def _layer(x, positions_tab, cache, w, sinks, weg, weu, wed, nfin, tm):
    batch, seq, _ = x.shape
    n = batch * seq
    x2d = x.reshape(n, D_MODEL)
    cos_t, sin_t = positions_tab
    qa, ka, va, qg, kg, vg, rg, la, ga, gb = _inproj(x2d, w["norm_mix"], cos_t, sin_t, w, tm)
    if cache is None:
        a_out = _swa_prompt(sinks, qa, ka, va, batch, seq)
        last = lambda z: z.reshape(batch, seq, KV_WIDTH)[:, seq - WINDOW:].reshape(batch, WINDOW, ATTN_KV_HEADS, HEAD_DIM)
        new_k, new_v = last(ka), last(va)
        g_out, new_s = _gla_prompt(qg, kg, vg, la, rg, w["gla_norm"], batch, seq)
    else:
        cache_k, cache_v, state = cache
        a_out, new_k, new_v = _swa_sample(sinks, qa, ka, va, cache_k.reshape(batch, WINDOW, KV_WIDTH),
                                          cache_v.reshape(batch, WINDOW, KV_WIDTH), batch, seq)
        new_k = new_k.reshape(batch, WINDOW, ATTN_KV_HEADS, HEAD_DIM)
        new_v = new_v.reshape(batch, WINDOW, ATTN_KV_HEADS, HEAD_DIM)
        g_out, new_s = _gla_sample(qg, kg, vg, la, rg, state, w["gla_norm"], batch, seq)
    x1t, rt = _post(x2d, a_out, g_out, ga, gb, w, tm * 2 if x2d.shape[0] % (tm * 2) == 0 else tm)
    y = _moe(x1t, rt, weg, weu, wed, w["nf"], nfin)
    return y.reshape(batch, seq, D_MODEL), new_k, new_v, new_s


def kernel(x_prompt, x_sample, cache_win_k, cache_win_v, state_gla, norm_mix, w_in, w_gla_f2, b_gla_f, gla_norm,
           attn_sinks, w_proj_attn, w_proj_gla, w_out, norm_ffn, w_router_group, b_router_group, w_router_expert,
           b_router_expert, w_exp_gate, w_exp_up, w_exp_down, norm_final):
    assert norm_mix.shape[0] == 1, "single-layer step"
    seq_p = x_prompt.shape[1]
    dec_b, dec_t = x_sample.shape[0], x_sample.shape[1]
    w = _prep_weights(norm_mix[0], w_in[0], w_gla_f2[0], b_gla_f[0], gla_norm[0], w_proj_attn[0], w_proj_gla[0],
                      w_out[0], norm_ffn[0], w_router_group[0], b_router_group[0], w_router_expert[0],
                      b_router_expert[0])
    weg = w_exp_gate[0].astype(BF16)
    weu = w_exp_up[0].astype(BF16)
    wed = w_exp_down[0].astype(BF16)
    nfin = norm_final.reshape(1, -1)
    sinks = attn_sinks[0]
    tab_p = _rope_tables(jnp.arange(seq_p, dtype=jnp.int32))
    pos_s = PAST_LEN + jnp.arange(dec_t, dtype=jnp.int32)
    tab_s = tuple(jnp.tile(t, (dec_b, 1)) for t in _rope_tables(pos_s))
    tm_p = min(512, seq_p)
    tm_s = dec_b * dec_t
    yp, pk, pv, ps = _layer(x_prompt, tab_p, None, w, sinks, weg, weu, wed, nfin, tm_p)
    ys, sk, sv, ss = _layer(x_sample, tab_s, (cache_win_k[0], cache_win_v[0], state_gla[0]), w, sinks, weg, weu, wed,
                            nfin, tm_s)
    return (yp, ys, pk[None], pv[None], ps[None], sk[None], sv[None], ss[None])
```

```python
import functools
import math

import numpy as np
import jax
import jax.numpy as jnp
from jax import lax
from jax.experimental import pallas as pl
from jax.experimental.pallas import tpu as pltpu

F32 = jnp.float32
BF16 = jnp.bfloat16

D_MODEL = 1024
ATTN_HEADS = 8
ATTN_KV_HEADS = 2
GROUP = ATTN_HEADS // ATTN_KV_HEADS
HEAD_DIM = 64
ATTN_WIDTH = ATTN_HEADS * HEAD_DIM
KV_WIDTH = ATTN_KV_HEADS * HEAD_DIM
WINDOW = 128
ROPE_THETA = 10000.0
PAST_LEN = 8192
GLA_HEADS = 4
GLA_KEY_WIDTH = D_MODEL // 2
GLA_VALUE_WIDTH = D_MODEL
GLA_DK = GLA_KEY_WIDTH // GLA_HEADS
GLA_DV = GLA_VALUE_WIDTH // GLA_HEADS
GLA_GATE_RANK = 16
GLA_GATE_NORMALIZER = 16.0
N_GROUPS = 4
EXPERTS_PER_GROUP = 8
N_EXPERTS = N_GROUPS * EXPERTS_PER_GROUP
EXPERT_FF = 256
EPS = 1e-6
LOG2_E = 1.4426950408889634

LANES = 128
GLA_CHUNK = 128
GLA_HEADS_PER_STEP = 4
SWA_BLOCKS_PER_STEP = 4
SAMPLE_SEQ_BLOCK = 16
POST_PART_ROWS = 512
MOE_TILE = 256
MOE_SMALL_CALL_TOKENS = 2048
TOKEN_ROWS = D_MODEL // LANES
INPROJ_WIDTHS = (ATTN_WIDTH, 2 * KV_WIDTH, GLA_KEY_WIDTH, GLA_KEY_WIDTH, GLA_VALUE_WIDTH, GLA_VALUE_WIDTH)
ROUTER_ROWS = 40
VMEM_LIMIT = 56 * 1024 * 1024


def _cparams(sem):
    return pltpu.CompilerParams(dimension_semantics=sem, vmem_limit_bytes=VMEM_LIMIT)


def _rms(x, g):
    return x * lax.rsqrt(jnp.mean(x * x, axis=-1, keepdims=True) + EPS) * g


def _sigmoid(x):
    return 1.0 / (1.0 + jnp.exp(-x))


def _dot(a, b):
    return jnp.dot(a, b, preferred_element_type=F32)


def _dot_nt(a, b):
    return lax.dot_general(a, b, (((1,), (1,)), ((), ())), preferred_element_type=F32)


def _dot_tn(a, b):
    return lax.dot_general(a, b, (((0,), (0,)), ((), ())), preferred_element_type=F32)


def _split_bf16(x):
    hi = x.astype(BF16)
    lo = (x - hi.astype(F32)).astype(BF16)
    return hi, lo


def _inproj_kernel(x_ref, g_ref, cos_ref, sin_ref, win, wf2, bf,
                   qa_o, ka_o, va_o, qg_o, kg_o, vg_o, rg_o, la_o, ga_o, gb_o, wgate):
    f0 = sum(INPROJ_WIDTHS)

    @pl.when(pl.program_id(0) == 0)
    def _():
        tail = win[:, f0:]
        wgate[...] = tail[:, GLA_GATE_RANK:GLA_GATE_RANK + 2 * D_MODEL]

    tm = x_ref.shape[0]
    halves = [slice(0, tm // 2), slice(tm // 2, tm)] if tm % 16 == 0 else [slice(0, tm)]
    hbs = [_rms(x_ref[r, :], g_ref[...]).astype(BF16) for r in halves]
    cols = np.cumsum((0,) + INPROJ_WIDTHS[:-1])
    wqa, wkva, wqg, wkg, wvg, wrg = (win.at[:, int(c):int(c) + wd] for c, wd in zip(cols, INPROJ_WIDTHS))
    wga = wgate.at[:, :D_MODEL]
    wgb = wgate.at[:, D_MODEL:]
    wf = win.at[:, f0:f0 + LANES]
    lane = lax.broadcasted_iota(jnp.int32, (halves[0].stop - halves[0].start, LANES), 1)
    first_half = (lane % HEAD_DIM) < (HEAD_DIM // 2)

    def rope(t, r):
        swapped = jnp.where(first_half, pltpu.roll(t, LANES - HEAD_DIM // 2, 1), pltpu.roll(t, HEAD_DIM // 2, 1))
        return t * cos_ref[r, :] + swapped * sin_ref[r, :]

    for r, hb in zip(halves, hbs):
        ga_o[r, :] = _sigmoid(_dot(hb, wga[...])).astype(BF16)
    for r, hb in zip(halves, hbs):
        gb_o[r, :] = _sigmoid(_dot(hb, wgb[...])).astype(BF16)
    for r, hb in zip(halves, hbs):
        z = _dot(_dot(hb, wf[...]).astype(BF16), wf2[...]) + bf[...]
        la_o[r, :] = (jnp.minimum(z, 0.0) - jnp.log1p(jnp.exp(-jnp.abs(z)))) * (1.0 / GLA_GATE_NORMALIZER)
    for r, hb in zip(halves, hbs):
        qa = _dot(hb, wqa[...])
        for c in range(ATTN_WIDTH // LANES):
            sl = slice(c * LANES, (c + 1) * LANES)
            qa_o[r, sl] = (rope(qa[:, sl], r) * (HEAD_DIM ** -0.5)).astype(BF16)
    for r, hb in zip(halves, hbs):
        kva = _dot(hb, wkva[...])
        ka_o[r, :] = rope(kva[:, :KV_WIDTH], r)
        va_o[r, :] = kva[:, KV_WIDTH:]
    for r, hb in zip(halves, hbs):
        qg_o[r, :] = (_dot(hb, wqg[...]) * (GLA_DK ** -0.5)).astype(BF16)
    for r, hb in zip(halves, hbs):
        kg_o[r, :] = _dot(hb, wkg[...]).astype(BF16)
    for r, hb in zip(halves, hbs):
        vg_o[r, :] = _dot(hb, wvg[...]).astype(BF16)
    for r, hb in zip(halves, hbs):
        rg_o[r, :] = _dot(hb, wrg[...]).astype(BF16)


def _inproj(x2d, norm_g, cos_t, sin_t, w, tm):
    n = x2d.shape[0]
    ntab = cos_t.shape[0] // tm
    row = lambda i: (i, 0)
    const = lambda i: (0, 0)
    tab = lambda i: (i % ntab, 0)
    wnames = ("win", "wf2", "bf")
    wspecs = [pl.BlockSpec(w[k].shape, const) for k in wnames]
    widths = (ATTN_WIDTH, KV_WIDTH, KV_WIDTH, GLA_KEY_WIDTH, GLA_KEY_WIDTH, GLA_VALUE_WIDTH, GLA_VALUE_WIDTH,
              GLA_KEY_WIDTH, D_MODEL, D_MODEL)
    dtypes = (BF16, F32, F32, BF16, BF16, BF16, BF16, F32, BF16, BF16)
    return pl.pallas_call(
        _inproj_kernel,
        grid=(n // tm,),
        in_specs=[pl.BlockSpec((tm, D_MODEL), row), pl.BlockSpec((1, D_MODEL), const),
                  pl.BlockSpec((tm, LANES), tab), pl.BlockSpec((tm, LANES), tab)] + wspecs,
        out_specs=[pl.BlockSpec((tm, wd), row) for wd in widths],
        out_shape=[jax.ShapeDtypeStruct((n, wd), dt) for wd, dt in zip(widths, dtypes)],
        scratch_shapes=[pltpu.VMEM((D_MODEL, 2 * D_MODEL), BF16)],
        compiler_params=_cparams(("arbitrary",)),
        name="inproj",
    )(x2d, norm_g, cos_t, sin_t, *[w[k] for k in wnames])


def _swa_prompt_kernel(sink_ref, q_ref, kc_ref, kp_ref, vc_ref, vp_ref, o_ref, *, qb):
    n = pl.program_id(1)
    k3 = jnp.concatenate([kp_ref[...], kc_ref[...]], axis=0).astype(BF16)
    v3 = jnp.concatenate([vp_ref[...], vc_ref[...]], axis=0).astype(BF16)
    t = lax.broadcasted_iota(jnp.int32, (WINDOW, 2 * WINDOW), 0)
    j = lax.broadcasted_iota(jnp.int32, (WINDOW, 2 * WINDOW), 1)
    band = (j >= t) & (j <= t + WINDOW)
    zeros = jnp.zeros((2 * WINDOW, HEAD_DIM), BF16)
    ones = jnp.ones((2 * WINDOW, HEAD_DIM), BF16)
    lane = lax.broadcasted_iota(jnp.int32, (WINDOW, LANES), 1)
    chains = [(blk, h) for blk in range(qb) for h in range(ATTN_HEADS)]
    scores = []
    for blk, h in chains:
        kv = h // GROUP
        rows = slice(blk * WINDOW, (blk + 1) * WINDOW)
        keys = slice(blk * WINDOW, (blk + 2) * WINDOW)
        s = _dot_nt(q_ref[rows, h * HEAD_DIM:(h + 1) * HEAD_DIM], k3[keys, kv * HEAD_DIM:(kv + 1) * HEAD_DIM])
        valid = band & ((j >= WINDOW) | (n > 0)) if blk == 0 else band
        scores.append(jnp.where(valid, s, -jnp.inf))
    probs, sink_terms = [], []
    for (blk, h), s in zip(chains, scores):
        m = jnp.maximum(jnp.max(s, axis=-1, keepdims=True), sink_ref[h])
        probs.append(jnp.exp(s - m).astype(BF16))
        sink_terms.append(jnp.exp(sink_ref[h] - m))
    for blk in range(qb):
        rows = slice(blk * WINDOW, (blk + 1) * WINDOW)
        keys = slice(blk * WINDOW, (blk + 2) * WINDOW)
        for kv in range(ATTN_KV_HEADS):
            vv = v3[keys, kv * HEAD_DIM:(kv + 1) * HEAD_DIM]
            vext = (jnp.concatenate([vv, zeros, ones, zeros], axis=1), jnp.concatenate([zeros, vv, zeros, ones], axis=1))
            for pr in range(GROUP // 2):
                h0 = kv * GROUP + pr * 2
                c0 = blk * ATTN_HEADS + h0
                acc = _dot(probs[c0], vext[0]) + _dot(probs[c0 + 1], vext[1])
                l = acc[:, LANES:] + jnp.where(lane < HEAD_DIM, sink_terms[c0], sink_terms[c0 + 1])
                o_ref[rows, h0 * HEAD_DIM:h0 * HEAD_DIM + LANES] = (acc[:, :LANES] / l).astype(BF16)


def _swa_prompt(sinks, qa, ka, va, batch, seq):
    nb = seq // WINDOW
    qb = math.gcd(SWA_BLOCKS_PER_STEP, nb)
    steps = nb // qb
    cur = lambda b, n: (b * steps + n, 0)
    prev = lambda b, n: (b * nb + jnp.maximum(n * qb - 1, 0), 0)
    return pl.pallas_call(
        functools.partial(_swa_prompt_kernel, qb=qb),
        grid=(batch, steps),
        in_specs=[pl.BlockSpec(memory_space=pltpu.SMEM),
                  pl.BlockSpec((qb * WINDOW, ATTN_WIDTH), cur),
                  pl.BlockSpec((qb * WINDOW, KV_WIDTH), cur), pl.BlockSpec((WINDOW, KV_WIDTH), prev),
                  pl.BlockSpec((qb * WINDOW, KV_WIDTH), cur), pl.BlockSpec((WINDOW, KV_WIDTH), prev)],
        out_specs=pl.BlockSpec((qb * WINDOW, ATTN_WIDTH), cur),
        out_shape=jax.ShapeDtypeStruct(qa.shape, BF16),
        compiler_params=_cparams(("parallel", "parallel")),
        name="swa_prompt",
    )(sinks, qa, ka, ka, va, va)


def _swa_sample_kernel(sink_ref, q_ref, kn_ref, vn_ref, kn3_ref, vn3_ref, ck_ref, cv_ref, o_ref, nk_ref, nv_ref,
                       *, t_new):
    sb = ck_ref.shape[0]
    spv = 8 // t_new
    nq = GROUP * 8
    nc = spv * WINDOW
    qi = lax.broadcasted_iota(jnp.int32, (nq, nc), 0) % 8
    ci = lax.broadcasted_iota(jnp.int32, (nq, nc), 1)
    valid_c = (qi // t_new == ci // WINDOW) & (ci % WINDOW >= qi % t_new)
    qn = lax.broadcasted_iota(jnp.int32, (nq, 8), 0) % 8
    cn = lax.broadcasted_iota(jnp.int32, (nq, 8), 1)
    valid_n = (qn // t_new == cn // t_new) & (cn <= qn)
    grow = lax.broadcasted_iota(jnp.int32, (nq, 1), 0) // 8
    chains = [(vr, kv) for vr in range(sb // spv) for kv in range(ATTN_KV_HEADS)]
    scored = []
    for vr, kv in chains:
        r8 = slice(8 * vr, 8 * vr + 8)
        cs = slice(kv * HEAD_DIM, (kv + 1) * HEAD_DIM)
        heads = [kv * GROUP + g for g in range(GROUP)]
        qs = jnp.concatenate([q_ref[r8, h * HEAD_DIM:(h + 1) * HEAD_DIM] for h in heads], axis=0)
        ck = ck_ref[vr * spv:(vr + 1) * spv, :, cs].reshape(nc, HEAD_DIM).astype(BF16)
        kn = kn_ref[r8, cs].astype(BF16)
        scored.append((jnp.where(valid_c, _dot_nt(qs, ck), -jnp.inf), jnp.where(valid_n, _dot_nt(qs, kn), -jnp.inf)))
    soft = []
    for (vr, kv), (s_c, s_n) in zip(chains, scored):
        sink = sink_ref[kv * GROUP]
        for g in range(1, GROUP):
            sink = jnp.where(grow == g, sink_ref[kv * GROUP + g], sink)
        m = jnp.maximum(jnp.maximum(jnp.max(s_c, axis=-1, keepdims=True), jnp.max(s_n, axis=-1, keepdims=True)), sink)
        p_c = jnp.exp(s_c - m)
        p_n = jnp.exp(s_n - m)
        l = jnp.sum(p_c, axis=-1, keepdims=True) + jnp.sum(p_n, axis=-1, keepdims=True) + jnp.exp(sink - m)
        soft.append((p_c.astype(BF16), p_n.astype(BF16), l))
    for (vr, kv), (p_c, p_n, l) in zip(chains, soft):
        r8 = slice(8 * vr, 8 * vr + 8)
        cs = slice(kv * HEAD_DIM, (kv + 1) * HEAD_DIM)
        cv = cv_ref[vr * spv:(vr + 1) * spv, :, cs].reshape(nc, HEAD_DIM).astype(BF16)
        o = (_dot(p_c, cv) + _dot(p_n, vn_ref[r8, cs].astype(BF16))) / l
        for a in range(GROUP // 2):
            pair = jnp.concatenate([o[16 * a:16 * a + 8], o[16 * a + 8:16 * a + 16]], axis=1)
            c0 = (kv * GROUP + 2 * a) * HEAD_DIM
            o_ref[r8, c0:c0 + LANES] = pair.astype(BF16)
    nk_ref[:, 0:WINDOW - t_new, :] = ck_ref[:, t_new:WINDOW, :]
    nk_ref[:, WINDOW - t_new:WINDOW, :] = kn3_ref[...]
    nv_ref[:, 0:WINDOW - t_new, :] = cv_ref[:, t_new:WINDOW, :]
    nv_ref[:, WINDOW - t_new:WINDOW, :] = vn3_ref[...]


def _swa_sample(sinks, qa, ka, va, cache_k, cache_v, batch, t_new):
    sb = SAMPLE_SEQ_BLOCK
    rows = sb * t_new
    r2 = lambda i: (i, 0)
    r3 = lambda i: (i, 0, 0)
    ka3 = ka.reshape(batch, t_new, KV_WIDTH)
    va3 = va.reshape(batch, t_new, KV_WIDTH)
    return pl.pallas_call(
        functools.partial(_swa_sample_kernel, t_new=t_new),
        grid=(batch // sb,),
        in_specs=[pl.BlockSpec(memory_space=pltpu.SMEM),
                  pl.BlockSpec((rows, ATTN_WIDTH), r2),
                  pl.BlockSpec((rows, KV_WIDTH), r2), pl.BlockSpec((rows, KV_WIDTH), r2),
                  pl.BlockSpec((sb, t_new, KV_WIDTH), r3), pl.BlockSpec((sb, t_new, KV_WIDTH), r3),
                  pl.BlockSpec((sb, WINDOW, KV_WIDTH), r3), pl.BlockSpec((sb, WINDOW, KV_WIDTH), r3)],
        out_specs=[pl.BlockSpec((rows, ATTN_WIDTH), r2),
                   pl.BlockSpec((sb, WINDOW, KV_WIDTH), r3), pl.BlockSpec((sb, WINDOW, KV_WIDTH), r3)],
        out_shape=[jax.ShapeDtypeStruct(qa.shape, BF16),
                   jax.ShapeDtypeStruct(cache_k.shape, F32), jax.ShapeDtypeStruct(cache_v.shape, F32)],
        compiler_params=_cparams(("parallel",)),
        name="swa_sample",
    )(sinks, qa, ka, va, ka3, va3, cache_k, cache_v)


def _gla_constants(c, seg, with_rem):
    t = np.arange(c)
    sid = t // seg
    same = sid[:, None] == sid[None, :]
    levels = []
    m = seg // 2
    while m >= 1:
        levels.append(m)
        m //= 2
    mats, roles, masks = [], [], []
    for m in levels:
        blk = t // (2 * m)
        second = (t // m) % 2 == 1
        p = blk * 2 * m + m - 1
        u = t[None, :]
        mq = (u > p[:, None]) & (u <= t[:, None])
        mk = (u > t[:, None]) & (u <= p[:, None])
        if m > 1:
            mats.append(np.where(second[:, None], mq, mk))
        roles.append(np.broadcast_to(second[:, None], (c, LANES)))
        masks.append((blk[:, None] == blk[None, :]) & second[:, None] & ~second[None, :])
    masks.append(np.eye(c, dtype=bool))
    mats.append(same & (t[None, :] <= t[:, None]))
    if with_rem:
        mats.append(same & (t[None, :] > t[:, None]))
    mall = np.concatenate(mats, 0).astype(np.float32)
    mall = jnp.asarray(np.concatenate([mall, mall], 1), BF16)
    role = jnp.asarray(np.concatenate(roles, 0).astype(np.float32))
    mask = jnp.asarray(np.concatenate(masks, 0).astype(np.float32))
    return len(levels), mall, role, mask


def _gla_exponents(la, mall):
    la2 = la * LOG2_E
    hl = jnp.concatenate(_split_bf16(la2), axis=0)
    return _dot(mall, hl), hl, la2


def _gla_scores(qb, kb, e2, la2, role_ref, mask_ref, nlev, c):
    qf = qb.astype(F32)
    kf = kb.astype(F32)
    terms = []
    for lv in range(nlev):
        sl = slice(lv * c, (lv + 1) * c)
        m = 1 << (nlev - 1 - lv)
        if m % 8 == 0:
            pe = jnp.exp2(e2[sl])
            x = jnp.concatenate([(qf if blk % 2 else kf)[blk * m:(blk + 1) * m] * pe[blk * m:(blk + 1) * m]
                                 for blk in range(c // m)], axis=0).astype(BF16)
        else:
            second = role_ref[sl, :] > 0.5
            e = e2[sl] if lv < nlev - 1 else jnp.where(second, la2, 0.0)
            x = (jnp.where(second, qf, kf) * jnp.exp2(e)).astype(BF16)
        terms.append((x, x, lv))
    terms.append((qb, kb, nlev))
    mask = lambda i: mask_ref[i * c:(i + 1) * c, :]
    a = None
    pair = c % LANES == 0
    while terms:
        if pair and len(terms) >= 2:
            (l0, r0, i0), (l1, r1, i1) = terms.pop(), terms.pop()
            z = jnp.zeros_like(r0)
            rhs = jnp.concatenate([jnp.concatenate([r0, z], axis=1), jnp.concatenate([z, r1], axis=1)], axis=0)
            g = _dot_nt(jnp.concatenate([l0, l1], axis=1), rhs)
            t = mask(i0) * g[:, :c] + mask(i1) * g[:, c:]
        else:
            l0, r0, i0 = terms.pop()
            t = mask(i0) * _dot_nt(l0, r0)
        a = t if a is None else a + t
    return a, qf, kf


def _gla_out(o, g, r):
    r = r.astype(F32)
    return (_rms(o, g) * (r * _sigmoid(r))).astype(BF16)


def _gla_prompt_kernel(q_ref, k_ref, v_ref, la_ref, r_ref, mall_ref, role_ref, mask_ref, g_ref, o_ref, s_ref,
                       s_scr, *, nlev, nchunks):
    c = GLA_CHUNK
    hp = GLA_HEADS_PER_STEP
    s_scr[...] = jnp.zeros_like(s_scr)

    def chunk(i, carry):
        rows = pl.ds(pl.multiple_of(i * c, c), c)
        e2_all, _, la2_all = _gla_exponents(la_ref[rows, :], mall_ref[...])
        ksl = [slice(h * GLA_DK, (h + 1) * GLA_DK) for h in range(hp)]
        vsl = [slice(h * GLA_DV, (h + 1) * GLA_DV) for h in range(hp)]
        scores = [_gla_scores(q_ref[rows, ksl[h]], k_ref[rows, ksl[h]], e2_all[:, ksl[h]], la2_all[:, ksl[h]], role_ref,
                              mask_ref, nlev, c)[0] for h in range(hp)]
        for h in range(hp):
            b = e2_all[(nlev - 1) * c:nlev * c, ksl[h]]
            qf = q_ref[rows, ksl[h]].astype(F32)
            o = (_dot(scores[h].astype(BF16), v_ref[rows, vsl[h]])
                 + _dot((qf * jnp.exp2(b)).astype(BF16), s_scr[h].astype(BF16)))
            o_ref[rows, vsl[h]] = _gla_out(o, g_ref[...], r_ref[rows, vsl[h]])
        for h in range(hp):
            b = e2_all[(nlev - 1) * c:nlev * c, ksl[h]]
            kf = k_ref[rows, ksl[h]].astype(F32)
            kt = (kf * jnp.exp2(b[c - 1:c, :] - b)).astype(BF16)
            dec = jnp.exp2(jnp.transpose(b[c - 8:c, :]))[:, 7:8]
            s_scr[h] = dec * s_scr[h] + _dot_tn(kt, v_ref[rows, vsl[h]])
        return carry

    lax.fori_loop(0, nchunks, chunk, 0)
    s_ref[0] = s_scr[...]


def _gla_prompt(qg, kg, vg, la, rg, gnorm, batch, seq):
    nlev, mall, role, mask = _gla_constants(GLA_CHUNK, GLA_CHUNK, with_rem=False)
    hp = GLA_HEADS_PER_STEP
    bh = lambda b, h: (b, h)
    const = lambda b, h: (0, 0)
    return pl.pallas_call(
        functools.partial(_gla_prompt_kernel, nlev=nlev, nchunks=seq // GLA_CHUNK),
        grid=(batch, GLA_HEADS // hp),
        in_specs=[pl.BlockSpec((seq, hp * GLA_DK), bh), pl.BlockSpec((seq, hp * GLA_DK), bh),
                  pl.BlockSpec((seq, hp * GLA_DV), bh), pl.BlockSpec((seq, hp * GLA_DK), bh),
                  pl.BlockSpec((seq, hp * GLA_DV), bh),
                  pl.BlockSpec(mall.shape, const), pl.BlockSpec(role.shape, const), pl.BlockSpec(mask.shape, const),
                  pl.BlockSpec((1, GLA_DV), const)],
        out_specs=[pl.BlockSpec((seq, hp * GLA_DV), bh),
                   pl.BlockSpec((1, hp, GLA_DK, GLA_DV), lambda b, h: (b, h, 0, 0))],
        out_shape=[jax.ShapeDtypeStruct(vg.shape, BF16),
                   jax.ShapeDtypeStruct((batch, GLA_HEADS, GLA_DK, GLA_DV), F32)],
        scratch_shapes=[pltpu.VMEM((hp, GLA_DK, GLA_DV), F32)],
        compiler_params=_cparams(("parallel", "parallel")),
        name="gla_prompt",
    )(qg, kg, vg, la, rg, mall, role, mask, gnorm)


def _gla_sample_kernel(q_ref, k_ref, v_ref, la_ref, r_ref, s0_ref, mall_ref, role_ref, mask_ref, msum_ref, g_ref,
                       o_ref, s_ref, *, nlev, t_new):
    sb = s0_ref.shape[0]
    c = sb * t_new
    spv = 8 // t_new
    e2_all, hl_all, la2_all = _gla_exponents(la_ref[...], mall_ref[...])
    seq_in_tile = lax.broadcasted_iota(jnp.int32, (8, GLA_DV), 0) // t_new
    seq_of_row = lax.broadcasted_iota(jnp.int32, (c, GLA_DV), 0) // t_new
    ksl = [slice(h * GLA_DK, (h + 1) * GLA_DK) for h in range(GLA_HEADS)]
    scores = [_gla_scores(q_ref[:, ksl[h]], k_ref[:, ksl[h]], e2_all[:, ksl[h]], la2_all[:, ksl[h]], role_ref, mask_ref,
                          nlev, c) for h in range(GLA_HEADS)]
    for h in range(GLA_HEADS):
        ks = ksl[h]
        vs = slice(h * GLA_DV, (h + 1) * GLA_DV)
        v = v_ref[:, vs]
        e2 = e2_all[:, ks]
        hl = hl_all[:, ks]
        a, qf, kf = scores[h]
        qe = (qf * jnp.exp2(e2[(nlev - 1) * c:nlev * c])).astype(BF16)
        kt_t = jnp.transpose(kf * jnp.exp2(e2[nlev * c:(nlev + 1) * c])).astype(BF16)
        bl_t = jnp.transpose(_dot(msum_ref[...], hl[:c]) + _dot(msum_ref[...], hl[c:]))
        dec_t = jnp.exp2(bl_t)
        inter = []
        for vr in range(c // 8):
            rows8 = qe[8 * vr:8 * vr + 8]
            tile = None
            for u in range(spv):
                j = vr * spv + u
                s0 = s0_ref[j, h]
                r = _dot(rows8, s0.astype(BF16))
                tile = r if tile is None else jnp.where(seq_in_tile == u, r, tile)
                vj = jnp.where(seq_of_row == j, v, jnp.zeros_like(v))
                s_ref[j, h] = dec_t[:, j:j + 1] * s0 + _dot(kt_t, vj)
            inter.append(tile)
        o = _dot(a.astype(BF16), v) + jnp.concatenate(inter, axis=0)
        o_ref[:, vs] = _gla_out(o, g_ref[...], r_ref[:, vs])


def _gla_sample(qg, kg, vg, la, rg, state, gnorm, batch, t_new):
    sb = SAMPLE_SEQ_BLOCK
    c = sb * t_new
    assert 8 % t_new == 0 and c % 8 == 0
    nlev, mall, role, mask = _gla_constants(c, t_new, with_rem=True)
    msum = jnp.asarray((np.arange(c)[None, :] // t_new == np.arange(sb)[:, None]).astype(np.float32), BF16)
    rows = lambda i: (i, 0)
    const = lambda i: (0, 0)
    st = lambda i: (i, 0, 0, 0)
    return pl.pallas_call(
        functools.partial(_gla_sample_kernel, nlev=nlev, t_new=t_new),
        grid=(batch // sb,),
        in_specs=[pl.BlockSpec((c, GLA_KEY_WIDTH), rows), pl.BlockSpec((c, GLA_KEY_WIDTH), rows),
                  pl.BlockSpec((c, GLA_VALUE_WIDTH), rows), pl.BlockSpec((c, GLA_KEY_WIDTH), rows),
                  pl.BlockSpec((c, GLA_VALUE_WIDTH), rows),
                  pl.BlockSpec((sb, GLA_HEADS, GLA_DK, GLA_DV), st),
                  pl.BlockSpec(mall.shape, const), pl.BlockSpec(role.shape, const), pl.BlockSpec(mask.shape, const),
                  pl.BlockSpec(msum.shape, const), pl.BlockSpec((1, GLA_DV), const)],
        out_specs=[pl.BlockSpec((c, GLA_VALUE_WIDTH), rows), pl.BlockSpec((sb, GLA_HEADS, GLA_DK, GLA_DV), st)],
        out_shape=[jax.ShapeDtypeStruct(vg.shape, BF16), jax.ShapeDtypeStruct(state.shape, F32)],
        compiler_params=_cparams(("parallel",)),
        name="gla_sample",
    )(qg, kg, vg, la, rg, state, mall, role, mask, msum, gnorm)


def _post_kernel(x_ref, a_ref, gl_ref, ga_ref, gb_ref, wpa, wpg, wo, nf_ref, wr_hi, wr_lo, br, x1t_o, rt_o):
    tm = x_ref.shape[0]
    part = POST_PART_ROWS if tm % POST_PART_ROWS == 0 else tm
    halves = [slice(r0, r0 + part) for r0 in range(0, tm, part)]
    proj =[(_dot(a_ref[r, :], wpa[...]), _dot(gl_ref[r, :], wpg[...])) for r in halves]
    x1s = []
    for r, (pa, pg) in zip(halves, proj):
        merged = ga_ref[r, :].astype(F32) * pa + gb_ref[r, :].astype(F32) * pg
        x1s.append(x_ref[r, :] + _dot(merged.astype(BF16), wo[...]))
    logits = []
    for r, x1 in zip(halves, x1s):
        rows = r.stop - r.start
        for j in range(TOKEN_ROWS):
            x1t_o[pl.ds(r.start * TOKEN_ROWS + j, rows, stride=TOKEN_ROWS), :] = x1[:, j * LANES:(j + 1) * LANES]
        h_hi, h_lo = _split_bf16(_rms(x1, nf_ref[...]))
        logits.append(_dot_nt(wr_hi[...], h_hi) + _dot_nt(wr_hi[...], h_lo) + _dot_nt(wr_lo[...], h_hi))
    nrow = ROUTER_ROWS
    big = jnp.int32(LANES)
    ninf = -jnp.inf
    for r, lt in zip(halves, logits):
        lt = lt[:nrow] + br[:nrow, 0:1]
        row = lax.broadcasted_iota(jnp.int32, lt.shape, 0)

        def first_max(vals):
            mx = jnp.max(vals, axis=0, keepdims=True)
            return mx, jnp.min(jnp.where(vals == mx, row, big), axis=0, keepdims=True)

        gl = jnp.where((row >= N_EXPERTS) & (row < N_EXPERTS + N_GROUPS), lt, ninf)
        gmax, gidx = first_max(gl)
        p_sel = 1.0 / jnp.sum(jnp.exp(gl - gmax), axis=0, keepdims=True)
        lo = (gidx - N_EXPERTS) * EXPERTS_PER_GROUP
        el = jnp.where((row >= lo) & (row < lo + EXPERTS_PER_GROUP), lt, ninf)
        v1, i1 = first_max(el)
        el2 = jnp.where(row == i1, ninf, el)
        v2, i2 = first_max(el2)
        t = jnp.exp(v2 - v1)
        w1 = p_sel / (1.0 + t)
        w2 = p_sel * t / (1.0 + t)
        row8 = lax.broadcasted_iota(jnp.int32, (rt_o.shape[0], r.stop - r.start), 0)
        pick = lambda k, val, rest: jnp.where(row8 == k, val, rest)
        rt_o[:, r] = pick(0, i1.astype(F32), pick(1, i2.astype(F32), pick(2, w1, pick(3, w2, 0.0))))


def _post(x2d, a_out, g_out, ga, gb, w, tm):
    n = x2d.shape[0]
    row = lambda i: (i, 0)
    const = lambda i: (0, 0)
    wnames = ("wpa", "wpg", "wo", "nf", "wr_hi", "wr_lo", "br")
    rt_rows = 8
    return pl.pallas_call(
        _post_kernel,
        grid=(n // tm,),
        in_specs=[pl.BlockSpec((tm, D_MODEL), row), pl.BlockSpec((tm, ATTN_WIDTH), row),
                  pl.BlockSpec((tm, GLA_VALUE_WIDTH), row), pl.BlockSpec((tm, D_MODEL), row),
                  pl.BlockSpec((tm, D_MODEL), row)] + [pl.BlockSpec(w[k].shape, const) for k in wnames],
        out_specs=[pl.BlockSpec((tm * TOKEN_ROWS, LANES), row),
                   pl.BlockSpec((rt_rows, tm), lambda i: (0, i))],
        out_shape=[jax.ShapeDtypeStruct((n * TOKEN_ROWS, LANES), F32), jax.ShapeDtypeStruct((rt_rows, n), F32)],
        compiler_params=_cparams(("parallel",)),
        name="post_mixer",
    )(x2d, a_out, g_out, ga, gb, *[w[k] for k in wnames])


def _moe_plan(rt, tme):
    n = rt.shape[1]
    ntiles = n // tme
    max_items = ntiles + N_GROUPS - 1
    i1, i2 = rt[0].astype(jnp.int32), rt[1].astype(jnp.int32)
    grp = i1 // EXPERTS_PER_GROUP
    lo = jnp.minimum(i1, i2) % EXPERTS_PER_GROUP
    hi = jnp.maximum(i1, i2) % EXPERTS_PER_GROUP
    key = (grp * EXPERTS_PER_GROUP + lo) * EXPERTS_PER_GROUP + hi
    order = jnp.argsort(key, stable=True).astype(jnp.int32)
    skey = key[order].reshape(ntiles, tme)
    sg = skey // (EXPERTS_PER_GROUP * EXPERTS_PER_GROUP)
    slo = (skey // EXPERTS_PER_GROUP) % EXPERTS_PER_GROUP
    shi = skey % EXPERTS_PER_GROUP
    ev = jnp.arange(EXPERTS_PER_GROUP)
    in_g = sg[:, :, None] == jnp.arange(N_GROUPS)
    uses_e = (slo[:, :, None] == ev) | (shi[:, :, None] == ev)
    flags_tge = jnp.any(in_g[:, :, :, None] & uses_e[:, :, None, :], axis=1)
    present = jnp.any(in_g, axis=1).reshape(-1)
    pos = jnp.cumsum(present) - 1
    n_items = pos[-1] + 1
    src = jnp.zeros((max_items,), jnp.int32).at[jnp.where(present, pos, max_items)].set(
        jnp.arange(ntiles * N_GROUPS, dtype=jnp.int32), mode="drop")
    it = jnp.arange(max_items)
    valid = it < n_items
    last_src = src[n_items - 1]
    src = jnp.where(valid, src, last_src)
    item_tile = src // N_GROUPS
    item_group = src % N_GROUPS
    prev_tile = jnp.concatenate([jnp.full((1,), -1, jnp.int32), item_tile[:-1]])
    next_tile = jnp.concatenate([item_tile[1:], jnp.full((1,), -1, jnp.int32)])
    first = valid & (item_tile != prev_tile)
    last = valid & ((item_tile != next_tile) | (it == n_items - 1))
    flags = flags_tge[item_tile, item_group] & valid[:, None]
    e0 = jnp.argmax(flags, axis=1)
    rest = flags & (ev[None, :] != e0[:, None])
    e1 = jnp.argmax(rest, axis=1)
    rest = rest & (ev[None, :] != e1[:, None])
    has2 = jnp.any(rest, axis=1)
    e2 = jnp.argmax(rest, axis=1)
    rest = rest & (ev[None, :] != e2[:, None])
    rt_sorted = rt[:, order].reshape(rt.shape[0], ntiles, tme).transpose(1, 0, 2)
    i32 = lambda z: z.astype(jnp.int32)
    plan = (order, i32(item_tile), i32(item_group), i32(first), i32(last), i32(valid), i32(e0), i32(e1), i32(e2),
            i32(has2), i32(rest.reshape(-1)))
    return plan, rt_sorted


def _moe_kernel(order, itile, igroup, ifirst, ilast, ivalid, ie0, ie1, ie2, ihas2, flags, x_hbm, rt_ref, wg, wu, wd, nffn, nfin,
                y_hbm, xbuf, ybuf, acc, hbuf, rcol, gsem, ssem, *, tme, ntiles):
    i = pl.program_id(0)
    t = itile[i]
    slot = t % 2
    g = igroup[i]
    is_first = ifirst[i] == 1

    def gather_row(tile, sl, r):
        tok = order[tile * tme + r]
        src = x_hbm.at[pl.ds(pl.multiple_of(tok * TOKEN_ROWS, TOKEN_ROWS), TOKEN_ROWS)]
        dst = xbuf.at[pl.ds(pl.multiple_of((sl * tme + r) * TOKEN_ROWS, TOKEN_ROWS), TOKEN_ROWS)]
        return pltpu.make_async_copy(src, dst, gsem.at[sl])

    def scatter_row(tile, sl, r):
        tok = order[tile * tme + r]
        return pltpu.make_async_copy(ybuf.at[sl, pl.ds(r, 1)], y_hbm.at[pl.ds(tok, 1)], ssem.at[sl])

    def start_rows(make, tile, sl):
        def body(r, c):
            make(tile, sl, r).start()
            return c
        lax.fori_loop(0, tme, body, 0, unroll=8)

    def wait_gather(sl):
        rows = tme * TOKEN_ROWS
        pltpu.make_async_copy(x_hbm.at[pl.ds(0, rows)], xbuf.at[pl.ds(pl.multiple_of(sl * rows, rows), rows)],
                              gsem.at[sl]).wait()

    def wait_scatter(sl):
        pltpu.make_async_copy(ybuf.at[sl], y_hbm.at[pl.ds(0, tme)], ssem.at[sl]).wait()

    def start_rows_inline(make, tile, sl, r0=0, r1=tme):
        for r in range(r0, r1):
            make(tile, sl, r).start()

    def expert(e):
        eid = (g * EXPERTS_PER_GROUP + e).astype(F32)
        ce = (jnp.where(rcol[:, 0:1] == eid, rcol[:, 2:3], 0.0)
              + jnp.where(rcol[:, 1:2] == eid, rcol[:, 3:4], 0.0))
        h = hbuf[...]
        a = _dot(h, wg[e])
        u = _dot(h, wu[e])
        act = (a * _sigmoid(a)) * u * ce
        acc[...] += _dot(act.astype(BF16), wd[e])

    def expert_with(e, run, alternatives):
        plain = run
        for cond, side_work in alternatives:
            @pl.when(cond)
            def _():
                side_work()
                expert(e)

            plain = jnp.logical_and(plain, jnp.logical_not(cond))

        @pl.when(plain)
        def _():
            expert(e)

    @pl.when(i == 0)
    def _():
        start_rows(gather_row, 0, 0)

    @pl.when(is_first)
    def _():
        wait_gather(slot)
        base = pl.multiple_of(slot * (tme * TOKEN_ROWS), tme * TOKEN_ROWS)
        x1 = jnp.concatenate([xbuf[pl.ds(base + j, tme, stride=TOKEN_ROWS), :] for j in range(TOKEN_ROWS)], axis=1)
        acc[...] = x1
        hbuf[...] = _rms(x1, nffn[...]).astype(BF16)
        rt = rt_ref[0]
        rcol[...] = jnp.transpose(jnp.concatenate([rt, jnp.zeros((LANES - rt.shape[0], tme), F32)], axis=0))

    valid = ivalid[i] == 1
    has2 = ihas2[i] == 1
    do_gather = jnp.logical_and(is_first, t + 1 < ntiles)
    do_scatter = jnp.logical_and(is_first, t >= 1)
    half = tme // 2
    scatter_rows = lambda r0, r1: (lambda: start_rows_inline(scatter_row, t - 1, 1 - slot, r0, r1))
    expert_with(ie0[i], valid, [(do_gather, lambda: start_rows_inline(gather_row, t + 1, 1 - slot))])
    expert_with(ie1[i], valid, [(jnp.logical_and(do_scatter, has2), scatter_rows(0, half)),
                                (jnp.logical_and(do_scatter, jnp.logical_not(has2)), scatter_rows(0, tme))])
    expert_with(ie2[i], has2, [(jnp.logical_and(do_scatter, has2), scatter_rows(half, tme))])
    for e in range(EXPERTS_PER_GROUP):
        @pl.when(flags[i * EXPERTS_PER_GROUP + e] == 1)
        def _():
            expert(e)

    @pl.when(ilast[i] == 1)
    def _():
        @pl.when(t >= 2)
        def _():
            wait_scatter(slot)

        ybuf[slot] = _rms(acc[...], nfin[...])

        @pl.when(t == ntiles - 1)
        def _():
            start_rows(scatter_row, t, slot)

    @pl.when(i == pl.num_programs(0) - 1)
    def _():
        for sl in range(min(2, ntiles)):
            wait_scatter(sl)


def _moe(x1t, rt, weg, weu, wed, nffn, nfin):
    n = x1t.shape[0] // TOKEN_ROWS
    tme = min(MOE_TILE if n >= MOE_SMALL_CALL_TOKENS else MOE_TILE // 2, n)
    assert n % tme == 0
    ntiles = n // tme
    plan, rt_sorted = _moe_plan(rt, tme)
    max_items = ntiles + N_GROUPS - 1
    grp = lambda i, order, itile, igroup, *_: (igroup[i], 0, 0)
    til = lambda i, order, itile, *_: (itile[i], 0, 0)
    const = lambda i, *_: (0, 0)
    grid_spec = pltpu.PrefetchScalarGridSpec(
        num_scalar_prefetch=len(plan),
        grid=(max_items,),
        in_specs=[pl.BlockSpec(memory_space=pl.ANY),
                  pl.BlockSpec((1,) + rt_sorted.shape[1:], til),
                  pl.BlockSpec((EXPERTS_PER_GROUP, D_MODEL, EXPERT_FF), grp),
                  pl.BlockSpec((EXPERTS_PER_GROUP, D_MODEL, EXPERT_FF), grp),
                  pl.BlockSpec((EXPERTS_PER_GROUP, EXPERT_FF, D_MODEL), grp),
                  pl.BlockSpec((1, D_MODEL), const), pl.BlockSpec((1, D_MODEL), const)],
        out_specs=pl.BlockSpec(memory_space=pl.ANY),
        scratch_shapes=[pltpu.VMEM((2 * tme * TOKEN_ROWS, LANES), F32), pltpu.VMEM((2, tme, D_MODEL), F32),
                        pltpu.VMEM((tme, D_MODEL), F32), pltpu.VMEM((tme, D_MODEL), BF16),
                        pltpu.VMEM((tme, LANES), F32),
                        pltpu.SemaphoreType.DMA((2,)), pltpu.SemaphoreType.DMA((2,))],
    )
    return pl.pallas_call(
        functools.partial(_moe_kernel, tme=tme, ntiles=ntiles),
        grid_spec=grid_spec,
        out_shape=jax.ShapeDtypeStruct((n, D_MODEL), F32),
        compiler_params=_cparams(("arbitrary",)),
        name="moe",
    )(*plan, x1t, rt_sorted, weg, weu, wed, nffn, nfin)


def _rope_tables(positions):
    half = HEAD_DIM // 2
    inv_freq = ROPE_THETA ** (-jnp.arange(half, dtype=F32) / half)
    ang = positions.astype(F32)[:, None] * inv_freq[None, :]
    cos, sin = jnp.cos(ang), jnp.sin(ang)
    reps = LANES // HEAD_DIM
    return (jnp.tile(jnp.concatenate([cos, cos], -1), (1, reps)),
            jnp.tile(jnp.concatenate([-sin, sin], -1), (1, reps)))


def _prep_weights(norm_mix, w_in, w_gla_f2, b_gla_f, gla_norm, w_proj_attn, w_proj_gla, w_out, norm_ffn,
                  w_router_group, b_router_group, w_router_expert, b_router_expert):
    w = {"win": w_in.astype(BF16)}
    w["wf2"] = jnp.pad(w_gla_f2.astype(BF16), ((0, LANES - GLA_GATE_RANK), (0, 0)))
    w["bf"] = b_gla_f.reshape(1, -1)
    w["norm_mix"] = norm_mix.reshape(1, -1)
    w["gla_norm"] = gla_norm.reshape(1, -1)
    w["wpa"] = w_proj_attn.astype(BF16)
    w["wpg"] = w_proj_gla.astype(BF16)
    w["wo"] = w_out.astype(BF16)
    w["nf"] = norm_ffn.reshape(1, -1)
    pad = LANES - N_EXPERTS - N_GROUPS
    wr_t = jnp.pad(jnp.concatenate([w_router_expert, w_router_group], axis=1), ((0, 0), (0, pad))).T
    w["wr_hi"], w["wr_lo"] = _split_bf16(wr_t)
    w["br"] = jnp.broadcast_to(jnp.pad(jnp.concatenate([b_router_expert, b_router_group]), (0, pad))[:, None],
                               (LANES, LANES))
    return w


def _layer(x, positions_tab, cache, w, sinks, weg, weu, wed, nfin, tm):
    batch, seq, _ = x.shape
    n = batch * seq
    x2d = x.reshape(n, D_MODEL)
    cos_t, sin_t = positions_tab
    qa, ka, va, qg, kg, vg, rg, la, ga, gb = _inproj(x2d, w["norm_mix"], cos_t, sin_t, w, tm)
    if cache is None:
        a_out = _swa_prompt(sinks, qa, ka, va, batch, seq)
        last = lambda z: z.reshape(batch, seq, KV_WIDTH)[:, seq - WINDOW:].reshape(batch, WINDOW, ATTN_KV_HEADS, HEAD_DIM)
        new_k, new_v = last(ka), last(va)
        g_out, new_s = _gla_prompt(qg, kg, vg, la, rg, w["gla_norm"], batch, seq)
    else:
        cache_k, cache_v, state = cache
        a_out, new_k, new_v = _swa_sample(sinks, qa, ka, va, cache_k.reshape(batch, WINDOW, KV_WIDTH),
                                          cache_v.reshape(batch, WINDOW, KV_WIDTH), batch, seq)
        new_k = new_k.reshape(batch, WINDOW, ATTN_KV_HEADS, HEAD_DIM)
        new_v = new_v.reshape(batch, WINDOW, ATTN_KV_HEADS, HEAD_DIM)
        g_out, new_s = _gla_sample(qg, kg, vg, la, rg, state, w["gla_norm"], batch, seq)
    x1t, rt = _post(x2d, a_out, g_out, ga, gb, w, tm * 2 if x2d.shape[0] % (tm * 2) == 0 else tm)
    y = _moe(x1t, rt, weg, weu, wed, w["nf"], nfin)
    return y.reshape(batch, seq, D_MODEL), new_k, new_v, new_s


def kernel(x_prompt, x_sample, cache_win_k, cache_win_v, state_gla, norm_mix, w_in, w_gla_f2, b_gla_f, gla_norm,
           attn_sinks, w_proj_attn, w_proj_gla, w_out, norm_ffn, w_router_group, b_router_group, w_router_expert,
           b_router_expert, w_exp_gate, w_exp_up, w_exp_down, norm_final):
    assert norm_mix.shape[0] == 1, "single-layer step"
    seq_p = x_prompt.shape[1]
    dec_b, dec_t = x_sample.shape[0], x_sample.shape[1]
    w = _prep_weights(norm_mix[0], w_in[0], w_gla_f2[0], b_gla_f[0], gla_norm[0], w_proj_attn[0], w_proj_gla[0],
                      w_out[0], norm_ffn[0], w_router_group[0], b_router_group[0], w_router_expert[0],
                      b_router_expert[0])
    weg = w_exp_gate[0].astype(BF16)
    weu = w_exp_up[0].astype(BF16)
    wed = w_exp_down[0].astype(BF16)
    nfin = norm_final.reshape(1, -1)
    sinks = attn_sinks[0]
    tab_p = _rope_tables(jnp.arange(seq_p, dtype=jnp.int32))
    pos_s = PAST_LEN + jnp.arange(dec_t, dtype=jnp.int32)
    tab_s = tuple(jnp.tile(t, (dec_b, 1)) for t in _rope_tables(pos_s))
    tm_p = min(512, seq_p)
    tm_s = dec_b * dec_t
    yp, pk, pv, ps = _layer(x_prompt, tab_p, None, w, sinks, weg, weu, wed, nfin, tm_p)
    ys, sk, sv, ss = _layer(x_sample, tab_s, (cache_win_k[0], cache_win_v[0], state_gla[0]), w, sinks, weg, weu, wed,
                            nfin, tm_s)
    return (yp, ys, pk[None], pv[None], ps[None], sk[None], sv[None], ss[None])
```

```python
import functools
import math

import numpy as np
import jax
import jax.numpy as jnp
from jax import lax
from jax.experimental import pallas as pl
from jax.experimental.pallas import tpu as pltpu

F32 = jnp.float32
BF16 = jnp.bfloat16

D_MODEL = 1024
ATTN_HEADS = 8
ATTN_KV_HEADS = 2
GROUP = ATTN_HEADS // ATTN_KV_HEADS
HEAD_DIM = 64
ATTN_WIDTH = ATTN_HEADS * HEAD_DIM
KV_WIDTH = ATTN_KV_HEADS * HEAD_DIM
WINDOW = 128
ROPE_THETA = 10000.0
PAST_LEN = 8192
GLA_HEADS = 4
GLA_KEY_WIDTH = D_MODEL // 2
GLA_VALUE_WIDTH = D_MODEL
GLA_DK = GLA_KEY_WIDTH // GLA_HEADS
GLA_DV = GLA_VALUE_WIDTH // GLA_HEADS
GLA_GATE_RANK = 16
GLA_GATE_NORMALIZER = 16.0
N_GROUPS = 4
EXPERTS_PER_GROUP = 8
N_EXPERTS = N_GROUPS * EXPERTS_PER_GROUP
EXPERT_FF = 256
EPS = 1e-6
LOG2_E = 1.4426950408889634

LANES = 128
GLA_CHUNK = 128
GLA_HEADS_PER_STEP = 4
SWA_BLOCKS_PER_STEP = 4
SAMPLE_SEQ_BLOCK = 16
POST_PART_ROWS = 512
MOE_TILE = 256
TOKEN_ROWS = D_MODEL // LANES
INPROJ_WIDTHS = (ATTN_WIDTH, 2 * KV_WIDTH, GLA_KEY_WIDTH, GLA_KEY_WIDTH, GLA_VALUE_WIDTH, GLA_VALUE_WIDTH)
ROUTER_ROWS = 40
VMEM_LIMIT = 56 * 1024 * 1024


def _cparams(sem):
    return pltpu.CompilerParams(dimension_semantics=sem, vmem_limit_bytes=VMEM_LIMIT)


def _rms(x, g):
    return x * lax.rsqrt(jnp.mean(x * x, axis=-1, keepdims=True) + EPS) * g


def _sigmoid(x):
    return 1.0 / (1.0 + jnp.exp(-x))


def _dot(a, b):
    return jnp.dot(a, b, preferred_element_type=F32)


def _dot_nt(a, b):
    return lax.dot_general(a, b, (((1,), (1,)), ((), ())), preferred_element_type=F32)


def _dot_tn(a, b):
    return lax.dot_general(a, b, (((0,), (0,)), ((), ())), preferred_element_type=F32)


def _split_bf16(x):
    hi = x.astype(BF16)
    lo = (x - hi.astype(F32)).astype(BF16)
    return hi, lo


def _inproj_kernel(x_ref, g_ref, cos_ref, sin_ref, win, wf2, bf,
                   qa_o, ka_o, va_o, qg_o, kg_o, vg_o, rg_o, la_o, ga_o, gb_o, wgate):
    f0 = sum(INPROJ_WIDTHS)

    @pl.when(pl.program_id(0) == 0)
    def _():
        tail = win[:, f0:]
        wgate[...] = tail[:, GLA_GATE_RANK:GLA_GATE_RANK + 2 * D_MODEL]

    tm = x_ref.shape[0]
    halves = [slice(0, tm // 2), slice(tm // 2, tm)] if tm % 16 == 0 else [slice(0, tm)]
    hbs = [_rms(x_ref[r, :], g_ref[...]).astype(BF16) for r in halves]
    cols = np.cumsum((0,) + INPROJ_WIDTHS[:-1])
    wqa, wkva, wqg, wkg, wvg, wrg = (win.at[:, int(c):int(c) + wd] for c, wd in zip(cols, INPROJ_WIDTHS))
    wga = wgate.at[:, :D_MODEL]
    wgb = wgate.at[:, D_MODEL:]
    wf = win.at[:, f0:f0 + LANES]
    lane = lax.broadcasted_iota(jnp.int32, (halves[0].stop - halves[0].start, LANES), 1)
    first_half = (lane % HEAD_DIM) < (HEAD_DIM // 2)

    def rope(t, r):
        swapped = jnp.where(first_half, pltpu.roll(t, LANES - HEAD_DIM // 2, 1), pltpu.roll(t, HEAD_DIM // 2, 1))
        return t * cos_ref[r, :] + swapped * sin_ref[r, :]

    for r, hb in zip(halves, hbs):
        ga_o[r, :] = _sigmoid(_dot(hb, wga[...])).astype(BF16)
    for r, hb in zip(halves, hbs):
        gb_o[r, :] = _sigmoid(_dot(hb, wgb[...])).astype(BF16)
    for r, hb in zip(halves, hbs):
        z = _dot(_dot(hb, wf[...]).astype(BF16), wf2[...]) + bf[...]
        la_o[r, :] = (jnp.minimum(z, 0.0) - jnp.log1p(jnp.exp(-jnp.abs(z)))) * (1.0 / GLA_GATE_NORMALIZER)
    for r, hb in zip(halves, hbs):
        qa = _dot(hb, wqa[...])
        for c in range(ATTN_WIDTH // LANES):
            sl = slice(c * LANES, (c + 1) * LANES)
            qa_o[r, sl] = (rope(qa[:, sl], r) * (HEAD_DIM ** -0.5)).astype(BF16)
    for r, hb in zip(halves, hbs):
        kva = _dot(hb, wkva[...])
        ka_o[r, :] = rope(kva[:, :KV_WIDTH], r)
        va_o[r, :] = kva[:, KV_WIDTH:]
    for r, hb in zip(halves, hbs):
        qg_o[r, :] = (_dot(hb, wqg[...]) * (GLA_DK ** -0.5)).astype(BF16)
    for r, hb in zip(halves, hbs):
        kg_o[r, :] = _dot(hb, wkg[...]).astype(BF16)
    for r, hb in zip(halves, hbs):
        vg_o[r, :] = _dot(hb, wvg[...]).astype(BF16)
    for r, hb in zip(halves, hbs):
        rg_o[r, :] = _dot(hb, wrg[...]).astype(BF16)


def _inproj(x2d, norm_g, cos_t, sin_t, w, tm):
    n = x2d.shape[0]
    ntab = cos_t.shape[0] // tm
    row = lambda i: (i, 0)
    const = lambda i: (0, 0)
    tab = lambda i: (i % ntab, 0)
    wnames = ("win", "wf2", "bf")
    wspecs = [pl.BlockSpec(w[k].shape, const) for k in wnames]
    widths = (ATTN_WIDTH, KV_WIDTH, KV_WIDTH, GLA_KEY_WIDTH, GLA_KEY_WIDTH, GLA_VALUE_WIDTH, GLA_VALUE_WIDTH,
              GLA_KEY_WIDTH, D_MODEL, D_MODEL)
    dtypes = (BF16, F32, F32, BF16, BF16, BF16, BF16, F32, BF16, BF16)
    return pl.pallas_call(
        _inproj_kernel,
        grid=(n // tm,),
        in_specs=[pl.BlockSpec((tm, D_MODEL), row), pl.BlockSpec((1, D_MODEL), const),
                  pl.BlockSpec((tm, LANES), tab), pl.BlockSpec((tm, LANES), tab)] + wspecs,
        out_specs=[pl.BlockSpec((tm, wd), row) for wd in widths],
        out_shape=[jax.ShapeDtypeStruct((n, wd), dt) for wd, dt in zip(widths, dtypes)],
        scratch_shapes=[pltpu.VMEM((D_MODEL, 2 * D_MODEL), BF16)],
        compiler_params=_cparams(("arbitrary",)),
        name="inproj",
    )(x2d, norm_g, cos_t, sin_t, *[w[k] for k in wnames])


def _swa_prompt_kernel(sink_ref, q_ref, kc_ref, kp_ref, vc_ref, vp_ref, o_ref, *, qb):
    n = pl.program_id(1)
    k3 = jnp.concatenate([kp_ref[...], kc_ref[...]], axis=0).astype(BF16)
    v3 = jnp.concatenate([vp_ref[...], vc_ref[...]], axis=0).astype(BF16)
    t = lax.broadcasted_iota(jnp.int32, (WINDOW, 2 * WINDOW), 0)
    j = lax.broadcasted_iota(jnp.int32, (WINDOW, 2 * WINDOW), 1)
    band = (j >= t) & (j <= t + WINDOW)
    zeros = jnp.zeros((2 * WINDOW, HEAD_DIM), BF16)
    ones = jnp.ones((2 * WINDOW, HEAD_DIM), BF16)
    lane = lax.broadcasted_iota(jnp.int32, (WINDOW, LANES), 1)
    chains = [(blk, h) for blk in range(qb) for h in range(ATTN_HEADS)]
    scores = []
    for blk, h in chains:
        kv = h // GROUP
        rows = slice(blk * WINDOW, (blk + 1) * WINDOW)
        keys = slice(blk * WINDOW, (blk + 2) * WINDOW)
        s = _dot_nt(q_ref[rows, h * HEAD_DIM:(h + 1) * HEAD_DIM], k3[keys, kv * HEAD_DIM:(kv + 1) * HEAD_DIM])
        valid = band & ((j >= WINDOW) | (n > 0)) if blk == 0 else band
        scores.append(jnp.where(valid, s, -jnp.inf))
    probs, sink_terms = [], []
    for (blk, h), s in zip(chains, scores):
        m = jnp.maximum(jnp.max(s, axis=-1, keepdims=True), sink_ref[h])
        probs.append(jnp.exp(s - m).astype(BF16))
        sink_terms.append(jnp.exp(sink_ref[h] - m))
    for blk in range(qb):
        rows = slice(blk * WINDOW, (blk + 1) * WINDOW)
        keys = slice(blk * WINDOW, (blk + 2) * WINDOW)
        for kv in range(ATTN_KV_HEADS):
            vv = v3[keys, kv * HEAD_DIM:(kv + 1) * HEAD_DIM]
            vext = (jnp.concatenate([vv, zeros, ones, zeros], axis=1), jnp.concatenate([zeros, vv, zeros, ones], axis=1))
            for pr in range(GROUP // 2):
                h0 = kv * GROUP + pr * 2
                c0 = blk * ATTN_HEADS + h0
                acc = _dot(probs[c0], vext[0]) + _dot(probs[c0 + 1], vext[1])
                l = acc[:, LANES:] + jnp.where(lane < HEAD_DIM, sink_terms[c0], sink_terms[c0 + 1])
                o_ref[rows, h0 * HEAD_DIM:h0 * HEAD_DIM + LANES] = (acc[:, :LANES] / l).astype(BF16)


def _swa_prompt(sinks, qa, ka, va, batch, seq):
    nb = seq // WINDOW
    qb = math.gcd(SWA_BLOCKS_PER_STEP, nb)
    steps = nb // qb
    cur = lambda b, n: (b * steps + n, 0)
    prev = lambda b, n: (b * nb + jnp.maximum(n * qb - 1, 0), 0)
    return pl.pallas_call(
        functools.partial(_swa_prompt_kernel, qb=qb),
        grid=(batch, steps),
        in_specs=[pl.BlockSpec(memory_space=pltpu.SMEM),
                  pl.BlockSpec((qb * WINDOW, ATTN_WIDTH), cur),
                  pl.BlockSpec((qb * WINDOW, KV_WIDTH), cur), pl.BlockSpec((WINDOW, KV_WIDTH), prev),
                  pl.BlockSpec((qb * WINDOW, KV_WIDTH), cur), pl.BlockSpec((WINDOW, KV_WIDTH), prev)],
        out_specs=pl.BlockSpec((qb * WINDOW, ATTN_WIDTH), cur),
        out_shape=jax.ShapeDtypeStruct(qa.shape, BF16),
        compiler_params=_cparams(("parallel", "parallel")),
        name="swa_prompt",
    )(sinks, qa, ka, ka, va, va)


def _swa_sample_kernel(sink_ref, q_ref, kn_ref, vn_ref, kn3_ref, vn3_ref, ck_ref, cv_ref, o_ref, nk_ref, nv_ref,
                       *, t_new):
    sb = ck_ref.shape[0]
    spv = 8 // t_new
    nq = GROUP * 8
    nc = spv * WINDOW
    qi = lax.broadcasted_iota(jnp.int32, (nq, nc), 0) % 8
    ci = lax.broadcasted_iota(jnp.int32, (nq, nc), 1)
    valid_c = (qi // t_new == ci // WINDOW) & (ci % WINDOW >= qi % t_new)
    qn = lax.broadcasted_iota(jnp.int32, (nq, 8), 0) % 8
    cn = lax.broadcasted_iota(jnp.int32, (nq, 8), 1)
    valid_n = (qn // t_new == cn // t_new) & (cn <= qn)
    grow = lax.broadcasted_iota(jnp.int32, (nq, 1), 0) // 8
    chains = [(vr, kv) for vr in range(sb // spv) for kv in range(ATTN_KV_HEADS)]
    scored = []
    for vr, kv in chains:
        r8 = slice(8 * vr, 8 * vr + 8)
        cs = slice(kv * HEAD_DIM, (kv + 1) * HEAD_DIM)
        heads = [kv * GROUP + g for g in range(GROUP)]
        qs = jnp.concatenate([q_ref[r8, h * HEAD_DIM:(h + 1) * HEAD_DIM] for h in heads], axis=0)
        ck = ck_ref[vr * spv:(vr + 1) * spv, :, cs].reshape(nc, HEAD_DIM).astype(BF16)
        kn = kn_ref[r8, cs].astype(BF16)
        scored.append((jnp.where(valid_c, _dot_nt(qs, ck), -jnp.inf), jnp.where(valid_n, _dot_nt(qs, kn), -jnp.inf)))
    soft = []
    for (vr, kv), (s_c, s_n) in zip(chains, scored):
        sink = sink_ref[kv * GROUP]
        for g in range(1, GROUP):
            sink = jnp.where(grow == g, sink_ref[kv * GROUP + g], sink)
        m = jnp.maximum(jnp.maximum(jnp.max(s_c, axis=-1, keepdims=True), jnp.max(s_n, axis=-1, keepdims=True)), sink)
        p_c = jnp.exp(s_c - m)
        p_n = jnp.exp(s_n - m)
        l = jnp.sum(p_c, axis=-1, keepdims=True) + jnp.sum(p_n, axis=-1, keepdims=True) + jnp.exp(sink - m)
        soft.append((p_c.astype(BF16), p_n.astype(BF16), l))
    for (vr, kv), (p_c, p_n, l) in zip(chains, soft):
        r8 = slice(8 * vr, 8 * vr + 8)
        cs = slice(kv * HEAD_DIM, (kv + 1) * HEAD_DIM)
        cv = cv_ref[vr * spv:(vr + 1) * spv, :, cs].reshape(nc, HEAD_DIM).astype(BF16)
        o = (_dot(p_c, cv) + _dot(p_n, vn_ref[r8, cs].astype(BF16))) / l
        for a in range(GROUP // 2):
            pair = jnp.concatenate([o[16 * a:16 * a + 8], o[16 * a + 8:16 * a + 16]], axis=1)
            c0 = (kv * GROUP + 2 * a) * HEAD_DIM
            o_ref[r8, c0:c0 + LANES] = pair.astype(BF16)
    nk_ref[:, 0:WINDOW - t_new, :] = ck_ref[:, t_new:WINDOW, :]
    nk_ref[:, WINDOW - t_new:WINDOW, :] = kn3_ref[...]
    nv_ref[:, 0:WINDOW - t_new, :] = cv_ref[:, t_new:WINDOW, :]
    nv_ref[:, WINDOW - t_new:WINDOW, :] = vn3_ref[...]


def _swa_sample(sinks, qa, ka, va, cache_k, cache_v, batch, t_new):
    sb = SAMPLE_SEQ_BLOCK
    rows = sb * t_new
    r2 = lambda i: (i, 0)
    r3 = lambda i: (i, 0, 0)
    ka3 = ka.reshape(batch, t_new, KV_WIDTH)
    va3 = va.reshape(batch, t_new, KV_WIDTH)
    return pl.pallas_call(
        functools.partial(_swa_sample_kernel, t_new=t_new),
        grid=(batch // sb,),
        in_specs=[pl.BlockSpec(memory_space=pltpu.SMEM),
                  pl.BlockSpec((rows, ATTN_WIDTH), r2),
                  pl.BlockSpec((rows, KV_WIDTH), r2), pl.BlockSpec((rows, KV_WIDTH), r2),
                  pl.BlockSpec((sb, t_new, KV_WIDTH), r3), pl.BlockSpec((sb, t_new, KV_WIDTH), r3),
                  pl.BlockSpec((sb, WINDOW, KV_WIDTH), r3), pl.BlockSpec((sb, WINDOW, KV_WIDTH), r3)],
        out_specs=[pl.BlockSpec((rows, ATTN_WIDTH), r2),
                   pl.BlockSpec((sb, WINDOW, KV_WIDTH), r3), pl.BlockSpec((sb, WINDOW, KV_WIDTH), r3)],
        out_shape=[jax.ShapeDtypeStruct(qa.shape, BF16),
                   jax.ShapeDtypeStruct(cache_k.shape, F32), jax.ShapeDtypeStruct(cache_v.shape, F32)],
        compiler_params=_cparams(("parallel",)),
        name="swa_sample",
    )(sinks, qa, ka, va, ka3, va3, cache_k, cache_v)


def _gla_constants(c, seg, with_rem):
    t = np.arange(c)
    sid = t // seg
    same = sid[:, None] == sid[None, :]
    levels = []
    m = seg // 2
    while m >= 1:
        levels.append(m)
        m //= 2
    mats, roles, masks = [], [], []
    for m in levels:
        blk = t // (2 * m)
        second = (t // m) % 2 == 1
        p = blk * 2 * m + m - 1
        u = t[None, :]
        mq = (u > p[:, None]) & (u <= t[:, None])
        mk = (u > t[:, None]) & (u <= p[:, None])
        if m > 1:
            mats.append(np.where(second[:, None], mq, mk))
        roles.append(np.broadcast_to(second[:, None], (c, LANES)))
        masks.append((blk[:, None] == blk[None, :]) & second[:, None] & ~second[None, :])
    masks.append(np.eye(c, dtype=bool))
    mats.append(same & (t[None, :] <= t[:, None]))
    if with_rem:
        mats.append(same & (t[None, :] > t[:, None]))
    mall = np.concatenate(mats, 0).astype(np.float32)
    mall = jnp.asarray(np.concatenate([mall, mall], 1), BF16)
    role = jnp.asarray(np.concatenate(roles, 0).astype(np.float32))
    mask = jnp.asarray(np.concatenate(masks, 0).astype(np.float32))
    return len(levels), mall, role, mask


def _gla_exponents(la, mall):
    la2 = la * LOG2_E
    hl = jnp.concatenate(_split_bf16(la2), axis=0)
    return _dot(mall, hl), hl, la2


def _gla_scores(qb, kb, e2, la2, role_ref, mask_ref, nlev, c):
    qf = qb.astype(F32)
    kf = kb.astype(F32)
    terms = []
    for lv in range(nlev):
        sl = slice(lv * c, (lv + 1) * c)
        m = 1 << (nlev - 1 - lv)
        if m % 8 == 0:
            pe = jnp.exp2(e2[sl])
            x = jnp.concatenate([(qf if blk % 2 else kf)[blk * m:(blk + 1) * m] * pe[blk * m:(blk + 1) * m]
                                 for blk in range(c // m)], axis=0).astype(BF16)
        else:
            second = role_ref[sl, :] > 0.5
            e = e2[sl] if lv < nlev - 1 else jnp.where(second, la2, 0.0)
            x = (jnp.where(second, qf, kf) * jnp.exp2(e)).astype(BF16)
        terms.append((x, x, lv))
    terms.append((qb, kb, nlev))
    mask = lambda i: mask_ref[i * c:(i + 1) * c, :]
    a = None
    pair = c % LANES == 0
    while terms:
        if pair and len(terms) >= 2:
            (l0, r0, i0), (l1, r1, i1) = terms.pop(), terms.pop()
            z = jnp.zeros_like(r0)
            rhs = jnp.concatenate([jnp.concatenate([r0, z], axis=1), jnp.concatenate([z, r1], axis=1)], axis=0)
            g = _dot_nt(jnp.concatenate([l0, l1], axis=1), rhs)
            t = mask(i0) * g[:, :c] + mask(i1) * g[:, c:]
        else:
            l0, r0, i0 = terms.pop()
            t = mask(i0) * _dot_nt(l0, r0)
        a = t if a is None else a + t
    return a, qf, kf


def _gla_out(o, g, r):
    r = r.astype(F32)
    return (_rms(o, g) * (r * _sigmoid(r))).astype(BF16)


def _gla_prompt_kernel(q_ref, k_ref, v_ref, la_ref, r_ref, mall_ref, role_ref, mask_ref, g_ref, o_ref, s_ref,
                       s_scr, *, nlev, nchunks):
    c = GLA_CHUNK
    hp = GLA_HEADS_PER_STEP
    s_scr[...] = jnp.zeros_like(s_scr)

    def chunk(i, carry):
        rows = pl.ds(pl.multiple_of(i * c, c), c)
        e2_all, _, la2_all = _gla_exponents(la_ref[rows, :], mall_ref[...])
        ksl = [slice(h * GLA_DK, (h + 1) * GLA_DK) for h in range(hp)]
        vsl = [slice(h * GLA_DV, (h + 1) * GLA_DV) for h in range(hp)]
        scores = [_gla_scores(q_ref[rows, ksl[h]], k_ref[rows, ksl[h]], e2_all[:, ksl[h]], la2_all[:, ksl[h]], role_ref,
                              mask_ref, nlev, c)[0] for h in range(hp)]
        for h in range(hp):
            b = e2_all[(nlev - 1) * c:nlev * c, ksl[h]]
            qf = q_ref[rows, ksl[h]].astype(F32)
            o = (_dot(scores[h].astype(BF16), v_ref[rows, vsl[h]])
                 + _dot((qf * jnp.exp2(b)).astype(BF16), s_scr[h].astype(BF16)))
            o_ref[rows, vsl[h]] = _gla_out(o, g_ref[...], r_ref[rows, vsl[h]])
        for h in range(hp):
            b = e2_all[(nlev - 1) * c:nlev * c, ksl[h]]
            kf = k_ref[rows, ksl[h]].astype(F32)
            kt = (kf * jnp.exp2(b[c - 1:c, :] - b)).astype(BF16)
            dec = jnp.exp2(jnp.transpose(b[c - 8:c, :]))[:, 7:8]
            s_scr[h] = dec * s_scr[h] + _dot_tn(kt, v_ref[rows, vsl[h]])
        return carry

    lax.fori_loop(0, nchunks, chunk, 0)
    s_ref[0] = s_scr[...]


def _gla_prompt(qg, kg, vg, la, rg, gnorm, batch, seq):
    nlev, mall, role, mask = _gla_constants(GLA_CHUNK, GLA_CHUNK, with_rem=False)
    hp = GLA_HEADS_PER_STEP
    bh = lambda b, h: (b, h)
    const = lambda b, h: (0, 0)
    return pl.pallas_call(
        functools.partial(_gla_prompt_kernel, nlev=nlev, nchunks=seq // GLA_CHUNK),
        grid=(batch, GLA_HEADS // hp),
        in_specs=[pl.BlockSpec((seq, hp * GLA_DK), bh), pl.BlockSpec((seq, hp * GLA_DK), bh),
                  pl.BlockSpec((seq, hp * GLA_DV), bh), pl.BlockSpec((seq, hp * GLA_DK), bh),
                  pl.BlockSpec((seq, hp * GLA_DV), bh),
                  pl.BlockSpec(mall.shape, const), pl.BlockSpec(role.shape, const), pl.BlockSpec(mask.shape, const),
                  pl.BlockSpec((1, GLA_DV), const)],
        out_specs=[pl.BlockSpec((seq, hp * GLA_DV), bh),
                   pl.BlockSpec((1, hp, GLA_DK, GLA_DV), lambda b, h: (b, h, 0, 0))],
        out_shape=[jax.ShapeDtypeStruct(vg.shape, BF16),
                   jax.ShapeDtypeStruct((batch, GLA_HEADS, GLA_DK, GLA_DV), F32)],
        scratch_shapes=[pltpu.VMEM((hp, GLA_DK, GLA_DV), F32)],
        compiler_params=_cparams(("parallel", "parallel")),
        name="gla_prompt",
    )(qg, kg, vg, la, rg, mall, role, mask, gnorm)


def _gla_sample_kernel(q_ref, k_ref, v_ref, la_ref, r_ref, s0_ref, mall_ref, role_ref, mask_ref, msum_ref, g_ref,
                       o_ref, s_ref, *, nlev, t_new):
    sb = s0_ref.shape[0]
    c = sb * t_new
    spv = 8 // t_new
    e2_all, hl_all, la2_all = _gla_exponents(la_ref[...], mall_ref[...])
    seq_in_tile = lax.broadcasted_iota(jnp.int32, (8, GLA_DV), 0) // t_new
    seq_of_row = lax.broadcasted_iota(jnp.int32, (c, GLA_DV), 0) // t_new
    ksl = [slice(h * GLA_DK, (h + 1) * GLA_DK) for h in range(GLA_HEADS)]
    scores = [_gla_scores(q_ref[:, ksl[h]], k_ref[:, ksl[h]], e2_all[:, ksl[h]], la2_all[:, ksl[h]], role_ref, mask_ref,
                          nlev, c) for h in range(GLA_HEADS)]
    for h in range(GLA_HEADS):
        ks = ksl[h]
        vs = slice(h * GLA_DV, (h + 1) * GLA_DV)
        v = v_ref[:, vs]
        e2 = e2_all[:, ks]
        hl = hl_all[:, ks]
        a, qf, kf = scores[h]
        qe = (qf * jnp.exp2(e2[(nlev - 1) * c:nlev * c])).astype(BF16)
        kt_t = jnp.transpose(kf * jnp.exp2(e2[nlev * c:(nlev + 1) * c])).astype(BF16)
        bl_t = jnp.transpose(_dot(msum_ref[...], hl[:c]) + _dot(msum_ref[...], hl[c:]))
        dec_t = jnp.exp2(bl_t)
        inter = []
        for vr in range(c // 8):
            rows8 = qe[8 * vr:8 * vr + 8]
            tile = None
            for u in range(spv):
                j = vr * spv + u
                s0 = s0_ref[j, h]
                r = _dot(rows8, s0.astype(BF16))
                tile = r if tile is None else jnp.where(seq_in_tile == u, r, tile)
                vj = jnp.where(seq_of_row == j, v, jnp.zeros_like(v))
                s_ref[j, h] = dec_t[:, j:j + 1] * s0 + _dot(kt_t, vj)
            inter.append(tile)
        o = _dot(a.astype(BF16), v) + jnp.concatenate(inter, axis=0)
        o_ref[:, vs] = _gla_out(o, g_ref[...], r_ref[:, vs])


def _gla_sample(qg, kg, vg, la, rg, state, gnorm, batch, t_new):
    sb = SAMPLE_SEQ_BLOCK
    c = sb * t_new
    assert 8 % t_new == 0 and c % 8 == 0
    nlev, mall, role, mask = _gla_constants(c, t_new, with_rem=True)
    msum = jnp.asarray((np.arange(c)[None, :] // t_new == np.arange(sb)[:, None]).astype(np.float32), BF16)
    rows = lambda i: (i, 0)
    const = lambda i: (0, 0)
    st = lambda i: (i, 0, 0, 0)
    return pl.pallas_call(
        functools.partial(_gla_sample_kernel, nlev=nlev, t_new=t_new),
        grid=(batch // sb,),
        in_specs=[pl.BlockSpec((c, GLA_KEY_WIDTH), rows), pl.BlockSpec((c, GLA_KEY_WIDTH), rows),
                  pl.BlockSpec((c, GLA_VALUE_WIDTH), rows), pl.BlockSpec((c, GLA_KEY_WIDTH), rows),
                  pl.BlockSpec((c, GLA_VALUE_WIDTH), rows),
                  pl.BlockSpec((sb, GLA_HEADS, GLA_DK, GLA_DV), st),
                  pl.BlockSpec(mall.shape, const), pl.BlockSpec(role.shape, const), pl.BlockSpec(mask.shape, const),
                  pl.BlockSpec(msum.shape, const), pl.BlockSpec((1, GLA_DV), const)],
        out_specs=[pl.BlockSpec((c, GLA_VALUE_WIDTH), rows), pl.BlockSpec((sb, GLA_HEADS, GLA_DK, GLA_DV), st)],
        out_shape=[jax.ShapeDtypeStruct(vg.shape, BF16), jax.ShapeDtypeStruct(state.shape, F32)],
        compiler_params=_cparams(("parallel",)),
        name="gla_sample",
    )(qg, kg, vg, la, rg, state, mall, role, mask, msum, gnorm)


def _post_kernel(x_ref, a_ref, gl_ref, ga_ref, gb_ref, wpa, wpg, wo, nf_ref, wr_hi, wr_lo, br, x1t_o, rt_o):
    tm = x_ref.shape[0]
    part = POST_PART_ROWS if tm % POST_PART_ROWS == 0 else tm
    halves = [slice(r0, r0 + part) for r0 in range(0, tm, part)]
    proj =[(_dot(a_ref[r, :], wpa[...]), _dot(gl_ref[r, :], wpg[...])) for r in halves]
    x1s = []
    for r, (pa, pg) in zip(halves, proj):
        merged = ga_ref[r, :].astype(F32) * pa + gb_ref[r, :].astype(F32) * pg
        x1s.append(x_ref[r, :] + _dot(merged.astype(BF16), wo[...]))
    logits = []
    for r, x1 in zip(halves, x1s):
        rows = r.stop - r.start
        for j in range(TOKEN_ROWS):
            x1t_o[pl.ds(r.start * TOKEN_ROWS + j, rows, stride=TOKEN_ROWS), :] = x1[:, j * LANES:(j + 1) * LANES]
        h_hi, h_lo = _split_bf16(_rms(x1, nf_ref[...]))
        logits.append(_dot_nt(wr_hi[...], h_hi) + _dot_nt(wr_hi[...], h_lo) + _dot_nt(wr_lo[...], h_hi))
    nrow = ROUTER_ROWS
    big = jnp.int32(LANES)
    ninf = -jnp.inf
    for r, lt in zip(halves, logits):
        lt = lt[:nrow] + br[:nrow, 0:1]
        row = lax.broadcasted_iota(jnp.int32, lt.shape, 0)

        def first_max(vals):
            mx = jnp.max(vals, axis=0, keepdims=True)
            return mx, jnp.min(jnp.where(vals == mx, row, big), axis=0, keepdims=True)

        gl = jnp.where((row >= N_EXPERTS) & (row < N_EXPERTS + N_GROUPS), lt, ninf)
        gmax, gidx = first_max(gl)
        p_sel = 1.0 / jnp.sum(jnp.exp(gl - gmax), axis=0, keepdims=True)
        lo = (gidx - N_EXPERTS) * EXPERTS_PER_GROUP
        el = jnp.where((row >= lo) & (row < lo + EXPERTS_PER_GROUP), lt, ninf)
        v1, i1 = first_max(el)
        el2 = jnp.where(row == i1, ninf, el)
        v2, i2 = first_max(el2)
        t = jnp.exp(v2 - v1)
        w1 = p_sel / (1.0 + t)
        w2 = p_sel * t / (1.0 + t)
        row8 = lax.broadcasted_iota(jnp.int32, (rt_o.shape[0], r.stop - r.start), 0)
        pick = lambda k, val, rest: jnp.where(row8 == k, val, rest)
        rt_o[:, r] = pick(0, i1.astype(F32), pick(1, i2.astype(F32), pick(2, w1, pick(3, w2, 0.0))))


def _post(x2d, a_out, g_out, ga, gb, w, tm):
    n = x2d.shape[0]
    row = lambda i: (i, 0)
    const = lambda i: (0, 0)
    wnames = ("wpa", "wpg", "wo", "nf", "wr_hi", "wr_lo", "br")
    rt_rows = 8
    return pl.pallas_call(
        _post_kernel,
        grid=(n // tm,),
        in_specs=[pl.BlockSpec((tm, D_MODEL), row), pl.BlockSpec((tm, ATTN_WIDTH), row),
                  pl.BlockSpec((tm, GLA_VALUE_WIDTH), row), pl.BlockSpec((tm, D_MODEL), row),
                  pl.BlockSpec((tm, D_MODEL), row)] + [pl.BlockSpec(w[k].shape, const) for k in wnames],
        out_specs=[pl.BlockSpec((tm * TOKEN_ROWS, LANES), row),
                   pl.BlockSpec((rt_rows, tm), lambda i: (0, i))],
        out_shape=[jax.ShapeDtypeStruct((n * TOKEN_ROWS, LANES), F32), jax.ShapeDtypeStruct((rt_rows, n), F32)],
        compiler_params=_cparams(("parallel",)),
        name="post_mixer",
    )(x2d, a_out, g_out, ga, gb, *[w[k] for k in wnames])


def _moe_plan(rt, tme):
    n = rt.shape[1]
    ntiles = n // tme
    max_items = ntiles + N_GROUPS - 1
    i1, i2 = rt[0].astype(jnp.int32), rt[1].astype(jnp.int32)
    grp = i1 // EXPERTS_PER_GROUP
    lo = jnp.minimum(i1, i2) % EXPERTS_PER_GROUP
    hi = jnp.maximum(i1, i2) % EXPERTS_PER_GROUP
    snake = lambda l, h: jnp.where(l % 2 == 0, h, EXPERTS_PER_GROUP + l - h)
    key = (grp * EXPERTS_PER_GROUP + lo) * EXPERTS_PER_GROUP + snake(lo, hi)
    order = jnp.argsort(key, stable=True).astype(jnp.int32)
    skey = key[order].reshape(ntiles, tme)
    sg = skey // (EXPERTS_PER_GROUP * EXPERTS_PER_GROUP)
    slo = (skey // EXPERTS_PER_GROUP) % EXPERTS_PER_GROUP
    shi = snake(slo, skey % EXPERTS_PER_GROUP)
    ev = jnp.arange(EXPERTS_PER_GROUP)
    in_g = sg[:, :, None] == jnp.arange(N_GROUPS)
    uses_e = (slo[:, :, None] == ev) | (shi[:, :, None] == ev)
    flags_tge = jnp.any(in_g[:, :, :, None] & uses_e[:, :, None, :], axis=1)
    present = jnp.any(in_g, axis=1).reshape(-1)
    pos = jnp.cumsum(present) - 1
    n_items = pos[-1] + 1
    src = jnp.zeros((max_items,), jnp.int32).at[jnp.where(present, pos, max_items)].set(
        jnp.arange(ntiles * N_GROUPS, dtype=jnp.int32), mode="drop")
    it = jnp.arange(max_items)
    valid = it < n_items
    last_src = src[n_items - 1]
    src = jnp.where(valid, src, last_src)
    item_tile = src // N_GROUPS
    item_group = src % N_GROUPS
    prev_tile = jnp.concatenate([jnp.full((1,), -1, jnp.int32), item_tile[:-1]])
    next_tile = jnp.concatenate([item_tile[1:], jnp.full((1,), -1, jnp.int32)])
    first = valid & (item_tile != prev_tile)
    last = valid & ((item_tile != next_tile) | (it == n_items - 1))
    flags = flags_tge[item_tile, item_group] & valid[:, None]
    e0 = jnp.argmax(flags, axis=1)
    rest = flags & (ev[None, :] != e0[:, None])
    e1 = jnp.argmax(rest, axis=1)
    rest = rest & (ev[None, :] != e1[:, None])
    has2 = jnp.any(rest, axis=1)
    e2 = jnp.argmax(rest, axis=1)
    rest = rest & (ev[None, :] != e2[:, None])
    rt_sorted = rt[:, order].reshape(rt.shape[0], ntiles, tme).transpose(1, 0, 2)
    i32 = lambda z: z.astype(jnp.int32)
    plan = (order, i32(item_tile), i32(item_group), i32(first), i32(last), i32(valid), i32(e0), i32(e1), i32(e2),
            i32(has2), i32(rest.reshape(-1)))
    return plan, rt_sorted


def _moe_kernel(order, itile, igroup, ifirst, ilast, ivalid, ie0, ie1, ie2, ihas2, flags, x_hbm, rt_ref, wg, wu, wd, nffn, nfin,
                y_hbm, xbuf, ybuf, acc, hbuf, rcol, gsem, ssem, *, tme, ntiles):
    i = pl.program_id(0)
    t = itile[i]
    slot = t % 2
    g = igroup[i]
    is_first = ifirst[i] == 1

    def gather_row(tile, sl, r):
        tok = order[tile * tme + r]
        src = x_hbm.at[pl.ds(pl.multiple_of(tok * TOKEN_ROWS, TOKEN_ROWS), TOKEN_ROWS)]
        dst = xbuf.at[pl.ds(pl.multiple_of((sl * tme + r) * TOKEN_ROWS, TOKEN_ROWS), TOKEN_ROWS)]
        return pltpu.make_async_copy(src, dst, gsem.at[sl])

    def scatter_row(tile, sl, r):
        tok = order[tile * tme + r]
        return pltpu.make_async_copy(ybuf.at[sl, pl.ds(r, 1)], y_hbm.at[pl.ds(tok, 1)], ssem.at[sl])

    def start_rows(make, tile, sl):
        def body(r, c):
            make(tile, sl, r).start()
            return c
        lax.fori_loop(0, tme, body, 0, unroll=8)

    def wait_gather(sl):
        rows = tme * TOKEN_ROWS
        pltpu.make_async_copy(x_hbm.at[pl.ds(0, rows)], xbuf.at[pl.ds(pl.multiple_of(sl * rows, rows), rows)],
                              gsem.at[sl]).wait()

    def wait_scatter(sl):
        pltpu.make_async_copy(ybuf.at[sl], y_hbm.at[pl.ds(0, tme)], ssem.at[sl]).wait()

    def start_rows_inline(make, tile, sl, r0=0, r1=tme):
        for r in range(r0, r1):
            make(tile, sl, r).start()

    def expert(e):
        eid = (g * EXPERTS_PER_GROUP + e).astype(F32)
        ce = (jnp.where(rcol[:, 0:1] == eid, rcol[:, 2:3], 0.0)
              + jnp.where(rcol[:, 1:2] == eid, rcol[:, 3:4], 0.0))
        h = hbuf[...]
        a = _dot(h, wg[e])
        u = _dot(h, wu[e])
        act = (a * _sigmoid(a)) * u * ce
        acc[...] += _dot(act.astype(BF16), wd[e])

    def expert_with(e, run, alternatives):
        plain = run
        for cond, side_work in alternatives:
            @pl.when(cond)
            def _():
                side_work()
                expert(e)

            plain = jnp.logical_and(plain, jnp.logical_not(cond))

        @pl.when(plain)
        def _():
            expert(e)

    @pl.when(i == 0)
    def _():
        start_rows(gather_row, 0, 0)

    @pl.when(is_first)
    def _():
        wait_gather(slot)
        base = pl.multiple_of(slot * (tme * TOKEN_ROWS), tme * TOKEN_ROWS)
        x1 = jnp.concatenate([xbuf[pl.ds(base + j, tme, stride=TOKEN_ROWS), :] for j in range(TOKEN_ROWS)], axis=1)
        acc[...] = x1
        hbuf[...] = _rms(x1, nffn[...]).astype(BF16)
        rt = rt_ref[0]
        rcol[...] = jnp.transpose(jnp.concatenate([rt, jnp.zeros((LANES - rt.shape[0], tme), F32)], axis=0))

    valid = ivalid[i] == 1
    has2 = ihas2[i] == 1
    do_gather = jnp.logical_and(is_first, t + 1 < ntiles)
    do_scatter = jnp.logical_and(is_first, t >= 1)
    half = tme // 2
    scatter_rows = lambda r0, r1: (lambda: start_rows_inline(scatter_row, t - 1, 1 - slot, r0, r1))
    expert_with(ie0[i], valid, [(do_gather, lambda: start_rows_inline(gather_row, t + 1, 1 - slot))])
    expert_with(ie1[i], valid, [(jnp.logical_and(do_scatter, has2), scatter_rows(0, half)),
                                (jnp.logical_and(do_scatter, jnp.logical_not(has2)), scatter_rows(0, tme))])
    expert_with(ie2[i], has2, [(jnp.logical_and(do_scatter, has2), scatter_rows(half, tme))])
    for e in range(EXPERTS_PER_GROUP):
        @pl.when(flags[i * EXPERTS_PER_GROUP + e] == 1)
        def _():
            expert(e)

    @pl.when(ilast[i] == 1)
    def _():
        @pl.when(t >= 2)
        def _():
            wait_scatter(slot)

        ybuf[slot] = _rms(acc[...], nfin[...])

        @pl.when(t == ntiles - 1)
        def _():
            start_rows(scatter_row, t, slot)

    @pl.when(i == pl.num_programs(0) - 1)
    def _():
        for sl in range(min(2, ntiles)):
            wait_scatter(sl)


def _moe(x1t, rt, weg, weu, wed, nffn, nfin):
    n = x1t.shape[0] // TOKEN_ROWS
    tme = min(MOE_TILE, n)
    assert n % tme == 0
    ntiles = n // tme
    plan, rt_sorted = _moe_plan(rt, tme)
    max_items = ntiles + N_GROUPS - 1
    grp = lambda i, order, itile, igroup, *_: (igroup[i], 0, 0)
    til = lambda i, order, itile, *_: (itile[i], 0, 0)
    const = lambda i, *_: (0, 0)
    grid_spec = pltpu.PrefetchScalarGridSpec(
        num_scalar_prefetch=len(plan),
        grid=(max_items,),
        in_specs=[pl.BlockSpec(memory_space=pl.ANY),
                  pl.BlockSpec((1,) + rt_sorted.shape[1:], til),
                  pl.BlockSpec((EXPERTS_PER_GROUP, D_MODEL, EXPERT_FF), grp),
                  pl.BlockSpec((EXPERTS_PER_GROUP, D_MODEL, EXPERT_FF), grp),
                  pl.BlockSpec((EXPERTS_PER_GROUP, EXPERT_FF, D_MODEL), grp),
                  pl.BlockSpec((1, D_MODEL), const), pl.BlockSpec((1, D_MODEL), const)],
        out_specs=pl.BlockSpec(memory_space=pl.ANY),
        scratch_shapes=[pltpu.VMEM((2 * tme * TOKEN_ROWS, LANES), F32), pltpu.VMEM((2, tme, D_MODEL), F32),
                        pltpu.VMEM((tme, D_MODEL), F32), pltpu.VMEM((tme, D_MODEL), BF16),
                        pltpu.VMEM((tme, LANES), F32),
                        pltpu.SemaphoreType.DMA((2,)), pltpu.SemaphoreType.DMA((2,))],
    )
    return pl.pallas_call(
        functools.partial(_moe_kernel, tme=tme, ntiles=ntiles),
        grid_spec=grid_spec,
        out_shape=jax.ShapeDtypeStruct((n, D_MODEL), F32),
        compiler_params=_cparams(("arbitrary",)),
        name="moe",
    )(*plan, x1t, rt_sorted, weg, weu, wed, nffn, nfin)


def _rope_tables(positions):
    half = HEAD_DIM // 2
    inv_freq = ROPE_THETA ** (-jnp.arange(half, dtype=F32) / half)
    ang = positions.astype(F32)[:, None] * inv_freq[None, :]
    cos, sin = jnp.cos(ang), jnp.sin(ang)
    reps = LANES // HEAD_DIM
    return (jnp.tile(jnp.concatenate([cos, cos], -1), (1, reps)),
            jnp.tile(jnp.concatenate([-sin, sin], -1), (1, reps)))


def _prep_weights(norm_mix, w_in, w_gla_f2, b_gla_f, gla_norm, w_proj_attn, w_proj_gla, w_out, norm_ffn,
                  w_router_group, b_router_group, w_router_expert, b_router_expert):
    w = {"win": w_in.astype(BF16)}
    w["wf2"] = jnp.pad(w_gla_f2.astype(BF16), ((0, LANES - GLA_GATE_RANK), (0, 0)))
    w["bf"] = b_gla_f.reshape(1, -1)
    w["norm_mix"] = norm_mix.reshape(1, -1)
    w["gla_norm"] = gla_norm.reshape(1, -1)
    w["wpa"] = w_proj_attn.astype(BF16)
    w["wpg"] = w_proj_gla.astype(BF16)
    w["wo"] = w_out.astype(BF16)
    w["nf"] = norm_ffn.reshape(1, -1)
    pad = LANES - N_EXPERTS - N_GROUPS
    wr_t = jnp.pad(jnp.concatenate([w_router_expert, w_router_group], axis=1), ((0, 0), (0, pad))).T
    w["wr_hi"], w["wr_lo"] = _split_bf16(wr_t)
    w["br"] = jnp.broadcast_to(jnp.pad(jnp.concatenate([b_router_expert, b_router_group]), (0, pad))[:, None],
                               (LANES, LANES))
    return w


def _layer(x, positions_tab, cache, w, sinks, weg, weu, wed, nfin, tm):
    batch, seq, _ = x.shape
    n = batch * seq
    x2d = x.reshape(n, D_MODEL)
    cos_t, sin_t = positions_tab
    qa, ka, va, qg, kg, vg, rg, la, ga, gb = _inproj(x2d, w["norm_mix"], cos_t, sin_t, w, tm)
    if cache is None:
        a_out = _swa_prompt(sinks, qa, ka, va, batch, seq)
        last = lambda z: z.reshape(batch, seq, KV_WIDTH)[:, seq - WINDOW:].reshape(batch, WINDOW, ATTN_KV_HEADS, HEAD_DIM)
        new_k, new_v = last(ka), last(va)
        g_out, new_s = _gla_prompt(qg, kg, vg, la, rg, w["gla_norm"], batch, seq)
    else:
        cache_k, cache_v, state = cache
        a_out, new_k, new_v = _swa_sample(sinks, qa, ka, va, cache_k.reshape(batch, WINDOW, KV_WIDTH),
                                          cache_v.reshape(batch, WINDOW, KV_WIDTH), batch, seq)
        new_k = new_k.reshape(batch, WINDOW, ATTN_KV_HEADS, HEAD_DIM)
        new_v = new_v.reshape(batch, WINDOW, ATTN_KV_HEADS, HEAD_DIM)
        g_out, new_s = _gla_sample(qg, kg, vg, la, rg, state, w["gla_norm"], batch, seq)
    x1t, rt = _post(x2d, a_out, g_out, ga, gb, w, tm * 2 if x2d.shape[0] % (tm * 2) == 0 else tm)
    y = _moe(x1t, rt, weg, weu, wed, w["nf"], nfin)
    return y.reshape(batch, seq, D_MODEL), new_k, new_v, new_s


def kernel(x_prompt, x_sample, cache_win_k, cache_win_v, state_gla, norm_mix, w_in, w_gla_f2, b_gla_f, gla_norm,
           attn_sinks, w_proj_attn, w_proj_gla, w_out, norm_ffn, w_router_group, b_router_group, w_router_expert,
           b_router_expert, w_exp_gate, w_exp_up, w_exp_down, norm_final):
    assert norm_mix.shape[0] == 1, "single-layer step"
    seq_p = x_prompt.shape[1]
    dec_b, dec_t = x_sample.shape[0], x_sample.shape[1]
    w = _prep_weights(norm_mix[0], w_in[0], w_gla_f2[0], b_gla_f[0], gla_norm[0], w_proj_attn[0], w_proj_gla[0],
                      w_out[0], norm_ffn[0], w_router_group[0], b_router_group[0], w_router_expert[0],
                      b_router_expert[0])
    weg = w_exp_gate[0].astype(BF16)
    weu = w_exp_up[0].astype(BF16)
    wed = w_exp_down[0].astype(BF16)
    nfin = norm_final.reshape(1, -1)
    sinks = attn_sinks[0]
    tab_p = _rope_tables(jnp.arange(seq_p, dtype=jnp.int32))
    pos_s = PAST_LEN + jnp.arange(dec_t, dtype=jnp.int32)
    tab_s = tuple(jnp.tile(t, (dec_b, 1)) for t in _rope_tables(pos_s))
    tm_p = min(512, seq_p)
    tm_s = dec_b * dec_t
    yp, pk, pv, ps = _layer(x_prompt, tab_p, None, w, sinks, weg, weu, wed, nfin, tm_p)
    ys, sk, sv, ss = _layer(x_sample, tab_s, (cache_win_k[0], cache_win_v[0], state_gla[0]), w, sinks, weg, weu, wed,
                            nfin, tm_s)
    return (yp, ys, pk[None], pv[None], ps[None], sk[None], sv[None], ss[None])
```

```python
import functools
import math

import numpy as np
import jax
import jax.numpy as jnp
from jax import lax
from jax.experimental import pallas as pl
from jax.experimental.pallas import tpu as pltpu

F32 = jnp.float32
BF16 = jnp.bfloat16

D_MODEL = 1024
ATTN_HEADS = 8
ATTN_KV_HEADS = 2
GROUP = ATTN_HEADS // ATTN_KV_HEADS
HEAD_DIM = 64
ATTN_WIDTH = ATTN_HEADS * HEAD_DIM
KV_WIDTH = ATTN_KV_HEADS * HEAD_DIM
WINDOW = 128
ROPE_THETA = 10000.0
PAST_LEN = 8192
GLA_HEADS = 4
GLA_KEY_WIDTH = D_MODEL // 2
GLA_VALUE_WIDTH = D_MODEL
GLA_DK = GLA_KEY_WIDTH // GLA_HEADS
GLA_DV = GLA_VALUE_WIDTH // GLA_HEADS
GLA_GATE_RANK = 16
GLA_GATE_NORMALIZER = 16.0
N_GROUPS = 4
EXPERTS_PER_GROUP = 8
N_EXPERTS = N_GROUPS * EXPERTS_PER_GROUP
EXPERT_FF = 256
EPS = 1e-6
LOG2_E = 1.4426950408889634

LANES = 128
GLA_CHUNK = 128
GLA_HEADS_PER_STEP = 4
SWA_BLOCKS_PER_STEP = 4
SAMPLE_SEQ_BLOCK = 16
POST_PART_ROWS = 512
MOE_TILE = 256
TOKEN_ROWS = D_MODEL // LANES
INPROJ_WIDTHS = (ATTN_WIDTH, 2 * KV_WIDTH, GLA_KEY_WIDTH, GLA_KEY_WIDTH, GLA_VALUE_WIDTH, GLA_VALUE_WIDTH)
ROUTER_ROWS = 40
VMEM_LIMIT = 56 * 1024 * 1024


def _cparams(sem):
    return pltpu.CompilerParams(dimension_semantics=sem, vmem_limit_bytes=VMEM_LIMIT)


def _rms(x, g):
    return x * lax.rsqrt(jnp.mean(x * x, axis=-1, keepdims=True) + EPS) * g


def _sigmoid(x):
    return 1.0 / (1.0 + jnp.exp(-x))


def _dot(a, b):
    return jnp.dot(a, b, preferred_element_type=F32)


def _dot_nt(a, b):
    return lax.dot_general(a, b, (((1,), (1,)), ((), ())), preferred_element_type=F32)


def _dot_tn(a, b):
    return lax.dot_general(a, b, (((0,), (0,)), ((), ())), preferred_element_type=F32)


def _split_bf16(x):
    hi = x.astype(BF16)
    lo = (x - hi.astype(F32)).astype(BF16)
    return hi, lo


def _inproj_kernel(x_ref, g_ref, cos_ref, sin_ref, win, wf2, bf,
                   qa_o, ka_o, va_o, qg_o, kg_o, vg_o, rg_o, la_o, ga_o, gb_o, wgate):
    f0 = sum(INPROJ_WIDTHS)

    @pl.when(pl.program_id(0) == 0)
    def _():
        tail = win[:, f0:]
        wgate[...] = tail[:, GLA_GATE_RANK:GLA_GATE_RANK + 2 * D_MODEL]

    tm = x_ref.shape[0]
    halves = [slice(0, tm // 2), slice(tm // 2, tm)] if tm % 16 == 0 else [slice(0, tm)]
    hbs = [_rms(x_ref[r, :], g_ref[...]).astype(BF16) for r in halves]
    cols = np.cumsum((0,) + INPROJ_WIDTHS[:-1])
    wqa, wkva, wqg, wkg, wvg, wrg = (win.at[:, int(c):int(c) + wd] for c, wd in zip(cols, INPROJ_WIDTHS))
    wga = wgate.at[:, :D_MODEL]
    wgb = wgate.at[:, D_MODEL:]
    wf = win.at[:, f0:f0 + LANES]
    lane = lax.broadcasted_iota(jnp.int32, (halves[0].stop - halves[0].start, LANES), 1)
    first_half = (lane % HEAD_DIM) < (HEAD_DIM // 2)

    def rope(t, r):
        swapped = jnp.where(first_half, pltpu.roll(t, LANES - HEAD_DIM // 2, 1), pltpu.roll(t, HEAD_DIM // 2, 1))
        return t * cos_ref[r, :] + swapped * sin_ref[r, :]

    for r, hb in zip(halves, hbs):
        ga_o[r, :] = _sigmoid(_dot(hb, wga[...])).astype(BF16)
    for r, hb in zip(halves, hbs):
        gb_o[r, :] = _sigmoid(_dot(hb, wgb[...])).astype(BF16)
    for r, hb in zip(halves, hbs):
        z = _dot(_dot(hb, wf[...]).astype(BF16), wf2[...]) + bf[...]
        la_o[r, :] = (jnp.minimum(z, 0.0) - jnp.log1p(jnp.exp(-jnp.abs(z)))) * (1.0 / GLA_GATE_NORMALIZER)
    for r, hb in zip(halves, hbs):
        qa = _dot(hb, wqa[...])
        for c in range(ATTN_WIDTH // LANES):
            sl = slice(c * LANES, (c + 1) * LANES)
            qa_o[r, sl] = (rope(qa[:, sl], r) * (HEAD_DIM ** -0.5)).astype(BF16)
    for r, hb in zip(halves, hbs):
        kva = _dot(hb, wkva[...])
        ka_o[r, :] = rope(kva[:, :KV_WIDTH], r)
        va_o[r, :] = kva[:, KV_WIDTH:]
    for r, hb in zip(halves, hbs):
        qg_o[r, :] = (_dot(hb, wqg[...]) * (GLA_DK ** -0.5)).astype(BF16)
    for r, hb in zip(halves, hbs):
        kg_o[r, :] = _dot(hb, wkg[...]).astype(BF16)
    for r, hb in zip(halves, hbs):
        vg_o[r, :] = _dot(hb, wvg[...]).astype(BF16)
    for r, hb in zip(halves, hbs):
        rg_o[r, :] = _dot(hb, wrg[...]).astype(BF16)


def _inproj(x2d, norm_g, cos_t, sin_t, w, tm):
    n = x2d.shape[0]
    ntab = cos_t.shape[0] // tm
    row = lambda i: (i, 0)
    const = lambda i: (0, 0)
    tab = lambda i: (i % ntab, 0)
    wnames = ("win", "wf2", "bf")
    wspecs = [pl.BlockSpec(w[k].shape, const) for k in wnames]
    widths = (ATTN_WIDTH, KV_WIDTH, KV_WIDTH, GLA_KEY_WIDTH, GLA_KEY_WIDTH, GLA_VALUE_WIDTH, GLA_VALUE_WIDTH,
              GLA_KEY_WIDTH, D_MODEL, D_MODEL)
    dtypes = (BF16, F32, F32, BF16, BF16, BF16, BF16, F32, BF16, BF16)
    return pl.pallas_call(
        _inproj_kernel,
        grid=(n // tm,),
        in_specs=[pl.BlockSpec((tm, D_MODEL), row), pl.BlockSpec((1, D_MODEL), const),
                  pl.BlockSpec((tm, LANES), tab), pl.BlockSpec((tm, LANES), tab)] + wspecs,
        out_specs=[pl.BlockSpec((tm, wd), row) for wd in widths],
        out_shape=[jax.ShapeDtypeStruct((n, wd), dt) for wd, dt in zip(widths, dtypes)],
        scratch_shapes=[pltpu.VMEM((D_MODEL, 2 * D_MODEL), BF16)],
        compiler_params=_cparams(("arbitrary",)),
        name="inproj",
    )(x2d, norm_g, cos_t, sin_t, *[w[k] for k in wnames])


def _swa_prompt_kernel(sink_ref, q_ref, kc_ref, kp_ref, vc_ref, vp_ref, o_ref, *, qb):
    n = pl.program_id(1)
    k3 = jnp.concatenate([kp_ref[...], kc_ref[...]], axis=0).astype(BF16)
    v3 = jnp.concatenate([vp_ref[...], vc_ref[...]], axis=0).astype(BF16)
    t = lax.broadcasted_iota(jnp.int32, (WINDOW, 2 * WINDOW), 0)
    j = lax.broadcasted_iota(jnp.int32, (WINDOW, 2 * WINDOW), 1)
    band = (j >= t) & (j <= t + WINDOW)
    zeros = jnp.zeros((2 * WINDOW, HEAD_DIM), BF16)
    ones = jnp.ones((2 * WINDOW, HEAD_DIM), BF16)
    lane = lax.broadcasted_iota(jnp.int32, (WINDOW, LANES), 1)
    chains = [(blk, h) for blk in range(qb) for h in range(ATTN_HEADS)]
    scores = []
    for blk, h in chains:
        kv = h // GROUP
        rows = slice(blk * WINDOW, (blk + 1) * WINDOW)
        keys = slice(blk * WINDOW, (blk + 2) * WINDOW)
        s = _dot_nt(q_ref[rows, h * HEAD_DIM:(h + 1) * HEAD_DIM], k3[keys, kv * HEAD_DIM:(kv + 1) * HEAD_DIM])
        valid = band & ((j >= WINDOW) | (n > 0)) if blk == 0 else band
        scores.append(jnp.where(valid, s, -jnp.inf))
    probs, sink_terms = [], []
    for (blk, h), s in zip(chains, scores):
        m = jnp.maximum(jnp.max(s, axis=-1, keepdims=True), sink_ref[h])
        probs.append(jnp.exp(s - m).astype(BF16))
        sink_terms.append(jnp.exp(sink_ref[h] - m))
    for blk in range(qb):
        rows = slice(blk * WINDOW, (blk + 1) * WINDOW)
        keys = slice(blk * WINDOW, (blk + 2) * WINDOW)
        for kv in range(ATTN_KV_HEADS):
            vv = v3[keys, kv * HEAD_DIM:(kv + 1) * HEAD_DIM]
            vext = (jnp.concatenate([vv, zeros, ones, zeros], axis=1), jnp.concatenate([zeros, vv, zeros, ones], axis=1))
            for pr in range(GROUP // 2):
                h0 = kv * GROUP + pr * 2
                c0 = blk * ATTN_HEADS + h0
                acc = _dot(probs[c0], vext[0]) + _dot(probs[c0 + 1], vext[1])
                l = acc[:, LANES:] + jnp.where(lane < HEAD_DIM, sink_terms[c0], sink_terms[c0 + 1])
                o_ref[rows, h0 * HEAD_DIM:h0 * HEAD_DIM + LANES] = (acc[:, :LANES] / l).astype(BF16)


def _swa_prompt(sinks, qa, ka, va, batch, seq):
    nb = seq // WINDOW
    qb = math.gcd(SWA_BLOCKS_PER_STEP, nb)
    steps = nb // qb
    cur = lambda b, n: (b * steps + n, 0)
    prev = lambda b, n: (b * nb + jnp.maximum(n * qb - 1, 0), 0)
    return pl.pallas_call(
        functools.partial(_swa_prompt_kernel, qb=qb),
        grid=(batch, steps),
        in_specs=[pl.BlockSpec(memory_space=pltpu.SMEM),
                  pl.BlockSpec((qb * WINDOW, ATTN_WIDTH), cur),
                  pl.BlockSpec((qb * WINDOW, KV_WIDTH), cur), pl.BlockSpec((WINDOW, KV_WIDTH), prev),
                  pl.BlockSpec((qb * WINDOW, KV_WIDTH), cur), pl.BlockSpec((WINDOW, KV_WIDTH), prev)],
        out_specs=pl.BlockSpec((qb * WINDOW, ATTN_WIDTH), cur),
        out_shape=jax.ShapeDtypeStruct(qa.shape, BF16),
        compiler_params=_cparams(("parallel", "parallel")),
        name="swa_prompt",
    )(sinks, qa, ka, ka, va, va)


def _swa_sample_kernel(sink_ref, q_ref, kn_ref, vn_ref, kn3_ref, vn3_ref, ck_ref, cv_ref, o_ref, nk_ref, nv_ref,
                       *, t_new):
    sb = ck_ref.shape[0]
    spv = 8 // t_new
    nq = GROUP * 8
    nc = spv * WINDOW
    qi = lax.broadcasted_iota(jnp.int32, (nq, nc), 0) % 8
    ci = lax.broadcasted_iota(jnp.int32, (nq, nc), 1)
    valid_c = (qi // t_new == ci // WINDOW) & (ci % WINDOW >= qi % t_new)
    qn = lax.broadcasted_iota(jnp.int32, (nq, 8), 0) % 8
    cn = lax.broadcasted_iota(jnp.int32, (nq, 8), 1)
    valid_n = (qn // t_new == cn // t_new) & (cn <= qn)
    grow = lax.broadcasted_iota(jnp.int32, (nq, 1), 0) // 8
    chains = [(vr, kv) for vr in range(sb // spv) for kv in range(ATTN_KV_HEADS)]
    scored = []
    for vr, kv in chains:
        r8 = slice(8 * vr, 8 * vr + 8)
        cs = slice(kv * HEAD_DIM, (kv + 1) * HEAD_DIM)
        heads = [kv * GROUP + g for g in range(GROUP)]
        qs = jnp.concatenate([q_ref[r8, h * HEAD_DIM:(h + 1) * HEAD_DIM] for h in heads], axis=0)
        ck = ck_ref[vr * spv:(vr + 1) * spv, :, cs].reshape(nc, HEAD_DIM).astype(BF16)
        kn = kn_ref[r8, cs].astype(BF16)
        scored.append((jnp.where(valid_c, _dot_nt(qs, ck), -jnp.inf), jnp.where(valid_n, _dot_nt(qs, kn), -jnp.inf)))
    soft = []
    for (vr, kv), (s_c, s_n) in zip(chains, scored):
        sink = sink_ref[kv * GROUP]
        for g in range(1, GROUP):
            sink = jnp.where(grow == g, sink_ref[kv * GROUP + g], sink)
        m = jnp.maximum(jnp.maximum(jnp.max(s_c, axis=-1, keepdims=True), jnp.max(s_n, axis=-1, keepdims=True)), sink)
        p_c = jnp.exp(s_c - m)
        p_n = jnp.exp(s_n - m)
        l = jnp.sum(p_c, axis=-1, keepdims=True) + jnp.sum(p_n, axis=-1, keepdims=True) + jnp.exp(sink - m)
        soft.append((p_c.astype(BF16), p_n.astype(BF16), l))
    for (vr, kv), (p_c, p_n, l) in zip(chains, soft):
        r8 = slice(8 * vr, 8 * vr + 8)
        cs = slice(kv * HEAD_DIM, (kv + 1) * HEAD_DIM)
        cv = cv_ref[vr * spv:(vr + 1) * spv, :, cs].reshape(nc, HEAD_DIM).astype(BF16)
        o = (_dot(p_c, cv) + _dot(p_n, vn_ref[r8, cs].astype(BF16))) / l
        for a in range(GROUP // 2):
            pair = jnp.concatenate([o[16 * a:16 * a + 8], o[16 * a + 8:16 * a + 16]], axis=1)
            c0 = (kv * GROUP + 2 * a) * HEAD_DIM
            o_ref[r8, c0:c0 + LANES] = pair.astype(BF16)
    nk_ref[:, 0:WINDOW - t_new, :] = ck_ref[:, t_new:WINDOW, :]
    nk_ref[:, WINDOW - t_new:WINDOW, :] = kn3_ref[...]
    nv_ref[:, 0:WINDOW - t_new, :] = cv_ref[:, t_new:WINDOW, :]
    nv_ref[:, WINDOW - t_new:WINDOW, :] = vn3_ref[...]


def _swa_sample(sinks, qa, ka, va, cache_k, cache_v, batch, t_new):
    sb = SAMPLE_SEQ_BLOCK
    rows = sb * t_new
    r2 = lambda i: (i, 0)
    r3 = lambda i: (i, 0, 0)
    ka3 = ka.reshape(batch, t_new, KV_WIDTH)
    va3 = va.reshape(batch, t_new, KV_WIDTH)
    return pl.pallas_call(
        functools.partial(_swa_sample_kernel, t_new=t_new),
        grid=(batch // sb,),
        in_specs=[pl.BlockSpec(memory_space=pltpu.SMEM),
                  pl.BlockSpec((rows, ATTN_WIDTH), r2),
                  pl.BlockSpec((rows, KV_WIDTH), r2), pl.BlockSpec((rows, KV_WIDTH), r2),
                  pl.BlockSpec((sb, t_new, KV_WIDTH), r3), pl.BlockSpec((sb, t_new, KV_WIDTH), r3),
                  pl.BlockSpec((sb, WINDOW, KV_WIDTH), r3), pl.BlockSpec((sb, WINDOW, KV_WIDTH), r3)],
        out_specs=[pl.BlockSpec((rows, ATTN_WIDTH), r2),
                   pl.BlockSpec((sb, WINDOW, KV_WIDTH), r3), pl.BlockSpec((sb, WINDOW, KV_WIDTH), r3)],
        out_shape=[jax.ShapeDtypeStruct(qa.shape, BF16),
                   jax.ShapeDtypeStruct(cache_k.shape, F32), jax.ShapeDtypeStruct(cache_v.shape, F32)],
        compiler_params=_cparams(("parallel",)),
        name="swa_sample",
    )(sinks, qa, ka, va, ka3, va3, cache_k, cache_v)


def _gla_constants(c, seg, with_rem):
    t = np.arange(c)
    sid = t // seg
    same = sid[:, None] == sid[None, :]
    levels = []
    m = seg // 2
    while m >= 1:
        levels.append(m)
        m //= 2
    mats, roles, masks = [], [], []
    for m in levels:
        blk = t // (2 * m)
        second = (t // m) % 2 == 1
        p = blk * 2 * m + m - 1
        u = t[None, :]
        mq = (u > p[:, None]) & (u <= t[:, None])
        mk = (u > t[:, None]) & (u <= p[:, None])
        if m > 1:
            mats.append(np.where(second[:, None], mq, mk))
        roles.append(np.broadcast_to(second[:, None], (c, LANES)))
        masks.append((blk[:, None] == blk[None, :]) & second[:, None] & ~second[None, :])
    masks.append(np.eye(c, dtype=bool))
    mats.append(same & (t[None, :] <= t[:, None]))
    if with_rem:
        mats.append(same & (t[None, :] > t[:, None]))
    mall = np.concatenate(mats, 0).astype(np.float32)
    mall = jnp.asarray(np.concatenate([mall, mall], 1), BF16)
    role = jnp.asarray(np.concatenate(roles, 0).astype(np.float32))
    mask = jnp.asarray(np.concatenate(masks, 0).astype(np.float32))
    return len(levels), mall, role, mask


def _gla_exponents(la, mall):
    la2 = la * LOG2_E
    hl = jnp.concatenate(_split_bf16(la2), axis=0)
    return _dot(mall, hl), hl, la2


def _gla_scores(qb, kb, e2, la2, role_ref, mask_ref, nlev, c):
    qf = qb.astype(F32)
    kf = kb.astype(F32)
    terms = []
    for lv in range(nlev):
        sl = slice(lv * c, (lv + 1) * c)
        m = 1 << (nlev - 1 - lv)
        if m % 8 == 0:
            pe = jnp.exp2(e2[sl])
            x = jnp.concatenate([(qf if blk % 2 else kf)[blk * m:(blk + 1) * m] * pe[blk * m:(blk + 1) * m]
                                 for blk in range(c // m)], axis=0).astype(BF16)
        else:
            second = role_ref[sl, :] > 0.5
            e = e2[sl] if lv < nlev - 1 else jnp.where(second, la2, 0.0)
            x = (jnp.where(second, qf, kf) * jnp.exp2(e)).astype(BF16)
        terms.append((x, x, lv))
    terms.append((qb, kb, nlev))
    mask = lambda i: mask_ref[i * c:(i + 1) * c, :]
    a = None
    pair = c % LANES == 0
    while terms:
        if pair and len(terms) >= 2:
            (l0, r0, i0), (l1, r1, i1) = terms.pop(), terms.pop()
            z = jnp.zeros_like(r0)
            rhs = jnp.concatenate([jnp.concatenate([r0, z], axis=1), jnp.concatenate([z, r1], axis=1)], axis=0)
            g = _dot_nt(jnp.concatenate([l0, l1], axis=1), rhs)
            t = mask(i0) * g[:, :c] + mask(i1) * g[:, c:]
        else:
            l0, r0, i0 = terms.pop()
            t = mask(i0) * _dot_nt(l0, r0)
        a = t if a is None else a + t
    return a, qf, kf


def _gla_out(o, g, r):
    r = r.astype(F32)
    return (_rms(o, g) * (r * _sigmoid(r))).astype(BF16)


def _gla_prompt_kernel(q_ref, k_ref, v_ref, la_ref, r_ref, mall_ref, role_ref, mask_ref, g_ref, o_ref, s_ref,
                       s_scr, *, nlev, nchunks):
    c = GLA_CHUNK
    hp = GLA_HEADS_PER_STEP
    s_scr[...] = jnp.zeros_like(s_scr)

    def chunk(i, carry):
        rows = pl.ds(pl.multiple_of(i * c, c), c)
        e2_all, _, la2_all = _gla_exponents(la_ref[rows, :], mall_ref[...])
        ksl = [slice(h * GLA_DK, (h + 1) * GLA_DK) for h in range(hp)]
        vsl = [slice(h * GLA_DV, (h + 1) * GLA_DV) for h in range(hp)]
        scores = [_gla_scores(q_ref[rows, ksl[h]], k_ref[rows, ksl[h]], e2_all[:, ksl[h]], la2_all[:, ksl[h]], role_ref,
                              mask_ref, nlev, c)[0] for h in range(hp)]
        for h in range(hp):
            b = e2_all[(nlev - 1) * c:nlev * c, ksl[h]]
            qf = q_ref[rows, ksl[h]].astype(F32)
            o = (_dot(scores[h].astype(BF16), v_ref[rows, vsl[h]])
                 + _dot((qf * jnp.exp2(b)).astype(BF16), s_scr[h].astype(BF16)))
            o_ref[rows, vsl[h]] = _gla_out(o, g_ref[...], r_ref[rows, vsl[h]])
        for h in range(hp):
            b = e2_all[(nlev - 1) * c:nlev * c, ksl[h]]
            kf = k_ref[rows, ksl[h]].astype(F32)
            kt = (kf * jnp.exp2(b[c - 1:c, :] - b)).astype(BF16)
            dec = jnp.exp2(jnp.transpose(b[c - 8:c, :]))[:, 7:8]
            s_scr[h] = dec * s_scr[h] + _dot_tn(kt, v_ref[rows, vsl[h]])
        return carry

    lax.fori_loop(0, nchunks, chunk, 0, unroll=2)
    s_ref[0] = s_scr[...]


def _gla_prompt(qg, kg, vg, la, rg, gnorm, batch, seq):
    nlev, mall, role, mask = _gla_constants(GLA_CHUNK, GLA_CHUNK, with_rem=False)
    hp = GLA_HEADS_PER_STEP
    bh = lambda b, h: (b, h)
    const = lambda b, h: (0, 0)
    return pl.pallas_call(
        functools.partial(_gla_prompt_kernel, nlev=nlev, nchunks=seq // GLA_CHUNK),
        grid=(batch, GLA_HEADS // hp),
        in_specs=[pl.BlockSpec((seq, hp * GLA_DK), bh), pl.BlockSpec((seq, hp * GLA_DK), bh),
                  pl.BlockSpec((seq, hp * GLA_DV), bh), pl.BlockSpec((seq, hp * GLA_DK), bh),
                  pl.BlockSpec((seq, hp * GLA_DV), bh),
                  pl.BlockSpec(mall.shape, const), pl.BlockSpec(role.shape, const), pl.BlockSpec(mask.shape, const),
                  pl.BlockSpec((1, GLA_DV), const)],
        out_specs=[pl.BlockSpec((seq, hp * GLA_DV), bh),
                   pl.BlockSpec((1, hp, GLA_DK, GLA_DV), lambda b, h: (b, h, 0, 0))],
        out_shape=[jax.ShapeDtypeStruct(vg.shape, BF16),
                   jax.ShapeDtypeStruct((batch, GLA_HEADS, GLA_DK, GLA_DV), F32)],
        scratch_shapes=[pltpu.VMEM((hp, GLA_DK, GLA_DV), F32)],
        compiler_params=_cparams(("parallel", "parallel")),
        name="gla_prompt",
    )(qg, kg, vg, la, rg, mall, role, mask, gnorm)


def _gla_sample_kernel(q_ref, k_ref, v_ref, la_ref, r_ref, s0_ref, mall_ref, role_ref, mask_ref, msum_ref, g_ref,
                       o_ref, s_ref, *, nlev, t_new):
    sb = s0_ref.shape[0]
    c = sb * t_new
    spv = 8 // t_new
    e2_all, hl_all, la2_all = _gla_exponents(la_ref[...], mall_ref[...])
    seq_in_tile = lax.broadcasted_iota(jnp.int32, (8, GLA_DV), 0) // t_new
    seq_of_row = lax.broadcasted_iota(jnp.int32, (c, GLA_DV), 0) // t_new
    ksl = [slice(h * GLA_DK, (h + 1) * GLA_DK) for h in range(GLA_HEADS)]
    scores = [_gla_scores(q_ref[:, ksl[h]], k_ref[:, ksl[h]], e2_all[:, ksl[h]], la2_all[:, ksl[h]], role_ref, mask_ref,
                          nlev, c) for h in range(GLA_HEADS)]
    for h in range(GLA_HEADS):
        ks = ksl[h]
        vs = slice(h * GLA_DV, (h + 1) * GLA_DV)
        v = v_ref[:, vs]
        e2 = e2_all[:, ks]
        hl = hl_all[:, ks]
        a, qf, kf = scores[h]
        qe = (qf * jnp.exp2(e2[(nlev - 1) * c:nlev * c])).astype(BF16)
        kt_t = jnp.transpose(kf * jnp.exp2(e2[nlev * c:(nlev + 1) * c])).astype(BF16)
        bl_t = jnp.transpose(_dot(msum_ref[...], hl[:c]) + _dot(msum_ref[...], hl[c:]))
        dec_t = jnp.exp2(bl_t)
        inter = []
        for vr in range(c // 8):
            rows8 = qe[8 * vr:8 * vr + 8]
            tile = None
            for u in range(spv):
                j = vr * spv + u
                s0 = s0_ref[j, h]
                r = _dot(rows8, s0.astype(BF16))
                tile = r if tile is None else jnp.where(seq_in_tile == u, r, tile)
                vj = jnp.where(seq_of_row == j, v, jnp.zeros_like(v))
                s_ref[j, h] = dec_t[:, j:j + 1] * s0 + _dot(kt_t, vj)
            inter.append(tile)
        o = _dot(a.astype(BF16), v) + jnp.concatenate(inter, axis=0)
        o_ref[:, vs] = _gla_out(o, g_ref[...], r_ref[:, vs])


def _gla_sample(qg, kg, vg, la, rg, state, gnorm, batch, t_new):
    sb = SAMPLE_SEQ_BLOCK
    c = sb * t_new
    assert 8 % t_new == 0 and c % 8 == 0
    nlev, mall, role, mask = _gla_constants(c, t_new, with_rem=True)
    msum = jnp.asarray((np.arange(c)[None, :] // t_new == np.arange(sb)[:, None]).astype(np.float32), BF16)
    rows = lambda i: (i, 0)
    const = lambda i: (0, 0)
    st = lambda i: (i, 0, 0, 0)
    return pl.pallas_call(
        functools.partial(_gla_sample_kernel, nlev=nlev, t_new=t_new),
        grid=(batch // sb,),
        in_specs=[pl.BlockSpec((c, GLA_KEY_WIDTH), rows), pl.BlockSpec((c, GLA_KEY_WIDTH), rows),
                  pl.BlockSpec((c, GLA_VALUE_WIDTH), rows), pl.BlockSpec((c, GLA_KEY_WIDTH), rows),
                  pl.BlockSpec((c, GLA_VALUE_WIDTH), rows),
                  pl.BlockSpec((sb, GLA_HEADS, GLA_DK, GLA_DV), st),
                  pl.BlockSpec(mall.shape, const), pl.BlockSpec(role.shape, const), pl.BlockSpec(mask.shape, const),
                  pl.BlockSpec(msum.shape, const), pl.BlockSpec((1, GLA_DV), const)],
        out_specs=[pl.BlockSpec((c, GLA_VALUE_WIDTH), rows), pl.BlockSpec((sb, GLA_HEADS, GLA_DK, GLA_DV), st)],
        out_shape=[jax.ShapeDtypeStruct(vg.shape, BF16), jax.ShapeDtypeStruct(state.shape, F32)],
        compiler_params=_cparams(("parallel",)),
        name="gla_sample",
    )(qg, kg, vg, la, rg, state, mall, role, mask, msum, gnorm)


def _post_kernel(x_ref, a_ref, gl_ref, ga_ref, gb_ref, wpa, wpg, wo, nf_ref, wr_hi, wr_lo, br, x1t_o, rt_o):
    tm = x_ref.shape[0]
    part = POST_PART_ROWS if tm % POST_PART_ROWS == 0 else tm
    halves = [slice(r0, r0 + part) for r0 in range(0, tm, part)]
    proj =[(_dot(a_ref[r, :], wpa[...]), _dot(gl_ref[r, :], wpg[...])) for r in halves]
    x1s = []
    for r, (pa, pg) in zip(halves, proj):
        merged = ga_ref[r, :].astype(F32) * pa + gb_ref[r, :].astype(F32) * pg
        x1s.append(x_ref[r, :] + _dot(merged.astype(BF16), wo[...]))
    logits = []
    for r, x1 in zip(halves, x1s):
        rows = r.stop - r.start
        for j in range(TOKEN_ROWS):
            x1t_o[pl.ds(r.start * TOKEN_ROWS + j, rows, stride=TOKEN_ROWS), :] = x1[:, j * LANES:(j + 1) * LANES]
        h_hi, h_lo = _split_bf16(_rms(x1, nf_ref[...]))
        logits.append(_dot_nt(wr_hi[...], h_hi) + _dot_nt(wr_hi[...], h_lo) + _dot_nt(wr_lo[...], h_hi))
    nrow = ROUTER_ROWS
    big = jnp.int32(LANES)
    ninf = -jnp.inf
    for r, lt in zip(halves, logits):
        lt = lt[:nrow] + br[:nrow, 0:1]
        row = lax.broadcasted_iota(jnp.int32, lt.shape, 0)

        def first_max(vals):
            mx = jnp.max(vals, axis=0, keepdims=True)
            return mx, jnp.min(jnp.where(vals == mx, row, big), axis=0, keepdims=True)

        gl = jnp.where((row >= N_EXPERTS) & (row < N_EXPERTS + N_GROUPS), lt, ninf)
        gmax, gidx = first_max(gl)
        p_sel = 1.0 / jnp.sum(jnp.exp(gl - gmax), axis=0, keepdims=True)
        lo = (gidx - N_EXPERTS) * EXPERTS_PER_GROUP
        el = jnp.where((row >= lo) & (row < lo + EXPERTS_PER_GROUP), lt, ninf)
        v1, i1 = first_max(el)
        el2 = jnp.where(row == i1, ninf, el)
        v2, i2 = first_max(el2)
        t = jnp.exp(v2 - v1)
        w1 = p_sel / (1.0 + t)
        w2 = p_sel * t / (1.0 + t)
        row8 = lax.broadcasted_iota(jnp.int32, (rt_o.shape[0], r.stop - r.start), 0)
        pick = lambda k, val, rest: jnp.where(row8 == k, val, rest)
        rt_o[:, r] = pick(0, i1.astype(F32), pick(1, i2.astype(F32), pick(2, w1, pick(3, w2, 0.0))))


def _post(x2d, a_out, g_out, ga, gb, w, tm):
    n = x2d.shape[0]
    row = lambda i: (i, 0)
    const = lambda i: (0, 0)
    wnames = ("wpa", "wpg", "wo", "nf", "wr_hi", "wr_lo", "br")
    rt_rows = 8
    return pl.pallas_call(
        _post_kernel,
        grid=(n // tm,),
        in_specs=[pl.BlockSpec((tm, D_MODEL), row), pl.BlockSpec((tm, ATTN_WIDTH), row),
                  pl.BlockSpec((tm, GLA_VALUE_WIDTH), row), pl.BlockSpec((tm, D_MODEL), row),
                  pl.BlockSpec((tm, D_MODEL), row)] + [pl.BlockSpec(w[k].shape, const) for k in wnames],
        out_specs=[pl.BlockSpec((tm * TOKEN_ROWS, LANES), row),
                   pl.BlockSpec((rt_rows, tm), lambda i: (0, i))],
        out_shape=[jax.ShapeDtypeStruct((n * TOKEN_ROWS, LANES), F32), jax.ShapeDtypeStruct((rt_rows, n), F32)],
        compiler_params=_cparams(("parallel",)),
        name="post_mixer",
    )(x2d, a_out, g_out, ga, gb, *[w[k] for k in wnames])


def _moe_plan(rt, tme):
    n = rt.shape[1]
    ntiles = n // tme
    max_items = ntiles + N_GROUPS - 1
    i1, i2 = rt[0].astype(jnp.int32), rt[1].astype(jnp.int32)
    grp = i1 // EXPERTS_PER_GROUP
    lo = jnp.minimum(i1, i2) % EXPERTS_PER_GROUP
    hi = jnp.maximum(i1, i2) % EXPERTS_PER_GROUP
    snake = lambda l, h: jnp.where(l % 2 == 0, h, EXPERTS_PER_GROUP + l - h)
    key = (grp * EXPERTS_PER_GROUP + lo) * EXPERTS_PER_GROUP + snake(lo, hi)
    order = jnp.argsort(key, stable=True).astype(jnp.int32)
    skey = key[order].reshape(ntiles, tme)
    sg = skey // (EXPERTS_PER_GROUP * EXPERTS_PER_GROUP)
    slo = (skey // EXPERTS_PER_GROUP) % EXPERTS_PER_GROUP
    shi = snake(slo, skey % EXPERTS_PER_GROUP)
    ev = jnp.arange(EXPERTS_PER_GROUP)
    in_g = sg[:, :, None] == jnp.arange(N_GROUPS)
    uses_e = (slo[:, :, None] == ev) | (shi[:, :, None] == ev)
    flags_tge = jnp.any(in_g[:, :, :, None] & uses_e[:, :, None, :], axis=1)
    present = jnp.any(in_g, axis=1).reshape(-1)
    pos = jnp.cumsum(present) - 1
    n_items = pos[-1] + 1
    src = jnp.zeros((max_items,), jnp.int32).at[jnp.where(present, pos, max_items)].set(
        jnp.arange(ntiles * N_GROUPS, dtype=jnp.int32), mode="drop")
    it = jnp.arange(max_items)
    valid = it < n_items
    last_src = src[n_items - 1]
    src = jnp.where(valid, src, last_src)
    item_tile = src // N_GROUPS
    item_group = src % N_GROUPS
    prev_tile = jnp.concatenate([jnp.full((1,), -1, jnp.int32), item_tile[:-1]])
    next_tile = jnp.concatenate([item_tile[1:], jnp.full((1,), -1, jnp.int32)])
    first = valid & (item_tile != prev_tile)
    last = valid & ((item_tile != next_tile) | (it == n_items - 1))
    flags = flags_tge[item_tile, item_group] & valid[:, None]
    e0 = jnp.argmax(flags, axis=1)
    rest = flags & (ev[None, :] != e0[:, None])
    e1 = jnp.argmax(rest, axis=1)
    rest = rest & (ev[None, :] != e1[:, None])
    has2 = jnp.any(rest, axis=1)
    e2 = jnp.argmax(rest, axis=1)
    rest = rest & (ev[None, :] != e2[:, None])
    rt_sorted = rt[:, order].reshape(rt.shape[0], ntiles, tme).transpose(1, 0, 2)
    i32 = lambda z: z.astype(jnp.int32)
    plan = (order, i32(item_tile), i32(item_group), i32(first), i32(last), i32(valid), i32(e0), i32(e1), i32(e2),
            i32(has2), i32(rest.reshape(-1)))
    return plan, rt_sorted


def _moe_kernel(order, itile, igroup, ifirst, ilast, ivalid, ie0, ie1, ie2, ihas2, flags, x_hbm, rt_ref, wg, wu, wd, nffn, nfin,
                y_hbm, xbuf, ybuf, acc, hbuf, rcol, gsem, ssem, *, tme, ntiles):
    i = pl.program_id(0)
    t = itile[i]
    slot = t % 2
    g = igroup[i]
    is_first = ifirst[i] == 1

    def gather_row(tile, sl, r):
        tok = order[tile * tme + r]
        src = x_hbm.at[pl.ds(pl.multiple_of(tok * TOKEN_ROWS, TOKEN_ROWS), TOKEN_ROWS)]
        dst = xbuf.at[pl.ds(pl.multiple_of((sl * tme + r) * TOKEN_ROWS, TOKEN_ROWS), TOKEN_ROWS)]
        return pltpu.make_async_copy(src, dst, gsem.at[sl])

    def scatter_row(tile, sl, r):
        tok = order[tile * tme + r]
        return pltpu.make_async_copy(ybuf.at[sl, pl.ds(r, 1)], y_hbm.at[pl.ds(tok, 1)], ssem.at[sl])

    def start_rows(make, tile, sl):
        def body(r, c):
            make(tile, sl, r).start()
            return c
        lax.fori_loop(0, tme, body, 0, unroll=8)

    def wait_gather(sl):
        rows = tme * TOKEN_ROWS
        pltpu.make_async_copy(x_hbm.at[pl.ds(0, rows)], xbuf.at[pl.ds(pl.multiple_of(sl * rows, rows), rows)],
                              gsem.at[sl]).wait()

    def wait_scatter(sl):
        pltpu.make_async_copy(ybuf.at[sl], y_hbm.at[pl.ds(0, tme)], ssem.at[sl]).wait()

    def start_rows_inline(make, tile, sl, r0=0, r1=tme):
        for r in range(r0, r1):
            make(tile, sl, r).start()

    def expert(e):
        eid = (g * EXPERTS_PER_GROUP + e).astype(F32)
        ce = (jnp.where(rcol[:, 0:1] == eid, rcol[:, 2:3], 0.0)
              + jnp.where(rcol[:, 1:2] == eid, rcol[:, 3:4], 0.0))
        h = hbuf[...]
        a = _dot(h, wg[e])
        u = _dot(h, wu[e])
        act = (a * _sigmoid(a)) * u * ce
        acc[...] += _dot(act.astype(BF16), wd[e])

    def expert_with(e, run, alternatives):
        plain = run
        for cond, side_work in alternatives:
            @pl.when(cond)
            def _():
                side_work()
                expert(e)

            plain = jnp.logical_and(plain, jnp.logical_not(cond))

        @pl.when(plain)
        def _():
            expert(e)

    @pl.when(i == 0)
    def _():
        start_rows(gather_row, 0, 0)

    @pl.when(is_first)
    def _():
        wait_gather(slot)
        base = pl.multiple_of(slot * (tme * TOKEN_ROWS), tme * TOKEN_ROWS)
        x1 = jnp.concatenate([xbuf[pl.ds(base + j, tme, stride=TOKEN_ROWS), :] for j in range(TOKEN_ROWS)], axis=1)
        acc[...] = x1
        hbuf[...] = _rms(x1, nffn[...]).astype(BF16)
        rt = rt_ref[0]
        rcol[...] = jnp.transpose(jnp.concatenate([rt, jnp.zeros((LANES - rt.shape[0], tme), F32)], axis=0))

    valid = ivalid[i] == 1
    has2 = ihas2[i] == 1
    do_gather = jnp.logical_and(is_first, t + 1 < ntiles)
    do_scatter = jnp.logical_and(is_first, t >= 1)
    half = tme // 2
    scatter_rows = lambda r0, r1: (lambda: start_rows_inline(scatter_row, t - 1, 1 - slot, r0, r1))
    expert_with(ie0[i], valid, [(do_gather, lambda: start_rows_inline(gather_row, t + 1, 1 - slot))])
    expert_with(ie1[i], valid, [(jnp.logical_and(do_scatter, has2), scatter_rows(0, half)),
                                (jnp.logical_and(do_scatter, jnp.logical_not(has2)), scatter_rows(0, tme))])
    expert_with(ie2[i], has2, [(jnp.logical_and(do_scatter, has2), scatter_rows(half, tme))])
    for e in range(EXPERTS_PER_GROUP):
        @pl.when(flags[i * EXPERTS_PER_GROUP + e] == 1)
        def _():
            expert(e)

    @pl.when(ilast[i] == 1)
    def _():
        @pl.when(t >= 2)
        def _():
            wait_scatter(slot)

        ybuf[slot] = _rms(acc[...], nfin[...])

        @pl.when(t == ntiles - 1)
        def _():
            start_rows(scatter_row, t, slot)

    @pl.when(i == pl.num_programs(0) - 1)
    def _():
        for sl in range(min(2, ntiles)):
            wait_scatter(sl)


def _moe(x1t, rt, weg, weu, wed, nffn, nfin):
    n = x1t.shape[0] // TOKEN_ROWS
    tme = min(MOE_TILE, n)
    assert n % tme == 0
    ntiles = n // tme
    plan, rt_sorted = _moe_plan(rt, tme)
    max_items = ntiles + N_GROUPS - 1
    grp = lambda i, order, itile, igroup, *_: (igroup[i], 0, 0)
    til = lambda i, order, itile, *_: (itile[i], 0, 0)
    const = lambda i, *_: (0, 0)
    grid_spec = pltpu.PrefetchScalarGridSpec(
        num_scalar_prefetch=len(plan),
        grid=(max_items,),
        in_specs=[pl.BlockSpec(memory_space=pl.ANY),
                  pl.BlockSpec((1,) + rt_sorted.shape[1:], til),
                  pl.BlockSpec((EXPERTS_PER_GROUP, D_MODEL, EXPERT_FF), grp),
                  pl.BlockSpec((EXPERTS_PER_GROUP, D_MODEL, EXPERT_FF), grp),
                  pl.BlockSpec((EXPERTS_PER_GROUP, EXPERT_FF, D_MODEL), grp),
                  pl.BlockSpec((1, D_MODEL), const), pl.BlockSpec((1, D_MODEL), const)],
        out_specs=pl.BlockSpec(memory_space=pl.ANY),
        scratch_shapes=[pltpu.VMEM((2 * tme * TOKEN_ROWS, LANES), F32), pltpu.VMEM((2, tme, D_MODEL), F32),
                        pltpu.VMEM((tme, D_MODEL), F32), pltpu.VMEM((tme, D_MODEL), BF16),
                        pltpu.VMEM((tme, LANES), F32),
                        pltpu.SemaphoreType.DMA((2,)), pltpu.SemaphoreType.DMA((2,))],
    )
    return pl.pallas_call(
        functools.partial(_moe_kernel, tme=tme, ntiles=ntiles),
        grid_spec=grid_spec,
        out_shape=jax.ShapeDtypeStruct((n, D_MODEL), F32),
        compiler_params=_cparams(("arbitrary",)),
        name="moe",
    )(*plan, x1t, rt_sorted, weg, weu, wed, nffn, nfin)


def _rope_tables(positions):
    half = HEAD_DIM // 2
    inv_freq = ROPE_THETA ** (-jnp.arange(half, dtype=F32) / half)
    ang = positions.astype(F32)[:, None] * inv_freq[None, :]
    cos, sin = jnp.cos(ang), jnp.sin(ang)
    reps = LANES // HEAD_DIM
    return (jnp.tile(jnp.concatenate([cos, cos], -1), (1, reps)),
            jnp.tile(jnp.concatenate([-sin, sin], -1), (1, reps)))


def _prep_weights(norm_mix, w_in, w_gla_f2, b_gla_f, gla_norm, w_proj_attn, w_proj_gla, w_out, norm_ffn,
                  w_router_group, b_router_group, w_router_expert, b_router_expert):
    w = {"win": w_in.astype(BF16)}
    w["wf2"] = jnp.pad(w_gla_f2.astype(BF16), ((0, LANES - GLA_GATE_RANK), (0, 0)))
    w["bf"] = b_gla_f.reshape(1, -1)
    w["norm_mix"] = norm_mix.reshape(1, -1)
    w["gla_norm"] = gla_norm.reshape(1, -1)
    w["wpa"] = w_proj_attn.astype(BF16)
    w["wpg"] = w_proj_gla.astype(BF16)
    w["wo"] = w_out.astype(BF16)
    w["nf"] = norm_ffn.reshape(1, -1)
    pad = LANES - N_EXPERTS - N_GROUPS
    wr_t = jnp.pad(jnp.concatenate([w_router_expert, w_router_group], axis=1), ((0, 0), (0, pad))).T
    w["wr_hi"], w["wr_lo"] = _split_bf16(wr_t)
    w["br"] = jnp.broadcast_to(jnp.pad(jnp.concatenate([b_router_expert, b_router_group]), (0, pad))[:, None],
                               (LANES, LANES))
    return w


def _layer(x, positions_tab, cache, w, sinks, weg, weu, wed, nfin, tm):
    batch, seq, _ = x.shape
    n = batch * seq
    x2d = x.reshape(n, D_MODEL)
    cos_t, sin_t = positions_tab
    qa, ka, va, qg, kg, vg, rg, la, ga, gb = _inproj(x2d, w["norm_mix"], cos_t, sin_t, w, tm)
    if cache is None:
        a_out = _swa_prompt(sinks, qa, ka, va, batch, seq)
        last = lambda z: z.reshape(batch, seq, KV_WIDTH)[:, seq - WINDOW:].reshape(batch, WINDOW, ATTN_KV_HEADS, HEAD_DIM)
        new_k, new_v = last(ka), last(va)
        g_out, new_s = _gla_prompt(qg, kg, vg, la, rg, w["gla_norm"], batch, seq)
    else:
        cache_k, cache_v, state = cache
        a_out, new_k, new_v = _swa_sample(sinks, qa, ka, va, cache_k.reshape(batch, WINDOW, KV_WIDTH),
                                          cache_v.reshape(batch, WINDOW, KV_WIDTH), batch, seq)
        new_k = new_k.reshape(batch, WINDOW, ATTN_KV_HEADS, HEAD_DIM)
        new_v = new_v.reshape(batch, WINDOW, ATTN_KV_HEADS, HEAD_DIM)
        g_out, new_s = _gla_sample(qg, kg, vg, la, rg, state, w["gla_norm"], batch, seq)
    x1t, rt = _post(x2d, a_out, g_out, ga, gb, w, tm * 2 if x2d.shape[0] % (tm * 2) == 0 else tm)
    y = _moe(x1t, rt, weg, weu, wed, w["nf"], nfin)
    return y.reshape(batch, seq, D_MODEL), new_k, new_v, new_s


def kernel(x_prompt, x_sample, cache_win_k, cache_win_v, state_gla, norm_mix, w_in, w_gla_f2, b_gla_f, gla_norm,
           attn_sinks, w_proj_attn, w_proj_gla, w_out, norm_ffn, w_router_group, b_router_group, w_router_expert,
           b_router_expert, w_exp_gate, w_exp_up, w_exp_down, norm_final):
    assert norm_mix.shape[0] == 1, "single-layer step"
    seq_p = x_prompt.shape[1]
    dec_b, dec_t = x_sample.shape[0], x_sample.shape[1]
    w = _prep_weights(norm_mix[0], w_in[0], w_gla_f2[0], b_gla_f[0], gla_norm[0], w_proj_attn[0], w_proj_gla[0],
                      w_out[0], norm_ffn[0], w_router_group[0], b_router_group[0], w_router_expert[0],
                      b_router_expert[0])
    weg = w_exp_gate[0].astype(BF16)
    weu = w_exp_up[0].astype(BF16)
    wed = w_exp_down[0].astype(BF16)
    nfin = norm_final.reshape(1, -1)
    sinks = attn_sinks[0]
    tab_p = _rope_tables(jnp.arange(seq_p, dtype=jnp.int32))
    pos_s = PAST_LEN + jnp.arange(dec_t, dtype=jnp.int32)
    tab_s = tuple(jnp.tile(t, (dec_b, 1)) for t in _rope_tables(pos_s))
    tm_p = min(512, seq_p)
    tm_s = dec_b * dec_t
    yp, pk, pv, ps = _layer(x_prompt, tab_p, None, w, sinks, weg, weu, wed, nfin, tm_p)
    ys, sk, sv, ss = _layer(x_sample, tab_s, (cache_win_k[0], cache_win_v[0], state_gla[0]), w, sinks, weg, weu, wed,
                            nfin, tm_s)
    return (yp, ys, pk[None], pv[None], ps[None], sk[None], sv[None], ss[None])
```

```python
import functools
import math

import numpy as np
import jax
import jax.numpy as jnp
from jax import lax
from jax.experimental import pallas as pl
from jax.experimental.pallas import tpu as pltpu

F32 = jnp.float32
BF16 = jnp.bfloat16

D_MODEL = 1024
ATTN_HEADS = 8
ATTN_KV_HEADS = 2
GROUP = ATTN_HEADS // ATTN_KV_HEADS
HEAD_DIM = 64
ATTN_WIDTH = ATTN_HEADS * HEAD_DIM
KV_WIDTH = ATTN_KV_HEADS * HEAD_DIM
WINDOW = 128
ROPE_THETA = 10000.0
PAST_LEN = 8192
GLA_HEADS = 4
GLA_KEY_WIDTH = D_MODEL // 2
GLA_VALUE_WIDTH = D_MODEL
GLA_DK = GLA_KEY_WIDTH // GLA_HEADS
GLA_DV = GLA_VALUE_WIDTH // GLA_HEADS
GLA_GATE_RANK = 16
GLA_GATE_NORMALIZER = 16.0
N_GROUPS = 4
EXPERTS_PER_GROUP = 8
N_EXPERTS = N_GROUPS * EXPERTS_PER_GROUP
EXPERT_FF = 256
EPS = 1e-6
LOG2_E = 1.4426950408889634

LANES = 128
GLA_CHUNK = 128
GLA_HEADS_PER_STEP = 4
SWA_BLOCKS_PER_STEP = 4
SAMPLE_SEQ_BLOCK = 16
POST_PART_ROWS = 512
MOE_TILE = 256
TOKEN_ROWS = D_MODEL // LANES
INPROJ_WIDTHS = (ATTN_WIDTH, 2 * KV_WIDTH, GLA_KEY_WIDTH, GLA_KEY_WIDTH, GLA_VALUE_WIDTH, GLA_VALUE_WIDTH)
ROUTER_ROWS = 40
VMEM_LIMIT = 56 * 1024 * 1024


def _cparams(sem):
    return pltpu.CompilerParams(dimension_semantics=sem, vmem_limit_bytes=VMEM_LIMIT)


def _rms(x, g):
    return x * lax.rsqrt(jnp.mean(x * x, axis=-1, keepdims=True) + EPS) * g


def _sigmoid(x):
    return 1.0 / (1.0 + jnp.exp(-x))


def _dot(a, b):
    return jnp.dot(a, b, preferred_element_type=F32)


def _dot_nt(a, b):
    return lax.dot_general(a, b, (((1,), (1,)), ((), ())), preferred_element_type=F32)


def _dot_tn(a, b):
    return lax.dot_general(a, b, (((0,), (0,)), ((), ())), preferred_element_type=F32)


def _split_bf16(x):
    hi = x.astype(BF16)
    lo = (x - hi.astype(F32)).astype(BF16)
    return hi, lo


def _inproj_kernel(x_ref, g_ref, cos_ref, sin_ref, win, wf2, bf,
                   qa_o, ka_o, va_o, qg_o, kg_o, vg_o, rg_o, la_o, ga_o, gb_o, wgate):
    f0 = sum(INPROJ_WIDTHS)

    @pl.when(pl.program_id(0) == 0)
    def _():
        tail = win[:, f0:]
        wgate[...] = tail[:, GLA_GATE_RANK:GLA_GATE_RANK + 2 * D_MODEL]

    tm = x_ref.shape[0]
    halves = [slice(0, tm // 2), slice(tm // 2, tm)] if tm % 16 == 0 else [slice(0, tm)]
    hbs = [_rms(x_ref[r, :], g_ref[...]).astype(BF16) for r in halves]
    cols = np.cumsum((0,) + INPROJ_WIDTHS[:-1])
    wqa, wkva, wqg, wkg, wvg, wrg = (win.at[:, int(c):int(c) + wd] for c, wd in zip(cols, INPROJ_WIDTHS))
    wga = wgate.at[:, :D_MODEL]
    wgb = wgate.at[:, D_MODEL:]
    wf = win.at[:, f0:f0 + LANES]
    lane = lax.broadcasted_iota(jnp.int32, (halves[0].stop - halves[0].start, LANES), 1)
    first_half = (lane % HEAD_DIM) < (HEAD_DIM // 2)

    def rope(t, r):
        swapped = jnp.where(first_half, pltpu.roll(t, LANES - HEAD_DIM // 2, 1), pltpu.roll(t, HEAD_DIM // 2, 1))
        return t * cos_ref[r, :] + swapped * sin_ref[r, :]

    for r, hb in zip(halves, hbs):
        ga_o[r, :] = _sigmoid(_dot(hb, wga[...])).astype(BF16)
    for r, hb in zip(halves, hbs):
        gb_o[r, :] = _sigmoid(_dot(hb, wgb[...])).astype(BF16)
    for r, hb in zip(halves, hbs):
        z = _dot(_dot(hb, wf[...]).astype(BF16), wf2[...]) + bf[...]
        la_o[r, :] = (jnp.minimum(z, 0.0) - jnp.log1p(jnp.exp(-jnp.abs(z)))) * (1.0 / GLA_GATE_NORMALIZER)
    for r, hb in zip(halves, hbs):
        qa = _dot(hb, wqa[...])
        for c in range(ATTN_WIDTH // LANES):
            sl = slice(c * LANES, (c + 1) * LANES)
            qa_o[r, sl] = (rope(qa[:, sl], r) * (HEAD_DIM ** -0.5)).astype(BF16)
    for r, hb in zip(halves, hbs):
        kva = _dot(hb, wkva[...])
        ka_o[r, :] = rope(kva[:, :KV_WIDTH], r)
        va_o[r, :] = kva[:, KV_WIDTH:]
    for r, hb in zip(halves, hbs):
        qg_o[r, :] = (_dot(hb, wqg[...]) * (GLA_DK ** -0.5)).astype(BF16)
    for r, hb in zip(halves, hbs):
        kg_o[r, :] = _dot(hb, wkg[...]).astype(BF16)
    for r, hb in zip(halves, hbs):
        vg_o[r, :] = _dot(hb, wvg[...]).astype(BF16)
    for r, hb in zip(halves, hbs):
        rg_o[r, :] = _dot(hb, wrg[...]).astype(BF16)


def _inproj(x2d, norm_g, cos_t, sin_t, w, tm):
    n = x2d.shape[0]
    ntab = cos_t.shape[0] // tm
    row = lambda i: (i, 0)
    const = lambda i: (0, 0)
    tab = lambda i: (i % ntab, 0)
    wnames = ("win", "wf2", "bf")
    wspecs = [pl.BlockSpec(w[k].shape, const) for k in wnames]
    widths = (ATTN_WIDTH, KV_WIDTH, KV_WIDTH, GLA_KEY_WIDTH, GLA_KEY_WIDTH, GLA_VALUE_WIDTH, GLA_VALUE_WIDTH,
              GLA_KEY_WIDTH, D_MODEL, D_MODEL)
    dtypes = (BF16, F32, F32, BF16, BF16, BF16, BF16, F32, BF16, BF16)
    return pl.pallas_call(
        _inproj_kernel,
        grid=(n // tm,),
        in_specs=[pl.BlockSpec((tm, D_MODEL), row), pl.BlockSpec((1, D_MODEL), const),
                  pl.BlockSpec((tm, LANES), tab), pl.BlockSpec((tm, LANES), tab)] + wspecs,
        out_specs=[pl.BlockSpec((tm, wd), row) for wd in widths],
        out_shape=[jax.ShapeDtypeStruct((n, wd), dt) for wd, dt in zip(widths, dtypes)],
        scratch_shapes=[pltpu.VMEM((D_MODEL, 2 * D_MODEL), BF16)],
        compiler_params=_cparams(("arbitrary",)),
        name="inproj",
    )(x2d, norm_g, cos_t, sin_t, *[w[k] for k in wnames])


def _swa_prompt_kernel(sink_ref, q_ref, kc_ref, kp_ref, vc_ref, vp_ref, o_ref, *, qb):
    n = pl.program_id(1)
    k3 = jnp.concatenate([kp_ref[...], kc_ref[...]], axis=0).astype(BF16)
    v3 = jnp.concatenate([vp_ref[...], vc_ref[...]], axis=0).astype(BF16)
    t = lax.broadcasted_iota(jnp.int32, (WINDOW, 2 * WINDOW), 0)
    j = lax.broadcasted_iota(jnp.int32, (WINDOW, 2 * WINDOW), 1)
    band = (j >= t) & (j <= t + WINDOW)
    zeros = jnp.zeros((2 * WINDOW, HEAD_DIM), BF16)
    ones = jnp.ones((2 * WINDOW, HEAD_DIM), BF16)
    lane = lax.broadcasted_iota(jnp.int32, (WINDOW, LANES), 1)
    chains = [(blk, h) for blk in range(qb) for h in range(ATTN_HEADS)]
    scores = []
    for blk, h in chains:
        kv = h // GROUP
        rows = slice(blk * WINDOW, (blk + 1) * WINDOW)
        keys = slice(blk * WINDOW, (blk + 2) * WINDOW)
        s = _dot_nt(q_ref[rows, h * HEAD_DIM:(h + 1) * HEAD_DIM], k3[keys, kv * HEAD_DIM:(kv + 1) * HEAD_DIM])
        valid = band & ((j >= WINDOW) | (n > 0)) if blk == 0 else band
        scores.append(jnp.where(valid, s, -jnp.inf))
    probs, sink_terms = [], []
    for (blk, h), s in zip(chains, scores):
        m = jnp.maximum(jnp.max(s, axis=-1, keepdims=True), sink_ref[h])
        probs.append(jnp.exp(s - m).astype(BF16))
        sink_terms.append(jnp.exp(sink_ref[h] - m))
    for blk in range(qb):
        rows = slice(blk * WINDOW, (blk + 1) * WINDOW)
        keys = slice(blk * WINDOW, (blk + 2) * WINDOW)
        for kv in range(ATTN_KV_HEADS):
            vv = v3[keys, kv * HEAD_DIM:(kv + 1) * HEAD_DIM]
            vext = (jnp.concatenate([vv, zeros, ones, zeros], axis=1), jnp.concatenate([zeros, vv, zeros, ones], axis=1))
            for pr in range(GROUP // 2):
                h0 = kv * GROUP + pr * 2
                c0 = blk * ATTN_HEADS + h0
                acc = _dot(probs[c0], vext[0]) + _dot(probs[c0 + 1], vext[1])
                l = acc[:, LANES:] + jnp.where(lane < HEAD_DIM, sink_terms[c0], sink_terms[c0 + 1])
                o_ref[rows, h0 * HEAD_DIM:h0 * HEAD_DIM + LANES] = (acc[:, :LANES] / l).astype(BF16)


def _swa_prompt(sinks, qa, ka, va, batch, seq):
    nb = seq // WINDOW
    qb = math.gcd(SWA_BLOCKS_PER_STEP, nb)
    steps = nb // qb
    cur = lambda b, n: (b * steps + n, 0)
    prev = lambda b, n: (b * nb + jnp.maximum(n * qb - 1, 0), 0)
    return pl.pallas_call(
        functools.partial(_swa_prompt_kernel, qb=qb),
        grid=(batch, steps),
        in_specs=[pl.BlockSpec(memory_space=pltpu.SMEM),
                  pl.BlockSpec((qb * WINDOW, ATTN_WIDTH), cur),
                  pl.BlockSpec((qb * WINDOW, KV_WIDTH), cur), pl.BlockSpec((WINDOW, KV_WIDTH), prev),
                  pl.BlockSpec((qb * WINDOW, KV_WIDTH), cur), pl.BlockSpec((WINDOW, KV_WIDTH), prev)],
        out_specs=pl.BlockSpec((qb * WINDOW, ATTN_WIDTH), cur),
        out_shape=jax.ShapeDtypeStruct(qa.shape, BF16),
        compiler_params=_cparams(("parallel", "parallel")),
        name="swa_prompt",
    )(sinks, qa, ka, ka, va, va)


def _swa_sample_kernel(sink_ref, q_ref, kn_ref, vn_ref, kn3_ref, vn3_ref, ck_ref, cv_ref, o_ref, nk_ref, nv_ref,
                       *, t_new):
    sb = ck_ref.shape[0]
    spv = 8 // t_new
    nq = GROUP * 8
    nc = spv * WINDOW
    qi = lax.broadcasted_iota(jnp.int32, (nq, nc), 0) % 8
    ci = lax.broadcasted_iota(jnp.int32, (nq, nc), 1)
    valid_c = (qi // t_new == ci // WINDOW) & (ci % WINDOW >= qi % t_new)
    qn = lax.broadcasted_iota(jnp.int32, (nq, 8), 0) % 8
    cn = lax.broadcasted_iota(jnp.int32, (nq, 8), 1)
    valid_n = (qn // t_new == cn // t_new) & (cn <= qn)
    grow = lax.broadcasted_iota(jnp.int32, (nq, 1), 0) // 8
    chains = [(vr, kv) for vr in range(sb // spv) for kv in range(ATTN_KV_HEADS)]
    scored = []
    for vr, kv in chains:
        r8 = slice(8 * vr, 8 * vr + 8)
        cs = slice(kv * HEAD_DIM, (kv + 1) * HEAD_DIM)
        heads = [kv * GROUP + g for g in range(GROUP)]
        qs = jnp.concatenate([q_ref[r8, h * HEAD_DIM:(h + 1) * HEAD_DIM] for h in heads], axis=0)
        ck = ck_ref[vr * spv:(vr + 1) * spv, :, cs].reshape(nc, HEAD_DIM).astype(BF16)
        kn = kn_ref[r8, cs].astype(BF16)
        scored.append((jnp.where(valid_c, _dot_nt(qs, ck), -jnp.inf), jnp.where(valid_n, _dot_nt(qs, kn), -jnp.inf)))
    soft = []
    for (vr, kv), (s_c, s_n) in zip(chains, scored):
        sink = sink_ref[kv * GROUP]
        for g in range(1, GROUP):
            sink = jnp.where(grow == g, sink_ref[kv * GROUP + g], sink)
        m = jnp.maximum(jnp.maximum(jnp.max(s_c, axis=-1, keepdims=True), jnp.max(s_n, axis=-1, keepdims=True)), sink)
        p_c = jnp.exp(s_c - m)
        p_n = jnp.exp(s_n - m)
        l = jnp.sum(p_c, axis=-1, keepdims=True) + jnp.sum(p_n, axis=-1, keepdims=True) + jnp.exp(sink - m)
        soft.append((p_c.astype(BF16), p_n.astype(BF16), l))
    for (vr, kv), (p_c, p_n, l) in zip(chains, soft):
        r8 = slice(8 * vr, 8 * vr + 8)
        cs = slice(kv * HEAD_DIM, (kv + 1) * HEAD_DIM)
        cv = cv_ref[vr * spv:(vr + 1) * spv, :, cs].reshape(nc, HEAD_DIM).astype(BF16)
        o = (_dot(p_c, cv) + _dot(p_n, vn_ref[r8, cs].astype(BF16))) / l
        for a in range(GROUP // 2):
            pair = jnp.concatenate([o[16 * a:16 * a + 8], o[16 * a + 8:16 * a + 16]], axis=1)
            c0 = (kv * GROUP + 2 * a) * HEAD_DIM
            o_ref[r8, c0:c0 + LANES] = pair.astype(BF16)
    nk_ref[:, 0:WINDOW - t_new, :] = ck_ref[:, t_new:WINDOW, :]
    nk_ref[:, WINDOW - t_new:WINDOW, :] = kn3_ref[...]
    nv_ref[:, 0:WINDOW - t_new, :] = cv_ref[:, t_new:WINDOW, :]
    nv_ref[:, WINDOW - t_new:WINDOW, :] = vn3_ref[...]


def _swa_sample(sinks, qa, ka, va, cache_k, cache_v, batch, t_new):
    sb = SAMPLE_SEQ_BLOCK
    rows = sb * t_new
    r2 = lambda i: (i, 0)
    r3 = lambda i: (i, 0, 0)
    ka3 = ka.reshape(batch, t_new, KV_WIDTH)
    va3 = va.reshape(batch, t_new, KV_WIDTH)
    return pl.pallas_call(
        functools.partial(_swa_sample_kernel, t_new=t_new),
        grid=(batch // sb,),
        in_specs=[pl.BlockSpec(memory_space=pltpu.SMEM),
                  pl.BlockSpec((rows, ATTN_WIDTH), r2),
                  pl.BlockSpec((rows, KV_WIDTH), r2), pl.BlockSpec((rows, KV_WIDTH), r2),
                  pl.BlockSpec((sb, t_new, KV_WIDTH), r3), pl.BlockSpec((sb, t_new, KV_WIDTH), r3),
                  pl.BlockSpec((sb, WINDOW, KV_WIDTH), r3), pl.BlockSpec((sb, WINDOW, KV_WIDTH), r3)],
        out_specs=[pl.BlockSpec((rows, ATTN_WIDTH), r2),
                   pl.BlockSpec((sb, WINDOW, KV_WIDTH), r3), pl.BlockSpec((sb, WINDOW, KV_WIDTH), r3)],
        out_shape=[jax.ShapeDtypeStruct(qa.shape, BF16),
                   jax.ShapeDtypeStruct(cache_k.shape, F32), jax.ShapeDtypeStruct(cache_v.shape, F32)],
        compiler_params=_cparams(("parallel",)),
        name="swa_sample",
    )(sinks, qa, ka, va, ka3, va3, cache_k, cache_v)


def _gla_constants(c, seg, with_rem):
    t = np.arange(c)
    sid = t // seg
    same = sid[:, None] == sid[None, :]
    levels = []
    m = seg // 2
    while m >= 1:
        levels.append(m)
        m //= 2
    mats, roles, masks = [], [], []
    for m in levels:
        blk = t // (2 * m)
        second = (t // m) % 2 == 1
        p = blk * 2 * m + m - 1
        u = t[None, :]
        mq = (u > p[:, None]) & (u <= t[:, None])
        mk = (u > t[:, None]) & (u <= p[:, None])
        if m > 1:
            mats.append(np.where(second[:, None], mq, mk))
        roles.append(np.broadcast_to(second[:, None], (c, LANES)))
        masks.append((blk[:, None] == blk[None, :]) & second[:, None] & ~second[None, :])
    masks.append(np.eye(c, dtype=bool))
    mats.append(same & (t[None, :] <= t[:, None]))
    if with_rem:
        mats.append(same & (t[None, :] > t[:, None]))
    mall = np.concatenate(mats, 0).astype(np.float32)
    mall = jnp.asarray(np.concatenate([mall, mall], 1), BF16)
    role = jnp.asarray(np.concatenate(roles, 0).astype(np.float32))
    mask = jnp.asarray(np.concatenate(masks, 0).astype(np.float32))
    return len(levels), mall, role, mask


def _gla_exponents(la, mall):
    la2 = la * LOG2_E
    hl = jnp.concatenate(_split_bf16(la2), axis=0)
    return _dot(mall, hl), hl, la2


def _gla_scores(qb, kb, e2, la2, role_ref, mask_ref, nlev, c):
    qf = qb.astype(F32)
    kf = kb.astype(F32)
    terms = []
    for lv in range(nlev):
        sl = slice(lv * c, (lv + 1) * c)
        m = 1 << (nlev - 1 - lv)
        if m % 8 == 0:
            pe = jnp.exp2(e2[sl])
            x = jnp.concatenate([(qf if blk % 2 else kf)[blk * m:(blk + 1) * m] * pe[blk * m:(blk + 1) * m]
                                 for blk in range(c // m)], axis=0).astype(BF16)
        else:
            second = role_ref[sl, :] > 0.5
            e = e2[sl] if lv < nlev - 1 else jnp.where(second, la2, 0.0)
            x = (jnp.where(second, qf, kf) * jnp.exp2(e)).astype(BF16)
        terms.append((x, x, lv))
    terms.append((qb, kb, nlev))
    mask = lambda i: mask_ref[i * c:(i + 1) * c, :]
    a = None
    pair = c % LANES == 0
    while terms:
        if pair and len(terms) >= 2:
            (l0, r0, i0), (l1, r1, i1) = terms.pop(), terms.pop()
            z = jnp.zeros_like(r0)
            rhs = jnp.concatenate([jnp.concatenate([r0, z], axis=1), jnp.concatenate([z, r1], axis=1)], axis=0)
            g = _dot_nt(jnp.concatenate([l0, l1], axis=1), rhs)
            t = mask(i0) * g[:, :c] + mask(i1) * g[:, c:]
        else:
            l0, r0, i0 = terms.pop()
            t = mask(i0) * _dot_nt(l0, r0)
        a = t if a is None else a + t
    return a, qf, kf


def _gla_out(o, g, r):
    r = r.astype(F32)
    return (_rms(o, g) * (r * _sigmoid(r))).astype(BF16)


def _gla_prompt_kernel(q_ref, k_ref, v_ref, la_ref, r_ref, mall_ref, role_ref, mask_ref, g_ref, o_ref, s_ref,
                       s_scr, *, nlev, nchunks):
    c = GLA_CHUNK
    hp = GLA_HEADS_PER_STEP
    s_scr[...] = jnp.zeros_like(s_scr)

    def chunk(i, carry):
        rows = pl.ds(pl.multiple_of(i * c, c), c)
        e2_all, _, la2_all = _gla_exponents(la_ref[rows, :], mall_ref[...])
        ksl = [slice(h * GLA_DK, (h + 1) * GLA_DK) for h in range(hp)]
        vsl = [slice(h * GLA_DV, (h + 1) * GLA_DV) for h in range(hp)]
        scores = [_gla_scores(q_ref[rows, ksl[h]], k_ref[rows, ksl[h]], e2_all[:, ksl[h]], la2_all[:, ksl[h]], role_ref,
                              mask_ref, nlev, c)[0] for h in range(hp)]
        for h in range(hp):
            b = e2_all[(nlev - 1) * c:nlev * c, ksl[h]]
            qf = q_ref[rows, ksl[h]].astype(F32)
            o = (_dot(scores[h].astype(BF16), v_ref[rows, vsl[h]])
                 + _dot((qf * jnp.exp2(b)).astype(BF16), s_scr[h].astype(BF16)))
            o_ref[rows, vsl[h]] = _gla_out(o, g_ref[...], r_ref[rows, vsl[h]])
        for h in range(hp):
            b = e2_all[(nlev - 1) * c:nlev * c, ksl[h]]
            kf = k_ref[rows, ksl[h]].astype(F32)
            kt = (kf * jnp.exp2(b[c - 1:c, :] - b)).astype(BF16)
            dec = jnp.exp2(jnp.transpose(b[c - 8:c, :]))[:, 7:8]
            s_scr[h] = dec * s_scr[h] + _dot_tn(kt, v_ref[rows, vsl[h]])
        return carry

    lax.fori_loop(0, nchunks, chunk, 0, unroll=4)
    s_ref[0] = s_scr[...]


def _gla_prompt(qg, kg, vg, la, rg, gnorm, batch, seq):
    nlev, mall, role, mask = _gla_constants(GLA_CHUNK, GLA_CHUNK, with_rem=False)
    hp = GLA_HEADS_PER_STEP
    bh = lambda b, h: (b, h)
    const = lambda b, h: (0, 0)
    return pl.pallas_call(
        functools.partial(_gla_prompt_kernel, nlev=nlev, nchunks=seq // GLA_CHUNK),
        grid=(batch, GLA_HEADS // hp),
        in_specs=[pl.BlockSpec((seq, hp * GLA_DK), bh), pl.BlockSpec((seq, hp * GLA_DK), bh),
                  pl.BlockSpec((seq, hp * GLA_DV), bh), pl.BlockSpec((seq, hp * GLA_DK), bh),
                  pl.BlockSpec((seq, hp * GLA_DV), bh),
                  pl.BlockSpec(mall.shape, const), pl.BlockSpec(role.shape, const), pl.BlockSpec(mask.shape, const),
                  pl.BlockSpec((1, GLA_DV), const)],
        out_specs=[pl.BlockSpec((seq, hp * GLA_DV), bh),
                   pl.BlockSpec((1, hp, GLA_DK, GLA_DV), lambda b, h: (b, h, 0, 0))],
        out_shape=[jax.ShapeDtypeStruct(vg.shape, BF16),
                   jax.ShapeDtypeStruct((batch, GLA_HEADS, GLA_DK, GLA_DV), F32)],
        scratch_shapes=[pltpu.VMEM((hp, GLA_DK, GLA_DV), F32)],
        compiler_params=_cparams(("parallel", "parallel")),
        name="gla_prompt",
    )(qg, kg, vg, la, rg, mall, role, mask, gnorm)


def _gla_sample_kernel(q_ref, k_ref, v_ref, la_ref, r_ref, s0_ref, mall_ref, role_ref, mask_ref, msum_ref, g_ref,
                       o_ref, s_ref, *, nlev, t_new):
    sb = s0_ref.shape[0]
    c = sb * t_new
    spv = 8 // t_new
    e2_all, hl_all, la2_all = _gla_exponents(la_ref[...], mall_ref[...])
    seq_in_tile = lax.broadcasted_iota(jnp.int32, (8, GLA_DV), 0) // t_new
    seq_of_row = lax.broadcasted_iota(jnp.int32, (c, GLA_DV), 0) // t_new
    ksl = [slice(h * GLA_DK, (h + 1) * GLA_DK) for h in range(GLA_HEADS)]
    scores = [_gla_scores(q_ref[:, ksl[h]], k_ref[:, ksl[h]], e2_all[:, ksl[h]], la2_all[:, ksl[h]], role_ref, mask_ref,
                          nlev, c) for h in range(GLA_HEADS)]
    for h in range(GLA_HEADS):
        ks = ksl[h]
        vs = slice(h * GLA_DV, (h + 1) * GLA_DV)
        v = v_ref[:, vs]
        e2 = e2_all[:, ks]
        hl = hl_all[:, ks]
        a, qf, kf = scores[h]
        qe = (qf * jnp.exp2(e2[(nlev - 1) * c:nlev * c])).astype(BF16)
        kt_t = jnp.transpose(kf * jnp.exp2(e2[nlev * c:(nlev + 1) * c])).astype(BF16)
        bl_t = jnp.transpose(_dot(msum_ref[...], hl[:c]) + _dot(msum_ref[...], hl[c:]))
        dec_t = jnp.exp2(bl_t)
        inter = []
        for vr in range(c // 8):
            rows8 = qe[8 * vr:8 * vr + 8]
            tile = None
            for u in range(spv):
                j = vr * spv + u
                s0 = s0_ref[j, h]
                r = _dot(rows8, s0.astype(BF16))
                tile = r if tile is None else jnp.where(seq_in_tile == u, r, tile)
                vj = jnp.where(seq_of_row == j, v, jnp.zeros_like(v))
                s_ref[j, h] = dec_t[:, j:j + 1] * s0 + _dot(kt_t, vj)
            inter.append(tile)
        o = _dot(a.astype(BF16), v) + jnp.concatenate(inter, axis=0)
        o_ref[:, vs] = _gla_out(o, g_ref[...], r_ref[:, vs])


def _gla_sample(qg, kg, vg, la, rg, state, gnorm, batch, t_new):
    sb = SAMPLE_SEQ_BLOCK
    c = sb * t_new
    assert 8 % t_new == 0 and c % 8 == 0
    nlev, mall, role, mask = _gla_constants(c, t_new, with_rem=True)
    msum = jnp.asarray((np.arange(c)[None, :] // t_new == np.arange(sb)[:, None]).astype(np.float32), BF16)
    rows = lambda i: (i, 0)
    const = lambda i: (0, 0)
    st = lambda i: (i, 0, 0, 0)
    return pl.pallas_call(
        functools.partial(_gla_sample_kernel, nlev=nlev, t_new=t_new),
        grid=(batch // sb,),
        in_specs=[pl.BlockSpec((c, GLA_KEY_WIDTH), rows), pl.BlockSpec((c, GLA_KEY_WIDTH), rows),
                  pl.BlockSpec((c, GLA_VALUE_WIDTH), rows), pl.BlockSpec((c, GLA_KEY_WIDTH), rows),
                  pl.BlockSpec((c, GLA_VALUE_WIDTH), rows),
                  pl.BlockSpec((sb, GLA_HEADS, GLA_DK, GLA_DV), st),
                  pl.BlockSpec(mall.shape, const), pl.BlockSpec(role.shape, const), pl.BlockSpec(mask.shape, const),
                  pl.BlockSpec(msum.shape, const), pl.BlockSpec((1, GLA_DV), const)],
        out_specs=[pl.BlockSpec((c, GLA_VALUE_WIDTH), rows), pl.BlockSpec((sb, GLA_HEADS, GLA_DK, GLA_DV), st)],
        out_shape=[jax.ShapeDtypeStruct(vg.shape, BF16), jax.ShapeDtypeStruct(state.shape, F32)],
        compiler_params=_cparams(("parallel",)),
        name="gla_sample",
    )(qg, kg, vg, la, rg, state, mall, role, mask, msum, gnorm)


def _post_kernel(x_ref, a_ref, gl_ref, ga_ref, gb_ref, wpa, wpg, wo, nf_ref, wr_hi, wr_lo, br, x1t_o, rt_o):
    tm = x_ref.shape[0]
    part = POST_PART_ROWS if tm % POST_PART_ROWS == 0 else tm
    halves = [slice(r0, r0 + part) for r0 in range(0, tm, part)]
    proj =[(_dot(a_ref[r, :], wpa[...]), _dot(gl_ref[r, :], wpg[...])) for r in halves]
    x1s = []
    for r, (pa, pg) in zip(halves, proj):
        merged = ga_ref[r, :].astype(F32) * pa + gb_ref[r, :].astype(F32) * pg
        x1s.append(x_ref[r, :] + _dot(merged.astype(BF16), wo[...]))
    logits = []
    for r, x1 in zip(halves, x1s):
        rows = r.stop - r.start
        for j in range(TOKEN_ROWS):
            x1t_o[pl.ds(r.start * TOKEN_ROWS + j, rows, stride=TOKEN_ROWS), :] = x1[:, j * LANES:(j + 1) * LANES]
        h_hi, h_lo = _split_bf16(_rms(x1, nf_ref[...]))
        logits.append(_dot_nt(wr_hi[...], h_hi) + _dot_nt(wr_hi[...], h_lo) + _dot_nt(wr_lo[...], h_hi))
    nrow = ROUTER_ROWS
    big = jnp.int32(LANES)
    ninf = -jnp.inf
    for r, lt in zip(halves, logits):
        lt = lt[:nrow] + br[:nrow, 0:1]
        row = lax.broadcasted_iota(jnp.int32, lt.shape, 0)

        def first_max(vals):
            mx = jnp.max(vals, axis=0, keepdims=True)
            return mx, jnp.min(jnp.where(vals == mx, row, big), axis=0, keepdims=True)

        gl = jnp.where((row >= N_EXPERTS) & (row < N_EXPERTS + N_GROUPS), lt, ninf)
        gmax, gidx = first_max(gl)
        p_sel = 1.0 / jnp.sum(jnp.exp(gl - gmax), axis=0, keepdims=True)
        lo = (gidx - N_EXPERTS) * EXPERTS_PER_GROUP
        el = jnp.where((row >= lo) & (row < lo + EXPERTS_PER_GROUP), lt, ninf)
        v1, i1 = first_max(el)
        el2 = jnp.where(row == i1, ninf, el)
        v2, i2 = first_max(el2)
        t = jnp.exp(v2 - v1)
        w1 = p_sel / (1.0 + t)
        w2 = p_sel * t / (1.0 + t)
        row8 = lax.broadcasted_iota(jnp.int32, (rt_o.shape[0], r.stop - r.start), 0)
        pick = lambda k, val, rest: jnp.where(row8 == k, val, rest)
        rt_o[:, r] = pick(0, i1.astype(F32), pick(1, i2.astype(F32), pick(2, w1, pick(3, w2, 0.0))))


def _post(x2d, a_out, g_out, ga, gb, w, tm):
    n = x2d.shape[0]
    row = lambda i: (i, 0)
    const = lambda i: (0, 0)
    wnames = ("wpa", "wpg", "wo", "nf", "wr_hi", "wr_lo", "br")
    rt_rows = 8
    return pl.pallas_call(
        _post_kernel,
        grid=(n // tm,),
        in_specs=[pl.BlockSpec((tm, D_MODEL), row), pl.BlockSpec((tm, ATTN_WIDTH), row),
                  pl.BlockSpec((tm, GLA_VALUE_WIDTH), row), pl.BlockSpec((tm, D_MODEL), row),
                  pl.BlockSpec((tm, D_MODEL), row)] + [pl.BlockSpec(w[k].shape, const) for k in wnames],
        out_specs=[pl.BlockSpec((tm * TOKEN_ROWS, LANES), row),
                   pl.BlockSpec((rt_rows, tm), lambda i: (0, i))],
        out_shape=[jax.ShapeDtypeStruct((n * TOKEN_ROWS, LANES), F32), jax.ShapeDtypeStruct((rt_rows, n), F32)],
        compiler_params=_cparams(("parallel",)),
        name="post_mixer",
    )(x2d, a_out, g_out, ga, gb, *[w[k] for k in wnames])


def _moe_plan(rt, tme):
    n = rt.shape[1]
    ntiles = n // tme
    max_items = ntiles + N_GROUPS - 1
    i1, i2 = rt[0].astype(jnp.int32), rt[1].astype(jnp.int32)
    grp = i1 // EXPERTS_PER_GROUP
    lo = jnp.minimum(i1, i2) % EXPERTS_PER_GROUP
    hi = jnp.maximum(i1, i2) % EXPERTS_PER_GROUP
    snake = lambda l, h: jnp.where(l % 2 == 0, h, EXPERTS_PER_GROUP + l - h)
    key = (grp * EXPERTS_PER_GROUP + lo) * EXPERTS_PER_GROUP + snake(lo, hi)
    order = jnp.argsort(key, stable=True).astype(jnp.int32)
    skey = key[order].reshape(ntiles, tme)
    sg = skey // (EXPERTS_PER_GROUP * EXPERTS_PER_GROUP)
    slo = (skey // EXPERTS_PER_GROUP) % EXPERTS_PER_GROUP
    shi = snake(slo, skey % EXPERTS_PER_GROUP)
    ev = jnp.arange(EXPERTS_PER_GROUP)
    in_g = sg[:, :, None] == jnp.arange(N_GROUPS)
    uses_e = (slo[:, :, None] == ev) | (shi[:, :, None] == ev)
    flags_tge = jnp.any(in_g[:, :, :, None] & uses_e[:, :, None, :], axis=1)
    present = jnp.any(in_g, axis=1).reshape(-1)
    pos = jnp.cumsum(present) - 1
    n_items = pos[-1] + 1
    src = jnp.zeros((max_items,), jnp.int32).at[jnp.where(present, pos, max_items)].set(
        jnp.arange(ntiles * N_GROUPS, dtype=jnp.int32), mode="drop")
    it = jnp.arange(max_items)
    valid = it < n_items
    last_src = src[n_items - 1]
    src = jnp.where(valid, src, last_src)
    item_tile = src // N_GROUPS
    item_group = src % N_GROUPS
    prev_tile = jnp.concatenate([jnp.full((1,), -1, jnp.int32), item_tile[:-1]])
    next_tile = jnp.concatenate([item_tile[1:], jnp.full((1,), -1, jnp.int32)])
    first = valid & (item_tile != prev_tile)
    last = valid & ((item_tile != next_tile) | (it == n_items - 1))
    flags = flags_tge[item_tile, item_group] & valid[:, None]
    e0 = jnp.argmax(flags, axis=1)
    rest = flags & (ev[None, :] != e0[:, None])
    e1 = jnp.argmax(rest, axis=1)
    rest = rest & (ev[None, :] != e1[:, None])
    has2 = jnp.any(rest, axis=1)
    e2 = jnp.argmax(rest, axis=1)
    rest = rest & (ev[None, :] != e2[:, None])
    rt_sorted = rt[:, order].reshape(rt.shape[0], ntiles, tme).transpose(1, 0, 2)
    i32 = lambda z: z.astype(jnp.int32)
    plan = (order, i32(item_tile), i32(item_group), i32(first), i32(last), i32(valid), i32(e0), i32(e1), i32(e2),
            i32(has2), i32(rest.reshape(-1)))
    return plan, rt_sorted


def _moe_kernel(order, itile, igroup, ifirst, ilast, ivalid, ie0, ie1, ie2, ihas2, flags, x_hbm, rt_ref, wg, wu, wd, nffn, nfin,
                y_hbm, xbuf, ybuf, acc, hbuf, rcol, gsem, ssem, *, tme, ntiles):
    i = pl.program_id(0)
    t = itile[i]
    slot = t % 2
    g = igroup[i]
    is_first = ifirst[i] == 1

    def gather_row(tile, sl, r):
        tok = order[tile * tme + r]
        src = x_hbm.at[pl.ds(pl.multiple_of(tok * TOKEN_ROWS, TOKEN_ROWS), TOKEN_ROWS)]
        dst = xbuf.at[pl.ds(pl.multiple_of((sl * tme + r) * TOKEN_ROWS, TOKEN_ROWS), TOKEN_ROWS)]
        return pltpu.make_async_copy(src, dst, gsem.at[sl])

    def scatter_row(tile, sl, r):
        tok = order[tile * tme + r]
        return pltpu.make_async_copy(ybuf.at[sl, pl.ds(r, 1)], y_hbm.at[pl.ds(tok, 1)], ssem.at[sl])

    def start_rows(make, tile, sl):
        def body(r, c):
            make(tile, sl, r).start()
            return c
        lax.fori_loop(0, tme, body, 0, unroll=8)

    def wait_gather(sl):
        rows = tme * TOKEN_ROWS
        pltpu.make_async_copy(x_hbm.at[pl.ds(0, rows)], xbuf.at[pl.ds(pl.multiple_of(sl * rows, rows), rows)],
                              gsem.at[sl]).wait()

    def wait_scatter(sl):
        pltpu.make_async_copy(ybuf.at[sl], y_hbm.at[pl.ds(0, tme)], ssem.at[sl]).wait()

    def start_rows_inline(make, tile, sl, r0=0, r1=tme):
        for r in range(r0, r1):
            make(tile, sl, r).start()

    def expert(e):
        eid = (g * EXPERTS_PER_GROUP + e).astype(F32)
        ce = (jnp.where(rcol[:, 0:1] == eid, rcol[:, 2:3], 0.0)
              + jnp.where(rcol[:, 1:2] == eid, rcol[:, 3:4], 0.0))
        h = hbuf[...]
        a = _dot(h, wg[e])
        u = _dot(h, wu[e])
        act = (a * _sigmoid(a)) * u * ce
        acc[...] += _dot(act.astype(BF16), wd[e])

    def expert_with(e, run, alternatives):
        plain = run
        for cond, side_work in alternatives:
            @pl.when(cond)
            def _():
                side_work()
                expert(e)

            plain = jnp.logical_and(plain, jnp.logical_not(cond))

        @pl.when(plain)
        def _():
            expert(e)

    @pl.when(i == 0)
    def _():
        start_rows(gather_row, 0, 0)

    @pl.when(is_first)
    def _():
        wait_gather(slot)
        base = pl.multiple_of(slot * (tme * TOKEN_ROWS), tme * TOKEN_ROWS)
        x1 = jnp.concatenate([xbuf[pl.ds(base + j, tme, stride=TOKEN_ROWS), :] for j in range(TOKEN_ROWS)], axis=1)
        acc[...] = x1
        hbuf[...] = _rms(x1, nffn[...]).astype(BF16)
        rt = rt_ref[0]
        rcol[...] = jnp.transpose(jnp.concatenate([rt, jnp.zeros((LANES - rt.shape[0], tme), F32)], axis=0))

    valid = ivalid[i] == 1
    has2 = ihas2[i] == 1
    do_gather = jnp.logical_and(is_first, t + 1 < ntiles)
    do_scatter = jnp.logical_and(is_first, t >= 1)
    half = tme // 2
    scatter_rows = lambda r0, r1: (lambda: start_rows_inline(scatter_row, t - 1, 1 - slot, r0, r1))
    expert_with(ie0[i], valid, [(do_gather, lambda: start_rows_inline(gather_row, t + 1, 1 - slot))])
    expert_with(ie1[i], valid, [(jnp.logical_and(do_scatter, has2), scatter_rows(0, half)),
                                (jnp.logical_and(do_scatter, jnp.logical_not(has2)), scatter_rows(0, tme))])
    expert_with(ie2[i], has2, [(jnp.logical_and(do_scatter, has2), scatter_rows(half, tme))])
    for e in range(EXPERTS_PER_GROUP):
        @pl.when(flags[i * EXPERTS_PER_GROUP + e] == 1)
        def _():
            expert(e)

    @pl.when(ilast[i] == 1)
    def _():
        @pl.when(t >= 2)
        def _():
            wait_scatter(slot)

        ybuf[slot] = _rms(acc[...], nfin[...])

        @pl.when(t == ntiles - 1)
        def _():
            start_rows(scatter_row, t, slot)

    @pl.when(i == pl.num_programs(0) - 1)
    def _():
        for sl in range(min(2, ntiles)):
            wait_scatter(sl)


def _moe(x1t, rt, weg, weu, wed, nffn, nfin):
    n = x1t.shape[0] // TOKEN_ROWS
    tme = min(MOE_TILE, n)
    assert n % tme == 0
    ntiles = n // tme
    plan, rt_sorted = _moe_plan(rt, tme)
    max_items = ntiles + N_GROUPS - 1
    grp = lambda i, order, itile, igroup, *_: (igroup[i], 0, 0)
    til = lambda i, order, itile, *_: (itile[i], 0, 0)
    const = lambda i, *_: (0, 0)
    grid_spec = pltpu.PrefetchScalarGridSpec(
        num_scalar_prefetch=len(plan),
        grid=(max_items,),
        in_specs=[pl.BlockSpec(memory_space=pl.ANY),
                  pl.BlockSpec((1,) + rt_sorted.shape[1:], til),
                  pl.BlockSpec((EXPERTS_PER_GROUP, D_MODEL, EXPERT_FF), grp),
                  pl.BlockSpec((EXPERTS_PER_GROUP, D_MODEL, EXPERT_FF), grp),
                  pl.BlockSpec((EXPERTS_PER_GROUP, EXPERT_FF, D_MODEL), grp),
                  pl.BlockSpec((1, D_MODEL), const), pl.BlockSpec((1, D_MODEL), const)],
        out_specs=pl.BlockSpec(memory_space=pl.ANY),
        scratch_shapes=[pltpu.VMEM((2 * tme * TOKEN_ROWS, LANES), F32), pltpu.VMEM((2, tme, D_MODEL), F32),
                        pltpu.VMEM((tme, D_MODEL), F32), pltpu.VMEM((tme, D_MODEL), BF16),
                        pltpu.VMEM((tme, LANES), F32),
                        pltpu.SemaphoreType.DMA((2,)), pltpu.SemaphoreType.DMA((2,))],
    )
    return pl.pallas_call(
        functools.partial(_moe_kernel, tme=tme, ntiles=ntiles),
        grid_spec=grid_spec,
        out_shape=jax.ShapeDtypeStruct((n, D_MODEL), F32),
        compiler_params=_cparams(("arbitrary",)),
        name="moe",
    )(*plan, x1t, rt_sorted, weg, weu, wed, nffn, nfin)


def _rope_tables(positions):
    half = HEAD_DIM // 2
    inv_freq = ROPE_THETA ** (-jnp.arange(half, dtype=F32) / half)
    ang = positions.astype(F32)[:, None] * inv_freq[None, :]
    cos, sin = jnp.cos(ang), jnp.sin(ang)
    reps = LANES // HEAD_DIM
    return (jnp.tile(jnp.concatenate([cos, cos], -1), (1, reps)),
            jnp.tile(jnp.concatenate([-sin, sin], -1), (1, reps)))


def _prep_weights(norm_mix, w_in, w_gla_f2, b_gla_f, gla_norm, w_proj_attn, w_proj_gla, w_out, norm_ffn,
                  w_router_group, b_router_group, w_router_expert, b_router_expert):
    w = {"win": w_in.astype(BF16)}
    w["wf2"] = jnp.pad(w_gla_f2.astype(BF16), ((0, LANES - GLA_GATE_RANK), (0, 0)))
    w["bf"] = b_gla_f.reshape(1, -1)
    w["norm_mix"] = norm_mix.reshape(1, -1)
    w["gla_norm"] = gla_norm.reshape(1, -1)
    w["wpa"] = w_proj_attn.astype(BF16)
    w["wpg"] = w_proj_gla.astype(BF16)
    w["wo"] = w_out.astype(BF16)
    w["nf"] = norm_ffn.reshape(1, -1)
    pad = LANES - N_EXPERTS - N_GROUPS
    wr_t = jnp.pad(jnp.concatenate([w_router_expert, w_router_group], axis=1), ((0, 0), (0, pad))).T
    w["wr_hi"], w["wr_lo"] = _split_bf16(wr_t)
    w["br"] = jnp.broadcast_to(jnp.pad(jnp.concatenate([b_router_expert, b_router_group]), (0, pad))[:, None],
                               (LANES, LANES))
    return w


def _layer(x, positions_tab, cache, w, sinks, weg, weu, wed, nfin, tm):
    batch, seq, _ = x.shape
    n = batch * seq
    x2d = x.reshape(n, D_MODEL)
    cos_t, sin_t = positions_tab
    qa, ka, va, qg, kg, vg, rg, la, ga, gb = _inproj(x2d, w["norm_mix"], cos_t, sin_t, w, tm)
    if cache is None:
        a_out = _swa_prompt(sinks, qa, ka, va, batch, seq)
        last = lambda z: z.reshape(batch, seq, KV_WIDTH)[:, seq - WINDOW:].reshape(batch, WINDOW, ATTN_KV_HEADS, HEAD_DIM)
        new_k, new_v = last(ka), last(va)
        g_out, new_s = _gla_prompt(qg, kg, vg, la, rg, w["gla_norm"], batch, seq)
    else:
        cache_k, cache_v, state = cache
        a_out, new_k, new_v = _swa_sample(sinks, qa, ka, va, cache_k.reshape(batch, WINDOW, KV_WIDTH),
                                          cache_v.reshape(batch, WINDOW, KV_WIDTH), batch, seq)
        new_k = new_k.reshape(batch, WINDOW, ATTN_KV_HEADS, HEAD_DIM)
        new_v = new_v.reshape(batch, WINDOW, ATTN_KV_HEADS, HEAD_DIM)
        g_out, new_s = _gla_sample(qg, kg, vg, la, rg, state, w["gla_norm"], batch, seq)
    x1t, rt = _post(x2d, a_out, g_out, ga, gb, w, tm * 2 if x2d.shape[0] % (tm * 2) == 0 else tm)
    y = _moe(x1t, rt, weg, weu, wed, w["nf"], nfin)
    return y.reshape(batch, seq, D_MODEL), new_k, new_v, new_s


def kernel(x_prompt, x_sample, cache_win_k, cache_win_v, state_gla, norm_mix, w_in, w_gla_f2, b_gla_f, gla_norm,
           attn_sinks, w_proj_attn, w_proj_gla, w_out, norm_ffn, w_router_group, b_router_group, w_router_expert,
           b_router_expert, w_exp_gate, w_exp_up, w_exp_down, norm_final):
    assert norm_mix.shape[0] == 1, "single-layer step"
    seq_p = x_prompt.shape[1]
    dec_b, dec_t = x_sample.shape[0], x_sample.shape[1]
    w = _prep_weights(norm_mix[0], w_in[0], w_gla_f2[0], b_gla_f[0], gla_norm[0], w_proj_attn[0], w_proj_gla[0],
                      w_out[0], norm_ffn[0], w_router_group[0], b_router_group[0], w_router_expert[0],
                      b_router_expert[0])
    weg = w_exp_gate[0].astype(BF16)
    weu = w_exp_up[0].astype(BF16)
    wed = w_exp_down[0].astype(BF16)
    nfin = norm_final.reshape(1, -1)
    sinks = attn_sinks[0]
    tab_p = _rope_tables(jnp.arange(seq_p, dtype=jnp.int32))
    pos_s = PAST_LEN + jnp.arange(dec_t, dtype=jnp.int32)
    tab_s = tuple(jnp.tile(t, (dec_b, 1)) for t in _rope_tables(pos_s))
    tm_p = min(512, seq_p)
    tm_s = dec_b * dec_t
    yp, pk, pv, ps = _layer(x_prompt, tab_p, None, w, sinks, weg, weu, wed, nfin, tm_p)
    ys, sk, sv, ss = _layer(x_sample, tab_s, (cache_win_k[0], cache_win_v[0], state_gla[0]), w, sinks, weg, weu, wed,
                            nfin, tm_s)
    return (yp, ys, pk[None], pv[None], ps[None], sk[None], sv[None], ss[None])
```

```python
import functools
import math

import numpy as np
import jax
import jax.numpy as jnp
from jax import lax
from jax.experimental import pallas as pl
from jax.experimental.pallas import tpu as pltpu

F32 = jnp.float32
BF16 = jnp.bfloat16

D_MODEL = 1024
ATTN_HEADS = 8
ATTN_KV_HEADS = 2
GROUP = ATTN_HEADS // ATTN_KV_HEADS
HEAD_DIM = 64
ATTN_WIDTH = ATTN_HEADS * HEAD_DIM
KV_WIDTH = ATTN_KV_HEADS * HEAD_DIM
WINDOW = 128
ROPE_THETA = 10000.0
PAST_LEN = 8192
GLA_HEADS = 4
GLA_KEY_WIDTH = D_MODEL // 2
GLA_VALUE_WIDTH = D_MODEL
GLA_DK = GLA_KEY_WIDTH // GLA_HEADS
GLA_DV = GLA_VALUE_WIDTH // GLA_HEADS
GLA_GATE_RANK = 16
GLA_GATE_NORMALIZER = 16.0
N_GROUPS = 4
EXPERTS_PER_GROUP = 8
N_EXPERTS = N_GROUPS * EXPERTS_PER_GROUP
EXPERT_FF = 256
EPS = 1e-6
LOG2_E = 1.4426950408889634

LANES = 128
GLA_CHUNK = 128
GLA_HEADS_PER_STEP = 4
SWA_BLOCKS_PER_STEP = 4
SAMPLE_SEQ_BLOCK = 16
POST_PART_ROWS = 512
MOE_TILE = 256
TOKEN_ROWS = D_MODEL // LANES
INPROJ_WIDTHS = (ATTN_WIDTH, 2 * KV_WIDTH, GLA_KEY_WIDTH, GLA_KEY_WIDTH, GLA_VALUE_WIDTH, GLA_VALUE_WIDTH)
ROUTER_ROWS = 40
VMEM_LIMIT = 56 * 1024 * 1024


def _cparams(sem):
    return pltpu.CompilerParams(dimension_semantics=sem, vmem_limit_bytes=VMEM_LIMIT)


def _rms(x, g):
    return x * lax.rsqrt(jnp.mean(x * x, axis=-1, keepdims=True) + EPS) * g


def _sigmoid(x):
    return 1.0 / (1.0 + jnp.exp(-x))


def _dot(a, b):
    return jnp.dot(a, b, preferred_element_type=F32)


def _dot_nt(a, b):
    return lax.dot_general(a, b, (((1,), (1,)), ((), ())), preferred_element_type=F32)


def _dot_tn(a, b):
    return lax.dot_general(a, b, (((0,), (0,)), ((), ())), preferred_element_type=F32)


def _split_bf16(x):
    hi = x.astype(BF16)
    lo = (x - hi.astype(F32)).astype(BF16)
    return hi, lo


def _inproj_kernel(x_ref, g_ref, cos_ref, sin_ref, win, wf2, bf,
                   qa_o, ka_o, va_o, qg_o, kg_o, vg_o, rg_o, la_o, ga_o, gb_o, wgate):
    f0 = sum(INPROJ_WIDTHS)

    @pl.when(pl.program_id(0) == 0)
    def _():
        tail = win[:, f0:]
        wgate[...] = tail[:, GLA_GATE_RANK:GLA_GATE_RANK + 2 * D_MODEL]

    tm = x_ref.shape[0]
    halves = [slice(0, tm // 2), slice(tm // 2, tm)] if tm % 16 == 0 else [slice(0, tm)]
    hbs = [_rms(x_ref[r, :], g_ref[...]).astype(BF16) for r in halves]
    cols = np.cumsum((0,) + INPROJ_WIDTHS[:-1])
    wqa, wkva, wqg, wkg, wvg, wrg = (win.at[:, int(c):int(c) + wd] for c, wd in zip(cols, INPROJ_WIDTHS))
    wga = wgate.at[:, :D_MODEL]
    wgb = wgate.at[:, D_MODEL:]
    wf = win.at[:, f0:f0 + LANES]
    lane = lax.broadcasted_iota(jnp.int32, (halves[0].stop - halves[0].start, LANES), 1)
    first_half = (lane % HEAD_DIM) < (HEAD_DIM // 2)

    def rope(t, r):
        swapped = jnp.where(first_half, pltpu.roll(t, LANES - HEAD_DIM // 2, 1), pltpu.roll(t, HEAD_DIM // 2, 1))
        return t * cos_ref[r, :] + swapped * sin_ref[r, :]

    for r, hb in zip(halves, hbs):
        ga_o[r, :] = _sigmoid(_dot(hb, wga[...])).astype(BF16)
    for r, hb in zip(halves, hbs):
        gb_o[r, :] = _sigmoid(_dot(hb, wgb[...])).astype(BF16)
    for r, hb in zip(halves, hbs):
        rz = _dot(hb, wrg[...])
        rg_o[r, :] = (rz * _sigmoid(rz)).astype(BF16)
    for r, hb in zip(halves, hbs):
        z = _dot(_dot(hb, wf[...]).astype(BF16), wf2[...]) + bf[...]
        la_o[r, :] = (jnp.minimum(z, 0.0) - jnp.log1p(jnp.exp(-jnp.abs(z)))) * (1.0 / GLA_GATE_NORMALIZER)
    for r, hb in zip(halves, hbs):
        qa = _dot(hb, wqa[...])
        for c in range(ATTN_WIDTH // LANES):
            sl = slice(c * LANES, (c + 1) * LANES)
            qa_o[r, sl] = (rope(qa[:, sl], r) * (HEAD_DIM ** -0.5)).astype(BF16)
    for r, hb in zip(halves, hbs):
        kva = _dot(hb, wkva[...])
        ka_o[r, :] = rope(kva[:, :KV_WIDTH], r)
        va_o[r, :] = kva[:, KV_WIDTH:]
    for r, hb in zip(halves, hbs):
        qg_o[r, :] = (_dot(hb, wqg[...]) * (GLA_DK ** -0.5)).astype(BF16)
    for r, hb in zip(halves, hbs):
        kg_o[r, :] = _dot(hb, wkg[...]).astype(BF16)
    for r, hb in zip(halves, hbs):
        vg_o[r, :] = _dot(hb, wvg[...]).astype(BF16)


def _inproj(x2d, norm_g, cos_t, sin_t, w, tm):
    n = x2d.shape[0]
    ntab = cos_t.shape[0] // tm
    row = lambda i: (i, 0)
    const = lambda i: (0, 0)
    tab = lambda i: (i % ntab, 0)
    wnames = ("win", "wf2", "bf")
    wspecs = [pl.BlockSpec(w[k].shape, const) for k in wnames]
    widths = (ATTN_WIDTH, KV_WIDTH, KV_WIDTH, GLA_KEY_WIDTH, GLA_KEY_WIDTH, GLA_VALUE_WIDTH, GLA_VALUE_WIDTH,
              GLA_KEY_WIDTH, D_MODEL, D_MODEL)
    dtypes = (BF16, F32, F32, BF16, BF16, BF16, BF16, F32, BF16, BF16)
    return pl.pallas_call(
        _inproj_kernel,
        grid=(n // tm,),
        in_specs=[pl.BlockSpec((tm, D_MODEL), row), pl.BlockSpec((1, D_MODEL), const),
                  pl.BlockSpec((tm, LANES), tab), pl.BlockSpec((tm, LANES), tab)] + wspecs,
        out_specs=[pl.BlockSpec((tm, wd), row) for wd in widths],
        out_shape=[jax.ShapeDtypeStruct((n, wd), dt) for wd, dt in zip(widths, dtypes)],
        scratch_shapes=[pltpu.VMEM((D_MODEL, 2 * D_MODEL), BF16)],
        compiler_params=_cparams(("arbitrary",)),
        name="inproj",
    )(x2d, norm_g, cos_t, sin_t, *[w[k] for k in wnames])


def _swa_prompt_kernel(sink_ref, q_ref, kc_ref, kp_ref, vc_ref, vp_ref, o_ref, *, qb):
    n = pl.program_id(1)
    k3 = jnp.concatenate([kp_ref[...], kc_ref[...]], axis=0).astype(BF16)
    v3 = jnp.concatenate([vp_ref[...], vc_ref[...]], axis=0).astype(BF16)
    t = lax.broadcasted_iota(jnp.int32, (WINDOW, 2 * WINDOW), 0)
    j = lax.broadcasted_iota(jnp.int32, (WINDOW, 2 * WINDOW), 1)
    band = (j >= t) & (j <= t + WINDOW)
    zeros = jnp.zeros((2 * WINDOW, HEAD_DIM), BF16)
    ones = jnp.ones((2 * WINDOW, HEAD_DIM), BF16)
    lane = lax.broadcasted_iota(jnp.int32, (WINDOW, LANES), 1)
    chains = [(blk, h) for blk in range(qb) for h in range(ATTN_HEADS)]
    scores = []
    for blk, h in chains:
        kv = h // GROUP
        rows = slice(blk * WINDOW, (blk + 1) * WINDOW)
        keys = slice(blk * WINDOW, (blk + 2) * WINDOW)
        s = _dot_nt(q_ref[rows, h * HEAD_DIM:(h + 1) * HEAD_DIM], k3[keys, kv * HEAD_DIM:(kv + 1) * HEAD_DIM])
        valid = band & ((j >= WINDOW) | (n > 0)) if blk == 0 else band
        scores.append(jnp.where(valid, s, -jnp.inf))
    probs, sink_terms = [], []
    for (blk, h), s in zip(chains, scores):
        m = jnp.maximum(jnp.max(s, axis=-1, keepdims=True), sink_ref[h])
        probs.append(jnp.exp(s - m).astype(BF16))
        sink_terms.append(jnp.exp(sink_ref[h] - m))
    for blk in range(qb):
        rows = slice(blk * WINDOW, (blk + 1) * WINDOW)
        keys = slice(blk * WINDOW, (blk + 2) * WINDOW)
        for kv in range(ATTN_KV_HEADS):
            vv = v3[keys, kv * HEAD_DIM:(kv + 1) * HEAD_DIM]
            vext = (jnp.concatenate([vv, zeros, ones, zeros], axis=1), jnp.concatenate([zeros, vv, zeros, ones], axis=1))
            for pr in range(GROUP // 2):
                h0 = kv * GROUP + pr * 2
                c0 = blk * ATTN_HEADS + h0
                acc = _dot(probs[c0], vext[0]) + _dot(probs[c0 + 1], vext[1])
                l = acc[:, LANES:] + jnp.where(lane < HEAD_DIM, sink_terms[c0], sink_terms[c0 + 1])
                o_ref[rows, h0 * HEAD_DIM:h0 * HEAD_DIM + LANES] = (acc[:, :LANES] / l).astype(BF16)


def _swa_prompt(sinks, qa, ka, va, batch, seq):
    nb = seq // WINDOW
    qb = math.gcd(SWA_BLOCKS_PER_STEP, nb)
    steps = nb // qb
    cur = lambda b, n: (b * steps + n, 0)
    prev = lambda b, n: (b * nb + jnp.maximum(n * qb - 1, 0), 0)
    return pl.pallas_call(
        functools.partial(_swa_prompt_kernel, qb=qb),
        grid=(batch, steps),
        in_specs=[pl.BlockSpec(memory_space=pltpu.SMEM),
                  pl.BlockSpec((qb * WINDOW, ATTN_WIDTH), cur),
                  pl.BlockSpec((qb * WINDOW, KV_WIDTH), cur), pl.BlockSpec((WINDOW, KV_WIDTH), prev),
                  pl.BlockSpec((qb * WINDOW, KV_WIDTH), cur), pl.BlockSpec((WINDOW, KV_WIDTH), prev)],
        out_specs=pl.BlockSpec((qb * WINDOW, ATTN_WIDTH), cur),
        out_shape=jax.ShapeDtypeStruct(qa.shape, BF16),
        compiler_params=_cparams(("parallel", "parallel")),
        name="swa_prompt",
    )(sinks, qa, ka, ka, va, va)


def _swa_sample_kernel(sink_ref, q_ref, kn_ref, vn_ref, kn3_ref, vn3_ref, ck_ref, cv_ref, o_ref, nk_ref, nv_ref,
                       *, t_new):
    sb = ck_ref.shape[0]
    spv = 8 // t_new
    nq = GROUP * 8
    nc = spv * WINDOW
    qi = lax.broadcasted_iota(jnp.int32, (nq, nc), 0) % 8
    ci = lax.broadcasted_iota(jnp.int32, (nq, nc), 1)
    valid_c = (qi // t_new == ci // WINDOW) & (ci % WINDOW >= qi % t_new)
    qn = lax.broadcasted_iota(jnp.int32, (nq, 8), 0) % 8
    cn = lax.broadcasted_iota(jnp.int32, (nq, 8), 1)
    valid_n = (qn // t_new == cn // t_new) & (cn <= qn)
    grow = lax.broadcasted_iota(jnp.int32, (nq, 1), 0) // 8
    chains = [(vr, kv) for vr in range(sb // spv) for kv in range(ATTN_KV_HEADS)]
    scored = []
    for vr, kv in chains:
        r8 = slice(8 * vr, 8 * vr + 8)
        cs = slice(kv * HEAD_DIM, (kv + 1) * HEAD_DIM)
        heads = [kv * GROUP + g for g in range(GROUP)]
        qs = jnp.concatenate([q_ref[r8, h * HEAD_DIM:(h + 1) * HEAD_DIM] for h in heads], axis=0)
        ck = ck_ref[vr * spv:(vr + 1) * spv, :, cs].reshape(nc, HEAD_DIM).astype(BF16)
        kn = kn_ref[r8, cs].astype(BF16)
        scored.append((jnp.where(valid_c, _dot_nt(qs, ck), -jnp.inf), jnp.where(valid_n, _dot_nt(qs, kn), -jnp.inf)))
    soft = []
    for (vr, kv), (s_c, s_n) in zip(chains, scored):
        sink = sink_ref[kv * GROUP]
        for g in range(1, GROUP):
            sink = jnp.where(grow == g, sink_ref[kv * GROUP + g], sink)
        m = jnp.maximum(jnp.maximum(jnp.max(s_c, axis=-1, keepdims=True), jnp.max(s_n, axis=-1, keepdims=True)), sink)
        p_c = jnp.exp(s_c - m)
        p_n = jnp.exp(s_n - m)
        l = jnp.sum(p_c, axis=-1, keepdims=True) + jnp.sum(p_n, axis=-1, keepdims=True) + jnp.exp(sink - m)
        soft.append((p_c.astype(BF16), p_n.astype(BF16), l))
    for (vr, kv), (p_c, p_n, l) in zip(chains, soft):
        r8 = slice(8 * vr, 8 * vr + 8)
        cs = slice(kv * HEAD_DIM, (kv + 1) * HEAD_DIM)
        cv = cv_ref[vr * spv:(vr + 1) * spv, :, cs].reshape(nc, HEAD_DIM).astype(BF16)
        o = (_dot(p_c, cv) + _dot(p_n, vn_ref[r8, cs].astype(BF16))) / l
        for a in range(GROUP // 2):
            pair = jnp.concatenate([o[16 * a:16 * a + 8], o[16 * a + 8:16 * a + 16]], axis=1)
            c0 = (kv * GROUP + 2 * a) * HEAD_DIM
            o_ref[r8, c0:c0 + LANES] = pair.astype(BF16)
    nk_ref[:, 0:WINDOW - t_new, :] = ck_ref[:, t_new:WINDOW, :]
    nk_ref[:, WINDOW - t_new:WINDOW, :] = kn3_ref[...]
    nv_ref[:, 0:WINDOW - t_new, :] = cv_ref[:, t_new:WINDOW, :]
    nv_ref[:, WINDOW - t_new:WINDOW, :] = vn3_ref[...]


def _swa_sample(sinks, qa, ka, va, cache_k, cache_v, batch, t_new):
    sb = SAMPLE_SEQ_BLOCK
    rows = sb * t_new
    r2 = lambda i: (i, 0)
    r3 = lambda i: (i, 0, 0)
    ka3 = ka.reshape(batch, t_new, KV_WIDTH)
    va3 = va.reshape(batch, t_new, KV_WIDTH)
    return pl.pallas_call(
        functools.partial(_swa_sample_kernel, t_new=t_new),
        grid=(batch // sb,),
        in_specs=[pl.BlockSpec(memory_space=pltpu.SMEM),
                  pl.BlockSpec((rows, ATTN_WIDTH), r2),
                  pl.BlockSpec((rows, KV_WIDTH), r2), pl.BlockSpec((rows, KV_WIDTH), r2),
                  pl.BlockSpec((sb, t_new, KV_WIDTH), r3), pl.BlockSpec((sb, t_new, KV_WIDTH), r3),
                  pl.BlockSpec((sb, WINDOW, KV_WIDTH), r3), pl.BlockSpec((sb, WINDOW, KV_WIDTH), r3)],
        out_specs=[pl.BlockSpec((rows, ATTN_WIDTH), r2),
                   pl.BlockSpec((sb, WINDOW, KV_WIDTH), r3), pl.BlockSpec((sb, WINDOW, KV_WIDTH), r3)],
        out_shape=[jax.ShapeDtypeStruct(qa.shape, BF16),
                   jax.ShapeDtypeStruct(cache_k.shape, F32), jax.ShapeDtypeStruct(cache_v.shape, F32)],
        compiler_params=_cparams(("parallel",)),
        name="swa_sample",
    )(sinks, qa, ka, va, ka3, va3, cache_k, cache_v)


def _gla_constants(c, seg, with_rem):
    t = np.arange(c)
    sid = t // seg
    same = sid[:, None] == sid[None, :]
    levels = []
    m = seg // 2
    while m >= 1:
        levels.append(m)
        m //= 2
    mats, roles, masks = [], [], []
    for m in levels:
        blk = t // (2 * m)
        second = (t // m) % 2 == 1
        p = blk * 2 * m + m - 1
        u = t[None, :]
        mq = (u > p[:, None]) & (u <= t[:, None])
        mk = (u > t[:, None]) & (u <= p[:, None])
        if m > 1:
            mats.append(np.where(second[:, None], mq, mk))
        roles.append(np.broadcast_to(second[:, None], (c, LANES)))
        masks.append((blk[:, None] == blk[None, :]) & second[:, None] & ~second[None, :])
    masks.append(np.eye(c, dtype=bool))
    mats.append(same & (t[None, :] <= t[:, None]))
    if with_rem:
        mats.append(same & (t[None, :] > t[:, None]))
    mall = np.concatenate(mats, 0).astype(np.float32)
    mall = jnp.asarray(np.concatenate([mall, mall], 1), BF16)
    role = jnp.asarray(np.concatenate(roles, 0).astype(np.float32))
    mask = jnp.asarray(np.concatenate(masks, 0).astype(np.float32))
    return len(levels), mall, role, mask


def _gla_exponents(la, mall):
    la2 = la * LOG2_E
    hl = jnp.concatenate(_split_bf16(la2), axis=0)
    return _dot(mall, hl), hl, la2


def _gla_scores(qb, kb, e2, la2, role_ref, mask_ref, nlev, c):
    qf = qb.astype(F32)
    kf = kb.astype(F32)
    terms = []
    for lv in range(nlev):
        sl = slice(lv * c, (lv + 1) * c)
        m = 1 << (nlev - 1 - lv)
        if m % 8 == 0:
            pe = jnp.exp2(e2[sl])
            x = jnp.concatenate([(qf if blk % 2 else kf)[blk * m:(blk + 1) * m] * pe[blk * m:(blk + 1) * m]
                                 for blk in range(c // m)], axis=0).astype(BF16)
        else:
            second = role_ref[sl, :] > 0.5
            e = e2[sl] if lv < nlev - 1 else jnp.where(second, la2, 0.0)
            x = (jnp.where(second, qf, kf) * jnp.exp2(e)).astype(BF16)
        terms.append((x, x, lv))
    terms.append((qb, kb, nlev))
    mask = lambda i: mask_ref[i * c:(i + 1) * c, :]
    a = None
    pair = c % LANES == 0
    while terms:
        if pair and len(terms) >= 2:
            (l0, r0, i0), (l1, r1, i1) = terms.pop(), terms.pop()
            z = jnp.zeros_like(r0)
            rhs = jnp.concatenate([jnp.concatenate([r0, z], axis=1), jnp.concatenate([z, r1], axis=1)], axis=0)
            g = _dot_nt(jnp.concatenate([l0, l1], axis=1), rhs)
            t = mask(i0) * g[:, :c] + mask(i1) * g[:, c:]
        else:
            l0, r0, i0 = terms.pop()
            t = mask(i0) * _dot_nt(l0, r0)
        a = t if a is None else a + t
    return a, qf, kf


def _gla_out(o, g, gate):
    return (_rms(o, g) * gate.astype(F32)).astype(BF16)


def _gla_prompt_kernel(q_ref, k_ref, v_ref, la_ref, r_ref, mall_ref, role_ref, mask_ref, g_ref, o_ref, s_ref,
                       s_scr, *, nlev, nchunks):
    c = GLA_CHUNK
    hp = GLA_HEADS_PER_STEP
    s_scr[...] = jnp.zeros_like(s_scr)

    def chunk(i, carry):
        rows = pl.ds(pl.multiple_of(i * c, c), c)
        e2_all, _, la2_all = _gla_exponents(la_ref[rows, :], mall_ref[...])
        ksl = [slice(h * GLA_DK, (h + 1) * GLA_DK) for h in range(hp)]
        vsl = [slice(h * GLA_DV, (h + 1) * GLA_DV) for h in range(hp)]
        scores = [_gla_scores(q_ref[rows, ksl[h]], k_ref[rows, ksl[h]], e2_all[:, ksl[h]], la2_all[:, ksl[h]], role_ref,
                              mask_ref, nlev, c)[0] for h in range(hp)]
        for h in range(hp):
            b = e2_all[(nlev - 1) * c:nlev * c, ksl[h]]
            qf = q_ref[rows, ksl[h]].astype(F32)
            o = (_dot(scores[h].astype(BF16), v_ref[rows, vsl[h]])
                 + _dot((qf * jnp.exp2(b)).astype(BF16), s_scr[h].astype(BF16)))
            o_ref[rows, vsl[h]] = _gla_out(o, g_ref[...], r_ref[rows, vsl[h]])
        for h in range(hp):
            b = e2_all[(nlev - 1) * c:nlev * c, ksl[h]]
            kf = k_ref[rows, ksl[h]].astype(F32)
            kt = (kf * jnp.exp2(b[c - 1:c, :] - b)).astype(BF16)
            dec = jnp.exp2(jnp.transpose(b[c - 8:c, :]))[:, 7:8]
            s_scr[h] = dec * s_scr[h] + _dot_tn(kt, v_ref[rows, vsl[h]])
        return carry

    lax.fori_loop(0, nchunks, chunk, 0, unroll=4)
    s_ref[0] = s_scr[...]


def _gla_prompt(qg, kg, vg, la, rg, gnorm, batch, seq):
    nlev, mall, role, mask = _gla_constants(GLA_CHUNK, GLA_CHUNK, with_rem=False)
    hp = GLA_HEADS_PER_STEP
    bh = lambda b, h: (b, h)
    const = lambda b, h: (0, 0)
    return pl.pallas_call(
        functools.partial(_gla_prompt_kernel, nlev=nlev, nchunks=seq // GLA_CHUNK),
        grid=(batch, GLA_HEADS // hp),
        in_specs=[pl.BlockSpec((seq, hp * GLA_DK), bh), pl.BlockSpec((seq, hp * GLA_DK), bh),
                  pl.BlockSpec((seq, hp * GLA_DV), bh), pl.BlockSpec((seq, hp * GLA_DK), bh),
                  pl.BlockSpec((seq, hp * GLA_DV), bh),
                  pl.BlockSpec(mall.shape, const), pl.BlockSpec(role.shape, const), pl.BlockSpec(mask.shape, const),
                  pl.BlockSpec((1, GLA_DV), const)],
        out_specs=[pl.BlockSpec((seq, hp * GLA_DV), bh),
                   pl.BlockSpec((1, hp, GLA_DK, GLA_DV), lambda b, h: (b, h, 0, 0))],
        out_shape=[jax.ShapeDtypeStruct(vg.shape, BF16),
                   jax.ShapeDtypeStruct((batch, GLA_HEADS, GLA_DK, GLA_DV), F32)],
        scratch_shapes=[pltpu.VMEM((hp, GLA_DK, GLA_DV), F32)],
        compiler_params=_cparams(("parallel", "parallel")),
        name="gla_prompt",
    )(qg, kg, vg, la, rg, mall, role, mask, gnorm)


def _gla_sample_kernel(q_ref, k_ref, v_ref, la_ref, r_ref, s0_ref, mall_ref, role_ref, mask_ref, msum_ref, g_ref,
                       o_ref, s_ref, *, nlev, t_new):
    sb = s0_ref.shape[0]
    c = sb * t_new
    spv = 8 // t_new
    e2_all, hl_all, la2_all = _gla_exponents(la_ref[...], mall_ref[...])
    seq_in_tile = lax.broadcasted_iota(jnp.int32, (8, GLA_DV), 0) // t_new
    seq_of_row = lax.broadcasted_iota(jnp.int32, (c, GLA_DV), 0) // t_new
    ksl = [slice(h * GLA_DK, (h + 1) * GLA_DK) for h in range(GLA_HEADS)]
    scores = [_gla_scores(q_ref[:, ksl[h]], k_ref[:, ksl[h]], e2_all[:, ksl[h]], la2_all[:, ksl[h]], role_ref, mask_ref,
                          nlev, c) for h in range(GLA_HEADS)]
    for h in range(GLA_HEADS):
        ks = ksl[h]
        vs = slice(h * GLA_DV, (h + 1) * GLA_DV)
        v = v_ref[:, vs]
        e2 = e2_all[:, ks]
        hl = hl_all[:, ks]
        a, qf, kf = scores[h]
        qe = (qf * jnp.exp2(e2[(nlev - 1) * c:nlev * c])).astype(BF16)
        kt_t = jnp.transpose(kf * jnp.exp2(e2[nlev * c:(nlev + 1) * c])).astype(BF16)
        bl_t = jnp.transpose(_dot(msum_ref[...], hl[:c]) + _dot(msum_ref[...], hl[c:]))
        dec_t = jnp.exp2(bl_t)
        inter = []
        for vr in range(c // 8):
            rows8 = qe[8 * vr:8 * vr + 8]
            tile = None
            for u in range(spv):
                j = vr * spv + u
                s0 = s0_ref[j, h]
                r = _dot(rows8, s0.astype(BF16))
                tile = r if tile is None else jnp.where(seq_in_tile == u, r, tile)
                vj = jnp.where(seq_of_row == j, v, jnp.zeros_like(v))
                s_ref[j, h] = dec_t[:, j:j + 1] * s0 + _dot(kt_t, vj)
            inter.append(tile)
        o = _dot(a.astype(BF16), v) + jnp.concatenate(inter, axis=0)
        o_ref[:, vs] = _gla_out(o, g_ref[...], r_ref[:, vs])


def _gla_sample(qg, kg, vg, la, rg, state, gnorm, batch, t_new):
    sb = SAMPLE_SEQ_BLOCK
    c = sb * t_new
    assert 8 % t_new == 0 and c % 8 == 0
    nlev, mall, role, mask = _gla_constants(c, t_new, with_rem=True)
    msum = jnp.asarray((np.arange(c)[None, :] // t_new == np.arange(sb)[:, None]).astype(np.float32), BF16)
    rows = lambda i: (i, 0)
    const = lambda i: (0, 0)
    st = lambda i: (i, 0, 0, 0)
    return pl.pallas_call(
        functools.partial(_gla_sample_kernel, nlev=nlev, t_new=t_new),
        grid=(batch // sb,),
        in_specs=[pl.BlockSpec((c, GLA_KEY_WIDTH), rows), pl.BlockSpec((c, GLA_KEY_WIDTH), rows),
                  pl.BlockSpec((c, GLA_VALUE_WIDTH), rows), pl.BlockSpec((c, GLA_KEY_WIDTH), rows),
                  pl.BlockSpec((c, GLA_VALUE_WIDTH), rows),
                  pl.BlockSpec((sb, GLA_HEADS, GLA_DK, GLA_DV), st),
                  pl.BlockSpec(mall.shape, const), pl.BlockSpec(role.shape, const), pl.BlockSpec(mask.shape, const),
                  pl.BlockSpec(msum.shape, const), pl.BlockSpec((1, GLA_DV), const)],
        out_specs=[pl.BlockSpec((c, GLA_VALUE_WIDTH), rows), pl.BlockSpec((sb, GLA_HEADS, GLA_DK, GLA_DV), st)],
        out_shape=[jax.ShapeDtypeStruct(vg.shape, BF16), jax.ShapeDtypeStruct(state.shape, F32)],
        compiler_params=_cparams(("parallel",)),
        name="gla_sample",
    )(qg, kg, vg, la, rg, state, mall, role, mask, msum, gnorm)


def _post_kernel(x_ref, a_ref, gl_ref, ga_ref, gb_ref, wpa, wpg, wo, nf_ref, wr_hi, wr_lo, br, x1t_o, rt_o):
    tm = x_ref.shape[0]
    part = POST_PART_ROWS if tm % POST_PART_ROWS == 0 else tm
    halves = [slice(r0, r0 + part) for r0 in range(0, tm, part)]
    proj =[(_dot(a_ref[r, :], wpa[...]), _dot(gl_ref[r, :], wpg[...])) for r in halves]
    x1s = []
    for r, (pa, pg) in zip(halves, proj):
        merged = ga_ref[r, :].astype(F32) * pa + gb_ref[r, :].astype(F32) * pg
        x1s.append(x_ref[r, :] + _dot(merged.astype(BF16), wo[...]))
    logits = []
    for r, x1 in zip(halves, x1s):
        rows = r.stop - r.start
        for j in range(TOKEN_ROWS):
            x1t_o[pl.ds(r.start * TOKEN_ROWS + j, rows, stride=TOKEN_ROWS), :] = x1[:, j * LANES:(j + 1) * LANES]
        h_hi, h_lo = _split_bf16(_rms(x1, nf_ref[...]))
        logits.append(_dot_nt(wr_hi[...], h_hi) + _dot_nt(wr_hi[...], h_lo) + _dot_nt(wr_lo[...], h_hi))
    nrow = ROUTER_ROWS
    big = jnp.int32(LANES)
    ninf = -jnp.inf
    for r, lt in zip(halves, logits):
        lt = lt[:nrow] + br[:nrow, 0:1]
        row = lax.broadcasted_iota(jnp.int32, lt.shape, 0)

        def first_max(vals):
            mx = jnp.max(vals, axis=0, keepdims=True)
            return mx, jnp.min(jnp.where(vals == mx, row, big), axis=0, keepdims=True)

        gl = jnp.where((row >= N_EXPERTS) & (row < N_EXPERTS + N_GROUPS), lt, ninf)
        gmax, gidx = first_max(gl)
        p_sel = 1.0 / jnp.sum(jnp.exp(gl - gmax), axis=0, keepdims=True)
        lo = (gidx - N_EXPERTS) * EXPERTS_PER_GROUP
        el = jnp.where((row >= lo) & (row < lo + EXPERTS_PER_GROUP), lt, ninf)
        v1, i1 = first_max(el)
        el2 = jnp.where(row == i1, ninf, el)
        v2, i2 = first_max(el2)
        t = jnp.exp(v2 - v1)
        w1 = p_sel / (1.0 + t)
        w2 = p_sel * t / (1.0 + t)
        row8 = lax.broadcasted_iota(jnp.int32, (rt_o.shape[0], r.stop - r.start), 0)
        pick = lambda k, val, rest: jnp.where(row8 == k, val, rest)
        rt_o[:, r] = pick(0, i1.astype(F32), pick(1, i2.astype(F32), pick(2, w1, pick(3, w2, 0.0))))


def _post(x2d, a_out, g_out, ga, gb, w, tm):
    n = x2d.shape[0]
    row = lambda i: (i, 0)
    const = lambda i: (0, 0)
    wnames = ("wpa", "wpg", "wo", "nf", "wr_hi", "wr_lo", "br")
    rt_rows = 8
    return pl.pallas_call(
        _post_kernel,
        grid=(n // tm,),
        in_specs=[pl.BlockSpec((tm, D_MODEL), row), pl.BlockSpec((tm, ATTN_WIDTH), row),
                  pl.BlockSpec((tm, GLA_VALUE_WIDTH), row), pl.BlockSpec((tm, D_MODEL), row),
                  pl.BlockSpec((tm, D_MODEL), row)] + [pl.BlockSpec(w[k].shape, const) for k in wnames],
        out_specs=[pl.BlockSpec((tm * TOKEN_ROWS, LANES), row),
                   pl.BlockSpec((rt_rows, tm), lambda i: (0, i))],
        out_shape=[jax.ShapeDtypeStruct((n * TOKEN_ROWS, LANES), F32), jax.ShapeDtypeStruct((rt_rows, n), F32)],
        compiler_params=_cparams(("parallel",)),
        name="post_mixer",
    )(x2d, a_out, g_out, ga, gb, *[w[k] for k in wnames])


def _moe_plan(rt, tme):
    n = rt.shape[1]
    ntiles = n // tme
    max_items = ntiles + N_GROUPS - 1
    i1, i2 = rt[0].astype(jnp.int32), rt[1].astype(jnp.int32)
    grp = i1 // EXPERTS_PER_GROUP
    lo = jnp.minimum(i1, i2) % EXPERTS_PER_GROUP
    hi = jnp.maximum(i1, i2) % EXPERTS_PER_GROUP
    snake = lambda l, h: jnp.where(l % 2 == 0, h, EXPERTS_PER_GROUP + l - h)
    key = (grp * EXPERTS_PER_GROUP + lo) * EXPERTS_PER_GROUP + snake(lo, hi)
    order = jnp.argsort(key, stable=True).astype(jnp.int32)
    skey = key[order].reshape(ntiles, tme)
    sg = skey // (EXPERTS_PER_GROUP * EXPERTS_PER_GROUP)
    slo = (skey // EXPERTS_PER_GROUP) % EXPERTS_PER_GROUP
    shi = snake(slo, skey % EXPERTS_PER_GROUP)
    ev = jnp.arange(EXPERTS_PER_GROUP)
    in_g = sg[:, :, None] == jnp.arange(N_GROUPS)
    uses_e = (slo[:, :, None] == ev) | (shi[:, :, None] == ev)
    flags_tge = jnp.any(in_g[:, :, :, None] & uses_e[:, :, None, :], axis=1)
    present = jnp.any(in_g, axis=1).reshape(-1)
    pos = jnp.cumsum(present) - 1
    n_items = pos[-1] + 1
    src = jnp.zeros((max_items,), jnp.int32).at[jnp.where(present, pos, max_items)].set(
        jnp.arange(ntiles * N_GROUPS, dtype=jnp.int32), mode="drop")
    it = jnp.arange(max_items)
    valid = it < n_items
    last_src = src[n_items - 1]
    src = jnp.where(valid, src, last_src)
    item_tile = src // N_GROUPS
    item_group = src % N_GROUPS
    prev_tile = jnp.concatenate([jnp.full((1,), -1, jnp.int32), item_tile[:-1]])
    next_tile = jnp.concatenate([item_tile[1:], jnp.full((1,), -1, jnp.int32)])
    first = valid & (item_tile != prev_tile)
    last = valid & ((item_tile != next_tile) | (it == n_items - 1))
    flags = flags_tge[item_tile, item_group] & valid[:, None]
    e0 = jnp.argmax(flags, axis=1)
    rest = flags & (ev[None, :] != e0[:, None])
    e1 = jnp.argmax(rest, axis=1)
    rest = rest & (ev[None, :] != e1[:, None])
    has2 = jnp.any(rest, axis=1)
    e2 = jnp.argmax(rest, axis=1)
    rest = rest & (ev[None, :] != e2[:, None])
    rt_sorted = rt[:, order].reshape(rt.shape[0], ntiles, tme).transpose(1, 0, 2)
    i32 = lambda z: z.astype(jnp.int32)
    plan = (order, i32(item_tile), i32(item_group), i32(first), i32(last), i32(valid), i32(e0), i32(e1), i32(e2),
            i32(has2), i32(rest.reshape(-1)))
    return plan, rt_sorted


def _moe_kernel(order, itile, igroup, ifirst, ilast, ivalid, ie0, ie1, ie2, ihas2, flags, x_hbm, rt_ref, wg, wu, wd, nffn, nfin,
                y_hbm, xbuf, ybuf, acc, hbuf, rcol, gsem, ssem, *, tme, ntiles):
    i = pl.program_id(0)
    t = itile[i]
    slot = t % 2
    g = igroup[i]
    is_first = ifirst[i] == 1

    def gather_row(tile, sl, r):
        tok = order[tile * tme + r]
        src = x_hbm.at[pl.ds(pl.multiple_of(tok * TOKEN_ROWS, TOKEN_ROWS), TOKEN_ROWS)]
        dst = xbuf.at[pl.ds(pl.multiple_of((sl * tme + r) * TOKEN_ROWS, TOKEN_ROWS), TOKEN_ROWS)]
        return pltpu.make_async_copy(src, dst, gsem.at[sl])

    def scatter_row(tile, sl, r):
        tok = order[tile * tme + r]
        return pltpu.make_async_copy(ybuf.at[sl, pl.ds(r, 1)], y_hbm.at[pl.ds(tok, 1)], ssem.at[sl])

    def start_rows(make, tile, sl):
        def body(r, c):
            make(tile, sl, r).start()
            return c
        lax.fori_loop(0, tme, body, 0, unroll=8)

    def wait_gather(sl):
        rows = tme * TOKEN_ROWS
        pltpu.make_async_copy(x_hbm.at[pl.ds(0, rows)], xbuf.at[pl.ds(pl.multiple_of(sl * rows, rows), rows)],
                              gsem.at[sl]).wait()

    def wait_scatter(sl):
        pltpu.make_async_copy(ybuf.at[sl], y_hbm.at[pl.ds(0, tme)], ssem.at[sl]).wait()

    def start_rows_inline(make, tile, sl, r0=0, r1=tme):
        for r in range(r0, r1):
            make(tile, sl, r).start()

    def expert(e):
        eid = (g * EXPERTS_PER_GROUP + e).astype(F32)
        ce = (jnp.where(rcol[:, 0:1] == eid, rcol[:, 2:3], 0.0)
              + jnp.where(rcol[:, 1:2] == eid, rcol[:, 3:4], 0.0))
        h = hbuf[...]
        a = _dot(h, wg[e])
        u = _dot(h, wu[e])
        act = (a * _sigmoid(a)) * u * ce
        acc[...] += _dot(act.astype(BF16), wd[e])

    def expert_with(e, run, alternatives):
        plain = run
        for cond, side_work in alternatives:
            @pl.when(cond)
            def _():
                side_work()
                expert(e)

            plain = jnp.logical_and(plain, jnp.logical_not(cond))

        @pl.when(plain)
        def _():
            expert(e)

    @pl.when(i == 0)
    def _():
        start_rows(gather_row, 0, 0)

    @pl.when(is_first)
    def _():
        wait_gather(slot)
        base = pl.multiple_of(slot * (tme * TOKEN_ROWS), tme * TOKEN_ROWS)
        x1 = jnp.concatenate([xbuf[pl.ds(base + j, tme, stride=TOKEN_ROWS), :] for j in range(TOKEN_ROWS)], axis=1)
        acc[...] = x1
        hbuf[...] = _rms(x1, nffn[...]).astype(BF16)
        rt = rt_ref[0]
        rcol[...] = jnp.transpose(jnp.concatenate([rt, jnp.zeros((LANES - rt.shape[0], tme), F32)], axis=0))

    valid = ivalid[i] == 1
    has2 = ihas2[i] == 1
    do_gather = jnp.logical_and(is_first, t + 1 < ntiles)
    do_scatter = jnp.logical_and(is_first, t >= 1)
    half = tme // 2
    scatter_rows = lambda r0, r1: (lambda: start_rows_inline(scatter_row, t - 1, 1 - slot, r0, r1))
    expert_with(ie0[i], valid, [(do_gather, lambda: start_rows_inline(gather_row, t + 1, 1 - slot))])
    expert_with(ie1[i], valid, [(jnp.logical_and(do_scatter, has2), scatter_rows(0, half)),
                                (jnp.logical_and(do_scatter, jnp.logical_not(has2)), scatter_rows(0, tme))])
    expert_with(ie2[i], has2, [(jnp.logical_and(do_scatter, has2), scatter_rows(half, tme))])
    for e in range(EXPERTS_PER_GROUP):
        @pl.when(flags[i * EXPERTS_PER_GROUP + e] == 1)
        def _():
            expert(e)

    @pl.when(ilast[i] == 1)
    def _():
        @pl.when(t >= 2)
        def _():
            wait_scatter(slot)

        ybuf[slot] = _rms(acc[...], nfin[...])

        @pl.when(t == ntiles - 1)
        def _():
            start_rows(scatter_row, t, slot)

    @pl.when(i == pl.num_programs(0) - 1)
    def _():
        for sl in range(min(2, ntiles)):
            wait_scatter(sl)


def _moe(x1t, rt, weg, weu, wed, nffn, nfin):
    n = x1t.shape[0] // TOKEN_ROWS
    tme = min(MOE_TILE, n)
    assert n % tme == 0
    ntiles = n // tme
    plan, rt_sorted = _moe_plan(rt, tme)
    max_items = ntiles + N_GROUPS - 1
    grp = lambda i, order, itile, igroup, *_: (igroup[i], 0, 0)
    til = lambda i, order, itile, *_: (itile[i], 0, 0)
    const = lambda i, *_: (0, 0)
    grid_spec = pltpu.PrefetchScalarGridSpec(
        num_scalar_prefetch=len(plan),
        grid=(max_items,),
        in_specs=[pl.BlockSpec(memory_space=pl.ANY),
                  pl.BlockSpec((1,) + rt_sorted.shape[1:], til),
                  pl.BlockSpec((EXPERTS_PER_GROUP, D_MODEL, EXPERT_FF), grp),
                  pl.BlockSpec((EXPERTS_PER_GROUP, D_MODEL, EXPERT_FF), grp),
                  pl.BlockSpec((EXPERTS_PER_GROUP, EXPERT_FF, D_MODEL), grp),
                  pl.BlockSpec((1, D_MODEL), const), pl.BlockSpec((1, D_MODEL), const)],
        out_specs=pl.BlockSpec(memory_space=pl.ANY),
        scratch_shapes=[pltpu.VMEM((2 * tme * TOKEN_ROWS, LANES), F32), pltpu.VMEM((2, tme, D_MODEL), F32),
                        pltpu.VMEM((tme, D_MODEL), F32), pltpu.VMEM((tme, D_MODEL), BF16),
                        pltpu.VMEM((tme, LANES), F32),
                        pltpu.SemaphoreType.DMA((2,)), pltpu.SemaphoreType.DMA((2,))],
    )
    return pl.pallas_call(
        functools.partial(_moe_kernel, tme=tme, ntiles=ntiles),
        grid_spec=grid_spec,
        out_shape=jax.ShapeDtypeStruct((n, D_MODEL), F32),
        compiler_params=_cparams(("arbitrary",)),
        name="moe",
    )(*plan, x1t, rt_sorted, weg, weu, wed, nffn, nfin)


def _rope_tables(positions):
    half = HEAD_DIM // 2
    inv_freq = ROPE_THETA ** (-jnp.arange(half, dtype=F32) / half)
    ang = positions.astype(F32)[:, None] * inv_freq[None, :]
    cos, sin = jnp.cos(ang), jnp.sin(ang)
    reps = LANES // HEAD_DIM
    return (jnp.tile(jnp.concatenate([cos, cos], -1), (1, reps)),
            jnp.tile(jnp.concatenate([-sin, sin], -1), (1, reps)))


def _prep_weights(norm_mix, w_in, w_gla_f2, b_gla_f, gla_norm, w_proj_attn, w_proj_gla, w_out, norm_ffn,
                  w_router_group, b_router_group, w_router_expert, b_router_expert):
    w = {"win": w_in.astype(BF16)}
    w["wf2"] = jnp.pad(w_gla_f2.astype(BF16), ((0, LANES - GLA_GATE_RANK), (0, 0)))
    w["bf"] = b_gla_f.reshape(1, -1)
    w["norm_mix"] = norm_mix.reshape(1, -1)
    w["gla_norm"] = gla_norm.reshape(1, -1)
    w["wpa"] = w_proj_attn.astype(BF16)
    w["wpg"] = w_proj_gla.astype(BF16)
    w["wo"] = w_out.astype(BF16)
    w["nf"] = norm_ffn.reshape(1, -1)
    pad = LANES - N_EXPERTS - N_GROUPS
    wr_t = jnp.pad(jnp.concatenate([w_router_expert, w_router_group], axis=1), ((0, 0), (0, pad))).T
    w["wr_hi"], w["wr_lo"] = _split_bf16(wr_t)
    w["br"] = jnp.broadcast_to(jnp.pad(jnp.concatenate([b_router_expert, b_router_group]), (0, pad))[:, None],
                               (LANES, LANES))
    return w


def _layer(x, positions_tab, cache, w, sinks, weg, weu, wed, nfin, tm):
    batch, seq, _ = x.shape
    n = batch * seq
    x2d = x.reshape(n, D_MODEL)
    cos_t, sin_t = positions_tab
    qa, ka, va, qg, kg, vg, rg, la, ga, gb = _inproj(x2d, w["norm_mix"], cos_t, sin_t, w, tm)
    if cache is None:
        a_out = _swa_prompt(sinks, qa, ka, va, batch, seq)
        last = lambda z: z.reshape(batch, seq, KV_WIDTH)[:, seq - WINDOW:].reshape(batch, WINDOW, ATTN_KV_HEADS, HEAD_DIM)
        new_k, new_v = last(ka), last(va)
        g_out, new_s = _gla_prompt(qg, kg, vg, la, rg, w["gla_norm"], batch, seq)
    else:
        cache_k, cache_v, state = cache
        a_out, new_k, new_v = _swa_sample(sinks, qa, ka, va, cache_k.reshape(batch, WINDOW, KV_WIDTH),
                                          cache_v.reshape(batch, WINDOW, KV_WIDTH), batch, seq)
        new_k = new_k.reshape(batch, WINDOW, ATTN_KV_HEADS, HEAD_DIM)
        new_v = new_v.reshape(batch, WINDOW, ATTN_KV_HEADS, HEAD_DIM)
        g_out, new_s = _gla_sample(qg, kg, vg, la, rg, state, w["gla_norm"], batch, seq)
    x1t, rt = _post(x2d, a_out, g_out, ga, gb, w, tm * 2 if x2d.shape[0] % (tm * 2) == 0 else tm)
    y = _moe(x1t, rt, weg, weu, wed, w["nf"], nfin)
    return y.reshape(batch, seq, D_MODEL), new_k, new_v, new_s


def kernel(x_prompt, x_sample, cache_win_k, cache_win_v, state_gla, norm_mix, w_in, w_gla_f2, b_gla_f, gla_norm,
           attn_sinks, w_proj_attn, w_proj_gla, w_out, norm_ffn, w_router_group, b_router_group, w_router_expert,
           b_router_expert, w_exp_gate, w_exp_up, w_exp_down, norm_final):
    assert norm_mix.shape[0] == 1, "single-layer step"
    seq_p = x_prompt.shape[1]
    dec_b, dec_t = x_sample.shape[0], x_sample.shape[1]
    w = _prep_weights(norm_mix[0], w_in[0], w_gla_f2[0], b_gla_f[0], gla_norm[0], w_proj_attn[0], w_proj_gla[0],
                      w_out[0], norm_ffn[0], w_router_group[0], b_router_group[0], w_router_expert[0],
                      b_router_expert[0])
    weg = w_exp_gate[0].astype(BF16)
    weu = w_exp_up[0].astype(BF16)
    wed = w_exp_down[0].astype(BF16)
    nfin = norm_final.reshape(1, -1)
    sinks = attn_sinks[0]
    tab_p = _rope_tables(jnp.arange(seq_p, dtype=jnp.int32))
    pos_s = PAST_LEN + jnp.arange(dec_t, dtype=jnp.int32)
    tab_s = tuple(jnp.tile(t, (dec_b, 1)) for t in _rope_tables(pos_s))
    tm_p = min(512, seq_p)
    tm_s = dec_b * dec_t
    yp, pk, pv, ps = _layer(x_prompt, tab_p, None, w, sinks, weg, weu, wed, nfin, tm_p)
    ys, sk, sv, ss = _layer(x_sample, tab_s, (cache_win_k[0], cache_win_v[0], state_gla[0]), w, sinks, weg, weu, wed,
                            nfin, tm_s)
    return (yp, ys, pk[None], pv[None], ps[None], sk[None], sv[None], ss[None])
```

```python
import functools
import math

import numpy as np
import jax
import jax.numpy as jnp
from jax import lax
from jax.experimental import pallas as pl
from jax.experimental.pallas import tpu as pltpu

F32 = jnp.float32
BF16 = jnp.bfloat16

D_MODEL = 1024
ATTN_HEADS = 8
ATTN_KV_HEADS = 2
GROUP = ATTN_HEADS // ATTN_KV_HEADS
HEAD_DIM = 64
ATTN_WIDTH = ATTN_HEADS * HEAD_DIM
KV_WIDTH = ATTN_KV_HEADS * HEAD_DIM
WINDOW = 128
ROPE_THETA = 10000.0
PAST_LEN = 8192
GLA_HEADS = 4
GLA_KEY_WIDTH = D_MODEL // 2
GLA_VALUE_WIDTH = D_MODEL
GLA_DK = GLA_KEY_WIDTH // GLA_HEADS
GLA_DV = GLA_VALUE_WIDTH // GLA_HEADS
GLA_GATE_RANK = 16
GLA_GATE_NORMALIZER = 16.0
N_GROUPS = 4
EXPERTS_PER_GROUP = 8
N_EXPERTS = N_GROUPS * EXPERTS_PER_GROUP
EXPERT_FF = 256
EPS = 1e-6
LOG2_E = 1.4426950408889634

LANES = 128
GLA_CHUNK = 128
GLA_HEADS_PER_STEP = 4
SWA_BLOCKS_PER_STEP = 4
SAMPLE_SEQ_BLOCK = 16
POST_PART_ROWS = 512
MOE_TILE = 256
TOKEN_ROWS = D_MODEL // LANES
INPROJ_WIDTHS = (ATTN_WIDTH, 2 * KV_WIDTH, GLA_KEY_WIDTH, GLA_KEY_WIDTH, GLA_VALUE_WIDTH, GLA_VALUE_WIDTH)
ROUTER_ROWS = 40
VMEM_LIMIT = 56 * 1024 * 1024


def _cparams(sem):
    return pltpu.CompilerParams(dimension_semantics=sem, vmem_limit_bytes=VMEM_LIMIT)


def _rms(x, g):
    return x * lax.rsqrt(jnp.mean(x * x, axis=-1, keepdims=True) + EPS) * g


def _sigmoid(x):
    return 1.0 / (1.0 + jnp.exp(-x))


def _dot(a, b):
    return jnp.dot(a, b, preferred_element_type=F32)


def _dot_nt(a, b):
    return lax.dot_general(a, b, (((1,), (1,)), ((), ())), preferred_element_type=F32)


def _dot_tn(a, b):
    return lax.dot_general(a, b, (((0,), (0,)), ((), ())), preferred_element_type=F32)


def _split_bf16(x):
    hi = x.astype(BF16)
    lo = (x - hi.astype(F32)).astype(BF16)
    return hi, lo


def _inproj_kernel(x_ref, g_ref, cos_ref, sin_ref, win, wf2, bf,
                   qa_o, ka_o, va_o, qg_o, kg_o, vg_o, rg_o, la_o, ga_o, gb_o, wgate):
    f0 = sum(INPROJ_WIDTHS)

    @pl.when(pl.program_id(0) == 0)
    def _():
        tail = win[:, f0:]
        wgate[...] = tail[:, GLA_GATE_RANK:GLA_GATE_RANK + 2 * D_MODEL]

    tm = x_ref.shape[0]
    halves = [slice(0, tm // 2), slice(tm // 2, tm)] if tm % 16 == 0 else [slice(0, tm)]
    hbs = [_rms(x_ref[r, :], g_ref[...]).astype(BF16) for r in halves]
    cols = np.cumsum((0,) + INPROJ_WIDTHS[:-1])
    wqa, wkva, wqg, wkg, wvg, wrg = (win.at[:, int(c):int(c) + wd] for c, wd in zip(cols, INPROJ_WIDTHS))
    wga = wgate.at[:, :D_MODEL]
    wgb = wgate.at[:, D_MODEL:]
    wf = win.at[:, f0:f0 + LANES]
    lane = lax.broadcasted_iota(jnp.int32, (halves[0].stop - halves[0].start, LANES), 1)
    first_half = (lane % HEAD_DIM) < (HEAD_DIM // 2)

    def rope(t, r):
        swapped = jnp.where(first_half, pltpu.roll(t, LANES - HEAD_DIM // 2, 1), pltpu.roll(t, HEAD_DIM // 2, 1))
        return t * cos_ref[r, :] + swapped * sin_ref[r, :]

    for r, hb in zip(halves, hbs):
        ga_o[r, :] = _sigmoid(_dot(hb, wga[...])).astype(BF16)
    for r, hb in zip(halves, hbs):
        gb_o[r, :] = _sigmoid(_dot(hb, wgb[...])).astype(BF16)
    for r, hb in zip(halves, hbs):
        rz = _dot(hb, wrg[...])
        rg_o[r, :] = (rz * _sigmoid(rz)).astype(BF16)
    for r, hb in zip(halves, hbs):
        z = _dot(_dot(hb, wf[...]).astype(BF16), wf2[...]) + bf[...]
        la_o[r, :] = (jnp.minimum(z, 0.0) - jnp.log1p(jnp.exp(-jnp.abs(z)))) * (1.0 / GLA_GATE_NORMALIZER)
    for r, hb in zip(halves, hbs):
        qa = _dot(hb, wqa[...])
        for c in range(ATTN_WIDTH // LANES):
            sl = slice(c * LANES, (c + 1) * LANES)
            qa_o[r, sl] = (rope(qa[:, sl], r) * (HEAD_DIM ** -0.5)).astype(BF16)
    for r, hb in zip(halves, hbs):
        kva = _dot(hb, wkva[...])
        ka_o[r, :] = rope(kva[:, :KV_WIDTH], r)
        va_o[r, :] = kva[:, KV_WIDTH:]
    for r, hb in zip(halves, hbs):
        qg_o[r, :] = (_dot(hb, wqg[...]) * (GLA_DK ** -0.5)).astype(BF16)
    for r, hb in zip(halves, hbs):
        kg_o[r, :] = _dot(hb, wkg[...]).astype(BF16)
    for r, hb in zip(halves, hbs):
        vg_o[r, :] = _dot(hb, wvg[...]).astype(BF16)


def _inproj(x2d, norm_g, cos_t, sin_t, w, tm):
    n = x2d.shape[0]
    ntab = cos_t.shape[0] // tm
    row = lambda i: (i, 0)
    const = lambda i: (0, 0)
    tab = lambda i: (i % ntab, 0)
    wnames = ("win", "wf2", "bf")
    wspecs = [pl.BlockSpec(w[k].shape, const) for k in wnames]
    widths = (ATTN_WIDTH, KV_WIDTH, KV_WIDTH, GLA_KEY_WIDTH, GLA_KEY_WIDTH, GLA_VALUE_WIDTH, GLA_VALUE_WIDTH,
              GLA_KEY_WIDTH, D_MODEL, D_MODEL)
    dtypes = (BF16, F32, F32, BF16, BF16, BF16, BF16, F32, BF16, BF16)
    return pl.pallas_call(
        _inproj_kernel,
        grid=(n // tm,),
        in_specs=[pl.BlockSpec((tm, D_MODEL), row), pl.BlockSpec((1, D_MODEL), const),
                  pl.BlockSpec((tm, LANES), tab), pl.BlockSpec((tm, LANES), tab)] + wspecs,
        out_specs=[pl.BlockSpec((tm, wd), row) for wd in widths],
        out_shape=[jax.ShapeDtypeStruct((n, wd), dt) for wd, dt in zip(widths, dtypes)],
        scratch_shapes=[pltpu.VMEM((D_MODEL, 2 * D_MODEL), BF16)],
        compiler_params=_cparams(("arbitrary",)),
        name="inproj",
    )(x2d, norm_g, cos_t, sin_t, *[w[k] for k in wnames])


def _swa_prompt_kernel(sink_ref, q_ref, kc_ref, kp_ref, vc_ref, vp_ref, o_ref, *, qb):
    n = pl.program_id(1)
    k3 = jnp.concatenate([kp_ref[...], kc_ref[...]], axis=0).astype(BF16)
    v3 = jnp.concatenate([vp_ref[...], vc_ref[...]], axis=0).astype(BF16)
    t = lax.broadcasted_iota(jnp.int32, (WINDOW, 2 * WINDOW), 0)
    j = lax.broadcasted_iota(jnp.int32, (WINDOW, 2 * WINDOW), 1)
    band = (j >= t) & (j <= t + WINDOW)
    zeros = jnp.zeros((2 * WINDOW, HEAD_DIM), BF16)
    ones = jnp.ones((2 * WINDOW, HEAD_DIM), BF16)
    lane = lax.broadcasted_iota(jnp.int32, (WINDOW, LANES), 1)
    chains = [(blk, h) for blk in range(qb) for h in range(ATTN_HEADS)]
    scores = []
    for blk, h in chains:
        kv = h // GROUP
        rows = slice(blk * WINDOW, (blk + 1) * WINDOW)
        keys = slice(blk * WINDOW, (blk + 2) * WINDOW)
        s = _dot_nt(q_ref[rows, h * HEAD_DIM:(h + 1) * HEAD_DIM], k3[keys, kv * HEAD_DIM:(kv + 1) * HEAD_DIM])
        valid = band & ((j >= WINDOW) | (n > 0)) if blk == 0 else band
        scores.append(jnp.where(valid, s, -jnp.inf))
    probs, sink_terms = [], []
    for (blk, h), s in zip(chains, scores):
        m = jnp.maximum(jnp.max(s, axis=-1, keepdims=True), sink_ref[h])
        probs.append(jnp.exp(s - m).astype(BF16))
        sink_terms.append(jnp.exp(sink_ref[h] - m))
    for blk in range(qb):
        rows = slice(blk * WINDOW, (blk + 1) * WINDOW)
        keys = slice(blk * WINDOW, (blk + 2) * WINDOW)
        for kv in range(ATTN_KV_HEADS):
            vv = v3[keys, kv * HEAD_DIM:(kv + 1) * HEAD_DIM]
            vext = (jnp.concatenate([vv, zeros, ones, zeros], axis=1), jnp.concatenate([zeros, vv, zeros, ones], axis=1))
            for pr in range(GROUP // 2):
                h0 = kv * GROUP + pr * 2
                c0 = blk * ATTN_HEADS + h0
                acc = _dot(probs[c0], vext[0]) + _dot(probs[c0 + 1], vext[1])
                l = acc[:, LANES:] + jnp.where(lane < HEAD_DIM, sink_terms[c0], sink_terms[c0 + 1])
                o_ref[rows, h0 * HEAD_DIM:h0 * HEAD_DIM + LANES] = (acc[:, :LANES] / l).astype(BF16)


def _swa_prompt(sinks, qa, ka, va, batch, seq):
    nb = seq // WINDOW
    qb = math.gcd(SWA_BLOCKS_PER_STEP, nb)
    steps = nb // qb
    cur = lambda b, n: (b * steps + n, 0)
    prev = lambda b, n: (b * nb + jnp.maximum(n * qb - 1, 0), 0)
    return pl.pallas_call(
        functools.partial(_swa_prompt_kernel, qb=qb),
        grid=(batch, steps),
        in_specs=[pl.BlockSpec(memory_space=pltpu.SMEM),
                  pl.BlockSpec((qb * WINDOW, ATTN_WIDTH), cur),
                  pl.BlockSpec((qb * WINDOW, KV_WIDTH), cur), pl.BlockSpec((WINDOW, KV_WIDTH), prev),
                  pl.BlockSpec((qb * WINDOW, KV_WIDTH), cur), pl.BlockSpec((WINDOW, KV_WIDTH), prev)],
        out_specs=pl.BlockSpec((qb * WINDOW, ATTN_WIDTH), cur),
        out_shape=jax.ShapeDtypeStruct(qa.shape, BF16),
        compiler_params=_cparams(("parallel", "parallel")),
        name="swa_prompt",
    )(sinks, qa, ka, ka, va, va)


def _swa_sample_kernel(sink_ref, q_ref, kn_ref, vn_ref, kn3_ref, vn3_ref, ck_ref, cv_ref, o_ref, nk_ref, nv_ref,
                       *, t_new):
    sb = ck_ref.shape[0]
    spv = 8 // t_new
    nq = GROUP * 8
    nc = spv * WINDOW
    qi = lax.broadcasted_iota(jnp.int32, (nq, nc), 0) % 8
    ci = lax.broadcasted_iota(jnp.int32, (nq, nc), 1)
    valid_c = (qi // t_new == ci // WINDOW) & (ci % WINDOW >= qi % t_new)
    qn = lax.broadcasted_iota(jnp.int32, (nq, 8), 0) % 8
    cn = lax.broadcasted_iota(jnp.int32, (nq, 8), 1)
    valid_n = (qn // t_new == cn // t_new) & (cn <= qn)
    grow = lax.broadcasted_iota(jnp.int32, (nq, 1), 0) // 8
    chains = [(vr, kv) for vr in range(sb // spv) for kv in range(ATTN_KV_HEADS)]
    scored = []
    for vr, kv in chains:
        r8 = slice(8 * vr, 8 * vr + 8)
        cs = slice(kv * HEAD_DIM, (kv + 1) * HEAD_DIM)
        heads = [kv * GROUP + g for g in range(GROUP)]
        qs = jnp.concatenate([q_ref[r8, h * HEAD_DIM:(h + 1) * HEAD_DIM] for h in heads], axis=0)
        ck = ck_ref[vr * spv:(vr + 1) * spv, :, cs].reshape(nc, HEAD_DIM).astype(BF16)
        kn = kn_ref[r8, cs].astype(BF16)
        scored.append((jnp.where(valid_c, _dot_nt(qs, ck), -jnp.inf), jnp.where(valid_n, _dot_nt(qs, kn), -jnp.inf)))
    soft = []
    for (vr, kv), (s_c, s_n) in zip(chains, scored):
        sink = sink_ref[kv * GROUP]
        for g in range(1, GROUP):
            sink = jnp.where(grow == g, sink_ref[kv * GROUP + g], sink)
        m = jnp.maximum(jnp.maximum(jnp.max(s_c, axis=-1, keepdims=True), jnp.max(s_n, axis=-1, keepdims=True)), sink)
        p_c = jnp.exp(s_c - m)
        p_n = jnp.exp(s_n - m)
        l = jnp.sum(p_c, axis=-1, keepdims=True) + jnp.sum(p_n, axis=-1, keepdims=True) + jnp.exp(sink - m)
        soft.append((p_c.astype(BF16), p_n.astype(BF16), l))
    for (vr, kv), (p_c, p_n, l) in zip(chains, soft):
        r8 = slice(8 * vr, 8 * vr + 8)
        cs = slice(kv * HEAD_DIM, (kv + 1) * HEAD_DIM)
        cv = cv_ref[vr * spv:(vr + 1) * spv, :, cs].reshape(nc, HEAD_DIM).astype(BF16)
        o = (_dot(p_c, cv) + _dot(p_n, vn_ref[r8, cs].astype(BF16))) / l
        for a in range(GROUP // 2):
            pair = jnp.concatenate([o[16 * a:16 * a + 8], o[16 * a + 8:16 * a + 16]], axis=1)
            c0 = (kv * GROUP + 2 * a) * HEAD_DIM
            o_ref[r8, c0:c0 + LANES] = pair.astype(BF16)
    nk_ref[:, 0:WINDOW - t_new, :] = ck_ref[:, t_new:WINDOW, :]
    nk_ref[:, WINDOW - t_new:WINDOW, :] = kn3_ref[...]
    nv_ref[:, 0:WINDOW - t_new, :] = cv_ref[:, t_new:WINDOW, :]
    nv_ref[:, WINDOW - t_new:WINDOW, :] = vn3_ref[...]


def _swa_sample(sinks, qa, ka, va, cache_k, cache_v, batch, t_new):
    sb = SAMPLE_SEQ_BLOCK
    rows = sb * t_new
    r2 = lambda i: (i, 0)
    r3 = lambda i: (i, 0, 0)
    ka3 = ka.reshape(batch, t_new, KV_WIDTH)
    va3 = va.reshape(batch, t_new, KV_WIDTH)
    return pl.pallas_call(
        functools.partial(_swa_sample_kernel, t_new=t_new),
        grid=(batch // sb,),
        in_specs=[pl.BlockSpec(memory_space=pltpu.SMEM),
                  pl.BlockSpec((rows, ATTN_WIDTH), r2),
                  pl.BlockSpec((rows, KV_WIDTH), r2), pl.BlockSpec((rows, KV_WIDTH), r2),
                  pl.BlockSpec((sb, t_new, KV_WIDTH), r3), pl.BlockSpec((sb, t_new, KV_WIDTH), r3),
                  pl.BlockSpec((sb, WINDOW, KV_WIDTH), r3), pl.BlockSpec((sb, WINDOW, KV_WIDTH), r3)],
        out_specs=[pl.BlockSpec((rows, ATTN_WIDTH), r2),
                   pl.BlockSpec((sb, WINDOW, KV_WIDTH), r3), pl.BlockSpec((sb, WINDOW, KV_WIDTH), r3)],
        out_shape=[jax.ShapeDtypeStruct(qa.shape, BF16),
                   jax.ShapeDtypeStruct(cache_k.shape, F32), jax.ShapeDtypeStruct(cache_v.shape, F32)],
        compiler_params=_cparams(("parallel",)),
        name="swa_sample",
    )(sinks, qa, ka, va, ka3, va3, cache_k, cache_v)


def _gla_constants(c, seg, with_rem):
    t = np.arange(c)
    sid = t // seg
    same = sid[:, None] == sid[None, :]
    levels = []
    m = seg // 2
    while m >= 1:
        levels.append(m)
        m //= 2
    mats, roles, masks = [], [], []
    for m in levels:
        blk = t // (2 * m)
        second = (t // m) % 2 == 1
        p = blk * 2 * m + m - 1
        u = t[None, :]
        mq = (u > p[:, None]) & (u <= t[:, None])
        mk = (u > t[:, None]) & (u <= p[:, None])
        if m > 1:
            mats.append(np.where(second[:, None], mq, mk))
        roles.append(np.broadcast_to(second[:, None], (c, LANES)))
        masks.append((blk[:, None] == blk[None, :]) & second[:, None] & ~second[None, :])
    masks.append(np.eye(c, dtype=bool))
    mats.append(same & (t[None, :] <= t[:, None]))
    if with_rem:
        mats.append(same & (t[None, :] > t[:, None]))
    mall = np.concatenate(mats, 0).astype(np.float32)
    mall = jnp.asarray(np.concatenate([mall, mall], 1), BF16)
    role = jnp.asarray(np.concatenate(roles, 0).astype(np.float32))
    mask = jnp.asarray(np.concatenate(masks, 0).astype(np.float32))
    return len(levels), mall, role, mask


def _gla_exponents(la, mall):
    la2 = la * LOG2_E
    hl = jnp.concatenate(_split_bf16(la2), axis=0)
    return _dot(mall, hl), hl, la2


def _gla_scores(qb, kb, e2, la2, role_ref, mask_ref, nlev, c):
    qf = qb.astype(F32)
    kf = kb.astype(F32)
    terms = []
    for lv in range(nlev):
        sl = slice(lv * c, (lv + 1) * c)
        m = 1 << (nlev - 1 - lv)
        if m % 8 == 0:
            pe = jnp.exp2(e2[sl])
            x = jnp.concatenate([(qf if blk % 2 else kf)[blk * m:(blk + 1) * m] * pe[blk * m:(blk + 1) * m]
                                 for blk in range(c // m)], axis=0).astype(BF16)
        else:
            second = role_ref[sl, :] > 0.5
            e = e2[sl] if lv < nlev - 1 else jnp.where(second, la2, 0.0)
            x = (jnp.where(second, qf, kf) * jnp.exp2(e)).astype(BF16)
        terms.append((x, x, lv))
    terms.append((qb, kb, nlev))
    mask = lambda i: mask_ref[i * c:(i + 1) * c, :]
    a = None
    pair = c % LANES == 0
    while terms:
        if pair and len(terms) >= 2:
            (l0, r0, i0), (l1, r1, i1) = terms.pop(), terms.pop()
            z = jnp.zeros_like(r0)
            rhs = jnp.concatenate([jnp.concatenate([r0, z], axis=1), jnp.concatenate([z, r1], axis=1)], axis=0)
            g = _dot_nt(jnp.concatenate([l0, l1], axis=1), rhs)
            t = mask(i0) * g[:, :c] + mask(i1) * g[:, c:]
        else:
            l0, r0, i0 = terms.pop()
            t = mask(i0) * _dot_nt(l0, r0)
        a = t if a is None else a + t
    return a, qf, kf


def _gla_out(o, g, gate):
    return (_rms(o, g) * gate.astype(F32)).astype(BF16)


def _gla_prompt_kernel(q_ref, k_ref, v_ref, la_ref, r_ref, mall_ref, role_ref, mask_ref, g_ref, o_ref, s_ref,
                       s_scr, *, nlev, nchunks):
    c = GLA_CHUNK
    hp = GLA_HEADS_PER_STEP
    s_scr[...] = jnp.zeros_like(s_scr)

    def chunk(i, carry):
        rows = pl.ds(pl.multiple_of(i * c, c), c)
        e2_all, _, la2_all = _gla_exponents(la_ref[rows, :], mall_ref[...])
        ksl = [slice(h * GLA_DK, (h + 1) * GLA_DK) for h in range(hp)]
        vsl = [slice(h * GLA_DV, (h + 1) * GLA_DV) for h in range(hp)]
        scores = [_gla_scores(q_ref[rows, ksl[h]], k_ref[rows, ksl[h]], e2_all[:, ksl[h]], la2_all[:, ksl[h]], role_ref,
                              mask_ref, nlev, c)[0] for h in range(hp)]
        for h in range(hp):
            b = e2_all[(nlev - 1) * c:nlev * c, ksl[h]]
            qf = q_ref[rows, ksl[h]].astype(F32)
            o = (_dot(scores[h].astype(BF16), v_ref[rows, vsl[h]])
                 + _dot((qf * jnp.exp2(b)).astype(BF16), s_scr[h].astype(BF16)))
            o_ref[rows, vsl[h]] = _gla_out(o, g_ref[...], r_ref[rows, vsl[h]])
        for h in range(hp):
            b = e2_all[(nlev - 1) * c:nlev * c, ksl[h]]
            kf = k_ref[rows, ksl[h]].astype(F32)
            kt = (kf * jnp.exp2(b[c - 1:c, :] - b)).astype(BF16)
            dec = jnp.exp2(jnp.transpose(b[c - 8:c, :]))[:, 7:8]
            s_scr[h] = dec * s_scr[h] + _dot_tn(kt, v_ref[rows, vsl[h]])
        return carry

    lax.fori_loop(0, nchunks, chunk, 0, unroll=4)
    s_ref[0] = s_scr[...]


def _gla_prompt(qg, kg, vg, la, rg, gnorm, batch, seq):
    nlev, mall, role, mask = _gla_constants(GLA_CHUNK, GLA_CHUNK, with_rem=False)
    hp = GLA_HEADS_PER_STEP
    bh = lambda b, h: (b, h)
    const = lambda b, h: (0, 0)
    return pl.pallas_call(
        functools.partial(_gla_prompt_kernel, nlev=nlev, nchunks=seq // GLA_CHUNK),
        grid=(batch, GLA_HEADS // hp),
        in_specs=[pl.BlockSpec((seq, hp * GLA_DK), bh), pl.BlockSpec((seq, hp * GLA_DK), bh),
                  pl.BlockSpec((seq, hp * GLA_DV), bh), pl.BlockSpec((seq, hp * GLA_DK), bh),
                  pl.BlockSpec((seq, hp * GLA_DV), bh),
                  pl.BlockSpec(mall.shape, const), pl.BlockSpec(role.shape, const), pl.BlockSpec(mask.shape, const),
                  pl.BlockSpec((1, GLA_DV), const)],
        out_specs=[pl.BlockSpec((seq, hp * GLA_DV), bh),
                   pl.BlockSpec((1, hp, GLA_DK, GLA_DV), lambda b, h: (b, h, 0, 0))],
        out_shape=[jax.ShapeDtypeStruct(vg.shape, BF16),
                   jax.ShapeDtypeStruct((batch, GLA_HEADS, GLA_DK, GLA_DV), F32)],
        scratch_shapes=[pltpu.VMEM((hp, GLA_DK, GLA_DV), F32)],
        compiler_params=_cparams(("parallel", "parallel")),
        name="gla_prompt",
    )(qg, kg, vg, la, rg, mall, role, mask, gnorm)


def _gla_sample_kernel(q_ref, k_ref, v_ref, la_ref, r_ref, s0_ref, mall_ref, role_ref, mask_ref, msum_ref, g_ref,
                       o_ref, s_ref, *, nlev, t_new):
    sb = s0_ref.shape[0]
    c = sb * t_new
    spv = 8 // t_new
    e2_all, hl_all, la2_all = _gla_exponents(la_ref[...], mall_ref[...])
    seq_in_tile = lax.broadcasted_iota(jnp.int32, (8, GLA_DV), 0) // t_new
    seq_of_row = lax.broadcasted_iota(jnp.int32, (c, GLA_DV), 0) // t_new
    ksl = [slice(h * GLA_DK, (h + 1) * GLA_DK) for h in range(GLA_HEADS)]
    scores = [_gla_scores(q_ref[:, ksl[h]], k_ref[:, ksl[h]], e2_all[:, ksl[h]], la2_all[:, ksl[h]], role_ref, mask_ref,
                          nlev, c) for h in range(GLA_HEADS)]
    for h in range(GLA_HEADS):
        ks = ksl[h]
        vs = slice(h * GLA_DV, (h + 1) * GLA_DV)
        v = v_ref[:, vs]
        e2 = e2_all[:, ks]
        hl = hl_all[:, ks]
        a, qf, kf = scores[h]
        qe = (qf * jnp.exp2(e2[(nlev - 1) * c:nlev * c])).astype(BF16)
        kt_t = jnp.transpose(kf * jnp.exp2(e2[nlev * c:(nlev + 1) * c])).astype(BF16)
        bl_t = jnp.transpose(_dot(msum_ref[...], hl[:c]) + _dot(msum_ref[...], hl[c:]))
        dec_t = jnp.exp2(bl_t)
        inter = []
        for vr in range(c // 8):
            rows8 = qe[8 * vr:8 * vr + 8]
            tile = None
            for u in range(spv):
                j = vr * spv + u
                s0 = s0_ref[j, h]
                r = _dot(rows8, s0.astype(BF16))
                tile = r if tile is None else jnp.where(seq_in_tile == u, r, tile)
                vj = jnp.where(seq_of_row == j, v, jnp.zeros_like(v))
                s_ref[j, h] = dec_t[:, j:j + 1] * s0 + _dot(kt_t, vj)
            inter.append(tile)
        o = _dot(a.astype(BF16), v) + jnp.concatenate(inter, axis=0)
        o_ref[:, vs] = _gla_out(o, g_ref[...], r_ref[:, vs])


def _gla_sample(qg, kg, vg, la, rg, state, gnorm, batch, t_new):
    sb = SAMPLE_SEQ_BLOCK
    c = sb * t_new
    assert 8 % t_new == 0 and c % 8 == 0
    nlev, mall, role, mask = _gla_constants(c, t_new, with_rem=True)
    msum = jnp.asarray((np.arange(c)[None, :] // t_new == np.arange(sb)[:, None]).astype(np.float32), BF16)
    rows = lambda i: (i, 0)
    const = lambda i: (0, 0)
    st = lambda i: (i, 0, 0, 0)
    return pl.pallas_call(
        functools.partial(_gla_sample_kernel, nlev=nlev, t_new=t_new),
        grid=(batch // sb,),
        in_specs=[pl.BlockSpec((c, GLA_KEY_WIDTH), rows), pl.BlockSpec((c, GLA_KEY_WIDTH), rows),
                  pl.BlockSpec((c, GLA_VALUE_WIDTH), rows), pl.BlockSpec((c, GLA_KEY_WIDTH), rows),
                  pl.BlockSpec((c, GLA_VALUE_WIDTH), rows),
                  pl.BlockSpec((sb, GLA_HEADS, GLA_DK, GLA_DV), st),
                  pl.BlockSpec(mall.shape, const), pl.BlockSpec(role.shape, const), pl.BlockSpec(mask.shape, const),
                  pl.BlockSpec(msum.shape, const), pl.BlockSpec((1, GLA_DV), const)],
        out_specs=[pl.BlockSpec((c, GLA_VALUE_WIDTH), rows), pl.BlockSpec((sb, GLA_HEADS, GLA_DK, GLA_DV), st)],
        out_shape=[jax.ShapeDtypeStruct(vg.shape, BF16), jax.ShapeDtypeStruct(state.shape, F32)],
        compiler_params=_cparams(("parallel",)),
        name="gla_sample",
    )(qg, kg, vg, la, rg, state, mall, role, mask, msum, gnorm)


def _post_kernel(x_ref, a_ref, gl_ref, ga_ref, gb_ref, wpa, wpg, wo, nf_ref, wr_hi, wr_lo, br, x1t_o, rt_o):
    tm = x_ref.shape[0]
    part = POST_PART_ROWS if tm % POST_PART_ROWS == 0 else tm
    halves = [slice(r0, r0 + part) for r0 in range(0, tm, part)]
    proj =[(_dot(a_ref[r, :], wpa[...]), _dot(gl_ref[r, :], wpg[...])) for r in halves]
    x1s = []
    for r, (pa, pg) in zip(halves, proj):
        merged = ga_ref[r, :].astype(F32) * pa + gb_ref[r, :].astype(F32) * pg
        x1s.append(x_ref[r, :] + _dot(merged.astype(BF16), wo[...]))
    logits = []
    for r, x1 in zip(halves, x1s):
        rows = r.stop - r.start
        for j in range(TOKEN_ROWS):
            x1t_o[pl.ds(r.start * TOKEN_ROWS + j, rows, stride=TOKEN_ROWS), :] = x1[:, j * LANES:(j + 1) * LANES]
        h_hi, h_lo = _split_bf16(_rms(x1, nf_ref[...]))
        logits.append(_dot_nt(wr_hi[...], h_hi) + _dot_nt(wr_hi[...], h_lo) + _dot_nt(wr_lo[...], h_hi))
    nrow = ROUTER_ROWS
    big = jnp.int32(LANES)
    ninf = -jnp.inf
    for r, lt in zip(halves, logits):
        lt = lt[:nrow] + br[:nrow, 0:1]
        row = lax.broadcasted_iota(jnp.int32, lt.shape, 0)

        def first_max(vals):
            mx = jnp.max(vals, axis=0, keepdims=True)
            return mx, jnp.min(jnp.where(vals == mx, row, big), axis=0, keepdims=True)

        gl = jnp.where((row >= N_EXPERTS) & (row < N_EXPERTS + N_GROUPS), lt, ninf)
        gmax, gidx = first_max(gl)
        p_sel = 1.0 / jnp.sum(jnp.exp(gl - gmax), axis=0, keepdims=True)
        lo = (gidx - N_EXPERTS) * EXPERTS_PER_GROUP
        el = jnp.where((row >= lo) & (row < lo + EXPERTS_PER_GROUP), lt, ninf)
        v1, i1 = first_max(el)
        el2 = jnp.where(row == i1, ninf, el)
        v2, i2 = first_max(el2)
        t = jnp.exp(v2 - v1)
        w1 = p_sel / (1.0 + t)
        w2 = p_sel * t / (1.0 + t)
        row8 = lax.broadcasted_iota(jnp.int32, (rt_o.shape[0], r.stop - r.start), 0)
        pick = lambda k, val, rest: jnp.where(row8 == k, val, rest)
        rt_o[:, r] = pick(0, i1.astype(F32), pick(1, i2.astype(F32), pick(2, w1, pick(3, w2, 0.0))))


def _post(x2d, a_out, g_out, ga, gb, w, tm):
    n = x2d.shape[0]
    row = lambda i: (i, 0)
    const = lambda i: (0, 0)
    wnames = ("wpa", "wpg", "wo", "nf", "wr_hi", "wr_lo", "br")
    rt_rows = 8
    return pl.pallas_call(
        _post_kernel,
        grid=(n // tm,),
        in_specs=[pl.BlockSpec((tm, D_MODEL), row), pl.BlockSpec((tm, ATTN_WIDTH), row),
                  pl.BlockSpec((tm, GLA_VALUE_WIDTH), row), pl.BlockSpec((tm, D_MODEL), row),
                  pl.BlockSpec((tm, D_MODEL), row)] + [pl.BlockSpec(w[k].shape, const) for k in wnames],
        out_specs=[pl.BlockSpec((tm * TOKEN_ROWS, LANES), row),
                   pl.BlockSpec((rt_rows, tm), lambda i: (0, i))],
        out_shape=[jax.ShapeDtypeStruct((n * TOKEN_ROWS, LANES), F32), jax.ShapeDtypeStruct((rt_rows, n), F32)],
        compiler_params=_cparams(("parallel",)),
        name="post_mixer",
    )(x2d, a_out, g_out, ga, gb, *[w[k] for k in wnames])


def _moe_plan(rt, tme):
    n = rt.shape[1]
    ntiles = n // tme
    max_items = ntiles + N_GROUPS - 1
    i1, i2 = rt[0].astype(jnp.int32), rt[1].astype(jnp.int32)
    grp = i1 // EXPERTS_PER_GROUP
    lo = jnp.minimum(i1, i2) % EXPERTS_PER_GROUP
    hi = jnp.maximum(i1, i2) % EXPERTS_PER_GROUP
    snake = lambda l, h: jnp.where(l % 2 == 0, h, EXPERTS_PER_GROUP + l - h)
    key = (grp * EXPERTS_PER_GROUP + lo) * EXPERTS_PER_GROUP + snake(lo, hi)
    order = jnp.argsort(key, stable=True).astype(jnp.int32)
    skey = key[order].reshape(ntiles, tme)
    sg = skey // (EXPERTS_PER_GROUP * EXPERTS_PER_GROUP)
    slo = (skey // EXPERTS_PER_GROUP) % EXPERTS_PER_GROUP
    shi = snake(slo, skey % EXPERTS_PER_GROUP)
    ev = jnp.arange(EXPERTS_PER_GROUP)
    in_g = sg[:, :, None] == jnp.arange(N_GROUPS)
    uses_e = (slo[:, :, None] == ev) | (shi[:, :, None] == ev)
    flags_tge = jnp.any(in_g[:, :, :, None] & uses_e[:, :, None, :], axis=1)
    present = jnp.any(in_g, axis=1).reshape(-1)
    pos = jnp.cumsum(present) - 1
    n_items = pos[-1] + 1
    src = jnp.zeros((max_items,), jnp.int32).at[jnp.where(present, pos, max_items)].set(
        jnp.arange(ntiles * N_GROUPS, dtype=jnp.int32), mode="drop")
    it = jnp.arange(max_items)
    valid = it < n_items
    last_src = src[n_items - 1]
    src = jnp.where(valid, src, last_src)
    item_tile = src // N_GROUPS
    item_group = src % N_GROUPS
    prev_tile = jnp.concatenate([jnp.full((1,), -1, jnp.int32), item_tile[:-1]])
    next_tile = jnp.concatenate([item_tile[1:], jnp.full((1,), -1, jnp.int32)])
    first = valid & (item_tile != prev_tile)
    last = valid & ((item_tile != next_tile) | (it == n_items - 1))
    flags = flags_tge[item_tile, item_group] & valid[:, None]
    e0 = jnp.argmax(flags, axis=1)
    rest = flags & (ev[None, :] != e0[:, None])
    e1 = jnp.argmax(rest, axis=1)
    rest = rest & (ev[None, :] != e1[:, None])
    has2 = jnp.any(rest, axis=1)
    e2 = jnp.argmax(rest, axis=1)
    rest = rest & (ev[None, :] != e2[:, None])
    rt_sorted = rt[:, order].reshape(rt.shape[0], ntiles, tme).transpose(1, 0, 2)
    i32 = lambda z: z.astype(jnp.int32)
    n_rest = jnp.sum(rest, axis=1)
    rest_list = jnp.argsort(jnp.logical_not(rest), axis=1, stable=True)
    plan = (order, i32(item_tile), i32(item_group), i32(first), i32(last), i32(valid), i32(e0), i32(e1), i32(e2),
            i32(has2), i32(n_rest), i32(rest_list.reshape(-1)))
    return plan, rt_sorted


def _moe_kernel(order, itile, igroup, ifirst, ilast, ivalid, ie0, ie1, ie2, ihas2, inrest, irest, x_hbm, rt_ref, wg, wu, wd,
                nffn, nfin,
                y_hbm, xbuf, ybuf, acc, hbuf, rcol, gsem, ssem, *, tme, ntiles):
    i = pl.program_id(0)
    t = itile[i]
    slot = t % 2
    g = igroup[i]
    is_first = ifirst[i] == 1

    def gather_row(tile, sl, r):
        tok = order[tile * tme + r]
        src = x_hbm.at[pl.ds(pl.multiple_of(tok * TOKEN_ROWS, TOKEN_ROWS), TOKEN_ROWS)]
        dst = xbuf.at[pl.ds(pl.multiple_of((sl * tme + r) * TOKEN_ROWS, TOKEN_ROWS), TOKEN_ROWS)]
        return pltpu.make_async_copy(src, dst, gsem.at[sl])

    def scatter_row(tile, sl, r):
        tok = order[tile * tme + r]
        return pltpu.make_async_copy(ybuf.at[sl, pl.ds(r, 1)], y_hbm.at[pl.ds(tok, 1)], ssem.at[sl])

    def start_rows(make, tile, sl):
        def body(r, c):
            make(tile, sl, r).start()
            return c
        lax.fori_loop(0, tme, body, 0, unroll=8)

    def wait_gather(sl):
        rows = tme * TOKEN_ROWS
        pltpu.make_async_copy(x_hbm.at[pl.ds(0, rows)], xbuf.at[pl.ds(pl.multiple_of(sl * rows, rows), rows)],
                              gsem.at[sl]).wait()

    def wait_scatter(sl):
        pltpu.make_async_copy(ybuf.at[sl], y_hbm.at[pl.ds(0, tme)], ssem.at[sl]).wait()

    def start_rows_inline(make, tile, sl, r0=0, r1=tme):
        for r in range(r0, r1):
            make(tile, sl, r).start()

    def expert(e):
        eid = (g * EXPERTS_PER_GROUP + e).astype(F32)
        ce = (jnp.where(rcol[:, 0:1] == eid, rcol[:, 2:3], 0.0)
              + jnp.where(rcol[:, 1:2] == eid, rcol[:, 3:4], 0.0))
        h = hbuf[...]
        a = _dot(h, wg[e])
        u = _dot(h, wu[e])
        act = (a * _sigmoid(a)) * u * ce
        acc[...] += _dot(act.astype(BF16), wd[e])

    def expert_with(e, run, alternatives):
        plain = run
        for cond, side_work in alternatives:
            @pl.when(cond)
            def _():
                side_work()
                expert(e)

            plain = jnp.logical_and(plain, jnp.logical_not(cond))

        @pl.when(plain)
        def _():
            expert(e)

    @pl.when(i == 0)
    def _():
        start_rows(gather_row, 0, 0)

    @pl.when(is_first)
    def _():
        wait_gather(slot)
        base = pl.multiple_of(slot * (tme * TOKEN_ROWS), tme * TOKEN_ROWS)
        x1 = jnp.concatenate([xbuf[pl.ds(base + j, tme, stride=TOKEN_ROWS), :] for j in range(TOKEN_ROWS)], axis=1)
        acc[...] = x1
        hbuf[...] = _rms(x1, nffn[...]).astype(BF16)
        rt = rt_ref[0]
        rcol[...] = jnp.transpose(jnp.concatenate([rt, jnp.zeros((LANES - rt.shape[0], tme), F32)], axis=0))

    valid = ivalid[i] == 1
    has2 = ihas2[i] == 1
    do_gather = jnp.logical_and(is_first, t + 1 < ntiles)
    do_scatter = jnp.logical_and(is_first, t >= 1)
    half = tme // 2
    scatter_rows = lambda r0, r1: (lambda: start_rows_inline(scatter_row, t - 1, 1 - slot, r0, r1))
    expert_with(ie0[i], valid, [(do_gather, lambda: start_rows_inline(gather_row, t + 1, 1 - slot))])
    expert_with(ie1[i], valid, [(jnp.logical_and(do_scatter, has2), scatter_rows(0, half)),
                                (jnp.logical_and(do_scatter, jnp.logical_not(has2)), scatter_rows(0, tme))])
    expert_with(ie2[i], has2, [(jnp.logical_and(do_scatter, has2), scatter_rows(half, tme))])

    def rest_expert(k, carry):
        expert(irest[i * EXPERTS_PER_GROUP + k])
        return carry

    lax.fori_loop(0, inrest[i], rest_expert, 0)

    @pl.when(ilast[i] == 1)
    def _():
        @pl.when(t >= 2)
        def _():
            wait_scatter(slot)

        ybuf[slot] = _rms(acc[...], nfin[...])

        @pl.when(t == ntiles - 1)
        def _():
            start_rows(scatter_row, t, slot)

    @pl.when(i == pl.num_programs(0) - 1)
    def _():
        for sl in range(min(2, ntiles)):
            wait_scatter(sl)


def _moe(x1t, rt, weg, weu, wed, nffn, nfin):
    n = x1t.shape[0] // TOKEN_ROWS
    tme = min(MOE_TILE, n)
    assert n % tme == 0
    ntiles = n // tme
    plan, rt_sorted = _moe_plan(rt, tme)
    max_items = ntiles + N_GROUPS - 1
    grp = lambda i, order, itile, igroup, *_: (igroup[i], 0, 0)
    til = lambda i, order, itile, *_: (itile[i], 0, 0)
    const = lambda i, *_: (0, 0)
    grid_spec = pltpu.PrefetchScalarGridSpec(
        num_scalar_prefetch=len(plan),
        grid=(max_items,),
        in_specs=[pl.BlockSpec(memory_space=pl.ANY),
                  pl.BlockSpec((1,) + rt_sorted.shape[1:], til),
                  pl.BlockSpec((EXPERTS_PER_GROUP, D_MODEL, EXPERT_FF), grp),
                  pl.BlockSpec((EXPERTS_PER_GROUP, D_MODEL, EXPERT_FF), grp),
                  pl.BlockSpec((EXPERTS_PER_GROUP, EXPERT_FF, D_MODEL), grp),
                  pl.BlockSpec((1, D_MODEL), const), pl.BlockSpec((1, D_MODEL), const)],
        out_specs=pl.BlockSpec(memory_space=pl.ANY),
        scratch_shapes=[pltpu.VMEM((2 * tme * TOKEN_ROWS, LANES), F32), pltpu.VMEM((2, tme, D_MODEL), F32),
                        pltpu.VMEM((tme, D_MODEL), F32), pltpu.VMEM((tme, D_MODEL), BF16),
                        pltpu.VMEM((tme, LANES), F32),
                        pltpu.SemaphoreType.DMA((2,)), pltpu.SemaphoreType.DMA((2,))],
    )
    return pl.pallas_call(
        functools.partial(_moe_kernel, tme=tme, ntiles=ntiles),
        grid_spec=grid_spec,
        out_shape=jax.ShapeDtypeStruct((n, D_MODEL), F32),
        compiler_params=_cparams(("arbitrary",)),
        name="moe",
    )(*plan, x1t, rt_sorted, weg, weu, wed, nffn, nfin)


def _rope_tables(positions):
    half = HEAD_DIM // 2
    inv_freq = ROPE_THETA ** (-jnp.arange(half, dtype=F32) / half)
    ang = positions.astype(F32)[:, None] * inv_freq[None, :]
    cos, sin = jnp.cos(ang), jnp.sin(ang)
    reps = LANES // HEAD_DIM
    return (jnp.tile(jnp.concatenate([cos, cos], -1), (1, reps)),
            jnp.tile(jnp.concatenate([-sin, sin], -1), (1, reps)))


def _prep_weights(norm_mix, w_in, w_gla_f2, b_gla_f, gla_norm, w_proj_attn, w_proj_gla, w_out, norm_ffn,
                  w_router_group, b_router_group, w_router_expert, b_router_expert):
    w = {"win": w_in.astype(BF16)}
    w["wf2"] = jnp.pad(w_gla_f2.astype(BF16), ((0, LANES - GLA_GATE_RANK), (0, 0)))
    w["bf"] = b_gla_f.reshape(1, -1)
    w["norm_mix"] = norm_mix.reshape(1, -1)
    w["gla_norm"] = gla_norm.reshape(1, -1)
    w["wpa"] = w_proj_attn.astype(BF16)
    w["wpg"] = w_proj_gla.astype(BF16)
    w["wo"] = w_out.astype(BF16)
    w["nf"] = norm_ffn.reshape(1, -1)
    pad = LANES - N_EXPERTS - N_GROUPS
    wr_t = jnp.pad(jnp.concatenate([w_router_expert, w_router_group], axis=1), ((0, 0), (0, pad))).T
    w["wr_hi"], w["wr_lo"] = _split_bf16(wr_t)
    w["br"] = jnp.broadcast_to(jnp.pad(jnp.concatenate([b_router_expert, b_router_group]), (0, pad))[:, None],
                               (LANES, LANES))
    return w


def _layer(x, positions_tab, cache, w, sinks, weg, weu, wed, nfin, tm):
    batch, seq, _ = x.shape
    n = batch * seq
    x2d = x.reshape(n, D_MODEL)
    cos_t, sin_t = positions_tab
    qa, ka, va, qg, kg, vg, rg, la, ga, gb = _inproj(x2d, w["norm_mix"], cos_t, sin_t, w, tm)
    if cache is None:
        a_out = _swa_prompt(sinks, qa, ka, va, batch, seq)
        last = lambda z: z.reshape(batch, seq, KV_WIDTH)[:, seq - WINDOW:].reshape(batch, WINDOW, ATTN_KV_HEADS, HEAD_DIM)
        new_k, new_v = last(ka), last(va)
        g_out, new_s = _gla_prompt(qg, kg, vg, la, rg, w["gla_norm"], batch, seq)
    else:
        cache_k, cache_v, state = cache
        a_out, new_k, new_v = _swa_sample(sinks, qa, ka, va, cache_k.reshape(batch, WINDOW, KV_WIDTH),
                                          cache_v.reshape(batch, WINDOW, KV_WIDTH), batch, seq)
        new_k = new_k.reshape(batch, WINDOW, ATTN_KV_HEADS, HEAD_DIM)
        new_v = new_v.reshape(batch, WINDOW, ATTN_KV_HEADS, HEAD_DIM)
        g_out, new_s = _gla_sample(qg, kg, vg, la, rg, state, w["gla_norm"], batch, seq)
    x1t, rt = _post(x2d, a_out, g_out, ga, gb, w, tm * 2 if x2d.shape[0] % (tm * 2) == 0 else tm)
    y = _moe(x1t, rt, weg, weu, wed, w["nf"], nfin)
    return y.reshape(batch, seq, D_MODEL), new_k, new_v, new_s


def kernel(x_prompt, x_sample, cache_win_k, cache_win_v, state_gla, norm_mix, w_in, w_gla_f2, b_gla_f, gla_norm,
           attn_sinks, w_proj_attn, w_proj_gla, w_out, norm_ffn, w_router_group, b_router_group, w_router_expert,
           b_router_expert, w_exp_gate, w_exp_up, w_exp_down, norm_final):
    assert norm_mix.shape[0] == 1, "single-layer step"
    seq_p = x_prompt.shape[1]
    dec_b, dec_t = x_sample.shape[0], x_sample.shape[1]
    w = _prep_weights(norm_mix[0], w_in[0], w_gla_f2[0], b_gla_f[0], gla_norm[0], w_proj_attn[0], w_proj_gla[0],
                      w_out[0], norm_ffn[0], w_router_group[0], b_router_group[0], w_router_expert[0],
                      b_router_expert[0])
    weg = w_exp_gate[0].astype(BF16)
    weu = w_exp_up[0].astype(BF16)
    wed = w_exp_down[0].astype(BF16)
    nfin = norm_final.reshape(1, -1)
    sinks = attn_sinks[0]
    tab_p = _rope_tables(jnp.arange(seq_p, dtype=jnp.int32))
    pos_s = PAST_LEN + jnp.arange(dec_t, dtype=jnp.int32)
    tab_s = tuple(jnp.tile(t, (dec_b, 1)) for t in _rope_tables(pos_s))
    tm_p = min(512, seq_p)
    tm_s = dec_b * dec_t
    yp, pk, pv, ps = _layer(x_prompt, tab_p, None, w, sinks, weg, weu, wed, nfin, tm_p)
    ys, sk, sv, ss = _layer(x_sample, tab_s, (cache_win_k[0], cache_win_v[0], state_gla[0]), w, sinks, weg, weu, wed,
                            nfin, tm_s)
    return (yp, ys, pk[None], pv[None], ps[None], sk[None], sv[None], ss[None])
```

```python
import functools
import math

import numpy as np
import jax
import jax.numpy as jnp
from jax import lax
from jax.experimental import pallas as pl
from jax.experimental.pallas import tpu as pltpu

F32 = jnp.float32
BF16 = jnp.bfloat16

D_MODEL = 1024
ATTN_HEADS = 8
ATTN_KV_HEADS = 2
GROUP = ATTN_HEADS // ATTN_KV_HEADS
HEAD_DIM = 64
ATTN_WIDTH = ATTN_HEADS * HEAD_DIM
KV_WIDTH = ATTN_KV_HEADS * HEAD_DIM
WINDOW = 128
ROPE_THETA = 10000.0
PAST_LEN = 8192
GLA_HEADS = 4
GLA_KEY_WIDTH = D_MODEL // 2
GLA_VALUE_WIDTH = D_MODEL
GLA_DK = GLA_KEY_WIDTH // GLA_HEADS
GLA_DV = GLA_VALUE_WIDTH // GLA_HEADS
GLA_GATE_RANK = 16
GLA_GATE_NORMALIZER = 16.0
N_GROUPS = 4
EXPERTS_PER_GROUP = 8
N_EXPERTS = N_GROUPS * EXPERTS_PER_GROUP
EXPERT_FF = 256
EPS = 1e-6
LOG2_E = 1.4426950408889634

LANES = 128
GLA_CHUNK = 128
GLA_HEADS_PER_STEP = 4
SWA_BLOCKS_PER_STEP = 4
SAMPLE_SEQ_BLOCK = 16
POST_PART_ROWS = 512
MOE_TILE = 256
TOKEN_ROWS = D_MODEL // LANES
INPROJ_WIDTHS = (ATTN_WIDTH, 2 * KV_WIDTH, GLA_KEY_WIDTH, GLA_KEY_WIDTH, GLA_VALUE_WIDTH, GLA_VALUE_WIDTH)
ROUTER_ROWS = 40
VMEM_LIMIT = 56 * 1024 * 1024


def _cparams(sem):
    return pltpu.CompilerParams(dimension_semantics=sem, vmem_limit_bytes=VMEM_LIMIT)


def _rms(x, g):
    return x * lax.rsqrt(jnp.mean(x * x, axis=-1, keepdims=True) + EPS) * g


def _sigmoid(x):
    return 1.0 / (1.0 + jnp.exp(-x))


def _dot(a, b):
    return jnp.dot(a, b, preferred_element_type=F32)


def _dot_nt(a, b):
    return lax.dot_general(a, b, (((1,), (1,)), ((), ())), preferred_element_type=F32)


def _dot_tn(a, b):
    return lax.dot_general(a, b, (((0,), (0,)), ((), ())), preferred_element_type=F32)


def _split_bf16(x):
    hi = x.astype(BF16)
    lo = (x - hi.astype(F32)).astype(BF16)
    return hi, lo


def _inproj_kernel(x_ref, g_ref, cos_ref, sin_ref, win, wf2, bf,
                   qa_o, ka_o, va_o, qg_o, kg_o, vg_o, rg_o, la_o, ga_o, gb_o, wgate):
    f0 = sum(INPROJ_WIDTHS)

    @pl.when(pl.program_id(0) == 0)
    def _():
        tail = win[:, f0:]
        wgate[...] = tail[:, GLA_GATE_RANK:GLA_GATE_RANK + 2 * D_MODEL]

    tm = x_ref.shape[0]
    halves = [slice(0, tm // 2), slice(tm // 2, tm)] if tm % 16 == 0 else [slice(0, tm)]
    hbs = [_rms(x_ref[r, :], g_ref[...]).astype(BF16) for r in halves]
    cols = np.cumsum((0,) + INPROJ_WIDTHS[:-1])
    wqa, wkva, wqg, wkg, wvg, wrg = (win.at[:, int(c):int(c) + wd] for c, wd in zip(cols, INPROJ_WIDTHS))
    wga = wgate.at[:, :D_MODEL]
    wgb = wgate.at[:, D_MODEL:]
    wf = win.at[:, f0:f0 + LANES]
    lane = lax.broadcasted_iota(jnp.int32, (halves[0].stop - halves[0].start, LANES), 1)
    first_half = (lane % HEAD_DIM) < (HEAD_DIM // 2)

    def rope(t, r):
        swapped = jnp.where(first_half, pltpu.roll(t, LANES - HEAD_DIM // 2, 1), pltpu.roll(t, HEAD_DIM // 2, 1))
        return t * cos_ref[r, :] + swapped * sin_ref[r, :]

    for r, hb in zip(halves, hbs):
        ga_o[r, :] = _sigmoid(_dot(hb, wga[...])).astype(BF16)
    for r, hb in zip(halves, hbs):
        gb_o[r, :] = _sigmoid(_dot(hb, wgb[...])).astype(BF16)
    for r, hb in zip(halves, hbs):
        rz = _dot(hb, wrg[...])
        rg_o[r, :] = (rz * _sigmoid(rz)).astype(BF16)
    for r, hb in zip(halves, hbs):
        z = _dot(_dot(hb, wf[...]).astype(BF16), wf2[...]) + bf[...]
        la_o[r, :] = (jnp.minimum(z, 0.0) - jnp.log1p(jnp.exp(-jnp.abs(z)))) * (1.0 / GLA_GATE_NORMALIZER)
    for r, hb in zip(halves, hbs):
        qa = _dot(hb, wqa[...])
        for c in range(ATTN_WIDTH // LANES):
            sl = slice(c * LANES, (c + 1) * LANES)
            qa_o[r, sl] = (rope(qa[:, sl], r) * (HEAD_DIM ** -0.5)).astype(BF16)
    for r, hb in zip(halves, hbs):
        kva = _dot(hb, wkva[...])
        ka_o[r, :] = rope(kva[:, :KV_WIDTH], r)
        va_o[r, :] = kva[:, KV_WIDTH:]
    for r, hb in zip(halves, hbs):
        qg_o[r, :] = (_dot(hb, wqg[...]) * (GLA_DK ** -0.5)).astype(BF16)
    for r, hb in zip(halves, hbs):
        kg_o[r, :] = _dot(hb, wkg[...]).astype(BF16)
    for r, hb in zip(halves, hbs):
        vg_o[r, :] = _dot(hb, wvg[...]).astype(BF16)


def _inproj(x2d, norm_g, cos_t, sin_t, w, tm):
    n = x2d.shape[0]
    ntab = cos_t.shape[0] // tm
    row = lambda i: (i, 0)
    const = lambda i: (0, 0)
    tab = lambda i: (i % ntab, 0)
    wnames = ("win", "wf2", "bf")
    wspecs = [pl.BlockSpec(w[k].shape, const) for k in wnames]
    widths = (ATTN_WIDTH, KV_WIDTH, KV_WIDTH, GLA_KEY_WIDTH, GLA_KEY_WIDTH, GLA_VALUE_WIDTH, GLA_VALUE_WIDTH,
              GLA_KEY_WIDTH, D_MODEL, D_MODEL)
    dtypes = (BF16, F32, F32, BF16, BF16, BF16, BF16, F32, BF16, BF16)
    return pl.pallas_call(
        _inproj_kernel,
        grid=(n // tm,),
        in_specs=[pl.BlockSpec((tm, D_MODEL), row), pl.BlockSpec((1, D_MODEL), const),
                  pl.BlockSpec((tm, LANES), tab), pl.BlockSpec((tm, LANES), tab)] + wspecs,
        out_specs=[pl.BlockSpec((tm, wd), row) for wd in widths],
        out_shape=[jax.ShapeDtypeStruct((n, wd), dt) for wd, dt in zip(widths, dtypes)],
        scratch_shapes=[pltpu.VMEM((D_MODEL, 2 * D_MODEL), BF16)],
        compiler_params=_cparams(("arbitrary",)),
        name="inproj",
    )(x2d, norm_g, cos_t, sin_t, *[w[k] for k in wnames])


def _swa_prompt_kernel(sink_ref, q_ref, kc_ref, kp_ref, vc_ref, vp_ref, o_ref, *, qb):
    n = pl.program_id(1)
    k3 = jnp.concatenate([kp_ref[...], kc_ref[...]], axis=0).astype(BF16)
    v3 = jnp.concatenate([vp_ref[...], vc_ref[...]], axis=0).astype(BF16)
    t = lax.broadcasted_iota(jnp.int32, (WINDOW, 2 * WINDOW), 0)
    j = lax.broadcasted_iota(jnp.int32, (WINDOW, 2 * WINDOW), 1)
    band = (j >= t) & (j <= t + WINDOW)
    zeros = jnp.zeros((2 * WINDOW, HEAD_DIM), BF16)
    ones = jnp.ones((2 * WINDOW, HEAD_DIM), BF16)
    lane = lax.broadcasted_iota(jnp.int32, (WINDOW, LANES), 1)
    chains = [(blk, h) for blk in range(qb) for h in range(ATTN_HEADS)]
    scores = []
    for blk, h in chains:
        kv = h // GROUP
        rows = slice(blk * WINDOW, (blk + 1) * WINDOW)
        keys = slice(blk * WINDOW, (blk + 2) * WINDOW)
        s = _dot_nt(q_ref[rows, h * HEAD_DIM:(h + 1) * HEAD_DIM], k3[keys, kv * HEAD_DIM:(kv + 1) * HEAD_DIM])
        valid = band & ((j >= WINDOW) | (n > 0)) if blk == 0 else band
        scores.append(jnp.where(valid, s, -jnp.inf))
    probs, sink_terms = [], []
    for (blk, h), s in zip(chains, scores):
        m = jnp.maximum(jnp.max(s, axis=-1, keepdims=True), sink_ref[h])
        probs.append(jnp.exp(s - m).astype(BF16))
        sink_terms.append(jnp.exp(sink_ref[h] - m))
    for blk in range(qb):
        rows = slice(blk * WINDOW, (blk + 1) * WINDOW)
        keys = slice(blk * WINDOW, (blk + 2) * WINDOW)
        for kv in range(ATTN_KV_HEADS):
            vv = v3[keys, kv * HEAD_DIM:(kv + 1) * HEAD_DIM]
            vext = (jnp.concatenate([vv, zeros, ones, zeros], axis=1), jnp.concatenate([zeros, vv, zeros, ones], axis=1))
            for pr in range(GROUP // 2):
                h0 = kv * GROUP + pr * 2
                c0 = blk * ATTN_HEADS + h0
                acc = _dot(probs[c0], vext[0]) + _dot(probs[c0 + 1], vext[1])
                l = acc[:, LANES:] + jnp.where(lane < HEAD_DIM, sink_terms[c0], sink_terms[c0 + 1])
                o_ref[rows, h0 * HEAD_DIM:h0 * HEAD_DIM + LANES] = (acc[:, :LANES] / l).astype(BF16)


def _swa_prompt(sinks, qa, ka, va, batch, seq):
    nb = seq // WINDOW
    qb = math.gcd(SWA_BLOCKS_PER_STEP, nb)
    steps = nb // qb
    cur = lambda b, n: (b * steps + n, 0)
    prev = lambda b, n: (b * nb + jnp.maximum(n * qb - 1, 0), 0)
    return pl.pallas_call(
        functools.partial(_swa_prompt_kernel, qb=qb),
        grid=(batch, steps),
        in_specs=[pl.BlockSpec(memory_space=pltpu.SMEM),
                  pl.BlockSpec((qb * WINDOW, ATTN_WIDTH), cur),
                  pl.BlockSpec((qb * WINDOW, KV_WIDTH), cur), pl.BlockSpec((WINDOW, KV_WIDTH), prev),
                  pl.BlockSpec((qb * WINDOW, KV_WIDTH), cur), pl.BlockSpec((WINDOW, KV_WIDTH), prev)],
        out_specs=pl.BlockSpec((qb * WINDOW, ATTN_WIDTH), cur),
        out_shape=jax.ShapeDtypeStruct(qa.shape, BF16),
        compiler_params=_cparams(("parallel", "parallel")),
        name="swa_prompt",
    )(sinks, qa, ka, ka, va, va)


def _swa_sample_kernel(sink_ref, q_ref, kn_ref, vn_ref, kn3_ref, vn3_ref, ck_ref, cv_ref, o_ref, nk_ref, nv_ref,
                       *, t_new):
    sb = ck_ref.shape[0]
    spv = 8 // t_new
    nq = GROUP * 8
    nc = spv * WINDOW
    qi = lax.broadcasted_iota(jnp.int32, (nq, nc), 0) % 8
    ci = lax.broadcasted_iota(jnp.int32, (nq, nc), 1)
    valid_c = (qi // t_new == ci // WINDOW) & (ci % WINDOW >= qi % t_new)
    qn = lax.broadcasted_iota(jnp.int32, (nq, 8), 0) % 8
    cn = lax.broadcasted_iota(jnp.int32, (nq, 8), 1)
    valid_n = (qn // t_new == cn // t_new) & (cn <= qn)
    grow = lax.broadcasted_iota(jnp.int32, (nq, 1), 0) // 8
    chains = [(vr, kv) for vr in range(sb // spv) for kv in range(ATTN_KV_HEADS)]
    scored = []
    for vr, kv in chains:
        r8 = slice(8 * vr, 8 * vr + 8)
        cs = slice(kv * HEAD_DIM, (kv + 1) * HEAD_DIM)
        heads = [kv * GROUP + g for g in range(GROUP)]
        qs = jnp.concatenate([q_ref[r8, h * HEAD_DIM:(h + 1) * HEAD_DIM] for h in heads], axis=0)
        ck = ck_ref[vr * spv:(vr + 1) * spv, :, cs].reshape(nc, HEAD_DIM).astype(BF16)
        kn = kn_ref[r8, cs].astype(BF16)
        scored.append((jnp.where(valid_c, _dot_nt(qs, ck), -jnp.inf), jnp.where(valid_n, _dot_nt(qs, kn), -jnp.inf)))
    soft = []
    for (vr, kv), (s_c, s_n) in zip(chains, scored):
        sink = sink_ref[kv * GROUP]
        for g in range(1, GROUP):
            sink = jnp.where(grow == g, sink_ref[kv * GROUP + g], sink)
        m = jnp.maximum(jnp.maximum(jnp.max(s_c, axis=-1, keepdims=True), jnp.max(s_n, axis=-1, keepdims=True)), sink)
        p_c = jnp.exp(s_c - m)
        p_n = jnp.exp(s_n - m)
        l = jnp.sum(p_c, axis=-1, keepdims=True) + jnp.sum(p_n, axis=-1, keepdims=True) + jnp.exp(sink - m)
        soft.append((p_c.astype(BF16), p_n.astype(BF16), l))
    for (vr, kv), (p_c, p_n, l) in zip(chains, soft):
        r8 = slice(8 * vr, 8 * vr + 8)
        cs = slice(kv * HEAD_DIM, (kv + 1) * HEAD_DIM)
        cv = cv_ref[vr * spv:(vr + 1) * spv, :, cs].reshape(nc, HEAD_DIM).astype(BF16)
        o = (_dot(p_c, cv) + _dot(p_n, vn_ref[r8, cs].astype(BF16))) / l
        for a in range(GROUP // 2):
            pair = jnp.concatenate([o[16 * a:16 * a + 8], o[16 * a + 8:16 * a + 16]], axis=1)
            c0 = (kv * GROUP + 2 * a) * HEAD_DIM
            o_ref[r8, c0:c0 + LANES] = pair.astype(BF16)
    nk_ref[:, 0:WINDOW - t_new, :] = ck_ref[:, t_new:WINDOW, :]
    nk_ref[:, WINDOW - t_new:WINDOW, :] = kn3_ref[...]
    nv_ref[:, 0:WINDOW - t_new, :] = cv_ref[:, t_new:WINDOW, :]
    nv_ref[:, WINDOW - t_new:WINDOW, :] = vn3_ref[...]


def _swa_sample(sinks, qa, ka, va, cache_k, cache_v, batch, t_new):
    sb = SAMPLE_SEQ_BLOCK
    rows = sb * t_new
    r2 = lambda i: (i, 0)
    r3 = lambda i: (i, 0, 0)
    ka3 = ka.reshape(batch, t_new, KV_WIDTH)
    va3 = va.reshape(batch, t_new, KV_WIDTH)
    return pl.pallas_call(
        functools.partial(_swa_sample_kernel, t_new=t_new),
        grid=(batch // sb,),
        in_specs=[pl.BlockSpec(memory_space=pltpu.SMEM),
                  pl.BlockSpec((rows, ATTN_WIDTH), r2),
                  pl.BlockSpec((rows, KV_WIDTH), r2), pl.BlockSpec((rows, KV_WIDTH), r2),
                  pl.BlockSpec((sb, t_new, KV_WIDTH), r3), pl.BlockSpec((sb, t_new, KV_WIDTH), r3),
                  pl.BlockSpec((sb, WINDOW, KV_WIDTH), r3), pl.BlockSpec((sb, WINDOW, KV_WIDTH), r3)],
        out_specs=[pl.BlockSpec((rows, ATTN_WIDTH), r2),
                   pl.BlockSpec((sb, WINDOW, KV_WIDTH), r3), pl.BlockSpec((sb, WINDOW, KV_WIDTH), r3)],
        out_shape=[jax.ShapeDtypeStruct(qa.shape, BF16),
                   jax.ShapeDtypeStruct(cache_k.shape, F32), jax.ShapeDtypeStruct(cache_v.shape, F32)],
        compiler_params=_cparams(("parallel",)),
        name="swa_sample",
    )(sinks, qa, ka, va, ka3, va3, cache_k, cache_v)


def _gla_constants(c, seg, with_rem):
    t = np.arange(c)
    sid = t // seg
    same = sid[:, None] == sid[None, :]
    levels = []
    m = seg // 2
    while m >= 1:
        levels.append(m)
        m //= 2
    mats, roles, masks = [], [], []
    for m in levels:
        blk = t // (2 * m)
        second = (t // m) % 2 == 1
        p = blk * 2 * m + m - 1
        u = t[None, :]
        mq = (u > p[:, None]) & (u <= t[:, None])
        mk = (u > t[:, None]) & (u <= p[:, None])
        if m > 1:
            mats.append(np.where(second[:, None], mq, mk))
        roles.append(np.broadcast_to(second[:, None], (c, LANES)))
        masks.append((blk[:, None] == blk[None, :]) & second[:, None] & ~second[None, :])
    masks.append(np.eye(c, dtype=bool))
    mats.append(same & (t[None, :] <= t[:, None]))
    if with_rem:
        mats.append(same & (t[None, :] > t[:, None]))
    mall = np.concatenate(mats, 0).astype(np.float32)
    mall = jnp.asarray(np.concatenate([mall, mall], 1), BF16)
    role = jnp.asarray(np.concatenate(roles, 0).astype(np.float32))
    mask = jnp.asarray(np.concatenate(masks, 0).astype(np.float32))
    return len(levels), mall, role, mask


def _gla_exponents(la, mall):
    la2 = la * LOG2_E
    hl = jnp.concatenate(_split_bf16(la2), axis=0)
    return _dot(mall, hl), hl, la2


def _gla_scores(qb, kb, e2, la2, role_ref, mask_ref, nlev, c):
    qf = qb.astype(F32)
    kf = kb.astype(F32)
    terms = []
    for lv in range(nlev):
        sl = slice(lv * c, (lv + 1) * c)
        m = 1 << (nlev - 1 - lv)
        if m % 8 == 0:
            pe = jnp.exp2(e2[sl])
            x = jnp.concatenate([(qf if blk % 2 else kf)[blk * m:(blk + 1) * m] * pe[blk * m:(blk + 1) * m]
                                 for blk in range(c // m)], axis=0).astype(BF16)
        else:
            second = role_ref[sl, :] > 0.5
            e = e2[sl] if lv < nlev - 1 else jnp.where(second, la2, 0.0)
            x = (jnp.where(second, qf, kf) * jnp.exp2(e)).astype(BF16)
        terms.append((x, x, lv))
    terms.append((qb, kb, nlev))
    mask = lambda i: mask_ref[i * c:(i + 1) * c, :]
    a = None
    pair = c % LANES == 0
    while terms:
        if pair and len(terms) >= 2:
            (l0, r0, i0), (l1, r1, i1) = terms.pop(), terms.pop()
            z = jnp.zeros_like(r0)
            rhs = jnp.concatenate([jnp.concatenate([r0, z], axis=1), jnp.concatenate([z, r1], axis=1)], axis=0)
            g = _dot_nt(jnp.concatenate([l0, l1], axis=1), rhs)
            t = mask(i0) * g[:, :c] + mask(i1) * g[:, c:]
        else:
            l0, r0, i0 = terms.pop()
            t = mask(i0) * _dot_nt(l0, r0)
        a = t if a is None else a + t
    return a, qf, kf


def _gla_out(o, g, gate):
    return (_rms(o, g) * gate.astype(F32)).astype(BF16)


def _gla_prompt_kernel(q_ref, k_ref, v_ref, la_ref, r_ref, mall_ref, role_ref, mask_ref, g_ref, o_ref, s_ref,
                       s_scr, *, nlev, nchunks):
    c = GLA_CHUNK
    hp = GLA_HEADS_PER_STEP
    s_scr[...] = jnp.zeros_like(s_scr)

    def chunk(i, carry):
        rows = pl.ds(pl.multiple_of(i * c, c), c)
        e2_all, _, la2_all = _gla_exponents(la_ref[rows, :], mall_ref[...])
        ksl = [slice(h * GLA_DK, (h + 1) * GLA_DK) for h in range(hp)]
        vsl = [slice(h * GLA_DV, (h + 1) * GLA_DV) for h in range(hp)]
        scores = [_gla_scores(q_ref[rows, ksl[h]], k_ref[rows, ksl[h]], e2_all[:, ksl[h]], la2_all[:, ksl[h]], role_ref,
                              mask_ref, nlev, c)[0] for h in range(hp)]
        for h in range(hp):
            b = e2_all[(nlev - 1) * c:nlev * c, ksl[h]]
            qf = q_ref[rows, ksl[h]].astype(F32)
            o = (_dot(scores[h].astype(BF16), v_ref[rows, vsl[h]])
                 + _dot((qf * jnp.exp2(b)).astype(BF16), s_scr[h].astype(BF16)))
            o_ref[rows, vsl[h]] = _gla_out(o, g_ref[...], r_ref[rows, vsl[h]])
        for h in range(hp):
            b = e2_all[(nlev - 1) * c:nlev * c, ksl[h]]
            kf = k_ref[rows, ksl[h]].astype(F32)
            kt = (kf * jnp.exp2(b[c - 1:c, :] - b)).astype(BF16)
            dec = jnp.exp2(jnp.transpose(b[c - 8:c, :]))[:, 7:8]
            s_scr[h] = dec * s_scr[h] + _dot_tn(kt, v_ref[rows, vsl[h]])
        return carry

    lax.fori_loop(0, nchunks, chunk, 0, unroll=4)
    s_ref[0] = s_scr[...]


def _gla_prompt(qg, kg, vg, la, rg, gnorm, batch, seq):
    nlev, mall, role, mask = _gla_constants(GLA_CHUNK, GLA_CHUNK, with_rem=False)
    hp = GLA_HEADS_PER_STEP
    bh = lambda b, h: (b, h)
    const = lambda b, h: (0, 0)
    return pl.pallas_call(
        functools.partial(_gla_prompt_kernel, nlev=nlev, nchunks=seq // GLA_CHUNK),
        grid=(batch, GLA_HEADS // hp),
        in_specs=[pl.BlockSpec((seq, hp * GLA_DK), bh), pl.BlockSpec((seq, hp * GLA_DK), bh),
                  pl.BlockSpec((seq, hp * GLA_DV), bh), pl.BlockSpec((seq, hp * GLA_DK), bh),
                  pl.BlockSpec((seq, hp * GLA_DV), bh),
                  pl.BlockSpec(mall.shape, const), pl.BlockSpec(role.shape, const), pl.BlockSpec(mask.shape, const),
                  pl.BlockSpec((1, GLA_DV), const)],
        out_specs=[pl.BlockSpec((seq, hp * GLA_DV), bh),
                   pl.BlockSpec((1, hp, GLA_DK, GLA_DV), lambda b, h: (b, h, 0, 0))],
        out_shape=[jax.ShapeDtypeStruct(vg.shape, BF16),
                   jax.ShapeDtypeStruct((batch, GLA_HEADS, GLA_DK, GLA_DV), F32)],
        scratch_shapes=[pltpu.VMEM((hp, GLA_DK, GLA_DV), F32)],
        compiler_params=_cparams(("parallel", "parallel")),
        name="gla_prompt",
    )(qg, kg, vg, la, rg, mall, role, mask, gnorm)


def _gla_sample_kernel(q_ref, k_ref, v_ref, la_ref, r_ref, s0_ref, mall_ref, role_ref, mask_ref, msum_ref, g_ref,
                       o_ref, s_ref, *, nlev, t_new):
    sb = s0_ref.shape[0]
    c = sb * t_new
    spv = 8 // t_new
    e2_all, hl_all, la2_all = _gla_exponents(la_ref[...], mall_ref[...])
    seq_in_tile = lax.broadcasted_iota(jnp.int32, (8, GLA_DV), 0) // t_new
    seq_of_row = lax.broadcasted_iota(jnp.int32, (c, GLA_DV), 0) // t_new
    ksl = [slice(h * GLA_DK, (h + 1) * GLA_DK) for h in range(GLA_HEADS)]
    scores = [_gla_scores(q_ref[:, ksl[h]], k_ref[:, ksl[h]], e2_all[:, ksl[h]], la2_all[:, ksl[h]], role_ref, mask_ref,
                          nlev, c) for h in range(GLA_HEADS)]
    for h in range(GLA_HEADS):
        ks = ksl[h]
        vs = slice(h * GLA_DV, (h + 1) * GLA_DV)
        v = v_ref[:, vs]
        e2 = e2_all[:, ks]
        hl = hl_all[:, ks]
        a, qf, kf = scores[h]
        qe = (qf * jnp.exp2(e2[(nlev - 1) * c:nlev * c])).astype(BF16)
        kt_t = jnp.transpose(kf * jnp.exp2(e2[nlev * c:(nlev + 1) * c])).astype(BF16)
        bl_t = jnp.transpose(_dot(msum_ref[...], hl[:c]) + _dot(msum_ref[...], hl[c:]))
        dec_t = jnp.exp2(bl_t)
        inter = []
        for vr in range(c // 8):
            rows8 = qe[8 * vr:8 * vr + 8]
            tile = None
            for u in range(spv):
                j = vr * spv + u
                s0 = s0_ref[j, h]
                r = _dot(rows8, s0.astype(BF16))
                tile = r if tile is None else jnp.where(seq_in_tile == u, r, tile)
                vj = jnp.where(seq_of_row == j, v, jnp.zeros_like(v))
                s_ref[j, h] = dec_t[:, j:j + 1] * s0 + _dot(kt_t, vj)
            inter.append(tile)
        o = _dot(a.astype(BF16), v) + jnp.concatenate(inter, axis=0)
        o_ref[:, vs] = _gla_out(o, g_ref[...], r_ref[:, vs])


def _gla_sample(qg, kg, vg, la, rg, state, gnorm, batch, t_new):
    sb = SAMPLE_SEQ_BLOCK
    c = sb * t_new
    assert 8 % t_new == 0 and c % 8 == 0
    nlev, mall, role, mask = _gla_constants(c, t_new, with_rem=True)
    msum = jnp.asarray((np.arange(c)[None, :] // t_new == np.arange(sb)[:, None]).astype(np.float32), BF16)
    rows = lambda i: (i, 0)
    const = lambda i: (0, 0)
    st = lambda i: (i, 0, 0, 0)
    return pl.pallas_call(
        functools.partial(_gla_sample_kernel, nlev=nlev, t_new=t_new),
        grid=(batch // sb,),
        in_specs=[pl.BlockSpec((c, GLA_KEY_WIDTH), rows), pl.BlockSpec((c, GLA_KEY_WIDTH), rows),
                  pl.BlockSpec((c, GLA_VALUE_WIDTH), rows), pl.BlockSpec((c, GLA_KEY_WIDTH), rows),
                  pl.BlockSpec((c, GLA_VALUE_WIDTH), rows),
                  pl.BlockSpec((sb, GLA_HEADS, GLA_DK, GLA_DV), st),
                  pl.BlockSpec(mall.shape, const), pl.BlockSpec(role.shape, const), pl.BlockSpec(mask.shape, const),
                  pl.BlockSpec(msum.shape, const), pl.BlockSpec((1, GLA_DV), const)],
        out_specs=[pl.BlockSpec((c, GLA_VALUE_WIDTH), rows), pl.BlockSpec((sb, GLA_HEADS, GLA_DK, GLA_DV), st)],
        out_shape=[jax.ShapeDtypeStruct(vg.shape, BF16), jax.ShapeDtypeStruct(state.shape, F32)],
        compiler_params=_cparams(("parallel",)),
        name="gla_sample",
    )(qg, kg, vg, la, rg, state, mall, role, mask, msum, gnorm)


def _post_kernel(x_ref, a_ref, gl_ref, ga_ref, gb_ref, wpa, wpg, wo, nf_ref, wr_hi, wr_lo, br, x1t_o, rt_o):
    tm = x_ref.shape[0]
    part = POST_PART_ROWS if tm % POST_PART_ROWS == 0 else tm
    halves = [slice(r0, r0 + part) for r0 in range(0, tm, part)]
    proj =[(_dot(a_ref[r, :], wpa[...]), _dot(gl_ref[r, :], wpg[...])) for r in halves]
    x1s = []
    for r, (pa, pg) in zip(halves, proj):
        merged = ga_ref[r, :].astype(F32) * pa + gb_ref[r, :].astype(F32) * pg
        x1s.append(x_ref[r, :] + _dot(merged.astype(BF16), wo[...]))
    logits = []
    for r, x1 in zip(halves, x1s):
        rows = r.stop - r.start
        for j in range(TOKEN_ROWS):
            x1t_o[pl.ds(r.start * TOKEN_ROWS + j, rows, stride=TOKEN_ROWS), :] = x1[:, j * LANES:(j + 1) * LANES]
        h_hi, h_lo = _split_bf16(_rms(x1, nf_ref[...]))
        logits.append(_dot_nt(wr_hi[...], h_hi) + _dot_nt(wr_hi[...], h_lo) + _dot_nt(wr_lo[...], h_hi))
    nrow = ROUTER_ROWS
    big = jnp.int32(LANES)
    ninf = -jnp.inf
    for r, lt in zip(halves, logits):
        lt = lt[:nrow] + br[:nrow, 0:1]
        row = lax.broadcasted_iota(jnp.int32, lt.shape, 0)

        def first_max(vals):
            mx = jnp.max(vals, axis=0, keepdims=True)
            return mx, jnp.min(jnp.where(vals == mx, row, big), axis=0, keepdims=True)

        gl = jnp.where((row >= N_EXPERTS) & (row < N_EXPERTS + N_GROUPS), lt, ninf)
        gmax, gidx = first_max(gl)
        p_sel = 1.0 / jnp.sum(jnp.exp(gl - gmax), axis=0, keepdims=True)
        lo = (gidx - N_EXPERTS) * EXPERTS_PER_GROUP
        el = jnp.where((row >= lo) & (row < lo + EXPERTS_PER_GROUP), lt, ninf)
        v1, i1 = first_max(el)
        el2 = jnp.where(row == i1, ninf, el)
        v2, i2 = first_max(el2)
        t = jnp.exp(v2 - v1)
        w1 = p_sel / (1.0 + t)
        w2 = p_sel * t / (1.0 + t)
        row8 = lax.broadcasted_iota(jnp.int32, (rt_o.shape[0], r.stop - r.start), 0)
        pick = lambda k, val, rest: jnp.where(row8 == k, val, rest)
        rt_o[:, r] = pick(0, i1.astype(F32), pick(1, i2.astype(F32), pick(2, w1, pick(3, w2, 0.0))))


def _post(x2d, a_out, g_out, ga, gb, w, tm):
    n = x2d.shape[0]
    row = lambda i: (i, 0)
    const = lambda i: (0, 0)
    wnames = ("wpa", "wpg", "wo", "nf", "wr_hi", "wr_lo", "br")
    rt_rows = 8
    return pl.pallas_call(
        _post_kernel,
        grid=(n // tm,),
        in_specs=[pl.BlockSpec((tm, D_MODEL), row), pl.BlockSpec((tm, ATTN_WIDTH), row),
                  pl.BlockSpec((tm, GLA_VALUE_WIDTH), row), pl.BlockSpec((tm, D_MODEL), row),
                  pl.BlockSpec((tm, D_MODEL), row)] + [pl.BlockSpec(w[k].shape, const) for k in wnames],
        out_specs=[pl.BlockSpec((tm * TOKEN_ROWS, LANES), row),
                   pl.BlockSpec((rt_rows, tm), lambda i: (0, i))],
        out_shape=[jax.ShapeDtypeStruct((n * TOKEN_ROWS, LANES), F32), jax.ShapeDtypeStruct((rt_rows, n), F32)],
        compiler_params=_cparams(("parallel",)),
        name="post_mixer",
    )(x2d, a_out, g_out, ga, gb, *[w[k] for k in wnames])


def _moe_plan(rt, tme):
    n = rt.shape[1]
    ntiles = n // tme
    max_items = ntiles + N_GROUPS - 1
    i1, i2 = rt[0].astype(jnp.int32), rt[1].astype(jnp.int32)
    grp = i1 // EXPERTS_PER_GROUP
    lo = jnp.minimum(i1, i2) % EXPERTS_PER_GROUP
    hi = jnp.maximum(i1, i2) % EXPERTS_PER_GROUP
    snake = lambda l, h: jnp.where(l % 2 == 0, h, EXPERTS_PER_GROUP + l - h)
    key = (grp * EXPERTS_PER_GROUP + lo) * EXPERTS_PER_GROUP + snake(lo, hi)
    order = jnp.argsort(key, stable=True).astype(jnp.int32)
    skey = key[order].reshape(ntiles, tme)
    sg = skey // (EXPERTS_PER_GROUP * EXPERTS_PER_GROUP)
    slo = (skey // EXPERTS_PER_GROUP) % EXPERTS_PER_GROUP
    shi = snake(slo, skey % EXPERTS_PER_GROUP)
    ev = jnp.arange(EXPERTS_PER_GROUP)
    in_g = sg[:, :, None] == jnp.arange(N_GROUPS)
    uses_e = (slo[:, :, None] == ev) | (shi[:, :, None] == ev)
    flags_tge = jnp.any(in_g[:, :, :, None] & uses_e[:, :, None, :], axis=1)
    present = jnp.any(in_g, axis=1).reshape(-1)
    pos = jnp.cumsum(present) - 1
    n_items = pos[-1] + 1
    src = jnp.zeros((max_items,), jnp.int32).at[jnp.where(present, pos, max_items)].set(
        jnp.arange(ntiles * N_GROUPS, dtype=jnp.int32), mode="drop")
    it = jnp.arange(max_items)
    valid = it < n_items
    last_src = src[n_items - 1]
    src = jnp.where(valid, src, last_src)
    item_tile = src // N_GROUPS
    item_group = src % N_GROUPS
    prev_tile = jnp.concatenate([jnp.full((1,), -1, jnp.int32), item_tile[:-1]])
    next_tile = jnp.concatenate([item_tile[1:], jnp.full((1,), -1, jnp.int32)])
    first = valid & (item_tile != prev_tile)
    last = valid & ((item_tile != next_tile) | (it == n_items - 1))
    flags = flags_tge[item_tile, item_group] & valid[:, None]
    e0 = jnp.argmax(flags, axis=1)
    rest = flags & (ev[None, :] != e0[:, None])
    e1 = jnp.argmax(rest, axis=1)
    rest = rest & (ev[None, :] != e1[:, None])
    has2 = jnp.any(rest, axis=1)
    e2 = jnp.argmax(rest, axis=1)
    rest = rest & (ev[None, :] != e2[:, None])
    rt_sorted = rt[:, order].reshape(rt.shape[0], ntiles, tme).transpose(1, 0, 2)
    i32 = lambda z: z.astype(jnp.int32)
    n_rest = jnp.sum(rest, axis=1)
    rest_list = jnp.argsort(jnp.logical_not(rest), axis=1, stable=True)
    plan = (order, i32(item_tile), i32(item_group), i32(first), i32(last), i32(valid), i32(e0), i32(e1), i32(e2),
            i32(has2), i32(n_rest), i32(rest_list.reshape(-1)))
    return plan, rt_sorted


def _moe_kernel(order, itile, igroup, ifirst, ilast, ivalid, ie0, ie1, ie2, ihas2, inrest, irest, x_hbm, rt_ref, wg, wu, wd,
                nffn, nfin,
                y_hbm, xbuf, ybuf, acc, hbuf, rcol, gsem, ssem, *, tme, ntiles):
    i = pl.program_id(0)
    t = itile[i]
    slot = t % 2
    g = igroup[i]
    is_first = ifirst[i] == 1

    def gather_row(tile, sl, r):
        tok = order[tile * tme + r]
        src = x_hbm.at[pl.ds(pl.multiple_of(tok * TOKEN_ROWS, TOKEN_ROWS), TOKEN_ROWS)]
        dst = xbuf.at[pl.ds(pl.multiple_of((sl * tme + r) * TOKEN_ROWS, TOKEN_ROWS), TOKEN_ROWS)]
        return pltpu.make_async_copy(src, dst, gsem.at[sl])

    def scatter_row(tile, sl, r):
        tok = order[tile * tme + r]
        return pltpu.make_async_copy(ybuf.at[sl, pl.ds(r, 1)], y_hbm.at[pl.ds(tok, 1)], ssem.at[sl])

    def start_rows(make, tile, sl):
        def body(r, c):
            make(tile, sl, r).start()
            return c
        lax.fori_loop(0, tme, body, 0, unroll=8)

    def wait_gather(sl):
        rows = tme * TOKEN_ROWS
        pltpu.make_async_copy(x_hbm.at[pl.ds(0, rows)], xbuf.at[pl.ds(pl.multiple_of(sl * rows, rows), rows)],
                              gsem.at[sl]).wait()

    def wait_scatter(sl):
        pltpu.make_async_copy(ybuf.at[sl], y_hbm.at[pl.ds(0, tme)], ssem.at[sl]).wait()

    def start_rows_inline(make, tile, sl, r0=0, r1=tme):
        for r in range(r0, r1):
            make(tile, sl, r).start(priority=r % 2)

    def expert(e):
        eid = (g * EXPERTS_PER_GROUP + e).astype(F32)
        ce = (jnp.where(rcol[:, 0:1] == eid, rcol[:, 2:3], 0.0)
              + jnp.where(rcol[:, 1:2] == eid, rcol[:, 3:4], 0.0))
        h = hbuf[...]
        a = _dot(h, wg[e])
        u = _dot(h, wu[e])
        act = (a * _sigmoid(a)) * u * ce
        acc[...] += _dot(act.astype(BF16), wd[e])

    def expert_with(e, run, alternatives):
        plain = run
        for cond, side_work in alternatives:
            @pl.when(cond)
            def _():
                side_work()
                expert(e)

            plain = jnp.logical_and(plain, jnp.logical_not(cond))

        @pl.when(plain)
        def _():
            expert(e)

    @pl.when(i == 0)
    def _():
        start_rows(gather_row, 0, 0)

    @pl.when(is_first)
    def _():
        wait_gather(slot)
        base = pl.multiple_of(slot * (tme * TOKEN_ROWS), tme * TOKEN_ROWS)
        x1 = jnp.concatenate([xbuf[pl.ds(base + j, tme, stride=TOKEN_ROWS), :] for j in range(TOKEN_ROWS)], axis=1)
        acc[...] = x1
        hbuf[...] = _rms(x1, nffn[...]).astype(BF16)
        rt = rt_ref[0]
        rcol[...] = jnp.transpose(jnp.concatenate([rt, jnp.zeros((LANES - rt.shape[0], tme), F32)], axis=0))

    valid = ivalid[i] == 1
    has2 = ihas2[i] == 1
    do_gather = jnp.logical_and(is_first, t + 1 < ntiles)
    do_scatter = jnp.logical_and(is_first, t >= 1)
    half = tme // 2
    scatter_rows = lambda r0, r1: (lambda: start_rows_inline(scatter_row, t - 1, 1 - slot, r0, r1))
    expert_with(ie0[i], valid, [(do_gather, lambda: start_rows_inline(gather_row, t + 1, 1 - slot))])
    expert_with(ie1[i], valid, [(jnp.logical_and(do_scatter, has2), scatter_rows(0, half)),
                                (jnp.logical_and(do_scatter, jnp.logical_not(has2)), scatter_rows(0, tme))])
    expert_with(ie2[i], has2, [(jnp.logical_and(do_scatter, has2), scatter_rows(half, tme))])

    def rest_expert(k, carry):
        expert(irest[i * EXPERTS_PER_GROUP + k])
        return carry

    lax.fori_loop(0, inrest[i], rest_expert, 0)

    @pl.when(ilast[i] == 1)
    def _():
        @pl.when(t >= 2)
        def _():
            wait_scatter(slot)

        ybuf[slot] = _rms(acc[...], nfin[...])

        @pl.when(t == ntiles - 1)
        def _():
            start_rows(scatter_row, t, slot)

    @pl.when(i == pl.num_programs(0) - 1)
    def _():
        for sl in range(min(2, ntiles)):
            wait_scatter(sl)


def _moe(x1t, rt, weg, weu, wed, nffn, nfin):
    n = x1t.shape[0] // TOKEN_ROWS
    tme = min(MOE_TILE, n)
    assert n % tme == 0
    ntiles = n // tme
    plan, rt_sorted = _moe_plan(rt, tme)
    max_items = ntiles + N_GROUPS - 1
    grp = lambda i, order, itile, igroup, *_: (igroup[i], 0, 0)
    til = lambda i, order, itile, *_: (itile[i], 0, 0)
    const = lambda i, *_: (0, 0)
    grid_spec = pltpu.PrefetchScalarGridSpec(
        num_scalar_prefetch=len(plan),
        grid=(max_items,),
        in_specs=[pl.BlockSpec(memory_space=pl.ANY),
                  pl.BlockSpec((1,) + rt_sorted.shape[1:], til),
                  pl.BlockSpec((EXPERTS_PER_GROUP, D_MODEL, EXPERT_FF), grp),
                  pl.BlockSpec((EXPERTS_PER_GROUP, D_MODEL, EXPERT_FF), grp),
                  pl.BlockSpec((EXPERTS_PER_GROUP, EXPERT_FF, D_MODEL), grp),
                  pl.BlockSpec((1, D_MODEL), const), pl.BlockSpec((1, D_MODEL), const)],
        out_specs=pl.BlockSpec(memory_space=pl.ANY),
        scratch_shapes=[pltpu.VMEM((2 * tme * TOKEN_ROWS, LANES), F32), pltpu.VMEM((2, tme, D_MODEL), F32),
                        pltpu.VMEM((tme, D_MODEL), F32), pltpu.VMEM((tme, D_MODEL), BF16),
                        pltpu.VMEM((tme, LANES), F32),
                        pltpu.SemaphoreType.DMA((2,)), pltpu.SemaphoreType.DMA((2,))],
    )
    return pl.pallas_call(
        functools.partial(_moe_kernel, tme=tme, ntiles=ntiles),
        grid_spec=grid_spec,
        out_shape=jax.ShapeDtypeStruct((n, D_MODEL), F32),
        compiler_params=_cparams(("arbitrary",)),
        name="moe",
    )(*plan, x1t, rt_sorted, weg, weu, wed, nffn, nfin)


def _rope_tables(positions):
    half = HEAD_DIM // 2
    inv_freq = ROPE_THETA ** (-jnp.arange(half, dtype=F32) / half)
    ang = positions.astype(F32)[:, None] * inv_freq[None, :]
    cos, sin = jnp.cos(ang), jnp.sin(ang)
    reps = LANES // HEAD_DIM
    return (jnp.tile(jnp.concatenate([cos, cos], -1), (1, reps)),
            jnp.tile(jnp.concatenate([-sin, sin], -1), (1, reps)))


def _prep_weights(norm_mix, w_in, w_gla_f2, b_gla_f, gla_norm, w_proj_attn, w_proj_gla, w_out, norm_ffn,
                  w_router_group, b_router_group, w_router_expert, b_router_expert):
    w = {"win": w_in.astype(BF16)}
    w["wf2"] = jnp.pad(w_gla_f2.astype(BF16), ((0, LANES - GLA_GATE_RANK), (0, 0)))
    w["bf"] = b_gla_f.reshape(1, -1)
    w["norm_mix"] = norm_mix.reshape(1, -1)
    w["gla_norm"] = gla_norm.reshape(1, -1)
    w["wpa"] = w_proj_attn.astype(BF16)
    w["wpg"] = w_proj_gla.astype(BF16)
    w["wo"] = w_out.astype(BF16)
    w["nf"] = norm_ffn.reshape(1, -1)
    pad = LANES - N_EXPERTS - N_GROUPS
    wr_t = jnp.pad(jnp.concatenate([w_router_expert, w_router_group], axis=1), ((0, 0), (0, pad))).T
    w["wr_hi"], w["wr_lo"] = _split_bf16(wr_t)
    w["br"] = jnp.broadcast_to(jnp.pad(jnp.concatenate([b_router_expert, b_router_group]), (0, pad))[:, None],
                               (LANES, LANES))
    return w


def _layer(x, positions_tab, cache, w, sinks, weg, weu, wed, nfin, tm):
    batch, seq, _ = x.shape
    n = batch * seq
    x2d = x.reshape(n, D_MODEL)
    cos_t, sin_t = positions_tab
    qa, ka, va, qg, kg, vg, rg, la, ga, gb = _inproj(x2d, w["norm_mix"], cos_t, sin_t, w, tm)
    if cache is None:
        a_out = _swa_prompt(sinks, qa, ka, va, batch, seq)
        last = lambda z: z.reshape(batch, seq, KV_WIDTH)[:, seq - WINDOW:].reshape(batch, WINDOW, ATTN_KV_HEADS, HEAD_DIM)
        new_k, new_v = last(ka), last(va)
        g_out, new_s = _gla_prompt(qg, kg, vg, la, rg, w["gla_norm"], batch, seq)
    else:
        cache_k, cache_v, state = cache
        a_out, new_k, new_v = _swa_sample(sinks, qa, ka, va, cache_k.reshape(batch, WINDOW, KV_WIDTH),
                                          cache_v.reshape(batch, WINDOW, KV_WIDTH), batch, seq)
        new_k = new_k.reshape(batch, WINDOW, ATTN_KV_HEADS, HEAD_DIM)
        new_v = new_v.reshape(batch, WINDOW, ATTN_KV_HEADS, HEAD_DIM)
        g_out, new_s = _gla_sample(qg, kg, vg, la, rg, state, w["gla_norm"], batch, seq)
    x1t, rt = _post(x2d, a_out, g_out, ga, gb, w, tm * 2 if x2d.shape[0] % (tm * 2) == 0 else tm)
    y = _moe(x1t, rt, weg, weu, wed, w["nf"], nfin)
    return y.reshape(batch, seq, D_MODEL), new_k, new_v, new_s


def kernel(x_prompt, x_sample, cache_win_k, cache_win_v, state_gla, norm_mix, w_in, w_gla_f2, b_gla_f, gla_norm,
           attn_sinks, w_proj_attn, w_proj_gla, w_out, norm_ffn, w_router_group, b_router_group, w_router_expert,
           b_router_expert, w_exp_gate, w_exp_up, w_exp_down, norm_final):
    assert norm_mix.shape[0] == 1, "single-layer step"
    seq_p = x_prompt.shape[1]
    dec_b, dec_t = x_sample.shape[0], x_sample.shape[1]
    w = _prep_weights(norm_mix[0], w_in[0], w_gla_f2[0], b_gla_f[0], gla_norm[0], w_proj_attn[0], w_proj_gla[0],
                      w_out[0], norm_ffn[0], w_router_group[0], b_router_group[0], w_router_expert[0],
                      b_router_expert[0])
    weg = w_exp_gate[0].astype(BF16)
    weu = w_exp_up[0].astype(BF16)
    wed = w_exp_down[0].astype(BF16)
    nfin = norm_final.reshape(1, -1)
    sinks = attn_sinks[0]
    tab_p = _rope_tables(jnp.arange(seq_p, dtype=jnp.int32))
    pos_s = PAST_LEN + jnp.arange(dec_t, dtype=jnp.int32)
    tab_s = tuple(jnp.tile(t, (dec_b, 1)) for t in _rope_tables(pos_s))
    tm_p = min(512, seq_p)
    tm_s = dec_b * dec_t
    yp, pk, pv, ps = _layer(x_prompt, tab_p, None, w, sinks, weg, weu, wed, nfin, tm_p)
    ys, sk, sv, ss = _layer(x_sample, tab_s, (cache_win_k[0], cache_win_v[0], state_gla[0]), w, sinks, weg, weu, wed,
                            nfin, tm_s)
    return (yp, ys, pk[None], pv[None], ps[None], sk[None], sv[None], ss[None])
```

```python
import functools
import math

import numpy as np
import jax
import jax.numpy as jnp
from jax import lax
from jax.experimental import pallas as pl
from jax.experimental.pallas import tpu as pltpu

F32 = jnp.float32
BF16 = jnp.bfloat16

D_MODEL = 1024
ATTN_HEADS = 8
ATTN_KV_HEADS = 2
GROUP = ATTN_HEADS // ATTN_KV_HEADS
HEAD_DIM = 64
ATTN_WIDTH = ATTN_HEADS * HEAD_DIM
KV_WIDTH = ATTN_KV_HEADS * HEAD_DIM
WINDOW = 128
ROPE_THETA = 10000.0
PAST_LEN = 8192
GLA_HEADS = 4
GLA_KEY_WIDTH = D_MODEL // 2
GLA_VALUE_WIDTH = D_MODEL
GLA_DK = GLA_KEY_WIDTH // GLA_HEADS
GLA_DV = GLA_VALUE_WIDTH // GLA_HEADS
GLA_GATE_RANK = 16
GLA_GATE_NORMALIZER = 16.0
N_GROUPS = 4
EXPERTS_PER_GROUP = 8
N_EXPERTS = N_GROUPS * EXPERTS_PER_GROUP
EXPERT_FF = 256
EPS = 1e-6
LOG2_E = 1.4426950408889634

LANES = 128
GLA_CHUNK = 128
GLA_HEADS_PER_STEP = 4
SWA_BLOCKS_PER_STEP = 4
SAMPLE_SEQ_BLOCK = 16
POST_PART_ROWS = 512
MOE_TILE = 256
TOKEN_ROWS = D_MODEL // LANES
INPROJ_WIDTHS = (ATTN_WIDTH, 2 * KV_WIDTH, GLA_KEY_WIDTH, GLA_KEY_WIDTH, GLA_VALUE_WIDTH, GLA_VALUE_WIDTH)
ROUTER_ROWS = 40
VMEM_LIMIT = 56 * 1024 * 1024


def _cparams(sem):
    return pltpu.CompilerParams(dimension_semantics=sem, vmem_limit_bytes=VMEM_LIMIT)


def _rms(x, g):
    return x * lax.rsqrt(jnp.mean(x * x, axis=-1, keepdims=True) + EPS) * g


def _sigmoid(x):
    return 1.0 / (1.0 + jnp.exp(-x))


def _dot(a, b):
    return jnp.dot(a, b, preferred_element_type=F32)


def _dot_nt(a, b):
    return lax.dot_general(a, b, (((1,), (1,)), ((), ())), preferred_element_type=F32)


def _dot_tn(a, b):
    return lax.dot_general(a, b, (((0,), (0,)), ((), ())), preferred_element_type=F32)


def _split_bf16(x):
    hi = x.astype(BF16)
    lo = (x - hi.astype(F32)).astype(BF16)
    return hi, lo


def _inproj_kernel(x_ref, g_ref, cos_ref, sin_ref, win, wf2, bf,
                   qa_o, ka_o, va_o, qg_o, kg_o, vg_o, rg_o, la_o, ga_o, gb_o, wgate):
    f0 = sum(INPROJ_WIDTHS)

    @pl.when(pl.program_id(0) == 0)
    def _():
        tail = win[:, f0:]
        wgate[...] = tail[:, GLA_GATE_RANK:GLA_GATE_RANK + 2 * D_MODEL]

    tm = x_ref.shape[0]
    halves = [slice(0, tm // 2), slice(tm // 2, tm)] if tm % 16 == 0 else [slice(0, tm)]
    hbs = [_rms(x_ref[r, :], g_ref[...]).astype(BF16) for r in halves]
    cols = np.cumsum((0,) + INPROJ_WIDTHS[:-1])
    wqa, wkva, wqg, wkg, wvg, wrg = (win.at[:, int(c):int(c) + wd] for c, wd in zip(cols, INPROJ_WIDTHS))
    wga = wgate.at[:, :D_MODEL]
    wgb = wgate.at[:, D_MODEL:]
    wf = win.at[:, f0:f0 + LANES]
    lane = lax.broadcasted_iota(jnp.int32, (halves[0].stop - halves[0].start, LANES), 1)
    first_half = (lane % HEAD_DIM) < (HEAD_DIM // 2)

    def rope(t, r):
        swapped = jnp.where(first_half, pltpu.roll(t, LANES - HEAD_DIM // 2, 1), pltpu.roll(t, HEAD_DIM // 2, 1))
        return t * cos_ref[r, :] + swapped * sin_ref[r, :]

    col_halves = [slice(0, D_MODEL // 2), slice(D_MODEL // 2, D_MODEL)]
    for cs in col_halves:
        for r, hb in zip(halves, hbs):
            ga_o[r, cs] = _sigmoid(_dot(hb, wga[:, cs])).astype(BF16)
    for cs in col_halves:
        for r, hb in zip(halves, hbs):
            gb_o[r, cs] = _sigmoid(_dot(hb, wgb[:, cs])).astype(BF16)
    for cs in col_halves:
        for r, hb in zip(halves, hbs):
            rz = _dot(hb, wrg[:, cs])
            rg_o[r, cs] = (rz * _sigmoid(rz)).astype(BF16)
    for r, hb in zip(halves, hbs):
        z = _dot(_dot(hb, wf[...]).astype(BF16), wf2[...]) + bf[...]
        la_o[r, :] = (jnp.minimum(z, 0.0) - jnp.log1p(jnp.exp(-jnp.abs(z)))) * (1.0 / GLA_GATE_NORMALIZER)
    for r, hb in zip(halves, hbs):
        qa = _dot(hb, wqa[...])
        for c in range(ATTN_WIDTH // LANES):
            sl = slice(c * LANES, (c + 1) * LANES)
            qa_o[r, sl] = (rope(qa[:, sl], r) * (HEAD_DIM ** -0.5)).astype(BF16)
    for r, hb in zip(halves, hbs):
        kva = _dot(hb, wkva[...])
        ka_o[r, :] = rope(kva[:, :KV_WIDTH], r)
        va_o[r, :] = kva[:, KV_WIDTH:]
    for r, hb in zip(halves, hbs):
        qg_o[r, :] = (_dot(hb, wqg[...]) * (GLA_DK ** -0.5)).astype(BF16)
    for r, hb in zip(halves, hbs):
        kg_o[r, :] = _dot(hb, wkg[...]).astype(BF16)
    for cs in col_halves:
        for r, hb in zip(halves, hbs):
            vg_o[r, cs] = _dot(hb, wvg[:, cs]).astype(BF16)


def _inproj(x2d, norm_g, cos_t, sin_t, w, tm):
    n = x2d.shape[0]
    ntab = cos_t.shape[0] // tm
    row = lambda i: (i, 0)
    const = lambda i: (0, 0)
    tab = lambda i: (i % ntab, 0)
    wnames = ("win", "wf2", "bf")
    wspecs = [pl.BlockSpec(w[k].shape, const) for k in wnames]
    widths = (ATTN_WIDTH, KV_WIDTH, KV_WIDTH, GLA_KEY_WIDTH, GLA_KEY_WIDTH, GLA_VALUE_WIDTH, GLA_VALUE_WIDTH,
              GLA_KEY_WIDTH, D_MODEL, D_MODEL)
    dtypes = (BF16, F32, F32, BF16, BF16, BF16, BF16, F32, BF16, BF16)
    return pl.pallas_call(
        _inproj_kernel,
        grid=(n // tm,),
        in_specs=[pl.BlockSpec((tm, D_MODEL), row), pl.BlockSpec((1, D_MODEL), const),
                  pl.BlockSpec((tm, LANES), tab), pl.BlockSpec((tm, LANES), tab)] + wspecs,
        out_specs=[pl.BlockSpec((tm, wd), row) for wd in widths],
        out_shape=[jax.ShapeDtypeStruct((n, wd), dt) for wd, dt in zip(widths, dtypes)],
        scratch_shapes=[pltpu.VMEM((D_MODEL, 2 * D_MODEL), BF16)],
        compiler_params=_cparams(("arbitrary",)),
        name="inproj",
    )(x2d, norm_g, cos_t, sin_t, *[w[k] for k in wnames])


def _swa_prompt_kernel(sink_ref, q_ref, kc_ref, kp_ref, vc_ref, vp_ref, o_ref, *, qb):
    n = pl.program_id(1)
    k3 = jnp.concatenate([kp_ref[...], kc_ref[...]], axis=0).astype(BF16)
    v3 = jnp.concatenate([vp_ref[...], vc_ref[...]], axis=0).astype(BF16)
    t = lax.broadcasted_iota(jnp.int32, (WINDOW, 2 * WINDOW), 0)
    j = lax.broadcasted_iota(jnp.int32, (WINDOW, 2 * WINDOW), 1)
    band = (j >= t) & (j <= t + WINDOW)
    zeros = jnp.zeros((2 * WINDOW, HEAD_DIM), BF16)
    ones = jnp.ones((2 * WINDOW, HEAD_DIM), BF16)
    lane = lax.broadcasted_iota(jnp.int32, (WINDOW, LANES), 1)
    chains = [(blk, h) for blk in range(qb) for h in range(ATTN_HEADS)]
    scores = []
    for blk, h in chains:
        kv = h // GROUP
        rows = slice(blk * WINDOW, (blk + 1) * WINDOW)
        keys = slice(blk * WINDOW, (blk + 2) * WINDOW)
        s = _dot_nt(q_ref[rows, h * HEAD_DIM:(h + 1) * HEAD_DIM], k3[keys, kv * HEAD_DIM:(kv + 1) * HEAD_DIM])
        valid = band & ((j >= WINDOW) | (n > 0)) if blk == 0 else band
        scores.append(jnp.where(valid, s, -jnp.inf))
    probs, sink_terms = [], []
    for (blk, h), s in zip(chains, scores):
        m = jnp.maximum(jnp.max(s, axis=-1, keepdims=True), sink_ref[h])
        probs.append(jnp.exp(s - m).astype(BF16))
        sink_terms.append(jnp.exp(sink_ref[h] - m))
    for blk in range(qb):
        rows = slice(blk * WINDOW, (blk + 1) * WINDOW)
        keys = slice(blk * WINDOW, (blk + 2) * WINDOW)
        for kv in range(ATTN_KV_HEADS):
            vv = v3[keys, kv * HEAD_DIM:(kv + 1) * HEAD_DIM]
            vext = (jnp.concatenate([vv, zeros, ones, zeros], axis=1), jnp.concatenate([zeros, vv, zeros, ones], axis=1))
            for pr in range(GROUP // 2):
                h0 = kv * GROUP + pr * 2
                c0 = blk * ATTN_HEADS + h0
                acc = _dot(probs[c0], vext[0]) + _dot(probs[c0 + 1], vext[1])
                l = acc[:, LANES:] + jnp.where(lane < HEAD_DIM, sink_terms[c0], sink_terms[c0 + 1])
                o_ref[rows, h0 * HEAD_DIM:h0 * HEAD_DIM + LANES] = (acc[:, :LANES] / l).astype(BF16)


def _swa_prompt(sinks, qa, ka, va, batch, seq):
    nb = seq // WINDOW
    qb = math.gcd(SWA_BLOCKS_PER_STEP, nb)
    steps = nb // qb
    cur = lambda b, n: (b * steps + n, 0)
    prev = lambda b, n: (b * nb + jnp.maximum(n * qb - 1, 0), 0)
    return pl.pallas_call(
        functools.partial(_swa_prompt_kernel, qb=qb),
        grid=(batch, steps),
        in_specs=[pl.BlockSpec(memory_space=pltpu.SMEM),
                  pl.BlockSpec((qb * WINDOW, ATTN_WIDTH), cur),
                  pl.BlockSpec((qb * WINDOW, KV_WIDTH), cur), pl.BlockSpec((WINDOW, KV_WIDTH), prev),
                  pl.BlockSpec((qb * WINDOW, KV_WIDTH), cur), pl.BlockSpec((WINDOW, KV_WIDTH), prev)],
        out_specs=pl.BlockSpec((qb * WINDOW, ATTN_WIDTH), cur),
        out_shape=jax.ShapeDtypeStruct(qa.shape, BF16),
        compiler_params=_cparams(("parallel", "parallel")),
        name="swa_prompt",
    )(sinks, qa, ka, ka, va, va)


def _swa_sample_kernel(sink_ref, q_ref, kn_ref, vn_ref, kn3_ref, vn3_ref, ck_ref, cv_ref, o_ref, nk_ref, nv_ref,
                       *, t_new):
    sb = ck_ref.shape[0]
    spv = 8 // t_new
    nq = GROUP * 8
    nc = spv * WINDOW
    qi = lax.broadcasted_iota(jnp.int32, (nq, nc), 0) % 8
    ci = lax.broadcasted_iota(jnp.int32, (nq, nc), 1)
    valid_c = (qi // t_new == ci // WINDOW) & (ci % WINDOW >= qi % t_new)
    qn = lax.broadcasted_iota(jnp.int32, (nq, 8), 0) % 8
    cn = lax.broadcasted_iota(jnp.int32, (nq, 8), 1)
    valid_n = (qn // t_new == cn // t_new) & (cn <= qn)
    grow = lax.broadcasted_iota(jnp.int32, (nq, 1), 0) // 8
    chains = [(vr, kv) for vr in range(sb // spv) for kv in range(ATTN_KV_HEADS)]
    scored = []
    for vr, kv in chains:
        r8 = slice(8 * vr, 8 * vr + 8)
        cs = slice(kv * HEAD_DIM, (kv + 1) * HEAD_DIM)
        heads = [kv * GROUP + g for g in range(GROUP)]
        qs = jnp.concatenate([q_ref[r8, h * HEAD_DIM:(h + 1) * HEAD_DIM] for h in heads], axis=0)
        ck = ck_ref[vr * spv:(vr + 1) * spv, :, cs].reshape(nc, HEAD_DIM).astype(BF16)
        kn = kn_ref[r8, cs].astype(BF16)
        scored.append((jnp.where(valid_c, _dot_nt(qs, ck), -jnp.inf), jnp.where(valid_n, _dot_nt(qs, kn), -jnp.inf)))
    soft = []
    for (vr, kv), (s_c, s_n) in zip(chains, scored):
        sink = sink_ref[kv * GROUP]
        for g in range(1, GROUP):
            sink = jnp.where(grow == g, sink_ref[kv * GROUP + g], sink)
        m = jnp.maximum(jnp.maximum(jnp.max(s_c, axis=-1, keepdims=True), jnp.max(s_n, axis=-1, keepdims=True)), sink)
        p_c = jnp.exp(s_c - m)
        p_n = jnp.exp(s_n - m)
        l = jnp.sum(p_c, axis=-1, keepdims=True) + jnp.sum(p_n, axis=-1, keepdims=True) + jnp.exp(sink - m)
        soft.append((p_c.astype(BF16), p_n.astype(BF16), l))
    for (vr, kv), (p_c, p_n, l) in zip(chains, soft):
        r8 = slice(8 * vr, 8 * vr + 8)
        cs = slice(kv * HEAD_DIM, (kv + 1) * HEAD_DIM)
        cv = cv_ref[vr * spv:(vr + 1) * spv, :, cs].reshape(nc, HEAD_DIM).astype(BF16)
        o = (_dot(p_c, cv) + _dot(p_n, vn_ref[r8, cs].astype(BF16))) / l
        for a in range(GROUP // 2):
            pair = jnp.concatenate([o[16 * a:16 * a + 8], o[16 * a + 8:16 * a + 16]], axis=1)
            c0 = (kv * GROUP + 2 * a) * HEAD_DIM
            o_ref[r8, c0:c0 + LANES] = pair.astype(BF16)
    nk_ref[:, 0:WINDOW - t_new, :] = ck_ref[:, t_new:WINDOW, :]
    nk_ref[:, WINDOW - t_new:WINDOW, :] = kn3_ref[...]
    nv_ref[:, 0:WINDOW - t_new, :] = cv_ref[:, t_new:WINDOW, :]
    nv_ref[:, WINDOW - t_new:WINDOW, :] = vn3_ref[...]


def _swa_sample(sinks, qa, ka, va, cache_k, cache_v, batch, t_new):
    sb = SAMPLE_SEQ_BLOCK
    rows = sb * t_new
    r2 = lambda i: (i, 0)
    r3 = lambda i: (i, 0, 0)
    ka3 = ka.reshape(batch, t_new, KV_WIDTH)
    va3 = va.reshape(batch, t_new, KV_WIDTH)
    return pl.pallas_call(
        functools.partial(_swa_sample_kernel, t_new=t_new),
        grid=(batch // sb,),
        in_specs=[pl.BlockSpec(memory_space=pltpu.SMEM),
                  pl.BlockSpec((rows, ATTN_WIDTH), r2),
                  pl.BlockSpec((rows, KV_WIDTH), r2), pl.BlockSpec((rows, KV_WIDTH), r2),
                  pl.BlockSpec((sb, t_new, KV_WIDTH), r3), pl.BlockSpec((sb, t_new, KV_WIDTH), r3),
                  pl.BlockSpec((sb, WINDOW, KV_WIDTH), r3), pl.BlockSpec((sb, WINDOW, KV_WIDTH), r3)],
        out_specs=[pl.BlockSpec((rows, ATTN_WIDTH), r2),
                   pl.BlockSpec((sb, WINDOW, KV_WIDTH), r3), pl.BlockSpec((sb, WINDOW, KV_WIDTH), r3)],
        out_shape=[jax.ShapeDtypeStruct(qa.shape, BF16),
                   jax.ShapeDtypeStruct(cache_k.shape, F32), jax.ShapeDtypeStruct(cache_v.shape, F32)],
        compiler_params=_cparams(("parallel",)),
        name="swa_sample",
    )(sinks, qa, ka, va, ka3, va3, cache_k, cache_v)


def _gla_constants(c, seg, with_rem):
    t = np.arange(c)
    sid = t // seg
    same = sid[:, None] == sid[None, :]
    levels = []
    m = seg // 2
    while m >= 1:
        levels.append(m)
        m //= 2
    mats, roles, masks = [], [], []
    for m in levels:
        blk = t // (2 * m)
        second = (t // m) % 2 == 1
        p = blk * 2 * m + m - 1
        u = t[None, :]
        mq = (u > p[:, None]) & (u <= t[:, None])
        mk = (u > t[:, None]) & (u <= p[:, None])
        if m > 1:
            mats.append(np.where(second[:, None], mq, mk))
        roles.append(np.broadcast_to(second[:, None], (c, LANES)))
        masks.append((blk[:, None] == blk[None, :]) & second[:, None] & ~second[None, :])
    masks.append(np.eye(c, dtype=bool))
    mats.append(same & (t[None, :] <= t[:, None]))
    if with_rem:
        mats.append(same & (t[None, :] > t[:, None]))
    mall = np.concatenate(mats, 0).astype(np.float32)
    mall = jnp.asarray(np.concatenate([mall, mall], 1), BF16)
    role = jnp.asarray(np.concatenate(roles, 0).astype(np.float32))
    mask = jnp.asarray(np.concatenate(masks, 0).astype(np.float32))
    return len(levels), mall, role, mask


def _gla_exponents(la, mall):
    la2 = la * LOG2_E
    hl = jnp.concatenate(_split_bf16(la2), axis=0)
    return _dot(mall, hl), hl, la2


def _gla_scores(qb, kb, e2, la2, role_ref, mask_ref, nlev, c):
    qf = qb.astype(F32)
    kf = kb.astype(F32)
    terms = []
    for lv in range(nlev):
        sl = slice(lv * c, (lv + 1) * c)
        m = 1 << (nlev - 1 - lv)
        if m % 8 == 0:
            pe = jnp.exp2(e2[sl])
            x = jnp.concatenate([(qf if blk % 2 else kf)[blk * m:(blk + 1) * m] * pe[blk * m:(blk + 1) * m]
                                 for blk in range(c // m)], axis=0).astype(BF16)
        else:
            second = role_ref[sl, :] > 0.5
            e = e2[sl] if lv < nlev - 1 else jnp.where(second, la2, 0.0)
            x = (jnp.where(second, qf, kf) * jnp.exp2(e)).astype(BF16)
        terms.append((x, x, lv))
    terms.append((qb, kb, nlev))
    mask = lambda i: mask_ref[i * c:(i + 1) * c, :]
    a = None
    pair = c % LANES == 0
    while terms:
        if pair and len(terms) >= 2:
            (l0, r0, i0), (l1, r1, i1) = terms.pop(), terms.pop()
            z = jnp.zeros_like(r0)
            rhs = jnp.concatenate([jnp.concatenate([r0, z], axis=1), jnp.concatenate([z, r1], axis=1)], axis=0)
            g = _dot_nt(jnp.concatenate([l0, l1], axis=1), rhs)
            t = mask(i0) * g[:, :c] + mask(i1) * g[:, c:]
        else:
            l0, r0, i0 = terms.pop()
            t = mask(i0) * _dot_nt(l0, r0)
        a = t if a is None else a + t
    return a, qf, kf


def _gla_out(o, g, gate):
    return (_rms(o, g) * gate.astype(F32)).astype(BF16)


def _gla_prompt_kernel(q_ref, k_ref, v_ref, la_ref, r_ref, mall_ref, role_ref, mask_ref, g_ref, o_ref, s_ref,
                       s_scr, *, nlev, nchunks):
    c = GLA_CHUNK
    hp = GLA_HEADS_PER_STEP
    s_scr[...] = jnp.zeros_like(s_scr)

    def chunk(i, carry):
        rows = pl.ds(pl.multiple_of(i * c, c), c)
        e2_all, _, la2_all = _gla_exponents(la_ref[rows, :], mall_ref[...])
        ksl = [slice(h * GLA_DK, (h + 1) * GLA_DK) for h in range(hp)]
        vsl = [slice(h * GLA_DV, (h + 1) * GLA_DV) for h in range(hp)]
        scores = [_gla_scores(q_ref[rows, ksl[h]], k_ref[rows, ksl[h]], e2_all[:, ksl[h]], la2_all[:, ksl[h]], role_ref,
                              mask_ref, nlev, c)[0] for h in range(hp)]
        for h in range(hp):
            b = e2_all[(nlev - 1) * c:nlev * c, ksl[h]]
            qf = q_ref[rows, ksl[h]].astype(F32)
            o = (_dot(scores[h].astype(BF16), v_ref[rows, vsl[h]])
                 + _dot((qf * jnp.exp2(b)).astype(BF16), s_scr[h].astype(BF16)))
            o_ref[rows, vsl[h]] = _gla_out(o, g_ref[...], r_ref[rows, vsl[h]])
        for h in range(hp):
            b = e2_all[(nlev - 1) * c:nlev * c, ksl[h]]
            kf = k_ref[rows, ksl[h]].astype(F32)
            kt = (kf * jnp.exp2(b[c - 1:c, :] - b)).astype(BF16)
            dec = jnp.exp2(jnp.transpose(b[c - 8:c, :]))[:, 7:8]
            s_scr[h] = dec * s_scr[h] + _dot_tn(kt, v_ref[rows, vsl[h]])
        return carry

    lax.fori_loop(0, nchunks, chunk, 0, unroll=4)
    s_ref[0] = s_scr[...]


def _gla_prompt(qg, kg, vg, la, rg, gnorm, batch, seq):
    nlev, mall, role, mask = _gla_constants(GLA_CHUNK, GLA_CHUNK, with_rem=False)
    hp = GLA_HEADS_PER_STEP
    bh = lambda b, h: (b, h)
    const = lambda b, h: (0, 0)
    return pl.pallas_call(
        functools.partial(_gla_prompt_kernel, nlev=nlev, nchunks=seq // GLA_CHUNK),
        grid=(batch, GLA_HEADS // hp),
        in_specs=[pl.BlockSpec((seq, hp * GLA_DK), bh), pl.BlockSpec((seq, hp * GLA_DK), bh),
                  pl.BlockSpec((seq, hp * GLA_DV), bh), pl.BlockSpec((seq, hp * GLA_DK), bh),
                  pl.BlockSpec((seq, hp * GLA_DV), bh),
                  pl.BlockSpec(mall.shape, const), pl.BlockSpec(role.shape, const), pl.BlockSpec(mask.shape, const),
                  pl.BlockSpec((1, GLA_DV), const)],
        out_specs=[pl.BlockSpec((seq, hp * GLA_DV), bh),
                   pl.BlockSpec((1, hp, GLA_DK, GLA_DV), lambda b, h: (b, h, 0, 0))],
        out_shape=[jax.ShapeDtypeStruct(vg.shape, BF16),
                   jax.ShapeDtypeStruct((batch, GLA_HEADS, GLA_DK, GLA_DV), F32)],
        scratch_shapes=[pltpu.VMEM((hp, GLA_DK, GLA_DV), F32)],
        compiler_params=_cparams(("parallel", "parallel")),
        name="gla_prompt",
    )(qg, kg, vg, la, rg, mall, role, mask, gnorm)


def _gla_sample_kernel(q_ref, k_ref, v_ref, la_ref, r_ref, s0_ref, mall_ref, role_ref, mask_ref, msum_ref, g_ref,
                       o_ref, s_ref, *, nlev, t_new):
    sb = s0_ref.shape[0]
    c = sb * t_new
    spv = 8 // t_new
    e2_all, hl_all, la2_all = _gla_exponents(la_ref[...], mall_ref[...])
    seq_in_tile = lax.broadcasted_iota(jnp.int32, (8, GLA_DV), 0) // t_new
    seq_of_row = lax.broadcasted_iota(jnp.int32, (c, GLA_DV), 0) // t_new
    ksl = [slice(h * GLA_DK, (h + 1) * GLA_DK) for h in range(GLA_HEADS)]
    scores = [_gla_scores(q_ref[:, ksl[h]], k_ref[:, ksl[h]], e2_all[:, ksl[h]], la2_all[:, ksl[h]], role_ref, mask_ref,
                          nlev, c) for h in range(GLA_HEADS)]
    for h in range(GLA_HEADS):
        ks = ksl[h]
        vs = slice(h * GLA_DV, (h + 1) * GLA_DV)
        v = v_ref[:, vs]
        e2 = e2_all[:, ks]
        hl = hl_all[:, ks]
        a, qf, kf = scores[h]
        qe = (qf * jnp.exp2(e2[(nlev - 1) * c:nlev * c])).astype(BF16)
        kt_t = jnp.transpose(kf * jnp.exp2(e2[nlev * c:(nlev + 1) * c])).astype(BF16)
        bl_t = jnp.transpose(_dot(msum_ref[...], hl[:c]) + _dot(msum_ref[...], hl[c:]))
        dec_t = jnp.exp2(bl_t)
        inter = []
        for vr in range(c // 8):
            rows8 = qe[8 * vr:8 * vr + 8]
            tile = None
            for u in range(spv):
                j = vr * spv + u
                s0 = s0_ref[j, h]
                r = _dot(rows8, s0.astype(BF16))
                tile = r if tile is None else jnp.where(seq_in_tile == u, r, tile)
                vj = jnp.where(seq_of_row == j, v, jnp.zeros_like(v))
                s_ref[j, h] = dec_t[:, j:j + 1] * s0 + _dot(kt_t, vj)
            inter.append(tile)
        o = _dot(a.astype(BF16), v) + jnp.concatenate(inter, axis=0)
        o_ref[:, vs] = _gla_out(o, g_ref[...], r_ref[:, vs])


def _gla_sample(qg, kg, vg, la, rg, state, gnorm, batch, t_new):
    sb = SAMPLE_SEQ_BLOCK
    c = sb * t_new
    assert 8 % t_new == 0 and c % 8 == 0
    nlev, mall, role, mask = _gla_constants(c, t_new, with_rem=True)
    msum = jnp.asarray((np.arange(c)[None, :] // t_new == np.arange(sb)[:, None]).astype(np.float32), BF16)
    rows = lambda i: (i, 0)
    const = lambda i: (0, 0)
    st = lambda i: (i, 0, 0, 0)
    return pl.pallas_call(
        functools.partial(_gla_sample_kernel, nlev=nlev, t_new=t_new),
        grid=(batch // sb,),
        in_specs=[pl.BlockSpec((c, GLA_KEY_WIDTH), rows), pl.BlockSpec((c, GLA_KEY_WIDTH), rows),
                  pl.BlockSpec((c, GLA_VALUE_WIDTH), rows), pl.BlockSpec((c, GLA_KEY_WIDTH), rows),
                  pl.BlockSpec((c, GLA_VALUE_WIDTH), rows),
                  pl.BlockSpec((sb, GLA_HEADS, GLA_DK, GLA_DV), st),
                  pl.BlockSpec(mall.shape, const), pl.BlockSpec(role.shape, const), pl.BlockSpec(mask.shape, const),
                  pl.BlockSpec(msum.shape, const), pl.BlockSpec((1, GLA_DV), const)],
        out_specs=[pl.BlockSpec((c, GLA_VALUE_WIDTH), rows), pl.BlockSpec((sb, GLA_HEADS, GLA_DK, GLA_DV), st)],
        out_shape=[jax.ShapeDtypeStruct(vg.shape, BF16), jax.ShapeDtypeStruct(state.shape, F32)],
        compiler_params=_cparams(("parallel",)),
        name="gla_sample",
    )(qg, kg, vg, la, rg, state, mall, role, mask, msum, gnorm)


def _post_kernel(x_ref, a_ref, gl_ref, ga_ref, gb_ref, wpa, wpg, wo, nf_ref, wr_hi, wr_lo, br, x1t_o, rt_o):
    tm = x_ref.shape[0]
    part = POST_PART_ROWS if tm % POST_PART_ROWS == 0 else tm
    halves = [slice(r0, r0 + part) for r0 in range(0, tm, part)]
    proj =[(_dot(a_ref[r, :], wpa[...]), _dot(gl_ref[r, :], wpg[...])) for r in halves]
    x1s = []
    for r, (pa, pg) in zip(halves, proj):
        merged = ga_ref[r, :].astype(F32) * pa + gb_ref[r, :].astype(F32) * pg
        x1s.append(x_ref[r, :] + _dot(merged.astype(BF16), wo[...]))
    logits = []
    for r, x1 in zip(halves, x1s):
        rows = r.stop - r.start
        for j in range(TOKEN_ROWS):
            x1t_o[pl.ds(r.start * TOKEN_ROWS + j, rows, stride=TOKEN_ROWS), :] = x1[:, j * LANES:(j + 1) * LANES]
        h_hi, h_lo = _split_bf16(_rms(x1, nf_ref[...]))
        logits.append(_dot_nt(wr_hi[...], h_hi) + _dot_nt(wr_hi[...], h_lo) + _dot_nt(wr_lo[...], h_hi))
    nrow = ROUTER_ROWS
    big = jnp.int32(LANES)
    ninf = -jnp.inf
    for r, lt in zip(halves, logits):
        lt = lt[:nrow] + br[:nrow, 0:1]
        row = lax.broadcasted_iota(jnp.int32, lt.shape, 0)

        def first_max(vals):
            mx = jnp.max(vals, axis=0, keepdims=True)
            return mx, jnp.min(jnp.where(vals == mx, row, big), axis=0, keepdims=True)

        gl = jnp.where((row >= N_EXPERTS) & (row < N_EXPERTS + N_GROUPS), lt, ninf)
        gmax, gidx = first_max(gl)
        p_sel = 1.0 / jnp.sum(jnp.exp(gl - gmax), axis=0, keepdims=True)
        lo = (gidx - N_EXPERTS) * EXPERTS_PER_GROUP
        el = jnp.where((row >= lo) & (row < lo + EXPERTS_PER_GROUP), lt, ninf)
        v1, i1 = first_max(el)
        el2 = jnp.where(row == i1, ninf, el)
        v2, i2 = first_max(el2)
        t = jnp.exp(v2 - v1)
        w1 = p_sel / (1.0 + t)
        w2 = p_sel * t / (1.0 + t)
        row8 = lax.broadcasted_iota(jnp.int32, (rt_o.shape[0], r.stop - r.start), 0)
        pick = lambda k, val, rest: jnp.where(row8 == k, val, rest)
        rt_o[:, r] = pick(0, i1.astype(F32), pick(1, i2.astype(F32), pick(2, w1, pick(3, w2, 0.0))))


def _post(x2d, a_out, g_out, ga, gb, w, tm):
    n = x2d.shape[0]
    row = lambda i: (i, 0)
    const = lambda i: (0, 0)
    wnames = ("wpa", "wpg", "wo", "nf", "wr_hi", "wr_lo", "br")
    rt_rows = 8
    return pl.pallas_call(
        _post_kernel,
        grid=(n // tm,),
        in_specs=[pl.BlockSpec((tm, D_MODEL), row), pl.BlockSpec((tm, ATTN_WIDTH), row),
                  pl.BlockSpec((tm, GLA_VALUE_WIDTH), row), pl.BlockSpec((tm, D_MODEL), row),
                  pl.BlockSpec((tm, D_MODEL), row)] + [pl.BlockSpec(w[k].shape, const) for k in wnames],
        out_specs=[pl.BlockSpec((tm * TOKEN_ROWS, LANES), row),
                   pl.BlockSpec((rt_rows, tm), lambda i: (0, i))],
        out_shape=[jax.ShapeDtypeStruct((n * TOKEN_ROWS, LANES), F32), jax.ShapeDtypeStruct((rt_rows, n), F32)],
        compiler_params=_cparams(("parallel",)),
        name="post_mixer",
    )(x2d, a_out, g_out, ga, gb, *[w[k] for k in wnames])


def _moe_plan(rt, tme):
    n = rt.shape[1]
    ntiles = n // tme
    max_items = ntiles + N_GROUPS - 1
    i1, i2 = rt[0].astype(jnp.int32), rt[1].astype(jnp.int32)
    grp = i1 // EXPERTS_PER_GROUP
    lo = jnp.minimum(i1, i2) % EXPERTS_PER_GROUP
    hi = jnp.maximum(i1, i2) % EXPERTS_PER_GROUP
    snake = lambda l, h: jnp.where(l % 2 == 0, h, EXPERTS_PER_GROUP + l - h)
    key = (grp * EXPERTS_PER_GROUP + lo) * EXPERTS_PER_GROUP + snake(lo, hi)
    order = jnp.argsort(key, stable=True).astype(jnp.int32)
    skey = key[order].reshape(ntiles, tme)
    sg = skey // (EXPERTS_PER_GROUP * EXPERTS_PER_GROUP)
    slo = (skey // EXPERTS_PER_GROUP) % EXPERTS_PER_GROUP
    shi = snake(slo, skey % EXPERTS_PER_GROUP)
    ev = jnp.arange(EXPERTS_PER_GROUP)
    in_g = sg[:, :, None] == jnp.arange(N_GROUPS)
    uses_e = (slo[:, :, None] == ev) | (shi[:, :, None] == ev)
    flags_tge = jnp.any(in_g[:, :, :, None] & uses_e[:, :, None, :], axis=1)
    present = jnp.any(in_g, axis=1).reshape(-1)
    pos = jnp.cumsum(present) - 1
    n_items = pos[-1] + 1
    src = jnp.zeros((max_items,), jnp.int32).at[jnp.where(present, pos, max_items)].set(
        jnp.arange(ntiles * N_GROUPS, dtype=jnp.int32), mode="drop")
    it = jnp.arange(max_items)
    valid = it < n_items
    last_src = src[n_items - 1]
    src = jnp.where(valid, src, last_src)
    item_tile = src // N_GROUPS
    item_group = src % N_GROUPS
    prev_tile = jnp.concatenate([jnp.full((1,), -1, jnp.int32), item_tile[:-1]])
    next_tile = jnp.concatenate([item_tile[1:], jnp.full((1,), -1, jnp.int32)])
    first = valid & (item_tile != prev_tile)
    last = valid & ((item_tile != next_tile) | (it == n_items - 1))
    flags = flags_tge[item_tile, item_group] & valid[:, None]
    e0 = jnp.argmax(flags, axis=1)
    rest = flags & (ev[None, :] != e0[:, None])
    e1 = jnp.argmax(rest, axis=1)
    rest = rest & (ev[None, :] != e1[:, None])
    has2 = jnp.any(rest, axis=1)
    e2 = jnp.argmax(rest, axis=1)
    rest = rest & (ev[None, :] != e2[:, None])
    rt_sorted = rt[:, order].reshape(rt.shape[0], ntiles, tme).transpose(1, 0, 2)
    i32 = lambda z: z.astype(jnp.int32)
    n_rest = jnp.sum(rest, axis=1)
    rest_list = jnp.argsort(jnp.logical_not(rest), axis=1, stable=True)
    plan = (order, i32(item_tile), i32(item_group), i32(first), i32(last), i32(valid), i32(e0), i32(e1), i32(e2),
            i32(has2), i32(n_rest), i32(rest_list.reshape(-1)))
    return plan, rt_sorted


def _moe_kernel(order, itile, igroup, ifirst, ilast, ivalid, ie0, ie1, ie2, ihas2, inrest, irest, x_hbm, rt_ref, wg, wu, wd,
                nffn, nfin,
                y_hbm, xbuf, ybuf, acc, hbuf, rcol, gsem, ssem, *, tme, ntiles):
    i = pl.program_id(0)
    t = itile[i]
    slot = t % 2
    g = igroup[i]
    is_first = ifirst[i] == 1

    def gather_row(tile, sl, r):
        tok = order[tile * tme + r]
        src = x_hbm.at[pl.ds(pl.multiple_of(tok * TOKEN_ROWS, TOKEN_ROWS), TOKEN_ROWS)]
        dst = xbuf.at[pl.ds(pl.multiple_of((sl * tme + r) * TOKEN_ROWS, TOKEN_ROWS), TOKEN_ROWS)]
        return pltpu.make_async_copy(src, dst, gsem.at[sl])

    def scatter_row(tile, sl, r):
        tok = order[tile * tme + r]
        return pltpu.make_async_copy(ybuf.at[sl, pl.ds(r, 1)], y_hbm.at[pl.ds(tok, 1)], ssem.at[sl])

    def start_rows(make, tile, sl):
        def body(r, c):
            make(tile, sl, r).start()
            return c
        lax.fori_loop(0, tme, body, 0, unroll=8)

    def wait_gather(sl):
        rows = tme * TOKEN_ROWS
        pltpu.make_async_copy(x_hbm.at[pl.ds(0, rows)], xbuf.at[pl.ds(pl.multiple_of(sl * rows, rows), rows)],
                              gsem.at[sl]).wait()

    def wait_scatter(sl):
        pltpu.make_async_copy(ybuf.at[sl], y_hbm.at[pl.ds(0, tme)], ssem.at[sl]).wait()

    def start_rows_inline(make, tile, sl, r0=0, r1=tme):
        for r in range(r0, r1):
            make(tile, sl, r).start()

    def expert(e):
        eid = (g * EXPERTS_PER_GROUP + e).astype(F32)
        ce = (jnp.where(rcol[:, 0:1] == eid, rcol[:, 2:3], 0.0)
              + jnp.where(rcol[:, 1:2] == eid, rcol[:, 3:4], 0.0))
        h = hbuf[...]
        a = _dot(h, wg[e])
        u = _dot(h, wu[e])
        act = (a * _sigmoid(a)) * u * ce
        acc[...] += _dot(act.astype(BF16), wd[e])

    def expert_with(e, run, alternatives):
        plain = run
        for cond, side_work in alternatives:
            @pl.when(cond)
            def _():
                side_work()
                expert(e)

            plain = jnp.logical_and(plain, jnp.logical_not(cond))

        @pl.when(plain)
        def _():
            expert(e)

    @pl.when(i == 0)
    def _():
        start_rows(gather_row, 0, 0)

    @pl.when(is_first)
    def _():
        wait_gather(slot)
        base = pl.multiple_of(slot * (tme * TOKEN_ROWS), tme * TOKEN_ROWS)
        x1 = jnp.concatenate([xbuf[pl.ds(base + j, tme, stride=TOKEN_ROWS), :] for j in range(TOKEN_ROWS)], axis=1)
        acc[...] = x1
        hbuf[...] = _rms(x1, nffn[...]).astype(BF16)
        rt = rt_ref[0]
        rcol[...] = jnp.transpose(jnp.concatenate([rt, jnp.zeros((LANES - rt.shape[0], tme), F32)], axis=0))

    valid = ivalid[i] == 1
    has2 = ihas2[i] == 1
    do_gather = jnp.logical_and(is_first, t + 1 < ntiles)
    do_scatter = jnp.logical_and(is_first, t >= 1)
    half = tme // 2
    scatter_rows = lambda r0, r1: (lambda: start_rows_inline(scatter_row, t - 1, 1 - slot, r0, r1))
    expert_with(ie0[i], valid, [(do_gather, lambda: start_rows_inline(gather_row, t + 1, 1 - slot))])
    expert_with(ie1[i], valid, [(jnp.logical_and(do_scatter, has2), scatter_rows(0, half)),
                                (jnp.logical_and(do_scatter, jnp.logical_not(has2)), scatter_rows(0, tme))])
    expert_with(ie2[i], has2, [(jnp.logical_and(do_scatter, has2), scatter_rows(half, tme))])

    def rest_expert(k, carry):
        expert(irest[i * EXPERTS_PER_GROUP + k])
        return carry

    lax.fori_loop(0, inrest[i], rest_expert, 0)

    @pl.when(ilast[i] == 1)
    def _():
        @pl.when(t >= 2)
        def _():
            wait_scatter(slot)

        ybuf[slot] = _rms(acc[...], nfin[...])

        @pl.when(t == ntiles - 1)
        def _():
            start_rows(scatter_row, t, slot)

    @pl.when(i == pl.num_programs(0) - 1)
    def _():
        for sl in range(min(2, ntiles)):
            wait_scatter(sl)


def _moe(x1t, rt, weg, weu, wed, nffn, nfin):
    n = x1t.shape[0] // TOKEN_ROWS
    tme = min(MOE_TILE, n)
    assert n % tme == 0
    ntiles = n // tme
    plan, rt_sorted = _moe_plan(rt, tme)
    max_items = ntiles + N_GROUPS - 1
    grp = lambda i, order, itile, igroup, *_: (igroup[i], 0, 0)
    til = lambda i, order, itile, *_: (itile[i], 0, 0)
    const = lambda i, *_: (0, 0)
    grid_spec = pltpu.PrefetchScalarGridSpec(
        num_scalar_prefetch=len(plan),
        grid=(max_items,),
        in_specs=[pl.BlockSpec(memory_space=pl.ANY),
                  pl.BlockSpec((1,) + rt_sorted.shape[1:], til),
                  pl.BlockSpec((EXPERTS_PER_GROUP, D_MODEL, EXPERT_FF), grp),
                  pl.BlockSpec((EXPERTS_PER_GROUP, D_MODEL, EXPERT_FF), grp),
                  pl.BlockSpec((EXPERTS_PER_GROUP, EXPERT_FF, D_MODEL), grp),
                  pl.BlockSpec((1, D_MODEL), const), pl.BlockSpec((1, D_MODEL), const)],
        out_specs=pl.BlockSpec(memory_space=pl.ANY),
        scratch_shapes=[pltpu.VMEM((2 * tme * TOKEN_ROWS, LANES), F32), pltpu.VMEM((2, tme, D_MODEL), F32),
                        pltpu.VMEM((tme, D_MODEL), F32), pltpu.VMEM((tme, D_MODEL), BF16),
                        pltpu.VMEM((tme, LANES), F32),
                        pltpu.SemaphoreType.DMA((2,)), pltpu.SemaphoreType.DMA((2,))],
    )
    return pl.pallas_call(
        functools.partial(_moe_kernel, tme=tme, ntiles=ntiles),
        grid_spec=grid_spec,
        out_shape=jax.ShapeDtypeStruct((n, D_MODEL), F32),
        compiler_params=_cparams(("arbitrary",)),
        name="moe",
    )(*plan, x1t, rt_sorted, weg, weu, wed, nffn, nfin)


def _rope_tables(positions):
    half = HEAD_DIM // 2
    inv_freq = ROPE_THETA ** (-jnp.arange(half, dtype=F32) / half)
    ang = positions.astype(F32)[:, None] * inv_freq[None, :]
    cos, sin = jnp.cos(ang), jnp.sin(ang)
    reps = LANES // HEAD_DIM
    return (jnp.tile(jnp.concatenate([cos, cos], -1), (1, reps)),
            jnp.tile(jnp.concatenate([-sin, sin], -1), (1, reps)))


def _prep_weights(norm_mix, w_in, w_gla_f2, b_gla_f, gla_norm, w_proj_attn, w_proj_gla, w_out, norm_ffn,
                  w_router_group, b_router_group, w_router_expert, b_router_expert):
    w = {"win": w_in.astype(BF16)}
    w["wf2"] = jnp.pad(w_gla_f2.astype(BF16), ((0, LANES - GLA_GATE_RANK), (0, 0)))
    w["bf"] = b_gla_f.reshape(1, -1)
    w["norm_mix"] = norm_mix.reshape(1, -1)
    w["gla_norm"] = gla_norm.reshape(1, -1)
    w["wpa"] = w_proj_attn.astype(BF16)
    w["wpg"] = w_proj_gla.astype(BF16)
    w["wo"] = w_out.astype(BF16)
    w["nf"] = norm_ffn.reshape(1, -1)
    pad = LANES - N_EXPERTS - N_GROUPS
    wr_t = jnp.pad(jnp.concatenate([w_router_expert, w_router_group], axis=1), ((0, 0), (0, pad))).T
    w["wr_hi"], w["wr_lo"] = _split_bf16(wr_t)
    w["br"] = jnp.broadcast_to(jnp.pad(jnp.concatenate([b_router_expert, b_router_group]), (0, pad))[:, None],
                               (LANES, LANES))
    return w


def _layer(x, positions_tab, cache, w, sinks, weg, weu, wed, nfin, tm):
    batch, seq, _ = x.shape
    n = batch * seq
    x2d = x.reshape(n, D_MODEL)
    cos_t, sin_t = positions_tab
    qa, ka, va, qg, kg, vg, rg, la, ga, gb = _inproj(x2d, w["norm_mix"], cos_t, sin_t, w, tm)
    if cache is None:
        a_out = _swa_prompt(sinks, qa, ka, va, batch, seq)
        last = lambda z: z.reshape(batch, seq, KV_WIDTH)[:, seq - WINDOW:].reshape(batch, WINDOW, ATTN_KV_HEADS, HEAD_DIM)
        new_k, new_v = last(ka), last(va)
        g_out, new_s = _gla_prompt(qg, kg, vg, la, rg, w["gla_norm"], batch, seq)
    else:
        cache_k, cache_v, state = cache
        a_out, new_k, new_v = _swa_sample(sinks, qa, ka, va, cache_k.reshape(batch, WINDOW, KV_WIDTH),
                                          cache_v.reshape(batch, WINDOW, KV_WIDTH), batch, seq)
        new_k = new_k.reshape(batch, WINDOW, ATTN_KV_HEADS, HEAD_DIM)
        new_v = new_v.reshape(batch, WINDOW, ATTN_KV_HEADS, HEAD_DIM)
        g_out, new_s = _gla_sample(qg, kg, vg, la, rg, state, w["gla_norm"], batch, seq)
    x1t, rt = _post(x2d, a_out, g_out, ga, gb, w, tm * 2 if x2d.shape[0] % (tm * 2) == 0 else tm)
    y = _moe(x1t, rt, weg, weu, wed, w["nf"], nfin)
    return y.reshape(batch, seq, D_MODEL), new_k, new_v, new_s


def kernel(x_prompt, x_sample, cache_win_k, cache_win_v, state_gla, norm_mix, w_in, w_gla_f2, b_gla_f, gla_norm,
           attn_sinks, w_proj_attn, w_proj_gla, w_out, norm_ffn, w_router_group, b_router_group, w_router_expert,
           b_router_expert, w_exp_gate, w_exp_up, w_exp_down, norm_final):
    assert norm_mix.shape[0] == 1, "single-layer step"
    seq_p = x_prompt.shape[1]
    dec_b, dec_t = x_sample.shape[0], x_sample.shape[1]
    w = _prep_weights(norm_mix[0], w_in[0], w_gla_f2[0], b_gla_f[0], gla_norm[0], w_proj_attn[0], w_proj_gla[0],
                      w_out[0], norm_ffn[0], w_router_group[0], b_router_group[0], w_router_expert[0],
                      b_router_expert[0])
    weg = w_exp_gate[0].astype(BF16)
    weu = w_exp_up[0].astype(BF16)
    wed = w_exp_down[0].astype(BF16)
    nfin = norm_final.reshape(1, -1)
    sinks = attn_sinks[0]
    tab_p = _rope_tables(jnp.arange(seq_p, dtype=jnp.int32))
    pos_s = PAST_LEN + jnp.arange(dec_t, dtype=jnp.int32)
    tab_s = tuple(jnp.tile(t, (dec_b, 1)) for t in _rope_tables(pos_s))
    tm_p = min(512, seq_p)
    tm_s = dec_b * dec_t
    yp, pk, pv, ps = _layer(x_prompt, tab_p, None, w, sinks, weg, weu, wed, nfin, tm_p)
    ys, sk, sv, ss = _layer(x_sample, tab_s, (cache_win_k[0], cache_win_v[0], state_gla[0]), w, sinks, weg, weu, wed,
                            nfin, tm_s)
    return (yp, ys, pk[None], pv[None], ps[None], sk[None], sv[None], ss[None])
```
